```python
import math
import jax, jax.numpy as jnp
from jax import lax
import numpy as np

D_MODEL = 1024
BATCH = 8
SEQ = 8192
DEPTH = 1

CHUNK = 64
MIX_WIDTH = D_MODEL
A_HEADS = 4
A_HEAD_DIM = 128
A_WIDTH = A_HEADS * A_HEAD_DIM
SGU_BLOCK = 128
B_GROUPS = 4
B_GROUP_DIM = 128
B_WIDTH = B_GROUPS * B_GROUP_DIM
POOL_WINDOWS = (2, 4, 8, 16)
IN_WIDTH = 2 * A_WIDTH + B_WIDTH
D_FF = 2816
CONV_W = 3
N_MOD = 6
EPS = 1e-6

kernel_name = "hybrid_sgu_pool_convffn_block"


def rmsnorm(x, g):
    xf = x.astype(jnp.float32)
    y = xf * lax.rsqrt(jnp.mean(xf * xf, axis=-1, keepdims=True) + EPS)
    return (y * g.astype(jnp.float32)).astype(x.dtype)


def chunk_causal_block_mask():
    pos = jnp.arange(SGU_BLOCK)
    return (pos[None, :] // CHUNK) <= (pos[:, None] // CHUNK)


def sgu_mixer(a_in, norm_g, w_s, b_s):
    bsz, seq, _ = a_in.shape
    a = jax.nn.gelu(a_in)
    u, v = a[..., :A_WIDTH], a[..., A_WIDTH:]
    nb = seq // SGU_BLOCK
    v = v.reshape(bsz, nb, SGU_BLOCK, A_HEADS, A_HEAD_DIM)
    v = rmsnorm(v, norm_g[None, None, None])
    w_m = jnp.where(chunk_causal_block_mask()[None], w_s, jnp.zeros_like(w_s))
    z = jnp.einsum('hij,bnjhc->bnihc', w_m, v) + b_s.T[None, None, :, :, None]
    return u * z.reshape(bsz, seq, A_WIDTH)


def pool_mixer(p_in, w_pool, pool_scale):
    bsz, seq, _ = p_in.shape
    pg = p_in.reshape(bsz, seq, B_GROUPS, B_GROUP_DIM)
    t = jnp.arange(seq)
    outs = []
    for g, w in enumerate(POOL_WINDOWS):
        xf = pg[:, :, g].astype(jnp.float32)
        cs = jnp.cumsum(xf, axis=1)
        cs_full = jnp.pad(cs, ((0, 0), (w, 0), (0, 0)))
        win_sum = cs_full[:, w:] - cs_full[:, :seq]
        cnt = jnp.minimum(t + 1, w).astype(jnp.float32)[None, :, None]
        outs.append((win_sum / cnt - xf).astype(p_in.dtype))
    pooled = jnp.stack(outs, axis=2)
    y = jnp.einsum('bsgc,gcd->bsgd', pooled, w_pool)
    return y.reshape(bsz, seq, B_WIDTH) * pool_scale


def conv_ffn(h, w_up, conv_w, conv_b, w_down):
    up = jnp.einsum('bsd,df->bsf', h, w_up)
    upp = jnp.pad(up, ((0, 0), (CONV_W - 1, 0), (0, 0)))
    seq = up.shape[1]
    y = conv_b + upp[:, 0:seq] * conv_w[0] + upp[:, 1:seq + 1] * conv_w[1] + upp[:, 2:seq + 2] * conv_w[2]
    gate, val = y[..., :D_FF], y[..., D_FF:]
    return jnp.einsum('bsf,fd->bsd', jax.nn.silu(gate) * val, w_down)


def _fwd_setup_inputs(seed: int = 0) -> dict:
    key = jax.random.key(seed)
    ks = jax.random.split(key, 20)
    f32 = jnp.float32
    nrm = lambda k, shape, s: jax.random.normal(k, shape, f32) * s
    L = DEPTH
    return {
        "x": nrm(ks[0], (BATCH, SEQ, D_MODEL), 1.0),
        "c": nrm(ks[1], (BATCH, D_MODEL), 1.0),
        "w_ada": nrm(ks[2], (L, D_MODEL, N_MOD * D_MODEL), D_MODEL ** -0.5),
        "b_ada": nrm(ks[3], (L, N_MOD * D_MODEL), 0.02),
        "pre_mix_g": 1.0 + nrm(ks[4], (L, D_MODEL), 0.02),
        "post_mix_g": 1.0 + nrm(ks[5], (L, D_MODEL), 0.02),
        "w_in": nrm(ks[6], (L, D_MODEL, IN_WIDTH), D_MODEL ** -0.5),
        "sgu_norm_g": 1.0 + nrm(ks[7], (L, A_HEADS, A_HEAD_DIM), 0.02),
        "w_spatial": nrm(ks[8], (L, A_HEADS, SGU_BLOCK, SGU_BLOCK), SGU_BLOCK ** -0.5),
        "b_spatial": 1.0 + nrm(ks[9], (L, A_HEADS, SGU_BLOCK), 0.02),
        "w_pool": nrm(ks[10], (L, B_GROUPS, B_GROUP_DIM, B_GROUP_DIM), B_GROUP_DIM ** -0.5),
        "pool_scale": 1.0 + nrm(ks[11], (L, B_WIDTH), 0.02),
        "w_out": nrm(ks[12], (L, MIX_WIDTH, D_MODEL), MIX_WIDTH ** -0.5),
        "pre_ffn_g": 1.0 + nrm(ks[13], (L, D_MODEL), 0.02),
        "post_ffn_g": 1.0 + nrm(ks[14], (L, D_MODEL), 0.02),
        "w_up": nrm(ks[15], (L, D_MODEL, 2 * D_FF), D_MODEL ** -0.5),
        "conv_w": nrm(ks[16], (L, CONV_W, 2 * D_FF), CONV_W ** -0.5),
        "conv_b": nrm(ks[17], (L, 2 * D_FF), 0.02),
        "w_down": nrm(ks[18], (L, D_FF, D_MODEL), D_FF ** -0.5),
    }


def _fwd_reference(x, c, w_ada, b_ada, pre_mix_g, post_mix_g, w_in, sgu_norm_g, w_spatial,
              b_spatial, w_pool, pool_scale, w_out, pre_ffn_g, post_ffn_g, w_up, conv_w,
              conv_b, w_down):
    sc = jax.nn.silu(c)
    for l in range(DEPTH):
        mod = (jnp.einsum('bd,de->be', sc, w_ada[l]) + b_ada[l])[:, None, :]
        sh_m, sc_m, g_m, sh_f, sc_f, g_f = jnp.split(mod, N_MOD, axis=-1)

        h = rmsnorm(x, pre_mix_g[l]) * (1.0 + sc_m) + sh_m
        proj = jnp.einsum('bsd,de->bse', h, w_in[l])
        out_a = sgu_mixer(proj[..., :2 * A_WIDTH], sgu_norm_g[l], w_spatial[l], b_spatial[l])
        out_b = pool_mixer(proj[..., 2 * A_WIDTH:], w_pool[l], pool_scale[l])
        mixed = jnp.einsum('bse,ed->bsd', jnp.concatenate([out_a, out_b], axis=-1), w_out[l])
        x = x + g_m * rmsnorm(mixed, post_mix_g[l])

        h = rmsnorm(x, pre_ffn_g[l]) * (1.0 + sc_f) + sh_f
        f = conv_ffn(h, w_up[l], conv_w[l], conv_b[l], w_down[l])
        x = x + g_f * rmsnorm(f, post_ffn_g[l])
    return x


import jax as _jax
import jax.numpy as _jnp

TWIN_FORMAT = 'train_step'
FWD_PARAMS = ['x', 'c', 'w_ada', 'b_ada', 'pre_mix_g', 'post_mix_g', 'w_in', 'sgu_norm_g', 'w_spatial', 'b_spatial', 'w_pool', 'pool_scale', 'w_out', 'pre_ffn_g', 'post_ffn_g', 'w_up', 'conv_w', 'conv_b', 'w_down']
TWIN_WEIGHTS = ['w_ada', 'b_ada', 'pre_mix_g', 'post_mix_g', 'w_in', 'sgu_norm_g', 'w_spatial', 'b_spatial', 'w_pool', 'pool_scale', 'w_out', 'pre_ffn_g', 'post_ffn_g', 'w_up', 'conv_w', 'conv_b', 'w_down']
TWIN_DIFF_INPUT = 'x'
TWIN_INPUTS = ['x', 'c', 'w_ada', 'b_ada', 'pre_mix_g', 'post_mix_g', 'w_in', 'sgu_norm_g', 'w_spatial', 'b_spatial', 'w_pool', 'pool_scale', 'w_out', 'pre_ffn_g', 'post_ffn_g', 'w_up', 'conv_w', 'conv_b', 'w_down', 'loss_target', 'm_w_ada', 'm_b_ada', 'm_pre_mix_g', 'm_post_mix_g', 'm_w_in', 'm_sgu_norm_g', 'm_w_spatial', 'm_b_spatial', 'm_w_pool', 'm_pool_scale', 'm_w_out', 'm_pre_ffn_g', 'm_post_ffn_g', 'm_w_up', 'm_conv_w', 'm_conv_b', 'm_w_down', 'v_w_ada', 'v_b_ada', 'v_pre_mix_g', 'v_post_mix_g', 'v_w_in', 'v_sgu_norm_g', 'v_w_spatial', 'v_b_spatial', 'v_w_pool', 'v_pool_scale', 'v_w_out', 'v_pre_ffn_g', 'v_post_ffn_g', 'v_w_up', 'v_conv_w', 'v_conv_b', 'v_w_down']
TWIN_OUTPUTS = ['loss', 'grad_x', 'grad_w_ada', 'grad_b_ada', 'grad_pre_mix_g', 'grad_post_mix_g', 'grad_w_in', 'grad_sgu_norm_g', 'grad_w_spatial', 'grad_b_spatial', 'grad_w_pool', 'grad_pool_scale', 'grad_w_out', 'grad_pre_ffn_g', 'grad_post_ffn_g', 'grad_w_up', 'grad_conv_w', 'grad_conv_b', 'grad_w_down', 'delta_w_ada', 'delta_b_ada', 'delta_pre_mix_g', 'delta_post_mix_g', 'delta_w_in', 'delta_sgu_norm_g', 'delta_w_spatial', 'delta_b_spatial', 'delta_w_pool', 'delta_pool_scale', 'delta_w_out', 'delta_pre_ffn_g', 'delta_post_ffn_g', 'delta_w_up', 'delta_conv_w', 'delta_conv_b', 'delta_w_down', 'new_m_w_ada', 'new_m_b_ada', 'new_m_pre_mix_g', 'new_m_post_mix_g', 'new_m_w_in', 'new_m_sgu_norm_g', 'new_m_w_spatial', 'new_m_b_spatial', 'new_m_w_pool', 'new_m_pool_scale', 'new_m_w_out', 'new_m_pre_ffn_g', 'new_m_post_ffn_g', 'new_m_w_up', 'new_m_conv_w', 'new_m_conv_b', 'new_m_w_down', 'new_v_w_ada', 'new_v_b_ada', 'new_v_pre_mix_g', 'new_v_post_mix_g', 'new_v_w_in', 'new_v_sgu_norm_g', 'new_v_w_spatial', 'new_v_b_spatial', 'new_v_w_pool', 'new_v_pool_scale', 'new_v_w_out', 'new_v_pre_ffn_g', 'new_v_post_ffn_g', 'new_v_w_up', 'new_v_conv_w', 'new_v_conv_b', 'new_v_w_down']
TWIN_LEAF_KINDS = {'loss': 'loss', 'grad_x': 'grad_x', 'grad_w_ada': 'grad_w', 'grad_b_ada': 'grad_w', 'grad_pre_mix_g': 'grad_w', 'grad_post_mix_g': 'grad_w', 'grad_w_in': 'grad_w', 'grad_sgu_norm_g': 'grad_w', 'grad_w_spatial': 'grad_w', 'grad_b_spatial': 'grad_w', 'grad_w_pool': 'grad_w', 'grad_pool_scale': 'grad_w', 'grad_w_out': 'grad_w', 'grad_pre_ffn_g': 'grad_w', 'grad_post_ffn_g': 'grad_w', 'grad_w_up': 'grad_w', 'grad_conv_w': 'grad_w', 'grad_conv_b': 'grad_w', 'grad_w_down': 'grad_w', 'delta_w_ada': 'delta_w', 'delta_b_ada': 'delta_w', 'delta_pre_mix_g': 'delta_w', 'delta_post_mix_g': 'delta_w', 'delta_w_in': 'delta_w', 'delta_sgu_norm_g': 'delta_w', 'delta_w_spatial': 'delta_w', 'delta_b_spatial': 'delta_w', 'delta_w_pool': 'delta_w', 'delta_pool_scale': 'delta_w', 'delta_w_out': 'delta_w', 'delta_pre_ffn_g': 'delta_w', 'delta_post_ffn_g': 'delta_w', 'delta_w_up': 'delta_w', 'delta_conv_w': 'delta_w', 'delta_conv_b': 'delta_w', 'delta_w_down': 'delta_w', 'new_m_w_ada': 'new_m', 'new_m_b_ada': 'new_m', 'new_m_pre_mix_g': 'new_m', 'new_m_post_mix_g': 'new_m', 'new_m_w_in': 'new_m', 'new_m_sgu_norm_g': 'new_m', 'new_m_w_spatial': 'new_m', 'new_m_b_spatial': 'new_m', 'new_m_w_pool': 'new_m', 'new_m_pool_scale': 'new_m', 'new_m_w_out': 'new_m', 'new_m_pre_ffn_g': 'new_m', 'new_m_post_ffn_g': 'new_m', 'new_m_w_up': 'new_m', 'new_m_conv_w': 'new_m', 'new_m_conv_b': 'new_m', 'new_m_w_down': 'new_m', 'new_v_w_ada': 'new_v', 'new_v_b_ada': 'new_v', 'new_v_pre_mix_g': 'new_v', 'new_v_post_mix_g': 'new_v', 'new_v_w_in': 'new_v', 'new_v_sgu_norm_g': 'new_v', 'new_v_w_spatial': 'new_v', 'new_v_b_spatial': 'new_v', 'new_v_w_pool': 'new_v', 'new_v_pool_scale': 'new_v', 'new_v_w_out': 'new_v', 'new_v_pre_ffn_g': 'new_v', 'new_v_post_ffn_g': 'new_v', 'new_v_w_up': 'new_v', 'new_v_conv_w': 'new_v', 'new_v_conv_b': 'new_v', 'new_v_w_down': 'new_v'}


def _forward(args):
    return _fwd_reference(*[args[k] for k in FWD_PARAMS])


def _output_shape():
    def fwd():
        inp = _fwd_setup_inputs(0)
        return _fwd_reference(*[inp[k] for k in FWD_PARAMS])
    out = _jax.eval_shape(fwd)
    return out.shape, out.dtype

N_MICROBATCH = 1
ADAM_LR = 0.001
ADAM_B1 = 0.9
ADAM_B2 = 0.999
ADAM_EPS = 1e-08
ADAM_WD = 0.01
ADAM_STEP = 10
PER_EXAMPLE_BATCH_AXIS = {'x': 0, 'c': 0, 'loss_target': 0}
SHARED_INPUTS = []
_WEIGHT_DTYPES = {'w_ada': _jnp.float32, 'b_ada': _jnp.float32, 'pre_mix_g': _jnp.float32, 'post_mix_g': _jnp.float32, 'w_in': _jnp.float32, 'sgu_norm_g': _jnp.float32, 'w_spatial': _jnp.float32, 'b_spatial': _jnp.float32, 'w_pool': _jnp.float32, 'pool_scale': _jnp.float32, 'w_out': _jnp.float32, 'pre_ffn_g': _jnp.float32, 'post_ffn_g': _jnp.float32, 'w_up': _jnp.float32, 'conv_w': _jnp.float32, 'conv_b': _jnp.float32, 'w_down': _jnp.float32}
MOMENT_SCALE = {'w_ada': 5.113276e+00, 'b_ada': 1.112254e+01, 'pre_mix_g': 4.376611e-01, 'post_mix_g': 2.735836e+01, 'w_in': 7.020674e-01, 'sgu_norm_g': 4.891002e-01, 'w_spatial': 2.403004e-01, 'b_spatial': 2.834328e-01, 'w_pool': 5.119008e-01, 'pool_scale': 7.175941e-01, 'w_out': 1.507744e+00, 'pre_ffn_g': 4.848870e-01, 'post_ffn_g': 2.675827e+01, 'w_up': 5.073875e-01, 'conv_w': 5.375699e-01, 'conv_b': 8.637777e-01, 'w_down': 1.043744e+00}


def _to_microbatches(a, axis):
    t = _jnp.moveaxis(a, axis, 0)
    t = t.reshape((N_MICROBATCH, t.shape[0] // N_MICROBATCH) + t.shape[1:])
    return _jnp.moveaxis(t, 1, axis + 1)


def setup_inputs(seed: int = 0) -> dict:
    inp = _fwd_setup_inputs(seed)
    key = _jax.random.fold_in(_jax.random.key(seed), 7919)
    shape, _ = _output_shape()
    out = dict(inp)
    out["loss_target"] = _jax.random.normal(_jax.random.fold_in(key, 0), shape, _jnp.float32)
    for i, name in enumerate(TWIN_WEIGHTS):
        w = inp[name].astype(_jnp.float32)
        if MOMENT_SCALE is None:
            s = _jnp.sqrt(_jnp.mean(_jnp.square(w)) + 1e-30)
        else:
            s = MOMENT_SCALE[name]
        km, kv = _jax.random.split(_jax.random.fold_in(key, i + 1))
        out[name] = w
        out["m_" + name] = s * _jax.random.normal(km, w.shape, _jnp.float32)
        out["v_" + name] = (s * s) * _jax.random.uniform(kv, w.shape, _jnp.float32, 0.5, 1.5)
    if N_MICROBATCH > 1:
        for name, axis in PER_EXAMPLE_BATCH_AXIS.items():
            out[name] = _to_microbatches(out[name], axis)
    return {'x': out['x'], 'c': out['c'], 'w_ada': out['w_ada'], 'b_ada': out['b_ada'], 'pre_mix_g': out['pre_mix_g'], 'post_mix_g': out['post_mix_g'], 'w_in': out['w_in'], 'sgu_norm_g': out['sgu_norm_g'], 'w_spatial': out['w_spatial'], 'b_spatial': out['b_spatial'], 'w_pool': out['w_pool'], 'pool_scale': out['pool_scale'], 'w_out': out['w_out'], 'pre_ffn_g': out['pre_ffn_g'], 'post_ffn_g': out['post_ffn_g'], 'w_up': out['w_up'], 'conv_w': out['conv_w'], 'conv_b': out['conv_b'], 'w_down': out['w_down'], 'loss_target': out['loss_target'], 'm_w_ada': out['m_w_ada'], 'm_b_ada': out['m_b_ada'], 'm_pre_mix_g': out['m_pre_mix_g'], 'm_post_mix_g': out['m_post_mix_g'], 'm_w_in': out['m_w_in'], 'm_sgu_norm_g': out['m_sgu_norm_g'], 'm_w_spatial': out['m_w_spatial'], 'm_b_spatial': out['m_b_spatial'], 'm_w_pool': out['m_w_pool'], 'm_pool_scale': out['m_pool_scale'], 'm_w_out': out['m_w_out'], 'm_pre_ffn_g': out['m_pre_ffn_g'], 'm_post_ffn_g': out['m_post_ffn_g'], 'm_w_up': out['m_w_up'], 'm_conv_w': out['m_conv_w'], 'm_conv_b': out['m_conv_b'], 'm_w_down': out['m_w_down'], 'v_w_ada': out['v_w_ada'], 'v_b_ada': out['v_b_ada'], 'v_pre_mix_g': out['v_pre_mix_g'], 'v_post_mix_g': out['v_post_mix_g'], 'v_w_in': out['v_w_in'], 'v_sgu_norm_g': out['v_sgu_norm_g'], 'v_w_spatial': out['v_w_spatial'], 'v_b_spatial': out['v_b_spatial'], 'v_w_pool': out['v_w_pool'], 'v_pool_scale': out['v_pool_scale'], 'v_w_out': out['v_w_out'], 'v_pre_ffn_g': out['v_pre_ffn_g'], 'v_post_ffn_g': out['v_post_ffn_g'], 'v_w_up': out['v_w_up'], 'v_conv_w': out['v_conv_w'], 'v_conv_b': out['v_conv_b'], 'v_w_down': out['v_w_down']}


def _loss(weights, diff, rest, loss_target):
    with _jax.named_scope("forward"):
        args = {**rest, TWIN_DIFF_INPUT: diff, **{k: w.astype(_WEIGHT_DTYPES[k]) for k, w in weights.items()}}
        y = _forward(args)
    with _jax.named_scope("loss_head"):
        err = _jnp.square(y.astype(_jnp.float32) - loss_target)
        return 0.5 * _jnp.sum(_jnp.mean(err, axis=-1)) if err.ndim else 0.5 * err


def _adamw(w, g, m, v):
    m = ADAM_B1 * m + (1.0 - ADAM_B1) * g
    v = ADAM_B2 * v + (1.0 - ADAM_B2) * _jnp.square(g)
    m_hat = m / (1.0 - ADAM_B1 ** ADAM_STEP)
    v_hat = v / (1.0 - ADAM_B2 ** ADAM_STEP)
    delta = -ADAM_LR * (m_hat / (_jnp.sqrt(v_hat) + ADAM_EPS) + ADAM_WD * w)
    return delta, m, v


def reference(x, c, w_ada, b_ada, pre_mix_g, post_mix_g, w_in, sgu_norm_g, w_spatial, b_spatial, w_pool, pool_scale, w_out, pre_ffn_g, post_ffn_g, w_up, conv_w, conv_b, w_down, loss_target, m_w_ada, m_b_ada, m_pre_mix_g, m_post_mix_g, m_w_in, m_sgu_norm_g, m_w_spatial, m_b_spatial, m_w_pool, m_pool_scale, m_w_out, m_pre_ffn_g, m_post_ffn_g, m_w_up, m_conv_w, m_conv_b, m_w_down, v_w_ada, v_b_ada, v_pre_mix_g, v_post_mix_g, v_w_in, v_sgu_norm_g, v_w_spatial, v_b_spatial, v_w_pool, v_pool_scale, v_w_out, v_pre_ffn_g, v_post_ffn_g, v_w_up, v_conv_w, v_conv_b, v_w_down):
    given = dict(x=x, c=c, w_ada=w_ada, b_ada=b_ada, pre_mix_g=pre_mix_g, post_mix_g=post_mix_g, w_in=w_in, sgu_norm_g=sgu_norm_g, w_spatial=w_spatial, b_spatial=b_spatial, w_pool=w_pool, pool_scale=pool_scale, w_out=w_out, pre_ffn_g=pre_ffn_g, post_ffn_g=post_ffn_g, w_up=w_up, conv_w=conv_w, conv_b=conv_b, w_down=w_down, loss_target=loss_target, m_w_ada=m_w_ada, m_b_ada=m_b_ada, m_pre_mix_g=m_pre_mix_g, m_post_mix_g=m_post_mix_g, m_w_in=m_w_in, m_sgu_norm_g=m_sgu_norm_g, m_w_spatial=m_w_spatial, m_b_spatial=m_b_spatial, m_w_pool=m_w_pool, m_pool_scale=m_pool_scale, m_w_out=m_w_out, m_pre_ffn_g=m_pre_ffn_g, m_post_ffn_g=m_post_ffn_g, m_w_up=m_w_up, m_conv_w=m_conv_w, m_conv_b=m_conv_b, m_w_down=m_w_down, v_w_ada=v_w_ada, v_b_ada=v_b_ada, v_pre_mix_g=v_pre_mix_g, v_post_mix_g=v_post_mix_g, v_w_in=v_w_in, v_sgu_norm_g=v_sgu_norm_g, v_w_spatial=v_w_spatial, v_b_spatial=v_b_spatial, v_w_pool=v_w_pool, v_pool_scale=v_pool_scale, v_w_out=v_w_out, v_pre_ffn_g=v_pre_ffn_g, v_post_ffn_g=v_post_ffn_g, v_w_up=v_w_up, v_conv_w=v_conv_w, v_conv_b=v_conv_b, v_w_down=v_w_down)
    weights = {n: given[n] for n in TWIN_WEIGHTS}
    shared = {n: given[n] for n in SHARED_INPUTS}
    per_example = {n: given[n] for n in ['x', 'c']}
    grad_fn = _jax.value_and_grad(_loss, argnums=(0, 1))

    def one_microbatch(ex, loss_target):
        ex = dict(ex)
        diff = ex.pop(TWIN_DIFF_INPUT)
        return grad_fn(weights, diff, {**shared, **ex}, loss_target)

    if N_MICROBATCH == 1:
        loss, (grad_w, grad_x) = one_microbatch(per_example, given["loss_target"])
    else:
        def body(carry, xs):
            loss_sum, grad_sum = carry
            l_k, (gw_k, gx_k) = one_microbatch(xs[0], xs[1])
            with _jax.named_scope("update"):
                return (loss_sum + l_k, _jax.tree.map(_jnp.add, grad_sum, gw_k)), gx_k

        init = (_jnp.zeros((), _jnp.float32), _jax.tree.map(_jnp.zeros_like, weights))
        (loss, grad_w), grad_x = _jax.lax.scan(body, init, (per_example, given["loss_target"]))
    with _jax.named_scope("update"):
        delta_w, new_m, new_v = {}, {}, {}
        for n in TWIN_WEIGHTS:
            delta_w[n], new_m[n], new_v[n] = _adamw(weights[n], grad_w[n], given["m_" + n], given["v_" + n])
    return (loss, grad_x, *[grad_w[n] for n in TWIN_WEIGHTS], *[delta_w[n] for n in TWIN_WEIGHTS],
            *[new_m[n] for n in TWIN_WEIGHTS], *[new_v[n] for n in TWIN_WEIGHTS])
```

```python
import functools
import math

import jax
import jax.numpy as jnp
from jax import lax
from jax.experimental import pallas as pl
from jax.experimental.pallas import tpu as pltpu

F32 = jnp.float32
BF16 = jnp.bfloat16
MESH = pl.DeviceIdType.MESH

EPS = 1e-6
D = 1024
HEAD = 128
N_HEAD = 4
A_WIDTH = 512
B_WIDTH = 512
IN_WIDTH = 1536
WINDOWS = (2, 4, 8, 16)
CHUNK = 64
FF = 2816
N_DEV = 8
FF_CHUNK = 704
N_MOD = 6
MOD_COLS = 768

ADAM_LR = 0.001
ADAM_B1 = 0.9
ADAM_B2 = 0.999
ADAM_EPS = 1e-08
ADAM_WD = 0.01
ADAM_STEP = 10

VMEM_LIMIT_V7X = 56 * 1024 * 1024
HALO = 8

VMEM_SPEC = pl.BlockSpec(memory_space=pltpu.VMEM)
ANY_SPEC = pl.BlockSpec(memory_space=pl.ANY)


def _bf(x):
    return x.astype(BF16)


def _mm(a, b):
    return jnp.dot(a, b, preferred_element_type=F32)


def _mm_nt(a, b):
    return lax.dot_general(a, b, (((1,), (1,)), ((), ())), preferred_element_type=F32)


def _mm_tn(a, b):
    return lax.dot_general(a, b, (((0,), (0,)), ((), ())), preferred_element_type=F32)


def _mm_f32(a, b):
    return jnp.dot(a, b, preferred_element_type=F32, precision=lax.Precision.HIGHEST)


def _rstd(x):
    return lax.rsqrt(jnp.mean(x * x, axis=-1, keepdims=True) + EPS)


def _sum0(x):
    return jnp.sum(x, axis=0, keepdims=True)


def _rowmean(x):
    return jnp.mean(x, axis=-1, keepdims=True)


_GELU_K = math.sqrt(2.0 / math.pi)


def _gelu_and_grad(x):
    x2 = x * x
    th = jnp.tanh(_GELU_K * (x + 0.044715 * (x * x2)))
    cdf = 0.5 * (1.0 + th)
    grad = cdf + 0.5 * x * (1.0 - th * th) * (_GELU_K * (1.0 + 3.0 * 0.044715 * x2))
    return x * cdf, grad


def _gelu(x):
    return x * (0.5 * (1.0 + jnp.tanh(_GELU_K * (x + 0.044715 * (x * x * x)))))


def _sigmoid(x):
    return 1.0 / (1.0 + jnp.exp(-x))


def _sgu_mask():
    ri = lax.broadcasted_iota(jnp.int32, (HEAD, HEAD), 0)
    ci = lax.broadcasted_iota(jnp.int32, (HEAD, HEAD), 1)
    return (ci // CHUNK) <= (ri // CHUNK)


def _band(w, transposed):
    ii = lax.broadcasted_iota(jnp.int32, (HEAD, 2 * HEAD), 0)
    jj = lax.broadcasted_iota(jnp.int32, (HEAD, 2 * HEAD), 1)
    dist = (jj - ii) if transposed else (ii + HEAD - jj)
    return jnp.where((dist >= 0) & (dist < w), 1.0, 0.0).astype(F32)


def _inv_count(row0, w):
    t = row0 + lax.broadcasted_iota(jnp.int32, (HEAD, 1), 0)
    return 1.0 / jnp.minimum(t + 1, w).astype(F32)


def _adamw(w, g, m, v):
    m = ADAM_B1 * m + (1.0 - ADAM_B1) * g
    v = ADAM_B2 * v + (1.0 - ADAM_B2) * (g * g)
    m_hat = m / (1.0 - ADAM_B1 ** ADAM_STEP)
    v_hat = v / (1.0 - ADAM_B2 ** ADAM_STEP)
    delta = -ADAM_LR * (m_hat / (jnp.sqrt(v_hat) + ADAM_EPS) + ADAM_WD * w)
    return delta, m, v


def _coords():
    return lax.axis_index("x"), lax.axis_index("y"), lax.axis_index("c")


def _peer(k):
    x, y, c = _coords()
    return (x ^ ((k >> 2) & 1), y ^ ((k >> 1) & 1), c ^ (k & 1))


def _my_index():
    x, y, c = _coords()
    return 4 * x + 2 * y + c


def _adaln_fwd(c_row, w_ada, b_ada3):
    def body(c_ref, w_ref, b_ref, mod_ref, scx_ref, scbuf, stage, recv, send_sems, recv_sems):
        me = _my_index()
        cv = c_ref[...]
        scbuf[0] = cv * _sigmoid(cv)
        first = [
            pltpu.make_async_remote_copy(scbuf.at[0], scbuf.at[k], send_sems.at[0, k], recv_sems.at[0, k],
                                         device_id=_peer(k), device_id_type=MESH)
            for k in range(1, N_DEV)
        ]
        for cp in first:
            cp.start()
        for cp in first:
            cp.wait()
        scx_ref[...] = jnp.zeros(scx_ref.shape, F32)
        for k in range(N_DEV):
            scx_ref[k:k + 1, :] = scbuf[k]
        prod = _mm(_bf(scx_ref[...]), _bf(w_ref[...]))
        for k in range(N_DEV):
            stage[k] = prod[k:k + 1, :] + b_ref[me]
        second = [
            pltpu.make_async_remote_copy(stage.at[k], recv.at[k], send_sems.at[1, k], recv_sems.at[1, k],
                                         device_id=_peer(k), device_id_type=MESH)
            for k in range(1, N_DEV)
        ]
        for cp in second:
            cp.start()
        mod_ref[me] = stage[0]
        for cp in second:
            cp.wait()
        for k in range(1, N_DEV):
            mod_ref[me ^ k] = recv[k]

    return pl.pallas_call(
        body,
        name="adaln_fwd",
        out_shape=(jax.ShapeDtypeStruct((N_DEV, 1, MOD_COLS), F32), jax.ShapeDtypeStruct((2 * N_DEV, D), F32)),
        in_specs=[VMEM_SPEC] * 3,
        out_specs=(VMEM_SPEC, VMEM_SPEC),
        scratch_shapes=[
            pltpu.VMEM((N_DEV, 1, D), F32),
            pltpu.VMEM((N_DEV, 1, MOD_COLS), F32),
            pltpu.VMEM((N_DEV, 1, MOD_COLS), F32),
            pltpu.SemaphoreType.DMA((2, N_DEV)),
            pltpu.SemaphoreType.DMA((2, N_DEV)),
        ],
        compiler_params=pltpu.CompilerParams(vmem_limit_bytes=VMEM_LIMIT_V7X),
    )(c_row, w_ada, b_ada3)


def _adaln_bwd(dmod3, scx, w_ada, m_ada, v_ada):
    def body(dm_ref, scx_ref, w_ref, m_ref, v_ref, g_ref, d_ref, nm_ref, nv_ref, recv, dm2d, send_sems, recv_sems):
        me = _my_index()
        copies = [
            pltpu.make_async_remote_copy(dm_ref.at[me ^ k], recv.at[k], send_sems.at[k], recv_sems.at[k],
                                         device_id=_peer(k), device_id_type=MESH)
            for k in range(1, N_DEV)
        ]
        for cp in copies:
            cp.start()
        dm2d[...] = jnp.zeros(dm2d.shape, F32)
        dm2d[0:1, :] = dm_ref[me]
        for cp in copies:
            cp.wait()
        for k in range(1, N_DEV):
            dm2d[k:k + 1, :] = recv[k]
        g = _mm_tn(_bf(scx_ref[...]), _bf(dm2d[...]))
        g_ref[...] = g
        delta, m, v = _adamw(w_ref[...], g, m_ref[...], v_ref[...])
        d_ref[...] = delta
        nm_ref[...] = m
        nv_ref[...] = v

    shard = jax.ShapeDtypeStruct((D, MOD_COLS), F32)
    return pl.pallas_call(
        body,
        name="adaln_bwd",
        out_shape=(shard,) * 4,
        in_specs=[VMEM_SPEC] * 5,
        out_specs=(VMEM_SPEC,) * 4,
        scratch_shapes=[
            pltpu.VMEM((N_DEV, 1, MOD_COLS), F32),
            pltpu.VMEM((2 * N_DEV, MOD_COLS), F32),
            pltpu.SemaphoreType.DMA((N_DEV,)),
            pltpu.SemaphoreType.DMA((N_DEV,)),
        ],
        compiler_params=pltpu.CompilerParams(vmem_limit_bytes=VMEM_LIMIT_V7X),
    )(dmod3, scx, w_ada, m_ada, v_ada)


def _allgather_shards(shards, out_dtypes):
    n = len(shards)

    def body(*refs):
        ins, outs = refs[:n], refs[n:2 * n]
        stages = refs[2 * n:3 * n]
        send_sems, recv_sems, local_sems = refs[3 * n:]
        x, y, c = _coords()
        me, sibling = (x, y, c), (x, y, 1 - c)
        chips = [(1 - x, y), (x, 1 - y), (1 - x, 1 - y)]

        def slot(px, py, pc):
            return 4 * px + 2 * py + pc

        def copy(a, k, block, to, src=None):
            dst = outs[a].at[slot(*block)]
            return pltpu.make_async_remote_copy(dst if src is None else src, dst, send_sems.at[a, k], recv_sems.at[a, k],
                                                device_id=to, device_id_type=MESH)

        mine, first, passed = [], [], []
        for a in range(n):
            stages[a][...] = ins[a][...].astype(stages[a].dtype)
            loc = pltpu.make_async_copy(stages[a], outs[a].at[slot(*me)], local_sems.at[a])
            loc.start()
            mine.append(loc)
            cps = [copy(a, 0, me, sibling, src=stages[a])]
            cps += [copy(a, 1 + j, me, (*chip, c), src=stages[a]) for j, chip in enumerate(chips)]
            for cp in cps:
                cp.start()
            first += cps
        for a in range(n):
            for j, chip in enumerate(chips):
                copy(a, 1 + j, (*chip, c), me).wait_recv()
                fwd = copy(a, 4 + j, (*chip, c), sibling)
                fwd.start()
                passed.append(fwd)
        for a in range(n):
            copy(a, 0, sibling, me).wait_recv()
            for j, chip in enumerate(chips):
                copy(a, 4 + j, (*chip, 1 - c), me).wait_recv()
        for cp in first + passed:
            cp.wait_send()
        for loc in mine:
            loc.wait()

    return pl.pallas_call(
        body,
        name="weight_allgather",
        out_shape=tuple(jax.ShapeDtypeStruct((N_DEV, *s.shape), dt) for s, dt in zip(shards, out_dtypes)),
        in_specs=[VMEM_SPEC] * n,
        out_specs=(ANY_SPEC,) * n,
        scratch_shapes=[pltpu.VMEM(s.shape, dt) for s, dt in zip(shards, out_dtypes)]
        + [pltpu.SemaphoreType.DMA((n, 7)), pltpu.SemaphoreType.DMA((n, 7)), pltpu.SemaphoreType.DMA((n,))],
        compiler_params=pltpu.CompilerParams(vmem_limit_bytes=VMEM_LIMIT_V7X),
    )(*shards)


def _mix_fwd(x, mod, g_pre, g_post, w_in_b, sgn, w_sp, b_sp_t, w_pool, p_scale, w_out_b, ts):
    t_len = x.shape[0]
    nt, nb = t_len // ts, ts // HEAD

    def body(x_ref, mod_ref, g1_ref, g2_ref, win_ref, sgn_ref, ws_ref, bst_ref, wp_ref, ps_ref, wout_ref,
             x1_ref, proj_ref, mixed_ref, pbuf, cat):
        i = pl.program_id(0)

        @pl.when(i == 0)
        def _():
            pbuf[0:HEAD, :] = jnp.zeros((HEAD, B_WIDTH), F32)

        xv = x_ref[...]
        sh, sc, gm = mod_ref[0:1, :], mod_ref[1:2, :], mod_ref[2:3, :]
        h = (xv * _rstd(xv) * g1_ref[...]) * (1.0 + sc) + sh
        proj_ref[...] = _mm(_bf(h), win_ref[...])
        pbuf[HEAD:HEAD + ts, :] = proj_ref[:, 2 * A_WIDTH:]
        smask = _sgu_mask()
        for hd in range(N_HEAD):
            u = _gelu(proj_ref[:, hd * HEAD:(hd + 1) * HEAD])
            v = _gelu(proj_ref[:, A_WIDTH + hd * HEAD:A_WIDTH + (hd + 1) * HEAD])
            vn = _bf(v * _rstd(v) * sgn_ref[hd:hd + 1, :])
            wm = _bf(jnp.where(smask, ws_ref[hd], 0.0))
            bias = bst_ref[:, hd:hd + 1]
            for b in range(nb):
                rows = slice(b * HEAD, (b + 1) * HEAD)
                z = _mm(wm, vn[rows]) + bias
                cat[rows, hd * HEAD:(hd + 1) * HEAD] = _bf(u[rows] * z)
        for g, w in enumerate(WINDOWS):
            cols = slice(g * HEAD, (g + 1) * HEAD)
            band = _band(w, False)
            wpg = _bf(wp_ref[g])
            psg = ps_ref[:, cols]
            for b in range(nb):
                seg = pbuf[b * HEAD:(b + 2) * HEAD, cols]
                pooled = _mm_f32(band, seg) * _inv_count(i * ts + b * HEAD, w) - seg[HEAD:]
                cat[b * HEAD:(b + 1) * HEAD, A_WIDTH + g * HEAD:A_WIDTH + (g + 1) * HEAD] = _bf(_mm(_bf(pooled), wpg) * psg)
        pbuf[0:HEAD, :] = pbuf[ts:ts + HEAD, :]
        mixed = _mm(cat[...], wout_ref[...])
        mixed_ref[...] = mixed
        x1_ref[...] = xv + gm * (mixed * _rstd(mixed) * g2_ref[...])

    tile = lambda wid: pl.BlockSpec((ts, wid), lambda i: (i, 0))
    return pl.pallas_call(
        body,
        name="mix_fwd",
        grid=(nt,),
        out_shape=(jax.ShapeDtypeStruct((t_len, D), F32), jax.ShapeDtypeStruct((t_len, IN_WIDTH), F32),
                   jax.ShapeDtypeStruct((t_len, D), F32)),
        in_specs=[tile(D)] + [VMEM_SPEC] * 10,
        out_specs=(tile(D), tile(IN_WIDTH), tile(D)),
        scratch_shapes=[pltpu.VMEM((HEAD + ts, B_WIDTH), F32), pltpu.VMEM((ts, D), BF16)],
        compiler_params=pltpu.CompilerParams(dimension_semantics=("arbitrary",), vmem_limit_bytes=VMEM_LIMIT_V7X),
    )(x, mod, g_pre, g_post, w_in_b, sgn, w_sp, b_sp_t, w_pool, p_scale, w_out_b)


def _ffn_fwd(x1, target, mod, g_pre, g_post, w_up_b, conv_w8, conv_b8, w_down_b, ts):
    t_len = x1.shape[0]
    nt = t_len // ts

    def body(x1_ref, tgt_ref, mod_ref, g3_ref, g4_ref, wup_ref, cw_ref, cb_ref, wdown_ref,
             up_ref, f_ref, dx2_ref, loss_ref, ubuf, ucarry):
        i = pl.program_id(0)

        @pl.when(i == 0)
        def _():
            ucarry[...] = jnp.zeros(ucarry.shape, F32)
            loss_ref[...] = jnp.zeros(loss_ref.shape, F32)

        x1v = x1_ref[...]
        sh, sc, gf = mod_ref[3:4, :], mod_ref[4:5, :], mod_ref[5:6, :]
        h2 = _bf((x1v * _rstd(x1v) * g3_ref[...]) * (1.0 + sc) + sh)
        f = jnp.zeros((ts, D), F32)
        for j in range(N_DEV // 2):
            ys = []
            for jj in (j, j + N_DEV // 2):
                up = _mm(h2, wup_ref[jj])
                up_ref[jj] = up
                ubuf[0:HALO, :] = ucarry[jj]
                ubuf[HALO:HALO + ts, :] = up
                ucarry[jj] = up[ts - HALO:, :]
                cw = cw_ref[jj]
                ys.append(cb_ref[jj:jj + 1, :] + ubuf[HALO - 2:HALO - 2 + ts, :] * cw[0:1, :]
                          + ubuf[HALO - 1:HALO - 1 + ts, :] * cw[1:2, :] + up * cw[2:3, :])
            gate, val = ys
            act = gate * _sigmoid(gate) * val
            f = f + _mm(_bf(act), wdown_ref[j * FF_CHUNK:(j + 1) * FF_CHUNK, :])
        f_ref[...] = f
        x2 = x1v + gf * (f * _rstd(f) * g4_ref[...])
        err = x2 - tgt_ref[...]
        loss_ref[...] += _sum0(err * err)
        dx2_ref[...] = err * (1.0 / D)

    tile = pl.BlockSpec((ts, D), lambda i: (i, 0))
    return pl.pallas_call(
        body,
        name="ffn_fwd",
        grid=(nt,),
        out_shape=(jax.ShapeDtypeStruct((N_DEV, t_len, FF_CHUNK), F32), jax.ShapeDtypeStruct((t_len, D), F32),
                   jax.ShapeDtypeStruct((t_len, D), F32), jax.ShapeDtypeStruct((1, D), F32)),
        in_specs=[tile, tile] + [VMEM_SPEC] * 7,
        out_specs=(pl.BlockSpec((N_DEV, ts, FF_CHUNK), lambda i: (0, i, 0)), tile, tile,
                   pl.BlockSpec((1, D), lambda i: (0, 0))),
        scratch_shapes=[pltpu.VMEM((HALO + ts, FF_CHUNK), F32), pltpu.VMEM((N_DEV, HALO, FF_CHUNK), F32)],
        compiler_params=pltpu.CompilerParams(dimension_semantics=("arbitrary",), vmem_limit_bytes=VMEM_LIMIT_V7X),
    )(x1, target, mod, g_pre, g_post, w_up_b, conv_w8, conv_b8, w_down_b)


def _ffn_bwd(dx2, f, x1, up, mod, g_pre, g_post, w_up_b, conv_w8, conv_b8, w_down_b, ts):
    t_len = x1.shape[0]
    nt = t_len // ts
    half = N_DEV // 2

    def body(dx2_ref, f_ref, x1_ref, up_ref, halo_ref, mod_ref, g3_ref, g4_ref, wup_ref, cw_ref, cb_ref, wdown_ref,
             dx1_ref, dup_ref, act_ref, df_ref, h2_ref, dmod_ref, dg3_ref, dg4_ref, dcb_ref, dcw_ref,
             ubuf, dybuf, dycarry, dh2acc):
        i = pl.program_id(0)
        r = nt - 1 - i

        @pl.when(i == 0)
        def _():
            for ref in (dmod_ref, dg3_ref, dg4_ref, dcb_ref, dcw_ref, dycarry):
                ref[...] = jnp.zeros(ref.shape, F32)

        dx2v, fv, x1v = dx2_ref[...], f_ref[...], x1_ref[...]
        sh, sc, gf = mod_ref[3:4, :], mod_ref[4:5, :], mod_ref[5:6, :]
        g3, g4 = g3_ref[...], g4_ref[...]
        rstd4 = _rstd(fv)
        fh = fv * rstd4
        dmod_ref[2:3, :] += _sum0(dx2v * (fh * g4))
        dr = dx2v * gf
        dg4_ref[...] += _sum0(dr * fh)
        dfh = dr * g4
        dfb = _bf(rstd4 * (dfh - fh * _rowmean(dfh * fh)))
        df_ref[...] = dfb
        rstd3 = _rstd(x1v)
        xh = x1v * rstd3
        n3 = xh * g3
        h2_ref[...] = _bf(n3 * (1.0 + sc) + sh)
        dh2acc[...] = jnp.zeros((ts, D), F32)
        keep = jnp.where(r > 0, 1.0, 0.0).astype(F32)
        for j in range(half):
            ys = []
            for q, jj in enumerate((j, j + half)):
                ubuf[q, 0:HALO, :] = halo_ref[jj] * keep
                ubuf[q, HALO:HALO + ts, :] = up_ref[jj]
                cw = cw_ref[jj]
                ys.append(cb_ref[jj:jj + 1, :] + ubuf[q, HALO - 2:HALO - 2 + ts, :] * cw[0:1, :]
                          + ubuf[q, HALO - 1:HALO - 1 + ts, :] * cw[1:2, :] + ubuf[q, HALO:HALO + ts, :] * cw[2:3, :])
            gate, val = ys
            sg = _sigmoid(gate)
            silu = gate * sg
            act_ref[j] = _bf(silu * val)
            dact = _mm_nt(dfb, wdown_ref[j * FF_CHUNK:(j + 1) * FF_CHUNK, :])
            dys = (dact * val * (sg * (1.0 + gate * (1.0 - sg))), dact * silu)
            for q, jj in enumerate((j, j + half)):
                dy = dys[q]
                cw = cw_ref[jj]
                dcb_ref[jj:jj + 1, :] += _sum0(dy)
                for k in range(3):
                    dcw_ref[jj, k:k + 1, :] += _sum0(dy * ubuf[q, HALO - 2 + k:HALO - 2 + k + ts, :])
                dybuf[0:ts, :] = dy
                dybuf[ts:ts + HALO, :] = dycarry[jj]
                dycarry[jj] = dy[0:HALO, :]
                dup = _bf(dy * cw[2:3, :] + dybuf[1:1 + ts, :] * cw[1:2, :] + dybuf[2:2 + ts, :] * cw[0:1, :])
                dup_ref[jj] = dup
                dh2acc[...] += _mm_nt(dup, wup_ref[jj])
        dh2 = dh2acc[...]
        dmod_ref[0:1, :] += _sum0(dh2)
        dmod_ref[1:2, :] += _sum0(dh2 * n3)
        dn3 = dh2 * (1.0 + sc)
        dg3_ref[...] += _sum0(dn3 * xh)
        dxh = dn3 * g3
        dx1_ref[...] = dx2v + rstd3 * (dxh - xh * _rowmean(dxh * xh))

    tile = pl.BlockSpec((ts, D), lambda i: (nt - 1 - i, 0))
    chunked = lambda n: pl.BlockSpec((n, ts, FF_CHUNK), lambda i: (0, nt - 1 - i, 0))
    halo = pl.BlockSpec((N_DEV, HALO, FF_CHUNK), lambda i: (0, jnp.maximum((nt - 1 - i) * (ts // HALO) - 1, 0), 0))
    const = lambda *shape: pl.BlockSpec(shape, lambda i: (0,) * len(shape))
    return pl.pallas_call(
        body,
        name="ffn_bwd",
        grid=(nt,),
        out_shape=(jax.ShapeDtypeStruct((t_len, D), F32), jax.ShapeDtypeStruct((N_DEV, t_len, FF_CHUNK), BF16),
                   jax.ShapeDtypeStruct((half, t_len, FF_CHUNK), BF16), jax.ShapeDtypeStruct((t_len, D), BF16),
                   jax.ShapeDtypeStruct((t_len, D), BF16), jax.ShapeDtypeStruct((3, D), F32),
                   jax.ShapeDtypeStruct((1, D), F32), jax.ShapeDtypeStruct((1, D), F32),
                   jax.ShapeDtypeStruct((N_DEV, FF_CHUNK), F32), jax.ShapeDtypeStruct((N_DEV, 3, FF_CHUNK), F32)),
        in_specs=[tile, tile, tile, chunked(N_DEV), halo] + [VMEM_SPEC] * 7,
        out_specs=(tile, chunked(N_DEV), chunked(half), tile, tile, const(3, D), const(1, D), const(1, D),
                   const(N_DEV, FF_CHUNK), const(N_DEV, 3, FF_CHUNK)),
        scratch_shapes=[pltpu.VMEM((2, HALO + ts, FF_CHUNK), F32), pltpu.VMEM((ts + HALO, FF_CHUNK), F32),
                        pltpu.VMEM((N_DEV, HALO, FF_CHUNK), F32), pltpu.VMEM((ts, D), F32)],
        compiler_params=pltpu.CompilerParams(dimension_semantics=("arbitrary",), vmem_limit_bytes=VMEM_LIMIT_V7X),
    )(dx2, f, x1, up, up, mod, g_pre, g_post, w_up_b, conv_w8, conv_b8, w_down_b)


def _wgrad_up(h2, dup, ts):
    t_len = h2.shape[0]
    nt, half = t_len // ts, N_DEV // 2

    def body(h2_ref, dup_ref, out_ref):
        @pl.when(pl.program_id(1) == 0)
        def _():
            out_ref[...] = jnp.zeros(out_ref.shape, F32)

        for q in range(half):
            out_ref[q] += _mm_tn(h2_ref[...], dup_ref[q])

    return pl.pallas_call(
        body,
        name="wgrad_up",
        grid=(2, nt),
        out_shape=jax.ShapeDtypeStruct((N_DEV, D, FF_CHUNK), F32),
        in_specs=[pl.BlockSpec((ts, D), lambda g, t: (t, 0)), pl.BlockSpec((half, ts, FF_CHUNK), lambda g, t: (g, t, 0))],
        out_specs=pl.BlockSpec((half, D, FF_CHUNK), lambda g, t: (g, 0, 0)),
        compiler_params=pltpu.CompilerParams(dimension_semantics=("arbitrary", "arbitrary"),
                                             vmem_limit_bytes=VMEM_LIMIT_V7X),
    )(h2, dup)


def _wgrad_down(act, df, ts):
    t_len = df.shape[0]
    nt, half = t_len // ts, N_DEV // 2

    def body(act_ref, df_ref, out_ref):
        @pl.when(pl.program_id(0) == 0)
        def _():
            out_ref[...] = jnp.zeros(out_ref.shape, F32)

        for q in range(half):
            out_ref[q] += _mm_tn(act_ref[q], df_ref[...])

    return pl.pallas_call(
        body,
        name="wgrad_down",
        grid=(nt,),
        out_shape=jax.ShapeDtypeStruct((half, FF_CHUNK, D), F32),
        in_specs=[pl.BlockSpec((half, ts, FF_CHUNK), lambda t: (0, t, 0)), pl.BlockSpec((ts, D), lambda t: (t, 0))],
        out_specs=pl.BlockSpec((half, FF_CHUNK, D), lambda t: (0, 0, 0)),
        compiler_params=pltpu.CompilerParams(dimension_semantics=("arbitrary",), vmem_limit_bytes=VMEM_LIMIT_V7X),
    )(act, df)


def _mix_bwd(dx1, x, proj, mixed, mod, g_pre, g_post, w_in_b, sgn, w_sp, b_sp_t, w_pool, p_scale, w_out_b, ts):
    t_len = x.shape[0]
    nt, nb = t_len // ts, ts // HEAD

    def body(dx1_ref, x_ref, proj_ref, halo_ref, mixed_ref, mod_ref, g1_ref, g2_ref, win_ref, sgn_ref, ws_ref,
             bst_ref, wp_ref, ps_ref, wout_ref,
             gx_ref, dwin_ref, dwout_ref, dmod_ref, dg1_ref, dg2_ref, dsgn_ref, dws_ref, dbst_ref, dwp_ref, dps_ref,
             pbuf, dwsbuf, cat, dproj, dcat):
        i = pl.program_id(0)
        r = nt - 1 - i

        @pl.when(i == 0)
        def _():
            for ref in (dwin_ref, dwout_ref, dmod_ref, dg1_ref, dg2_ref, dsgn_ref, dws_ref, dbst_ref, dwp_ref, dps_ref):
                ref[...] = jnp.zeros(ref.shape, F32)
            dwsbuf[ts:ts + HEAD, :] = jnp.zeros((HEAD, B_WIDTH), F32)

        xv, dx1v, mixed = x_ref[...], dx1_ref[...], mixed_ref[...]
        sh, sc, gm = mod_ref[0:1, :], mod_ref[1:2, :], mod_ref[2:3, :]
        g1, g2 = g1_ref[...], g2_ref[...]
        rstd2 = _rstd(mixed)
        mh = mixed * rstd2
        dmod_ref[2:3, :] += _sum0(dx1v * (mh * g2))
        dr = dx1v * gm
        dg2_ref[...] += _sum0(dr * mh)
        dmh = dr * g2
        dmb = _bf(rstd2 * (dmh - mh * _rowmean(dmh * mh)))
        dcat[...] = _mm_nt(dmb, wout_ref[...])
        smask = _sgu_mask()
        for hd in range(N_HEAD):
            ucols = slice(hd * HEAD, (hd + 1) * HEAD)
            vcols = slice(A_WIDTH + hd * HEAD, A_WIDTH + (hd + 1) * HEAD)
            u, du_dp = _gelu_and_grad(proj_ref[:, ucols])
            v, dv_dp = _gelu_and_grad(proj_ref[:, vcols])
            rs = _rstd(v)
            vhat = v * rs
            gn = sgn_ref[hd:hd + 1, :]
            vn = _bf(vhat * gn)
            wm = _bf(jnp.where(smask, ws_ref[hd], 0.0))
            bias = bst_ref[:, hd:hd + 1]
            dzsum = jnp.zeros((HEAD, HEAD), F32)
            dwm = jnp.zeros((HEAD, HEAD), F32)
            dvn_parts = []
            for b in range(nb):
                rows = slice(b * HEAD, (b + 1) * HEAD)
                z = _mm(wm, vn[rows]) + bias
                da = dcat[rows, ucols]
                cat[rows, ucols] = _bf(u[rows] * z)
                dz = da * u[rows]
                dzsum = dzsum + dz
                dzb = _bf(dz)
                dwm = dwm + _mm_nt(dzb, vn[rows])
                dvn_parts.append(_mm_tn(wm, dzb))
                dproj[rows, ucols] = (da * z) * du_dp[rows]
            dvn = jnp.concatenate(dvn_parts, axis=0)
            dsgn_ref[hd:hd + 1, :] += _sum0(dvn * vhat)
            dvh = dvn * gn
            dproj[:, vcols] = (rs * (dvh - vhat * _rowmean(dvh * vhat))) * dv_dp
            dws_ref[hd] += jnp.where(smask, dwm, 0.0)
            dbst_ref[:, hd:hd + 1] += jnp.sum(dzsum, axis=1, keepdims=True)
        keep = jnp.where(r > 0, 1.0, 0.0).astype(F32)
        pbuf[0:HEAD, :] = halo_ref[...] * keep
        pbuf[HEAD:HEAD + ts, :] = proj_ref[:, 2 * A_WIDTH:]
        for g, w in enumerate(WINDOWS):
            cols = slice(g * HEAD, (g + 1) * HEAD)
            ccols = slice(A_WIDTH + g * HEAD, A_WIDTH + (g + 1) * HEAD)
            pcols = slice(2 * A_WIDTH + g * HEAD, 2 * A_WIDTH + (g + 1) * HEAD)
            band, band_t = _band(w, False), _band(w, True)
            wpg = _bf(wp_ref[g])
            psg = ps_ref[:, cols]
            dps = jnp.zeros((1, HEAD), F32)
            dwp = jnp.zeros((HEAD, HEAD), F32)
            for b in range(nb):
                rows = slice(b * HEAD, (b + 1) * HEAD)
                seg = pbuf[b * HEAD:(b + 2) * HEAD, cols]
                inv = _inv_count(r * ts + b * HEAD, w)
                pb = _bf(_mm_f32(band, seg) * inv - seg[HEAD:])
                yb = _mm(pb, wpg)
                dob = dcat[rows, ccols]
                cat[rows, ccols] = _bf(yb * psg)
                dps = dps + _sum0(dob * yb)
                dyb = _bf(dob * psg)
                dwp = dwp + _mm_tn(pb, dyb)
                dpooled = _mm_nt(dyb, wpg)
                dwsbuf[rows, cols] = dpooled * inv
                dproj[rows, pcols] = -dpooled
            for b in range(nb):
                rows = slice(b * HEAD, (b + 1) * HEAD)
                dproj[rows, pcols] += _mm_f32(band_t, dwsbuf[b * HEAD:(b + 2) * HEAD, cols])
            dwp_ref[g] += dwp
            dps_ref[:, cols] += dps
        dwsbuf[ts:ts + HEAD, :] = dwsbuf[0:HEAD, :]
        dpb = _bf(dproj[...])
        rstd1 = _rstd(xv)
        xh = xv * rstd1
        n1 = xh * g1
        dwin_ref[...] += _mm_tn(_bf(n1 * (1.0 + sc) + sh), dpb)
        dwout_ref[...] += _mm_tn(cat[...], dmb)
        dh = _mm_nt(dpb, win_ref[...])
        dmod_ref[0:1, :] += _sum0(dh)
        dmod_ref[1:2, :] += _sum0(dh * n1)
        dn1 = dh * (1.0 + sc)
        dg1_ref[...] += _sum0(dn1 * xh)
        dxh = dn1 * g1
        gx_ref[...] = dx1v + rstd1 * (dxh - xh * _rowmean(dxh * xh))

    tile = lambda wid: pl.BlockSpec((ts, wid), lambda i: (nt - 1 - i, 0))
    halo = pl.BlockSpec((HEAD, B_WIDTH), lambda i: (jnp.maximum((nt - 1 - i) * nb - 1, 0), 2 * A_WIDTH // B_WIDTH))
    const = lambda *shape: pl.BlockSpec(shape, lambda i: (0,) * len(shape))
    return pl.pallas_call(
        body,
        name="mix_bwd",
        grid=(nt,),
        out_shape=(jax.ShapeDtypeStruct((t_len, D), F32), jax.ShapeDtypeStruct((D, IN_WIDTH), F32),
                   jax.ShapeDtypeStruct((D, D), F32), jax.ShapeDtypeStruct((3, D), F32),
                   jax.ShapeDtypeStruct((1, D), F32), jax.ShapeDtypeStruct((1, D), F32),
                   jax.ShapeDtypeStruct((N_HEAD, HEAD), F32), jax.ShapeDtypeStruct((N_HEAD, HEAD, HEAD), F32),
                   jax.ShapeDtypeStruct((HEAD, N_HEAD), F32), jax.ShapeDtypeStruct((N_HEAD, HEAD, HEAD), F32),
                   jax.ShapeDtypeStruct((1, B_WIDTH), F32)),
        in_specs=[tile(D), tile(D), tile(IN_WIDTH), halo, tile(D)] + [VMEM_SPEC] * 10,
        out_specs=(tile(D), const(D, IN_WIDTH), const(D, D), const(3, D), const(1, D), const(1, D),
                   const(N_HEAD, HEAD), const(N_HEAD, HEAD, HEAD), const(HEAD, N_HEAD), const(N_HEAD, HEAD, HEAD),
                   const(1, B_WIDTH)),
        scratch_shapes=[pltpu.VMEM((HEAD + ts, B_WIDTH), F32), pltpu.VMEM((ts + HEAD, B_WIDTH), F32),
                        pltpu.VMEM((ts, D), BF16), pltpu.VMEM((ts, IN_WIDTH), F32), pltpu.VMEM((ts, D), F32)],
        compiler_params=pltpu.CompilerParams(dimension_semantics=("arbitrary",), vmem_limit_bytes=VMEM_LIMIT_V7X),
    )(dx1, x, proj, proj, mixed, mod, g_pre, g_post, w_in_b, sgn, w_sp, b_sp_t, w_pool, p_scale, w_out_b)


def _exchange(name, srcs, out_shapes, plan):
    n = len(srcs)
    n_copies = len(plan(0, 0, 0))

    def body(*refs):
        ins, outs = refs[:n], refs[n:2 * n]
        send_sems, recv_sems = refs[2 * n:]
        copies = [
            pltpu.make_async_remote_copy(ins[a].at[src], outs[a].at[dst] if dst else outs[a], send_sems.at[k],
                                         recv_sems.at[k], device_id=peer, device_id_type=MESH)
            for k, (a, src, dst, peer) in enumerate(plan(*_coords()))
        ]
        for cp in copies:
            cp.start()
        for cp in copies:
            cp.wait()

    return pl.pallas_call(
        body,
        name=name,
        out_shape=tuple(jax.ShapeDtypeStruct(s, F32) for s in out_shapes),
        in_specs=[ANY_SPEC] * n,
        out_specs=(ANY_SPEC,) * n,
        scratch_shapes=[pltpu.SemaphoreType.DMA((n_copies,)), pltpu.SemaphoreType.DMA((n_copies,))],
    )(*srcs)


def _pair_add(name, coords, grid, specs_a, specs_b, out_specs, out_shapes, a_arrays, b_arrays):
    n = len(a_arrays)

    def body(co_ref, *refs):
        for k in range(n):
            refs[2 * n + k][...] = refs[k][...] + refs[n + k][...]

    return pl.pallas_call(
        body,
        name=name,
        grid_spec=pltpu.PrefetchScalarGridSpec(num_scalar_prefetch=1, grid=grid, in_specs=specs_a + specs_b,
                                               out_specs=out_specs),
        out_shape=tuple(jax.ShapeDtypeStruct(s, F32) for s in out_shapes),
        compiler_params=pltpu.CompilerParams(dimension_semantics=("arbitrary",) * len(grid),
                                             vmem_limit_bytes=VMEM_LIMIT_V7X),
    )(coords, *a_arrays, *b_arrays)


def _final_add_adamw(coords, x_first, s2, r3, ws, ms, vs, n_split):
    n = len(s2)

    def body(co_ref, *refs):
        for k in range(n):
            s_ref, r_ref, w_ref, m_ref, v_ref = (refs[q * n + k] for q in range(5))
            g_ref, d_ref, nm_ref, nv_ref = (refs[(5 + q) * n + k] for q in range(4))
            g = s_ref[...] + r_ref[...]
            g_ref[...] = g
            delta, m, v = _adamw(w_ref[...], g, m_ref[...], v_ref[...])
            d_ref[...] = delta
            nm_ref[...] = m
            nv_ref[...] = v

    def shard_spec(a):
        rows, cols = a.shape
        return pl.BlockSpec((rows // n_split, cols), lambda i, co: (i, 0))

    def picked_spec(a, xf):
        _, rows, cols = a.shape
        axis = 1 if xf else 0
        return pl.BlockSpec((None, rows // n_split, cols), lambda i, co: (co[axis], i, 0))

    in_specs = ([picked_spec(a, xf) for a, xf in zip(s2, x_first)] + [shard_spec(a) for a in r3]
                + [shard_spec(a) for a in ws] * 3)
    out_specs = [shard_spec(a) for a in ws] * 4
    outs = pl.pallas_call(
        body,
        name="grad_final_adamw",
        grid_spec=pltpu.PrefetchScalarGridSpec(num_scalar_prefetch=1, grid=(n_split,), in_specs=in_specs,
                                               out_specs=out_specs),
        out_shape=tuple(jax.ShapeDtypeStruct(a.shape, F32) for a in ws) * 4,
        compiler_params=pltpu.CompilerParams(dimension_semantics=("arbitrary",), vmem_limit_bytes=VMEM_LIMIT_V7X),
    )(coords, *s2, *r3, *ws, *ms, *vs)
    return [tuple(outs[q * n + k] for q in range(4)) for k in range(n)]


def _reduce_scatter_adamw(grads, x_first, ws, ms, vs, coords, n_split=4):
    n = len(grads)
    shapes = [g.shape[1:] for g in grads]
    g5 = [g.reshape(2, 2, 2, *s) for g, s in zip(grads, shapes)]

    def plan1(x, y, c):
        return [(a, (xs, ys, 1 - c), (xs, ys), (x, y, 1 - c)) for a in range(n) for xs in range(2) for ys in range(2)]

    r1 = _exchange("grad_swap_core", g5, [(2, 2, *s) for s in shapes], plan1)

    spec_g = [pl.BlockSpec((None, None, None, s[0] // n_split, s[1]), lambda i, j, k, co: (i, j, co[2], k, 0))
              for s in shapes]
    spec_r = [pl.BlockSpec((None, None, s[0] // n_split, s[1]), lambda i, j, k, co: (i, j, k, 0)) for s in shapes]
    s1 = _pair_add("grad_add_core", coords, (2, 2, n_split), spec_g, spec_r, spec_r, [(2, 2, *s) for s in shapes],
                   g5, list(r1))

    def plan2(x, y, c):
        out = []
        for a in range(n):
            for o in range(2):
                if x_first[a]:
                    out.append((a, (1 - x, o), (o,), (1 - x, y, c)))
                else:
                    out.append((a, (o, 1 - y), (o,), (x, 1 - y, c)))
        return out

    r2 = _exchange("grad_swap_chip_a", list(s1), [(2, *s) for s in shapes], plan2)
    spec_s1 = [
        pl.BlockSpec((None, None, s[0] // n_split, s[1]),
                     (lambda o, k, co: (co[0], o, k, 0)) if xf else (lambda o, k, co: (o, co[1], k, 0)))
        for s, xf in zip(shapes, x_first)
    ]
    spec_r2 = [pl.BlockSpec((None, s[0] // n_split, s[1]), lambda o, k, co: (o, k, 0)) for s in shapes]
    s2 = _pair_add("grad_add_chip_a", coords, (2, n_split), spec_s1, spec_r2, spec_r2, [(2, *s) for s in shapes],
                   list(s1), list(r2))

    def plan3(x, y, c):
        out = []
        for a in range(n):
            if x_first[a]:
                out.append((a, (1 - y,), (), (x, 1 - y, c)))
            else:
                out.append((a, (1 - x,), (), (1 - x, y, c)))
        return out

    r3 = _exchange("grad_swap_chip_b", list(s2), shapes, plan3)
    return _final_add_adamw(coords, x_first, list(s2), list(r3), ws, ms, vs, n_split)


def _small_allreduce_adamw(partials, ws, ms, vs, pick_mine):
    n = len(partials)

    def body(*refs):
        p_in = refs[:n]
        w_in, m_in, v_in = refs[n:2 * n], refs[2 * n:3 * n], refs[3 * n:4 * n]
        g_out, d_out, nm_out, nv_out = (refs[(4 + q) * n:(5 + q) * n] for q in range(4))
        acc = refs[8 * n:9 * n]
        rbuf = refs[9 * n:10 * n]
        send_sems, recv_sems = refs[10 * n:]
        x, y, c = _coords()
        me = 4 * x + 2 * y + c
        for a in range(n):
            acc[a][...] = p_in[a][...]
        for ph, peer in enumerate([(x, y, 1 - c), (1 - x, y, c), (x, 1 - y, c)]):
            copies = [
                pltpu.make_async_remote_copy(acc[a], rbuf[a].at[ph], send_sems.at[ph, a], recv_sems.at[ph, a],
                                             device_id=peer, device_id_type=MESH)
                for a in range(n)
            ]
            for cp in copies:
                cp.start()
            for cp in copies:
                cp.wait()
            for a in range(n):
                acc[a][...] = acc[a][...] + rbuf[a][ph]
        for a in range(n):
            g = acc[a][me] if pick_mine[a] else acc[a][...]
            g_out[a][...] = g
            delta, m, v = _adamw(w_in[a][...], g, m_in[a][...], v_in[a][...])
            d_out[a][...] = delta
            nm_out[a][...] = m
            nv_out[a][...] = v

    w_shapes = tuple(jax.ShapeDtypeStruct(w.shape, F32) for w in ws)
    outs = pl.pallas_call(
        body,
        name="small_allreduce_adamw",
        out_shape=w_shapes * 4,
        in_specs=[VMEM_SPEC] * (4 * n),
        out_specs=(VMEM_SPEC,) * (4 * n),
        scratch_shapes=[pltpu.VMEM(p.shape, F32) for p in partials]
        + [pltpu.VMEM((3, *p.shape), F32) for p in partials]
        + [pltpu.SemaphoreType.DMA((3, n)), pltpu.SemaphoreType.DMA((3, n))],
        compiler_params=pltpu.CompilerParams(vmem_limit_bytes=VMEM_LIMIT_V7X),
    )(*partials, *ws, *ms, *vs)
    return [tuple(outs[q * n + k] for q in range(4)) for k in range(n)]


def kernel(x, c, w_ada, b_ada, pre_mix_g, post_mix_g, w_in, sgu_norm_g, w_spatial, b_spatial, w_pool, pool_scale, w_out, pre_ffn_g, post_ffn_g, w_up, conv_w, conv_b, w_down, loss_target, m_w_ada, m_b_ada, m_pre_mix_g, m_post_mix_g, m_w_in, m_sgu_norm_g, m_w_spatial, m_b_spatial, m_w_pool, m_pool_scale, m_w_out, m_pre_ffn_g, m_post_ffn_g, m_w_up, m_conv_w, m_conv_b, m_w_down, v_w_ada, v_b_ada, v_pre_mix_g, v_post_mix_g, v_w_in, v_sgu_norm_g, v_w_spatial, v_b_spatial, v_w_pool, v_pool_scale, v_w_out, v_pre_ffn_g, v_post_ffn_g, v_w_up, v_conv_w, v_conv_b, v_w_down):
    t_len = x.shape[1]
    ts = min(256, t_len)
    ts_w = min(512, t_len)
    coords = jnp.stack([lax.axis_index("x"), lax.axis_index("y"), lax.axis_index("c")]).astype(jnp.int32)

    mod3, scx = _adaln_fwd(c, w_ada[0], b_ada.reshape(N_DEV, 1, MOD_COLS))
    mod = mod3.reshape(N_MOD, D)

    g_in, g_out, g_up, g_down, g_cw = _allgather_shards(
        [w_in[0], w_out[0], w_up[0], w_down[0], conv_w[0]], [BF16, BF16, BF16, BF16, F32])
    w_in_b = g_in.transpose(1, 0, 2).reshape(D, IN_WIDTH)
    w_out_b = g_out.reshape(D, D)
    w_down_b = g_down.reshape(FF, D)
    conv_b8 = conv_b.reshape(N_DEV, FF_CHUNK)
    b_sp_t = b_spatial[0].T

    x2d, tgt = x[0], loss_target[0]
    x1, proj, mixed = _mix_fwd(x2d, mod, pre_mix_g, post_mix_g, w_in_b, sgu_norm_g[0], w_spatial[0], b_sp_t,
                               w_pool[0], pool_scale, w_out_b, ts)
    up, f, dx2, loss_lanes = _ffn_fwd(x1, tgt, mod, pre_ffn_g, post_ffn_g, g_up, g_cw, conv_b8, w_down_b, ts)
    loss = lax.psum(0.5 * jnp.sum(loss_lanes) / D, ("x", "y", "c"))

    (dx1, dup, act, df, h2, dmod_f, d_pre_ffn, d_post_ffn, d_cb8, d_cw8) = _ffn_bwd(
        dx2, f, x1, up, mod, pre_ffn_g, post_ffn_g, g_up, g_cw, conv_b8, w_down_b, ts)
    gw_up = _wgrad_up(h2, dup, ts_w)
    gw_down = _wgrad_down(act, df, ts_w).reshape(N_DEV, FF // N_DEV, D)
    (grad_x, gw_in, gw_out, dmod_m, d_pre_mix, d_post_mix, d_sgn, d_wsp, d_bsp_t, d_wpool, d_ps) = _mix_bwd(
        dx1, x2d, proj, mixed, mod, pre_mix_g, post_mix_g, w_in_b, sgu_norm_g[0], w_spatial[0], b_sp_t,
        w_pool[0], pool_scale, w_out_b, ts)
    gw_in = gw_in.reshape(D, N_DEV, IN_WIDTH // N_DEV).transpose(1, 0, 2)
    gw_out = gw_out.reshape(N_DEV, D // N_DEV, D)

    big = _reduce_scatter_adamw(
        [gw_in, gw_out, gw_up, gw_down], [False, False, True, False],
        [w_in[0], w_out[0], w_up[0], w_down[0]], [m_w_in[0], m_w_out[0], m_w_up[0], m_w_down[0]],
        [v_w_in[0], v_w_out[0], v_w_up[0], v_w_down[0]], coords)
    r_in, r_out, r_up, r_down = [tuple(a[None] for a in four) for four in big]

    dmod = jnp.concatenate([dmod_m, dmod_f], axis=0)
    r_ada = tuple(a[None] for a in _adaln_bwd(dmod.reshape(N_DEV, 1, MOD_COLS), scx, w_ada[0], m_w_ada[0], v_w_ada[0]))

    names = ["b_ada", "pre_mix_g", "post_mix_g", "sgu_norm_g", "w_spatial", "b_spatial", "w_pool", "pool_scale",
             "pre_ffn_g", "post_ffn_g", "conv_w", "conv_b"]
    partials = [dmod.reshape(1, N_MOD * D), d_pre_mix, d_post_mix, d_sgn, d_wsp, d_bsp_t.T, d_wpool, d_ps,
                d_pre_ffn, d_post_ffn, d_cw8, d_cb8.reshape(1, 2 * FF)]
    small_w = [b_ada, pre_mix_g, post_mix_g, sgu_norm_g[0], w_spatial[0], b_spatial[0], w_pool[0], pool_scale,
               pre_ffn_g, post_ffn_g, conv_w[0], conv_b]
    small_m = [m_b_ada, m_pre_mix_g, m_post_mix_g, m_sgu_norm_g[0], m_w_spatial[0], m_b_spatial[0], m_w_pool[0],
               m_pool_scale, m_pre_ffn_g, m_post_ffn_g, m_conv_w[0], m_conv_b]
    small_v = [v_b_ada, v_pre_mix_g, v_post_mix_g, v_sgu_norm_g[0], v_w_spatial[0], v_b_spatial[0], v_w_pool[0],
               v_pool_scale, v_pre_ffn_g, v_post_ffn_g, v_conv_w[0], v_conv_b]
    small = _small_allreduce_adamw(partials, small_w, small_m, small_v, [nm == "conv_w" for nm in names])
    lead = {"sgu_norm_g", "w_spatial", "b_spatial", "w_pool", "conv_w"}
    res = {nm: tuple(a[None] if nm in lead else a for a in four) for nm, four in zip(names, small)}
    res.update(w_ada=r_ada, w_in=r_in, w_out=r_out, w_up=r_up, w_down=r_down)

    order = ["w_ada", "b_ada", "pre_mix_g", "post_mix_g", "w_in", "sgu_norm_g", "w_spatial", "b_spatial", "w_pool",
             "pool_scale", "w_out", "pre_ffn_g", "post_ffn_g", "w_up", "conv_w", "conv_b", "w_down"]
    return (loss, grad_x[None], *[res[nm][0] for nm in order], *[res[nm][1] for nm in order],
            *[res[nm][2] for nm in order], *[res[nm][3] for nm in order])
```

```python
import functools
import math

import jax
import jax.numpy as jnp
from jax import lax
from jax.experimental import pallas as pl
from jax.experimental.pallas import tpu as pltpu

F32 = jnp.float32
BF16 = jnp.bfloat16
MESH = pl.DeviceIdType.MESH

EPS = 1e-6
D = 1024
HEAD = 128
N_HEAD = 4
A_WIDTH = 512
B_WIDTH = 512
IN_WIDTH = 1536
WINDOWS = (2, 4, 8, 16)
CHUNK = 64
FF = 2816
N_DEV = 8
FF_CHUNK = 704
N_MOD = 6
MOD_COLS = 768

ADAM_LR = 0.001
ADAM_B1 = 0.9
ADAM_B2 = 0.999
ADAM_EPS = 1e-08
ADAM_WD = 0.01
ADAM_STEP = 10

VMEM_LIMIT_V7X = 56 * 1024 * 1024
HALO = 8

VMEM_SPEC = pl.BlockSpec(memory_space=pltpu.VMEM)
ANY_SPEC = pl.BlockSpec(memory_space=pl.ANY)


def _bf(x):
    return x.astype(BF16)


def _mm(a, b):
    return jnp.dot(a, b, preferred_element_type=F32)


def _mm_nt(a, b):
    return lax.dot_general(a, b, (((1,), (1,)), ((), ())), preferred_element_type=F32)


def _mm_tn(a, b):
    return lax.dot_general(a, b, (((0,), (0,)), ((), ())), preferred_element_type=F32)


def _mm_f32(a, b):
    return jnp.dot(a, b, preferred_element_type=F32, precision=lax.Precision.HIGHEST)


def _rstd(x):
    return lax.rsqrt(jnp.mean(x * x, axis=-1, keepdims=True) + EPS)


def _sum0(x):
    return jnp.sum(x, axis=0, keepdims=True)


def _rowmean(x):
    return jnp.mean(x, axis=-1, keepdims=True)


_GELU_K = math.sqrt(2.0 / math.pi)


def _gelu_and_grad(x):
    x2 = x * x
    th = jnp.tanh(_GELU_K * (x + 0.044715 * (x * x2)))
    cdf = 0.5 * (1.0 + th)
    grad = cdf + 0.5 * x * (1.0 - th * th) * (_GELU_K * (1.0 + 3.0 * 0.044715 * x2))
    return x * cdf, grad


def _gelu(x):
    return x * (0.5 * (1.0 + jnp.tanh(_GELU_K * (x + 0.044715 * (x * x * x)))))


def _sigmoid(x):
    return 0.5 * jnp.tanh(0.5 * x) + 0.5


def _sgu_mask():
    ri = lax.broadcasted_iota(jnp.int32, (HEAD, HEAD), 0)
    ci = lax.broadcasted_iota(jnp.int32, (HEAD, HEAD), 1)
    return (ci // CHUNK) <= (ri // CHUNK)


def _band(w, transposed):
    ii = lax.broadcasted_iota(jnp.int32, (HEAD, 2 * HEAD), 0)
    jj = lax.broadcasted_iota(jnp.int32, (HEAD, 2 * HEAD), 1)
    dist = (jj - ii) if transposed else (ii + HEAD - jj)
    return jnp.where((dist >= 0) & (dist < w), 1.0, 0.0).astype(F32)


def _inv_count(row0, w):
    t = row0 + lax.broadcasted_iota(jnp.int32, (HEAD, 1), 0)
    return 1.0 / jnp.minimum(t + 1, w).astype(F32)


def _adamw(w, g, m, v):
    m = ADAM_B1 * m + (1.0 - ADAM_B1) * g
    v = ADAM_B2 * v + (1.0 - ADAM_B2) * (g * g)
    m_hat = m / (1.0 - ADAM_B1 ** ADAM_STEP)
    v_hat = v / (1.0 - ADAM_B2 ** ADAM_STEP)
    delta = -ADAM_LR * (m_hat / (jnp.sqrt(v_hat) + ADAM_EPS) + ADAM_WD * w)
    return delta, m, v


def _coords():
    return lax.axis_index("x"), lax.axis_index("y"), lax.axis_index("c")


def _peer(k):
    x, y, c = _coords()
    return (x ^ ((k >> 2) & 1), y ^ ((k >> 1) & 1), c ^ (k & 1))


def _my_index():
    x, y, c = _coords()
    return 4 * x + 2 * y + c


def _adaln_fwd(c_row, w_ada, b_ada3):
    def body(c_ref, w_ref, b_ref, mod_ref, scx_ref, scbuf, stage, recv, send_sems, recv_sems):
        me = _my_index()
        cv = c_ref[...]
        scbuf[0] = cv * _sigmoid(cv)
        first = [
            pltpu.make_async_remote_copy(scbuf.at[0], scbuf.at[k], send_sems.at[0, k], recv_sems.at[0, k],
                                         device_id=_peer(k), device_id_type=MESH)
            for k in range(1, N_DEV)
        ]
        for cp in first:
            cp.start()
        for cp in first:
            cp.wait()
        scx_ref[...] = jnp.zeros(scx_ref.shape, F32)
        for k in range(N_DEV):
            scx_ref[k:k + 1, :] = scbuf[k]
        prod = _mm(_bf(scx_ref[...]), _bf(w_ref[...]))
        for k in range(N_DEV):
            stage[k] = prod[k:k + 1, :] + b_ref[me]
        second = [
            pltpu.make_async_remote_copy(stage.at[k], recv.at[k], send_sems.at[1, k], recv_sems.at[1, k],
                                         device_id=_peer(k), device_id_type=MESH)
            for k in range(1, N_DEV)
        ]
        for cp in second:
            cp.start()
        mod_ref[me] = stage[0]
        for cp in second:
            cp.wait()
        for k in range(1, N_DEV):
            mod_ref[me ^ k] = recv[k]

    return pl.pallas_call(
        body,
        name="adaln_fwd",
        out_shape=(jax.ShapeDtypeStruct((N_DEV, 1, MOD_COLS), F32), jax.ShapeDtypeStruct((2 * N_DEV, D), F32)),
        in_specs=[VMEM_SPEC] * 3,
        out_specs=(VMEM_SPEC, VMEM_SPEC),
        scratch_shapes=[
            pltpu.VMEM((N_DEV, 1, D), F32),
            pltpu.VMEM((N_DEV, 1, MOD_COLS), F32),
            pltpu.VMEM((N_DEV, 1, MOD_COLS), F32),
            pltpu.SemaphoreType.DMA((2, N_DEV)),
            pltpu.SemaphoreType.DMA((2, N_DEV)),
        ],
        compiler_params=pltpu.CompilerParams(vmem_limit_bytes=VMEM_LIMIT_V7X),
    )(c_row, w_ada, b_ada3)


def _adaln_bwd(dmod3, scx, w_ada, m_ada, v_ada):
    def body(dm_ref, scx_ref, w_ref, m_ref, v_ref, g_ref, d_ref, nm_ref, nv_ref, recv, dm2d, send_sems, recv_sems):
        me = _my_index()
        copies = [
            pltpu.make_async_remote_copy(dm_ref.at[me ^ k], recv.at[k], send_sems.at[k], recv_sems.at[k],
                                         device_id=_peer(k), device_id_type=MESH)
            for k in range(1, N_DEV)
        ]
        for cp in copies:
            cp.start()
        dm2d[...] = jnp.zeros(dm2d.shape, F32)
        dm2d[0:1, :] = dm_ref[me]
        for cp in copies:
            cp.wait()
        for k in range(1, N_DEV):
            dm2d[k:k + 1, :] = recv[k]
        g = _mm_tn(_bf(scx_ref[...]), _bf(dm2d[...]))
        g_ref[...] = g
        delta, m, v = _adamw(w_ref[...], g, m_ref[...], v_ref[...])
        d_ref[...] = delta
        nm_ref[...] = m
        nv_ref[...] = v

    shard = jax.ShapeDtypeStruct((D, MOD_COLS), F32)
    return pl.pallas_call(
        body,
        name="adaln_bwd",
        out_shape=(shard,) * 4,
        in_specs=[VMEM_SPEC] * 5,
        out_specs=(VMEM_SPEC,) * 4,
        scratch_shapes=[
            pltpu.VMEM((N_DEV, 1, MOD_COLS), F32),
            pltpu.VMEM((2 * N_DEV, MOD_COLS), F32),
            pltpu.SemaphoreType.DMA((N_DEV,)),
            pltpu.SemaphoreType.DMA((N_DEV,)),
        ],
        compiler_params=pltpu.CompilerParams(vmem_limit_bytes=VMEM_LIMIT_V7X),
    )(dmod3, scx, w_ada, m_ada, v_ada)


class _GatherSteps:
    def __init__(self, ins, outs, stages, send_sems, recv_sems, local_sems):
        self.ins, self.outs, self.stages = ins, outs, stages
        self.send_sems, self.recv_sems, self.local_sems = send_sems, recv_sems, local_sems
        x, y, c = _coords()
        self.c = c
        self.me, self.sibling = (x, y, c), (x, y, 1 - c)
        self.chips = [(1 - x, y), (x, 1 - y), (1 - x, 1 - y)]

    def _copy(self, a, k, block, to, from_stage=False):
        dst = self.outs[a].at[4 * block[0] + 2 * block[1] + block[2]]
        return pltpu.make_async_remote_copy(self.stages[a] if from_stage else dst, dst, self.send_sems.at[a, k],
                                            self.recv_sems.at[a, k], device_id=to, device_id_type=MESH)

    def _local(self, a):
        me = self.me
        return pltpu.make_async_copy(self.stages[a], self.outs[a].at[4 * me[0] + 2 * me[1] + me[2]],
                                     self.local_sems.at[a])

    def _first(self, a):
        cps = [self._copy(a, 0, self.me, self.sibling, from_stage=True)]
        return cps + [self._copy(a, 1 + j, self.me, (*chip, self.c), from_stage=True)
                      for j, chip in enumerate(self.chips)]

    def _passed(self, a, j):
        return self._copy(a, 4 + j, (*self.chips[j], self.c), self.sibling)

    def start(self):
        for a in range(len(self.ins)):
            self.stages[a][...] = self.ins[a][...].astype(self.stages[a].dtype)
            self._local(a).start()
            for cp in self._first(a):
                cp.start()

    def forward(self):
        for a in range(len(self.ins)):
            for j, chip in enumerate(self.chips):
                self._copy(a, 1 + j, (*chip, self.c), self.me).wait_recv()
                self._passed(a, j).start()

    def finish(self):
        for a in range(len(self.ins)):
            self._copy(a, 0, self.sibling, self.me).wait_recv()
            for j, chip in enumerate(self.chips):
                self._copy(a, 4 + j, (*chip, 1 - self.c), self.me).wait_recv()
            for cp in self._first(a) + [self._passed(a, j) for j in range(3)]:
                cp.wait_send()
            self._local(a).wait()


def _gather_scratch(shards, out_dtypes):
    n = len(shards)
    return ([pltpu.VMEM(s.shape, dt) for s, dt in zip(shards, out_dtypes)]
            + [pltpu.SemaphoreType.DMA((n, 7)), pltpu.SemaphoreType.DMA((n, 7)), pltpu.SemaphoreType.DMA((n,))])


def _gather_out_shapes(shards, out_dtypes):
    return tuple(jax.ShapeDtypeStruct((N_DEV, *s.shape), dt) for s, dt in zip(shards, out_dtypes))


def _allgather_shards(shards, out_dtypes):
    n = len(shards)

    def body(*refs):
        steps = _GatherSteps(refs[:n], refs[n:2 * n], refs[2 * n:3 * n], *refs[3 * n:])
        steps.start()
        steps.forward()
        steps.finish()

    return pl.pallas_call(
        body,
        name="weight_allgather",
        out_shape=_gather_out_shapes(shards, out_dtypes),
        in_specs=[VMEM_SPEC] * n,
        out_specs=(ANY_SPEC,) * n,
        scratch_shapes=_gather_scratch(shards, out_dtypes),
        compiler_params=pltpu.CompilerParams(vmem_limit_bytes=VMEM_LIMIT_V7X),
    )(*shards)


class _ChipExchangeSteps:
    FLIPS = ((1, 0), (0, 1), (1, 1))

    def __init__(self, srcs, dsts, send_sems, recv_sems):
        self.srcs, self.dsts, self.send_sems, self.recv_sems = srcs, dsts, send_sems, recv_sems

    def _copies(self):
        x, y, c = _coords()
        out = []
        for a in range(len(self.srcs)):
            for j, (fx, fy) in enumerate(self.FLIPS):
                k = 3 * a + j
                out.append(pltpu.make_async_remote_copy(
                    self.srcs[a].at[x ^ fx, y ^ fy], self.dsts[a].at[j], self.send_sems.at[k], self.recv_sems.at[k],
                    device_id=(x ^ fx, y ^ fy, c), device_id_type=MESH))
        return out

    def start(self):
        for cp in self._copies():
            cp.start()

    def finish(self):
        for cp in self._copies():
            cp.wait()


def _mix_fwd(x, mod, g_pre, g_post, w_in_b, sgn, w_sp, b_sp_t, w_pool, p_scale, w_out_b, ts, shards, shard_dtypes):
    t_len = x.shape[0]
    nt, nb = t_len // ts, ts // HEAD
    ns = len(shards)

    def body(*refs):
        (x_ref, mod_ref, g1_ref, g2_ref, win_ref, sgn_ref, ws_ref, bst_ref, wp_ref, ps_ref, wout_ref) = refs[:11]
        x1_ref, proj_ref, mixed_ref = refs[11 + ns:14 + ns]
        pbuf, cat = refs[14 + 2 * ns:16 + 2 * ns]
        gather = _GatherSteps(refs[11:11 + ns], refs[14 + ns:14 + 2 * ns], refs[16 + 2 * ns:16 + 3 * ns],
                              *refs[16 + 3 * ns:])
        i = pl.program_id(0)

        @pl.when(i == 0)
        def _():
            pbuf[0:HEAD, :] = jnp.zeros((HEAD, B_WIDTH), F32)
            gather.start()

        @pl.when(i == nt // 2)
        def _():
            gather.forward()

        xv = x_ref[...]
        sh, sc, gm = mod_ref[0:1, :], mod_ref[1:2, :], mod_ref[2:3, :]
        h = (xv * _rstd(xv) * g1_ref[...]) * (1.0 + sc) + sh
        proj_ref[...] = _mm(_bf(h), win_ref[...])
        pbuf[HEAD:HEAD + ts, :] = proj_ref[:, 2 * A_WIDTH:]
        smask = _sgu_mask()
        for hd in range(N_HEAD):
            u = _gelu(proj_ref[:, hd * HEAD:(hd + 1) * HEAD])
            v = _gelu(proj_ref[:, A_WIDTH + hd * HEAD:A_WIDTH + (hd + 1) * HEAD])
            vn = _bf(v * _rstd(v) * sgn_ref[hd:hd + 1, :])
            wm = _bf(jnp.where(smask, ws_ref[hd], 0.0))
            bias = bst_ref[:, hd:hd + 1]
            for b in range(nb):
                rows = slice(b * HEAD, (b + 1) * HEAD)
                z = _mm(wm, vn[rows]) + bias
                cat[rows, hd * HEAD:(hd + 1) * HEAD] = _bf(u[rows] * z)
        for g, w in enumerate(WINDOWS):
            cols = slice(g * HEAD, (g + 1) * HEAD)
            band = _band(w, False)
            wpg = _bf(wp_ref[g])
            psg = ps_ref[:, cols]
            for b in range(nb):
                seg = pbuf[b * HEAD:(b + 2) * HEAD, cols]
                pooled = _mm_f32(band, seg) * _inv_count(i * ts + b * HEAD, w) - seg[HEAD:]
                cat[b * HEAD:(b + 1) * HEAD, A_WIDTH + g * HEAD:A_WIDTH + (g + 1) * HEAD] = _bf(_mm(_bf(pooled), wpg) * psg)
        pbuf[0:HEAD, :] = pbuf[ts:ts + HEAD, :]
        mixed = _mm(cat[...], wout_ref[...])
        mixed_ref[...] = mixed
        x1_ref[...] = xv + gm * (mixed * _rstd(mixed) * g2_ref[...])

        @pl.when(i == nt - 1)
        def _():
            gather.finish()

    tile = lambda wid: pl.BlockSpec((ts, wid), lambda i: (i, 0))
    outs = pl.pallas_call(
        body,
        name="mix_fwd",
        grid=(nt,),
        out_shape=(jax.ShapeDtypeStruct((t_len, D), F32), jax.ShapeDtypeStruct((t_len, IN_WIDTH), F32),
                   jax.ShapeDtypeStruct((t_len, D), F32)) + _gather_out_shapes(shards, shard_dtypes),
        in_specs=[tile(D)] + [VMEM_SPEC] * (10 + ns),
        out_specs=(tile(D), tile(IN_WIDTH), tile(D)) + (ANY_SPEC,) * ns,
        scratch_shapes=[pltpu.VMEM((HEAD + ts, B_WIDTH), F32), pltpu.VMEM((ts, D), BF16)]
        + _gather_scratch(shards, shard_dtypes),
        compiler_params=pltpu.CompilerParams(dimension_semantics=("arbitrary",), vmem_limit_bytes=VMEM_LIMIT_V7X),
    )(x, mod, g_pre, g_post, w_in_b, sgn, w_sp, b_sp_t, w_pool, p_scale, w_out_b, *shards)
    return outs[:3], outs[3:]


def _ffn_fwd(x1, target, mod, g_pre, g_post, w_up_b, conv_w8, conv_b8, w_down_b, ts):
    t_len = x1.shape[0]
    nt = t_len // ts

    def body(x1_ref, tgt_ref, mod_ref, g3_ref, g4_ref, wup_ref, cw_ref, cb_ref, wdown_ref,
             up_ref, f_ref, dx2_ref, loss_ref, ubuf, ucarry):
        i = pl.program_id(0)

        @pl.when(i == 0)
        def _():
            ucarry[...] = jnp.zeros(ucarry.shape, F32)
            loss_ref[...] = jnp.zeros(loss_ref.shape, F32)

        x1v = x1_ref[...]
        sh, sc, gf = mod_ref[3:4, :], mod_ref[4:5, :], mod_ref[5:6, :]
        h2 = _bf((x1v * _rstd(x1v) * g3_ref[...]) * (1.0 + sc) + sh)
        f = jnp.zeros((ts, D), F32)
        for j in range(N_DEV // 2):
            ys = []
            for jj in (j, j + N_DEV // 2):
                up = _mm(h2, wup_ref[jj])
                up_ref[jj] = up
                ubuf[0:HALO, :] = ucarry[jj]
                ubuf[HALO:HALO + ts, :] = up
                ucarry[jj] = up[ts - HALO:, :]
                cw = cw_ref[jj]
                ys.append(cb_ref[jj:jj + 1, :] + ubuf[HALO - 2:HALO - 2 + ts, :] * cw[0:1, :]
                          + ubuf[HALO - 1:HALO - 1 + ts, :] * cw[1:2, :] + up * cw[2:3, :])
            gate, val = ys
            act = gate * _sigmoid(gate) * val
            f = f + _mm(_bf(act), wdown_ref[j * FF_CHUNK:(j + 1) * FF_CHUNK, :])
        f_ref[...] = f
        x2 = x1v + gf * (f * _rstd(f) * g4_ref[...])
        err = x2 - tgt_ref[...]
        loss_ref[...] += _sum0(err * err)
        dx2_ref[...] = err * (1.0 / D)

    tile = pl.BlockSpec((ts, D), lambda i: (i, 0))
    return pl.pallas_call(
        body,
        name="ffn_fwd",
        grid=(nt,),
        out_shape=(jax.ShapeDtypeStruct((N_DEV, t_len, FF_CHUNK), F32), jax.ShapeDtypeStruct((t_len, D), F32),
                   jax.ShapeDtypeStruct((t_len, D), F32), jax.ShapeDtypeStruct((1, D), F32)),
        in_specs=[tile, tile] + [VMEM_SPEC] * 7,
        out_specs=(pl.BlockSpec((N_DEV, ts, FF_CHUNK), lambda i: (0, i, 0)), tile, tile,
                   pl.BlockSpec((1, D), lambda i: (0, 0))),
        scratch_shapes=[pltpu.VMEM((HALO + ts, FF_CHUNK), F32), pltpu.VMEM((N_DEV, HALO, FF_CHUNK), F32)],
        compiler_params=pltpu.CompilerParams(dimension_semantics=("arbitrary",), vmem_limit_bytes=VMEM_LIMIT_V7X),
    )(x1, target, mod, g_pre, g_post, w_up_b, conv_w8, conv_b8, w_down_b)


def _ffn_bwd(dx2, f, x1, up, mod, g_pre, g_post, w_up_b, conv_w8, conv_b8, w_down_b, ts):
    t_len = x1.shape[0]
    nt = t_len // ts
    half = N_DEV // 2

    def body(dx2_ref, f_ref, x1_ref, up_ref, halo_ref, mod_ref, g3_ref, g4_ref, wup_ref, cw_ref, cb_ref, wdown_ref,
             dx1_ref, dup_ref, act_ref, df_ref, h2_ref, dmod_ref, dg3_ref, dg4_ref, dcb_ref, dcw_ref,
             ubuf, dybuf, dycarry, dh2acc):
        i = pl.program_id(0)
        r = nt - 1 - i

        @pl.when(i == 0)
        def _():
            for ref in (dmod_ref, dg3_ref, dg4_ref, dcb_ref, dcw_ref, dycarry):
                ref[...] = jnp.zeros(ref.shape, F32)

        dx2v, fv, x1v = dx2_ref[...], f_ref[...], x1_ref[...]
        sh, sc, gf = mod_ref[3:4, :], mod_ref[4:5, :], mod_ref[5:6, :]
        g3, g4 = g3_ref[...], g4_ref[...]
        rstd4 = _rstd(fv)
        fh = fv * rstd4
        dmod_ref[2:3, :] += _sum0(dx2v * (fh * g4))
        dr = dx2v * gf
        dg4_ref[...] += _sum0(dr * fh)
        dfh = dr * g4
        dfb = _bf(rstd4 * (dfh - fh * _rowmean(dfh * fh)))
        df_ref[...] = dfb
        rstd3 = _rstd(x1v)
        xh = x1v * rstd3
        n3 = xh * g3
        h2_ref[...] = _bf(n3 * (1.0 + sc) + sh)
        dh2acc[...] = jnp.zeros((ts, D), F32)
        keep = jnp.where(r > 0, 1.0, 0.0).astype(F32)
        for j in range(half):
            ys = []
            for q, jj in enumerate((j, j + half)):
                ubuf[q, 0:HALO, :] = halo_ref[jj] * keep
                ubuf[q, HALO:HALO + ts, :] = up_ref[jj]
                cw = cw_ref[jj]
                ys.append(cb_ref[jj:jj + 1, :] + ubuf[q, HALO - 2:HALO - 2 + ts, :] * cw[0:1, :]
                          + ubuf[q, HALO - 1:HALO - 1 + ts, :] * cw[1:2, :] + ubuf[q, HALO:HALO + ts, :] * cw[2:3, :])
            gate, val = ys
            sg = _sigmoid(gate)
            silu = gate * sg
            act_ref[j] = _bf(silu * val)
            dact = _mm_nt(dfb, wdown_ref[j * FF_CHUNK:(j + 1) * FF_CHUNK, :])
            dys = (dact * val * (sg * (1.0 + gate * (1.0 - sg))), dact * silu)
            for q, jj in enumerate((j, j + half)):
                dy = dys[q]
                cw = cw_ref[jj]
                dcb_ref[jj:jj + 1, :] += _sum0(dy)
                dybuf[0:ts, :] = dy
                dybuf[ts:ts + HALO, :] = dycarry[jj]
                dycarry[jj] = dy[0:HALO, :]
                dy1, dy2 = dybuf[1:1 + ts, :], dybuf[2:2 + ts, :]
                upc = ubuf[q, HALO:HALO + ts, :]
                dcw_ref[jj, 0:1, :] += _sum0(dy2 * upc)
                dcw_ref[jj, 1:2, :] += _sum0(dy1 * upc)
                dcw_ref[jj, 2:3, :] += _sum0(dy * upc)
                dup = _bf(dy * cw[2:3, :] + dy1 * cw[1:2, :] + dy2 * cw[0:1, :])
                dup_ref[jj] = dup
                dh2acc[...] += _mm_nt(dup, wup_ref[jj])
        dh2 = dh2acc[...]
        dmod_ref[0:1, :] += _sum0(dh2)
        dmod_ref[1:2, :] += _sum0(dh2 * n3)
        dn3 = dh2 * (1.0 + sc)
        dg3_ref[...] += _sum0(dn3 * xh)
        dxh = dn3 * g3
        dx1_ref[...] = dx2v + rstd3 * (dxh - xh * _rowmean(dxh * xh))

    tile = pl.BlockSpec((ts, D), lambda i: (nt - 1 - i, 0))
    chunked = lambda n: pl.BlockSpec((n, ts, FF_CHUNK), lambda i: (0, nt - 1 - i, 0))
    halo = pl.BlockSpec((N_DEV, HALO, FF_CHUNK), lambda i: (0, jnp.maximum((nt - 1 - i) * (ts // HALO) - 1, 0), 0))
    const = lambda *shape: pl.BlockSpec(shape, lambda i: (0,) * len(shape))
    return pl.pallas_call(
        body,
        name="ffn_bwd",
        grid=(nt,),
        out_shape=(jax.ShapeDtypeStruct((t_len, D), F32), jax.ShapeDtypeStruct((N_DEV, t_len, FF_CHUNK), BF16),
                   jax.ShapeDtypeStruct((half, t_len, FF_CHUNK), BF16), jax.ShapeDtypeStruct((t_len, D), BF16),
                   jax.ShapeDtypeStruct((t_len, D), BF16), jax.ShapeDtypeStruct((3, D), F32),
                   jax.ShapeDtypeStruct((1, D), F32), jax.ShapeDtypeStruct((1, D), F32),
                   jax.ShapeDtypeStruct((N_DEV, FF_CHUNK), F32), jax.ShapeDtypeStruct((N_DEV, 3, FF_CHUNK), F32)),
        in_specs=[tile, tile, tile, chunked(N_DEV), halo] + [VMEM_SPEC] * 7,
        out_specs=(tile, chunked(N_DEV), chunked(half), tile, tile, const(3, D), const(1, D), const(1, D),
                   const(N_DEV, FF_CHUNK), const(N_DEV, 3, FF_CHUNK)),
        scratch_shapes=[pltpu.VMEM((2, HALO + ts, FF_CHUNK), F32), pltpu.VMEM((ts + HALO, FF_CHUNK), F32),
                        pltpu.VMEM((N_DEV, HALO, FF_CHUNK), F32), pltpu.VMEM((ts, D), F32)],
        compiler_params=pltpu.CompilerParams(dimension_semantics=("arbitrary",), vmem_limit_bytes=VMEM_LIMIT_V7X),
    )(dx2, f, x1, up, up, mod, g_pre, g_post, w_up_b, conv_w8, conv_b8, w_down_b)


def _wgrad_up(h2, dup, ts):
    t_len = h2.shape[0]
    nt, half = t_len // ts, N_DEV // 2

    def body(h2_ref, dup_ref, out_ref):
        @pl.when(pl.program_id(1) == 0)
        def _():
            out_ref[...] = jnp.zeros(out_ref.shape, F32)

        for q in range(half):
            out_ref[q] += _mm_tn(h2_ref[...], dup_ref[q])

    return pl.pallas_call(
        body,
        name="wgrad_up",
        grid=(2, nt),
        out_shape=jax.ShapeDtypeStruct((N_DEV, D, FF_CHUNK), F32),
        in_specs=[pl.BlockSpec((ts, D), lambda g, t: (t, 0)), pl.BlockSpec((half, ts, FF_CHUNK), lambda g, t: (g, t, 0))],
        out_specs=pl.BlockSpec((half, D, FF_CHUNK), lambda g, t: (g, 0, 0)),
        compiler_params=pltpu.CompilerParams(dimension_semantics=("arbitrary", "arbitrary"),
                                             vmem_limit_bytes=VMEM_LIMIT_V7X),
    )(h2, dup)


def _wgrad_down(act, df, ts):
    t_len = df.shape[0]
    nt, half = t_len // ts, N_DEV // 2

    def body(act_ref, df_ref, out_ref):
        @pl.when(pl.program_id(0) == 0)
        def _():
            out_ref[...] = jnp.zeros(out_ref.shape, F32)

        for q in range(half):
            out_ref[q] += _mm_tn(act_ref[q], df_ref[...])

    return pl.pallas_call(
        body,
        name="wgrad_down",
        grid=(nt,),
        out_shape=jax.ShapeDtypeStruct((half, FF_CHUNK, D), F32),
        in_specs=[pl.BlockSpec((half, ts, FF_CHUNK), lambda t: (0, t, 0)), pl.BlockSpec((ts, D), lambda t: (t, 0))],
        out_specs=pl.BlockSpec((half, FF_CHUNK, D), lambda t: (0, 0, 0)),
        compiler_params=pltpu.CompilerParams(dimension_semantics=("arbitrary",), vmem_limit_bytes=VMEM_LIMIT_V7X),
    )(act, df)


def _mix_bwd(dx1, x, proj, mixed, mod, g_pre, g_post, w_in_b, sgn, w_sp, b_sp_t, w_pool, p_scale, w_out_b, ts, rs_srcs):
    t_len = x.shape[0]
    nt, nb = t_len // ts, ts // HEAD
    nr = len(rs_srcs)

    def body(*refs):
        (dx1_ref, x_ref, proj_ref, halo_ref, mixed_ref, mod_ref, g1_ref, g2_ref, win_ref, sgn_ref, ws_ref,
         bst_ref, wp_ref, ps_ref, wout_ref) = refs[:15]
        (gx_ref, dwin_ref, dwout_ref, dmod_ref, dg1_ref, dg2_ref, dsgn_ref, dws_ref, dbst_ref, dwp_ref,
         dps_ref) = refs[15 + nr:26 + nr]
        pbuf, dwsbuf, cat, dproj, dcat = refs[26 + 2 * nr:31 + 2 * nr]
        exchange = _ChipExchangeSteps(refs[15:15 + nr], refs[26 + nr:26 + 2 * nr], *refs[31 + 2 * nr:])
        i = pl.program_id(0)
        r = nt - 1 - i

        @pl.when(i == 0)
        def _():
            exchange.start()
            for ref in (dwin_ref, dwout_ref, dmod_ref, dg1_ref, dg2_ref, dsgn_ref, dws_ref, dbst_ref, dwp_ref, dps_ref):
                ref[...] = jnp.zeros(ref.shape, F32)
            dwsbuf[ts:ts + HEAD, :] = jnp.zeros((HEAD, B_WIDTH), F32)

        xv, dx1v, mixed = x_ref[...], dx1_ref[...], mixed_ref[...]
        sh, sc, gm = mod_ref[0:1, :], mod_ref[1:2, :], mod_ref[2:3, :]
        g1, g2 = g1_ref[...], g2_ref[...]
        rstd2 = _rstd(mixed)
        mh = mixed * rstd2
        dmod_ref[2:3, :] += _sum0(dx1v * (mh * g2))
        dr = dx1v * gm
        dg2_ref[...] += _sum0(dr * mh)
        dmh = dr * g2
        dmb = _bf(rstd2 * (dmh - mh * _rowmean(dmh * mh)))
        dcat[...] = _mm_nt(dmb, wout_ref[...])
        smask = _sgu_mask()
        for hd in range(N_HEAD):
            ucols = slice(hd * HEAD, (hd + 1) * HEAD)
            vcols = slice(A_WIDTH + hd * HEAD, A_WIDTH + (hd + 1) * HEAD)
            u, du_dp = _gelu_and_grad(proj_ref[:, ucols])
            v, dv_dp = _gelu_and_grad(proj_ref[:, vcols])
            rs = _rstd(v)
            vhat = v * rs
            gn = sgn_ref[hd:hd + 1, :]
            vn = _bf(vhat * gn)
            wm = _bf(jnp.where(smask, ws_ref[hd], 0.0))
            bias = bst_ref[:, hd:hd + 1]
            dzsum = jnp.zeros((HEAD, HEAD), F32)
            dwm = jnp.zeros((HEAD, HEAD), F32)
            dvn_parts = []
            for b in range(nb):
                rows = slice(b * HEAD, (b + 1) * HEAD)
                z = _mm(wm, vn[rows]) + bias
                da = dcat[rows, ucols]
                cat[rows, ucols] = _bf(u[rows] * z)
                dz = da * u[rows]
                dzsum = dzsum + dz
                dzb = _bf(dz)
                dwm = dwm + _mm_nt(dzb, vn[rows])
                dvn_parts.append(_mm_tn(wm, dzb))
                dproj[rows, ucols] = (da * z) * du_dp[rows]
            dvn = jnp.concatenate(dvn_parts, axis=0)
            dsgn_ref[hd:hd + 1, :] += _sum0(dvn * vhat)
            dvh = dvn * gn
            dproj[:, vcols] = (rs * (dvh - vhat * _rowmean(dvh * vhat))) * dv_dp
            dws_ref[hd] += jnp.where(smask, dwm, 0.0)
            dbst_ref[:, hd:hd + 1] += jnp.sum(dzsum, axis=1, keepdims=True)
        keep = jnp.where(r > 0, 1.0, 0.0).astype(F32)
        pbuf[0:HEAD, :] = halo_ref[...] * keep
        pbuf[HEAD:HEAD + ts, :] = proj_ref[:, 2 * A_WIDTH:]
        for g, w in enumerate(WINDOWS):
            cols = slice(g * HEAD, (g + 1) * HEAD)
            ccols = slice(A_WIDTH + g * HEAD, A_WIDTH + (g + 1) * HEAD)
            pcols = slice(2 * A_WIDTH + g * HEAD, 2 * A_WIDTH + (g + 1) * HEAD)
            band, band_t = _band(w, False), _band(w, True)
            wpg = _bf(wp_ref[g])
            psg = ps_ref[:, cols]
            dps = jnp.zeros((1, HEAD), F32)
            dwp = jnp.zeros((HEAD, HEAD), F32)
            for b in range(nb):
                rows = slice(b * HEAD, (b + 1) * HEAD)
                seg = pbuf[b * HEAD:(b + 2) * HEAD, cols]
                inv = _inv_count(r * ts + b * HEAD, w)
                pb = _bf(_mm_f32(band, seg) * inv - seg[HEAD:])
                yb = _mm(pb, wpg)
                dob = dcat[rows, ccols]
                cat[rows, ccols] = _bf(yb * psg)
                dps = dps + _sum0(dob * yb)
                dyb = _bf(dob * psg)
                dwp = dwp + _mm_tn(pb, dyb)
                dpooled = _mm_nt(dyb, wpg)
                dwsbuf[rows, cols] = dpooled * inv
                dproj[rows, pcols] = -dpooled
            for b in range(nb):
                rows = slice(b * HEAD, (b + 1) * HEAD)
                dproj[rows, pcols] += _mm_f32(band_t, dwsbuf[b * HEAD:(b + 2) * HEAD, cols])
            dwp_ref[g] += dwp
            dps_ref[:, cols] += dps
        dwsbuf[ts:ts + HEAD, :] = dwsbuf[0:HEAD, :]
        dpb = _bf(dproj[...])
        rstd1 = _rstd(xv)
        xh = xv * rstd1
        n1 = xh * g1
        dwin_ref[...] += _mm_tn(_bf(n1 * (1.0 + sc) + sh), dpb)
        dwout_ref[...] += _mm_tn(cat[...], dmb)
        dh = _mm_nt(dpb, win_ref[...])
        dmod_ref[0:1, :] += _sum0(dh)
        dmod_ref[1:2, :] += _sum0(dh * n1)
        dn1 = dh * (1.0 + sc)
        dg1_ref[...] += _sum0(dn1 * xh)
        dxh = dn1 * g1
        gx_ref[...] = dx1v + rstd1 * (dxh - xh * _rowmean(dxh * xh))

        @pl.when(i == nt - 1)
        def _():
            exchange.finish()

    tile = lambda wid: pl.BlockSpec((ts, wid), lambda i: (nt - 1 - i, 0))
    halo = pl.BlockSpec((HEAD, B_WIDTH), lambda i: (jnp.maximum((nt - 1 - i) * nb - 1, 0), 2 * A_WIDTH // B_WIDTH))
    const = lambda *shape: pl.BlockSpec(shape, lambda i: (0,) * len(shape))
    outs = pl.pallas_call(
        body,
        name="mix_bwd",
        grid=(nt,),
        out_shape=(jax.ShapeDtypeStruct((t_len, D), F32), jax.ShapeDtypeStruct((D, IN_WIDTH), F32),
                   jax.ShapeDtypeStruct((D, D), F32), jax.ShapeDtypeStruct((3, D), F32),
                   jax.ShapeDtypeStruct((1, D), F32), jax.ShapeDtypeStruct((1, D), F32),
                   jax.ShapeDtypeStruct((N_HEAD, HEAD), F32), jax.ShapeDtypeStruct((N_HEAD, HEAD, HEAD), F32),
                   jax.ShapeDtypeStruct((HEAD, N_HEAD), F32), jax.ShapeDtypeStruct((N_HEAD, HEAD, HEAD), F32),
                   jax.ShapeDtypeStruct((1, B_WIDTH), F32))
        + tuple(jax.ShapeDtypeStruct((3, *s.shape[2:]), F32) for s in rs_srcs),
        in_specs=[tile(D), tile(D), tile(IN_WIDTH), halo, tile(D)] + [VMEM_SPEC] * 10 + [ANY_SPEC] * nr,
        out_specs=(tile(D), const(D, IN_WIDTH), const(D, D), const(3, D), const(1, D), const(1, D),
                   const(N_HEAD, HEAD), const(N_HEAD, HEAD, HEAD), const(HEAD, N_HEAD), const(N_HEAD, HEAD, HEAD),
                   const(1, B_WIDTH)) + (ANY_SPEC,) * nr,
        scratch_shapes=[pltpu.VMEM((HEAD + ts, B_WIDTH), F32), pltpu.VMEM((ts + HEAD, B_WIDTH), F32),
                        pltpu.VMEM((ts, D), BF16), pltpu.VMEM((ts, IN_WIDTH), F32), pltpu.VMEM((ts, D), F32),
                        pltpu.SemaphoreType.DMA((3 * nr,)), pltpu.SemaphoreType.DMA((3 * nr,))],
        compiler_params=pltpu.CompilerParams(dimension_semantics=("arbitrary",), vmem_limit_bytes=VMEM_LIMIT_V7X),
    )(dx1, x, proj, proj, mixed, mod, g_pre, g_post, w_in_b, sgn, w_sp, b_sp_t, w_pool, p_scale, w_out_b, *rs_srcs)
    return outs[:11], outs[11:]


def _exchange(name, srcs, out_shapes, plan):
    n = len(srcs)
    n_copies = len(plan(0, 0, 0))

    def body(*refs):
        ins, outs = refs[:n], refs[n:2 * n]
        send_sems, recv_sems = refs[2 * n:]
        copies = [
            pltpu.make_async_remote_copy(ins[a].at[src], outs[a].at[dst] if dst else outs[a], send_sems.at[k],
                                         recv_sems.at[k], device_id=peer, device_id_type=MESH)
            for k, (a, src, dst, peer) in enumerate(plan(*_coords()))
        ]
        for cp in copies:
            cp.start()
        for cp in copies:
            cp.wait()

    return pl.pallas_call(
        body,
        name=name,
        out_shape=tuple(jax.ShapeDtypeStruct(s, F32) for s in out_shapes),
        in_specs=[ANY_SPEC] * n,
        out_specs=(ANY_SPEC,) * n,
        scratch_shapes=[pltpu.SemaphoreType.DMA((n_copies,)), pltpu.SemaphoreType.DMA((n_copies,))],
    )(*srcs)


def _pair_add(name, coords, grid, specs_a, specs_b, out_specs, out_shapes, a_arrays, b_arrays):
    n = len(a_arrays)

    def body(co_ref, *refs):
        for k in range(n):
            refs[2 * n + k][...] = refs[k][...] + refs[n + k][...]

    return pl.pallas_call(
        body,
        name=name,
        grid_spec=pltpu.PrefetchScalarGridSpec(num_scalar_prefetch=1, grid=grid, in_specs=specs_a + specs_b,
                                               out_specs=out_specs),
        out_shape=tuple(jax.ShapeDtypeStruct(s, F32) for s in out_shapes),
        compiler_params=pltpu.CompilerParams(dimension_semantics=("arbitrary",) * len(grid),
                                             vmem_limit_bytes=VMEM_LIMIT_V7X),
    )(coords, *a_arrays, *b_arrays)


def _final_add_adamw(coords, s1, r, ws, ms, vs, n_split=4):
    n = len(s1)

    def body(co_ref, *refs):
        for k in range(n):
            s_ref, r_ref, w_ref, m_ref, v_ref = (refs[q * n + k] for q in range(5))
            g_ref, d_ref, nm_ref, nv_ref = (refs[(5 + q) * n + k] for q in range(4))
            g = ((s_ref[...] + r_ref[0]) + r_ref[1]) + r_ref[2]
            g_ref[...] = g
            delta, m, v = _adamw(w_ref[...], g, m_ref[...], v_ref[...])
            d_ref[...] = delta
            nm_ref[...] = m
            nv_ref[...] = v

    def shard_spec(a):
        rows, cols = a.shape
        return pl.BlockSpec((rows // n_split, cols), lambda i, co: (i, 0))

    def mine_spec(a):
        rows, cols = a.shape[2:]
        return pl.BlockSpec((None, None, rows // n_split, cols), lambda i, co: (co[0], co[1], i, 0))

    def recv_spec(a):
        rows, cols = a.shape[1:]
        return pl.BlockSpec((3, rows // n_split, cols), lambda i, co: (0, i, 0))

    in_specs = ([mine_spec(a) for a in s1] + [recv_spec(a) for a in r] + [shard_spec(a) for a in ws] * 3)
    out_specs = [shard_spec(a) for a in ws] * 4
    outs = pl.pallas_call(
        body,
        name="grad_final_adamw",
        grid_spec=pltpu.PrefetchScalarGridSpec(num_scalar_prefetch=1, grid=(n_split,), in_specs=in_specs,
                                               out_specs=out_specs),
        out_shape=tuple(jax.ShapeDtypeStruct(a.shape, F32) for a in ws) * 4,
        compiler_params=pltpu.CompilerParams(dimension_semantics=("arbitrary",), vmem_limit_bytes=VMEM_LIMIT_V7X),
    )(coords, *s1, *r, *ws, *ms, *vs)
    return [tuple(outs[q * n + k] for q in range(4)) for k in range(n)]


def _sibling_sum(tag, grads, coords, n_split=4):
    n = len(grads)
    shapes = [g.shape[1:] for g in grads]
    g5 = [g.reshape(2, 2, 2, *s) for g, s in zip(grads, shapes)]

    def plan(x, y, c):
        return [(a, (xs, ys, 1 - c), (xs, ys), (x, y, 1 - c)) for a in range(n) for xs in range(2) for ys in range(2)]

    r1 = _exchange("grad_swap_core_" + tag, g5, [(2, 2, *s) for s in shapes], plan)
    spec_g = [pl.BlockSpec((None, None, None, s[0] // n_split, s[1]), lambda i, j, k, co: (i, j, co[2], k, 0))
              for s in shapes]
    spec_r = [pl.BlockSpec((None, None, s[0] // n_split, s[1]), lambda i, j, k, co: (i, j, k, 0)) for s in shapes]
    return list(_pair_add("grad_add_core_" + tag, coords, (2, 2, n_split), spec_g, spec_r, spec_r,
                          [(2, 2, *s) for s in shapes], g5, list(r1)))


def _chip_exchange(tag, s1):
    n = len(s1)

    def body(*refs):
        steps = _ChipExchangeSteps(refs[:n], refs[n:2 * n], *refs[2 * n:])
        steps.start()
        steps.finish()

    return list(pl.pallas_call(
        body,
        name="grad_swap_chips_" + tag,
        out_shape=tuple(jax.ShapeDtypeStruct((3, *s.shape[2:]), F32) for s in s1),
        in_specs=[ANY_SPEC] * n,
        out_specs=(ANY_SPEC,) * n,
        scratch_shapes=[pltpu.SemaphoreType.DMA((3 * n,)), pltpu.SemaphoreType.DMA((3 * n,))],
    )(*s1))


def _small_allreduce_adamw(partials, ws, ms, vs, pick_mine):
    n = len(partials)

    def body(*refs):
        p_in = refs[:n]
        w_in, m_in, v_in = refs[n:2 * n], refs[2 * n:3 * n], refs[3 * n:4 * n]
        g_out, d_out, nm_out, nv_out = (refs[(4 + q) * n:(5 + q) * n] for q in range(4))
        acc = refs[8 * n:9 * n]
        rbuf = refs[9 * n:10 * n]
        send_sems, recv_sems = refs[10 * n:]
        x, y, c = _coords()
        me = 4 * x + 2 * y + c
        for a in range(n):
            acc[a][...] = p_in[a][...]
        for ph, peer in enumerate([(x, y, 1 - c), (1 - x, y, c), (x, 1 - y, c)]):
            copies = [
                pltpu.make_async_remote_copy(acc[a], rbuf[a].at[ph], send_sems.at[ph, a], recv_sems.at[ph, a],
                                             device_id=peer, device_id_type=MESH)
                for a in range(n)
            ]
            for cp in copies:
                cp.start()
            for cp in copies:
                cp.wait()
            for a in range(n):
                acc[a][...] = acc[a][...] + rbuf[a][ph]
        for a in range(n):
            g = acc[a][me] if pick_mine[a] else acc[a][...]
            g_out[a][...] = g
            delta, m, v = _adamw(w_in[a][...], g, m_in[a][...], v_in[a][...])
            d_out[a][...] = delta
            nm_out[a][...] = m
            nv_out[a][...] = v

    w_shapes = tuple(jax.ShapeDtypeStruct(w.shape, F32) for w in ws)
    outs = pl.pallas_call(
        body,
        name="small_allreduce_adamw",
        out_shape=w_shapes * 4,
        in_specs=[VMEM_SPEC] * (4 * n),
        out_specs=(VMEM_SPEC,) * (4 * n),
        scratch_shapes=[pltpu.VMEM(p.shape, F32) for p in partials]
        + [pltpu.VMEM((3, *p.shape), F32) for p in partials]
        + [pltpu.SemaphoreType.DMA((3, n)), pltpu.SemaphoreType.DMA((3, n))],
        compiler_params=pltpu.CompilerParams(vmem_limit_bytes=VMEM_LIMIT_V7X),
    )(*partials, *ws, *ms, *vs)
    return [tuple(outs[q * n + k] for q in range(4)) for k in range(n)]


def kernel(x, c, w_ada, b_ada, pre_mix_g, post_mix_g, w_in, sgu_norm_g, w_spatial, b_spatial, w_pool, pool_scale, w_out, pre_ffn_g, post_ffn_g, w_up, conv_w, conv_b, w_down, loss_target, m_w_ada, m_b_ada, m_pre_mix_g, m_post_mix_g, m_w_in, m_sgu_norm_g, m_w_spatial, m_b_spatial, m_w_pool, m_pool_scale, m_w_out, m_pre_ffn_g, m_post_ffn_g, m_w_up, m_conv_w, m_conv_b, m_w_down, v_w_ada, v_b_ada, v_pre_mix_g, v_post_mix_g, v_w_in, v_sgu_norm_g, v_w_spatial, v_b_spatial, v_w_pool, v_pool_scale, v_w_out, v_pre_ffn_g, v_post_ffn_g, v_w_up, v_conv_w, v_conv_b, v_w_down):
    t_len = x.shape[1]
    ts = min(256, t_len)
    ts_w = min(512, t_len)
    coords = jnp.stack([lax.axis_index("x"), lax.axis_index("y"), lax.axis_index("c")]).astype(jnp.int32)

    mod3, scx = _adaln_fwd(c, w_ada[0], b_ada.reshape(N_DEV, 1, MOD_COLS))
    mod = mod3.reshape(N_MOD, D)

    g_in, g_out = _allgather_shards([w_in[0], w_out[0]], [BF16, BF16])
    w_in_b = g_in.transpose(1, 0, 2).reshape(D, IN_WIDTH)
    w_out_b = g_out.reshape(D, D)
    conv_b8 = conv_b.reshape(N_DEV, FF_CHUNK)
    b_sp_t = b_spatial[0].T

    x2d, tgt = x[0], loss_target[0]
    (x1, proj, mixed), (g_up, g_down, g_cw) = _mix_fwd(
        x2d, mod, pre_mix_g, post_mix_g, w_in_b, sgu_norm_g[0], w_spatial[0], b_sp_t, w_pool[0], pool_scale, w_out_b,
        ts, [w_up[0], w_down[0], conv_w[0]], [BF16, BF16, F32])
    w_down_b = g_down.reshape(FF, D)
    up, f, dx2, loss_lanes = _ffn_fwd(x1, tgt, mod, pre_ffn_g, post_ffn_g, g_up, g_cw, conv_b8, w_down_b, ts)
    loss = lax.psum(0.5 * jnp.sum(loss_lanes) / D, ("x", "y", "c"))

    (dx1, dup, act, df, h2, dmod_f, d_pre_ffn, d_post_ffn, d_cb8, d_cw8) = _ffn_bwd(
        dx2, f, x1, up, mod, pre_ffn_g, post_ffn_g, g_up, g_cw, conv_b8, w_down_b, ts)
    gw_up = _wgrad_up(h2, dup, ts_w)
    gw_down = _wgrad_down(act, df, ts_w).reshape(N_DEV, FF // N_DEV, D)
    s1_ffn = _sibling_sum("ffn", [gw_up, gw_down], coords)
    ((grad_x, gw_in, gw_out, dmod_m, d_pre_mix, d_post_mix, d_sgn, d_wsp, d_bsp_t, d_wpool, d_ps), r_ffn) = _mix_bwd(
        dx1, x2d, proj, mixed, mod, pre_mix_g, post_mix_g, w_in_b, sgu_norm_g[0], w_spatial[0], b_sp_t,
        w_pool[0], pool_scale, w_out_b, ts, s1_ffn)
    gw_in = gw_in.reshape(D, N_DEV, IN_WIDTH // N_DEV).transpose(1, 0, 2)
    gw_out = gw_out.reshape(N_DEV, D // N_DEV, D)
    s1_mix = _sibling_sum("mix", [gw_in, gw_out], coords)
    r_mix = _chip_exchange("mix", s1_mix)

    big = _final_add_adamw(
        coords, s1_mix + s1_ffn, r_mix + list(r_ffn),
        [w_in[0], w_out[0], w_up[0], w_down[0]], [m_w_in[0], m_w_out[0], m_w_up[0], m_w_down[0]],
        [v_w_in[0], v_w_out[0], v_w_up[0], v_w_down[0]])
    r_in, r_out, r_up, r_down = [tuple(a[None] for a in four) for four in big]

    dmod = jnp.concatenate([dmod_m, dmod_f], axis=0)
    r_ada = tuple(a[None] for a in _adaln_bwd(dmod.reshape(N_DEV, 1, MOD_COLS), scx, w_ada[0], m_w_ada[0], v_w_ada[0]))

    names = ["b_ada", "pre_mix_g", "post_mix_g", "sgu_norm_g", "w_spatial", "b_spatial", "w_pool", "pool_scale",
             "pre_ffn_g", "post_ffn_g", "conv_w", "conv_b"]
    partials = [dmod.reshape(1, N_MOD * D), d_pre_mix, d_post_mix, d_sgn, d_wsp, d_bsp_t.T, d_wpool, d_ps,
                d_pre_ffn, d_post_ffn, d_cw8, d_cb8.reshape(1, 2 * FF)]
    small_w = [b_ada, pre_mix_g, post_mix_g, sgu_norm_g[0], w_spatial[0], b_spatial[0], w_pool[0], pool_scale,
               pre_ffn_g, post_ffn_g, conv_w[0], conv_b]
    small_m = [m_b_ada, m_pre_mix_g, m_post_mix_g, m_sgu_norm_g[0], m_w_spatial[0], m_b_spatial[0], m_w_pool[0],
               m_pool_scale, m_pre_ffn_g, m_post_ffn_g, m_conv_w[0], m_conv_b]
    small_v = [v_b_ada, v_pre_mix_g, v_post_mix_g, v_sgu_norm_g[0], v_w_spatial[0], v_b_spatial[0], v_w_pool[0],
               v_pool_scale, v_pre_ffn_g, v_post_ffn_g, v_conv_w[0], v_conv_b]
    small = _small_allreduce_adamw(partials, small_w, small_m, small_v, [nm == "conv_w" for nm in names])
    lead = {"sgu_norm_g", "w_spatial", "b_spatial", "w_pool", "conv_w"}
    res = {nm: tuple(a[None] if nm in lead else a for a in four) for nm, four in zip(names, small)}
    res.update(w_ada=r_ada, w_in=r_in, w_out=r_out, w_up=r_up, w_down=r_down)

    order = ["w_ada", "b_ada", "pre_mix_g", "post_mix_g", "w_in", "sgu_norm_g", "w_spatial", "b_spatial", "w_pool",
             "pool_scale", "w_out", "pre_ffn_g", "post_ffn_g", "w_up", "conv_w", "conv_b", "w_down"]
    return (loss, grad_x[None], *[res[nm][0] for nm in order], *[res[nm][1] for nm in order],
            *[res[nm][2] for nm in order], *[res[nm][3] for nm in order])
```

```python
import functools
import math

import jax
import jax.numpy as jnp
from jax import lax
from jax.experimental import pallas as pl
from jax.experimental.pallas import tpu as pltpu

F32 = jnp.float32
BF16 = jnp.bfloat16
MESH = pl.DeviceIdType.MESH

EPS = 1e-6
D = 1024
HEAD = 128
N_HEAD = 4
A_WIDTH = 512
B_WIDTH = 512
IN_WIDTH = 1536
WINDOWS = (2, 4, 8, 16)
CHUNK = 64
FF = 2816
N_DEV = 8
FF_CHUNK = 704
N_MOD = 6
MOD_COLS = 768

ADAM_LR = 0.001
ADAM_B1 = 0.9
ADAM_B2 = 0.999
ADAM_EPS = 1e-08
ADAM_WD = 0.01
ADAM_STEP = 10

VMEM_LIMIT_V7X = 56 * 1024 * 1024
HALO = 8

VMEM_SPEC = pl.BlockSpec(memory_space=pltpu.VMEM)
ANY_SPEC = pl.BlockSpec(memory_space=pl.ANY)


def _bf(x):
    return x.astype(BF16)


def _mm(a, b):
    return jnp.dot(a, b, preferred_element_type=F32)


def _mm_nt(a, b):
    return lax.dot_general(a, b, (((1,), (1,)), ((), ())), preferred_element_type=F32)


def _mm_tn(a, b):
    return lax.dot_general(a, b, (((0,), (0,)), ((), ())), preferred_element_type=F32)


def _mm_f32(a, b):
    return jnp.dot(a, b, preferred_element_type=F32, precision=lax.Precision.HIGHEST)


def _rstd(x):
    return lax.rsqrt(jnp.mean(x * x, axis=-1, keepdims=True) + EPS)


def _sum0(x):
    return jnp.sum(x, axis=0, keepdims=True)


def _rowmean(x):
    return jnp.mean(x, axis=-1, keepdims=True)


_GELU_K = math.sqrt(2.0 / math.pi)


def _gelu_and_grad(x):
    x2 = x * x
    th = jnp.tanh(_GELU_K * (x + 0.044715 * (x * x2)))
    cdf = 0.5 * (1.0 + th)
    grad = cdf + 0.5 * x * (1.0 - th * th) * (_GELU_K * (1.0 + 3.0 * 0.044715 * x2))
    return x * cdf, grad


def _gelu(x):
    return x * (0.5 * (1.0 + jnp.tanh(_GELU_K * (x + 0.044715 * (x * x * x)))))


def _sigmoid(x):
    return 0.5 * jnp.tanh(0.5 * x) + 0.5


def _sgu_mask():
    ri = lax.broadcasted_iota(jnp.int32, (HEAD, HEAD), 0)
    ci = lax.broadcasted_iota(jnp.int32, (HEAD, HEAD), 1)
    return (ci // CHUNK) <= (ri // CHUNK)


def _band(w, transposed):
    ii = lax.broadcasted_iota(jnp.int32, (HEAD, 2 * HEAD), 0)
    jj = lax.broadcasted_iota(jnp.int32, (HEAD, 2 * HEAD), 1)
    dist = (jj - ii) if transposed else (ii + HEAD - jj)
    return jnp.where((dist >= 0) & (dist < w), 1.0, 0.0).astype(F32)


def _inv_count(row0, w):
    t = row0 + lax.broadcasted_iota(jnp.int32, (HEAD, 1), 0)
    return 1.0 / jnp.minimum(t + 1, w).astype(F32)


def _shift_down(v, before, k):
    rows = lax.broadcasted_iota(jnp.int32, before.shape, 0)
    r = pltpu.roll(v, k, 0)
    top = jnp.where(rows < k, pltpu.roll(before, k, 0), r[0:HALO])
    return jnp.concatenate([top, r[HALO:]], axis=0)


def _shift_up(v, after, k):
    n = v.shape[0]
    rows = lax.broadcasted_iota(jnp.int32, after.shape, 0)
    r = pltpu.roll(v, n - k, 0)
    bottom = jnp.where(rows >= HALO - k, pltpu.roll(after, HALO - k, 0), r[n - HALO:])
    return jnp.concatenate([r[:n - HALO], bottom], axis=0)


def _adamw(w, g, m, v):
    m = ADAM_B1 * m + (1.0 - ADAM_B1) * g
    v = ADAM_B2 * v + (1.0 - ADAM_B2) * (g * g)
    m_hat = m / (1.0 - ADAM_B1 ** ADAM_STEP)
    v_hat = v / (1.0 - ADAM_B2 ** ADAM_STEP)
    delta = -ADAM_LR * (m_hat / (jnp.sqrt(v_hat) + ADAM_EPS) + ADAM_WD * w)
    return delta, m, v


def _coords():
    return lax.axis_index("x"), lax.axis_index("y"), lax.axis_index("c")


def _peer(k):
    x, y, c = _coords()
    return (x ^ ((k >> 2) & 1), y ^ ((k >> 1) & 1), c ^ (k & 1))


def _my_index():
    x, y, c = _coords()
    return 4 * x + 2 * y + c


def _adaln_fwd(c_row, w_ada, b_ada3):
    def body(c_ref, w_ref, b_ref, mod_ref, scx_ref, scbuf, stage, recv, send_sems, recv_sems):
        me = _my_index()
        cv = c_ref[...]
        scbuf[0] = cv * _sigmoid(cv)
        first = [
            pltpu.make_async_remote_copy(scbuf.at[0], scbuf.at[k], send_sems.at[0, k], recv_sems.at[0, k],
                                         device_id=_peer(k), device_id_type=MESH)
            for k in range(1, N_DEV)
        ]
        for cp in first:
            cp.start()
        for cp in first:
            cp.wait()
        scx_ref[...] = jnp.zeros(scx_ref.shape, F32)
        for k in range(N_DEV):
            scx_ref[k:k + 1, :] = scbuf[k]
        prod = _mm(_bf(scx_ref[...]), _bf(w_ref[...]))
        for k in range(N_DEV):
            stage[k] = prod[k:k + 1, :] + b_ref[me]
        second = [
            pltpu.make_async_remote_copy(stage.at[k], recv.at[k], send_sems.at[1, k], recv_sems.at[1, k],
                                         device_id=_peer(k), device_id_type=MESH)
            for k in range(1, N_DEV)
        ]
        for cp in second:
            cp.start()
        mod_ref[me] = stage[0]
        for cp in second:
            cp.wait()
        for k in range(1, N_DEV):
            mod_ref[me ^ k] = recv[k]

    return pl.pallas_call(
        body,
        name="adaln_fwd",
        out_shape=(jax.ShapeDtypeStruct((N_DEV, 1, MOD_COLS), F32), jax.ShapeDtypeStruct((2 * N_DEV, D), F32)),
        in_specs=[VMEM_SPEC] * 3,
        out_specs=(VMEM_SPEC, VMEM_SPEC),
        scratch_shapes=[
            pltpu.VMEM((N_DEV, 1, D), F32),
            pltpu.VMEM((N_DEV, 1, MOD_COLS), F32),
            pltpu.VMEM((N_DEV, 1, MOD_COLS), F32),
            pltpu.SemaphoreType.DMA((2, N_DEV)),
            pltpu.SemaphoreType.DMA((2, N_DEV)),
        ],
        compiler_params=pltpu.CompilerParams(vmem_limit_bytes=VMEM_LIMIT_V7X),
    )(c_row, w_ada, b_ada3)


def _adaln_bwd(dmod3, scx, w_ada, m_ada, v_ada):
    def body(dm_ref, scx_ref, w_ref, m_ref, v_ref, g_ref, d_ref, nm_ref, nv_ref, recv, dm2d, send_sems, recv_sems):
        me = _my_index()
        copies = [
            pltpu.make_async_remote_copy(dm_ref.at[me ^ k], recv.at[k], send_sems.at[k], recv_sems.at[k],
                                         device_id=_peer(k), device_id_type=MESH)
            for k in range(1, N_DEV)
        ]
        for cp in copies:
            cp.start()
        dm2d[...] = jnp.zeros(dm2d.shape, F32)
        dm2d[0:1, :] = dm_ref[me]
        for cp in copies:
            cp.wait()
        for k in range(1, N_DEV):
            dm2d[k:k + 1, :] = recv[k]
        g = _mm_tn(_bf(scx_ref[...]), _bf(dm2d[...]))
        g_ref[...] = g
        delta, m, v = _adamw(w_ref[...], g, m_ref[...], v_ref[...])
        d_ref[...] = delta
        nm_ref[...] = m
        nv_ref[...] = v

    shard = jax.ShapeDtypeStruct((D, MOD_COLS), F32)
    return pl.pallas_call(
        body,
        name="adaln_bwd",
        out_shape=(shard,) * 4,
        in_specs=[VMEM_SPEC] * 5,
        out_specs=(VMEM_SPEC,) * 4,
        scratch_shapes=[
            pltpu.VMEM((N_DEV, 1, MOD_COLS), F32),
            pltpu.VMEM((2 * N_DEV, MOD_COLS), F32),
            pltpu.SemaphoreType.DMA((N_DEV,)),
            pltpu.SemaphoreType.DMA((N_DEV,)),
        ],
        compiler_params=pltpu.CompilerParams(vmem_limit_bytes=VMEM_LIMIT_V7X),
    )(dmod3, scx, w_ada, m_ada, v_ada)


class _GatherSteps:
    def __init__(self, ins, outs, stages, send_sems, recv_sems, local_sems):
        self.ins, self.outs, self.stages = ins, outs, stages
        self.send_sems, self.recv_sems, self.local_sems = send_sems, recv_sems, local_sems
        x, y, c = _coords()
        self.c = c
        self.me, self.sibling = (x, y, c), (x, y, 1 - c)
        self.chips = [(1 - x, y), (x, 1 - y), (1 - x, 1 - y)]

    def _copy(self, a, k, block, to, from_stage=False):
        dst = self.outs[a].at[4 * block[0] + 2 * block[1] + block[2]]
        return pltpu.make_async_remote_copy(self.stages[a] if from_stage else dst, dst, self.send_sems.at[a, k],
                                            self.recv_sems.at[a, k], device_id=to, device_id_type=MESH)

    def _local(self, a):
        me = self.me
        return pltpu.make_async_copy(self.stages[a], self.outs[a].at[4 * me[0] + 2 * me[1] + me[2]],
                                     self.local_sems.at[a])

    def _first(self, a):
        cps = [self._copy(a, 0, self.me, self.sibling, from_stage=True)]
        return cps + [self._copy(a, 1 + j, self.me, (*chip, self.c), from_stage=True)
                      for j, chip in enumerate(self.chips)]

    def _passed(self, a, j):
        return self._copy(a, 4 + j, (*self.chips[j], self.c), self.sibling)

    def start(self):
        for a in range(len(self.ins)):
            self.stages[a][...] = self.ins[a][...].astype(self.stages[a].dtype)
            self._local(a).start()
            for cp in self._first(a):
                cp.start()

    def forward(self):
        for a in range(len(self.ins)):
            for j, chip in enumerate(self.chips):
                self._copy(a, 1 + j, (*chip, self.c), self.me).wait_recv()
                self._passed(a, j).start()

    def finish(self):
        for a in range(len(self.ins)):
            self._copy(a, 0, self.sibling, self.me).wait_recv()
            for j, chip in enumerate(self.chips):
                self._copy(a, 4 + j, (*chip, 1 - self.c), self.me).wait_recv()
            for cp in self._first(a) + [self._passed(a, j) for j in range(3)]:
                cp.wait_send()
            self._local(a).wait()


def _gather_scratch(shards, out_dtypes):
    n = len(shards)
    return ([pltpu.VMEM(s.shape, dt) for s, dt in zip(shards, out_dtypes)]
            + [pltpu.SemaphoreType.DMA((n, 7)), pltpu.SemaphoreType.DMA((n, 7)), pltpu.SemaphoreType.DMA((n,))])


def _gather_out_shapes(shards, out_dtypes):
    return tuple(jax.ShapeDtypeStruct((N_DEV, *s.shape), dt) for s, dt in zip(shards, out_dtypes))


def _allgather_shards(shards, out_dtypes):
    n = len(shards)

    def body(*refs):
        steps = _GatherSteps(refs[:n], refs[n:2 * n], refs[2 * n:3 * n], *refs[3 * n:])
        steps.start()
        steps.forward()
        steps.finish()

    return pl.pallas_call(
        body,
        name="weight_allgather",
        out_shape=_gather_out_shapes(shards, out_dtypes),
        in_specs=[VMEM_SPEC] * n,
        out_specs=(ANY_SPEC,) * n,
        scratch_shapes=_gather_scratch(shards, out_dtypes),
        compiler_params=pltpu.CompilerParams(vmem_limit_bytes=VMEM_LIMIT_V7X),
    )(*shards)


class _ChipExchangeSteps:
    FLIPS = ((1, 0), (0, 1), (1, 1))

    def __init__(self, srcs, dsts, send_sems, recv_sems):
        self.srcs, self.dsts, self.send_sems, self.recv_sems = srcs, dsts, send_sems, recv_sems

    def _copies(self):
        x, y, c = _coords()
        out = []
        for a in range(len(self.srcs)):
            for j, (fx, fy) in enumerate(self.FLIPS):
                k = 3 * a + j
                out.append(pltpu.make_async_remote_copy(
                    self.srcs[a].at[x ^ fx, y ^ fy], self.dsts[a].at[j], self.send_sems.at[k], self.recv_sems.at[k],
                    device_id=(x ^ fx, y ^ fy, c), device_id_type=MESH))
        return out

    def start(self):
        for cp in self._copies():
            cp.start()

    def finish(self):
        for cp in self._copies():
            cp.wait()


def _mix_fwd(x, mod, g_pre, g_post, w_in_b, sgn, w_sp, b_sp_t, w_pool, p_scale, w_out_b, ts, shards, shard_dtypes):
    t_len = x.shape[0]
    nt, nb = t_len // ts, ts // HEAD
    ns = len(shards)

    def body(*refs):
        (x_ref, mod_ref, g1_ref, g2_ref, win_ref, sgn_ref, ws_ref, bst_ref, wp_ref, ps_ref, wout_ref) = refs[:11]
        x1_ref, proj_ref, mixed_ref = refs[11 + ns:14 + ns]
        pbuf, cat = refs[14 + 2 * ns:16 + 2 * ns]
        gather = _GatherSteps(refs[11:11 + ns], refs[14 + ns:14 + 2 * ns], refs[16 + 2 * ns:16 + 3 * ns],
                              *refs[16 + 3 * ns:])
        i = pl.program_id(0)

        @pl.when(i == 0)
        def _():
            pbuf[0:HEAD, :] = jnp.zeros((HEAD, B_WIDTH), F32)
            gather.start()

        @pl.when(i == (3 * nt) // 4)
        def _():
            gather.forward()

        xv = x_ref[...]
        sh, sc, gm = mod_ref[0:1, :], mod_ref[1:2, :], mod_ref[2:3, :]
        h = (xv * _rstd(xv) * g1_ref[...]) * (1.0 + sc) + sh
        proj_ref[...] = _mm(_bf(h), win_ref[...])
        pbuf[HEAD:HEAD + ts, :] = proj_ref[:, 2 * A_WIDTH:]
        smask = _sgu_mask()
        for hd in range(N_HEAD):
            u = _gelu(proj_ref[:, hd * HEAD:(hd + 1) * HEAD])
            v = _gelu(proj_ref[:, A_WIDTH + hd * HEAD:A_WIDTH + (hd + 1) * HEAD])
            vn = _bf(v * _rstd(v) * sgn_ref[hd:hd + 1, :])
            wm = _bf(jnp.where(smask, ws_ref[hd], 0.0))
            bias = bst_ref[:, hd:hd + 1]
            for b in range(nb):
                rows = slice(b * HEAD, (b + 1) * HEAD)
                z = _mm(wm, vn[rows]) + bias
                cat[rows, hd * HEAD:(hd + 1) * HEAD] = _bf(u[rows] * z)
        for g, w in enumerate(WINDOWS):
            cols = slice(g * HEAD, (g + 1) * HEAD)
            band = _band(w, False)
            wpg = _bf(wp_ref[g])
            psg = ps_ref[:, cols]
            for b in range(nb):
                seg = pbuf[b * HEAD:(b + 2) * HEAD, cols]
                pooled = _mm_f32(band, seg) * _inv_count(i * ts + b * HEAD, w) - seg[HEAD:]
                cat[b * HEAD:(b + 1) * HEAD, A_WIDTH + g * HEAD:A_WIDTH + (g + 1) * HEAD] = _bf(_mm(_bf(pooled), wpg) * psg)
        pbuf[0:HEAD, :] = pbuf[ts:ts + HEAD, :]
        mixed = _mm(cat[...], wout_ref[...])
        mixed_ref[...] = mixed
        x1_ref[...] = xv + gm * (mixed * _rstd(mixed) * g2_ref[...])

        @pl.when(i == nt - 1)
        def _():
            gather.finish()

    tile = lambda wid: pl.BlockSpec((ts, wid), lambda i: (i, 0))
    outs = pl.pallas_call(
        body,
        name="mix_fwd",
        grid=(nt,),
        out_shape=(jax.ShapeDtypeStruct((t_len, D), F32), jax.ShapeDtypeStruct((t_len, IN_WIDTH), F32),
                   jax.ShapeDtypeStruct((t_len, D), F32)) + _gather_out_shapes(shards, shard_dtypes),
        in_specs=[tile(D)] + [VMEM_SPEC] * (10 + ns),
        out_specs=(tile(D), tile(IN_WIDTH), tile(D)) + (ANY_SPEC,) * ns,
        scratch_shapes=[pltpu.VMEM((HEAD + ts, B_WIDTH), F32), pltpu.VMEM((ts, D), BF16)]
        + _gather_scratch(shards, shard_dtypes),
        compiler_params=pltpu.CompilerParams(dimension_semantics=("arbitrary",), vmem_limit_bytes=VMEM_LIMIT_V7X),
    )(x, mod, g_pre, g_post, w_in_b, sgn, w_sp, b_sp_t, w_pool, p_scale, w_out_b, *shards)
    return outs[:3], outs[3:]


def _ffn_fwd(x1, target, mod, g_pre, g_post, w_up_b, conv_w8, conv_b8, w_down_b, ts):
    t_len = x1.shape[0]
    nt = t_len // ts

    def body(x1_ref, tgt_ref, mod_ref, g3_ref, g4_ref, wup_ref, cw_ref, cb_ref, wdown_ref,
             up_ref, f_ref, dx2_ref, loss_ref, ucarry):
        i = pl.program_id(0)

        @pl.when(i == 0)
        def _():
            ucarry[...] = jnp.zeros(ucarry.shape, F32)
            loss_ref[...] = jnp.zeros(loss_ref.shape, F32)

        x1v = x1_ref[...]
        sh, sc, gf = mod_ref[3:4, :], mod_ref[4:5, :], mod_ref[5:6, :]
        h2 = _bf((x1v * _rstd(x1v) * g3_ref[...]) * (1.0 + sc) + sh)
        half = N_DEV // 2

        def up_pair(j):
            return [_mm(h2, wup_ref[jj]) for jj in (j, j + half)]

        f = jnp.zeros((ts, D), F32)
        ups = up_pair(0)
        for j in range(half):
            nxt = up_pair(j + 1) if j + 1 < half else None
            ys = []
            for up, jj in zip(ups, (j, j + half)):
                up_ref[jj] = up
                before = ucarry[jj]
                ucarry[jj] = up[ts - HALO:, :]
                cw = cw_ref[jj]
                ys.append(cb_ref[jj:jj + 1, :] + _shift_down(up, before, 2) * cw[0:1, :]
                          + _shift_down(up, before, 1) * cw[1:2, :] + up * cw[2:3, :])
            gate, val = ys
            act = gate * _sigmoid(gate) * val
            f = f + _mm(_bf(act), wdown_ref[j * FF_CHUNK:(j + 1) * FF_CHUNK, :])
            ups = nxt
        f_ref[...] = f
        x2 = x1v + gf * (f * _rstd(f) * g4_ref[...])
        err = x2 - tgt_ref[...]
        loss_ref[...] += _sum0(err * err)
        dx2_ref[...] = err * (1.0 / D)

    tile = pl.BlockSpec((ts, D), lambda i: (i, 0))
    return pl.pallas_call(
        body,
        name="ffn_fwd",
        grid=(nt,),
        out_shape=(jax.ShapeDtypeStruct((N_DEV, t_len, FF_CHUNK), F32), jax.ShapeDtypeStruct((t_len, D), F32),
                   jax.ShapeDtypeStruct((t_len, D), F32), jax.ShapeDtypeStruct((1, D), F32)),
        in_specs=[tile, tile] + [VMEM_SPEC] * 7,
        out_specs=(pl.BlockSpec((N_DEV, ts, FF_CHUNK), lambda i: (0, i, 0)), tile, tile,
                   pl.BlockSpec((1, D), lambda i: (0, 0))),
        scratch_shapes=[pltpu.VMEM((N_DEV, HALO, FF_CHUNK), F32)],
        compiler_params=pltpu.CompilerParams(dimension_semantics=("arbitrary",), vmem_limit_bytes=VMEM_LIMIT_V7X),
    )(x1, target, mod, g_pre, g_post, w_up_b, conv_w8, conv_b8, w_down_b)


def _ffn_bwd(dx2, f, x1, up, mod, g_pre, g_post, w_up_b, conv_w8, conv_b8, w_down_b, ts):
    t_len = x1.shape[0]
    nt = t_len // ts
    half = N_DEV // 2

    def body(dx2_ref, f_ref, x1_ref, up_ref, halo_ref, mod_ref, g3_ref, g4_ref, wup_ref, cw_ref, cb_ref, wdown_ref,
             dx1_ref, dup_ref, act_ref, df_ref, h2_ref, dmod_ref, dg3_ref, dg4_ref, dcb_ref, dcw_ref,
             dycarry, dh2acc):
        i = pl.program_id(0)
        r = nt - 1 - i

        @pl.when(i == 0)
        def _():
            for ref in (dmod_ref, dg3_ref, dg4_ref, dcb_ref, dcw_ref, dycarry):
                ref[...] = jnp.zeros(ref.shape, F32)

        dx2v, fv, x1v = dx2_ref[...], f_ref[...], x1_ref[...]
        sh, sc, gf = mod_ref[3:4, :], mod_ref[4:5, :], mod_ref[5:6, :]
        g3, g4 = g3_ref[...], g4_ref[...]
        rstd4 = _rstd(fv)
        fh = fv * rstd4
        dmod_ref[2:3, :] += _sum0(dx2v * (fh * g4))
        dr = dx2v * gf
        dg4_ref[...] += _sum0(dr * fh)
        dfh = dr * g4
        dfb = _bf(rstd4 * (dfh - fh * _rowmean(dfh * fh)))
        df_ref[...] = dfb
        rstd3 = _rstd(x1v)
        xh = x1v * rstd3
        n3 = xh * g3
        h2_ref[...] = _bf(n3 * (1.0 + sc) + sh)
        dh2acc[...] = jnp.zeros((ts, D), F32)
        keep = jnp.where(r > 0, 1.0, 0.0).astype(F32)

        def dact_of(j):
            return _mm_nt(dfb, wdown_ref[j * FF_CHUNK:(j + 1) * FF_CHUNK, :])

        dact_next = dact_of(0)
        for j in range(half):
            dact = dact_next
            if j + 1 < half:
                dact_next = dact_of(j + 1)
            ys = []
            for jj in (j, j + half):
                before = halo_ref[jj] * keep
                upc = up_ref[jj]
                cw = cw_ref[jj]
                ys.append(cb_ref[jj:jj + 1, :] + _shift_down(upc, before, 2) * cw[0:1, :]
                          + _shift_down(upc, before, 1) * cw[1:2, :] + upc * cw[2:3, :])
            gate, val = ys
            sg = _sigmoid(gate)
            silu = gate * sg
            act_ref[j] = _bf(silu * val)
            dys = (dact * val * (sg * (1.0 + gate * (1.0 - sg))), dact * silu)
            for q, jj in enumerate((j, j + half)):
                dy = dys[q]
                cw = cw_ref[jj]
                dcb_ref[jj:jj + 1, :] += _sum0(dy)
                after = dycarry[jj]
                dycarry[jj] = dy[0:HALO, :]
                dy1, dy2 = _shift_up(dy, after, 1), _shift_up(dy, after, 2)
                upc = up_ref[jj]
                dcw_ref[jj, 0:1, :] += _sum0(dy2 * upc)
                dcw_ref[jj, 1:2, :] += _sum0(dy1 * upc)
                dcw_ref[jj, 2:3, :] += _sum0(dy * upc)
                dup = _bf(dy * cw[2:3, :] + dy1 * cw[1:2, :] + dy2 * cw[0:1, :])
                dup_ref[jj] = dup
                dh2acc[...] += _mm_nt(dup, wup_ref[jj])
        dh2 = dh2acc[...]
        dmod_ref[0:1, :] += _sum0(dh2)
        dmod_ref[1:2, :] += _sum0(dh2 * n3)
        dn3 = dh2 * (1.0 + sc)
        dg3_ref[...] += _sum0(dn3 * xh)
        dxh = dn3 * g3
        dx1_ref[...] = dx2v + rstd3 * (dxh - xh * _rowmean(dxh * xh))

    tile = pl.BlockSpec((ts, D), lambda i: (nt - 1 - i, 0))
    chunked = lambda n: pl.BlockSpec((n, ts, FF_CHUNK), lambda i: (0, nt - 1 - i, 0))
    halo = pl.BlockSpec((N_DEV, HALO, FF_CHUNK), lambda i: (0, jnp.maximum((nt - 1 - i) * (ts // HALO) - 1, 0), 0))
    const = lambda *shape: pl.BlockSpec(shape, lambda i: (0,) * len(shape))
    return pl.pallas_call(
        body,
        name="ffn_bwd",
        grid=(nt,),
        out_shape=(jax.ShapeDtypeStruct((t_len, D), F32), jax.ShapeDtypeStruct((N_DEV, t_len, FF_CHUNK), BF16),
                   jax.ShapeDtypeStruct((half, t_len, FF_CHUNK), BF16), jax.ShapeDtypeStruct((t_len, D), BF16),
                   jax.ShapeDtypeStruct((t_len, D), BF16), jax.ShapeDtypeStruct((3, D), F32),
                   jax.ShapeDtypeStruct((1, D), F32), jax.ShapeDtypeStruct((1, D), F32),
                   jax.ShapeDtypeStruct((N_DEV, FF_CHUNK), F32), jax.ShapeDtypeStruct((N_DEV, 3, FF_CHUNK), F32)),
        in_specs=[tile, tile, tile, chunked(N_DEV), halo] + [VMEM_SPEC] * 7,
        out_specs=(tile, chunked(N_DEV), chunked(half), tile, tile, const(3, D), const(1, D), const(1, D),
                   const(N_DEV, FF_CHUNK), const(N_DEV, 3, FF_CHUNK)),
        scratch_shapes=[pltpu.VMEM((N_DEV, HALO, FF_CHUNK), F32), pltpu.VMEM((ts, D), F32)],
        compiler_params=pltpu.CompilerParams(dimension_semantics=("arbitrary",), vmem_limit_bytes=VMEM_LIMIT_V7X),
    )(dx2, f, x1, up, up, mod, g_pre, g_post, w_up_b, conv_w8, conv_b8, w_down_b)


def _wgrad_up(h2, dup, ts):
    t_len = h2.shape[0]
    nt, half = t_len // ts, N_DEV // 2

    def body(h2_ref, dup_ref, out_ref):
        @pl.when(pl.program_id(1) == 0)
        def _():
            out_ref[...] = jnp.zeros(out_ref.shape, F32)

        for q in range(half):
            out_ref[q] += _mm_tn(h2_ref[...], dup_ref[q])

    return pl.pallas_call(
        body,
        name="wgrad_up",
        grid=(2, nt),
        out_shape=jax.ShapeDtypeStruct((N_DEV, D, FF_CHUNK), F32),
        in_specs=[pl.BlockSpec((ts, D), lambda g, t: (t, 0)), pl.BlockSpec((half, ts, FF_CHUNK), lambda g, t: (g, t, 0))],
        out_specs=pl.BlockSpec((half, D, FF_CHUNK), lambda g, t: (g, 0, 0)),
        compiler_params=pltpu.CompilerParams(dimension_semantics=("arbitrary", "arbitrary"),
                                             vmem_limit_bytes=VMEM_LIMIT_V7X),
    )(h2, dup)


def _wgrad_down(act, df, ts):
    t_len = df.shape[0]
    nt, half = t_len // ts, N_DEV // 2

    def body(act_ref, df_ref, out_ref):
        @pl.when(pl.program_id(0) == 0)
        def _():
            out_ref[...] = jnp.zeros(out_ref.shape, F32)

        for q in range(half):
            out_ref[q] += _mm_tn(act_ref[q], df_ref[...])

    return pl.pallas_call(
        body,
        name="wgrad_down",
        grid=(nt,),
        out_shape=jax.ShapeDtypeStruct((half, FF_CHUNK, D), F32),
        in_specs=[pl.BlockSpec((half, ts, FF_CHUNK), lambda t: (0, t, 0)), pl.BlockSpec((ts, D), lambda t: (t, 0))],
        out_specs=pl.BlockSpec((half, FF_CHUNK, D), lambda t: (0, 0, 0)),
        compiler_params=pltpu.CompilerParams(dimension_semantics=("arbitrary",), vmem_limit_bytes=VMEM_LIMIT_V7X),
    )(act, df)


def _mix_bwd(dx1, x, proj, mixed, mod, g_pre, g_post, w_in_b, sgn, w_sp, b_sp_t, w_pool, p_scale, w_out_b, ts, rs_srcs):
    t_len = x.shape[0]
    nt, nb = t_len // ts, ts // HEAD
    nr = len(rs_srcs)

    def body(*refs):
        (dx1_ref, x_ref, proj_ref, halo_ref, mixed_ref, mod_ref, g1_ref, g2_ref, win_ref, sgn_ref, ws_ref,
         bst_ref, wp_ref, ps_ref, wout_ref) = refs[:15]
        (gx_ref, dwin_ref, dwout_ref, dmod_ref, dg1_ref, dg2_ref, dsgn_ref, dws_ref, dbst_ref, dwp_ref,
         dps_ref) = refs[15 + nr:26 + nr]
        pbuf, dwsbuf, cat, dproj, dcat = refs[26 + 2 * nr:31 + 2 * nr]
        exchange = _ChipExchangeSteps(refs[15:15 + nr], refs[26 + nr:26 + 2 * nr], *refs[31 + 2 * nr:])
        i = pl.program_id(0)
        r = nt - 1 - i

        @pl.when(i == 0)
        def _():
            exchange.start()
            for ref in (dwin_ref, dwout_ref, dmod_ref, dg1_ref, dg2_ref, dsgn_ref, dws_ref, dbst_ref, dwp_ref, dps_ref):
                ref[...] = jnp.zeros(ref.shape, F32)
            dwsbuf[ts:ts + HEAD, :] = jnp.zeros((HEAD, B_WIDTH), F32)

        xv, dx1v, mixed = x_ref[...], dx1_ref[...], mixed_ref[...]
        sh, sc, gm = mod_ref[0:1, :], mod_ref[1:2, :], mod_ref[2:3, :]
        g1, g2 = g1_ref[...], g2_ref[...]
        rstd2 = _rstd(mixed)
        mh = mixed * rstd2
        dmod_ref[2:3, :] += _sum0(dx1v * (mh * g2))
        dr = dx1v * gm
        dg2_ref[...] += _sum0(dr * mh)
        dmh = dr * g2
        dmb = _bf(rstd2 * (dmh - mh * _rowmean(dmh * mh)))
        dcat[...] = _mm_nt(dmb, wout_ref[...])
        smask = _sgu_mask()
        for hd in range(N_HEAD):
            ucols = slice(hd * HEAD, (hd + 1) * HEAD)
            vcols = slice(A_WIDTH + hd * HEAD, A_WIDTH + (hd + 1) * HEAD)
            u, du_dp = _gelu_and_grad(proj_ref[:, ucols])
            v, dv_dp = _gelu_and_grad(proj_ref[:, vcols])
            rs = _rstd(v)
            vhat = v * rs
            gn = sgn_ref[hd:hd + 1, :]
            vn = _bf(vhat * gn)
            wm = _bf(jnp.where(smask, ws_ref[hd], 0.0))
            bias = bst_ref[:, hd:hd + 1]
            dzsum = jnp.zeros((HEAD, HEAD), F32)
            dwm = jnp.zeros((HEAD, HEAD), F32)
            dvn_parts = []
            for b in range(nb):
                rows = slice(b * HEAD, (b + 1) * HEAD)
                z = _mm(wm, vn[rows]) + bias
                da = dcat[rows, ucols]
                cat[rows, ucols] = _bf(u[rows] * z)
                dz = da * u[rows]
                dzsum = dzsum + dz
                dzb = _bf(dz)
                dwm = dwm + _mm_nt(dzb, vn[rows])
                dvn_parts.append(_mm_tn(wm, dzb))
                dproj[rows, ucols] = (da * z) * du_dp[rows]
            dvn = jnp.concatenate(dvn_parts, axis=0)
            dsgn_ref[hd:hd + 1, :] += _sum0(dvn * vhat)
            dvh = dvn * gn
            dproj[:, vcols] = (rs * (dvh - vhat * _rowmean(dvh * vhat))) * dv_dp
            dws_ref[hd] += jnp.where(smask, dwm, 0.0)
            dbst_ref[:, hd:hd + 1] += jnp.sum(dzsum, axis=1, keepdims=True)
        keep = jnp.where(r > 0, 1.0, 0.0).astype(F32)
        pbuf[0:HEAD, :] = halo_ref[...] * keep
        pbuf[HEAD:HEAD + ts, :] = proj_ref[:, 2 * A_WIDTH:]
        for g, w in enumerate(WINDOWS):
            cols = slice(g * HEAD, (g + 1) * HEAD)
            ccols = slice(A_WIDTH + g * HEAD, A_WIDTH + (g + 1) * HEAD)
            pcols = slice(2 * A_WIDTH + g * HEAD, 2 * A_WIDTH + (g + 1) * HEAD)
            band, band_t = _band(w, False), _band(w, True)
            wpg = _bf(wp_ref[g])
            psg = ps_ref[:, cols]
            dps = jnp.zeros((1, HEAD), F32)
            dwp = jnp.zeros((HEAD, HEAD), F32)
            for b in range(nb):
                rows = slice(b * HEAD, (b + 1) * HEAD)
                seg = pbuf[b * HEAD:(b + 2) * HEAD, cols]
                inv = _inv_count(r * ts + b * HEAD, w)
                pb = _bf(_mm_f32(band, seg) * inv - seg[HEAD:])
                yb = _mm(pb, wpg)
                dob = dcat[rows, ccols]
                cat[rows, ccols] = _bf(yb * psg)
                dps = dps + _sum0(dob * yb)
                dyb = _bf(dob * psg)
                dwp = dwp + _mm_tn(pb, dyb)
                dpooled = _mm_nt(dyb, wpg)
                dwsbuf[rows, cols] = dpooled * inv
                dproj[rows, pcols] = -dpooled
            for b in range(nb):
                rows = slice(b * HEAD, (b + 1) * HEAD)
                dproj[rows, pcols] += _mm_f32(band_t, dwsbuf[b * HEAD:(b + 2) * HEAD, cols])
            dwp_ref[g] += dwp
            dps_ref[:, cols] += dps
        dwsbuf[ts:ts + HEAD, :] = dwsbuf[0:HEAD, :]
        dpb = _bf(dproj[...])
        rstd1 = _rstd(xv)
        xh = xv * rstd1
        n1 = xh * g1
        dwin_ref[...] += _mm_tn(_bf(n1 * (1.0 + sc) + sh), dpb)
        dwout_ref[...] += _mm_tn(cat[...], dmb)
        dh = _mm_nt(dpb, win_ref[...])
        dmod_ref[0:1, :] += _sum0(dh)
        dmod_ref[1:2, :] += _sum0(dh * n1)
        dn1 = dh * (1.0 + sc)
        dg1_ref[...] += _sum0(dn1 * xh)
        dxh = dn1 * g1
        gx_ref[...] = dx1v + rstd1 * (dxh - xh * _rowmean(dxh * xh))

        @pl.when(i == nt - 1)
        def _():
            exchange.finish()

    tile = lambda wid: pl.BlockSpec((ts, wid), lambda i: (nt - 1 - i, 0))
    halo = pl.BlockSpec((HEAD, B_WIDTH), lambda i: (jnp.maximum((nt - 1 - i) * nb - 1, 0), 2 * A_WIDTH // B_WIDTH))
    const = lambda *shape: pl.BlockSpec(shape, lambda i: (0,) * len(shape))
    outs = pl.pallas_call(
        body,
        name="mix_bwd",
        grid=(nt,),
        out_shape=(jax.ShapeDtypeStruct((t_len, D), F32), jax.ShapeDtypeStruct((D, IN_WIDTH), F32),
                   jax.ShapeDtypeStruct((D, D), F32), jax.ShapeDtypeStruct((3, D), F32),
                   jax.ShapeDtypeStruct((1, D), F32), jax.ShapeDtypeStruct((1, D), F32),
                   jax.ShapeDtypeStruct((N_HEAD, HEAD), F32), jax.ShapeDtypeStruct((N_HEAD, HEAD, HEAD), F32),
                   jax.ShapeDtypeStruct((HEAD, N_HEAD), F32), jax.ShapeDtypeStruct((N_HEAD, HEAD, HEAD), F32),
                   jax.ShapeDtypeStruct((1, B_WIDTH), F32))
        + tuple(jax.ShapeDtypeStruct((3, *s.shape[2:]), F32) for s in rs_srcs),
        in_specs=[tile(D), tile(D), tile(IN_WIDTH), halo, tile(D)] + [VMEM_SPEC] * 10 + [ANY_SPEC] * nr,
        out_specs=(tile(D), const(D, IN_WIDTH), const(D, D), const(3, D), const(1, D), const(1, D),
                   const(N_HEAD, HEAD), const(N_HEAD, HEAD, HEAD), const(HEAD, N_HEAD), const(N_HEAD, HEAD, HEAD),
                   const(1, B_WIDTH)) + (ANY_SPEC,) * nr,
        scratch_shapes=[pltpu.VMEM((HEAD + ts, B_WIDTH), F32), pltpu.VMEM((ts + HEAD, B_WIDTH), F32),
                        pltpu.VMEM((ts, D), BF16), pltpu.VMEM((ts, IN_WIDTH), F32), pltpu.VMEM((ts, D), F32),
                        pltpu.SemaphoreType.DMA((3 * nr,)), pltpu.SemaphoreType.DMA((3 * nr,))],
        compiler_params=pltpu.CompilerParams(dimension_semantics=("arbitrary",), vmem_limit_bytes=VMEM_LIMIT_V7X),
    )(dx1, x, proj, proj, mixed, mod, g_pre, g_post, w_in_b, sgn, w_sp, b_sp_t, w_pool, p_scale, w_out_b, *rs_srcs)
    return outs[:11], outs[11:]


def _exchange(name, srcs, out_shapes, plan):
    n = len(srcs)
    n_copies = len(plan(0, 0, 0))

    def body(*refs):
        ins, outs = refs[:n], refs[n:2 * n]
        send_sems, recv_sems = refs[2 * n:]
        copies = [
            pltpu.make_async_remote_copy(ins[a].at[src], outs[a].at[dst] if dst else outs[a], send_sems.at[k],
                                         recv_sems.at[k], device_id=peer, device_id_type=MESH)
            for k, (a, src, dst, peer) in enumerate(plan(*_coords()))
        ]
        for cp in copies:
            cp.start()
        for cp in copies:
            cp.wait()

    return pl.pallas_call(
        body,
        name=name,
        out_shape=tuple(jax.ShapeDtypeStruct(s, F32) for s in out_shapes),
        in_specs=[ANY_SPEC] * n,
        out_specs=(ANY_SPEC,) * n,
        scratch_shapes=[pltpu.SemaphoreType.DMA((n_copies,)), pltpu.SemaphoreType.DMA((n_copies,))],
    )(*srcs)


def _pair_add(name, coords, grid, specs_a, specs_b, out_specs, out_shapes, a_arrays, b_arrays):
    n = len(a_arrays)

    def body(co_ref, *refs):
        for k in range(n):
            refs[2 * n + k][...] = refs[k][...] + refs[n + k][...]

    return pl.pallas_call(
        body,
        name=name,
        grid_spec=pltpu.PrefetchScalarGridSpec(num_scalar_prefetch=1, grid=grid, in_specs=specs_a + specs_b,
                                               out_specs=out_specs),
        out_shape=tuple(jax.ShapeDtypeStruct(s, F32) for s in out_shapes),
        compiler_params=pltpu.CompilerParams(dimension_semantics=("arbitrary",) * len(grid),
                                             vmem_limit_bytes=VMEM_LIMIT_V7X),
    )(coords, *a_arrays, *b_arrays)


def _final_add_adamw(coords, s1, r, ws, ms, vs, n_split=4):
    n = len(s1)

    def body(co_ref, *refs):
        for k in range(n):
            s_ref, r_ref, w_ref, m_ref, v_ref = (refs[q * n + k] for q in range(5))
            g_ref, d_ref, nm_ref, nv_ref = (refs[(5 + q) * n + k] for q in range(4))
            g = ((s_ref[...] + r_ref[0]) + r_ref[1]) + r_ref[2]
            g_ref[...] = g
            delta, m, v = _adamw(w_ref[...], g, m_ref[...], v_ref[...])
            d_ref[...] = delta
            nm_ref[...] = m
            nv_ref[...] = v

    def shard_spec(a):
        rows, cols = a.shape
        return pl.BlockSpec((rows // n_split, cols), lambda i, co: (i, 0))

    def mine_spec(a):
        rows, cols = a.shape[2:]
        return pl.BlockSpec((None, None, rows // n_split, cols), lambda i, co: (co[0], co[1], i, 0))

    def recv_spec(a):
        rows, cols = a.shape[1:]
        return pl.BlockSpec((3, rows // n_split, cols), lambda i, co: (0, i, 0))

    in_specs = ([mine_spec(a) for a in s1] + [recv_spec(a) for a in r] + [shard_spec(a) for a in ws] * 3)
    out_specs = [shard_spec(a) for a in ws] * 4
    outs = pl.pallas_call(
        body,
        name="grad_final_adamw",
        grid_spec=pltpu.PrefetchScalarGridSpec(num_scalar_prefetch=1, grid=(n_split,), in_specs=in_specs,
                                               out_specs=out_specs),
        out_shape=tuple(jax.ShapeDtypeStruct(a.shape, F32) for a in ws) * 4,
        compiler_params=pltpu.CompilerParams(dimension_semantics=("arbitrary",), vmem_limit_bytes=VMEM_LIMIT_V7X),
    )(coords, *s1, *r, *ws, *ms, *vs)
    return [tuple(outs[q * n + k] for q in range(4)) for k in range(n)]


def _sibling_sum(tag, grads, coords, n_split=4):
    n = len(grads)
    shapes = [g.shape[1:] for g in grads]
    g5 = [g.reshape(2, 2, 2, *s) for g, s in zip(grads, shapes)]

    def plan(x, y, c):
        return [(a, (xs, ys, 1 - c), (xs, ys), (x, y, 1 - c)) for a in range(n) for xs in range(2) for ys in range(2)]

    r1 = _exchange("grad_swap_core_" + tag, g5, [(2, 2, *s) for s in shapes], plan)
    spec_g = [pl.BlockSpec((None, None, None, s[0] // n_split, s[1]), lambda i, j, k, co: (i, j, co[2], k, 0))
              for s in shapes]
    spec_r = [pl.BlockSpec((None, None, s[0] // n_split, s[1]), lambda i, j, k, co: (i, j, k, 0)) for s in shapes]
    return list(_pair_add("grad_add_core_" + tag, coords, (2, 2, n_split), spec_g, spec_r, spec_r,
                          [(2, 2, *s) for s in shapes], g5, list(r1)))


def _chip_exchange(tag, s1):
    n = len(s1)

    def body(*refs):
        steps = _ChipExchangeSteps(refs[:n], refs[n:2 * n], *refs[2 * n:])
        steps.start()
        steps.finish()

    return list(pl.pallas_call(
        body,
        name="grad_swap_chips_" + tag,
        out_shape=tuple(jax.ShapeDtypeStruct((3, *s.shape[2:]), F32) for s in s1),
        in_specs=[ANY_SPEC] * n,
        out_specs=(ANY_SPEC,) * n,
        scratch_shapes=[pltpu.SemaphoreType.DMA((3 * n,)), pltpu.SemaphoreType.DMA((3 * n,))],
    )(*s1))


def _small_allreduce_adamw(partials, ws, ms, vs, pick_mine):
    n = len(partials)

    def body(*refs):
        p_in = refs[:n]
        w_in, m_in, v_in = refs[n:2 * n], refs[2 * n:3 * n], refs[3 * n:4 * n]
        g_out, d_out, nm_out, nv_out = (refs[(4 + q) * n:(5 + q) * n] for q in range(4))
        acc = refs[8 * n:9 * n]
        rbuf = refs[9 * n:10 * n]
        send_sems, recv_sems = refs[10 * n:]
        x, y, c = _coords()
        me = 4 * x + 2 * y + c
        for a in range(n):
            acc[a][...] = p_in[a][...]
        for ph, peer in enumerate([(x, y, 1 - c), (1 - x, y, c), (x, 1 - y, c)]):
            copies = [
                pltpu.make_async_remote_copy(acc[a], rbuf[a].at[ph], send_sems.at[ph, a], recv_sems.at[ph, a],
                                             device_id=peer, device_id_type=MESH)
                for a in range(n)
            ]
            for cp in copies:
                cp.start()
            for cp in copies:
                cp.wait()
            for a in range(n):
                acc[a][...] = acc[a][...] + rbuf[a][ph]
        for a in range(n):
            g = acc[a][me] if pick_mine[a] else acc[a][...]
            g_out[a][...] = g
            delta, m, v = _adamw(w_in[a][...], g, m_in[a][...], v_in[a][...])
            d_out[a][...] = delta
            nm_out[a][...] = m
            nv_out[a][...] = v

    w_shapes = tuple(jax.ShapeDtypeStruct(w.shape, F32) for w in ws)
    outs = pl.pallas_call(
        body,
        name="small_allreduce_adamw",
        out_shape=w_shapes * 4,
        in_specs=[VMEM_SPEC] * (4 * n),
        out_specs=(VMEM_SPEC,) * (4 * n),
        scratch_shapes=[pltpu.VMEM(p.shape, F32) for p in partials]
        + [pltpu.VMEM((3, *p.shape), F32) for p in partials]
        + [pltpu.SemaphoreType.DMA((3, n)), pltpu.SemaphoreType.DMA((3, n))],
        compiler_params=pltpu.CompilerParams(vmem_limit_bytes=VMEM_LIMIT_V7X),
    )(*partials, *ws, *ms, *vs)
    return [tuple(outs[q * n + k] for q in range(4)) for k in range(n)]


def kernel(x, c, w_ada, b_ada, pre_mix_g, post_mix_g, w_in, sgu_norm_g, w_spatial, b_spatial, w_pool, pool_scale, w_out, pre_ffn_g, post_ffn_g, w_up, conv_w, conv_b, w_down, loss_target, m_w_ada, m_b_ada, m_pre_mix_g, m_post_mix_g, m_w_in, m_sgu_norm_g, m_w_spatial, m_b_spatial, m_w_pool, m_pool_scale, m_w_out, m_pre_ffn_g, m_post_ffn_g, m_w_up, m_conv_w, m_conv_b, m_w_down, v_w_ada, v_b_ada, v_pre_mix_g, v_post_mix_g, v_w_in, v_sgu_norm_g, v_w_spatial, v_b_spatial, v_w_pool, v_pool_scale, v_w_out, v_pre_ffn_g, v_post_ffn_g, v_w_up, v_conv_w, v_conv_b, v_w_down):
    t_len = x.shape[1]
    ts = min(256, t_len)
    ts_w = min(512, t_len)
    coords = jnp.stack([lax.axis_index("x"), lax.axis_index("y"), lax.axis_index("c")]).astype(jnp.int32)

    mod3, scx = _adaln_fwd(c, w_ada[0], b_ada.reshape(N_DEV, 1, MOD_COLS))
    mod = mod3.reshape(N_MOD, D)

    g_in, g_out = _allgather_shards([w_in[0], w_out[0]], [BF16, BF16])
    w_in_b = g_in.transpose(1, 0, 2).reshape(D, IN_WIDTH)
    w_out_b = g_out.reshape(D, D)
    conv_b8 = conv_b.reshape(N_DEV, FF_CHUNK)
    b_sp_t = b_spatial[0].T

    x2d, tgt = x[0], loss_target[0]
    (x1, proj, mixed), (g_up, g_down, g_cw) = _mix_fwd(
        x2d, mod, pre_mix_g, post_mix_g, w_in_b, sgu_norm_g[0], w_spatial[0], b_sp_t, w_pool[0], pool_scale, w_out_b,
        ts, [w_up[0], w_down[0], conv_w[0]], [BF16, BF16, F32])
    w_down_b = g_down.reshape(FF, D)
    up, f, dx2, loss_lanes = _ffn_fwd(x1, tgt, mod, pre_ffn_g, post_ffn_g, g_up, g_cw, conv_b8, w_down_b, ts)
    loss = lax.psum(0.5 * jnp.sum(loss_lanes) / D, ("x", "y", "c"))

    (dx1, dup, act, df, h2, dmod_f, d_pre_ffn, d_post_ffn, d_cb8, d_cw8) = _ffn_bwd(
        dx2, f, x1, up, mod, pre_ffn_g, post_ffn_g, g_up, g_cw, conv_b8, w_down_b, ts)
    gw_up = _wgrad_up(h2, dup, ts_w)
    gw_down = _wgrad_down(act, df, ts_w).reshape(N_DEV, FF // N_DEV, D)
    s1_ffn = _sibling_sum("ffn", [gw_up, gw_down], coords)
    ((grad_x, gw_in, gw_out, dmod_m, d_pre_mix, d_post_mix, d_sgn, d_wsp, d_bsp_t, d_wpool, d_ps), r_ffn) = _mix_bwd(
        dx1, x2d, proj, mixed, mod, pre_mix_g, post_mix_g, w_in_b, sgu_norm_g[0], w_spatial[0], b_sp_t,
        w_pool[0], pool_scale, w_out_b, ts, s1_ffn)
    gw_in = gw_in.reshape(D, N_DEV, IN_WIDTH // N_DEV).transpose(1, 0, 2)
    gw_out = gw_out.reshape(N_DEV, D // N_DEV, D)
    s1_mix = _sibling_sum("mix", [gw_in, gw_out], coords)
    r_mix = _chip_exchange("mix", s1_mix)

    big = _final_add_adamw(
        coords, s1_mix + s1_ffn, r_mix + list(r_ffn),
        [w_in[0], w_out[0], w_up[0], w_down[0]], [m_w_in[0], m_w_out[0], m_w_up[0], m_w_down[0]],
        [v_w_in[0], v_w_out[0], v_w_up[0], v_w_down[0]])
    r_in, r_out, r_up, r_down = [tuple(a[None] for a in four) for four in big]

    dmod = jnp.concatenate([dmod_m, dmod_f], axis=0)
    r_ada = tuple(a[None] for a in _adaln_bwd(dmod.reshape(N_DEV, 1, MOD_COLS), scx, w_ada[0], m_w_ada[0], v_w_ada[0]))

    names = ["b_ada", "pre_mix_g", "post_mix_g", "sgu_norm_g", "w_spatial", "b_spatial", "w_pool", "pool_scale",
             "pre_ffn_g", "post_ffn_g", "conv_w", "conv_b"]
    partials = [dmod.reshape(1, N_MOD * D), d_pre_mix, d_post_mix, d_sgn, d_wsp, d_bsp_t.T, d_wpool, d_ps,
                d_pre_ffn, d_post_ffn, d_cw8, d_cb8.reshape(1, 2 * FF)]
    small_w = [b_ada, pre_mix_g, post_mix_g, sgu_norm_g[0], w_spatial[0], b_spatial[0], w_pool[0], pool_scale,
               pre_ffn_g, post_ffn_g, conv_w[0], conv_b]
    small_m = [m_b_ada, m_pre_mix_g, m_post_mix_g, m_sgu_norm_g[0], m_w_spatial[0], m_b_spatial[0], m_w_pool[0],
               m_pool_scale, m_pre_ffn_g, m_post_ffn_g, m_conv_w[0], m_conv_b]
    small_v = [v_b_ada, v_pre_mix_g, v_post_mix_g, v_sgu_norm_g[0], v_w_spatial[0], v_b_spatial[0], v_w_pool[0],
               v_pool_scale, v_pre_ffn_g, v_post_ffn_g, v_conv_w[0], v_conv_b]
    small = _small_allreduce_adamw(partials, small_w, small_m, small_v, [nm == "conv_w" for nm in names])
    lead = {"sgu_norm_g", "w_spatial", "b_spatial", "w_pool", "conv_w"}
    res = {nm: tuple(a[None] if nm in lead else a for a in four) for nm, four in zip(names, small)}
    res.update(w_ada=r_ada, w_in=r_in, w_out=r_out, w_up=r_up, w_down=r_down)

    order = ["w_ada", "b_ada", "pre_mix_g", "post_mix_g", "w_in", "sgu_norm_g", "w_spatial", "b_spatial", "w_pool",
             "pool_scale", "w_out", "pre_ffn_g", "post_ffn_g", "w_up", "conv_w", "conv_b", "w_down"]
    return (loss, grad_x[None], *[res[nm][0] for nm in order], *[res[nm][1] for nm in order],
            *[res[nm][2] for nm in order], *[res[nm][3] for nm in order])
```

```python
import functools
import math

import jax
import jax.numpy as jnp
from jax import lax
from jax.experimental import pallas as pl
from jax.experimental.pallas import tpu as pltpu

F32 = jnp.float32
BF16 = jnp.bfloat16
MESH = pl.DeviceIdType.MESH

EPS = 1e-6
D = 1024
HEAD = 128
N_HEAD = 4
A_WIDTH = 512
B_WIDTH = 512
IN_WIDTH = 1536
WINDOWS = (2, 4, 8, 16)
CHUNK = 64
FF = 2816
N_DEV = 8
FF_CHUNK = 704
N_MOD = 6
MOD_COLS = 768

ADAM_LR = 0.001
ADAM_B1 = 0.9
ADAM_B2 = 0.999
ADAM_EPS = 1e-08
ADAM_WD = 0.01
ADAM_STEP = 10

VMEM_LIMIT_V7X = 56 * 1024 * 1024
HALO = 8
POOL_HALO = 16

VMEM_SPEC = pl.BlockSpec(memory_space=pltpu.VMEM)
ANY_SPEC = pl.BlockSpec(memory_space=pl.ANY)


def _bf(x):
    return x.astype(BF16)


def _mm(a, b):
    return jnp.dot(a, b, preferred_element_type=F32)


def _mm_nt(a, b):
    return lax.dot_general(a, b, (((1,), (1,)), ((), ())), preferred_element_type=F32)


def _mm_tn(a, b):
    return lax.dot_general(a, b, (((0,), (0,)), ((), ())), preferred_element_type=F32)


def _rstd(x):
    return lax.rsqrt(jnp.mean(x * x, axis=-1, keepdims=True) + EPS)


def _sum0(x):
    return jnp.sum(x, axis=0, keepdims=True)


def _rowmean(x):
    return jnp.mean(x, axis=-1, keepdims=True)


_GELU_K = math.sqrt(2.0 / math.pi)


def _gelu_and_grad(x):
    x2 = x * x
    th = jnp.tanh(_GELU_K * (x + 0.044715 * (x * x2)))
    cdf = 0.5 * (1.0 + th)
    grad = cdf + 0.5 * x * (1.0 - th * th) * (_GELU_K * (1.0 + 3.0 * 0.044715 * x2))
    return x * cdf, grad


def _gelu(x):
    return x * (0.5 * (1.0 + jnp.tanh(_GELU_K * (x + 0.044715 * (x * x * x)))))


def _sigmoid(x):
    return 0.5 * jnp.tanh(0.5 * x) + 0.5


def _sgu_mask():
    ri = lax.broadcasted_iota(jnp.int32, (HEAD, HEAD), 0)
    ci = lax.broadcasted_iota(jnp.int32, (HEAD, HEAD), 1)
    return (ci // CHUNK) <= (ri // CHUNK)


def _window_sum(ext, w, trailing):
    n = ext.shape[0]
    s, k = ext, 1
    while k < w:
        s = s + pltpu.roll(s, k if trailing else n - k, 0)
        k *= 2
    return s


def _inv_count(row0, n, w):
    t = row0 + lax.broadcasted_iota(jnp.int32, (n, 1), 0)
    return 1.0 / jnp.minimum(t + 1, w).astype(F32)


def _shift_down(v, before, k):
    rows = lax.broadcasted_iota(jnp.int32, before.shape, 0)
    r = pltpu.roll(v, k, 0)
    top = jnp.where(rows < k, pltpu.roll(before, k, 0), r[0:HALO])
    return jnp.concatenate([top, r[HALO:]], axis=0)


def _shift_up(v, after, k):
    n = v.shape[0]
    rows = lax.broadcasted_iota(jnp.int32, after.shape, 0)
    r = pltpu.roll(v, n - k, 0)
    bottom = jnp.where(rows >= HALO - k, pltpu.roll(after, HALO - k, 0), r[n - HALO:])
    return jnp.concatenate([r[:n - HALO], bottom], axis=0)


def _adamw(w, g, m, v):
    m = ADAM_B1 * m + (1.0 - ADAM_B1) * g
    v = ADAM_B2 * v + (1.0 - ADAM_B2) * (g * g)
    m_hat = m / (1.0 - ADAM_B1 ** ADAM_STEP)
    v_hat = v / (1.0 - ADAM_B2 ** ADAM_STEP)
    delta = -ADAM_LR * (m_hat / (jnp.sqrt(v_hat) + ADAM_EPS) + ADAM_WD * w)
    return delta, m, v


def _coords():
    return lax.axis_index("x"), lax.axis_index("y"), lax.axis_index("c")


def _peer(k):
    x, y, c = _coords()
    return (x ^ ((k >> 2) & 1), y ^ ((k >> 1) & 1), c ^ (k & 1))


def _my_index():
    x, y, c = _coords()
    return 4 * x + 2 * y + c


def _adaln_fwd(c_row, w_ada, b_ada3):
    def body(c_ref, w_ref, b_ref, mod_ref, scx_ref, scbuf, stage, recv, send_sems, recv_sems):
        me = _my_index()
        cv = c_ref[...]
        scbuf[0] = cv * _sigmoid(cv)
        first = [
            pltpu.make_async_remote_copy(scbuf.at[0], scbuf.at[k], send_sems.at[0, k], recv_sems.at[0, k],
                                         device_id=_peer(k), device_id_type=MESH)
            for k in range(1, N_DEV)
        ]
        for cp in first:
            cp.start()
        for cp in first:
            cp.wait()
        scx_ref[...] = jnp.zeros(scx_ref.shape, F32)
        for k in range(N_DEV):
            scx_ref[k:k + 1, :] = scbuf[k]
        prod = _mm(_bf(scx_ref[...]), _bf(w_ref[...]))
        for k in range(N_DEV):
            stage[k] = prod[k:k + 1, :] + b_ref[me]
        second = [
            pltpu.make_async_remote_copy(stage.at[k], recv.at[k], send_sems.at[1, k], recv_sems.at[1, k],
                                         device_id=_peer(k), device_id_type=MESH)
            for k in range(1, N_DEV)
        ]
        for cp in second:
            cp.start()
        mod_ref[me] = stage[0]
        for cp in second:
            cp.wait()
        for k in range(1, N_DEV):
            mod_ref[me ^ k] = recv[k]

    return pl.pallas_call(
        body,
        name="adaln_fwd",
        out_shape=(jax.ShapeDtypeStruct((N_DEV, 1, MOD_COLS), F32), jax.ShapeDtypeStruct((2 * N_DEV, D), F32)),
        in_specs=[VMEM_SPEC] * 3,
        out_specs=(VMEM_SPEC, VMEM_SPEC),
        scratch_shapes=[
            pltpu.VMEM((N_DEV, 1, D), F32),
            pltpu.VMEM((N_DEV, 1, MOD_COLS), F32),
            pltpu.VMEM((N_DEV, 1, MOD_COLS), F32),
            pltpu.SemaphoreType.DMA((2, N_DEV)),
            pltpu.SemaphoreType.DMA((2, N_DEV)),
        ],
        compiler_params=pltpu.CompilerParams(vmem_limit_bytes=VMEM_LIMIT_V7X),
    )(c_row, w_ada, b_ada3)


def _adaln_bwd(dmod3, scx, w_ada, m_ada, v_ada):
    def body(dm_ref, scx_ref, w_ref, m_ref, v_ref, g_ref, d_ref, nm_ref, nv_ref, recv, dm2d, send_sems, recv_sems):
        me = _my_index()
        copies = [
            pltpu.make_async_remote_copy(dm_ref.at[me ^ k], recv.at[k], send_sems.at[k], recv_sems.at[k],
                                         device_id=_peer(k), device_id_type=MESH)
            for k in range(1, N_DEV)
        ]
        for cp in copies:
            cp.start()
        dm2d[...] = jnp.zeros(dm2d.shape, F32)
        dm2d[0:1, :] = dm_ref[me]
        for cp in copies:
            cp.wait()
        for k in range(1, N_DEV):
            dm2d[k:k + 1, :] = recv[k]
        g = _mm_tn(_bf(scx_ref[...]), _bf(dm2d[...]))
        g_ref[...] = g
        delta, m, v = _adamw(w_ref[...], g, m_ref[...], v_ref[...])
        d_ref[...] = delta
        nm_ref[...] = m
        nv_ref[...] = v

    shard = jax.ShapeDtypeStruct((D, MOD_COLS), F32)
    return pl.pallas_call(
        body,
        name="adaln_bwd",
        out_shape=(shard,) * 4,
        in_specs=[VMEM_SPEC] * 5,
        out_specs=(VMEM_SPEC,) * 4,
        scratch_shapes=[
            pltpu.VMEM((N_DEV, 1, MOD_COLS), F32),
            pltpu.VMEM((2 * N_DEV, MOD_COLS), F32),
            pltpu.SemaphoreType.DMA((N_DEV,)),
            pltpu.SemaphoreType.DMA((N_DEV,)),
        ],
        compiler_params=pltpu.CompilerParams(vmem_limit_bytes=VMEM_LIMIT_V7X),
    )(dmod3, scx, w_ada, m_ada, v_ada)


class _GatherSteps:
    def __init__(self, ins, outs, stages, send_sems, recv_sems, local_sems):
        self.ins, self.outs, self.stages = ins, outs, stages
        self.send_sems, self.recv_sems, self.local_sems = send_sems, recv_sems, local_sems
        x, y, c = _coords()
        self.c = c
        self.me, self.sibling = (x, y, c), (x, y, 1 - c)
        self.chips = [(1 - x, y), (x, 1 - y), (1 - x, 1 - y)]

    def _copy(self, a, k, block, to, from_stage=False):
        dst = self.outs[a].at[4 * block[0] + 2 * block[1] + block[2]]
        return pltpu.make_async_remote_copy(self.stages[a] if from_stage else dst, dst, self.send_sems.at[a, k],
                                            self.recv_sems.at[a, k], device_id=to, device_id_type=MESH)

    def _local(self, a):
        me = self.me
        return pltpu.make_async_copy(self.stages[a], self.outs[a].at[4 * me[0] + 2 * me[1] + me[2]],
                                     self.local_sems.at[a])

    def _first(self, a):
        cps = [self._copy(a, 0, self.me, self.sibling, from_stage=True)]
        return cps + [self._copy(a, 1 + j, self.me, (*chip, self.c), from_stage=True)
                      for j, chip in enumerate(self.chips)]

    def _passed(self, a, j):
        return self._copy(a, 4 + j, (*self.chips[j], self.c), self.sibling)

    def start(self):
        for a in range(len(self.ins)):
            self.stages[a][...] = self.ins[a][...].astype(self.stages[a].dtype)
            self._local(a).start()
            for cp in self._first(a):
                cp.start()

    def forward(self):
        for a in range(len(self.ins)):
            for j, chip in enumerate(self.chips):
                self._copy(a, 1 + j, (*chip, self.c), self.me).wait_recv()
                self._passed(a, j).start()

    def finish(self):
        for a in range(len(self.ins)):
            self._copy(a, 0, self.sibling, self.me).wait_recv()
            for j, chip in enumerate(self.chips):
                self._copy(a, 4 + j, (*chip, 1 - self.c), self.me).wait_recv()
            for cp in self._first(a) + [self._passed(a, j) for j in range(3)]:
                cp.wait_send()
            self._local(a).wait()


def _gather_scratch(shards, out_dtypes):
    n = len(shards)
    return ([pltpu.VMEM(s.shape, dt) for s, dt in zip(shards, out_dtypes)]
            + [pltpu.SemaphoreType.DMA((n, 7)), pltpu.SemaphoreType.DMA((n, 7)), pltpu.SemaphoreType.DMA((n,))])


def _gather_out_shapes(shards, out_dtypes):
    return tuple(jax.ShapeDtypeStruct((N_DEV, *s.shape), dt) for s, dt in zip(shards, out_dtypes))


def _allgather_shards(shards, out_dtypes):
    n = len(shards)

    def body(*refs):
        steps = _GatherSteps(refs[:n], refs[n:2 * n], refs[2 * n:3 * n], *refs[3 * n:])
        steps.start()
        steps.forward()
        steps.finish()

    return pl.pallas_call(
        body,
        name="weight_allgather",
        out_shape=_gather_out_shapes(shards, out_dtypes),
        in_specs=[VMEM_SPEC] * n,
        out_specs=(ANY_SPEC,) * n,
        scratch_shapes=_gather_scratch(shards, out_dtypes),
        compiler_params=pltpu.CompilerParams(vmem_limit_bytes=VMEM_LIMIT_V7X),
    )(*shards)


class _ChipExchangeSteps:
    FLIPS = ((1, 0), (0, 1), (1, 1))

    def __init__(self, srcs, dsts, send_sems, recv_sems):
        self.srcs, self.dsts, self.send_sems, self.recv_sems = srcs, dsts, send_sems, recv_sems

    def _copies(self):
        x, y, c = _coords()
        out = []
        for a in range(len(self.srcs)):
            for j, (fx, fy) in enumerate(self.FLIPS):
                k = 3 * a + j
                out.append(pltpu.make_async_remote_copy(
                    self.srcs[a].at[x ^ fx, y ^ fy], self.dsts[a].at[j], self.send_sems.at[k], self.recv_sems.at[k],
                    device_id=(x ^ fx, y ^ fy, c), device_id_type=MESH))
        return out

    def start(self):
        for cp in self._copies():
            cp.start()

    def finish(self):
        for cp in self._copies():
            cp.wait()


def _mix_fwd(x, mod, g_pre, g_post, w_in_b, sgn, w_sp, b_sp_t, w_pool, p_scale, w_out_b, ts, shards, shard_dtypes):
    t_len = x.shape[0]
    nt, nb = t_len // ts, ts // HEAD
    ns = len(shards)

    def body(*refs):
        (x_ref, mod_ref, g1_ref, g2_ref, win_ref, sgn_ref, ws_ref, bst_ref, wp_ref, ps_ref, wout_ref) = refs[:11]
        x1_ref, proj_ref, mixed_ref = refs[11 + ns:14 + ns]
        pbuf, cat = refs[14 + 2 * ns:16 + 2 * ns]
        gather = _GatherSteps(refs[11:11 + ns], refs[14 + ns:14 + 2 * ns], refs[16 + 2 * ns:16 + 3 * ns],
                              *refs[16 + 3 * ns:])
        i = pl.program_id(0)

        @pl.when(i == 0)
        def _():
            pbuf[0:POOL_HALO, :] = jnp.zeros((POOL_HALO, B_WIDTH), F32)
            gather.start()

        @pl.when(i == (3 * nt) // 4)
        def _():
            gather.forward()

        xv = x_ref[...]
        sh, sc, gm = mod_ref[0:1, :], mod_ref[1:2, :], mod_ref[2:3, :]
        h = (xv * _rstd(xv) * g1_ref[...]) * (1.0 + sc) + sh
        proj_ref[...] = _mm(_bf(h), win_ref[...])
        pbuf[POOL_HALO:POOL_HALO + ts, :] = proj_ref[:, 2 * A_WIDTH:]
        smask = _sgu_mask()
        for hd in range(N_HEAD):
            u = _gelu(proj_ref[:, hd * HEAD:(hd + 1) * HEAD])
            v = _gelu(proj_ref[:, A_WIDTH + hd * HEAD:A_WIDTH + (hd + 1) * HEAD])
            vn = _bf(v * _rstd(v) * sgn_ref[hd:hd + 1, :])
            wm = _bf(jnp.where(smask, ws_ref[hd], 0.0))
            bias = bst_ref[:, hd:hd + 1]
            for b in range(nb):
                rows = slice(b * HEAD, (b + 1) * HEAD)
                z = _mm(wm, vn[rows]) + bias
                cat[rows, hd * HEAD:(hd + 1) * HEAD] = _bf(u[rows] * z)
        for g, w in enumerate(WINDOWS):
            cols = slice(g * HEAD, (g + 1) * HEAD)
            ext = pbuf[:, cols]
            pooled = _window_sum(ext, w, True)[POOL_HALO:] * _inv_count(i * ts, ts, w) - ext[POOL_HALO:]
            cat[:, A_WIDTH + g * HEAD:A_WIDTH + (g + 1) * HEAD] = _bf(_mm(_bf(pooled), _bf(wp_ref[g])) * ps_ref[:, cols])
        pbuf[0:POOL_HALO, :] = pbuf[ts:ts + POOL_HALO, :]
        mixed = _mm(cat[...], wout_ref[...])
        mixed_ref[...] = mixed
        x1_ref[...] = xv + gm * (mixed * _rstd(mixed) * g2_ref[...])

        @pl.when(i == nt - 1)
        def _():
            gather.finish()

    tile = lambda wid: pl.BlockSpec((ts, wid), lambda i: (i, 0))
    outs = pl.pallas_call(
        body,
        name="mix_fwd",
        grid=(nt,),
        out_shape=(jax.ShapeDtypeStruct((t_len, D), F32), jax.ShapeDtypeStruct((t_len, IN_WIDTH), F32),
                   jax.ShapeDtypeStruct((t_len, D), F32)) + _gather_out_shapes(shards, shard_dtypes),
        in_specs=[tile(D)] + [VMEM_SPEC] * (10 + ns),
        out_specs=(tile(D), tile(IN_WIDTH), tile(D)) + (ANY_SPEC,) * ns,
        scratch_shapes=[pltpu.VMEM((POOL_HALO + ts, B_WIDTH), F32), pltpu.VMEM((ts, D), BF16)]
        + _gather_scratch(shards, shard_dtypes),
        compiler_params=pltpu.CompilerParams(dimension_semantics=("arbitrary",), vmem_limit_bytes=VMEM_LIMIT_V7X),
    )(x, mod, g_pre, g_post, w_in_b, sgn, w_sp, b_sp_t, w_pool, p_scale, w_out_b, *shards)
    return outs[:3], outs[3:]


def _ffn_fwd(x1, target, mod, g_pre, g_post, w_up_b, conv_w8, conv_b8, w_down_b, ts):
    t_len = x1.shape[0]
    nt = t_len // ts

    def body(x1_ref, tgt_ref, mod_ref, g3_ref, g4_ref, wup_ref, cw_ref, cb_ref, wdown_ref,
             up_ref, f_ref, dx2_ref, loss_ref, ucarry):
        i = pl.program_id(0)

        @pl.when(i == 0)
        def _():
            ucarry[...] = jnp.zeros(ucarry.shape, F32)
            loss_ref[...] = jnp.zeros(loss_ref.shape, F32)

        x1v = x1_ref[...]
        sh, sc, gf = mod_ref[3:4, :], mod_ref[4:5, :], mod_ref[5:6, :]
        h2 = _bf((x1v * _rstd(x1v) * g3_ref[...]) * (1.0 + sc) + sh)
        half = N_DEV // 2

        def up_pair(j):
            return [_mm(h2, wup_ref[jj]) for jj in (j, j + half)]

        f = jnp.zeros((ts, D), F32)
        ups = up_pair(0)
        for j in range(half):
            nxt = up_pair(j + 1) if j + 1 < half else None
            ys = []
            for up, jj in zip(ups, (j, j + half)):
                up_ref[jj] = up
                before = ucarry[jj]
                ucarry[jj] = up[ts - HALO:, :]
                cw = cw_ref[jj]
                ys.append(cb_ref[jj:jj + 1, :] + _shift_down(up, before, 2) * cw[0:1, :]
                          + _shift_down(up, before, 1) * cw[1:2, :] + up * cw[2:3, :])
            gate, val = ys
            act = gate * _sigmoid(gate) * val
            f = f + _mm(_bf(act), wdown_ref[j * FF_CHUNK:(j + 1) * FF_CHUNK, :])
            ups = nxt
        f_ref[...] = f
        x2 = x1v + gf * (f * _rstd(f) * g4_ref[...])
        err = x2 - tgt_ref[...]
        loss_ref[...] += _sum0(err * err)
        dx2_ref[...] = err * (1.0 / D)

    tile = pl.BlockSpec((ts, D), lambda i: (i, 0))
    return pl.pallas_call(
        body,
        name="ffn_fwd",
        grid=(nt,),
        out_shape=(jax.ShapeDtypeStruct((N_DEV, t_len, FF_CHUNK), F32), jax.ShapeDtypeStruct((t_len, D), F32),
                   jax.ShapeDtypeStruct((t_len, D), F32), jax.ShapeDtypeStruct((1, D), F32)),
        in_specs=[tile, tile] + [VMEM_SPEC] * 7,
        out_specs=(pl.BlockSpec((N_DEV, ts, FF_CHUNK), lambda i: (0, i, 0)), tile, tile,
                   pl.BlockSpec((1, D), lambda i: (0, 0))),
        scratch_shapes=[pltpu.VMEM((N_DEV, HALO, FF_CHUNK), F32)],
        compiler_params=pltpu.CompilerParams(dimension_semantics=("arbitrary",), vmem_limit_bytes=VMEM_LIMIT_V7X),
    )(x1, target, mod, g_pre, g_post, w_up_b, conv_w8, conv_b8, w_down_b)


def _ffn_bwd(dx2, f, x1, up, mod, g_pre, g_post, w_up_b, conv_w8, conv_b8, w_down_b, ts):
    t_len = x1.shape[0]
    nt = t_len // ts
    half = N_DEV // 2

    def body(dx2_ref, f_ref, x1_ref, up_ref, halo_ref, mod_ref, g3_ref, g4_ref, wup_ref, cw_ref, cb_ref, wdown_ref,
             dx1_ref, dup_ref, act_ref, df_ref, h2_ref, dmod_ref, dg3_ref, dg4_ref, dcb_ref, dcw_ref,
             dycarry, dh2acc):
        i = pl.program_id(0)
        r = nt - 1 - i

        @pl.when(i == 0)
        def _():
            for ref in (dmod_ref, dg3_ref, dg4_ref, dcb_ref, dcw_ref, dycarry):
                ref[...] = jnp.zeros(ref.shape, F32)

        dx2v, fv, x1v = dx2_ref[...], f_ref[...], x1_ref[...]
        sh, sc, gf = mod_ref[3:4, :], mod_ref[4:5, :], mod_ref[5:6, :]
        g3, g4 = g3_ref[...], g4_ref[...]
        rstd4 = _rstd(fv)
        fh = fv * rstd4
        dmod_ref[2:3, :] += _sum0(dx2v * (fh * g4))
        dr = dx2v * gf
        dg4_ref[...] += _sum0(dr * fh)
        dfh = dr * g4
        dfb = _bf(rstd4 * (dfh - fh * _rowmean(dfh * fh)))
        df_ref[...] = dfb
        rstd3 = _rstd(x1v)
        xh = x1v * rstd3
        n3 = xh * g3
        h2_ref[...] = _bf(n3 * (1.0 + sc) + sh)
        dh2acc[...] = jnp.zeros((ts, D), F32)
        keep = jnp.where(r > 0, 1.0, 0.0).astype(F32)

        def dact_of(j):
            return _mm_nt(dfb, wdown_ref[j * FF_CHUNK:(j + 1) * FF_CHUNK, :])

        dact_next = dact_of(0)
        for j in range(half):
            dact = dact_next
            if j + 1 < half:
                dact_next = dact_of(j + 1)
            ys = []
            for jj in (j, j + half):
                before = halo_ref[jj] * keep
                upc = up_ref[jj]
                cw = cw_ref[jj]
                ys.append(cb_ref[jj:jj + 1, :] + _shift_down(upc, before, 2) * cw[0:1, :]
                          + _shift_down(upc, before, 1) * cw[1:2, :] + upc * cw[2:3, :])
            gate, val = ys
            sg = _sigmoid(gate)
            silu = gate * sg
            act_ref[j] = _bf(silu * val)
            dys = (dact * val * (sg * (1.0 + gate * (1.0 - sg))), dact * silu)
            for q, jj in enumerate((j, j + half)):
                dy = dys[q]
                cw = cw_ref[jj]
                dcb_ref[jj:jj + 1, :] += _sum0(dy)
                after = dycarry[jj]
                dycarry[jj] = dy[0:HALO, :]
                dy1, dy2 = _shift_up(dy, after, 1), _shift_up(dy, after, 2)
                upc = up_ref[jj]
                dcw_ref[jj, 0:1, :] += _sum0(dy2 * upc)
                dcw_ref[jj, 1:2, :] += _sum0(dy1 * upc)
                dcw_ref[jj, 2:3, :] += _sum0(dy * upc)
                dup = _bf(dy * cw[2:3, :] + dy1 * cw[1:2, :] + dy2 * cw[0:1, :])
                dup_ref[jj] = dup
                dh2acc[...] += _mm_nt(dup, wup_ref[jj])
        dh2 = dh2acc[...]
        dmod_ref[0:1, :] += _sum0(dh2)
        dmod_ref[1:2, :] += _sum0(dh2 * n3)
        dn3 = dh2 * (1.0 + sc)
        dg3_ref[...] += _sum0(dn3 * xh)
        dxh = dn3 * g3
        dx1_ref[...] = dx2v + rstd3 * (dxh - xh * _rowmean(dxh * xh))

    tile = pl.BlockSpec((ts, D), lambda i: (nt - 1 - i, 0))
    chunked = lambda n: pl.BlockSpec((n, ts, FF_CHUNK), lambda i: (0, nt - 1 - i, 0))
    halo = pl.BlockSpec((N_DEV, HALO, FF_CHUNK), lambda i: (0, jnp.maximum((nt - 1 - i) * (ts // HALO) - 1, 0), 0))
    const = lambda *shape: pl.BlockSpec(shape, lambda i: (0,) * len(shape))
    return pl.pallas_call(
        body,
        name="ffn_bwd",
        grid=(nt,),
        out_shape=(jax.ShapeDtypeStruct((t_len, D), F32), jax.ShapeDtypeStruct((N_DEV, t_len, FF_CHUNK), BF16),
                   jax.ShapeDtypeStruct((half, t_len, FF_CHUNK), BF16), jax.ShapeDtypeStruct((t_len, D), BF16),
                   jax.ShapeDtypeStruct((t_len, D), BF16), jax.ShapeDtypeStruct((3, D), F32),
                   jax.ShapeDtypeStruct((1, D), F32), jax.ShapeDtypeStruct((1, D), F32),
                   jax.ShapeDtypeStruct((N_DEV, FF_CHUNK), F32), jax.ShapeDtypeStruct((N_DEV, 3, FF_CHUNK), F32)),
        in_specs=[tile, tile, tile, chunked(N_DEV), halo] + [VMEM_SPEC] * 7,
        out_specs=(tile, chunked(N_DEV), chunked(half), tile, tile, const(3, D), const(1, D), const(1, D),
                   const(N_DEV, FF_CHUNK), const(N_DEV, 3, FF_CHUNK)),
        scratch_shapes=[pltpu.VMEM((N_DEV, HALO, FF_CHUNK), F32), pltpu.VMEM((ts, D), F32)],
        compiler_params=pltpu.CompilerParams(dimension_semantics=("arbitrary",), vmem_limit_bytes=VMEM_LIMIT_V7X),
    )(dx2, f, x1, up, up, mod, g_pre, g_post, w_up_b, conv_w8, conv_b8, w_down_b)


def _wgrad_up(h2, dup, ts):
    t_len = h2.shape[0]
    nt, half = t_len // ts, N_DEV // 2

    def body(h2_ref, dup_ref, out_ref):
        @pl.when(pl.program_id(1) == 0)
        def _():
            out_ref[...] = jnp.zeros(out_ref.shape, F32)

        for q in range(half):
            out_ref[q] += _mm_tn(h2_ref[...], dup_ref[q])

    return pl.pallas_call(
        body,
        name="wgrad_up",
        grid=(2, nt),
        out_shape=jax.ShapeDtypeStruct((N_DEV, D, FF_CHUNK), F32),
        in_specs=[pl.BlockSpec((ts, D), lambda g, t: (t, 0)), pl.BlockSpec((half, ts, FF_CHUNK), lambda g, t: (g, t, 0))],
        out_specs=pl.BlockSpec((half, D, FF_CHUNK), lambda g, t: (g, 0, 0)),
        compiler_params=pltpu.CompilerParams(dimension_semantics=("arbitrary", "arbitrary"),
                                             vmem_limit_bytes=VMEM_LIMIT_V7X),
    )(h2, dup)


def _wgrad_down(act, df, ts):
    t_len = df.shape[0]
    nt, half = t_len // ts, N_DEV // 2

    def body(act_ref, df_ref, out_ref):
        @pl.when(pl.program_id(0) == 0)
        def _():
            out_ref[...] = jnp.zeros(out_ref.shape, F32)

        for q in range(half):
            out_ref[q] += _mm_tn(act_ref[q], df_ref[...])

    return pl.pallas_call(
        body,
        name="wgrad_down",
        grid=(nt,),
        out_shape=jax.ShapeDtypeStruct((half, FF_CHUNK, D), F32),
        in_specs=[pl.BlockSpec((half, ts, FF_CHUNK), lambda t: (0, t, 0)), pl.BlockSpec((ts, D), lambda t: (t, 0))],
        out_specs=pl.BlockSpec((half, FF_CHUNK, D), lambda t: (0, 0, 0)),
        compiler_params=pltpu.CompilerParams(dimension_semantics=("arbitrary",), vmem_limit_bytes=VMEM_LIMIT_V7X),
    )(act, df)


def _mix_bwd(dx1, x, proj, mixed, mod, g_pre, g_post, w_in_b, sgn, w_sp, b_sp_t, w_pool, p_scale, w_out_b, ts, rs_srcs):
    t_len = x.shape[0]
    nt, nb = t_len // ts, ts // HEAD
    nr = len(rs_srcs)

    def body(*refs):
        (dx1_ref, x_ref, proj_ref, halo_ref, mixed_ref, mod_ref, g1_ref, g2_ref, win_ref, sgn_ref, ws_ref,
         bst_ref, wp_ref, ps_ref, wout_ref) = refs[:15]
        (gx_ref, dwin_ref, dwout_ref, dmod_ref, dg1_ref, dg2_ref, dsgn_ref, dws_ref, dbst_ref, dwp_ref,
         dps_ref) = refs[15 + nr:26 + nr]
        pbuf, dwsbuf, cat, dproj, dcat = refs[26 + 2 * nr:31 + 2 * nr]
        exchange = _ChipExchangeSteps(refs[15:15 + nr], refs[26 + nr:26 + 2 * nr], *refs[31 + 2 * nr:])
        i = pl.program_id(0)
        r = nt - 1 - i

        @pl.when(i == 0)
        def _():
            exchange.start()
            for ref in (dwin_ref, dwout_ref, dmod_ref, dg1_ref, dg2_ref, dsgn_ref, dws_ref, dbst_ref, dwp_ref, dps_ref):
                ref[...] = jnp.zeros(ref.shape, F32)
            dwsbuf[ts:ts + POOL_HALO, :] = jnp.zeros((POOL_HALO, B_WIDTH), F32)

        xv, dx1v, mixed = x_ref[...], dx1_ref[...], mixed_ref[...]
        sh, sc, gm = mod_ref[0:1, :], mod_ref[1:2, :], mod_ref[2:3, :]
        g1, g2 = g1_ref[...], g2_ref[...]
        rstd2 = _rstd(mixed)
        mh = mixed * rstd2
        dmod_ref[2:3, :] += _sum0(dx1v * (mh * g2))
        dr = dx1v * gm
        dg2_ref[...] += _sum0(dr * mh)
        dmh = dr * g2
        dmb = _bf(rstd2 * (dmh - mh * _rowmean(dmh * mh)))
        dcat[...] = _mm_nt(dmb, wout_ref[...])
        smask = _sgu_mask()
        for hd in range(N_HEAD):
            ucols = slice(hd * HEAD, (hd + 1) * HEAD)
            vcols = slice(A_WIDTH + hd * HEAD, A_WIDTH + (hd + 1) * HEAD)
            u, du_dp = _gelu_and_grad(proj_ref[:, ucols])
            v, dv_dp = _gelu_and_grad(proj_ref[:, vcols])
            rs = _rstd(v)
            vhat = v * rs
            gn = sgn_ref[hd:hd + 1, :]
            vn = _bf(vhat * gn)
            wm = _bf(jnp.where(smask, ws_ref[hd], 0.0))
            bias = bst_ref[:, hd:hd + 1]
            dzsum = jnp.zeros((HEAD, HEAD), F32)
            dwm = jnp.zeros((HEAD, HEAD), F32)
            dvn_parts = []
            for b in range(nb):
                rows = slice(b * HEAD, (b + 1) * HEAD)
                z = _mm(wm, vn[rows]) + bias
                da = dcat[rows, ucols]
                cat[rows, ucols] = _bf(u[rows] * z)
                dz = da * u[rows]
                dzsum = dzsum + dz
                dzb = _bf(dz)
                dwm = dwm + _mm_nt(dzb, vn[rows])
                dvn_parts.append(_mm_tn(wm, dzb))
                dproj[rows, ucols] = (da * z) * du_dp[rows]
            dvn = jnp.concatenate(dvn_parts, axis=0)
            dsgn_ref[hd:hd + 1, :] += _sum0(dvn * vhat)
            dvh = dvn * gn
            dproj[:, vcols] = (rs * (dvh - vhat * _rowmean(dvh * vhat))) * dv_dp
            dws_ref[hd] += jnp.where(smask, dwm, 0.0)
            dbst_ref[:, hd:hd + 1] += jnp.sum(dzsum, axis=1, keepdims=True)
        keep = jnp.where(r > 0, 1.0, 0.0).astype(F32)
        pbuf[0:POOL_HALO, :] = halo_ref[...] * keep
        pbuf[POOL_HALO:POOL_HALO + ts, :] = proj_ref[:, 2 * A_WIDTH:]
        for g, w in enumerate(WINDOWS):
            cols = slice(g * HEAD, (g + 1) * HEAD)
            ccols = slice(A_WIDTH + g * HEAD, A_WIDTH + (g + 1) * HEAD)
            pcols = slice(2 * A_WIDTH + g * HEAD, 2 * A_WIDTH + (g + 1) * HEAD)
            wpg = _bf(wp_ref[g])
            psg = ps_ref[:, cols]
            ext = pbuf[:, cols]
            inv = _inv_count(r * ts, ts, w)
            pb = _bf(_window_sum(ext, w, True)[POOL_HALO:] * inv - ext[POOL_HALO:])
            yb = _mm(pb, wpg)
            dob = dcat[:, ccols]
            cat[:, ccols] = _bf(yb * psg)
            dps_ref[:, cols] += _sum0(dob * yb)
            dyb = _bf(dob * psg)
            dwp_ref[g] += _mm_tn(pb, dyb)
            dpooled = _mm_nt(dyb, wpg)
            dwsbuf[0:ts, cols] = dpooled * inv
            dproj[:, pcols] = _window_sum(dwsbuf[:, cols], w, False)[0:ts] - dpooled
        dwsbuf[ts:ts + POOL_HALO, :] = dwsbuf[0:POOL_HALO, :]
        dpb = _bf(dproj[...])
        rstd1 = _rstd(xv)
        xh = xv * rstd1
        n1 = xh * g1
        dwin_ref[...] += _mm_tn(_bf(n1 * (1.0 + sc) + sh), dpb)
        dwout_ref[...] += _mm_tn(cat[...], dmb)
        dh = _mm_nt(dpb, win_ref[...])
        dmod_ref[0:1, :] += _sum0(dh)
        dmod_ref[1:2, :] += _sum0(dh * n1)
        dn1 = dh * (1.0 + sc)
        dg1_ref[...] += _sum0(dn1 * xh)
        dxh = dn1 * g1
        gx_ref[...] = dx1v + rstd1 * (dxh - xh * _rowmean(dxh * xh))

        @pl.when(i == nt - 1)
        def _():
            exchange.finish()

    tile = lambda wid: pl.BlockSpec((ts, wid), lambda i: (nt - 1 - i, 0))
    halo = pl.BlockSpec((POOL_HALO, B_WIDTH),
                        lambda i: (jnp.maximum((nt - 1 - i) * (ts // POOL_HALO) - 1, 0), 2 * A_WIDTH // B_WIDTH))
    const = lambda *shape: pl.BlockSpec(shape, lambda i: (0,) * len(shape))
    outs = pl.pallas_call(
        body,
        name="mix_bwd",
        grid=(nt,),
        out_shape=(jax.ShapeDtypeStruct((t_len, D), F32), jax.ShapeDtypeStruct((D, IN_WIDTH), F32),
                   jax.ShapeDtypeStruct((D, D), F32), jax.ShapeDtypeStruct((3, D), F32),
                   jax.ShapeDtypeStruct((1, D), F32), jax.ShapeDtypeStruct((1, D), F32),
                   jax.ShapeDtypeStruct((N_HEAD, HEAD), F32), jax.ShapeDtypeStruct((N_HEAD, HEAD, HEAD), F32),
                   jax.ShapeDtypeStruct((HEAD, N_HEAD), F32), jax.ShapeDtypeStruct((N_HEAD, HEAD, HEAD), F32),
                   jax.ShapeDtypeStruct((1, B_WIDTH), F32))
        + tuple(jax.ShapeDtypeStruct((3, *s.shape[2:]), F32) for s in rs_srcs),
        in_specs=[tile(D), tile(D), tile(IN_WIDTH), halo, tile(D)] + [VMEM_SPEC] * 10 + [ANY_SPEC] * nr,
        out_specs=(tile(D), const(D, IN_WIDTH), const(D, D), const(3, D), const(1, D), const(1, D),
                   const(N_HEAD, HEAD), const(N_HEAD, HEAD, HEAD), const(HEAD, N_HEAD), const(N_HEAD, HEAD, HEAD),
                   const(1, B_WIDTH)) + (ANY_SPEC,) * nr,
        scratch_shapes=[pltpu.VMEM((POOL_HALO + ts, B_WIDTH), F32), pltpu.VMEM((ts + POOL_HALO, B_WIDTH), F32),
                        pltpu.VMEM((ts, D), BF16), pltpu.VMEM((ts, IN_WIDTH), F32), pltpu.VMEM((ts, D), F32),
                        pltpu.SemaphoreType.DMA((3 * nr,)), pltpu.SemaphoreType.DMA((3 * nr,))],
        compiler_params=pltpu.CompilerParams(dimension_semantics=("arbitrary",), vmem_limit_bytes=VMEM_LIMIT_V7X),
    )(dx1, x, proj, proj, mixed, mod, g_pre, g_post, w_in_b, sgn, w_sp, b_sp_t, w_pool, p_scale, w_out_b, *rs_srcs)
    return outs[:11], outs[11:]


def _exchange(name, srcs, out_shapes, plan):
    n = len(srcs)
    n_copies = len(plan(0, 0, 0))

    def body(*refs):
        ins, outs = refs[:n], refs[n:2 * n]
        send_sems, recv_sems = refs[2 * n:]
        copies = [
            pltpu.make_async_remote_copy(ins[a].at[src], outs[a].at[dst] if dst else outs[a], send_sems.at[k],
                                         recv_sems.at[k], device_id=peer, device_id_type=MESH)
            for k, (a, src, dst, peer) in enumerate(plan(*_coords()))
        ]
        for cp in copies:
            cp.start()
        for cp in copies:
            cp.wait()

    return pl.pallas_call(
        body,
        name=name,
        out_shape=tuple(jax.ShapeDtypeStruct(s, F32) for s in out_shapes),
        in_specs=[ANY_SPEC] * n,
        out_specs=(ANY_SPEC,) * n,
        scratch_shapes=[pltpu.SemaphoreType.DMA((n_copies,)), pltpu.SemaphoreType.DMA((n_copies,))],
    )(*srcs)


def _pair_add(name, coords, grid, specs_a, specs_b, out_specs, out_shapes, a_arrays, b_arrays):
    n = len(a_arrays)

    def body(co_ref, *refs):
        for k in range(n):
            refs[2 * n + k][...] = refs[k][...] + refs[n + k][...]

    return pl.pallas_call(
        body,
        name=name,
        grid_spec=pltpu.PrefetchScalarGridSpec(num_scalar_prefetch=1, grid=grid, in_specs=specs_a + specs_b,
                                               out_specs=out_specs),
        out_shape=tuple(jax.ShapeDtypeStruct(s, F32) for s in out_shapes),
        compiler_params=pltpu.CompilerParams(dimension_semantics=("arbitrary",) * len(grid),
                                             vmem_limit_bytes=VMEM_LIMIT_V7X),
    )(coords, *a_arrays, *b_arrays)


def _final_add_adamw(coords, s1, r, ws, ms, vs, n_split=4):
    n = len(s1)

    def body(co_ref, *refs):
        for k in range(n):
            s_ref, r_ref, w_ref, m_ref, v_ref = (refs[q * n + k] for q in range(5))
            g_ref, d_ref, nm_ref, nv_ref = (refs[(5 + q) * n + k] for q in range(4))
            g = ((s_ref[...] + r_ref[0]) + r_ref[1]) + r_ref[2]
            g_ref[...] = g
            delta, m, v = _adamw(w_ref[...], g, m_ref[...], v_ref[...])
            d_ref[...] = delta
            nm_ref[...] = m
            nv_ref[...] = v

    def shard_spec(a):
        rows, cols = a.shape
        return pl.BlockSpec((rows // n_split, cols), lambda i, co: (i, 0))

    def mine_spec(a):
        rows, cols = a.shape[2:]
        return pl.BlockSpec((None, None, rows // n_split, cols), lambda i, co: (co[0], co[1], i, 0))

    def recv_spec(a):
        rows, cols = a.shape[1:]
        return pl.BlockSpec((3, rows // n_split, cols), lambda i, co: (0, i, 0))

    in_specs = ([mine_spec(a) for a in s1] + [recv_spec(a) for a in r] + [shard_spec(a) for a in ws] * 3)
    out_specs = [shard_spec(a) for a in ws] * 4
    outs = pl.pallas_call(
        body,
        name="grad_final_adamw",
        grid_spec=pltpu.PrefetchScalarGridSpec(num_scalar_prefetch=1, grid=(n_split,), in_specs=in_specs,
                                               out_specs=out_specs),
        out_shape=tuple(jax.ShapeDtypeStruct(a.shape, F32) for a in ws) * 4,
        compiler_params=pltpu.CompilerParams(dimension_semantics=("arbitrary",), vmem_limit_bytes=VMEM_LIMIT_V7X),
    )(coords, *s1, *r, *ws, *ms, *vs)
    return [tuple(outs[q * n + k] for q in range(4)) for k in range(n)]


def _sibling_sum(tag, grads, coords, n_split=4):
    n = len(grads)
    shapes = [g.shape[1:] for g in grads]
    g5 = [g.reshape(2, 2, 2, *s) for g, s in zip(grads, shapes)]

    def plan(x, y, c):
        return [(a, (xs, ys, 1 - c), (xs, ys), (x, y, 1 - c)) for a in range(n) for xs in range(2) for ys in range(2)]

    r1 = _exchange("grad_swap_core_" + tag, g5, [(2, 2, *s) for s in shapes], plan)
    spec_g = [pl.BlockSpec((None, None, None, s[0] // n_split, s[1]), lambda i, j, k, co: (i, j, co[2], k, 0))
              for s in shapes]
    spec_r = [pl.BlockSpec((None, None, s[0] // n_split, s[1]), lambda i, j, k, co: (i, j, k, 0)) for s in shapes]
    return list(_pair_add("grad_add_core_" + tag, coords, (2, 2, n_split), spec_g, spec_r, spec_r,
                          [(2, 2, *s) for s in shapes], g5, list(r1)))


def _chip_exchange(tag, s1):
    n = len(s1)

    def body(*refs):
        steps = _ChipExchangeSteps(refs[:n], refs[n:2 * n], *refs[2 * n:])
        steps.start()
        steps.finish()

    return list(pl.pallas_call(
        body,
        name="grad_swap_chips_" + tag,
        out_shape=tuple(jax.ShapeDtypeStruct((3, *s.shape[2:]), F32) for s in s1),
        in_specs=[ANY_SPEC] * n,
        out_specs=(ANY_SPEC,) * n,
        scratch_shapes=[pltpu.SemaphoreType.DMA((3 * n,)), pltpu.SemaphoreType.DMA((3 * n,))],
    )(*s1))


def _small_allreduce_adamw(partials, ws, ms, vs, pick_mine):
    n = len(partials)

    def body(*refs):
        p_in = refs[:n]
        w_in, m_in, v_in = refs[n:2 * n], refs[2 * n:3 * n], refs[3 * n:4 * n]
        g_out, d_out, nm_out, nv_out = (refs[(4 + q) * n:(5 + q) * n] for q in range(4))
        acc = refs[8 * n:9 * n]
        rbuf = refs[9 * n:10 * n]
        send_sems, recv_sems = refs[10 * n:]
        x, y, c = _coords()
        me = 4 * x + 2 * y + c
        for a in range(n):
            acc[a][...] = p_in[a][...]
        for ph, peer in enumerate([(x, y, 1 - c), (1 - x, y, c), (x, 1 - y, c)]):
            copies = [
                pltpu.make_async_remote_copy(acc[a], rbuf[a].at[ph], send_sems.at[ph, a], recv_sems.at[ph, a],
                                             device_id=peer, device_id_type=MESH)
                for a in range(n)
            ]
            for cp in copies:
                cp.start()
            for cp in copies:
                cp.wait()
            for a in range(n):
                acc[a][...] = acc[a][...] + rbuf[a][ph]
        for a in range(n):
            g = acc[a][me] if pick_mine[a] else acc[a][...]
            g_out[a][...] = g
            delta, m, v = _adamw(w_in[a][...], g, m_in[a][...], v_in[a][...])
            d_out[a][...] = delta
            nm_out[a][...] = m
            nv_out[a][...] = v

    w_shapes = tuple(jax.ShapeDtypeStruct(w.shape, F32) for w in ws)
    outs = pl.pallas_call(
        body,
        name="small_allreduce_adamw",
        out_shape=w_shapes * 4,
        in_specs=[VMEM_SPEC] * (4 * n),
        out_specs=(VMEM_SPEC,) * (4 * n),
        scratch_shapes=[pltpu.VMEM(p.shape, F32) for p in partials]
        + [pltpu.VMEM((3, *p.shape), F32) for p in partials]
        + [pltpu.SemaphoreType.DMA((3, n)), pltpu.SemaphoreType.DMA((3, n))],
        compiler_params=pltpu.CompilerParams(vmem_limit_bytes=VMEM_LIMIT_V7X),
    )(*partials, *ws, *ms, *vs)
    return [tuple(outs[q * n + k] for q in range(4)) for k in range(n)]


def kernel(x, c, w_ada, b_ada, pre_mix_g, post_mix_g, w_in, sgu_norm_g, w_spatial, b_spatial, w_pool, pool_scale, w_out, pre_ffn_g, post_ffn_g, w_up, conv_w, conv_b, w_down, loss_target, m_w_ada, m_b_ada, m_pre_mix_g, m_post_mix_g, m_w_in, m_sgu_norm_g, m_w_spatial, m_b_spatial, m_w_pool, m_pool_scale, m_w_out, m_pre_ffn_g, m_post_ffn_g, m_w_up, m_conv_w, m_conv_b, m_w_down, v_w_ada, v_b_ada, v_pre_mix_g, v_post_mix_g, v_w_in, v_sgu_norm_g, v_w_spatial, v_b_spatial, v_w_pool, v_pool_scale, v_w_out, v_pre_ffn_g, v_post_ffn_g, v_w_up, v_conv_w, v_conv_b, v_w_down):
    t_len = x.shape[1]
    ts = min(256, t_len)
    ts_w = min(1024, t_len)
    coords = jnp.stack([lax.axis_index("x"), lax.axis_index("y"), lax.axis_index("c")]).astype(jnp.int32)

    mod3, scx = _adaln_fwd(c, w_ada[0], b_ada.reshape(N_DEV, 1, MOD_COLS))
    mod = mod3.reshape(N_MOD, D)

    g_in, g_out = _allgather_shards([w_in[0], w_out[0]], [BF16, BF16])
    w_in_b = g_in.transpose(1, 0, 2).reshape(D, IN_WIDTH)
    w_out_b = g_out.reshape(D, D)
    conv_b8 = conv_b.reshape(N_DEV, FF_CHUNK)
    b_sp_t = b_spatial[0].T

    x2d, tgt = x[0], loss_target[0]
    (x1, proj, mixed), (g_up, g_down, g_cw) = _mix_fwd(
        x2d, mod, pre_mix_g, post_mix_g, w_in_b, sgu_norm_g[0], w_spatial[0], b_sp_t, w_pool[0], pool_scale, w_out_b,
        ts, [w_up[0], w_down[0], conv_w[0]], [BF16, BF16, F32])
    w_down_b = g_down.reshape(FF, D)
    up, f, dx2, loss_lanes = _ffn_fwd(x1, tgt, mod, pre_ffn_g, post_ffn_g, g_up, g_cw, conv_b8, w_down_b, ts)
    loss = lax.psum(0.5 * jnp.sum(loss_lanes) / D, ("x", "y", "c"))

    (dx1, dup, act, df, h2, dmod_f, d_pre_ffn, d_post_ffn, d_cb8, d_cw8) = _ffn_bwd(
        dx2, f, x1, up, mod, pre_ffn_g, post_ffn_g, g_up, g_cw, conv_b8, w_down_b, ts)
    gw_up = _wgrad_up(h2, dup, ts_w)
    gw_down = _wgrad_down(act, df, ts_w).reshape(N_DEV, FF // N_DEV, D)
    s1_ffn = _sibling_sum("ffn", [gw_up, gw_down], coords)
    ((grad_x, gw_in, gw_out, dmod_m, d_pre_mix, d_post_mix, d_sgn, d_wsp, d_bsp_t, d_wpool, d_ps), r_ffn) = _mix_bwd(
        dx1, x2d, proj, mixed, mod, pre_mix_g, post_mix_g, w_in_b, sgu_norm_g[0], w_spatial[0], b_sp_t,
        w_pool[0], pool_scale, w_out_b, ts, s1_ffn)
    gw_in = gw_in.reshape(D, N_DEV, IN_WIDTH // N_DEV).transpose(1, 0, 2)
    gw_out = gw_out.reshape(N_DEV, D // N_DEV, D)
    s1_mix = _sibling_sum("mix", [gw_in, gw_out], coords)
    r_mix = _chip_exchange("mix", s1_mix)

    big = _final_add_adamw(
        coords, s1_mix + s1_ffn, r_mix + list(r_ffn),
        [w_in[0], w_out[0], w_up[0], w_down[0]], [m_w_in[0], m_w_out[0], m_w_up[0], m_w_down[0]],
        [v_w_in[0], v_w_out[0], v_w_up[0], v_w_down[0]])
    r_in, r_out, r_up, r_down = [tuple(a[None] for a in four) for four in big]

    dmod = jnp.concatenate([dmod_m, dmod_f], axis=0)
    r_ada = tuple(a[None] for a in _adaln_bwd(dmod.reshape(N_DEV, 1, MOD_COLS), scx, w_ada[0], m_w_ada[0], v_w_ada[0]))

    names = ["b_ada", "pre_mix_g", "post_mix_g", "sgu_norm_g", "w_spatial", "b_spatial", "w_pool", "pool_scale",
             "pre_ffn_g", "post_ffn_g", "conv_w", "conv_b"]
    partials = [dmod.reshape(1, N_MOD * D), d_pre_mix, d_post_mix, d_sgn, d_wsp, d_bsp_t.T, d_wpool, d_ps,
                d_pre_ffn, d_post_ffn, d_cw8, d_cb8.reshape(1, 2 * FF)]
    small_w = [b_ada, pre_mix_g, post_mix_g, sgu_norm_g[0], w_spatial[0], b_spatial[0], w_pool[0], pool_scale,
               pre_ffn_g, post_ffn_g, conv_w[0], conv_b]
    small_m = [m_b_ada, m_pre_mix_g, m_post_mix_g, m_sgu_norm_g[0], m_w_spatial[0], m_b_spatial[0], m_w_pool[0],
               m_pool_scale, m_pre_ffn_g, m_post_ffn_g, m_conv_w[0], m_conv_b]
    small_v = [v_b_ada, v_pre_mix_g, v_post_mix_g, v_sgu_norm_g[0], v_w_spatial[0], v_b_spatial[0], v_w_pool[0],
               v_pool_scale, v_pre_ffn_g, v_post_ffn_g, v_conv_w[0], v_conv_b]
    small = _small_allreduce_adamw(partials, small_w, small_m, small_v, [nm == "conv_w" for nm in names])
    lead = {"sgu_norm_g", "w_spatial", "b_spatial", "w_pool", "conv_w"}
    res = {nm: tuple(a[None] if nm in lead else a for a in four) for nm, four in zip(names, small)}
    res.update(w_ada=r_ada, w_in=r_in, w_out=r_out, w_up=r_up, w_down=r_down)

    order = ["w_ada", "b_ada", "pre_mix_g", "post_mix_g", "w_in", "sgu_norm_g", "w_spatial", "b_spatial", "w_pool",
             "pool_scale", "w_out", "pre_ffn_g", "post_ffn_g", "w_up", "conv_w", "conv_b", "w_down"]
    return (loss, grad_x[None], *[res[nm][0] for nm in order], *[res[nm][1] for nm in order],
            *[res[nm][2] for nm in order], *[res[nm][3] for nm in order])
```

```python
import functools
import math

import jax
import jax.numpy as jnp
from jax import lax
from jax.experimental import pallas as pl
from jax.experimental.pallas import tpu as pltpu

F32 = jnp.float32
BF16 = jnp.bfloat16
MESH = pl.DeviceIdType.MESH

EPS = 1e-6
D = 1024
HEAD = 128
N_HEAD = 4
A_WIDTH = 512
B_WIDTH = 512
IN_WIDTH = 1536
WINDOWS = (2, 4, 8, 16)
CHUNK = 64
FF = 2816
N_DEV = 8
FF_CHUNK = 704
N_MOD = 6
MOD_COLS = 768

ADAM_LR = 0.001
ADAM_B1 = 0.9
ADAM_B2 = 0.999
ADAM_EPS = 1e-08
ADAM_WD = 0.01
ADAM_STEP = 10

VMEM_LIMIT_V7X = 56 * 1024 * 1024
HALO = 8
POOL_HALO = 16

VMEM_SPEC = pl.BlockSpec(memory_space=pltpu.VMEM)
ANY_SPEC = pl.BlockSpec(memory_space=pl.ANY)


def _bf(x):
    return x.astype(BF16)


def _mm(a, b):
    return jnp.dot(a, b, preferred_element_type=F32)


def _mm_nt(a, b):
    return lax.dot_general(a, b, (((1,), (1,)), ((), ())), preferred_element_type=F32)


def _mm_tn(a, b):
    return lax.dot_general(a, b, (((0,), (0,)), ((), ())), preferred_element_type=F32)


def _rstd(x):
    return lax.rsqrt(jnp.mean(x * x, axis=-1, keepdims=True) + EPS)


def _sum0(x):
    return jnp.sum(x, axis=0, keepdims=True)


def _rowmean(x):
    return jnp.mean(x, axis=-1, keepdims=True)


_GELU_K = math.sqrt(2.0 / math.pi)


def _gelu_and_grad(x):
    x2 = x * x
    th = jnp.tanh(_GELU_K * (x + 0.044715 * (x * x2)))
    cdf = 0.5 * (1.0 + th)
    grad = cdf + 0.5 * x * (1.0 - th * th) * (_GELU_K * (1.0 + 3.0 * 0.044715 * x2))
    return x * cdf, grad


def _gelu(x):
    return x * (0.5 * (1.0 + jnp.tanh(_GELU_K * (x + 0.044715 * (x * x * x)))))


def _sigmoid(x):
    return 0.5 * jnp.tanh(0.5 * x) + 0.5


def _sgu_mask():
    ri = lax.broadcasted_iota(jnp.int32, (HEAD, HEAD), 0)
    ci = lax.broadcasted_iota(jnp.int32, (HEAD, HEAD), 1)
    return (ci // CHUNK) <= (ri // CHUNK)


def _window_sum(ext, w, trailing):
    n = ext.shape[0]
    s, k = ext, 1
    while k < w:
        s = s + pltpu.roll(s, k if trailing else n - k, 0)
        k *= 2
    return s


def _inv_count(row0, n, w):
    t = row0 + lax.broadcasted_iota(jnp.int32, (n, 1), 0)
    return 1.0 / jnp.minimum(t + 1, w).astype(F32)


def _shift_down(v, before, k):
    rows = lax.broadcasted_iota(jnp.int32, before.shape, 0)
    r = pltpu.roll(v, k, 0)
    top = jnp.where(rows < k, pltpu.roll(before, k, 0), r[0:HALO])
    return jnp.concatenate([top, r[HALO:]], axis=0)


def _shift_up(v, after, k):
    n = v.shape[0]
    rows = lax.broadcasted_iota(jnp.int32, after.shape, 0)
    r = pltpu.roll(v, n - k, 0)
    bottom = jnp.where(rows >= HALO - k, pltpu.roll(after, HALO - k, 0), r[n - HALO:])
    return jnp.concatenate([r[:n - HALO], bottom], axis=0)


def _adamw(w, g, m, v):
    m = ADAM_B1 * m + (1.0 - ADAM_B1) * g
    v = ADAM_B2 * v + (1.0 - ADAM_B2) * (g * g)
    m_hat = m / (1.0 - ADAM_B1 ** ADAM_STEP)
    v_hat = v / (1.0 - ADAM_B2 ** ADAM_STEP)
    delta = -ADAM_LR * (m_hat / (jnp.sqrt(v_hat) + ADAM_EPS) + ADAM_WD * w)
    return delta, m, v


def _coords():
    return lax.axis_index("x"), lax.axis_index("y"), lax.axis_index("c")


def _peer(k):
    x, y, c = _coords()
    return (x ^ ((k >> 2) & 1), y ^ ((k >> 1) & 1), c ^ (k & 1))


def _my_index():
    x, y, c = _coords()
    return 4 * x + 2 * y + c


def _adaln_modulation(c_ref, w_ref, b_ref, mod_ref, scx_ref, scbuf, stage, recv, send_sems, recv_sems):
    me = _my_index()
    cv = c_ref[...]
    scbuf[0] = cv * _sigmoid(cv)
    first = [
        pltpu.make_async_remote_copy(scbuf.at[0], scbuf.at[k], send_sems.at[0, k], recv_sems.at[0, k],
                                     device_id=_peer(k), device_id_type=MESH)
        for k in range(1, N_DEV)
    ]
    for cp in first:
        cp.start()
    for cp in first:
        cp.wait()
    scx_ref[...] = jnp.zeros(scx_ref.shape, F32)
    for k in range(N_DEV):
        scx_ref[k:k + 1, :] = scbuf[k]
    prod = _mm(_bf(scx_ref[...]), _bf(w_ref[...]))
    for k in range(N_DEV):
        stage[k] = prod[k:k + 1, :] + b_ref[me]
    second = [
        pltpu.make_async_remote_copy(stage.at[k], recv.at[k], send_sems.at[1, k], recv_sems.at[1, k],
                                     device_id=_peer(k), device_id_type=MESH)
        for k in range(1, N_DEV)
    ]
    for cp in second:
        cp.start()
    mod_ref[me] = stage[0]
    for cp in second:
        cp.wait()
    for k in range(1, N_DEV):
        mod_ref[me ^ k] = recv[k]


class _GatherSteps:
    def __init__(self, ins, outs, stages, send_sems, recv_sems, local_sems):
        self.ins, self.outs, self.stages = ins, outs, stages
        self.send_sems, self.recv_sems, self.local_sems = send_sems, recv_sems, local_sems
        x, y, c = _coords()
        self.c = c
        self.me, self.sibling = (x, y, c), (x, y, 1 - c)
        self.chips = [(1 - x, y), (x, 1 - y), (1 - x, 1 - y)]

    def _copy(self, a, k, block, to, from_stage=False):
        dst = self.outs[a].at[4 * block[0] + 2 * block[1] + block[2]]
        return pltpu.make_async_remote_copy(self.stages[a] if from_stage else dst, dst, self.send_sems.at[a, k],
                                            self.recv_sems.at[a, k], device_id=to, device_id_type=MESH)

    def _local(self, a):
        me = self.me
        return pltpu.make_async_copy(self.stages[a], self.outs[a].at[4 * me[0] + 2 * me[1] + me[2]],
                                     self.local_sems.at[a])

    def _first(self, a):
        cps = [self._copy(a, 0, self.me, self.sibling, from_stage=True)]
        return cps + [self._copy(a, 1 + j, self.me, (*chip, self.c), from_stage=True)
                      for j, chip in enumerate(self.chips)]

    def _passed(self, a, j):
        return self._copy(a, 4 + j, (*self.chips[j], self.c), self.sibling)

    def start(self):
        for a in range(len(self.ins)):
            self.stages[a][...] = self.ins[a][...].astype(self.stages[a].dtype)
            self._local(a).start()
            for cp in self._first(a):
                cp.start()

    def forward(self):
        for a in range(len(self.ins)):
            for j, chip in enumerate(self.chips):
                self._copy(a, 1 + j, (*chip, self.c), self.me).wait_recv()
                self._passed(a, j).start()

    def finish(self):
        for a in range(len(self.ins)):
            self._copy(a, 0, self.sibling, self.me).wait_recv()
            for j, chip in enumerate(self.chips):
                self._copy(a, 4 + j, (*chip, 1 - self.c), self.me).wait_recv()
            for cp in self._first(a) + [self._passed(a, j) for j in range(3)]:
                cp.wait_send()
            self._local(a).wait()


def _gather_scratch(shards, out_dtypes):
    n = len(shards)
    return ([pltpu.VMEM(s.shape, dt) for s, dt in zip(shards, out_dtypes)]
            + [pltpu.SemaphoreType.DMA((n, 7)), pltpu.SemaphoreType.DMA((n, 7)), pltpu.SemaphoreType.DMA((n,))])


def _gather_out_shapes(shards, out_dtypes):
    return tuple(jax.ShapeDtypeStruct((N_DEV, *s.shape), dt) for s, dt in zip(shards, out_dtypes))


def _prologue(c_row, w_ada, b_ada3, shards, out_dtypes):
    n = len(shards)

    def body(*refs):
        c_ref, w_ref, b_ref = refs[:3]
        mod_ref, scx_ref = refs[3 + n:5 + n]
        gather = _GatherSteps(refs[3:3 + n], refs[5 + n:5 + 2 * n], refs[5 + 2 * n:5 + 3 * n],
                              *refs[5 + 3 * n:8 + 3 * n])
        gather.start()
        _adaln_modulation(c_ref, w_ref, b_ref, mod_ref, scx_ref, *refs[8 + 3 * n:])
        gather.forward()
        gather.finish()

    outs = pl.pallas_call(
        body,
        name="prologue",
        out_shape=(jax.ShapeDtypeStruct((N_DEV, 1, MOD_COLS), F32), jax.ShapeDtypeStruct((2 * N_DEV, D), F32))
        + _gather_out_shapes(shards, out_dtypes),
        in_specs=[VMEM_SPEC] * (3 + n),
        out_specs=(VMEM_SPEC, VMEM_SPEC) + (ANY_SPEC,) * n,
        scratch_shapes=_gather_scratch(shards, out_dtypes) + [
            pltpu.VMEM((N_DEV, 1, D), F32),
            pltpu.VMEM((N_DEV, 1, MOD_COLS), F32),
            pltpu.VMEM((N_DEV, 1, MOD_COLS), F32),
            pltpu.SemaphoreType.DMA((2, N_DEV)),
            pltpu.SemaphoreType.DMA((2, N_DEV)),
        ],
        compiler_params=pltpu.CompilerParams(vmem_limit_bytes=VMEM_LIMIT_V7X),
    )(c_row, w_ada, b_ada3, *shards)
    return outs[0], outs[1], outs[2:]


class _ChipExchangeSteps:
    FLIPS = ((1, 0), (0, 1), (1, 1))

    def __init__(self, srcs, dsts, send_sems, recv_sems):
        self.srcs, self.dsts, self.send_sems, self.recv_sems = srcs, dsts, send_sems, recv_sems

    def _copies(self):
        x, y, c = _coords()
        out = []
        for a in range(len(self.srcs)):
            for j, (fx, fy) in enumerate(self.FLIPS):
                k = 3 * a + j
                out.append(pltpu.make_async_remote_copy(
                    self.srcs[a].at[x ^ fx, y ^ fy], self.dsts[a].at[j], self.send_sems.at[k], self.recv_sems.at[k],
                    device_id=(x ^ fx, y ^ fy, c), device_id_type=MESH))
        return out

    def start(self):
        for cp in self._copies():
            cp.start()

    def finish(self):
        for cp in self._copies():
            cp.wait()


def _mix_fwd(x, mod, g_pre, g_post, w_in_b, sgn, w_sp, b_sp_t, w_pool, p_scale, w_out_b, ts, shards, shard_dtypes):
    t_len = x.shape[0]
    nt, nb = t_len // ts, ts // HEAD
    ns = len(shards)

    def body(*refs):
        (x_ref, mod_ref, g1_ref, g2_ref, win_ref, sgn_ref, ws_ref, bst_ref, wp_ref, ps_ref, wout_ref) = refs[:11]
        x1_ref, proj_ref, mixed_ref = refs[11 + ns:14 + ns]
        pbuf, cat = refs[14 + 2 * ns:16 + 2 * ns]
        gather = _GatherSteps(refs[11:11 + ns], refs[14 + ns:14 + 2 * ns], refs[16 + 2 * ns:16 + 3 * ns],
                              *refs[16 + 3 * ns:])
        i = pl.program_id(0)

        @pl.when(i == 0)
        def _():
            pbuf[0:POOL_HALO, :] = jnp.zeros((POOL_HALO, B_WIDTH), F32)
            gather.start()

        @pl.when(i == (3 * nt) // 4)
        def _():
            gather.forward()

        xv = x_ref[...]
        sh, sc, gm = mod_ref[0:1, :], mod_ref[1:2, :], mod_ref[2:3, :]
        h = (xv * _rstd(xv) * g1_ref[...]) * (1.0 + sc) + sh
        proj_ref[...] = _mm(_bf(h), win_ref[...])
        pbuf[POOL_HALO:POOL_HALO + ts, :] = proj_ref[:, 2 * A_WIDTH:]
        smask = _sgu_mask()
        for hd in range(N_HEAD):
            u = _gelu(proj_ref[:, hd * HEAD:(hd + 1) * HEAD])
            v = _gelu(proj_ref[:, A_WIDTH + hd * HEAD:A_WIDTH + (hd + 1) * HEAD])
            vn = _bf(v * _rstd(v) * sgn_ref[hd:hd + 1, :])
            wm = _bf(jnp.where(smask, ws_ref[hd], 0.0))
            bias = bst_ref[:, hd:hd + 1]
            for b in range(nb):
                rows = slice(b * HEAD, (b + 1) * HEAD)
                z = _mm(wm, vn[rows]) + bias
                cat[rows, hd * HEAD:(hd + 1) * HEAD] = _bf(u[rows] * z)
        for g, w in enumerate(WINDOWS):
            cols = slice(g * HEAD, (g + 1) * HEAD)
            ext = pbuf[:, cols]
            pooled = _window_sum(ext, w, True)[POOL_HALO:] * _inv_count(i * ts, ts, w) - ext[POOL_HALO:]
            cat[:, A_WIDTH + g * HEAD:A_WIDTH + (g + 1) * HEAD] = _bf(_mm(_bf(pooled), _bf(wp_ref[g])) * ps_ref[:, cols])
        pbuf[0:POOL_HALO, :] = pbuf[ts:ts + POOL_HALO, :]
        mixed = _mm(cat[...], wout_ref[...])
        mixed_ref[...] = mixed
        x1_ref[...] = xv + gm * (mixed * _rstd(mixed) * g2_ref[...])

        @pl.when(i == nt - 1)
        def _():
            gather.finish()

    tile = lambda wid: pl.BlockSpec((ts, wid), lambda i: (i, 0))
    outs = pl.pallas_call(
        body,
        name="mix_fwd",
        grid=(nt,),
        out_shape=(jax.ShapeDtypeStruct((t_len, D), F32), jax.ShapeDtypeStruct((t_len, IN_WIDTH), F32),
                   jax.ShapeDtypeStruct((t_len, D), F32)) + _gather_out_shapes(shards, shard_dtypes),
        in_specs=[tile(D)] + [VMEM_SPEC] * (10 + ns),
        out_specs=(tile(D), tile(IN_WIDTH), tile(D)) + (ANY_SPEC,) * ns,
        scratch_shapes=[pltpu.VMEM((POOL_HALO + ts, B_WIDTH), F32), pltpu.VMEM((ts, D), BF16)]
        + _gather_scratch(shards, shard_dtypes),
        compiler_params=pltpu.CompilerParams(dimension_semantics=("arbitrary",), vmem_limit_bytes=VMEM_LIMIT_V7X),
    )(x, mod, g_pre, g_post, w_in_b, sgn, w_sp, b_sp_t, w_pool, p_scale, w_out_b, *shards)
    return outs[:3], outs[3:]


def _ffn_fwd(x1, target, mod, g_pre, g_post, w_up_b, conv_w8, conv_b8, w_down_b, ts):
    t_len = x1.shape[0]
    nt = t_len // ts

    def body(x1_ref, tgt_ref, mod_ref, g3_ref, g4_ref, wup_ref, cw_ref, cb_ref, wdown_ref,
             up_ref, f_ref, dx2_ref, loss_ref, ucarry):
        i = pl.program_id(0)

        @pl.when(i == 0)
        def _():
            ucarry[...] = jnp.zeros(ucarry.shape, F32)
            loss_ref[...] = jnp.zeros(loss_ref.shape, F32)

        x1v = x1_ref[...]
        sh, sc, gf = mod_ref[3:4, :], mod_ref[4:5, :], mod_ref[5:6, :]
        h2 = _bf((x1v * _rstd(x1v) * g3_ref[...]) * (1.0 + sc) + sh)
        half = N_DEV // 2

        def up_pair(j):
            return [_mm(h2, wup_ref[jj]) for jj in (j, j + half)]

        f = jnp.zeros((ts, D), F32)
        ups = up_pair(0)
        for j in range(half):
            nxt = up_pair(j + 1) if j + 1 < half else None
            ys = []
            for up, jj in zip(ups, (j, j + half)):
                up_ref[jj] = up
                before = ucarry[jj]
                ucarry[jj] = up[ts - HALO:, :]
                cw = cw_ref[jj]
                ys.append(cb_ref[jj:jj + 1, :] + _shift_down(up, before, 2) * cw[0:1, :]
                          + _shift_down(up, before, 1) * cw[1:2, :] + up * cw[2:3, :])
            gate, val = ys
            act = gate * _sigmoid(gate) * val
            f = f + _mm(_bf(act), wdown_ref[j * FF_CHUNK:(j + 1) * FF_CHUNK, :])
            ups = nxt
        f_ref[...] = f
        x2 = x1v + gf * (f * _rstd(f) * g4_ref[...])
        err = x2 - tgt_ref[...]
        loss_ref[...] += _sum0(err * err)
        dx2_ref[...] = err * (1.0 / D)

    tile = pl.BlockSpec((ts, D), lambda i: (i, 0))
    return pl.pallas_call(
        body,
        name="ffn_fwd",
        grid=(nt,),
        out_shape=(jax.ShapeDtypeStruct((N_DEV, t_len, FF_CHUNK), F32), jax.ShapeDtypeStruct((t_len, D), F32),
                   jax.ShapeDtypeStruct((t_len, D), F32), jax.ShapeDtypeStruct((1, D), F32)),
        in_specs=[tile, tile] + [VMEM_SPEC] * 7,
        out_specs=(pl.BlockSpec((N_DEV, ts, FF_CHUNK), lambda i: (0, i, 0)), tile, tile,
                   pl.BlockSpec((1, D), lambda i: (0, 0))),
        scratch_shapes=[pltpu.VMEM((N_DEV, HALO, FF_CHUNK), F32)],
        compiler_params=pltpu.CompilerParams(dimension_semantics=("arbitrary",), vmem_limit_bytes=VMEM_LIMIT_V7X),
    )(x1, target, mod, g_pre, g_post, w_up_b, conv_w8, conv_b8, w_down_b)


def _ffn_bwd(dx2, f, x1, up, mod, g_pre, g_post, w_up_b, conv_w8, conv_b8, w_down_b, ts):
    t_len = x1.shape[0]
    nt = t_len // ts
    half = N_DEV // 2

    def body(dx2_ref, f_ref, x1_ref, up_ref, halo_ref, mod_ref, g3_ref, g4_ref, wup_ref, cw_ref, cb_ref, wdown_ref,
             dx1_ref, dup_ref, act_ref, df_ref, h2_ref, dmod_ref, dg3_ref, dg4_ref, dcb_ref, dcw_ref,
             dycarry, dh2acc):
        i = pl.program_id(0)
        r = nt - 1 - i

        @pl.when(i == 0)
        def _():
            for ref in (dmod_ref, dg3_ref, dg4_ref, dcb_ref, dcw_ref, dycarry):
                ref[...] = jnp.zeros(ref.shape, F32)

        dx2v, fv, x1v = dx2_ref[...], f_ref[...], x1_ref[...]
        sh, sc, gf = mod_ref[3:4, :], mod_ref[4:5, :], mod_ref[5:6, :]
        g3, g4 = g3_ref[...], g4_ref[...]
        rstd4 = _rstd(fv)
        fh = fv * rstd4
        dmod_ref[2:3, :] += _sum0(dx2v * (fh * g4))
        dr = dx2v * gf
        dg4_ref[...] += _sum0(dr * fh)
        dfh = dr * g4
        dfb = _bf(rstd4 * (dfh - fh * _rowmean(dfh * fh)))
        df_ref[...] = dfb
        rstd3 = _rstd(x1v)
        xh = x1v * rstd3
        n3 = xh * g3
        h2_ref[...] = _bf(n3 * (1.0 + sc) + sh)
        dh2acc[...] = jnp.zeros((ts, D), F32)
        keep = jnp.where(r > 0, 1.0, 0.0).astype(F32)

        def dact_of(j):
            return _mm_nt(dfb, wdown_ref[j * FF_CHUNK:(j + 1) * FF_CHUNK, :])

        dact_next = dact_of(0)
        for j in range(half):
            dact = dact_next
            if j + 1 < half:
                dact_next = dact_of(j + 1)
            ys = []
            for jj in (j, j + half):
                before = halo_ref[jj] * keep
                upc = up_ref[jj]
                cw = cw_ref[jj]
                ys.append(cb_ref[jj:jj + 1, :] + _shift_down(upc, before, 2) * cw[0:1, :]
                          + _shift_down(upc, before, 1) * cw[1:2, :] + upc * cw[2:3, :])
            gate, val = ys
            sg = _sigmoid(gate)
            silu = gate * sg
            act_ref[j] = _bf(silu * val)
            dys = (dact * val * (sg * (1.0 + gate * (1.0 - sg))), dact * silu)
            for q, jj in enumerate((j, j + half)):
                dy = dys[q]
                cw = cw_ref[jj]
                dcb_ref[jj:jj + 1, :] += _sum0(dy)
                after = dycarry[jj]
                dycarry[jj] = dy[0:HALO, :]
                dy1, dy2 = _shift_up(dy, after, 1), _shift_up(dy, after, 2)
                upc = up_ref[jj]
                dcw_ref[jj, 0:1, :] += _sum0(dy2 * upc)
                dcw_ref[jj, 1:2, :] += _sum0(dy1 * upc)
                dcw_ref[jj, 2:3, :] += _sum0(dy * upc)
                dup = _bf(dy * cw[2:3, :] + dy1 * cw[1:2, :] + dy2 * cw[0:1, :])
                dup_ref[jj] = dup
                dh2acc[...] += _mm_nt(dup, wup_ref[jj])
        dh2 = dh2acc[...]
        dmod_ref[0:1, :] += _sum0(dh2)
        dmod_ref[1:2, :] += _sum0(dh2 * n3)
        dn3 = dh2 * (1.0 + sc)
        dg3_ref[...] += _sum0(dn3 * xh)
        dxh = dn3 * g3
        dx1_ref[...] = dx2v + rstd3 * (dxh - xh * _rowmean(dxh * xh))

    tile = pl.BlockSpec((ts, D), lambda i: (nt - 1 - i, 0))
    chunked = lambda n: pl.BlockSpec((n, ts, FF_CHUNK), lambda i: (0, nt - 1 - i, 0))
    halo = pl.BlockSpec((N_DEV, HALO, FF_CHUNK), lambda i: (0, jnp.maximum((nt - 1 - i) * (ts // HALO) - 1, 0), 0))
    const = lambda *shape: pl.BlockSpec(shape, lambda i: (0,) * len(shape))
    return pl.pallas_call(
        body,
        name="ffn_bwd",
        grid=(nt,),
        out_shape=(jax.ShapeDtypeStruct((t_len, D), F32), jax.ShapeDtypeStruct((N_DEV, t_len, FF_CHUNK), BF16),
                   jax.ShapeDtypeStruct((half, t_len, FF_CHUNK), BF16), jax.ShapeDtypeStruct((t_len, D), BF16),
                   jax.ShapeDtypeStruct((t_len, D), BF16), jax.ShapeDtypeStruct((3, D), F32),
                   jax.ShapeDtypeStruct((1, D), F32), jax.ShapeDtypeStruct((1, D), F32),
                   jax.ShapeDtypeStruct((N_DEV, FF_CHUNK), F32), jax.ShapeDtypeStruct((N_DEV, 3, FF_CHUNK), F32)),
        in_specs=[tile, tile, tile, chunked(N_DEV), halo] + [VMEM_SPEC] * 7,
        out_specs=(tile, chunked(N_DEV), chunked(half), tile, tile, const(3, D), const(1, D), const(1, D),
                   const(N_DEV, FF_CHUNK), const(N_DEV, 3, FF_CHUNK)),
        scratch_shapes=[pltpu.VMEM((N_DEV, HALO, FF_CHUNK), F32), pltpu.VMEM((ts, D), F32)],
        compiler_params=pltpu.CompilerParams(dimension_semantics=("arbitrary",), vmem_limit_bytes=VMEM_LIMIT_V7X),
    )(dx2, f, x1, up, up, mod, g_pre, g_post, w_up_b, conv_w8, conv_b8, w_down_b)


def _wgrad_up(h2, dup, ts):
    t_len = h2.shape[0]
    nt, half = t_len // ts, N_DEV // 2

    def body(h2_ref, dup_ref, out_ref):
        @pl.when(pl.program_id(1) == 0)
        def _():
            out_ref[...] = jnp.zeros(out_ref.shape, F32)

        for q in range(half):
            out_ref[q] += _mm_tn(h2_ref[...], dup_ref[q])

    return pl.pallas_call(
        body,
        name="wgrad_up",
        grid=(2, nt),
        out_shape=jax.ShapeDtypeStruct((N_DEV, D, FF_CHUNK), F32),
        in_specs=[pl.BlockSpec((ts, D), lambda g, t: (t, 0)), pl.BlockSpec((half, ts, FF_CHUNK), lambda g, t: (g, t, 0))],
        out_specs=pl.BlockSpec((half, D, FF_CHUNK), lambda g, t: (g, 0, 0)),
        compiler_params=pltpu.CompilerParams(dimension_semantics=("arbitrary", "arbitrary"),
                                             vmem_limit_bytes=VMEM_LIMIT_V7X),
    )(h2, dup)


def _wgrad_down(act, df, ts):
    t_len = df.shape[0]
    nt, half = t_len // ts, N_DEV // 2

    def body(act_ref, df_ref, out_ref):
        @pl.when(pl.program_id(0) == 0)
        def _():
            out_ref[...] = jnp.zeros(out_ref.shape, F32)

        for q in range(half):
            out_ref[q] += _mm_tn(act_ref[q], df_ref[...])

    return pl.pallas_call(
        body,
        name="wgrad_down",
        grid=(nt,),
        out_shape=jax.ShapeDtypeStruct((half, FF_CHUNK, D), F32),
        in_specs=[pl.BlockSpec((half, ts, FF_CHUNK), lambda t: (0, t, 0)), pl.BlockSpec((ts, D), lambda t: (t, 0))],
        out_specs=pl.BlockSpec((half, FF_CHUNK, D), lambda t: (0, 0, 0)),
        compiler_params=pltpu.CompilerParams(dimension_semantics=("arbitrary",), vmem_limit_bytes=VMEM_LIMIT_V7X),
    )(act, df)


def _mix_bwd(dx1, x, proj, mixed, mod, g_pre, g_post, w_in_b, sgn, w_sp, b_sp_t, w_pool, p_scale, w_out_b, ts, rs_srcs):
    t_len = x.shape[0]
    nt, nb = t_len // ts, ts // HEAD
    nr = len(rs_srcs)

    def body(*refs):
        (dx1_ref, x_ref, proj_ref, halo_ref, mixed_ref, mod_ref, g1_ref, g2_ref, win_ref, sgn_ref, ws_ref,
         bst_ref, wp_ref, ps_ref, wout_ref) = refs[:15]
        (gx_ref, dwin_ref, dwout_ref, dmod_ref, dg1_ref, dg2_ref, dsgn_ref, dws_ref, dbst_ref, dwp_ref,
         dps_ref) = refs[15 + nr:26 + nr]
        pbuf, dwsbuf, cat, dproj, dcat = refs[26 + 2 * nr:31 + 2 * nr]
        exchange = _ChipExchangeSteps(refs[15:15 + nr], refs[26 + nr:26 + 2 * nr], *refs[31 + 2 * nr:])
        i = pl.program_id(0)
        r = nt - 1 - i

        @pl.when(i == 0)
        def _():
            exchange.start()
            for ref in (dwin_ref, dwout_ref, dmod_ref, dg1_ref, dg2_ref, dsgn_ref, dws_ref, dbst_ref, dwp_ref, dps_ref):
                ref[...] = jnp.zeros(ref.shape, F32)
            dwsbuf[ts:ts + POOL_HALO, :] = jnp.zeros((POOL_HALO, B_WIDTH), F32)

        xv, dx1v, mixed = x_ref[...], dx1_ref[...], mixed_ref[...]
        sh, sc, gm = mod_ref[0:1, :], mod_ref[1:2, :], mod_ref[2:3, :]
        g1, g2 = g1_ref[...], g2_ref[...]
        rstd2 = _rstd(mixed)
        mh = mixed * rstd2
        dmod_ref[2:3, :] += _sum0(dx1v * (mh * g2))
        dr = dx1v * gm
        dg2_ref[...] += _sum0(dr * mh)
        dmh = dr * g2
        dmb = _bf(rstd2 * (dmh - mh * _rowmean(dmh * mh)))
        dcat[...] = _mm_nt(dmb, wout_ref[...])
        smask = _sgu_mask()
        for hd in range(N_HEAD):
            ucols = slice(hd * HEAD, (hd + 1) * HEAD)
            vcols = slice(A_WIDTH + hd * HEAD, A_WIDTH + (hd + 1) * HEAD)
            u, du_dp = _gelu_and_grad(proj_ref[:, ucols])
            v, dv_dp = _gelu_and_grad(proj_ref[:, vcols])
            rs = _rstd(v)
            vhat = v * rs
            gn = sgn_ref[hd:hd + 1, :]
            vn = _bf(vhat * gn)
            wm = _bf(jnp.where(smask, ws_ref[hd], 0.0))
            bias = bst_ref[:, hd:hd + 1]
            dzsum = jnp.zeros((HEAD, HEAD), F32)
            dwm = jnp.zeros((HEAD, HEAD), F32)
            dvn_parts = []
            for b in range(nb):
                rows = slice(b * HEAD, (b + 1) * HEAD)
                z = _mm(wm, vn[rows]) + bias
                da = dcat[rows, ucols]
                cat[rows, ucols] = _bf(u[rows] * z)
                dz = da * u[rows]
                dzsum = dzsum + dz
                dzb = _bf(dz)
                dwm = dwm + _mm_nt(dzb, vn[rows])
                dvn_parts.append(_mm_tn(wm, dzb))
                dproj[rows, ucols] = (da * z) * du_dp[rows]
            dvn = jnp.concatenate(dvn_parts, axis=0)
            dsgn_ref[hd:hd + 1, :] += _sum0(dvn * vhat)
            dvh = dvn * gn
            dproj[:, vcols] = (rs * (dvh - vhat * _rowmean(dvh * vhat))) * dv_dp
            dws_ref[hd] += jnp.where(smask, dwm, 0.0)
            dbst_ref[:, hd:hd + 1] += jnp.sum(dzsum, axis=1, keepdims=True)
        keep = jnp.where(r > 0, 1.0, 0.0).astype(F32)
        pbuf[0:POOL_HALO, :] = halo_ref[...] * keep
        pbuf[POOL_HALO:POOL_HALO + ts, :] = proj_ref[:, 2 * A_WIDTH:]
        for g, w in enumerate(WINDOWS):
            cols = slice(g * HEAD, (g + 1) * HEAD)
            ccols = slice(A_WIDTH + g * HEAD, A_WIDTH + (g + 1) * HEAD)
            pcols = slice(2 * A_WIDTH + g * HEAD, 2 * A_WIDTH + (g + 1) * HEAD)
            wpg = _bf(wp_ref[g])
            psg = ps_ref[:, cols]
            ext = pbuf[:, cols]
            inv = _inv_count(r * ts, ts, w)
            pb = _bf(_window_sum(ext, w, True)[POOL_HALO:] * inv - ext[POOL_HALO:])
            yb = _mm(pb, wpg)
            dob = dcat[:, ccols]
            cat[:, ccols] = _bf(yb * psg)
            dps_ref[:, cols] += _sum0(dob * yb)
            dyb = _bf(dob * psg)
            dwp_ref[g] += _mm_tn(pb, dyb)
            dpooled = _mm_nt(dyb, wpg)
            dwsbuf[0:ts, cols] = dpooled * inv
            dproj[:, pcols] = _window_sum(dwsbuf[:, cols], w, False)[0:ts] - dpooled
        dwsbuf[ts:ts + POOL_HALO, :] = dwsbuf[0:POOL_HALO, :]
        dpb = _bf(dproj[...])
        rstd1 = _rstd(xv)
        xh = xv * rstd1
        n1 = xh * g1
        dwin_ref[...] += _mm_tn(_bf(n1 * (1.0 + sc) + sh), dpb)
        dwout_ref[...] += _mm_tn(cat[...], dmb)
        dh = _mm_nt(dpb, win_ref[...])
        dmod_ref[0:1, :] += _sum0(dh)
        dmod_ref[1:2, :] += _sum0(dh * n1)
        dn1 = dh * (1.0 + sc)
        dg1_ref[...] += _sum0(dn1 * xh)
        dxh = dn1 * g1
        gx_ref[...] = dx1v + rstd1 * (dxh - xh * _rowmean(dxh * xh))

        @pl.when(i == nt - 1)
        def _():
            exchange.finish()

    tile = lambda wid: pl.BlockSpec((ts, wid), lambda i: (nt - 1 - i, 0))
    halo = pl.BlockSpec((POOL_HALO, B_WIDTH),
                        lambda i: (jnp.maximum((nt - 1 - i) * (ts // POOL_HALO) - 1, 0), 2 * A_WIDTH // B_WIDTH))
    const = lambda *shape: pl.BlockSpec(shape, lambda i: (0,) * len(shape))
    outs = pl.pallas_call(
        body,
        name="mix_bwd",
        grid=(nt,),
        out_shape=(jax.ShapeDtypeStruct((t_len, D), F32), jax.ShapeDtypeStruct((D, IN_WIDTH), F32),
                   jax.ShapeDtypeStruct((D, D), F32), jax.ShapeDtypeStruct((3, D), F32),
                   jax.ShapeDtypeStruct((1, D), F32), jax.ShapeDtypeStruct((1, D), F32),
                   jax.ShapeDtypeStruct((N_HEAD, HEAD), F32), jax.ShapeDtypeStruct((N_HEAD, HEAD, HEAD), F32),
                   jax.ShapeDtypeStruct((HEAD, N_HEAD), F32), jax.ShapeDtypeStruct((N_HEAD, HEAD, HEAD), F32),
                   jax.ShapeDtypeStruct((1, B_WIDTH), F32))
        + tuple(jax.ShapeDtypeStruct((3, *s.shape[2:]), s.dtype) for s in rs_srcs),
        in_specs=[tile(D), tile(D), tile(IN_WIDTH), halo, tile(D)] + [VMEM_SPEC] * 10 + [ANY_SPEC] * nr,
        out_specs=(tile(D), const(D, IN_WIDTH), const(D, D), const(3, D), const(1, D), const(1, D),
                   const(N_HEAD, HEAD), const(N_HEAD, HEAD, HEAD), const(HEAD, N_HEAD), const(N_HEAD, HEAD, HEAD),
                   const(1, B_WIDTH)) + (ANY_SPEC,) * nr,
        scratch_shapes=[pltpu.VMEM((POOL_HALO + ts, B_WIDTH), F32), pltpu.VMEM((ts + POOL_HALO, B_WIDTH), F32),
                        pltpu.VMEM((ts, D), BF16), pltpu.VMEM((ts, IN_WIDTH), F32), pltpu.VMEM((ts, D), F32),
                        pltpu.SemaphoreType.DMA((3 * nr,)), pltpu.SemaphoreType.DMA((3 * nr,))],
        compiler_params=pltpu.CompilerParams(dimension_semantics=("arbitrary",), vmem_limit_bytes=VMEM_LIMIT_V7X),
    )(dx1, x, proj, proj, mixed, mod, g_pre, g_post, w_in_b, sgn, w_sp, b_sp_t, w_pool, p_scale, w_out_b, *rs_srcs)
    return outs[:11], outs[11:]


def _exchange(name, srcs, out_shapes, plan):
    n = len(srcs)
    n_copies = len(plan(0, 0, 0))

    def body(*refs):
        ins, outs = refs[:n], refs[n:2 * n]
        send_sems, recv_sems = refs[2 * n:]
        copies = [
            pltpu.make_async_remote_copy(ins[a].at[src], outs[a].at[dst] if dst else outs[a], send_sems.at[k],
                                         recv_sems.at[k], device_id=peer, device_id_type=MESH)
            for k, (a, src, dst, peer) in enumerate(plan(*_coords()))
        ]
        for cp in copies:
            cp.start()
        for cp in copies:
            cp.wait()

    return pl.pallas_call(
        body,
        name=name,
        out_shape=tuple(jax.ShapeDtypeStruct(s, F32) for s in out_shapes),
        in_specs=[ANY_SPEC] * n,
        out_specs=(ANY_SPEC,) * n,
        scratch_shapes=[pltpu.SemaphoreType.DMA((n_copies,)), pltpu.SemaphoreType.DMA((n_copies,))],
    )(*srcs)


def _pair_add(name, coords, grid, specs_a, specs_b, out_specs, out_shapes, a_arrays, b_arrays, also_bf16):
    n = len(a_arrays)

    def body(co_ref, *refs):
        for k in range(n):
            total = refs[k][...] + refs[n + k][...]
            refs[2 * n + k][...] = total
            if also_bf16:
                refs[3 * n + k][...] = _bf(total)

    dtypes = [F32] * n + ([BF16] * n if also_bf16 else [])
    copies = 2 if also_bf16 else 1
    outs = pl.pallas_call(
        body,
        name=name,
        grid_spec=pltpu.PrefetchScalarGridSpec(num_scalar_prefetch=1, grid=grid, in_specs=specs_a + specs_b,
                                               out_specs=out_specs * copies),
        out_shape=tuple(jax.ShapeDtypeStruct(s, dt) for s, dt in zip(out_shapes * copies, dtypes)),
        compiler_params=pltpu.CompilerParams(dimension_semantics=("arbitrary",) * len(grid),
                                             vmem_limit_bytes=VMEM_LIMIT_V7X),
    )(coords, *a_arrays, *b_arrays)
    return list(outs[:n]), list(outs[n:])


def _final_add_adamw(coords, s1, r, ws, ms, vs, n_split=4):
    n = len(s1)

    def body(co_ref, *refs):
        for k in range(n):
            s_ref, r_ref, w_ref, m_ref, v_ref = (refs[q * n + k] for q in range(5))
            g_ref, d_ref, nm_ref, nv_ref = (refs[(5 + q) * n + k] for q in range(4))
            g = ((s_ref[...] + r_ref[0].astype(F32)) + r_ref[1].astype(F32)) + r_ref[2].astype(F32)
            g_ref[...] = g
            delta, m, v = _adamw(w_ref[...], g, m_ref[...], v_ref[...])
            d_ref[...] = delta
            nm_ref[...] = m
            nv_ref[...] = v

    def shard_spec(a):
        rows, cols = a.shape
        return pl.BlockSpec((rows // n_split, cols), lambda i, co: (i, 0))

    def mine_spec(a):
        rows, cols = a.shape[2:]
        return pl.BlockSpec((None, None, rows // n_split, cols), lambda i, co: (co[0], co[1], i, 0))

    def recv_spec(a):
        rows, cols = a.shape[1:]
        return pl.BlockSpec((3, rows // n_split, cols), lambda i, co: (0, i, 0))

    in_specs = ([mine_spec(a) for a in s1] + [recv_spec(a) for a in r] + [shard_spec(a) for a in ws] * 3)
    out_specs = [shard_spec(a) for a in ws] * 4
    outs = pl.pallas_call(
        body,
        name="grad_final_adamw",
        grid_spec=pltpu.PrefetchScalarGridSpec(num_scalar_prefetch=1, grid=(n_split,), in_specs=in_specs,
                                               out_specs=out_specs),
        out_shape=tuple(jax.ShapeDtypeStruct(a.shape, F32) for a in ws) * 4,
        compiler_params=pltpu.CompilerParams(dimension_semantics=("arbitrary",), vmem_limit_bytes=VMEM_LIMIT_V7X),
    )(coords, *s1, *r, *ws, *ms, *vs)
    return [tuple(outs[q * n + k] for q in range(4)) for k in range(n)]


def _sibling_sum(tag, grads, coords, also_bf16, n_split=4):
    n = len(grads)
    shapes = [g.shape[1:] for g in grads]
    g5 = [g.reshape(2, 2, 2, *s) for g, s in zip(grads, shapes)]

    def plan(x, y, c):
        return [(a, (xs, ys, 1 - c), (xs, ys), (x, y, 1 - c)) for a in range(n) for xs in range(2) for ys in range(2)]

    r1 = _exchange("grad_swap_core_" + tag, g5, [(2, 2, *s) for s in shapes], plan)
    spec_g = [pl.BlockSpec((None, None, None, s[0] // n_split, s[1]), lambda i, j, k, co: (i, j, co[2], k, 0))
              for s in shapes]
    spec_r = [pl.BlockSpec((None, None, s[0] // n_split, s[1]), lambda i, j, k, co: (i, j, k, 0)) for s in shapes]
    return _pair_add("grad_add_core_" + tag, coords, (2, 2, n_split), spec_g, spec_r, spec_r,
                     [(2, 2, *s) for s in shapes], g5, list(r1), also_bf16)


def _chip_exchange(tag, s1):
    n = len(s1)

    def body(*refs):
        steps = _ChipExchangeSteps(refs[:n], refs[n:2 * n], *refs[2 * n:])
        steps.start()
        steps.finish()

    return list(pl.pallas_call(
        body,
        name="grad_swap_chips_" + tag,
        out_shape=tuple(jax.ShapeDtypeStruct((3, *s.shape[2:]), s.dtype) for s in s1),
        in_specs=[ANY_SPEC] * n,
        out_specs=(ANY_SPEC,) * n,
        scratch_shapes=[pltpu.SemaphoreType.DMA((3 * n,)), pltpu.SemaphoreType.DMA((3 * n,))],
    )(*s1))


def _small_allreduce(partials, pick_mine, dmod3):
    n = len(partials)

    def body(*refs):
        p_in, dm_ref = refs[:n], refs[n]
        sums, dm2d = refs[n + 1:2 * n + 1], refs[2 * n + 1]
        acc = refs[2 * n + 2:3 * n + 2]
        rbuf = refs[3 * n + 2:4 * n + 2]
        dm_recv, send_sems, recv_sems, dm_send_sems, dm_recv_sems = refs[4 * n + 2:]
        x, y, c = _coords()
        me = 4 * x + 2 * y + c
        dm_copies = [
            pltpu.make_async_remote_copy(dm_ref.at[me ^ k], dm_recv.at[k], dm_send_sems.at[k], dm_recv_sems.at[k],
                                         device_id=_peer(k), device_id_type=MESH)
            for k in range(1, N_DEV)
        ]
        for cp in dm_copies:
            cp.start()
        for a in range(n):
            acc[a][...] = p_in[a][...]
        for ph, peer in enumerate([(x, y, 1 - c), (1 - x, y, c), (x, 1 - y, c)]):
            copies = [
                pltpu.make_async_remote_copy(acc[a], rbuf[a].at[ph], send_sems.at[ph, a], recv_sems.at[ph, a],
                                             device_id=peer, device_id_type=MESH)
                for a in range(n)
            ]
            for cp in copies:
                cp.start()
            for cp in copies:
                cp.wait()
            for a in range(n):
                acc[a][...] = acc[a][...] + rbuf[a][ph]
        for a in range(n):
            sums[a][...] = acc[a][me] if pick_mine[a] else acc[a][...]
        dm2d[...] = jnp.zeros(dm2d.shape, F32)
        dm2d[0:1, :] = dm_ref[me]
        for cp in dm_copies:
            cp.wait()
        for k in range(1, N_DEV):
            dm2d[k:k + 1, :] = dm_recv[k]

    out_shapes = tuple(jax.ShapeDtypeStruct(p.shape[1:] if pk else p.shape, F32) for p, pk in zip(partials, pick_mine))
    outs = pl.pallas_call(
        body,
        name="small_allreduce",
        out_shape=out_shapes + (jax.ShapeDtypeStruct((2 * N_DEV, MOD_COLS), F32),),
        in_specs=[VMEM_SPEC] * (n + 1),
        out_specs=(VMEM_SPEC,) * (n + 1),
        scratch_shapes=[pltpu.VMEM(p.shape, F32) for p in partials]
        + [pltpu.VMEM((3, *p.shape), F32) for p in partials]
        + [pltpu.VMEM((N_DEV, 1, MOD_COLS), F32), pltpu.SemaphoreType.DMA((3, n)), pltpu.SemaphoreType.DMA((3, n)),
           pltpu.SemaphoreType.DMA((N_DEV,)), pltpu.SemaphoreType.DMA((N_DEV,))],
        compiler_params=pltpu.CompilerParams(vmem_limit_bytes=VMEM_LIMIT_V7X),
    )(*partials, dmod3)
    return list(outs[:n]), outs[n]


def _small_update(grads, ws, ms, vs, scx, dm2d, w_ada, m_ada, v_ada, loss_lanes):
    n = len(grads)

    def body(*refs):
        g_in, w_in, m_in, v_in = (refs[q * n:(q + 1) * n] for q in range(4))
        scx_ref, dm_ref, wa_ref, ma_ref, va_ref, ll_ref = refs[4 * n:4 * n + 6]
        outs = refs[4 * n + 6:]
        g_out, d_out, nm_out, nv_out = (outs[q * (n + 1):(q + 1) * (n + 1)] for q in range(4))
        loss_ref = outs[4 * (n + 1)]
        for a in range(n + 1):
            if a < n:
                g, w, m, v = g_in[a][...], w_in[a][...], m_in[a][...], v_in[a][...]
            else:
                g = _mm_tn(_bf(scx_ref[...]), _bf(dm_ref[...]))
                w, m, v = wa_ref[...], ma_ref[...], va_ref[...]
            g_out[a][...] = g
            delta, m, v = _adamw(w, g, m, v)
            d_out[a][...] = delta
            nm_out[a][...] = m
            nv_out[a][...] = v
        loss_ref[...] = jnp.sum(ll_ref[...], axis=1, keepdims=True) * (0.5 / D)

    w_shapes = tuple(jax.ShapeDtypeStruct(w.shape, F32) for w in list(ws) + [w_ada])
    outs = pl.pallas_call(
        body,
        name="small_update",
        out_shape=w_shapes * 4 + (jax.ShapeDtypeStruct((1, 1), F32),),
        in_specs=[VMEM_SPEC] * (4 * n + 6),
        out_specs=(VMEM_SPEC,) * (4 * (n + 1) + 1),
        compiler_params=pltpu.CompilerParams(vmem_limit_bytes=VMEM_LIMIT_V7X),
    )(*grads, *ws, *ms, *vs, scx, dm2d, w_ada, m_ada, v_ada, loss_lanes)
    return [tuple(outs[q * (n + 1) + k] for q in range(4)) for k in range(n + 1)], outs[4 * (n + 1)]


def kernel(x, c, w_ada, b_ada, pre_mix_g, post_mix_g, w_in, sgu_norm_g, w_spatial, b_spatial, w_pool, pool_scale, w_out, pre_ffn_g, post_ffn_g, w_up, conv_w, conv_b, w_down, loss_target, m_w_ada, m_b_ada, m_pre_mix_g, m_post_mix_g, m_w_in, m_sgu_norm_g, m_w_spatial, m_b_spatial, m_w_pool, m_pool_scale, m_w_out, m_pre_ffn_g, m_post_ffn_g, m_w_up, m_conv_w, m_conv_b, m_w_down, v_w_ada, v_b_ada, v_pre_mix_g, v_post_mix_g, v_w_in, v_sgu_norm_g, v_w_spatial, v_b_spatial, v_w_pool, v_pool_scale, v_w_out, v_pre_ffn_g, v_post_ffn_g, v_w_up, v_conv_w, v_conv_b, v_w_down):
    t_len = x.shape[1]
    ts = min(256, t_len)
    ts_w = min(1024, t_len)
    coords = jnp.stack([lax.axis_index("x"), lax.axis_index("y"), lax.axis_index("c")]).astype(jnp.int32)

    mod3, scx, (g_in, g_out) = _prologue(c, w_ada[0], b_ada.reshape(N_DEV, 1, MOD_COLS), [w_in[0], w_out[0]],
                                         [BF16, BF16])
    mod = mod3.reshape(N_MOD, D)
    w_in_b = g_in.transpose(1, 0, 2).reshape(D, IN_WIDTH)
    w_out_b = g_out.reshape(D, D)
    conv_b8 = conv_b.reshape(N_DEV, FF_CHUNK)
    b_sp_t = b_spatial[0].T

    x2d, tgt = x[0], loss_target[0]
    (x1, proj, mixed), (g_up, g_down, g_cw) = _mix_fwd(
        x2d, mod, pre_mix_g, post_mix_g, w_in_b, sgu_norm_g[0], w_spatial[0], b_sp_t, w_pool[0], pool_scale, w_out_b,
        ts, [w_up[0], w_down[0], conv_w[0]], [BF16, BF16, F32])
    w_down_b = g_down.reshape(FF, D)
    up, f, dx2, loss_lanes = _ffn_fwd(x1, tgt, mod, pre_ffn_g, post_ffn_g, g_up, g_cw, conv_b8, w_down_b, ts)

    (dx1, dup, act, df, h2, dmod_f, d_pre_ffn, d_post_ffn, d_cb8, d_cw8) = _ffn_bwd(
        dx2, f, x1, up, mod, pre_ffn_g, post_ffn_g, g_up, g_cw, conv_b8, w_down_b, ts)
    gw_up = _wgrad_up(h2, dup, ts_w)
    gw_down = _wgrad_down(act, df, ts_w).reshape(N_DEV, FF // N_DEV, D)
    s1_ffn, _ = _sibling_sum("ffn", [gw_up, gw_down], coords, False)
    ((grad_x, gw_in, gw_out, dmod_m, d_pre_mix, d_post_mix, d_sgn, d_wsp, d_bsp_t, d_wpool, d_ps), r_ffn) = _mix_bwd(
        dx1, x2d, proj, mixed, mod, pre_mix_g, post_mix_g, w_in_b, sgu_norm_g[0], w_spatial[0], b_sp_t,
        w_pool[0], pool_scale, w_out_b, ts, s1_ffn)
    gw_in = gw_in.reshape(D, N_DEV, IN_WIDTH // N_DEV).transpose(1, 0, 2)
    gw_out = gw_out.reshape(N_DEV, D // N_DEV, D)
    s1_mix, s1_mix_b = _sibling_sum("mix", [gw_in, gw_out], coords, True)
    r_mix = _chip_exchange("mix", s1_mix_b)

    big = _final_add_adamw(
        coords, s1_mix + s1_ffn, r_mix + list(r_ffn),
        [w_in[0], w_out[0], w_up[0], w_down[0]], [m_w_in[0], m_w_out[0], m_w_up[0], m_w_down[0]],
        [v_w_in[0], v_w_out[0], v_w_up[0], v_w_down[0]])
    r_in, r_out, r_up, r_down = [tuple(a[None] for a in four) for four in big]

    dmod = jnp.concatenate([dmod_m, dmod_f], axis=0)
    names = ["b_ada", "pre_mix_g", "post_mix_g", "sgu_norm_g", "w_spatial", "b_spatial", "w_pool", "pool_scale",
             "pre_ffn_g", "post_ffn_g", "conv_w", "conv_b"]
    partials = [dmod.reshape(1, N_MOD * D), d_pre_mix, d_post_mix, d_sgn, d_wsp, d_bsp_t.T, d_wpool, d_ps,
                d_pre_ffn, d_post_ffn, d_cw8, d_cb8.reshape(1, 2 * FF), loss_lanes]
    small_w = [b_ada, pre_mix_g, post_mix_g, sgu_norm_g[0], w_spatial[0], b_spatial[0], w_pool[0], pool_scale,
               pre_ffn_g, post_ffn_g, conv_w[0], conv_b]
    small_m = [m_b_ada, m_pre_mix_g, m_post_mix_g, m_sgu_norm_g[0], m_w_spatial[0], m_b_spatial[0], m_w_pool[0],
               m_pool_scale, m_pre_ffn_g, m_post_ffn_g, m_conv_w[0], m_conv_b]
    small_v = [v_b_ada, v_pre_mix_g, v_post_mix_g, v_sgu_norm_g[0], v_w_spatial[0], v_b_spatial[0], v_w_pool[0],
               v_pool_scale, v_pre_ffn_g, v_post_ffn_g, v_conv_w[0], v_conv_b]
    sums, dm2d = _small_allreduce(partials, [nm == "conv_w" for nm in names] + [False],
                                  dmod.reshape(N_DEV, 1, MOD_COLS))
    small, loss11 = _small_update(sums[:-1], small_w, small_m, small_v, scx, dm2d, w_ada[0], m_w_ada[0], v_w_ada[0],
                                  sums[-1])
    loss = loss11.reshape(())
    lead = {"sgu_norm_g", "w_spatial", "b_spatial", "w_pool", "conv_w", "w_ada"}
    res = {nm: tuple(a[None] if nm in lead else a for a in four) for nm, four in zip(names + ["w_ada"], small)}
    res.update(w_in=r_in, w_out=r_out, w_up=r_up, w_down=r_down)

    order = ["w_ada", "b_ada", "pre_mix_g", "post_mix_g", "w_in", "sgu_norm_g", "w_spatial", "b_spatial", "w_pool",
             "pool_scale", "w_out", "pre_ffn_g", "post_ffn_g", "w_up", "conv_w", "conv_b", "w_down"]
    return (loss, grad_x[None], *[res[nm][0] for nm in order], *[res[nm][1] for nm in order],
            *[res[nm][2] for nm in order], *[res[nm][3] for nm in order])
```

```python
import functools
import math

import jax
import jax.numpy as jnp
from jax import lax
from jax.experimental import pallas as pl
from jax.experimental.pallas import tpu as pltpu

F32 = jnp.float32
BF16 = jnp.bfloat16
MESH = pl.DeviceIdType.MESH

EPS = 1e-6
D = 1024
HEAD = 128
N_HEAD = 4
A_WIDTH = 512
B_WIDTH = 512
IN_WIDTH = 1536
WINDOWS = (2, 4, 8, 16)
CHUNK = 64
FF = 2816
N_DEV = 8
FF_CHUNK = 704
N_MOD = 6
MOD_COLS = 768

ADAM_LR = 0.001
ADAM_B1 = 0.9
ADAM_B2 = 0.999
ADAM_EPS = 1e-08
ADAM_WD = 0.01
ADAM_STEP = 10

VMEM_LIMIT_V7X = 56 * 1024 * 1024
HALO = 8
POOL_HALO = 16

VMEM_SPEC = pl.BlockSpec(memory_space=pltpu.VMEM)
ANY_SPEC = pl.BlockSpec(memory_space=pl.ANY)


def _bf(x):
    return x.astype(BF16)


def _mm(a, b):
    return jnp.dot(a, b, preferred_element_type=F32)


def _mm_nt(a, b):
    return lax.dot_general(a, b, (((1,), (1,)), ((), ())), preferred_element_type=F32)


def _mm_tn(a, b):
    return lax.dot_general(a, b, (((0,), (0,)), ((), ())), preferred_element_type=F32)


def _rstd(x):
    return lax.rsqrt(jnp.mean(x * x, axis=-1, keepdims=True) + EPS)


def _sum0(x):
    return jnp.sum(x, axis=0, keepdims=True)


def _rowmean(x):
    return jnp.mean(x, axis=-1, keepdims=True)


_GELU_K = math.sqrt(2.0 / math.pi)


def _gelu_and_grad(x):
    x2 = x * x
    th = jnp.tanh(_GELU_K * (x + 0.044715 * (x * x2)))
    cdf = 0.5 * (1.0 + th)
    grad = cdf + 0.5 * x * (1.0 - th * th) * (_GELU_K * (1.0 + 3.0 * 0.044715 * x2))
    return x * cdf, grad


def _gelu(x):
    return x * (0.5 * (1.0 + jnp.tanh(_GELU_K * (x + 0.044715 * (x * x * x)))))


def _sigmoid(x):
    return 0.5 * jnp.tanh(0.5 * x) + 0.5


def _sgu_mask():
    ri = lax.broadcasted_iota(jnp.int32, (HEAD, HEAD), 0)
    ci = lax.broadcasted_iota(jnp.int32, (HEAD, HEAD), 1)
    return (ci // CHUNK) <= (ri // CHUNK)


def _window_sum(ext, w, trailing):
    n = ext.shape[0]
    s, k = ext, 1
    while k < w:
        s = s + pltpu.roll(s, k if trailing else n - k, 0)
        k *= 2
    return s


def _inv_count(row0, n, w):
    t = row0 + lax.broadcasted_iota(jnp.int32, (n, 1), 0)
    return 1.0 / jnp.minimum(t + 1, w).astype(F32)


def _shift_down(v, before, k):
    rows = lax.broadcasted_iota(jnp.int32, before.shape, 0)
    r = pltpu.roll(v, k, 0)
    top = jnp.where(rows < k, pltpu.roll(before, k, 0), r[0:HALO])
    return jnp.concatenate([top, r[HALO:]], axis=0)


def _shift_up(v, after, k):
    n = v.shape[0]
    rows = lax.broadcasted_iota(jnp.int32, after.shape, 0)
    r = pltpu.roll(v, n - k, 0)
    bottom = jnp.where(rows >= HALO - k, pltpu.roll(after, HALO - k, 0), r[n - HALO:])
    return jnp.concatenate([r[:n - HALO], bottom], axis=0)


def _adamw(w, g, m, v):
    m = ADAM_B1 * m + (1.0 - ADAM_B1) * g
    v = ADAM_B2 * v + (1.0 - ADAM_B2) * (g * g)
    m_hat = m / (1.0 - ADAM_B1 ** ADAM_STEP)
    v_hat = v / (1.0 - ADAM_B2 ** ADAM_STEP)
    delta = -ADAM_LR * (m_hat / (jnp.sqrt(v_hat) + ADAM_EPS) + ADAM_WD * w)
    return delta, m, v


def _coords():
    return lax.axis_index("x"), lax.axis_index("y"), lax.axis_index("c")


def _peer(k):
    x, y, c = _coords()
    return (x ^ ((k >> 2) & 1), y ^ ((k >> 1) & 1), c ^ (k & 1))


def _my_index():
    x, y, c = _coords()
    return 4 * x + 2 * y + c


def _adaln_modulation(c_ref, w_ref, b_ref, mod_ref, scx_ref, scbuf, stage, recv, send_sems, recv_sems):
    me = _my_index()
    cv = c_ref[...]
    scbuf[0] = cv * _sigmoid(cv)
    first = [
        pltpu.make_async_remote_copy(scbuf.at[0], scbuf.at[k], send_sems.at[0, k], recv_sems.at[0, k],
                                     device_id=_peer(k), device_id_type=MESH)
        for k in range(1, N_DEV)
    ]
    for cp in first:
        cp.start()
    for cp in first:
        cp.wait()
    scx_ref[...] = jnp.zeros(scx_ref.shape, F32)
    for k in range(N_DEV):
        scx_ref[k:k + 1, :] = scbuf[k]
    prod = _mm(_bf(scx_ref[...]), _bf(w_ref[...]))
    for k in range(N_DEV):
        stage[k] = prod[k:k + 1, :] + b_ref[me]
    second = [
        pltpu.make_async_remote_copy(stage.at[k], recv.at[k], send_sems.at[1, k], recv_sems.at[1, k],
                                     device_id=_peer(k), device_id_type=MESH)
        for k in range(1, N_DEV)
    ]
    for cp in second:
        cp.start()
    mod_ref[me] = stage[0]
    for cp in second:
        cp.wait()
    for k in range(1, N_DEV):
        mod_ref[me ^ k] = recv[k]


class _GatherSteps:
    def __init__(self, ins, outs, stages, send_sems, recv_sems, local_sems):
        self.ins, self.outs, self.stages = ins, outs, stages
        self.send_sems, self.recv_sems, self.local_sems = send_sems, recv_sems, local_sems
        x, y, c = _coords()
        self.c = c
        self.me, self.sibling = (x, y, c), (x, y, 1 - c)
        self.chips = [(1 - x, y), (x, 1 - y), (1 - x, 1 - y)]

    def _copy(self, a, k, block, to, from_stage=False):
        dst = self.outs[a].at[4 * block[0] + 2 * block[1] + block[2]]
        return pltpu.make_async_remote_copy(self.stages[a] if from_stage else dst, dst, self.send_sems.at[a, k],
                                            self.recv_sems.at[a, k], device_id=to, device_id_type=MESH)

    def _local(self, a):
        me = self.me
        return pltpu.make_async_copy(self.stages[a], self.outs[a].at[4 * me[0] + 2 * me[1] + me[2]],
                                     self.local_sems.at[a])

    def _first(self, a):
        cps = [self._copy(a, 0, self.me, self.sibling, from_stage=True)]
        return cps + [self._copy(a, 1 + j, self.me, (*chip, self.c), from_stage=True)
                      for j, chip in enumerate(self.chips)]

    def _passed(self, a, j):
        return self._copy(a, 4 + j, (*self.chips[j], self.c), self.sibling)

    def start(self):
        for a in range(len(self.ins)):
            self.stages[a][...] = self.ins[a][...].astype(self.stages[a].dtype)
            self._local(a).start()
            for cp in self._first(a):
                cp.start()

    def forward(self):
        for a in range(len(self.ins)):
            for j, chip in enumerate(self.chips):
                self._copy(a, 1 + j, (*chip, self.c), self.me).wait_recv()
                self._passed(a, j).start()

    def finish(self):
        for a in range(len(self.ins)):
            self._copy(a, 0, self.sibling, self.me).wait_recv()
            for j, chip in enumerate(self.chips):
                self._copy(a, 4 + j, (*chip, 1 - self.c), self.me).wait_recv()
            for cp in self._first(a) + [self._passed(a, j) for j in range(3)]:
                cp.wait_send()
            self._local(a).wait()


def _gather_scratch(shards, out_dtypes):
    n = len(shards)
    return ([pltpu.VMEM(s.shape, dt) for s, dt in zip(shards, out_dtypes)]
            + [pltpu.SemaphoreType.DMA((n, 7)), pltpu.SemaphoreType.DMA((n, 7)), pltpu.SemaphoreType.DMA((n,))])


def _gather_out_shapes(shards, out_dtypes):
    return tuple(jax.ShapeDtypeStruct((N_DEV, *s.shape), dt) for s, dt in zip(shards, out_dtypes))


def _prologue(c_row, w_ada, b_ada3, shards, out_dtypes):
    n = len(shards)

    def body(*refs):
        c_ref, w_ref, b_ref = refs[:3]
        mod_ref, scx_ref = refs[3 + n:5 + n]
        gather = _GatherSteps(refs[3:3 + n], refs[5 + n:5 + 2 * n], refs[5 + 2 * n:5 + 3 * n],
                              *refs[5 + 3 * n:8 + 3 * n])
        gather.start()
        _adaln_modulation(c_ref, w_ref, b_ref, mod_ref, scx_ref, *refs[8 + 3 * n:])
        gather.forward()
        gather.finish()

    outs = pl.pallas_call(
        body,
        name="prologue",
        out_shape=(jax.ShapeDtypeStruct((N_DEV, 1, MOD_COLS), F32), jax.ShapeDtypeStruct((2 * N_DEV, D), F32))
        + _gather_out_shapes(shards, out_dtypes),
        in_specs=[VMEM_SPEC] * (3 + n),
        out_specs=(VMEM_SPEC, VMEM_SPEC) + (ANY_SPEC,) * n,
        scratch_shapes=_gather_scratch(shards, out_dtypes) + [
            pltpu.VMEM((N_DEV, 1, D), F32),
            pltpu.VMEM((N_DEV, 1, MOD_COLS), F32),
            pltpu.VMEM((N_DEV, 1, MOD_COLS), F32),
            pltpu.SemaphoreType.DMA((2, N_DEV)),
            pltpu.SemaphoreType.DMA((2, N_DEV)),
        ],
        compiler_params=pltpu.CompilerParams(vmem_limit_bytes=VMEM_LIMIT_V7X),
    )(c_row, w_ada, b_ada3, *shards)
    return outs[0], outs[1], outs[2:]


class _ChipExchangeSteps:
    FLIPS = ((1, 0), (0, 1), (1, 1))

    def __init__(self, srcs, dsts, send_sems, recv_sems):
        self.srcs, self.dsts, self.send_sems, self.recv_sems = srcs, dsts, send_sems, recv_sems

    def _copies(self):
        x, y, c = _coords()
        out = []
        for a in range(len(self.srcs)):
            for j, (fx, fy) in enumerate(self.FLIPS):
                k = 3 * a + j
                out.append(pltpu.make_async_remote_copy(
                    self.srcs[a].at[x ^ fx, y ^ fy], self.dsts[a].at[j], self.send_sems.at[k], self.recv_sems.at[k],
                    device_id=(x ^ fx, y ^ fy, c), device_id_type=MESH))
        return out

    def start(self):
        for cp in self._copies():
            cp.start()

    def finish(self):
        for cp in self._copies():
            cp.wait()


def _mix_fwd(x, mod, g_pre, g_post, w_in_b, sgn, w_sp, b_sp_t, w_pool, p_scale, w_out_b, ts, shards, shard_dtypes):
    t_len = x.shape[0]
    nt, nb = t_len // ts, ts // HEAD
    ns = len(shards)

    def body(*refs):
        (x_ref, mod_ref, g1_ref, g2_ref, win_ref, sgn_ref, ws_ref, bst_ref, wp_ref, ps_ref, wout_ref) = refs[:11]
        x1_ref, proj_ref, mixed_ref = refs[11 + ns:14 + ns]
        pbuf, cat = refs[14 + 2 * ns:16 + 2 * ns]
        gather = _GatherSteps(refs[11:11 + ns], refs[14 + ns:14 + 2 * ns], refs[16 + 2 * ns:16 + 3 * ns],
                              *refs[16 + 3 * ns:])
        i = pl.program_id(0)

        @pl.when(i == 0)
        def _():
            pbuf[0:POOL_HALO, :] = jnp.zeros((POOL_HALO, B_WIDTH), F32)
            gather.start()

        @pl.when(i == (3 * nt) // 4)
        def _():
            gather.forward()

        xv = x_ref[...]
        sh, sc, gm = mod_ref[0:1, :], mod_ref[1:2, :], mod_ref[2:3, :]
        h = (xv * _rstd(xv) * g1_ref[...]) * (1.0 + sc) + sh
        proj_ref[...] = _mm(_bf(h), win_ref[...])
        pbuf[POOL_HALO:POOL_HALO + ts, :] = proj_ref[:, 2 * A_WIDTH:]
        smask = _sgu_mask()
        for hd in range(N_HEAD):
            u = _gelu(proj_ref[:, hd * HEAD:(hd + 1) * HEAD])
            v = _gelu(proj_ref[:, A_WIDTH + hd * HEAD:A_WIDTH + (hd + 1) * HEAD])
            vn = _bf(v * _rstd(v) * sgn_ref[hd:hd + 1, :])
            wm = _bf(jnp.where(smask, ws_ref[hd], 0.0))
            bias = bst_ref[:, hd:hd + 1]
            for b in range(nb):
                rows = slice(b * HEAD, (b + 1) * HEAD)
                z = _mm(wm, vn[rows]) + bias
                cat[rows, hd * HEAD:(hd + 1) * HEAD] = _bf(u[rows] * z)
        for g, w in enumerate(WINDOWS):
            cols = slice(g * HEAD, (g + 1) * HEAD)
            ext = pbuf[:, cols]
            pooled = _window_sum(ext, w, True)[POOL_HALO:] * _inv_count(i * ts, ts, w) - ext[POOL_HALO:]
            cat[:, A_WIDTH + g * HEAD:A_WIDTH + (g + 1) * HEAD] = _bf(_mm(_bf(pooled), _bf(wp_ref[g])) * ps_ref[:, cols])
        pbuf[0:POOL_HALO, :] = pbuf[ts:ts + POOL_HALO, :]
        mixed = _mm(cat[...], wout_ref[...])
        mixed_ref[...] = mixed
        x1_ref[...] = xv + gm * (mixed * _rstd(mixed) * g2_ref[...])

        @pl.when(i == nt - 1)
        def _():
            gather.finish()

    tile = lambda wid: pl.BlockSpec((ts, wid), lambda i: (i, 0))
    outs = pl.pallas_call(
        body,
        name="mix_fwd",
        grid=(nt,),
        out_shape=(jax.ShapeDtypeStruct((t_len, D), F32), jax.ShapeDtypeStruct((t_len, IN_WIDTH), F32),
                   jax.ShapeDtypeStruct((t_len, D), F32)) + _gather_out_shapes(shards, shard_dtypes),
        in_specs=[tile(D)] + [VMEM_SPEC] * (10 + ns),
        out_specs=(tile(D), tile(IN_WIDTH), tile(D)) + (ANY_SPEC,) * ns,
        scratch_shapes=[pltpu.VMEM((POOL_HALO + ts, B_WIDTH), F32), pltpu.VMEM((ts, D), BF16)]
        + _gather_scratch(shards, shard_dtypes),
        compiler_params=pltpu.CompilerParams(dimension_semantics=("arbitrary",), vmem_limit_bytes=VMEM_LIMIT_V7X),
    )(x, mod, g_pre, g_post, w_in_b, sgn, w_sp, b_sp_t, w_pool, p_scale, w_out_b, *shards)
    return outs[:3], outs[3:]


def _ffn_fwd(x1, target, mod, g_pre, g_post, w_up_b, conv_w8, conv_b8, w_down_b, ts):
    t_len = x1.shape[0]
    nt = t_len // ts

    def body(x1_ref, tgt_ref, mod_ref, g3_ref, g4_ref, wup_ref, cw_ref, cb_ref, wdown_ref,
             up_ref, f_ref, dx2_ref, loss_ref, ucarry):
        i = pl.program_id(0)

        @pl.when(i == 0)
        def _():
            ucarry[...] = jnp.zeros(ucarry.shape, F32)
            loss_ref[...] = jnp.zeros(loss_ref.shape, F32)

        x1v = x1_ref[...]
        sh, sc, gf = mod_ref[3:4, :], mod_ref[4:5, :], mod_ref[5:6, :]
        h2 = _bf((x1v * _rstd(x1v) * g3_ref[...]) * (1.0 + sc) + sh)
        half = N_DEV // 2

        def up_pair(j):
            return [_mm(h2, wup_ref[jj]) for jj in (j, j + half)]

        f = jnp.zeros((ts, D), F32)
        ups = up_pair(0)
        for j in range(half):
            nxt = up_pair(j + 1) if j + 1 < half else None
            ys = []
            for up, jj in zip(ups, (j, j + half)):
                up_ref[jj] = up
                before = ucarry[jj]
                ucarry[jj] = up[ts - HALO:, :]
                cw = cw_ref[jj]
                ys.append(cb_ref[jj:jj + 1, :] + _shift_down(up, before, 2) * cw[0:1, :]
                          + _shift_down(up, before, 1) * cw[1:2, :] + up * cw[2:3, :])
            gate, val = ys
            act = gate * _sigmoid(gate) * val
            f = f + _mm(_bf(act), wdown_ref[j * FF_CHUNK:(j + 1) * FF_CHUNK, :])
            ups = nxt
        f_ref[...] = f
        x2 = x1v + gf * (f * _rstd(f) * g4_ref[...])
        err = x2 - tgt_ref[...]
        loss_ref[...] += _sum0(err * err)
        dx2_ref[...] = err * (1.0 / D)

    tile = pl.BlockSpec((ts, D), lambda i: (i, 0))
    return pl.pallas_call(
        body,
        name="ffn_fwd",
        grid=(nt,),
        out_shape=(jax.ShapeDtypeStruct((N_DEV, t_len, FF_CHUNK), F32), jax.ShapeDtypeStruct((t_len, D), F32),
                   jax.ShapeDtypeStruct((t_len, D), F32), jax.ShapeDtypeStruct((1, D), F32)),
        in_specs=[tile, tile] + [VMEM_SPEC] * 7,
        out_specs=(pl.BlockSpec((N_DEV, ts, FF_CHUNK), lambda i: (0, i, 0)), tile, tile,
                   pl.BlockSpec((1, D), lambda i: (0, 0))),
        scratch_shapes=[pltpu.VMEM((N_DEV, HALO, FF_CHUNK), F32)],
        compiler_params=pltpu.CompilerParams(dimension_semantics=("arbitrary",), vmem_limit_bytes=VMEM_LIMIT_V7X),
    )(x1, target, mod, g_pre, g_post, w_up_b, conv_w8, conv_b8, w_down_b)


def _ffn_bwd(dx2, f, x1, up, mod, g_pre, g_post, w_up_b, conv_w8, conv_b8, w_down_b, ts):
    t_len = x1.shape[0]
    nt = t_len // ts
    half = N_DEV // 2

    def body(dx2_ref, f_ref, x1_ref, up_ref, halo_ref, mod_ref, g3_ref, g4_ref, wup_ref, cw_ref, cb_ref, wdown_ref,
             dx1_ref, dup_ref, act_ref, df_ref, h2_ref, dmod_ref, dg3_ref, dg4_ref, dcb_ref, dcw_ref,
             dycarry, dh2acc):
        i = pl.program_id(0)
        r = nt - 1 - i

        @pl.when(i == 0)
        def _():
            for ref in (dmod_ref, dg3_ref, dg4_ref, dcb_ref, dcw_ref, dycarry):
                ref[...] = jnp.zeros(ref.shape, F32)

        dx2v, fv, x1v = dx2_ref[...], f_ref[...], x1_ref[...]
        sh, sc, gf = mod_ref[3:4, :], mod_ref[4:5, :], mod_ref[5:6, :]
        g3, g4 = g3_ref[...], g4_ref[...]
        rstd4 = _rstd(fv)
        fh = fv * rstd4
        dmod_ref[2:3, :] += _sum0(dx2v * (fh * g4))
        dr = dx2v * gf
        dg4_ref[...] += _sum0(dr * fh)
        dfh = dr * g4
        dfb = _bf(rstd4 * (dfh - fh * _rowmean(dfh * fh)))
        df_ref[...] = dfb
        rstd3 = _rstd(x1v)
        xh = x1v * rstd3
        n3 = xh * g3
        h2_ref[...] = _bf(n3 * (1.0 + sc) + sh)
        dh2acc[...] = jnp.zeros((ts, D), F32)
        keep = jnp.where(r > 0, 1.0, 0.0).astype(F32)

        def dact_of(j):
            return _mm_nt(dfb, wdown_ref[j * FF_CHUNK:(j + 1) * FF_CHUNK, :])

        dact_next = dact_of(0)
        for j in range(half):
            dact = dact_next
            if j + 1 < half:
                dact_next = dact_of(j + 1)
            ys = []
            for jj in (j, j + half):
                before = halo_ref[jj] * keep
                upc = up_ref[jj]
                cw = cw_ref[jj]
                ys.append(cb_ref[jj:jj + 1, :] + _shift_down(upc, before, 2) * cw[0:1, :]
                          + _shift_down(upc, before, 1) * cw[1:2, :] + upc * cw[2:3, :])
            gate, val = ys
            sg = _sigmoid(gate)
            silu = gate * sg
            act_ref[j] = _bf(silu * val)
            dys = (dact * val * (sg * (1.0 + gate * (1.0 - sg))), dact * silu)
            for q, jj in enumerate((j, j + half)):
                dy = dys[q]
                cw = cw_ref[jj]
                dcb_ref[jj:jj + 1, :] += _sum0(dy)
                after = dycarry[jj]
                dycarry[jj] = dy[0:HALO, :]
                dy1, dy2 = _shift_up(dy, after, 1), _shift_up(dy, after, 2)
                upc = up_ref[jj]
                dcw_ref[jj, 0:1, :] += _sum0(dy2 * upc)
                dcw_ref[jj, 1:2, :] += _sum0(dy1 * upc)
                dcw_ref[jj, 2:3, :] += _sum0(dy * upc)
                dup = _bf(dy * cw[2:3, :] + dy1 * cw[1:2, :] + dy2 * cw[0:1, :])
                dup_ref[jj] = dup
                dh2acc[...] += _mm_nt(dup, wup_ref[jj])
        dh2 = dh2acc[...]
        dmod_ref[0:1, :] += _sum0(dh2)
        dmod_ref[1:2, :] += _sum0(dh2 * n3)
        dn3 = dh2 * (1.0 + sc)
        dg3_ref[...] += _sum0(dn3 * xh)
        dxh = dn3 * g3
        dx1_ref[...] = dx2v + rstd3 * (dxh - xh * _rowmean(dxh * xh))

    tile = pl.BlockSpec((ts, D), lambda i: (nt - 1 - i, 0))
    chunked = lambda n: pl.BlockSpec((n, ts, FF_CHUNK), lambda i: (0, nt - 1 - i, 0))
    halo = pl.BlockSpec((N_DEV, HALO, FF_CHUNK), lambda i: (0, jnp.maximum((nt - 1 - i) * (ts // HALO) - 1, 0), 0))
    const = lambda *shape: pl.BlockSpec(shape, lambda i: (0,) * len(shape))
    return pl.pallas_call(
        body,
        name="ffn_bwd",
        grid=(nt,),
        out_shape=(jax.ShapeDtypeStruct((t_len, D), F32), jax.ShapeDtypeStruct((N_DEV, t_len, FF_CHUNK), BF16),
                   jax.ShapeDtypeStruct((half, t_len, FF_CHUNK), BF16), jax.ShapeDtypeStruct((t_len, D), BF16),
                   jax.ShapeDtypeStruct((t_len, D), BF16), jax.ShapeDtypeStruct((3, D), F32),
                   jax.ShapeDtypeStruct((1, D), F32), jax.ShapeDtypeStruct((1, D), F32),
                   jax.ShapeDtypeStruct((N_DEV, FF_CHUNK), F32), jax.ShapeDtypeStruct((N_DEV, 3, FF_CHUNK), F32)),
        in_specs=[tile, tile, tile, chunked(N_DEV), halo] + [VMEM_SPEC] * 7,
        out_specs=(tile, chunked(N_DEV), chunked(half), tile, tile, const(3, D), const(1, D), const(1, D),
                   const(N_DEV, FF_CHUNK), const(N_DEV, 3, FF_CHUNK)),
        scratch_shapes=[pltpu.VMEM((N_DEV, HALO, FF_CHUNK), F32), pltpu.VMEM((ts, D), F32)],
        compiler_params=pltpu.CompilerParams(dimension_semantics=("arbitrary",), vmem_limit_bytes=VMEM_LIMIT_V7X),
    )(dx2, f, x1, up, up, mod, g_pre, g_post, w_up_b, conv_w8, conv_b8, w_down_b)


def _wgrad_up(h2, dup, ts):
    t_len = h2.shape[0]
    nt, half = t_len // ts, N_DEV // 2

    def body(h2_ref, dup_ref, out_ref):
        @pl.when(pl.program_id(1) == 0)
        def _():
            out_ref[...] = jnp.zeros(out_ref.shape, F32)

        for q in range(half):
            out_ref[q] += _mm_tn(h2_ref[...], dup_ref[q])

    return pl.pallas_call(
        body,
        name="wgrad_up",
        grid=(2, nt),
        out_shape=jax.ShapeDtypeStruct((N_DEV, D, FF_CHUNK), F32),
        in_specs=[pl.BlockSpec((ts, D), lambda g, t: (t, 0)), pl.BlockSpec((half, ts, FF_CHUNK), lambda g, t: (g, t, 0))],
        out_specs=pl.BlockSpec((half, D, FF_CHUNK), lambda g, t: (g, 0, 0)),
        compiler_params=pltpu.CompilerParams(dimension_semantics=("arbitrary", "arbitrary"),
                                             vmem_limit_bytes=VMEM_LIMIT_V7X),
    )(h2, dup)


def _sibling_swap_copies(srcs, dsts, send_sems, recv_sems):
    x, y, c = _coords()
    return [
        pltpu.make_async_remote_copy(srcs[a].at[xs, ys, 1 - c], dsts[a].at[xs, ys], send_sems.at[a, 2 * xs + ys],
                                     recv_sems.at[a, 2 * xs + ys], device_id=(x, y, 1 - c), device_id_type=MESH)
        for a in range(len(srcs)) for xs in range(2) for ys in range(2)
    ]


def _wgrad_down(act, df, ts, swap_src):
    t_len = df.shape[0]
    nt, half = t_len // ts, N_DEV // 2

    def body(act_ref, df_ref, src_ref, out_ref, dst_ref, send_sems, recv_sems):
        t = pl.program_id(0)

        @pl.when(t == 0)
        def _():
            for cp in _sibling_swap_copies([src_ref], [dst_ref], send_sems, recv_sems):
                cp.start()
            out_ref[...] = jnp.zeros(out_ref.shape, F32)

        for q in range(half):
            out_ref[q] += _mm_tn(act_ref[q], df_ref[...])

        @pl.when(t == nt - 1)
        def _():
            for cp in _sibling_swap_copies([src_ref], [dst_ref], send_sems, recv_sems):
                cp.wait()

    return pl.pallas_call(
        body,
        name="wgrad_down",
        grid=(nt,),
        out_shape=(jax.ShapeDtypeStruct((half, FF_CHUNK, D), F32), jax.ShapeDtypeStruct(swap_src.shape[1:], F32)),
        in_specs=[pl.BlockSpec((half, ts, FF_CHUNK), lambda t: (0, t, 0)), pl.BlockSpec((ts, D), lambda t: (t, 0)),
                  ANY_SPEC],
        out_specs=(pl.BlockSpec((half, FF_CHUNK, D), lambda t: (0, 0, 0)), ANY_SPEC),
        scratch_shapes=[pltpu.SemaphoreType.DMA((1, 4)), pltpu.SemaphoreType.DMA((1, 4))],
        compiler_params=pltpu.CompilerParams(dimension_semantics=("arbitrary",), vmem_limit_bytes=VMEM_LIMIT_V7X),
    )(act, df, swap_src)


def _mix_bwd(dx1, x, proj, mixed, mod, g_pre, g_post, w_in_b, sgn, w_sp, b_sp_t, w_pool, p_scale, w_out_b, ts, rs_srcs):
    t_len = x.shape[0]
    nt, nb = t_len // ts, ts // HEAD
    nr = len(rs_srcs)

    def body(*refs):
        (dx1_ref, x_ref, proj_ref, halo_ref, mixed_ref, mod_ref, g1_ref, g2_ref, win_ref, sgn_ref, ws_ref,
         bst_ref, wp_ref, ps_ref, wout_ref) = refs[:15]
        (gx_ref, dwin_ref, dwout_ref, dmod_ref, dg1_ref, dg2_ref, dsgn_ref, dws_ref, dbst_ref, dwp_ref,
         dps_ref) = refs[15 + nr:26 + nr]
        pbuf, dwsbuf, cat, dproj, dcat = refs[26 + 2 * nr:31 + 2 * nr]
        exchange = _ChipExchangeSteps(refs[15:15 + nr], refs[26 + nr:26 + 2 * nr], *refs[31 + 2 * nr:])
        i = pl.program_id(0)
        r = nt - 1 - i

        @pl.when(i == 0)
        def _():
            exchange.start()
            for ref in (dwin_ref, dwout_ref, dmod_ref, dg1_ref, dg2_ref, dsgn_ref, dws_ref, dbst_ref, dwp_ref, dps_ref):
                ref[...] = jnp.zeros(ref.shape, F32)
            dwsbuf[ts:ts + POOL_HALO, :] = jnp.zeros((POOL_HALO, B_WIDTH), F32)

        xv, dx1v, mixed = x_ref[...], dx1_ref[...], mixed_ref[...]
        sh, sc, gm = mod_ref[0:1, :], mod_ref[1:2, :], mod_ref[2:3, :]
        g1, g2 = g1_ref[...], g2_ref[...]
        rstd2 = _rstd(mixed)
        mh = mixed * rstd2
        dmod_ref[2:3, :] += _sum0(dx1v * (mh * g2))
        dr = dx1v * gm
        dg2_ref[...] += _sum0(dr * mh)
        dmh = dr * g2
        dmb = _bf(rstd2 * (dmh - mh * _rowmean(dmh * mh)))
        dcat[...] = _mm_nt(dmb, wout_ref[...])
        smask = _sgu_mask()
        for hd in range(N_HEAD):
            ucols = slice(hd * HEAD, (hd + 1) * HEAD)
            vcols = slice(A_WIDTH + hd * HEAD, A_WIDTH + (hd + 1) * HEAD)
            u, du_dp = _gelu_and_grad(proj_ref[:, ucols])
            v, dv_dp = _gelu_and_grad(proj_ref[:, vcols])
            rs = _rstd(v)
            vhat = v * rs
            gn = sgn_ref[hd:hd + 1, :]
            vn = _bf(vhat * gn)
            wm = _bf(jnp.where(smask, ws_ref[hd], 0.0))
            bias = bst_ref[:, hd:hd + 1]
            dzsum = jnp.zeros((HEAD, HEAD), F32)
            dwm = jnp.zeros((HEAD, HEAD), F32)
            dvn_parts = []
            for b in range(nb):
                rows = slice(b * HEAD, (b + 1) * HEAD)
                z = _mm(wm, vn[rows]) + bias
                da = dcat[rows, ucols]
                cat[rows, ucols] = _bf(u[rows] * z)
                dz = da * u[rows]
                dzsum = dzsum + dz
                dzb = _bf(dz)
                dwm = dwm + _mm_nt(dzb, vn[rows])
                dvn_parts.append(_mm_tn(wm, dzb))
                dproj[rows, ucols] = (da * z) * du_dp[rows]
            dvn = jnp.concatenate(dvn_parts, axis=0)
            dsgn_ref[hd:hd + 1, :] += _sum0(dvn * vhat)
            dvh = dvn * gn
            dproj[:, vcols] = (rs * (dvh - vhat * _rowmean(dvh * vhat))) * dv_dp
            dws_ref[hd] += jnp.where(smask, dwm, 0.0)
            dbst_ref[:, hd:hd + 1] += jnp.sum(dzsum, axis=1, keepdims=True)
        keep = jnp.where(r > 0, 1.0, 0.0).astype(F32)
        pbuf[0:POOL_HALO, :] = halo_ref[...] * keep
        pbuf[POOL_HALO:POOL_HALO + ts, :] = proj_ref[:, 2 * A_WIDTH:]
        for g, w in enumerate(WINDOWS):
            cols = slice(g * HEAD, (g + 1) * HEAD)
            ccols = slice(A_WIDTH + g * HEAD, A_WIDTH + (g + 1) * HEAD)
            pcols = slice(2 * A_WIDTH + g * HEAD, 2 * A_WIDTH + (g + 1) * HEAD)
            wpg = _bf(wp_ref[g])
            psg = ps_ref[:, cols]
            ext = pbuf[:, cols]
            inv = _inv_count(r * ts, ts, w)
            pb = _bf(_window_sum(ext, w, True)[POOL_HALO:] * inv - ext[POOL_HALO:])
            yb = _mm(pb, wpg)
            dob = dcat[:, ccols]
            cat[:, ccols] = _bf(yb * psg)
            dps_ref[:, cols] += _sum0(dob * yb)
            dyb = _bf(dob * psg)
            dwp_ref[g] += _mm_tn(pb, dyb)
            dpooled = _mm_nt(dyb, wpg)
            dwsbuf[0:ts, cols] = dpooled * inv
            dproj[:, pcols] = _window_sum(dwsbuf[:, cols], w, False)[0:ts] - dpooled
        dwsbuf[ts:ts + POOL_HALO, :] = dwsbuf[0:POOL_HALO, :]
        dpb = _bf(dproj[...])
        rstd1 = _rstd(xv)
        xh = xv * rstd1
        n1 = xh * g1
        dwin_ref[...] += _mm_tn(_bf(n1 * (1.0 + sc) + sh), dpb)
        dwout_ref[...] += _mm_tn(cat[...], dmb)
        dh = _mm_nt(dpb, win_ref[...])
        dmod_ref[0:1, :] += _sum0(dh)
        dmod_ref[1:2, :] += _sum0(dh * n1)
        dn1 = dh * (1.0 + sc)
        dg1_ref[...] += _sum0(dn1 * xh)
        dxh = dn1 * g1
        gx_ref[...] = dx1v + rstd1 * (dxh - xh * _rowmean(dxh * xh))

        @pl.when(i == nt - 1)
        def _():
            exchange.finish()

    tile = lambda wid: pl.BlockSpec((ts, wid), lambda i: (nt - 1 - i, 0))
    halo = pl.BlockSpec((POOL_HALO, B_WIDTH),
                        lambda i: (jnp.maximum((nt - 1 - i) * (ts // POOL_HALO) - 1, 0), 2 * A_WIDTH // B_WIDTH))
    const = lambda *shape: pl.BlockSpec(shape, lambda i: (0,) * len(shape))
    outs = pl.pallas_call(
        body,
        name="mix_bwd",
        grid=(nt,),
        out_shape=(jax.ShapeDtypeStruct((t_len, D), F32), jax.ShapeDtypeStruct((D, IN_WIDTH), F32),
                   jax.ShapeDtypeStruct((D, D), F32), jax.ShapeDtypeStruct((3, D), F32),
                   jax.ShapeDtypeStruct((1, D), F32), jax.ShapeDtypeStruct((1, D), F32),
                   jax.ShapeDtypeStruct((N_HEAD, HEAD), F32), jax.ShapeDtypeStruct((N_HEAD, HEAD, HEAD), F32),
                   jax.ShapeDtypeStruct((HEAD, N_HEAD), F32), jax.ShapeDtypeStruct((N_HEAD, HEAD, HEAD), F32),
                   jax.ShapeDtypeStruct((1, B_WIDTH), F32))
        + tuple(jax.ShapeDtypeStruct((3, *s.shape[2:]), s.dtype) for s in rs_srcs),
        in_specs=[tile(D), tile(D), tile(IN_WIDTH), halo, tile(D)] + [VMEM_SPEC] * 10 + [ANY_SPEC] * nr,
        out_specs=(tile(D), const(D, IN_WIDTH), const(D, D), const(3, D), const(1, D), const(1, D),
                   const(N_HEAD, HEAD), const(N_HEAD, HEAD, HEAD), const(HEAD, N_HEAD), const(N_HEAD, HEAD, HEAD),
                   const(1, B_WIDTH)) + (ANY_SPEC,) * nr,
        scratch_shapes=[pltpu.VMEM((POOL_HALO + ts, B_WIDTH), F32), pltpu.VMEM((ts + POOL_HALO, B_WIDTH), F32),
                        pltpu.VMEM((ts, D), BF16), pltpu.VMEM((ts, IN_WIDTH), F32), pltpu.VMEM((ts, D), F32),
                        pltpu.SemaphoreType.DMA((3 * nr,)), pltpu.SemaphoreType.DMA((3 * nr,))],
        compiler_params=pltpu.CompilerParams(dimension_semantics=("arbitrary",), vmem_limit_bytes=VMEM_LIMIT_V7X),
    )(dx1, x, proj, proj, mixed, mod, g_pre, g_post, w_in_b, sgn, w_sp, b_sp_t, w_pool, p_scale, w_out_b, *rs_srcs)
    return outs[:11], outs[11:]


def _pair_add(name, coords, grid, specs_a, specs_b, out_specs, out_shapes, a_arrays, b_arrays):
    n = len(a_arrays)

    def body(co_ref, *refs):
        for k in range(n):
            refs[2 * n + k][...] = refs[k][...] + refs[n + k][...]

    return list(pl.pallas_call(
        body,
        name=name,
        grid_spec=pltpu.PrefetchScalarGridSpec(num_scalar_prefetch=1, grid=grid, in_specs=specs_a + specs_b,
                                               out_specs=out_specs),
        out_shape=tuple(jax.ShapeDtypeStruct(s, F32) for s in out_shapes),
        compiler_params=pltpu.CompilerParams(dimension_semantics=("arbitrary",) * len(grid),
                                             vmem_limit_bytes=VMEM_LIMIT_V7X),
    )(coords, *a_arrays, *b_arrays))


def _final_add_adamw(coords, s1, r, ws, ms, vs, n_split=4):
    n = len(s1)

    def body(co_ref, *refs):
        for k in range(n):
            s_ref, r_ref, w_ref, m_ref, v_ref = (refs[q * n + k] for q in range(5))
            g_ref, d_ref, nm_ref, nv_ref = (refs[(5 + q) * n + k] for q in range(4))
            g = ((s_ref[...] + r_ref[0]) + r_ref[1]) + r_ref[2]
            g_ref[...] = g
            delta, m, v = _adamw(w_ref[...], g, m_ref[...], v_ref[...])
            d_ref[...] = delta
            nm_ref[...] = m
            nv_ref[...] = v

    def shard_spec(a):
        rows, cols = a.shape
        return pl.BlockSpec((rows // n_split, cols), lambda i, co: (i, 0))

    def mine_spec(a):
        rows, cols = a.shape[2:]
        return pl.BlockSpec((None, None, rows // n_split, cols), lambda i, co: (co[0], co[1], i, 0))

    def recv_spec(a):
        rows, cols = a.shape[1:]
        return pl.BlockSpec((3, rows // n_split, cols), lambda i, co: (0, i, 0))

    in_specs = ([mine_spec(a) for a in s1] + [recv_spec(a) for a in r] + [shard_spec(a) for a in ws] * 3)
    out_specs = [shard_spec(a) for a in ws] * 4
    outs = pl.pallas_call(
        body,
        name="grad_final_adamw",
        grid_spec=pltpu.PrefetchScalarGridSpec(num_scalar_prefetch=1, grid=(n_split,), in_specs=in_specs,
                                               out_specs=out_specs),
        out_shape=tuple(jax.ShapeDtypeStruct(a.shape, F32) for a in ws) * 4,
        compiler_params=pltpu.CompilerParams(dimension_semantics=("arbitrary",), vmem_limit_bytes=VMEM_LIMIT_V7X),
    )(coords, *s1, *r, *ws, *ms, *vs)
    return [tuple(outs[q * n + k] for q in range(4)) for k in range(n)]


def _sibling_swap(tag, g5):
    n = len(g5)

    def body(*refs):
        copies = _sibling_swap_copies(refs[:n], refs[n:2 * n], *refs[2 * n:])
        for cp in copies:
            cp.start()
        for cp in copies:
            cp.wait()

    return list(pl.pallas_call(
        body,
        name="grad_swap_core_" + tag,
        out_shape=tuple(jax.ShapeDtypeStruct(g.shape[1:], F32) for g in g5),
        in_specs=[ANY_SPEC] * n,
        out_specs=(ANY_SPEC,) * n,
        scratch_shapes=[pltpu.SemaphoreType.DMA((n, 4)), pltpu.SemaphoreType.DMA((n, 4))],
    )(*g5))


def _sibling_add(tag, g5, r1, coords, n_split=4):
    shapes = [g.shape[3:] for g in g5]
    spec_g = [pl.BlockSpec((None, None, None, s[0] // n_split, s[1]), lambda i, j, k, co: (i, j, co[2], k, 0))
              for s in shapes]
    spec_r = [pl.BlockSpec((None, None, s[0] // n_split, s[1]), lambda i, j, k, co: (i, j, k, 0)) for s in shapes]
    return _pair_add("grad_add_core_" + tag, coords, (2, 2, n_split), spec_g, spec_r, spec_r,
                     [(2, 2, *s) for s in shapes], g5, r1)


def _tail_exchange(big, partials, pick_mine, dmod3):
    n, nb = len(partials), len(big)
    big_shapes = [g.shape[1:] for g in big]
    big5 = [g.reshape(2, 2, 2, *s) for g, s in zip(big, big_shapes)]
    flips = _ChipExchangeSteps.FLIPS

    def body(*refs):
        g5, p_in, dm_ref = refs[:nb], refs[nb:nb + n], refs[nb + n]
        outs = refs[nb + n + 1:2 * nb + 2 * n + 2]
        g_out, sums, dm2d = outs[:nb], outs[nb:nb + n], outs[nb + n]
        scratch = refs[2 * nb + 2 * n + 2:]
        s1, stage, chip_recv = scratch[:nb], scratch[nb:2 * nb], scratch[2 * nb:3 * nb]
        acc, rbuf = scratch[3 * nb:3 * nb + n], scratch[3 * nb + n:3 * nb + 2 * n]
        (dm_recv, send_sems, recv_sems, dm_send_sems, dm_recv_sems, sib_send, sib_recv, chip_send,
         chip_recv_sems) = scratch[3 * nb + 2 * n:]
        x, y, c = _coords()
        me = 4 * x + 2 * y + c
        sibling = (x, y, 1 - c)
        dm_copies = [
            pltpu.make_async_remote_copy(dm_ref.at[me ^ k], dm_recv.at[k], dm_send_sems.at[k], dm_recv_sems.at[k],
                                         device_id=_peer(k), device_id_type=MESH)
            for k in range(1, N_DEV)
        ]
        for cp in dm_copies:
            cp.start()
        sib_copies = _sibling_swap_copies(g5, s1, sib_send, sib_recv)
        for cp in sib_copies:
            cp.start()
        for a in range(n):
            acc[a][...] = p_in[a][...]

        def small_phase(ph, peer):
            copies = [
                pltpu.make_async_remote_copy(acc[a], rbuf[a].at[ph], send_sems.at[ph, a], recv_sems.at[ph, a],
                                             device_id=peer, device_id_type=MESH)
                for a in range(n)
            ]
            for cp in copies:
                cp.start()
            for cp in copies:
                cp.wait()
            for a in range(n):
                acc[a][...] = acc[a][...] + rbuf[a][ph]

        small_phase(0, sibling)
        for cp in sib_copies:
            cp.wait()
        for a in range(nb):
            for xs in range(2):
                for ys in range(2):
                    total = g5[a][xs, ys, c] + s1[a][xs, ys]
                    s1[a][xs, ys] = total
                    stage[a][xs, ys] = _bf(total)
        chip_copies = [
            pltpu.make_async_remote_copy(stage[a].at[x ^ fx, y ^ fy], chip_recv[a].at[j], chip_send.at[a, j],
                                         chip_recv_sems.at[a, j], device_id=(x ^ fx, y ^ fy, c), device_id_type=MESH)
            for a in range(nb) for j, (fx, fy) in enumerate(flips)
        ]
        for cp in chip_copies:
            cp.start()
        small_phase(1, (1 - x, y, c))
        small_phase(2, (x, 1 - y, c))
        for a in range(n):
            sums[a][...] = acc[a][me] if pick_mine[a] else acc[a][...]
        dm2d[...] = jnp.zeros(dm2d.shape, F32)
        dm2d[0:1, :] = dm_ref[me]
        for cp in dm_copies:
            cp.wait()
        for k in range(1, N_DEV):
            dm2d[k:k + 1, :] = dm_recv[k]
        for cp in chip_copies:
            cp.wait()
        for a in range(nb):
            g_out[a][...] = ((s1[a][x, y] + chip_recv[a][0].astype(F32)) + chip_recv[a][1].astype(F32)) \
                + chip_recv[a][2].astype(F32)

    out_shapes = tuple(jax.ShapeDtypeStruct(s, F32) for s in big_shapes) + tuple(
        jax.ShapeDtypeStruct(p.shape[1:] if pk else p.shape, F32) for p, pk in zip(partials, pick_mine))
    outs = pl.pallas_call(
        body,
        name="tail_exchange",
        out_shape=out_shapes + (jax.ShapeDtypeStruct((2 * N_DEV, MOD_COLS), F32),),
        in_specs=[VMEM_SPEC] * (nb + n + 1),
        out_specs=(VMEM_SPEC,) * (nb + n + 1),
        scratch_shapes=[pltpu.VMEM((2, 2, *s), F32) for s in big_shapes]
        + [pltpu.VMEM((2, 2, *s), BF16) for s in big_shapes]
        + [pltpu.VMEM((3, *s), BF16) for s in big_shapes]
        + [pltpu.VMEM(p.shape, F32) for p in partials]
        + [pltpu.VMEM((3, *p.shape), F32) for p in partials]
        + [pltpu.VMEM((N_DEV, 1, MOD_COLS), F32), pltpu.SemaphoreType.DMA((3, n)), pltpu.SemaphoreType.DMA((3, n)),
           pltpu.SemaphoreType.DMA((N_DEV,)), pltpu.SemaphoreType.DMA((N_DEV,)),
           pltpu.SemaphoreType.DMA((nb, 4)), pltpu.SemaphoreType.DMA((nb, 4)),
           pltpu.SemaphoreType.DMA((nb, 3)), pltpu.SemaphoreType.DMA((nb, 3))],
        compiler_params=pltpu.CompilerParams(vmem_limit_bytes=VMEM_LIMIT_V7X),
    )(*big5, *partials, dmod3)
    return list(outs[:nb]), list(outs[nb:nb + n]), outs[nb + n]


def _small_update(grads, ws, ms, vs, scx, dm2d, w_ada, m_ada, v_ada, loss_lanes):
    n = len(grads)

    def body(*refs):
        g_in, w_in, m_in, v_in = (refs[q * n:(q + 1) * n] for q in range(4))
        scx_ref, dm_ref, wa_ref, ma_ref, va_ref, ll_ref = refs[4 * n:4 * n + 6]
        outs = refs[4 * n + 6:]
        g_out, d_out, nm_out, nv_out = (outs[q * (n + 1):(q + 1) * (n + 1)] for q in range(4))
        loss_ref = outs[4 * (n + 1)]
        for a in range(n + 1):
            if a < n:
                g, w, m, v = g_in[a][...], w_in[a][...], m_in[a][...], v_in[a][...]
            else:
                g = _mm_tn(_bf(scx_ref[...]), _bf(dm_ref[...]))
                w, m, v = wa_ref[...], ma_ref[...], va_ref[...]
            g_out[a][...] = g
            delta, m, v = _adamw(w, g, m, v)
            d_out[a][...] = delta
            nm_out[a][...] = m
            nv_out[a][...] = v
        loss_ref[...] = jnp.sum(ll_ref[...], axis=1, keepdims=True) * (0.5 / D)

    w_shapes = tuple(jax.ShapeDtypeStruct(w.shape, F32) for w in list(ws) + [w_ada])
    outs = pl.pallas_call(
        body,
        name="small_update",
        out_shape=w_shapes * 4 + (jax.ShapeDtypeStruct((1, 1), F32),),
        in_specs=[VMEM_SPEC] * (4 * n + 6),
        out_specs=(VMEM_SPEC,) * (4 * (n + 1) + 1),
        compiler_params=pltpu.CompilerParams(vmem_limit_bytes=VMEM_LIMIT_V7X),
    )(*grads, *ws, *ms, *vs, scx, dm2d, w_ada, m_ada, v_ada, loss_lanes)
    return [tuple(outs[q * (n + 1) + k] for q in range(4)) for k in range(n + 1)], outs[4 * (n + 1)]


def kernel(x, c, w_ada, b_ada, pre_mix_g, post_mix_g, w_in, sgu_norm_g, w_spatial, b_spatial, w_pool, pool_scale, w_out, pre_ffn_g, post_ffn_g, w_up, conv_w, conv_b, w_down, loss_target, m_w_ada, m_b_ada, m_pre_mix_g, m_post_mix_g, m_w_in, m_sgu_norm_g, m_w_spatial, m_b_spatial, m_w_pool, m_pool_scale, m_w_out, m_pre_ffn_g, m_post_ffn_g, m_w_up, m_conv_w, m_conv_b, m_w_down, v_w_ada, v_b_ada, v_pre_mix_g, v_post_mix_g, v_w_in, v_sgu_norm_g, v_w_spatial, v_b_spatial, v_w_pool, v_pool_scale, v_w_out, v_pre_ffn_g, v_post_ffn_g, v_w_up, v_conv_w, v_conv_b, v_w_down):
    t_len = x.shape[1]
    ts = min(256, t_len)
    ts_w = min(1024, t_len)
    coords = jnp.stack([lax.axis_index("x"), lax.axis_index("y"), lax.axis_index("c")]).astype(jnp.int32)

    mod3, scx, (g_in, g_out) = _prologue(c, w_ada[0], b_ada.reshape(N_DEV, 1, MOD_COLS), [w_in[0], w_out[0]],
                                         [BF16, BF16])
    mod = mod3.reshape(N_MOD, D)
    w_in_b = g_in.transpose(1, 0, 2).reshape(D, IN_WIDTH)
    w_out_b = g_out.reshape(D, D)
    conv_b8 = conv_b.reshape(N_DEV, FF_CHUNK)
    b_sp_t = b_spatial[0].T

    x2d, tgt = x[0], loss_target[0]
    (x1, proj, mixed), (g_up, g_down, g_cw) = _mix_fwd(
        x2d, mod, pre_mix_g, post_mix_g, w_in_b, sgu_norm_g[0], w_spatial[0], b_sp_t, w_pool[0], pool_scale, w_out_b,
        ts, [w_up[0], w_down[0], conv_w[0]], [BF16, BF16, F32])
    w_down_b = g_down.reshape(FF, D)
    up, f, dx2, loss_lanes = _ffn_fwd(x1, tgt, mod, pre_ffn_g, post_ffn_g, g_up, g_cw, conv_b8, w_down_b, ts)

    (dx1, dup, act, df, h2, dmod_f, d_pre_ffn, d_post_ffn, d_cb8, d_cw8) = _ffn_bwd(
        dx2, f, x1, up, mod, pre_ffn_g, post_ffn_g, g_up, g_cw, conv_b8, w_down_b, ts)
    gw_up = _wgrad_up(h2, dup, ts_w).reshape(2, 2, 2, D, FF_CHUNK)
    gw_down, r1_up = _wgrad_down(act, df, ts_w, gw_up)
    gw_down = gw_down.reshape(2, 2, 2, FF // N_DEV, D)
    s1_ffn = _sibling_add("ffn", [gw_up, gw_down], [r1_up] + _sibling_swap("down", [gw_down]), coords)
    ((grad_x, gw_in, gw_out, dmod_m, d_pre_mix, d_post_mix, d_sgn, d_wsp, d_bsp_t, d_wpool, d_ps), r_ffn) = _mix_bwd(
        dx1, x2d, proj, mixed, mod, pre_mix_g, post_mix_g, w_in_b, sgu_norm_g[0], w_spatial[0], b_sp_t,
        w_pool[0], pool_scale, w_out_b, ts, s1_ffn)
    gw_in = gw_in.reshape(D, N_DEV, IN_WIDTH // N_DEV).transpose(1, 0, 2)
    gw_out = gw_out.reshape(N_DEV, D // N_DEV, D)

    big = _final_add_adamw(coords, s1_ffn, list(r_ffn), [w_up[0], w_down[0]], [m_w_up[0], m_w_down[0]],
                           [v_w_up[0], v_w_down[0]])
    r_up, r_down = [tuple(a[None] for a in four) for four in big]

    dmod = jnp.concatenate([dmod_m, dmod_f], axis=0)
    names = ["b_ada", "pre_mix_g", "post_mix_g", "sgu_norm_g", "w_spatial", "b_spatial", "w_pool", "pool_scale",
             "pre_ffn_g", "post_ffn_g", "conv_w", "conv_b"]
    partials = [dmod.reshape(1, N_MOD * D), d_pre_mix, d_post_mix, d_sgn, d_wsp, d_bsp_t.T, d_wpool, d_ps,
                d_pre_ffn, d_post_ffn, d_cw8, d_cb8.reshape(1, 2 * FF), loss_lanes]
    small_w = [b_ada, pre_mix_g, post_mix_g, sgu_norm_g[0], w_spatial[0], b_spatial[0], w_pool[0], pool_scale,
               pre_ffn_g, post_ffn_g, conv_w[0], conv_b]
    small_m = [m_b_ada, m_pre_mix_g, m_post_mix_g, m_sgu_norm_g[0], m_w_spatial[0], m_b_spatial[0], m_w_pool[0],
               m_pool_scale, m_pre_ffn_g, m_post_ffn_g, m_conv_w[0], m_conv_b]
    small_v = [v_b_ada, v_pre_mix_g, v_post_mix_g, v_sgu_norm_g[0], v_w_spatial[0], v_b_spatial[0], v_w_pool[0],
               v_pool_scale, v_pre_ffn_g, v_post_ffn_g, v_conv_w[0], v_conv_b]
    g_mix, sums, dm2d = _tail_exchange([gw_in, gw_out], partials, [nm == "conv_w" for nm in names] + [False],
                                       dmod.reshape(N_DEV, 1, MOD_COLS))
    small, loss11 = _small_update(
        sums[:-1] + g_mix, small_w + [w_in[0], w_out[0]], small_m + [m_w_in[0], m_w_out[0]],
        small_v + [v_w_in[0], v_w_out[0]], scx, dm2d, w_ada[0], m_w_ada[0], v_w_ada[0], sums[-1])
    loss = loss11.reshape(())
    lead = {"sgu_norm_g", "w_spatial", "b_spatial", "w_pool", "conv_w", "w_in", "w_out", "w_ada"}
    res = {nm: tuple(a[None] if nm in lead else a for a in four)
           for nm, four in zip(names + ["w_in", "w_out", "w_ada"], small)}
    res.update(w_up=r_up, w_down=r_down)

    order = ["w_ada", "b_ada", "pre_mix_g", "post_mix_g", "w_in", "sgu_norm_g", "w_spatial", "b_spatial", "w_pool",
             "pool_scale", "w_out", "pre_ffn_g", "post_ffn_g", "w_up", "conv_w", "conv_b", "w_down"]
    return (loss, grad_x[None], *[res[nm][0] for nm in order], *[res[nm][1] for nm in order],
            *[res[nm][2] for nm in order], *[res[nm][3] for nm in order])
```

```python
import functools
import math

import jax
import jax.numpy as jnp
from jax import lax
from jax.experimental import pallas as pl
from jax.experimental.pallas import tpu as pltpu

F32 = jnp.float32
BF16 = jnp.bfloat16
MESH = pl.DeviceIdType.MESH

EPS = 1e-6
D = 1024
HEAD = 128
N_HEAD = 4
A_WIDTH = 512
B_WIDTH = 512
IN_WIDTH = 1536
WINDOWS = (2, 4, 8, 16)
CHUNK = 64
FF = 2816
N_DEV = 8
FF_CHUNK = 704
N_MOD = 6
MOD_COLS = 768

ADAM_LR = 0.001
ADAM_B1 = 0.9
ADAM_B2 = 0.999
ADAM_EPS = 1e-08
ADAM_WD = 0.01
ADAM_STEP = 10

VMEM_LIMIT_V7X = 60 * 1024 * 1024
HALO = 8
POOL_HALO = 16

VMEM_SPEC = pl.BlockSpec(memory_space=pltpu.VMEM)
ANY_SPEC = pl.BlockSpec(memory_space=pl.ANY)


def _bf(x):
    return x.astype(BF16)


def _mm(a, b):
    return jnp.dot(a, b, preferred_element_type=F32)


def _mm_nt(a, b):
    return lax.dot_general(a, b, (((1,), (1,)), ((), ())), preferred_element_type=F32)


def _mm_tn(a, b):
    return lax.dot_general(a, b, (((0,), (0,)), ((), ())), preferred_element_type=F32)


def _rstd(x):
    return lax.rsqrt(jnp.mean(x * x, axis=-1, keepdims=True) + EPS)


def _sum0(x):
    return jnp.sum(x, axis=0, keepdims=True)


def _rowmean(x):
    return jnp.mean(x, axis=-1, keepdims=True)


_GELU_K = math.sqrt(2.0 / math.pi)


def _gelu_and_grad(x):
    x2 = x * x
    th = jnp.tanh(_GELU_K * (x + 0.044715 * (x * x2)))
    cdf = 0.5 * (1.0 + th)
    grad = cdf + 0.5 * x * (1.0 - th * th) * (_GELU_K * (1.0 + 3.0 * 0.044715 * x2))
    return x * cdf, grad


def _gelu(x):
    return x * (0.5 * (1.0 + jnp.tanh(_GELU_K * (x + 0.044715 * (x * x * x)))))


def _sigmoid(x):
    return 0.5 * jnp.tanh(0.5 * x) + 0.5


def _sgu_mask():
    ri = lax.broadcasted_iota(jnp.int32, (HEAD, HEAD), 0)
    ci = lax.broadcasted_iota(jnp.int32, (HEAD, HEAD), 1)
    return (ci // CHUNK) <= (ri // CHUNK)


def _window_sum(ext, w, trailing):
    n = ext.shape[0]
    s, k = ext, 1
    while k < w:
        s = s + pltpu.roll(s, k if trailing else n - k, 0)
        k *= 2
    return s


def _inv_count(row0, n, w):
    t = row0 + lax.broadcasted_iota(jnp.int32, (n, 1), 0)
    return 1.0 / jnp.minimum(t + 1, w).astype(F32)


def _shift_down(v, before, k):
    rows = lax.broadcasted_iota(jnp.int32, before.shape, 0)
    r = pltpu.roll(v, k, 0)
    top = jnp.where(rows < k, pltpu.roll(before, k, 0), r[0:HALO])
    return jnp.concatenate([top, r[HALO:]], axis=0)


def _shift_up(v, after, k):
    n = v.shape[0]
    rows = lax.broadcasted_iota(jnp.int32, after.shape, 0)
    r = pltpu.roll(v, n - k, 0)
    bottom = jnp.where(rows >= HALO - k, pltpu.roll(after, HALO - k, 0), r[n - HALO:])
    return jnp.concatenate([r[:n - HALO], bottom], axis=0)


def _adamw(w, g, m, v):
    m = ADAM_B1 * m + (1.0 - ADAM_B1) * g
    v = ADAM_B2 * v + (1.0 - ADAM_B2) * (g * g)
    m_hat = m / (1.0 - ADAM_B1 ** ADAM_STEP)
    v_hat = v / (1.0 - ADAM_B2 ** ADAM_STEP)
    delta = -ADAM_LR * (m_hat / (jnp.sqrt(v_hat) + ADAM_EPS) + ADAM_WD * w)
    return delta, m, v


def _coords():
    return lax.axis_index("x"), lax.axis_index("y"), lax.axis_index("c")


def _peer(k):
    x, y, c = _coords()
    return (x ^ ((k >> 2) & 1), y ^ ((k >> 1) & 1), c ^ (k & 1))


def _my_index():
    x, y, c = _coords()
    return 4 * x + 2 * y + c


def _adaln_modulation(c_ref, w_ref, b_ref, mod_ref, scx_ref, scbuf, stage, recv, send_sems, recv_sems):
    me = _my_index()
    cv = c_ref[...]
    scbuf[0] = cv * _sigmoid(cv)
    first = [
        pltpu.make_async_remote_copy(scbuf.at[0], scbuf.at[k], send_sems.at[0, k], recv_sems.at[0, k],
                                     device_id=_peer(k), device_id_type=MESH)
        for k in range(1, N_DEV)
    ]
    for cp in first:
        cp.start()
    for cp in first:
        cp.wait()
    scx_ref[...] = jnp.zeros(scx_ref.shape, F32)
    for k in range(N_DEV):
        scx_ref[k:k + 1, :] = scbuf[k]
    prod = _mm(_bf(scx_ref[...]), _bf(w_ref[...]))
    for k in range(N_DEV):
        stage[k] = prod[k:k + 1, :] + b_ref[me]
    second = [
        pltpu.make_async_remote_copy(stage.at[k], recv.at[k], send_sems.at[1, k], recv_sems.at[1, k],
                                     device_id=_peer(k), device_id_type=MESH)
        for k in range(1, N_DEV)
    ]
    for cp in second:
        cp.start()
    mod_ref[me] = stage[0]
    for cp in second:
        cp.wait()
    for k in range(1, N_DEV):
        mod_ref[me ^ k] = recv[k]


class _GatherSteps:
    def __init__(self, ins, outs, stages, send_sems, recv_sems, local_sems):
        self.ins, self.outs, self.stages = ins, outs, stages
        self.send_sems, self.recv_sems, self.local_sems = send_sems, recv_sems, local_sems
        x, y, c = _coords()
        self.c = c
        self.me, self.sibling = (x, y, c), (x, y, 1 - c)
        self.chips = [(1 - x, y), (x, 1 - y), (1 - x, 1 - y)]

    def _copy(self, a, k, block, to, from_stage=False):
        dst = self.outs[a].at[4 * block[0] + 2 * block[1] + block[2]]
        return pltpu.make_async_remote_copy(self.stages[a] if from_stage else dst, dst, self.send_sems.at[a, k],
                                            self.recv_sems.at[a, k], device_id=to, device_id_type=MESH)

    def _local(self, a):
        me = self.me
        return pltpu.make_async_copy(self.stages[a], self.outs[a].at[4 * me[0] + 2 * me[1] + me[2]],
                                     self.local_sems.at[a])

    def _first(self, a):
        cps = [self._copy(a, 0, self.me, self.sibling, from_stage=True)]
        return cps + [self._copy(a, 1 + j, self.me, (*chip, self.c), from_stage=True)
                      for j, chip in enumerate(self.chips)]

    def _passed(self, a, j):
        return self._copy(a, 4 + j, (*self.chips[j], self.c), self.sibling)

    def start(self):
        for a in range(len(self.ins)):
            self.stages[a][...] = self.ins[a][...].astype(self.stages[a].dtype)
            self._local(a).start()
            for cp in self._first(a):
                cp.start()

    def forward(self):
        for a in range(len(self.ins)):
            for j, chip in enumerate(self.chips):
                self._copy(a, 1 + j, (*chip, self.c), self.me).wait_recv()
                self._passed(a, j).start()

    def finish(self):
        for a in range(len(self.ins)):
            self._copy(a, 0, self.sibling, self.me).wait_recv()
            for j, chip in enumerate(self.chips):
                self._copy(a, 4 + j, (*chip, 1 - self.c), self.me).wait_recv()
            for cp in self._first(a) + [self._passed(a, j) for j in range(3)]:
                cp.wait_send()
            self._local(a).wait()


def _gather_scratch(shards, out_dtypes):
    n = len(shards)
    return ([pltpu.VMEM(s.shape, dt) for s, dt in zip(shards, out_dtypes)]
            + [pltpu.SemaphoreType.DMA((n, 7)), pltpu.SemaphoreType.DMA((n, 7)), pltpu.SemaphoreType.DMA((n,))])


def _gather_out_shapes(shards, out_dtypes):
    return tuple(jax.ShapeDtypeStruct((N_DEV, *s.shape), dt) for s, dt in zip(shards, out_dtypes))


def _prologue(c_row, w_ada, b_ada3, shards, out_dtypes):
    n = len(shards)

    def body(*refs):
        c_ref, w_ref, b_ref = refs[:3]
        mod_ref, scx_ref = refs[3 + n:5 + n]
        gather = _GatherSteps(refs[3:3 + n], refs[5 + n:5 + 2 * n], refs[5 + 2 * n:5 + 3 * n],
                              *refs[5 + 3 * n:8 + 3 * n])
        gather.start()
        _adaln_modulation(c_ref, w_ref, b_ref, mod_ref, scx_ref, *refs[8 + 3 * n:])
        gather.forward()
        gather.finish()

    outs = pl.pallas_call(
        body,
        name="prologue",
        out_shape=(jax.ShapeDtypeStruct((N_DEV, 1, MOD_COLS), F32), jax.ShapeDtypeStruct((2 * N_DEV, D), F32))
        + _gather_out_shapes(shards, out_dtypes),
        in_specs=[VMEM_SPEC] * (3 + n),
        out_specs=(VMEM_SPEC, VMEM_SPEC) + (ANY_SPEC,) * n,
        scratch_shapes=_gather_scratch(shards, out_dtypes) + [
            pltpu.VMEM((N_DEV, 1, D), F32),
            pltpu.VMEM((N_DEV, 1, MOD_COLS), F32),
            pltpu.VMEM((N_DEV, 1, MOD_COLS), F32),
            pltpu.SemaphoreType.DMA((2, N_DEV)),
            pltpu.SemaphoreType.DMA((2, N_DEV)),
        ],
        compiler_params=pltpu.CompilerParams(vmem_limit_bytes=VMEM_LIMIT_V7X),
    )(c_row, w_ada, b_ada3, *shards)
    return outs[0], outs[1], outs[2:]


class _ChipExchangeSteps:
    FLIPS = ((1, 0), (0, 1), (1, 1))

    def __init__(self, srcs, dsts, send_sems, recv_sems):
        self.srcs, self.dsts, self.send_sems, self.recv_sems = srcs, dsts, send_sems, recv_sems

    def _copies(self):
        x, y, c = _coords()
        out = []
        for a in range(len(self.srcs)):
            for j, (fx, fy) in enumerate(self.FLIPS):
                k = 3 * a + j
                out.append(pltpu.make_async_remote_copy(
                    self.srcs[a].at[x ^ fx, y ^ fy], self.dsts[a].at[j], self.send_sems.at[k], self.recv_sems.at[k],
                    device_id=(x ^ fx, y ^ fy, c), device_id_type=MESH))
        return out

    def start(self):
        for cp in self._copies():
            cp.start()

    def finish(self):
        for cp in self._copies():
            cp.wait()


def _mix_fwd(x, mod, g_pre, g_post, w_in_b, sgn, w_sp, b_sp_t, w_pool, p_scale, w_out_b, ts, shards, shard_dtypes):
    t_len = x.shape[0]
    nt, nb = t_len // ts, ts // HEAD
    ns = len(shards)

    def body(*refs):
        (x_ref, mod_ref, g1_ref, g2_ref, win_ref, sgn_ref, ws_ref, bst_ref, wp_ref, ps_ref, wout_ref) = refs[:11]
        x1_ref, proj_ref, mixed_ref = refs[11 + ns:14 + ns]
        pbuf, cat = refs[14 + 2 * ns:16 + 2 * ns]
        gather = _GatherSteps(refs[11:11 + ns], refs[14 + ns:14 + 2 * ns], refs[16 + 2 * ns:16 + 3 * ns],
                              *refs[16 + 3 * ns:])
        i = pl.program_id(0)

        @pl.when(i == 0)
        def _():
            pbuf[0:POOL_HALO, :] = jnp.zeros((POOL_HALO, B_WIDTH), F32)
            gather.start()

        @pl.when(i == (3 * nt) // 4)
        def _():
            gather.forward()

        xv = x_ref[...]
        sh, sc, gm = mod_ref[0:1, :], mod_ref[1:2, :], mod_ref[2:3, :]
        h = (xv * _rstd(xv) * g1_ref[...]) * (1.0 + sc) + sh
        proj_ref[...] = _mm(_bf(h), win_ref[...])
        pbuf[POOL_HALO:POOL_HALO + ts, :] = proj_ref[:, 2 * A_WIDTH:]
        smask = _sgu_mask()
        for hd in range(N_HEAD):
            u = _gelu(proj_ref[:, hd * HEAD:(hd + 1) * HEAD])
            v = _gelu(proj_ref[:, A_WIDTH + hd * HEAD:A_WIDTH + (hd + 1) * HEAD])
            vn = _bf(v * _rstd(v) * sgn_ref[hd:hd + 1, :])
            wm = _bf(jnp.where(smask, ws_ref[hd], 0.0))
            bias = bst_ref[:, hd:hd + 1]
            for b in range(nb):
                rows = slice(b * HEAD, (b + 1) * HEAD)
                z = _mm(wm, vn[rows]) + bias
                cat[rows, hd * HEAD:(hd + 1) * HEAD] = _bf(u[rows] * z)
        for g, w in enumerate(WINDOWS):
            cols = slice(g * HEAD, (g + 1) * HEAD)
            ext = pbuf[:, cols]
            pooled = _window_sum(ext, w, True)[POOL_HALO:] * _inv_count(i * ts, ts, w) - ext[POOL_HALO:]
            cat[:, A_WIDTH + g * HEAD:A_WIDTH + (g + 1) * HEAD] = _bf(_mm(_bf(pooled), _bf(wp_ref[g])) * ps_ref[:, cols])
        pbuf[0:POOL_HALO, :] = pbuf[ts:ts + POOL_HALO, :]
        mixed = _mm(cat[...], wout_ref[...])
        mixed_ref[...] = mixed
        x1_ref[...] = xv + gm * (mixed * _rstd(mixed) * g2_ref[...])

        @pl.when(i == nt - 1)
        def _():
            gather.finish()

    tile = lambda wid: pl.BlockSpec((ts, wid), lambda i: (i, 0))
    outs = pl.pallas_call(
        body,
        name="mix_fwd",
        grid=(nt,),
        out_shape=(jax.ShapeDtypeStruct((t_len, D), F32), jax.ShapeDtypeStruct((t_len, IN_WIDTH), F32),
                   jax.ShapeDtypeStruct((t_len, D), F32)) + _gather_out_shapes(shards, shard_dtypes),
        in_specs=[tile(D)] + [VMEM_SPEC] * (10 + ns),
        out_specs=(tile(D), tile(IN_WIDTH), tile(D)) + (ANY_SPEC,) * ns,
        scratch_shapes=[pltpu.VMEM((POOL_HALO + ts, B_WIDTH), F32), pltpu.VMEM((ts, D), BF16)]
        + _gather_scratch(shards, shard_dtypes),
        compiler_params=pltpu.CompilerParams(dimension_semantics=("arbitrary",), vmem_limit_bytes=VMEM_LIMIT_V7X),
    )(x, mod, g_pre, g_post, w_in_b, sgn, w_sp, b_sp_t, w_pool, p_scale, w_out_b, *shards)
    return outs[:3], outs[3:]


def _ffn_fwd(x1, target, mod, g_pre, g_post, w_up_b, conv_w8, conv_b8, w_down_b, ts):
    t_len = x1.shape[0]
    nt = t_len // ts

    def body(x1_ref, tgt_ref, mod_ref, g3_ref, g4_ref, wup_ref, cw_ref, cb_ref, wdown_ref,
             up_ref, f_ref, dx2_ref, loss_ref, ucarry):
        i = pl.program_id(0)

        @pl.when(i == 0)
        def _():
            ucarry[...] = jnp.zeros(ucarry.shape, F32)
            loss_ref[...] = jnp.zeros(loss_ref.shape, F32)

        x1v = x1_ref[...]
        sh, sc, gf = mod_ref[3:4, :], mod_ref[4:5, :], mod_ref[5:6, :]
        h2 = _bf((x1v * _rstd(x1v) * g3_ref[...]) * (1.0 + sc) + sh)
        half = N_DEV // 2

        def up_pair(j):
            return [_mm(h2, wup_ref[jj]) for jj in (j, j + half)]

        f = jnp.zeros((ts, D), F32)
        ups = up_pair(0)
        for j in range(half):
            nxt = up_pair(j + 1) if j + 1 < half else None
            ys = []
            for up, jj in zip(ups, (j, j + half)):
                up_ref[jj] = up
                before = ucarry[jj]
                ucarry[jj] = up[ts - HALO:, :]
                cw = cw_ref[jj]
                ys.append(cb_ref[jj:jj + 1, :] + _shift_down(up, before, 2) * cw[0:1, :]
                          + _shift_down(up, before, 1) * cw[1:2, :] + up * cw[2:3, :])
            gate, val = ys
            act = gate * _sigmoid(gate) * val
            f = f + _mm(_bf(act), wdown_ref[j * FF_CHUNK:(j + 1) * FF_CHUNK, :])
            ups = nxt
        f_ref[...] = f
        x2 = x1v + gf * (f * _rstd(f) * g4_ref[...])
        err = x2 - tgt_ref[...]
        loss_ref[...] += _sum0(err * err)
        dx2_ref[...] = err * (1.0 / D)

    tile = pl.BlockSpec((ts, D), lambda i: (i, 0))
    return pl.pallas_call(
        body,
        name="ffn_fwd",
        grid=(nt,),
        out_shape=(jax.ShapeDtypeStruct((N_DEV, t_len, FF_CHUNK), F32), jax.ShapeDtypeStruct((t_len, D), F32),
                   jax.ShapeDtypeStruct((t_len, D), F32), jax.ShapeDtypeStruct((1, D), F32)),
        in_specs=[tile, tile] + [VMEM_SPEC] * 7,
        out_specs=(pl.BlockSpec((N_DEV, ts, FF_CHUNK), lambda i: (0, i, 0)), tile, tile,
                   pl.BlockSpec((1, D), lambda i: (0, 0))),
        scratch_shapes=[pltpu.VMEM((N_DEV, HALO, FF_CHUNK), F32)],
        compiler_params=pltpu.CompilerParams(dimension_semantics=("arbitrary",), vmem_limit_bytes=VMEM_LIMIT_V7X),
    )(x1, target, mod, g_pre, g_post, w_up_b, conv_w8, conv_b8, w_down_b)


def _ffn_bwd(dx2, f, x1, up, mod, g_pre, g_post, w_up_b, conv_w8, conv_b8, w_down_b, ts):
    t_len = x1.shape[0]
    nt = t_len // ts
    half = N_DEV // 2

    def body(dx2_ref, f_ref, x1_ref, up_ref, halo_ref, mod_ref, g3_ref, g4_ref, wup_ref, cw_ref, cb_ref, wdown_ref,
             dx1_ref, dup_ref, act_ref, df_ref, h2_ref, dmod_ref, dg3_ref, dg4_ref, dcb_ref, dcw_ref,
             dycarry, dh2acc):
        i = pl.program_id(0)
        r = nt - 1 - i

        @pl.when(i == 0)
        def _():
            for ref in (dmod_ref, dg3_ref, dg4_ref, dcb_ref, dcw_ref, dycarry):
                ref[...] = jnp.zeros(ref.shape, F32)

        dx2v, fv, x1v = dx2_ref[...], f_ref[...], x1_ref[...]
        sh, sc, gf = mod_ref[3:4, :], mod_ref[4:5, :], mod_ref[5:6, :]
        g3, g4 = g3_ref[...], g4_ref[...]
        rstd4 = _rstd(fv)
        fh = fv * rstd4
        dmod_ref[2:3, :] += _sum0(dx2v * (fh * g4))
        dr = dx2v * gf
        dg4_ref[...] += _sum0(dr * fh)
        dfh = dr * g4
        dfb = _bf(rstd4 * (dfh - fh * _rowmean(dfh * fh)))
        df_ref[...] = dfb
        rstd3 = _rstd(x1v)
        xh = x1v * rstd3
        n3 = xh * g3
        h2_ref[...] = _bf(n3 * (1.0 + sc) + sh)
        dh2acc[...] = jnp.zeros((ts, D), F32)
        keep = jnp.where(r > 0, 1.0, 0.0).astype(F32)

        def dact_of(j):
            return _mm_nt(dfb, wdown_ref[j * FF_CHUNK:(j + 1) * FF_CHUNK, :])

        dact_next = dact_of(0)
        for j in range(half):
            dact = dact_next
            if j + 1 < half:
                dact_next = dact_of(j + 1)
            ys = []
            for jj in (j, j + half):
                before = halo_ref[jj] * keep
                upc = up_ref[jj]
                cw = cw_ref[jj]
                ys.append(cb_ref[jj:jj + 1, :] + _shift_down(upc, before, 2) * cw[0:1, :]
                          + _shift_down(upc, before, 1) * cw[1:2, :] + upc * cw[2:3, :])
            gate, val = ys
            sg = _sigmoid(gate)
            silu = gate * sg
            act_ref[j] = _bf(silu * val)
            dys = (dact * val * (sg * (1.0 + gate * (1.0 - sg))), dact * silu)
            for q, jj in enumerate((j, j + half)):
                dy = dys[q]
                cw = cw_ref[jj]
                dcb_ref[jj:jj + 1, :] += _sum0(dy)
                after = dycarry[jj]
                dycarry[jj] = dy[0:HALO, :]
                dy1, dy2 = _shift_up(dy, after, 1), _shift_up(dy, after, 2)
                upc = up_ref[jj]
                dcw_ref[jj, 0:1, :] += _sum0(dy2 * upc)
                dcw_ref[jj, 1:2, :] += _sum0(dy1 * upc)
                dcw_ref[jj, 2:3, :] += _sum0(dy * upc)
                dup = _bf(dy * cw[2:3, :] + dy1 * cw[1:2, :] + dy2 * cw[0:1, :])
                dup_ref[jj] = dup
                dh2acc[...] += _mm_nt(dup, wup_ref[jj])
        dh2 = dh2acc[...]
        dmod_ref[0:1, :] += _sum0(dh2)
        dmod_ref[1:2, :] += _sum0(dh2 * n3)
        dn3 = dh2 * (1.0 + sc)
        dg3_ref[...] += _sum0(dn3 * xh)
        dxh = dn3 * g3
        dx1_ref[...] = dx2v + rstd3 * (dxh - xh * _rowmean(dxh * xh))

    tile = pl.BlockSpec((ts, D), lambda i: (nt - 1 - i, 0))
    chunked = lambda n: pl.BlockSpec((n, ts, FF_CHUNK), lambda i: (0, nt - 1 - i, 0))
    halo = pl.BlockSpec((N_DEV, HALO, FF_CHUNK), lambda i: (0, jnp.maximum((nt - 1 - i) * (ts // HALO) - 1, 0), 0))
    const = lambda *shape: pl.BlockSpec(shape, lambda i: (0,) * len(shape))
    return pl.pallas_call(
        body,
        name="ffn_bwd",
        grid=(nt,),
        out_shape=(jax.ShapeDtypeStruct((t_len, D), F32), jax.ShapeDtypeStruct((N_DEV, t_len, FF_CHUNK), BF16),
                   jax.ShapeDtypeStruct((half, t_len, FF_CHUNK), BF16), jax.ShapeDtypeStruct((t_len, D), BF16),
                   jax.ShapeDtypeStruct((t_len, D), BF16), jax.ShapeDtypeStruct((3, D), F32),
                   jax.ShapeDtypeStruct((1, D), F32), jax.ShapeDtypeStruct((1, D), F32),
                   jax.ShapeDtypeStruct((N_DEV, FF_CHUNK), F32), jax.ShapeDtypeStruct((N_DEV, 3, FF_CHUNK), F32)),
        in_specs=[tile, tile, tile, chunked(N_DEV), halo] + [VMEM_SPEC] * 7,
        out_specs=(tile, chunked(N_DEV), chunked(half), tile, tile, const(3, D), const(1, D), const(1, D),
                   const(N_DEV, FF_CHUNK), const(N_DEV, 3, FF_CHUNK)),
        scratch_shapes=[pltpu.VMEM((N_DEV, HALO, FF_CHUNK), F32), pltpu.VMEM((ts, D), F32)],
        compiler_params=pltpu.CompilerParams(dimension_semantics=("arbitrary",), vmem_limit_bytes=VMEM_LIMIT_V7X),
    )(dx2, f, x1, up, up, mod, g_pre, g_post, w_up_b, conv_w8, conv_b8, w_down_b)


def _wgrad_up(h2, dup, ts):
    t_len = h2.shape[0]
    nt, half = t_len // ts, N_DEV // 2

    def body(h2_ref, dup_ref, out_ref):
        @pl.when(pl.program_id(1) == 0)
        def _():
            out_ref[...] = jnp.zeros(out_ref.shape, F32)

        for q in range(half):
            out_ref[q] += _mm_tn(h2_ref[...], dup_ref[q])

    return pl.pallas_call(
        body,
        name="wgrad_up",
        grid=(2, nt),
        out_shape=jax.ShapeDtypeStruct((N_DEV, D, FF_CHUNK), F32),
        in_specs=[pl.BlockSpec((ts, D), lambda g, t: (t, 0)), pl.BlockSpec((half, ts, FF_CHUNK), lambda g, t: (g, t, 0))],
        out_specs=pl.BlockSpec((half, D, FF_CHUNK), lambda g, t: (g, 0, 0)),
        compiler_params=pltpu.CompilerParams(dimension_semantics=("arbitrary", "arbitrary"),
                                             vmem_limit_bytes=VMEM_LIMIT_V7X),
    )(h2, dup)


def _sibling_swap_copies(srcs, dsts, send_sems, recv_sems):
    x, y, c = _coords()
    return [
        pltpu.make_async_remote_copy(srcs[a].at[xs, ys, 1 - c], dsts[a].at[xs, ys], send_sems.at[a, 2 * xs + ys],
                                     recv_sems.at[a, 2 * xs + ys], device_id=(x, y, 1 - c), device_id_type=MESH)
        for a in range(len(srcs)) for xs in range(2) for ys in range(2)
    ]


def _wgrad_down(act, df, ts, swap_src):
    t_len = df.shape[0]
    nt, half = t_len // ts, N_DEV // 2

    def body(act_ref, df_ref, src_ref, out_ref, dst_ref, send_sems, recv_sems):
        t = pl.program_id(0)

        @pl.when(t == 0)
        def _():
            for cp in _sibling_swap_copies([src_ref], [dst_ref], send_sems, recv_sems):
                cp.start()
            out_ref[...] = jnp.zeros(out_ref.shape, F32)

        for q in range(half):
            out_ref[q] += _mm_tn(act_ref[q], df_ref[...])

        @pl.when(t == nt - 1)
        def _():
            for cp in _sibling_swap_copies([src_ref], [dst_ref], send_sems, recv_sems):
                cp.wait()

    return pl.pallas_call(
        body,
        name="wgrad_down",
        grid=(nt,),
        out_shape=(jax.ShapeDtypeStruct((half, FF_CHUNK, D), F32), jax.ShapeDtypeStruct(swap_src.shape[1:], F32)),
        in_specs=[pl.BlockSpec((half, ts, FF_CHUNK), lambda t: (0, t, 0)), pl.BlockSpec((ts, D), lambda t: (t, 0)),
                  ANY_SPEC],
        out_specs=(pl.BlockSpec((half, FF_CHUNK, D), lambda t: (0, 0, 0)), ANY_SPEC),
        scratch_shapes=[pltpu.SemaphoreType.DMA((1, 4)), pltpu.SemaphoreType.DMA((1, 4))],
        compiler_params=pltpu.CompilerParams(dimension_semantics=("arbitrary",), vmem_limit_bytes=VMEM_LIMIT_V7X),
    )(act, df, swap_src)


def _mix_bwd(dx1, x, proj, mixed, mod, g_pre, g_post, w_in_b, sgn, w_sp, b_sp_t, w_pool, p_scale, w_out_b, ts, rs_srcs):
    t_len = x.shape[0]
    nt, nb = t_len // ts, ts // HEAD
    nr = len(rs_srcs)

    def body(*refs):
        (dx1_ref, x_ref, proj_ref, halo_ref, mixed_ref, mod_ref, g1_ref, g2_ref, win_ref, sgn_ref, ws_ref,
         bst_ref, wp_ref, ps_ref, wout_ref) = refs[:15]
        (gx_ref, dwin_ref, dwout_ref, dmod_ref, dg1_ref, dg2_ref, dsgn_ref, dws_ref, dbst_ref, dwp_ref,
         dps_ref) = refs[15 + nr:26 + nr]
        pbuf, dwsbuf, cat, dproj, dcat = refs[26 + 2 * nr:31 + 2 * nr]
        exchange = _ChipExchangeSteps(refs[15:15 + nr], refs[26 + nr:26 + 2 * nr], *refs[31 + 2 * nr:])
        i = pl.program_id(0)
        r = nt - 1 - i

        @pl.when(i == 0)
        def _():
            exchange.start()
            for ref in (dwin_ref, dwout_ref, dmod_ref, dg1_ref, dg2_ref, dsgn_ref, dws_ref, dbst_ref, dwp_ref, dps_ref):
                ref[...] = jnp.zeros(ref.shape, F32)
            dwsbuf[ts:ts + POOL_HALO, :] = jnp.zeros((POOL_HALO, B_WIDTH), F32)

        xv, dx1v, mixed = x_ref[...], dx1_ref[...], mixed_ref[...]
        sh, sc, gm = mod_ref[0:1, :], mod_ref[1:2, :], mod_ref[2:3, :]
        g1, g2 = g1_ref[...], g2_ref[...]
        rstd2 = _rstd(mixed)
        mh = mixed * rstd2
        dmod_ref[2:3, :] += _sum0(dx1v * (mh * g2))
        dr = dx1v * gm
        dg2_ref[...] += _sum0(dr * mh)
        dmh = dr * g2
        dmb = _bf(rstd2 * (dmh - mh * _rowmean(dmh * mh)))
        dcat[...] = _mm_nt(dmb, wout_ref[...])
        smask = _sgu_mask()
        for hd in range(N_HEAD):
            ucols = slice(hd * HEAD, (hd + 1) * HEAD)
            vcols = slice(A_WIDTH + hd * HEAD, A_WIDTH + (hd + 1) * HEAD)
            u, du_dp = _gelu_and_grad(proj_ref[:, ucols])
            v, dv_dp = _gelu_and_grad(proj_ref[:, vcols])
            rs = _rstd(v)
            vhat = v * rs
            gn = sgn_ref[hd:hd + 1, :]
            vn = _bf(vhat * gn)
            wm = _bf(jnp.where(smask, ws_ref[hd], 0.0))
            bias = bst_ref[:, hd:hd + 1]
            dzsum = jnp.zeros((HEAD, HEAD), F32)
            dwm = jnp.zeros((HEAD, HEAD), F32)
            dvn_parts = []
            for b in range(nb):
                rows = slice(b * HEAD, (b + 1) * HEAD)
                z = _mm(wm, vn[rows]) + bias
                da = dcat[rows, ucols]
                cat[rows, ucols] = _bf(u[rows] * z)
                dz = da * u[rows]
                dzsum = dzsum + dz
                dzb = _bf(dz)
                dwm = dwm + _mm_nt(dzb, vn[rows])
                dvn_parts.append(_mm_tn(wm, dzb))
                dproj[rows, ucols] = _bf((da * z) * du_dp[rows])
            dvn = jnp.concatenate(dvn_parts, axis=0)
            dsgn_ref[hd:hd + 1, :] += _sum0(dvn * vhat)
            dvh = dvn * gn
            dproj[:, vcols] = _bf((rs * (dvh - vhat * _rowmean(dvh * vhat))) * dv_dp)
            dws_ref[hd] += jnp.where(smask, dwm, 0.0)
            dbst_ref[:, hd:hd + 1] += jnp.sum(dzsum, axis=1, keepdims=True)
        keep = jnp.where(r > 0, 1.0, 0.0).astype(F32)
        pbuf[0:POOL_HALO, :] = halo_ref[...] * keep
        pbuf[POOL_HALO:POOL_HALO + ts, :] = proj_ref[:, 2 * A_WIDTH:]
        for g, w in enumerate(WINDOWS):
            cols = slice(g * HEAD, (g + 1) * HEAD)
            ccols = slice(A_WIDTH + g * HEAD, A_WIDTH + (g + 1) * HEAD)
            pcols = slice(2 * A_WIDTH + g * HEAD, 2 * A_WIDTH + (g + 1) * HEAD)
            wpg = _bf(wp_ref[g])
            psg = ps_ref[:, cols]
            ext = pbuf[:, cols]
            inv = _inv_count(r * ts, ts, w)
            pb = _bf(_window_sum(ext, w, True)[POOL_HALO:] * inv - ext[POOL_HALO:])
            yb = _mm(pb, wpg)
            dob = dcat[:, ccols]
            cat[:, ccols] = _bf(yb * psg)
            dps_ref[:, cols] += _sum0(dob * yb)
            dyb = _bf(dob * psg)
            dwp_ref[g] += _mm_tn(pb, dyb)
            dpooled = _mm_nt(dyb, wpg)
            dwsbuf[0:ts, cols] = dpooled * inv
            dproj[:, pcols] = _bf(_window_sum(dwsbuf[:, cols], w, False)[0:ts] - dpooled)
        dwsbuf[ts:ts + POOL_HALO, :] = dwsbuf[0:POOL_HALO, :]
        dpb = dproj[...]
        rstd1 = _rstd(xv)
        xh = xv * rstd1
        n1 = xh * g1
        dwin_ref[...] += _mm_tn(_bf(n1 * (1.0 + sc) + sh), dpb)
        dwout_ref[...] += _mm_tn(cat[...], dmb)
        dh = _mm_nt(dpb, win_ref[...])
        dmod_ref[0:1, :] += _sum0(dh)
        dmod_ref[1:2, :] += _sum0(dh * n1)
        dn1 = dh * (1.0 + sc)
        dg1_ref[...] += _sum0(dn1 * xh)
        dxh = dn1 * g1
        gx_ref[...] = dx1v + rstd1 * (dxh - xh * _rowmean(dxh * xh))

        @pl.when(i == nt - 1)
        def _():
            exchange.finish()

    tile = lambda wid: pl.BlockSpec((ts, wid), lambda i: (nt - 1 - i, 0))
    halo = pl.BlockSpec((POOL_HALO, B_WIDTH),
                        lambda i: (jnp.maximum((nt - 1 - i) * (ts // POOL_HALO) - 1, 0), 2 * A_WIDTH // B_WIDTH))
    const = lambda *shape: pl.BlockSpec(shape, lambda i: (0,) * len(shape))
    resident = lambda *shape: pl.BlockSpec(shape, lambda i: (0,) * len(shape), pipeline_mode=pl.Buffered(1))
    outs = pl.pallas_call(
        body,
        name="mix_bwd",
        grid=(nt,),
        out_shape=(jax.ShapeDtypeStruct((t_len, D), F32), jax.ShapeDtypeStruct((D, IN_WIDTH), F32),
                   jax.ShapeDtypeStruct((D, D), F32), jax.ShapeDtypeStruct((3, D), F32),
                   jax.ShapeDtypeStruct((1, D), F32), jax.ShapeDtypeStruct((1, D), F32),
                   jax.ShapeDtypeStruct((N_HEAD, HEAD), F32), jax.ShapeDtypeStruct((N_HEAD, HEAD, HEAD), F32),
                   jax.ShapeDtypeStruct((HEAD, N_HEAD), F32), jax.ShapeDtypeStruct((N_HEAD, HEAD, HEAD), F32),
                   jax.ShapeDtypeStruct((1, B_WIDTH), F32))
        + tuple(jax.ShapeDtypeStruct((3, *s.shape[2:]), s.dtype) for s in rs_srcs),
        in_specs=[tile(D), tile(D), tile(IN_WIDTH), halo, tile(D)] + [VMEM_SPEC] * 10 + [ANY_SPEC] * nr,
        out_specs=(tile(D), resident(D, IN_WIDTH), resident(D, D), const(3, D), const(1, D), const(1, D),
                   const(N_HEAD, HEAD), const(N_HEAD, HEAD, HEAD), const(HEAD, N_HEAD), const(N_HEAD, HEAD, HEAD),
                   const(1, B_WIDTH)) + (ANY_SPEC,) * nr,
        scratch_shapes=[pltpu.VMEM((POOL_HALO + ts, B_WIDTH), F32), pltpu.VMEM((ts + POOL_HALO, B_WIDTH), F32),
                        pltpu.VMEM((ts, D), BF16), pltpu.VMEM((ts, IN_WIDTH), BF16), pltpu.VMEM((ts, D), F32),
                        pltpu.SemaphoreType.DMA((3 * nr,)), pltpu.SemaphoreType.DMA((3 * nr,))],
        compiler_params=pltpu.CompilerParams(dimension_semantics=("arbitrary",), vmem_limit_bytes=VMEM_LIMIT_V7X),
    )(dx1, x, proj, proj, mixed, mod, g_pre, g_post, w_in_b, sgn, w_sp, b_sp_t, w_pool, p_scale, w_out_b, *rs_srcs)
    return outs[:11], outs[11:]


def _pair_add(name, coords, grid, specs_a, specs_b, out_specs, out_shapes, a_arrays, b_arrays):
    n = len(a_arrays)

    def body(co_ref, *refs):
        for k in range(n):
            total = refs[k][...] + refs[n + k][...]
            refs[2 * n + k][...] = total
            refs[3 * n + k][...] = _bf(total)

    outs = pl.pallas_call(
        body,
        name=name,
        grid_spec=pltpu.PrefetchScalarGridSpec(num_scalar_prefetch=1, grid=grid, in_specs=specs_a + specs_b,
                                               out_specs=out_specs * 2),
        out_shape=tuple(jax.ShapeDtypeStruct(s, dt) for dt in (F32, BF16) for s in out_shapes),
        compiler_params=pltpu.CompilerParams(dimension_semantics=("arbitrary",) * len(grid),
                                             vmem_limit_bytes=VMEM_LIMIT_V7X),
    )(coords, *a_arrays, *b_arrays)
    return list(outs[:n]), list(outs[n:])


def _final_add_adamw(coords, s1, r, ws, ms, vs, n_split=4):
    n = len(s1)

    def body(co_ref, *refs):
        for k in range(n):
            s_ref, r_ref, w_ref, m_ref, v_ref = (refs[q * n + k] for q in range(5))
            g_ref, d_ref, nm_ref, nv_ref = (refs[(5 + q) * n + k] for q in range(4))
            g = ((s_ref[...] + r_ref[0].astype(F32)) + r_ref[1].astype(F32)) + r_ref[2].astype(F32)
            g_ref[...] = g
            delta, m, v = _adamw(w_ref[...], g, m_ref[...], v_ref[...])
            d_ref[...] = delta
            nm_ref[...] = m
            nv_ref[...] = v

    def shard_spec(a):
        rows, cols = a.shape
        return pl.BlockSpec((rows // n_split, cols), lambda i, co: (i, 0))

    def mine_spec(a):
        rows, cols = a.shape[2:]
        return pl.BlockSpec((None, None, rows // n_split, cols), lambda i, co: (co[0], co[1], i, 0))

    def recv_spec(a):
        rows, cols = a.shape[1:]
        return pl.BlockSpec((3, rows // n_split, cols), lambda i, co: (0, i, 0))

    in_specs = ([mine_spec(a) for a in s1] + [recv_spec(a) for a in r] + [shard_spec(a) for a in ws] * 3)
    out_specs = [shard_spec(a) for a in ws] * 4
    outs = pl.pallas_call(
        body,
        name="grad_final_adamw",
        grid_spec=pltpu.PrefetchScalarGridSpec(num_scalar_prefetch=1, grid=(n_split,), in_specs=in_specs,
                                               out_specs=out_specs),
        out_shape=tuple(jax.ShapeDtypeStruct(a.shape, F32) for a in ws) * 4,
        compiler_params=pltpu.CompilerParams(dimension_semantics=("arbitrary",), vmem_limit_bytes=VMEM_LIMIT_V7X),
    )(coords, *s1, *r, *ws, *ms, *vs)
    return [tuple(outs[q * n + k] for q in range(4)) for k in range(n)]


def _sibling_swap(tag, g5):
    n = len(g5)

    def body(*refs):
        copies = _sibling_swap_copies(refs[:n], refs[n:2 * n], *refs[2 * n:])
        for cp in copies:
            cp.start()
        for cp in copies:
            cp.wait()

    return list(pl.pallas_call(
        body,
        name="grad_swap_core_" + tag,
        out_shape=tuple(jax.ShapeDtypeStruct(g.shape[1:], F32) for g in g5),
        in_specs=[ANY_SPEC] * n,
        out_specs=(ANY_SPEC,) * n,
        scratch_shapes=[pltpu.SemaphoreType.DMA((n, 4)), pltpu.SemaphoreType.DMA((n, 4))],
    )(*g5))


def _sibling_add(tag, g5, r1, coords, n_split=4):
    shapes = [g.shape[3:] for g in g5]
    spec_g = [pl.BlockSpec((None, None, None, s[0] // n_split, s[1]), lambda i, j, k, co: (i, j, co[2], k, 0))
              for s in shapes]
    spec_r = [pl.BlockSpec((None, None, s[0] // n_split, s[1]), lambda i, j, k, co: (i, j, k, 0)) for s in shapes]
    return _pair_add("grad_add_core_" + tag, coords, (2, 2, n_split), spec_g, spec_r, spec_r,
                     [(2, 2, *s) for s in shapes], g5, r1)


def _tail_exchange(big, partials, pick_mine, dmod3):
    n, nb = len(partials), len(big)
    big_shapes = [g.shape[1:] for g in big]
    big5 = [g.reshape(2, 2, 2, *s) for g, s in zip(big, big_shapes)]
    flips = _ChipExchangeSteps.FLIPS

    def body(*refs):
        g5, p_in, dm_ref = refs[:nb], refs[nb:nb + n], refs[nb + n]
        outs = refs[nb + n + 1:2 * nb + 2 * n + 2]
        g_out, sums, dm2d = outs[:nb], outs[nb:nb + n], outs[nb + n]
        scratch = refs[2 * nb + 2 * n + 2:]
        s1, stage, chip_recv = scratch[:nb], scratch[nb:2 * nb], scratch[2 * nb:3 * nb]
        acc, rbuf = scratch[3 * nb:3 * nb + n], scratch[3 * nb + n:3 * nb + 2 * n]
        (dm_recv, send_sems, recv_sems, dm_send_sems, dm_recv_sems, sib_send, sib_recv, chip_send,
         chip_recv_sems) = scratch[3 * nb + 2 * n:]
        x, y, c = _coords()
        me = 4 * x + 2 * y + c
        sibling = (x, y, 1 - c)
        dm_copies = [
            pltpu.make_async_remote_copy(dm_ref.at[me ^ k], dm_recv.at[k], dm_send_sems.at[k], dm_recv_sems.at[k],
                                         device_id=_peer(k), device_id_type=MESH)
            for k in range(1, N_DEV)
        ]
        for cp in dm_copies:
            cp.start()
        sib_copies = _sibling_swap_copies(g5, s1, sib_send, sib_recv)
        for cp in sib_copies:
            cp.start()
        for a in range(n):
            acc[a][...] = p_in[a][...]

        def small_phase(ph, peer):
            copies = [
                pltpu.make_async_remote_copy(acc[a], rbuf[a].at[ph], send_sems.at[ph, a], recv_sems.at[ph, a],
                                             device_id=peer, device_id_type=MESH)
                for a in range(n)
            ]
            for cp in copies:
                cp.start()
            for cp in copies:
                cp.wait()
            for a in range(n):
                acc[a][...] = acc[a][...] + rbuf[a][ph]

        small_phase(0, sibling)
        for cp in sib_copies:
            cp.wait()
        for a in range(nb):
            for xs in range(2):
                for ys in range(2):
                    total = g5[a][xs, ys, c] + s1[a][xs, ys]
                    s1[a][xs, ys] = total
                    stage[a][xs, ys] = _bf(total)
        chip_copies = [
            pltpu.make_async_remote_copy(stage[a].at[x ^ fx, y ^ fy], chip_recv[a].at[j], chip_send.at[a, j],
                                         chip_recv_sems.at[a, j], device_id=(x ^ fx, y ^ fy, c), device_id_type=MESH)
            for a in range(nb) for j, (fx, fy) in enumerate(flips)
        ]
        for cp in chip_copies:
            cp.start()
        small_phase(1, (1 - x, y, c))
        small_phase(2, (x, 1 - y, c))
        for a in range(n):
            sums[a][...] = acc[a][me] if pick_mine[a] else acc[a][...]
        dm2d[...] = jnp.zeros(dm2d.shape, F32)
        dm2d[0:1, :] = dm_ref[me]
        for cp in dm_copies:
            cp.wait()
        for k in range(1, N_DEV):
            dm2d[k:k + 1, :] = dm_recv[k]
        for cp in chip_copies:
            cp.wait()
        for a in range(nb):
            g_out[a][...] = ((s1[a][x, y] + chip_recv[a][0].astype(F32)) + chip_recv[a][1].astype(F32)) \
                + chip_recv[a][2].astype(F32)

    out_shapes = tuple(jax.ShapeDtypeStruct(s, F32) for s in big_shapes) + tuple(
        jax.ShapeDtypeStruct(p.shape[1:] if pk else p.shape, F32) for p, pk in zip(partials, pick_mine))
    outs = pl.pallas_call(
        body,
        name="tail_exchange",
        out_shape=out_shapes + (jax.ShapeDtypeStruct((2 * N_DEV, MOD_COLS), F32),),
        in_specs=[VMEM_SPEC] * (nb + n + 1),
        out_specs=(VMEM_SPEC,) * (nb + n + 1),
        scratch_shapes=[pltpu.VMEM((2, 2, *s), F32) for s in big_shapes]
        + [pltpu.VMEM((2, 2, *s), BF16) for s in big_shapes]
        + [pltpu.VMEM((3, *s), BF16) for s in big_shapes]
        + [pltpu.VMEM(p.shape, F32) for p in partials]
        + [pltpu.VMEM((3, *p.shape), F32) for p in partials]
        + [pltpu.VMEM((N_DEV, 1, MOD_COLS), F32), pltpu.SemaphoreType.DMA((3, n)), pltpu.SemaphoreType.DMA((3, n)),
           pltpu.SemaphoreType.DMA((N_DEV,)), pltpu.SemaphoreType.DMA((N_DEV,)),
           pltpu.SemaphoreType.DMA((nb, 4)), pltpu.SemaphoreType.DMA((nb, 4)),
           pltpu.SemaphoreType.DMA((nb, 3)), pltpu.SemaphoreType.DMA((nb, 3))],
        compiler_params=pltpu.CompilerParams(vmem_limit_bytes=VMEM_LIMIT_V7X),
    )(*big5, *partials, dmod3)
    return list(outs[:nb]), list(outs[nb:nb + n]), outs[nb + n]


def _small_update(grads, ws, ms, vs, scx, dm2d, w_ada, m_ada, v_ada, loss_lanes):
    n = len(grads)

    def body(*refs):
        g_in, w_in, m_in, v_in = (refs[q * n:(q + 1) * n] for q in range(4))
        scx_ref, dm_ref, wa_ref, ma_ref, va_ref, ll_ref = refs[4 * n:4 * n + 6]
        outs = refs[4 * n + 6:]
        g_out, d_out, nm_out, nv_out = (outs[q * (n + 1):(q + 1) * (n + 1)] for q in range(4))
        loss_ref = outs[4 * (n + 1)]
        for a in range(n + 1):
            if a < n:
                g, w, m, v = g_in[a][...], w_in[a][...], m_in[a][...], v_in[a][...]
            else:
                g = _mm_tn(_bf(scx_ref[...]), _bf(dm_ref[...]))
                w, m, v = wa_ref[...], ma_ref[...], va_ref[...]
            g_out[a][...] = g
            delta, m, v = _adamw(w, g, m, v)
            d_out[a][...] = delta
            nm_out[a][...] = m
            nv_out[a][...] = v
        loss_ref[...] = jnp.sum(ll_ref[...], axis=1, keepdims=True) * (0.5 / D)

    w_shapes = tuple(jax.ShapeDtypeStruct(w.shape, F32) for w in list(ws) + [w_ada])
    outs = pl.pallas_call(
        body,
        name="small_update",
        out_shape=w_shapes * 4 + (jax.ShapeDtypeStruct((1, 1), F32),),
        in_specs=[VMEM_SPEC] * (4 * n + 6),
        out_specs=(VMEM_SPEC,) * (4 * (n + 1) + 1),
        compiler_params=pltpu.CompilerParams(vmem_limit_bytes=VMEM_LIMIT_V7X),
    )(*grads, *ws, *ms, *vs, scx, dm2d, w_ada, m_ada, v_ada, loss_lanes)
    return [tuple(outs[q * (n + 1) + k] for q in range(4)) for k in range(n + 1)], outs[4 * (n + 1)]


def kernel(x, c, w_ada, b_ada, pre_mix_g, post_mix_g, w_in, sgu_norm_g, w_spatial, b_spatial, w_pool, pool_scale, w_out, pre_ffn_g, post_ffn_g, w_up, conv_w, conv_b, w_down, loss_target, m_w_ada, m_b_ada, m_pre_mix_g, m_post_mix_g, m_w_in, m_sgu_norm_g, m_w_spatial, m_b_spatial, m_w_pool, m_pool_scale, m_w_out, m_pre_ffn_g, m_post_ffn_g, m_w_up, m_conv_w, m_conv_b, m_w_down, v_w_ada, v_b_ada, v_pre_mix_g, v_post_mix_g, v_w_in, v_sgu_norm_g, v_w_spatial, v_b_spatial, v_w_pool, v_pool_scale, v_w_out, v_pre_ffn_g, v_post_ffn_g, v_w_up, v_conv_w, v_conv_b, v_w_down):
    t_len = x.shape[1]
    ts = min(256, t_len)
    ts_mix = min(512, t_len)
    ts_w = min(1024, t_len)
    coords = jnp.stack([lax.axis_index("x"), lax.axis_index("y"), lax.axis_index("c")]).astype(jnp.int32)

    mod3, scx, (g_in, g_out) = _prologue(c, w_ada[0], b_ada.reshape(N_DEV, 1, MOD_COLS), [w_in[0], w_out[0]],
                                         [BF16, BF16])
    mod = mod3.reshape(N_MOD, D)
    w_in_b = g_in.transpose(1, 0, 2).reshape(D, IN_WIDTH)
    w_out_b = g_out.reshape(D, D)
    conv_b8 = conv_b.reshape(N_DEV, FF_CHUNK)
    b_sp_t = b_spatial[0].T

    x2d, tgt = x[0], loss_target[0]
    (x1, proj, mixed), (g_up, g_down, g_cw) = _mix_fwd(
        x2d, mod, pre_mix_g, post_mix_g, w_in_b, sgu_norm_g[0], w_spatial[0], b_sp_t, w_pool[0], pool_scale, w_out_b,
        ts_mix, [w_up[0], w_down[0], conv_w[0]], [BF16, BF16, F32])
    w_down_b = g_down.reshape(FF, D)
    up, f, dx2, loss_lanes = _ffn_fwd(x1, tgt, mod, pre_ffn_g, post_ffn_g, g_up, g_cw, conv_b8, w_down_b, ts)

    (dx1, dup, act, df, h2, dmod_f, d_pre_ffn, d_post_ffn, d_cb8, d_cw8) = _ffn_bwd(
        dx2, f, x1, up, mod, pre_ffn_g, post_ffn_g, g_up, g_cw, conv_b8, w_down_b, ts)
    gw_up = _wgrad_up(h2, dup, ts_w).reshape(2, 2, 2, D, FF_CHUNK)
    gw_down, r1_up = _wgrad_down(act, df, ts_w, gw_up)
    gw_down = gw_down.reshape(2, 2, 2, FF // N_DEV, D)
    s1_ffn, s1_ffn_b = _sibling_add("ffn", [gw_up, gw_down], [r1_up] + _sibling_swap("down", [gw_down]), coords)
    ((grad_x, gw_in, gw_out, dmod_m, d_pre_mix, d_post_mix, d_sgn, d_wsp, d_bsp_t, d_wpool, d_ps), r_ffn) = _mix_bwd(
        dx1, x2d, proj, mixed, mod, pre_mix_g, post_mix_g, w_in_b, sgu_norm_g[0], w_spatial[0], b_sp_t,
        w_pool[0], pool_scale, w_out_b, ts, s1_ffn_b)
    gw_in = gw_in.reshape(D, N_DEV, IN_WIDTH // N_DEV).transpose(1, 0, 2)
    gw_out = gw_out.reshape(N_DEV, D // N_DEV, D)

    big = _final_add_adamw(coords, s1_ffn, list(r_ffn), [w_up[0], w_down[0]], [m_w_up[0], m_w_down[0]],
                           [v_w_up[0], v_w_down[0]])
    r_up, r_down = [tuple(a[None] for a in four) for four in big]

    dmod = jnp.concatenate([dmod_m, dmod_f], axis=0)
    names = ["b_ada", "pre_mix_g", "post_mix_g", "sgu_norm_g", "w_spatial", "b_spatial", "w_pool", "pool_scale",
             "pre_ffn_g", "post_ffn_g", "conv_w", "conv_b"]
    partials = [dmod.reshape(1, N_MOD * D), d_pre_mix, d_post_mix, d_sgn, d_wsp, d_bsp_t.T, d_wpool, d_ps,
                d_pre_ffn, d_post_ffn, d_cw8, d_cb8.reshape(1, 2 * FF), loss_lanes]
    small_w = [b_ada, pre_mix_g, post_mix_g, sgu_norm_g[0], w_spatial[0], b_spatial[0], w_pool[0], pool_scale,
               pre_ffn_g, post_ffn_g, conv_w[0], conv_b]
    small_m = [m_b_ada, m_pre_mix_g, m_post_mix_g, m_sgu_norm_g[0], m_w_spatial[0], m_b_spatial[0], m_w_pool[0],
               m_pool_scale, m_pre_ffn_g, m_post_ffn_g, m_conv_w[0], m_conv_b]
    small_v = [v_b_ada, v_pre_mix_g, v_post_mix_g, v_sgu_norm_g[0], v_w_spatial[0], v_b_spatial[0], v_w_pool[0],
               v_pool_scale, v_pre_ffn_g, v_post_ffn_g, v_conv_w[0], v_conv_b]
    g_mix, sums, dm2d = _tail_exchange([gw_in, gw_out], partials, [nm == "conv_w" for nm in names] + [False],
                                       dmod.reshape(N_DEV, 1, MOD_COLS))
    small, loss11 = _small_update(
        sums[:-1] + g_mix, small_w + [w_in[0], w_out[0]], small_m + [m_w_in[0], m_w_out[0]],
        small_v + [v_w_in[0], v_w_out[0]], scx, dm2d, w_ada[0], m_w_ada[0], v_w_ada[0], sums[-1])
    loss = loss11.reshape(())
    lead = {"sgu_norm_g", "w_spatial", "b_spatial", "w_pool", "conv_w", "w_in", "w_out", "w_ada"}
    res = {nm: tuple(a[None] if nm in lead else a for a in four)
           for nm, four in zip(names + ["w_in", "w_out", "w_ada"], small)}
    res.update(w_up=r_up, w_down=r_down)

    order = ["w_ada", "b_ada", "pre_mix_g", "post_mix_g", "w_in", "sgu_norm_g", "w_spatial", "b_spatial", "w_pool",
             "pool_scale", "w_out", "pre_ffn_g", "post_ffn_g", "w_up", "conv_w", "conv_b", "w_down"]
    return (loss, grad_x[None], *[res[nm][0] for nm in order], *[res[nm][1] for nm in order],
            *[res[nm][2] for nm in order], *[res[nm][3] for nm in order])
```

```python
import functools
import math

import jax
import jax.numpy as jnp
from jax import lax
from jax.experimental import pallas as pl
from jax.experimental.pallas import tpu as pltpu

F32 = jnp.float32
BF16 = jnp.bfloat16
MESH = pl.DeviceIdType.MESH

EPS = 1e-6
D = 1024
HEAD = 128
N_HEAD = 4
A_WIDTH = 512
B_WIDTH = 512
IN_WIDTH = 1536
WINDOWS = (2, 4, 8, 16)
CHUNK = 64
FF = 2816
N_DEV = 8
FF_CHUNK = 704
N_MOD = 6
MOD_COLS = 768

ADAM_LR = 0.001
ADAM_B1 = 0.9
ADAM_B2 = 0.999
ADAM_EPS = 1e-08
ADAM_WD = 0.01
ADAM_STEP = 10

VMEM_LIMIT_V7X = 60 * 1024 * 1024
HALO = 8
POOL_HALO = 16

VMEM_SPEC = pl.BlockSpec(memory_space=pltpu.VMEM)
ANY_SPEC = pl.BlockSpec(memory_space=pl.ANY)


def _bf(x):
    return x.astype(BF16)


def _mm(a, b):
    return jnp.dot(a, b, preferred_element_type=F32)


def _mm_nt(a, b):
    return lax.dot_general(a, b, (((1,), (1,)), ((), ())), preferred_element_type=F32)


def _mm_tn(a, b):
    return lax.dot_general(a, b, (((0,), (0,)), ((), ())), preferred_element_type=F32)


def _rstd(x):
    return lax.rsqrt(jnp.mean(x * x, axis=-1, keepdims=True) + EPS)


def _sum0(x):
    return jnp.sum(x, axis=0, keepdims=True)


def _rowmean(x):
    return jnp.mean(x, axis=-1, keepdims=True)


_GELU_K = math.sqrt(2.0 / math.pi)


def _gelu_and_grad(x):
    x2 = x * x
    th = jnp.tanh(_GELU_K * (x + 0.044715 * (x * x2)))
    cdf = 0.5 * (1.0 + th)
    grad = cdf + 0.5 * x * (1.0 - th * th) * (_GELU_K * (1.0 + 3.0 * 0.044715 * x2))
    return x * cdf, grad


def _gelu(x):
    return x * (0.5 * (1.0 + jnp.tanh(_GELU_K * (x + 0.044715 * (x * x * x)))))


def _sigmoid(x):
    return 0.5 * jnp.tanh(0.5 * x) + 0.5


def _sgu_mask():
    ri = lax.broadcasted_iota(jnp.int32, (HEAD, HEAD), 0)
    ci = lax.broadcasted_iota(jnp.int32, (HEAD, HEAD), 1)
    return (ci // CHUNK) <= (ri // CHUNK)


def _window_sum(ext, w, trailing):
    n = ext.shape[0]
    s, k = ext, 1
    while k < w:
        s = s + pltpu.roll(s, k if trailing else n - k, 0)
        k *= 2
    return s


def _inv_count(row0, n, w):
    t = row0 + lax.broadcasted_iota(jnp.int32, (n, 1), 0)
    return 1.0 / jnp.minimum(t + 1, w).astype(F32)


def _shift_down(v, before, k):
    rows = lax.broadcasted_iota(jnp.int32, before.shape, 0)
    r = pltpu.roll(v, k, 0)
    top = jnp.where(rows < k, pltpu.roll(before, k, 0), r[0:HALO])
    return jnp.concatenate([top, r[HALO:]], axis=0)


def _shift_up(v, after, k):
    n = v.shape[0]
    rows = lax.broadcasted_iota(jnp.int32, after.shape, 0)
    r = pltpu.roll(v, n - k, 0)
    bottom = jnp.where(rows >= HALO - k, pltpu.roll(after, HALO - k, 0), r[n - HALO:])
    return jnp.concatenate([r[:n - HALO], bottom], axis=0)


def _adamw(w, g, m, v):
    m = ADAM_B1 * m + (1.0 - ADAM_B1) * g
    v = ADAM_B2 * v + (1.0 - ADAM_B2) * (g * g)
    m_hat = m / (1.0 - ADAM_B1 ** ADAM_STEP)
    v_hat = v / (1.0 - ADAM_B2 ** ADAM_STEP)
    delta = -ADAM_LR * (m_hat / (jnp.sqrt(v_hat) + ADAM_EPS) + ADAM_WD * w)
    return delta, m, v


def _coords():
    return lax.axis_index("x"), lax.axis_index("y"), lax.axis_index("c")


def _peer(k):
    x, y, c = _coords()
    return (x ^ ((k >> 2) & 1), y ^ ((k >> 1) & 1), c ^ (k & 1))


def _my_index():
    x, y, c = _coords()
    return 4 * x + 2 * y + c


def _adaln_modulation(c_ref, w_ref, b_ref, mod_ref, scx_ref, scbuf, stage, recv, send_sems, recv_sems):
    me = _my_index()
    cv = c_ref[...]
    scbuf[0] = cv * _sigmoid(cv)
    first = [
        pltpu.make_async_remote_copy(scbuf.at[0], scbuf.at[k], send_sems.at[0, k], recv_sems.at[0, k],
                                     device_id=_peer(k), device_id_type=MESH)
        for k in range(1, N_DEV)
    ]
    for cp in first:
        cp.start()
    for cp in first:
        cp.wait()
    scx_ref[...] = jnp.zeros(scx_ref.shape, F32)
    for k in range(N_DEV):
        scx_ref[k:k + 1, :] = scbuf[k]
    prod = _mm(_bf(scx_ref[...]), _bf(w_ref[...]))
    for k in range(N_DEV):
        stage[k] = prod[k:k + 1, :] + b_ref[me]
    second = [
        pltpu.make_async_remote_copy(stage.at[k], recv.at[k], send_sems.at[1, k], recv_sems.at[1, k],
                                     device_id=_peer(k), device_id_type=MESH)
        for k in range(1, N_DEV)
    ]
    for cp in second:
        cp.start()
    mod_ref[me] = stage[0]
    for cp in second:
        cp.wait()
    for k in range(1, N_DEV):
        mod_ref[me ^ k] = recv[k]


class _GatherSteps:
    def __init__(self, ins, outs, stages, send_sems, recv_sems, local_sems):
        self.ins, self.outs, self.stages = ins, outs, stages
        self.send_sems, self.recv_sems, self.local_sems = send_sems, recv_sems, local_sems
        x, y, c = _coords()
        self.c = c
        self.me, self.sibling = (x, y, c), (x, y, 1 - c)
        self.chips = [(1 - x, y), (x, 1 - y), (1 - x, 1 - y)]

    def _copy(self, a, k, block, to, from_stage=False):
        dst = self.outs[a].at[4 * block[0] + 2 * block[1] + block[2]]
        return pltpu.make_async_remote_copy(self.stages[a] if from_stage else dst, dst, self.send_sems.at[a, k],
                                            self.recv_sems.at[a, k], device_id=to, device_id_type=MESH)

    def _local(self, a):
        me = self.me
        return pltpu.make_async_copy(self.stages[a], self.outs[a].at[4 * me[0] + 2 * me[1] + me[2]],
                                     self.local_sems.at[a])

    def _first(self, a):
        cps = [self._copy(a, 0, self.me, self.sibling, from_stage=True)]
        return cps + [self._copy(a, 1 + j, self.me, (*chip, self.c), from_stage=True)
                      for j, chip in enumerate(self.chips)]

    def _passed(self, a, j):
        return self._copy(a, 4 + j, (*self.chips[j], self.c), self.sibling)

    def start(self):
        for a in range(len(self.ins)):
            self.stages[a][...] = self.ins[a][...].astype(self.stages[a].dtype)
            self._local(a).start()
            for cp in self._first(a):
                cp.start()

    def forward(self):
        for a in range(len(self.ins)):
            for j, chip in enumerate(self.chips):
                self._copy(a, 1 + j, (*chip, self.c), self.me).wait_recv()
                self._passed(a, j).start()

    def finish(self):
        for a in range(len(self.ins)):
            self._copy(a, 0, self.sibling, self.me).wait_recv()
            for j, chip in enumerate(self.chips):
                self._copy(a, 4 + j, (*chip, 1 - self.c), self.me).wait_recv()
            for cp in self._first(a) + [self._passed(a, j) for j in range(3)]:
                cp.wait_send()
            self._local(a).wait()


def _gather_scratch(shards, out_dtypes):
    n = len(shards)
    return ([pltpu.VMEM(s.shape, dt) for s, dt in zip(shards, out_dtypes)]
            + [pltpu.SemaphoreType.DMA((n, 7)), pltpu.SemaphoreType.DMA((n, 7)), pltpu.SemaphoreType.DMA((n,))])


def _gather_out_shapes(shards, out_dtypes):
    return tuple(jax.ShapeDtypeStruct((N_DEV, *s.shape), dt) for s, dt in zip(shards, out_dtypes))


def _prologue(c_row, w_ada, b_ada3, shards, out_dtypes):
    n = len(shards)

    def body(*refs):
        c_ref, w_ref, b_ref = refs[:3]
        mod_ref, scx_ref = refs[3 + n:5 + n]
        gather = _GatherSteps(refs[3:3 + n], refs[5 + n:5 + 2 * n], refs[5 + 2 * n:5 + 3 * n],
                              *refs[5 + 3 * n:8 + 3 * n])
        gather.start()
        _adaln_modulation(c_ref, w_ref, b_ref, mod_ref, scx_ref, *refs[8 + 3 * n:])
        gather.forward()
        gather.finish()

    outs = pl.pallas_call(
        body,
        name="prologue",
        out_shape=(jax.ShapeDtypeStruct((N_DEV, 1, MOD_COLS), F32), jax.ShapeDtypeStruct((2 * N_DEV, D), F32))
        + _gather_out_shapes(shards, out_dtypes),
        in_specs=[VMEM_SPEC] * (3 + n),
        out_specs=(VMEM_SPEC, VMEM_SPEC) + (ANY_SPEC,) * n,
        scratch_shapes=_gather_scratch(shards, out_dtypes) + [
            pltpu.VMEM((N_DEV, 1, D), F32),
            pltpu.VMEM((N_DEV, 1, MOD_COLS), F32),
            pltpu.VMEM((N_DEV, 1, MOD_COLS), F32),
            pltpu.SemaphoreType.DMA((2, N_DEV)),
            pltpu.SemaphoreType.DMA((2, N_DEV)),
        ],
        compiler_params=pltpu.CompilerParams(vmem_limit_bytes=VMEM_LIMIT_V7X),
    )(c_row, w_ada, b_ada3, *shards)
    return outs[0], outs[1], outs[2:]


class _ChipExchangeSteps:
    FLIPS = ((1, 0), (0, 1), (1, 1))

    def __init__(self, srcs, dsts, send_sems, recv_sems):
        self.srcs, self.dsts, self.send_sems, self.recv_sems = srcs, dsts, send_sems, recv_sems

    def _copies(self):
        x, y, c = _coords()
        out = []
        for a in range(len(self.srcs)):
            for j, (fx, fy) in enumerate(self.FLIPS):
                k = 3 * a + j
                out.append(pltpu.make_async_remote_copy(
                    self.srcs[a].at[x ^ fx, y ^ fy], self.dsts[a].at[j], self.send_sems.at[k], self.recv_sems.at[k],
                    device_id=(x ^ fx, y ^ fy, c), device_id_type=MESH))
        return out

    def start(self):
        for cp in self._copies():
            cp.start()

    def finish(self):
        for cp in self._copies():
            cp.wait()


def _mix_fwd(x, mod, g_pre, g_post, w_in_b, sgn, w_sp, b_sp_t, w_pool, p_scale, w_out_b, ts, shards, shard_dtypes):
    t_len = x.shape[0]
    nt, nb = t_len // ts, ts // HEAD
    ns = len(shards)

    def body(*refs):
        (x_ref, mod_ref, g1_ref, g2_ref, win_ref, sgn_ref, ws_ref, bst_ref, wp_ref, ps_ref, wout_ref) = refs[:11]
        x1_ref, proj_ref, mixed_ref = refs[11 + ns:14 + ns]
        pbuf, cat = refs[14 + 2 * ns:16 + 2 * ns]
        gather = _GatherSteps(refs[11:11 + ns], refs[14 + ns:14 + 2 * ns], refs[16 + 2 * ns:16 + 3 * ns],
                              *refs[16 + 3 * ns:])
        i = pl.program_id(0)

        @pl.when(i == 0)
        def _():
            pbuf[0:POOL_HALO, :] = jnp.zeros((POOL_HALO, B_WIDTH), F32)
            gather.start()

        @pl.when(i == (3 * nt) // 4)
        def _():
            gather.forward()

        xv = x_ref[...]
        sh, sc, gm = mod_ref[0:1, :], mod_ref[1:2, :], mod_ref[2:3, :]
        h = (xv * _rstd(xv) * g1_ref[...]) * (1.0 + sc) + sh
        proj_ref[...] = _mm(_bf(h), win_ref[...])
        pbuf[POOL_HALO:POOL_HALO + ts, :] = proj_ref[:, 2 * A_WIDTH:]
        smask = _sgu_mask()
        for hd in range(N_HEAD):
            u = _gelu(proj_ref[:, hd * HEAD:(hd + 1) * HEAD])
            v = _gelu(proj_ref[:, A_WIDTH + hd * HEAD:A_WIDTH + (hd + 1) * HEAD])
            vn = _bf(v * _rstd(v) * sgn_ref[hd:hd + 1, :])
            wm = _bf(jnp.where(smask, ws_ref[hd], 0.0))
            bias = bst_ref[:, hd:hd + 1]
            for b in range(nb):
                rows = slice(b * HEAD, (b + 1) * HEAD)
                z = _mm(wm, vn[rows]) + bias
                cat[rows, hd * HEAD:(hd + 1) * HEAD] = _bf(u[rows] * z)
        for g, w in enumerate(WINDOWS):
            cols = slice(g * HEAD, (g + 1) * HEAD)
            ext = pbuf[:, cols]
            pooled = _window_sum(ext, w, True)[POOL_HALO:] * _inv_count(i * ts, ts, w) - ext[POOL_HALO:]
            cat[:, A_WIDTH + g * HEAD:A_WIDTH + (g + 1) * HEAD] = _bf(_mm(_bf(pooled), _bf(wp_ref[g])) * ps_ref[:, cols])
        pbuf[0:POOL_HALO, :] = pbuf[ts:ts + POOL_HALO, :]
        mixed = _mm(cat[...], wout_ref[...])
        mixed_ref[...] = mixed
        x1_ref[...] = xv + gm * (mixed * _rstd(mixed) * g2_ref[...])

        @pl.when(i == nt - 1)
        def _():
            gather.finish()

    tile = lambda wid: pl.BlockSpec((ts, wid), lambda i: (i, 0))
    outs = pl.pallas_call(
        body,
        name="mix_fwd",
        grid=(nt,),
        out_shape=(jax.ShapeDtypeStruct((t_len, D), F32), jax.ShapeDtypeStruct((t_len, IN_WIDTH), F32),
                   jax.ShapeDtypeStruct((t_len, D), F32)) + _gather_out_shapes(shards, shard_dtypes),
        in_specs=[tile(D)] + [VMEM_SPEC] * (10 + ns),
        out_specs=(tile(D), tile(IN_WIDTH), tile(D)) + (ANY_SPEC,) * ns,
        scratch_shapes=[pltpu.VMEM((POOL_HALO + ts, B_WIDTH), F32), pltpu.VMEM((ts, D), BF16)]
        + _gather_scratch(shards, shard_dtypes),
        compiler_params=pltpu.CompilerParams(dimension_semantics=("arbitrary",), vmem_limit_bytes=VMEM_LIMIT_V7X),
    )(x, mod, g_pre, g_post, w_in_b, sgn, w_sp, b_sp_t, w_pool, p_scale, w_out_b, *shards)
    return outs[:3], outs[3:]


N_HALVES = 2
PAIRS = N_DEV // 2 // N_HALVES


def _ffn_fwd(x1, target, mod, g_pre, g_post, w_up_b, conv_w8, conv_b8, w_down_b4, ts):
    t_len = x1.shape[0]
    nt = t_len // ts
    half = N_DEV // 2

    def body(x1_ref, tgt_ref, mod_ref, g3_ref, g4_ref, wup_ref, cw_ref, cb_ref, wdown_ref,
             up_ref, y_ref, act_ref, f_ref, dx2_ref, loss_ref, ucarry, h2buf, facc):
        i, s = pl.program_id(0), pl.program_id(1)

        @pl.when((i == 0) & (s == 0))
        def _():
            ucarry[...] = jnp.zeros(ucarry.shape, F32)
            loss_ref[...] = jnp.zeros(loss_ref.shape, F32)

        @pl.when(s == 0)
        def _():
            x1v = x1_ref[...]
            sh, sc = mod_ref[3:4, :], mod_ref[4:5, :]
            h2buf[...] = _bf((x1v * _rstd(x1v) * g3_ref[...]) * (1.0 + sc) + sh)
            facc[...] = jnp.zeros((ts, D), F32)

        h2 = h2buf[...]

        def up_pair(q):
            return [_mm(h2, wup_ref[jj]) for jj in (PAIRS * s + q, half + PAIRS * s + q)]

        ups = up_pair(0)
        for q in range(PAIRS):
            nxt = up_pair(q + 1) if q + 1 < PAIRS else None
            ys = []
            for up, pos, jj in zip(ups, (q, PAIRS + q), (PAIRS * s + q, half + PAIRS * s + q)):
                up_ref[pos] = up
                before = ucarry[jj]
                ucarry[jj] = up[ts - HALO:, :]
                cw = cw_ref[jj]
                y = (cb_ref[jj] + _shift_down(up, before, 2) * cw[0:1, :]
                     + _shift_down(up, before, 1) * cw[1:2, :] + up * cw[2:3, :])
                y_ref[pos] = y
                ys.append(y)
            gate, val = ys
            act = _bf(gate * _sigmoid(gate) * val)
            act_ref[q] = act
            facc[...] += _mm(act, wdown_ref[PAIRS * s + q])
            ups = nxt

        @pl.when(s == N_HALVES - 1)
        def _():
            x1v = x1_ref[...]
            gf = mod_ref[5:6, :]
            f = facc[...]
            f_ref[...] = f
            x2 = x1v + gf * (f * _rstd(f) * g4_ref[...])
            err = x2 - tgt_ref[...]
            loss_ref[...] += _sum0(err * err)
            dx2_ref[...] = err * (1.0 / D)

    tile = pl.BlockSpec((ts, D), lambda i, s: (i, 0))
    chunks = lambda n: pl.BlockSpec((n, ts, FF_CHUNK), lambda i, s: (s, i, 0))
    return pl.pallas_call(
        body,
        name="ffn_fwd",
        grid=(nt, N_HALVES),
        out_shape=(jax.ShapeDtypeStruct((N_DEV, t_len, FF_CHUNK), F32), jax.ShapeDtypeStruct((N_DEV, t_len, FF_CHUNK), F32),
                   jax.ShapeDtypeStruct((half, t_len, FF_CHUNK), BF16), jax.ShapeDtypeStruct((t_len, D), F32),
                   jax.ShapeDtypeStruct((t_len, D), F32), jax.ShapeDtypeStruct((1, D), F32)),
        in_specs=[tile, tile] + [VMEM_SPEC] * 7,
        out_specs=(chunks(2 * PAIRS), chunks(2 * PAIRS), chunks(PAIRS), tile, tile,
                   pl.BlockSpec((1, D), lambda i, s: (0, 0))),
        scratch_shapes=[pltpu.VMEM((N_DEV, HALO, FF_CHUNK), F32), pltpu.VMEM((ts, D), BF16), pltpu.VMEM((ts, D), F32)],
        compiler_params=pltpu.CompilerParams(dimension_semantics=("arbitrary", "arbitrary"),
                                             vmem_limit_bytes=VMEM_LIMIT_V7X),
    )(x1, target, mod, g_pre, g_post, w_up_b, conv_w8, conv_b8, w_down_b4)


def _ffn_bwd(dx2, f, x1, up, y, mod, g_pre, g_post, w_up_b, conv_w8, w_down_b4, ts):
    t_len = x1.shape[0]
    nt = t_len // ts
    half = N_DEV // 2

    def body(dx2_ref, f_ref, x1_ref, up_ref, y_ref, mod_ref, g3_ref, g4_ref, wup_ref, cw_ref, wdown_ref,
             dx1_ref, dup_ref, df_ref, h2_ref, dmod_ref, dg3_ref, dg4_ref, dcb_ref, dcw_ref,
             dycarry, dh2acc, dfbuf):
        i, s = pl.program_id(0), pl.program_id(1)

        @pl.when((i == 0) & (s == 0))
        def _():
            for ref in (dmod_ref, dg3_ref, dg4_ref, dcb_ref, dcw_ref, dycarry):
                ref[...] = jnp.zeros(ref.shape, F32)

        @pl.when(s == 0)
        def _():
            dx2v, fv, x1v = dx2_ref[...], f_ref[...], x1_ref[...]
            sh, sc, gf = mod_ref[3:4, :], mod_ref[4:5, :], mod_ref[5:6, :]
            g4 = g4_ref[...]
            rstd4 = _rstd(fv)
            fh = fv * rstd4
            dmod_ref[2:3, :] += _sum0(dx2v * (fh * g4))
            dr = dx2v * gf
            dg4_ref[...] += _sum0(dr * fh)
            dfh = dr * g4
            dfb = _bf(rstd4 * (dfh - fh * _rowmean(dfh * fh)))
            df_ref[...] = dfb
            dfbuf[...] = dfb
            h2_ref[...] = _bf((x1v * _rstd(x1v) * g3_ref[...]) * (1.0 + sc) + sh)
            dh2acc[...] = jnp.zeros((ts, D), F32)

        dfb = dfbuf[...]

        def dact_of(q):
            return _mm_nt(dfb, wdown_ref[PAIRS * s + q])

        dact_next = dact_of(0)
        for q in range(PAIRS):
            dact = dact_next
            if q + 1 < PAIRS:
                dact_next = dact_of(q + 1)
            gate, val = y_ref[q], y_ref[PAIRS + q]
            sg = _sigmoid(gate)
            dys = (dact * val * (sg * (1.0 + gate * (1.0 - sg))), dact * (gate * sg))
            for dy, pos, jj in zip(dys, (q, PAIRS + q), (PAIRS * s + q, half + PAIRS * s + q)):
                cw = cw_ref[jj]
                dcb_ref[jj] += _sum0(dy)
                after = dycarry[jj]
                dycarry[jj] = dy[0:HALO, :]
                dy1, dy2 = _shift_up(dy, after, 1), _shift_up(dy, after, 2)
                upc = up_ref[pos]
                dcw_ref[jj, 0:1, :] += _sum0(dy2 * upc)
                dcw_ref[jj, 1:2, :] += _sum0(dy1 * upc)
                dcw_ref[jj, 2:3, :] += _sum0(dy * upc)
                dup = _bf(dy * cw[2:3, :] + dy1 * cw[1:2, :] + dy2 * cw[0:1, :])
                dup_ref[pos] = dup
                dh2acc[...] += _mm_nt(dup, wup_ref[jj])

        @pl.when(s == N_HALVES - 1)
        def _():
            dx2v, x1v = dx2_ref[...], x1_ref[...]
            sc = mod_ref[4:5, :]
            g3 = g3_ref[...]
            rstd3 = _rstd(x1v)
            xh = x1v * rstd3
            n3 = xh * g3
            dh2 = dh2acc[...]
            dmod_ref[0:1, :] += _sum0(dh2)
            dmod_ref[1:2, :] += _sum0(dh2 * n3)
            dn3 = dh2 * (1.0 + sc)
            dg3_ref[...] += _sum0(dn3 * xh)
            dxh = dn3 * g3
            dx1_ref[...] = dx2v + rstd3 * (dxh - xh * _rowmean(dxh * xh))

    tile = pl.BlockSpec((ts, D), lambda i, s: (nt - 1 - i, 0))
    chunks = pl.BlockSpec((2 * PAIRS, ts, FF_CHUNK), lambda i, s: (s, nt - 1 - i, 0))
    const = lambda *shape: pl.BlockSpec(shape, lambda i, s: (0,) * len(shape))
    return pl.pallas_call(
        body,
        name="ffn_bwd",
        grid=(nt, N_HALVES),
        out_shape=(jax.ShapeDtypeStruct((t_len, D), F32), jax.ShapeDtypeStruct((N_DEV, t_len, FF_CHUNK), BF16),
                   jax.ShapeDtypeStruct((t_len, D), BF16), jax.ShapeDtypeStruct((t_len, D), BF16),
                   jax.ShapeDtypeStruct((3, D), F32), jax.ShapeDtypeStruct((1, D), F32), jax.ShapeDtypeStruct((1, D), F32),
                   jax.ShapeDtypeStruct((N_DEV, 1, FF_CHUNK), F32), jax.ShapeDtypeStruct((N_DEV, 3, FF_CHUNK), F32)),
        in_specs=[tile, tile, tile, chunks, chunks] + [VMEM_SPEC] * 6,
        out_specs=(tile, chunks, tile, tile, const(3, D), const(1, D), const(1, D),
                   const(N_DEV, 1, FF_CHUNK), const(N_DEV, 3, FF_CHUNK)),
        scratch_shapes=[pltpu.VMEM((N_DEV, HALO, FF_CHUNK), F32), pltpu.VMEM((ts, D), F32), pltpu.VMEM((ts, D), BF16)],
        compiler_params=pltpu.CompilerParams(dimension_semantics=("arbitrary", "arbitrary"),
                                             vmem_limit_bytes=VMEM_LIMIT_V7X),
    )(dx2, f, x1, up, y, mod, g_pre, g_post, w_up_b, conv_w8, w_down_b4)


def _wgrad_up(h2, dup, ts):
    t_len = h2.shape[0]
    nt = t_len // ts

    def body(h2_ref, dup_ref, out_ref):
        @pl.when(pl.program_id(1) == 0)
        def _():
            out_ref[...] = jnp.zeros(out_ref.shape, F32)

        for q in range(PAIRS):
            out_ref[q] += _mm_tn(h2_ref[...], dup_ref[q])

    return pl.pallas_call(
        body,
        name="wgrad_up",
        grid=(N_DEV // PAIRS, nt),
        out_shape=jax.ShapeDtypeStruct((N_DEV, D, FF_CHUNK), F32),
        in_specs=[pl.BlockSpec((ts, D), lambda p, t: (t, 0)), pl.BlockSpec((PAIRS, ts, FF_CHUNK), lambda p, t: (p, t, 0))],
        out_specs=pl.BlockSpec((PAIRS, D, FF_CHUNK), lambda p, t: (2 * (p % 2) + p // 2, 0, 0)),
        compiler_params=pltpu.CompilerParams(dimension_semantics=("arbitrary", "arbitrary"),
                                             vmem_limit_bytes=VMEM_LIMIT_V7X),
    )(h2, dup)


def _sibling_swap_copies(srcs, dsts, send_sems, recv_sems):
    x, y, c = _coords()
    return [
        pltpu.make_async_remote_copy(srcs[a].at[xs, ys, 1 - c], dsts[a].at[xs, ys], send_sems.at[a, 2 * xs + ys],
                                     recv_sems.at[a, 2 * xs + ys], device_id=(x, y, 1 - c), device_id_type=MESH)
        for a in range(len(srcs)) for xs in range(2) for ys in range(2)
    ]


def _wgrad_down(act, df, ts, swap_src):
    t_len = df.shape[0]
    nt, half = t_len // ts, N_DEV // 2

    def body(act_ref, df_ref, src_ref, out_ref, dst_ref, send_sems, recv_sems):
        t = pl.program_id(0)

        @pl.when(t == 0)
        def _():
            for cp in _sibling_swap_copies([src_ref], [dst_ref], send_sems, recv_sems):
                cp.start()
            out_ref[...] = jnp.zeros(out_ref.shape, F32)

        for q in range(half):
            out_ref[q] += _mm_tn(act_ref[q], df_ref[...])

        @pl.when(t == nt - 1)
        def _():
            for cp in _sibling_swap_copies([src_ref], [dst_ref], send_sems, recv_sems):
                cp.wait()

    return pl.pallas_call(
        body,
        name="wgrad_down",
        grid=(nt,),
        out_shape=(jax.ShapeDtypeStruct((half, FF_CHUNK, D), F32), jax.ShapeDtypeStruct(swap_src.shape[1:], F32)),
        in_specs=[pl.BlockSpec((half, ts, FF_CHUNK), lambda t: (0, t, 0)), pl.BlockSpec((ts, D), lambda t: (t, 0)),
                  ANY_SPEC],
        out_specs=(pl.BlockSpec((half, FF_CHUNK, D), lambda t: (0, 0, 0)), ANY_SPEC),
        scratch_shapes=[pltpu.SemaphoreType.DMA((1, 4)), pltpu.SemaphoreType.DMA((1, 4))],
        compiler_params=pltpu.CompilerParams(dimension_semantics=("arbitrary",), vmem_limit_bytes=VMEM_LIMIT_V7X),
    )(act, df, swap_src)


def _mix_bwd(dx1, x, proj, mixed, mod, g_pre, g_post, w_in_b, sgn, w_sp, b_sp_t, w_pool, p_scale, w_out_b, ts, rs_srcs):
    t_len = x.shape[0]
    nt, nb = t_len // ts, ts // HEAD
    nr = len(rs_srcs)

    def body(*refs):
        (dx1_ref, x_ref, proj_ref, halo_ref, mixed_ref, mod_ref, g1_ref, g2_ref, win_ref, sgn_ref, ws_ref,
         bst_ref, wp_ref, ps_ref, wout_ref) = refs[:15]
        (gx_ref, dwin_ref, dwout_ref, dmod_ref, dg1_ref, dg2_ref, dsgn_ref, dws_ref, dbst_ref, dwp_ref,
         dps_ref) = refs[15 + nr:26 + nr]
        pbuf, dwsbuf, cat, dproj, dcat = refs[26 + 2 * nr:31 + 2 * nr]
        exchange = _ChipExchangeSteps(refs[15:15 + nr], refs[26 + nr:26 + 2 * nr], *refs[31 + 2 * nr:])
        i = pl.program_id(0)
        r = nt - 1 - i

        @pl.when(i == 0)
        def _():
            exchange.start()
            for ref in (dwin_ref, dwout_ref, dmod_ref, dg1_ref, dg2_ref, dsgn_ref, dws_ref, dbst_ref, dwp_ref, dps_ref):
                ref[...] = jnp.zeros(ref.shape, F32)
            dwsbuf[ts:ts + POOL_HALO, :] = jnp.zeros((POOL_HALO, B_WIDTH), F32)

        xv, dx1v, mixed = x_ref[...], dx1_ref[...], mixed_ref[...]
        sh, sc, gm = mod_ref[0:1, :], mod_ref[1:2, :], mod_ref[2:3, :]
        g1, g2 = g1_ref[...], g2_ref[...]
        rstd2 = _rstd(mixed)
        mh = mixed * rstd2
        dmod_ref[2:3, :] += _sum0(dx1v * (mh * g2))
        dr = dx1v * gm
        dg2_ref[...] += _sum0(dr * mh)
        dmh = dr * g2
        dmb = _bf(rstd2 * (dmh - mh * _rowmean(dmh * mh)))
        dcat[...] = _mm_nt(dmb, wout_ref[...])
        smask = _sgu_mask()
        for hd in range(N_HEAD):
            ucols = slice(hd * HEAD, (hd + 1) * HEAD)
            vcols = slice(A_WIDTH + hd * HEAD, A_WIDTH + (hd + 1) * HEAD)
            u, du_dp = _gelu_and_grad(proj_ref[:, ucols])
            v, dv_dp = _gelu_and_grad(proj_ref[:, vcols])
            rs = _rstd(v)
            vhat = v * rs
            gn = sgn_ref[hd:hd + 1, :]
            vn = _bf(vhat * gn)
            wm = _bf(jnp.where(smask, ws_ref[hd], 0.0))
            bias = bst_ref[:, hd:hd + 1]
            dzsum = jnp.zeros((HEAD, HEAD), F32)
            dwm = jnp.zeros((HEAD, HEAD), F32)
            dvn_parts = []
            for b in range(nb):
                rows = slice(b * HEAD, (b + 1) * HEAD)
                z = _mm(wm, vn[rows]) + bias
                da = dcat[rows, ucols]
                cat[rows, ucols] = _bf(u[rows] * z)
                dz = da * u[rows]
                dzsum = dzsum + dz
                dzb = _bf(dz)
                dwm = dwm + _mm_nt(dzb, vn[rows])
                dvn_parts.append(_mm_tn(wm, dzb))
                dproj[rows, ucols] = _bf((da * z) * du_dp[rows])
            dvn = jnp.concatenate(dvn_parts, axis=0)
            dsgn_ref[hd:hd + 1, :] += _sum0(dvn * vhat)
            dvh = dvn * gn
            dproj[:, vcols] = _bf((rs * (dvh - vhat * _rowmean(dvh * vhat))) * dv_dp)
            dws_ref[hd] += jnp.where(smask, dwm, 0.0)
            dbst_ref[:, hd:hd + 1] += jnp.sum(dzsum, axis=1, keepdims=True)
        keep = jnp.where(r > 0, 1.0, 0.0).astype(F32)
        pbuf[0:POOL_HALO, :] = halo_ref[...] * keep
        pbuf[POOL_HALO:POOL_HALO + ts, :] = proj_ref[:, 2 * A_WIDTH:]
        for g, w in enumerate(WINDOWS):
            cols = slice(g * HEAD, (g + 1) * HEAD)
            ccols = slice(A_WIDTH + g * HEAD, A_WIDTH + (g + 1) * HEAD)
            pcols = slice(2 * A_WIDTH + g * HEAD, 2 * A_WIDTH + (g + 1) * HEAD)
            wpg = _bf(wp_ref[g])
            psg = ps_ref[:, cols]
            ext = pbuf[:, cols]
            inv = _inv_count(r * ts, ts, w)
            pb = _bf(_window_sum(ext, w, True)[POOL_HALO:] * inv - ext[POOL_HALO:])
            yb = _mm(pb, wpg)
            dob = dcat[:, ccols]
            cat[:, ccols] = _bf(yb * psg)
            dps_ref[:, cols] += _sum0(dob * yb)
            dyb = _bf(dob * psg)
            dwp_ref[g] += _mm_tn(pb, dyb)
            dpooled = _mm_nt(dyb, wpg)
            dwsbuf[0:ts, cols] = dpooled * inv
            dproj[:, pcols] = _bf(_window_sum(dwsbuf[:, cols], w, False)[0:ts] - dpooled)
        dwsbuf[ts:ts + POOL_HALO, :] = dwsbuf[0:POOL_HALO, :]
        dpb = dproj[...]
        rstd1 = _rstd(xv)
        xh = xv * rstd1
        n1 = xh * g1
        dwin_ref[...] += _mm_tn(_bf(n1 * (1.0 + sc) + sh), dpb)
        dwout_ref[...] += _mm_tn(cat[...], dmb)
        dh = _mm_nt(dpb, win_ref[...])
        dmod_ref[0:1, :] += _sum0(dh)
        dmod_ref[1:2, :] += _sum0(dh * n1)
        dn1 = dh * (1.0 + sc)
        dg1_ref[...] += _sum0(dn1 * xh)
        dxh = dn1 * g1
        gx_ref[...] = dx1v + rstd1 * (dxh - xh * _rowmean(dxh * xh))

        @pl.when(i == nt - 1)
        def _():
            exchange.finish()

    tile = lambda wid: pl.BlockSpec((ts, wid), lambda i: (nt - 1 - i, 0))
    halo = pl.BlockSpec((POOL_HALO, B_WIDTH),
                        lambda i: (jnp.maximum((nt - 1 - i) * (ts // POOL_HALO) - 1, 0), 2 * A_WIDTH // B_WIDTH))
    const = lambda *shape: pl.BlockSpec(shape, lambda i: (0,) * len(shape))
    resident = lambda *shape: pl.BlockSpec(shape, lambda i: (0,) * len(shape), pipeline_mode=pl.Buffered(1))
    outs = pl.pallas_call(
        body,
        name="mix_bwd",
        grid=(nt,),
        out_shape=(jax.ShapeDtypeStruct((t_len, D), F32), jax.ShapeDtypeStruct((D, IN_WIDTH), F32),
                   jax.ShapeDtypeStruct((D, D), F32), jax.ShapeDtypeStruct((3, D), F32),
                   jax.ShapeDtypeStruct((1, D), F32), jax.ShapeDtypeStruct((1, D), F32),
                   jax.ShapeDtypeStruct((N_HEAD, HEAD), F32), jax.ShapeDtypeStruct((N_HEAD, HEAD, HEAD), F32),
                   jax.ShapeDtypeStruct((HEAD, N_HEAD), F32), jax.ShapeDtypeStruct((N_HEAD, HEAD, HEAD), F32),
                   jax.ShapeDtypeStruct((1, B_WIDTH), F32))
        + tuple(jax.ShapeDtypeStruct((3, *s.shape[2:]), s.dtype) for s in rs_srcs),
        in_specs=[tile(D), tile(D), tile(IN_WIDTH), halo, tile(D)] + [VMEM_SPEC] * 10 + [ANY_SPEC] * nr,
        out_specs=(tile(D), resident(D, IN_WIDTH), resident(D, D), const(3, D), const(1, D), const(1, D),
                   const(N_HEAD, HEAD), const(N_HEAD, HEAD, HEAD), const(HEAD, N_HEAD), const(N_HEAD, HEAD, HEAD),
                   const(1, B_WIDTH)) + (ANY_SPEC,) * nr,
        scratch_shapes=[pltpu.VMEM((POOL_HALO + ts, B_WIDTH), F32), pltpu.VMEM((ts + POOL_HALO, B_WIDTH), F32),
                        pltpu.VMEM((ts, D), BF16), pltpu.VMEM((ts, IN_WIDTH), BF16), pltpu.VMEM((ts, D), F32),
                        pltpu.SemaphoreType.DMA((3 * nr,)), pltpu.SemaphoreType.DMA((3 * nr,))],
        compiler_params=pltpu.CompilerParams(dimension_semantics=("arbitrary",), vmem_limit_bytes=VMEM_LIMIT_V7X),
    )(dx1, x, proj, proj, mixed, mod, g_pre, g_post, w_in_b, sgn, w_sp, b_sp_t, w_pool, p_scale, w_out_b, *rs_srcs)
    return outs[:11], outs[11:]


def _pair_add(name, coords, grid, specs_a, specs_b, out_specs, out_shapes, a_arrays, b_arrays):
    n = len(a_arrays)

    def body(co_ref, *refs):
        for k in range(n):
            total = refs[k][...] + refs[n + k][...]
            refs[2 * n + k][...] = total
            refs[3 * n + k][...] = _bf(total)

    outs = pl.pallas_call(
        body,
        name=name,
        grid_spec=pltpu.PrefetchScalarGridSpec(num_scalar_prefetch=1, grid=grid, in_specs=specs_a + specs_b,
                                               out_specs=out_specs * 2),
        out_shape=tuple(jax.ShapeDtypeStruct(s, dt) for dt in (F32, BF16) for s in out_shapes),
        compiler_params=pltpu.CompilerParams(dimension_semantics=("arbitrary",) * len(grid),
                                             vmem_limit_bytes=VMEM_LIMIT_V7X),
    )(coords, *a_arrays, *b_arrays)
    return list(outs[:n]), list(outs[n:])


def _final_add_adamw(coords, s1, r, ws, ms, vs, n_split=4):
    n = len(s1)

    def body(co_ref, *refs):
        for k in range(n):
            s_ref, r_ref, w_ref, m_ref, v_ref = (refs[q * n + k] for q in range(5))
            g_ref, d_ref, nm_ref, nv_ref = (refs[(5 + q) * n + k] for q in range(4))
            g = ((s_ref[...] + r_ref[0].astype(F32)) + r_ref[1].astype(F32)) + r_ref[2].astype(F32)
            g_ref[...] = g
            delta, m, v = _adamw(w_ref[...], g, m_ref[...], v_ref[...])
            d_ref[...] = delta
            nm_ref[...] = m
            nv_ref[...] = v

    def shard_spec(a):
        rows, cols = a.shape
        return pl.BlockSpec((rows // n_split, cols), lambda i, co: (i, 0))

    def mine_spec(a):
        rows, cols = a.shape[2:]
        return pl.BlockSpec((None, None, rows // n_split, cols), lambda i, co: (co[0], co[1], i, 0))

    def recv_spec(a):
        rows, cols = a.shape[1:]
        return pl.BlockSpec((3, rows // n_split, cols), lambda i, co: (0, i, 0))

    in_specs = ([mine_spec(a) for a in s1] + [recv_spec(a) for a in r] + [shard_spec(a) for a in ws] * 3)
    out_specs = [shard_spec(a) for a in ws] * 4
    outs = pl.pallas_call(
        body,
        name="grad_final_adamw",
        grid_spec=pltpu.PrefetchScalarGridSpec(num_scalar_prefetch=1, grid=(n_split,), in_specs=in_specs,
                                               out_specs=out_specs),
        out_shape=tuple(jax.ShapeDtypeStruct(a.shape, F32) for a in ws) * 4,
        compiler_params=pltpu.CompilerParams(dimension_semantics=("arbitrary",), vmem_limit_bytes=VMEM_LIMIT_V7X),
    )(coords, *s1, *r, *ws, *ms, *vs)
    return [tuple(outs[q * n + k] for q in range(4)) for k in range(n)]


def _sibling_swap(tag, g5):
    n = len(g5)

    def body(*refs):
        copies = _sibling_swap_copies(refs[:n], refs[n:2 * n], *refs[2 * n:])
        for cp in copies:
            cp.start()
        for cp in copies:
            cp.wait()

    return list(pl.pallas_call(
        body,
        name="grad_swap_core_" + tag,
        out_shape=tuple(jax.ShapeDtypeStruct(g.shape[1:], F32) for g in g5),
        in_specs=[ANY_SPEC] * n,
        out_specs=(ANY_SPEC,) * n,
        scratch_shapes=[pltpu.SemaphoreType.DMA((n, 4)), pltpu.SemaphoreType.DMA((n, 4))],
    )(*g5))


def _sibling_add(tag, g5, r1, coords, n_split=4):
    shapes = [g.shape[3:] for g in g5]
    spec_g = [pl.BlockSpec((None, None, None, s[0] // n_split, s[1]), lambda i, j, k, co: (i, j, co[2], k, 0))
              for s in shapes]
    spec_r = [pl.BlockSpec((None, None, s[0] // n_split, s[1]), lambda i, j, k, co: (i, j, k, 0)) for s in shapes]
    return _pair_add("grad_add_core_" + tag, coords, (2, 2, n_split), spec_g, spec_r, spec_r,
                     [(2, 2, *s) for s in shapes], g5, r1)


def _tail_exchange(big, partials, pick_mine, dmod3):
    n, nb = len(partials), len(big)
    big_shapes = [g.shape[1:] for g in big]
    big5 = [g.reshape(2, 2, 2, *s) for g, s in zip(big, big_shapes)]
    flips = _ChipExchangeSteps.FLIPS

    def body(*refs):
        g5, p_in, dm_ref = refs[:nb], refs[nb:nb + n], refs[nb + n]
        outs = refs[nb + n + 1:2 * nb + 2 * n + 2]
        g_out, sums, dm2d = outs[:nb], outs[nb:nb + n], outs[nb + n]
        scratch = refs[2 * nb + 2 * n + 2:]
        s1, stage, chip_recv = scratch[:nb], scratch[nb:2 * nb], scratch[2 * nb:3 * nb]
        acc, rbuf = scratch[3 * nb:3 * nb + n], scratch[3 * nb + n:3 * nb + 2 * n]
        (dm_recv, send_sems, recv_sems, dm_send_sems, dm_recv_sems, sib_send, sib_recv, chip_send,
         chip_recv_sems) = scratch[3 * nb + 2 * n:]
        x, y, c = _coords()
        me = 4 * x + 2 * y + c
        sibling = (x, y, 1 - c)
        dm_copies = [
            pltpu.make_async_remote_copy(dm_ref.at[me ^ k], dm_recv.at[k], dm_send_sems.at[k], dm_recv_sems.at[k],
                                         device_id=_peer(k), device_id_type=MESH)
            for k in range(1, N_DEV)
        ]
        for cp in dm_copies:
            cp.start()
        sib_copies = _sibling_swap_copies(g5, s1, sib_send, sib_recv)
        for cp in sib_copies:
            cp.start()
        for a in range(n):
            acc[a][...] = p_in[a][...]

        def small_phase(ph, peer):
            copies = [
                pltpu.make_async_remote_copy(acc[a], rbuf[a].at[ph], send_sems.at[ph, a], recv_sems.at[ph, a],
                                             device_id=peer, device_id_type=MESH)
                for a in range(n)
            ]
            for cp in copies:
                cp.start()
            for cp in copies:
                cp.wait()
            for a in range(n):
                acc[a][...] = acc[a][...] + rbuf[a][ph]

        small_phase(0, sibling)
        for cp in sib_copies:
            cp.wait()
        for a in range(nb):
            for xs in range(2):
                for ys in range(2):
                    total = g5[a][xs, ys, c] + s1[a][xs, ys]
                    s1[a][xs, ys] = total
                    stage[a][xs, ys] = _bf(total)
        chip_copies = [
            pltpu.make_async_remote_copy(stage[a].at[x ^ fx, y ^ fy], chip_recv[a].at[j], chip_send.at[a, j],
                                         chip_recv_sems.at[a, j], device_id=(x ^ fx, y ^ fy, c), device_id_type=MESH)
            for a in range(nb) for j, (fx, fy) in enumerate(flips)
        ]
        for cp in chip_copies:
            cp.start()
        small_phase(1, (1 - x, y, c))
        small_phase(2, (x, 1 - y, c))
        for a in range(n):
            sums[a][...] = acc[a][me] if pick_mine[a] else acc[a][...]
        dm2d[...] = jnp.zeros(dm2d.shape, F32)
        dm2d[0:1, :] = dm_ref[me]
        for cp in dm_copies:
            cp.wait()
        for k in range(1, N_DEV):
            dm2d[k:k + 1, :] = dm_recv[k]
        for cp in chip_copies:
            cp.wait()
        for a in range(nb):
            g_out[a][...] = ((s1[a][x, y] + chip_recv[a][0].astype(F32)) + chip_recv[a][1].astype(F32)) \
                + chip_recv[a][2].astype(F32)

    out_shapes = tuple(jax.ShapeDtypeStruct(s, F32) for s in big_shapes) + tuple(
        jax.ShapeDtypeStruct(p.shape[1:] if pk else p.shape, F32) for p, pk in zip(partials, pick_mine))
    outs = pl.pallas_call(
        body,
        name="tail_exchange",
        out_shape=out_shapes + (jax.ShapeDtypeStruct((2 * N_DEV, MOD_COLS), F32),),
        in_specs=[VMEM_SPEC] * (nb + n + 1),
        out_specs=(VMEM_SPEC,) * (nb + n + 1),
        scratch_shapes=[pltpu.VMEM((2, 2, *s), F32) for s in big_shapes]
        + [pltpu.VMEM((2, 2, *s), BF16) for s in big_shapes]
        + [pltpu.VMEM((3, *s), BF16) for s in big_shapes]
        + [pltpu.VMEM(p.shape, F32) for p in partials]
        + [pltpu.VMEM((3, *p.shape), F32) for p in partials]
        + [pltpu.VMEM((N_DEV, 1, MOD_COLS), F32), pltpu.SemaphoreType.DMA((3, n)), pltpu.SemaphoreType.DMA((3, n)),
           pltpu.SemaphoreType.DMA((N_DEV,)), pltpu.SemaphoreType.DMA((N_DEV,)),
           pltpu.SemaphoreType.DMA((nb, 4)), pltpu.SemaphoreType.DMA((nb, 4)),
           pltpu.SemaphoreType.DMA((nb, 3)), pltpu.SemaphoreType.DMA((nb, 3))],
        compiler_params=pltpu.CompilerParams(vmem_limit_bytes=VMEM_LIMIT_V7X),
    )(*big5, *partials, dmod3)
    return list(outs[:nb]), list(outs[nb:nb + n]), outs[nb + n]


def _small_update(grads, ws, ms, vs, scx, dm2d, w_ada, m_ada, v_ada, loss_lanes):
    n = len(grads)

    def body(*refs):
        g_in, w_in, m_in, v_in = (refs[q * n:(q + 1) * n] for q in range(4))
        scx_ref, dm_ref, wa_ref, ma_ref, va_ref, ll_ref = refs[4 * n:4 * n + 6]
        outs = refs[4 * n + 6:]
        g_out, d_out, nm_out, nv_out = (outs[q * (n + 1):(q + 1) * (n + 1)] for q in range(4))
        loss_ref = outs[4 * (n + 1)]
        for a in range(n + 1):
            if a < n:
                g, w, m, v = g_in[a][...], w_in[a][...], m_in[a][...], v_in[a][...]
            else:
                g = _mm_tn(_bf(scx_ref[...]), _bf(dm_ref[...]))
                w, m, v = wa_ref[...], ma_ref[...], va_ref[...]
            g_out[a][...] = g
            delta, m, v = _adamw(w, g, m, v)
            d_out[a][...] = delta
            nm_out[a][...] = m
            nv_out[a][...] = v
        loss_ref[...] = jnp.sum(ll_ref[...], axis=1, keepdims=True) * (0.5 / D)

    w_shapes = tuple(jax.ShapeDtypeStruct(w.shape, F32) for w in list(ws) + [w_ada])
    outs = pl.pallas_call(
        body,
        name="small_update",
        out_shape=w_shapes * 4 + (jax.ShapeDtypeStruct((1, 1), F32),),
        in_specs=[VMEM_SPEC] * (4 * n + 6),
        out_specs=(VMEM_SPEC,) * (4 * (n + 1) + 1),
        compiler_params=pltpu.CompilerParams(vmem_limit_bytes=VMEM_LIMIT_V7X),
    )(*grads, *ws, *ms, *vs, scx, dm2d, w_ada, m_ada, v_ada, loss_lanes)
    return [tuple(outs[q * (n + 1) + k] for q in range(4)) for k in range(n + 1)], outs[4 * (n + 1)]


def kernel(x, c, w_ada, b_ada, pre_mix_g, post_mix_g, w_in, sgu_norm_g, w_spatial, b_spatial, w_pool, pool_scale, w_out, pre_ffn_g, post_ffn_g, w_up, conv_w, conv_b, w_down, loss_target, m_w_ada, m_b_ada, m_pre_mix_g, m_post_mix_g, m_w_in, m_sgu_norm_g, m_w_spatial, m_b_spatial, m_w_pool, m_pool_scale, m_w_out, m_pre_ffn_g, m_post_ffn_g, m_w_up, m_conv_w, m_conv_b, m_w_down, v_w_ada, v_b_ada, v_pre_mix_g, v_post_mix_g, v_w_in, v_sgu_norm_g, v_w_spatial, v_b_spatial, v_w_pool, v_pool_scale, v_w_out, v_pre_ffn_g, v_post_ffn_g, v_w_up, v_conv_w, v_conv_b, v_w_down):
    t_len = x.shape[1]
    ts = min(256, t_len)
    ts_mix = min(512, t_len)
    ts_w = min(1024, t_len)
    coords = jnp.stack([lax.axis_index("x"), lax.axis_index("y"), lax.axis_index("c")]).astype(jnp.int32)

    mod3, scx, (g_in, g_out) = _prologue(c, w_ada[0], b_ada.reshape(N_DEV, 1, MOD_COLS), [w_in[0], w_out[0]],
                                         [BF16, BF16])
    mod = mod3.reshape(N_MOD, D)
    w_in_b = g_in.transpose(1, 0, 2).reshape(D, IN_WIDTH)
    w_out_b = g_out.reshape(D, D)
    conv_b8 = conv_b.reshape(N_DEV, 1, FF_CHUNK)
    b_sp_t = b_spatial[0].T

    x2d, tgt = x[0], loss_target[0]
    (x1, proj, mixed), (g_up, g_down, g_cw) = _mix_fwd(
        x2d, mod, pre_mix_g, post_mix_g, w_in_b, sgu_norm_g[0], w_spatial[0], b_sp_t, w_pool[0], pool_scale, w_out_b,
        ts_mix, [w_up[0], w_down[0], conv_w[0]], [BF16, BF16, F32])
    w_down_b4 = g_down.reshape(N_DEV // 2, FF_CHUNK, D)
    up, y, act, f, dx2, loss_lanes = _ffn_fwd(x1, tgt, mod, pre_ffn_g, post_ffn_g, g_up, g_cw, conv_b8, w_down_b4, ts)

    (dx1, dup, df, h2, dmod_f, d_pre_ffn, d_post_ffn, d_cb8, d_cw8) = _ffn_bwd(
        dx2, f, x1, up, y, mod, pre_ffn_g, post_ffn_g, g_up, g_cw, w_down_b4, ts)
    gw_up = _wgrad_up(h2, dup, ts_w).reshape(2, 2, 2, D, FF_CHUNK)
    gw_down, r1_up = _wgrad_down(act, df, ts_w, gw_up)
    gw_down = gw_down.reshape(2, 2, 2, FF // N_DEV, D)
    s1_ffn, s1_ffn_b = _sibling_add("ffn", [gw_up, gw_down], [r1_up] + _sibling_swap("down", [gw_down]), coords)
    ((grad_x, gw_in, gw_out, dmod_m, d_pre_mix, d_post_mix, d_sgn, d_wsp, d_bsp_t, d_wpool, d_ps), r_ffn) = _mix_bwd(
        dx1, x2d, proj, mixed, mod, pre_mix_g, post_mix_g, w_in_b, sgu_norm_g[0], w_spatial[0], b_sp_t,
        w_pool[0], pool_scale, w_out_b, ts, s1_ffn_b)
    gw_in = gw_in.reshape(D, N_DEV, IN_WIDTH // N_DEV).transpose(1, 0, 2)
    gw_out = gw_out.reshape(N_DEV, D // N_DEV, D)

    big = _final_add_adamw(coords, s1_ffn, list(r_ffn), [w_up[0], w_down[0]], [m_w_up[0], m_w_down[0]],
                           [v_w_up[0], v_w_down[0]])
    r_up, r_down = [tuple(a[None] for a in four) for four in big]

    dmod = jnp.concatenate([dmod_m, dmod_f], axis=0)
    names = ["b_ada", "pre_mix_g", "post_mix_g", "sgu_norm_g", "w_spatial", "b_spatial", "w_pool", "pool_scale",
             "pre_ffn_g", "post_ffn_g", "conv_w", "conv_b"]
    partials = [dmod.reshape(1, N_MOD * D), d_pre_mix, d_post_mix, d_sgn, d_wsp, d_bsp_t.T, d_wpool, d_ps,
                d_pre_ffn, d_post_ffn, d_cw8, d_cb8.reshape(1, 2 * FF), loss_lanes]
    small_w = [b_ada, pre_mix_g, post_mix_g, sgu_norm_g[0], w_spatial[0], b_spatial[0], w_pool[0], pool_scale,
               pre_ffn_g, post_ffn_g, conv_w[0], conv_b]
    small_m = [m_b_ada, m_pre_mix_g, m_post_mix_g, m_sgu_norm_g[0], m_w_spatial[0], m_b_spatial[0], m_w_pool[0],
               m_pool_scale, m_pre_ffn_g, m_post_ffn_g, m_conv_w[0], m_conv_b]
    small_v = [v_b_ada, v_pre_mix_g, v_post_mix_g, v_sgu_norm_g[0], v_w_spatial[0], v_b_spatial[0], v_w_pool[0],
               v_pool_scale, v_pre_ffn_g, v_post_ffn_g, v_conv_w[0], v_conv_b]
    g_mix, sums, dm2d = _tail_exchange([gw_in, gw_out], partials, [nm == "conv_w" for nm in names] + [False],
                                       dmod.reshape(N_DEV, 1, MOD_COLS))
    small, loss11 = _small_update(
        sums[:-1] + g_mix, small_w + [w_in[0], w_out[0]], small_m + [m_w_in[0], m_w_out[0]],
        small_v + [v_w_in[0], v_w_out[0]], scx, dm2d, w_ada[0], m_w_ada[0], v_w_ada[0], sums[-1])
    loss = loss11.reshape(())
    lead = {"sgu_norm_g", "w_spatial", "b_spatial", "w_pool", "conv_w", "w_in", "w_out", "w_ada"}
    res = {nm: tuple(a[None] if nm in lead else a for a in four)
           for nm, four in zip(names + ["w_in", "w_out", "w_ada"], small)}
    res.update(w_up=r_up, w_down=r_down)

    order = ["w_ada", "b_ada", "pre_mix_g", "post_mix_g", "w_in", "sgu_norm_g", "w_spatial", "b_spatial", "w_pool",
             "pool_scale", "w_out", "pre_ffn_g", "post_ffn_g", "w_up", "conv_w", "conv_b", "w_down"]
    return (loss, grad_x[None], *[res[nm][0] for nm in order], *[res[nm][1] for nm in order],
            *[res[nm][2] for nm in order], *[res[nm][3] for nm in order])
```

```python
import functools
import math

import jax
import jax.numpy as jnp
from jax import lax
from jax.experimental import pallas as pl
from jax.experimental.pallas import tpu as pltpu

F32 = jnp.float32
BF16 = jnp.bfloat16
MESH = pl.DeviceIdType.MESH

EPS = 1e-6
D = 1024
HEAD = 128
N_HEAD = 4
A_WIDTH = 512
B_WIDTH = 512
IN_WIDTH = 1536
WINDOWS = (2, 4, 8, 16)
CHUNK = 64
FF = 2816
N_DEV = 8
FF_CHUNK = 704
N_MOD = 6
MOD_COLS = 768

ADAM_LR = 0.001
ADAM_B1 = 0.9
ADAM_B2 = 0.999
ADAM_EPS = 1e-08
ADAM_WD = 0.01
ADAM_STEP = 10

VMEM_LIMIT_V7X = 60 * 1024 * 1024
HALO = 8
POOL_HALO = 16

VMEM_SPEC = pl.BlockSpec(memory_space=pltpu.VMEM)
ANY_SPEC = pl.BlockSpec(memory_space=pl.ANY)


def _bf(x):
    return x.astype(BF16)


def _mm(a, b):
    return jnp.dot(a, b, preferred_element_type=F32)


def _mm_nt(a, b):
    return lax.dot_general(a, b, (((1,), (1,)), ((), ())), preferred_element_type=F32)


def _mm_tn(a, b):
    return lax.dot_general(a, b, (((0,), (0,)), ((), ())), preferred_element_type=F32)


def _rstd(x):
    return lax.rsqrt(jnp.mean(x * x, axis=-1, keepdims=True) + EPS)


def _sum0(x):
    return jnp.sum(x, axis=0, keepdims=True)


def _rowmean(x):
    return jnp.mean(x, axis=-1, keepdims=True)


_GELU_K = math.sqrt(2.0 / math.pi)


def _gelu_and_grad(x):
    x2 = x * x
    th = jnp.tanh(_GELU_K * (x + 0.044715 * (x * x2)))
    cdf = 0.5 * (1.0 + th)
    grad = cdf + 0.5 * x * (1.0 - th * th) * (_GELU_K * (1.0 + 3.0 * 0.044715 * x2))
    return x * cdf, grad


def _gelu(x):
    return x * (0.5 * (1.0 + jnp.tanh(_GELU_K * (x + 0.044715 * (x * x * x)))))


def _sigmoid(x):
    return 0.5 * jnp.tanh(0.5 * x) + 0.5


def _sgu_mask():
    ri = lax.broadcasted_iota(jnp.int32, (HEAD, HEAD), 0)
    ci = lax.broadcasted_iota(jnp.int32, (HEAD, HEAD), 1)
    return (ci // CHUNK) <= (ri // CHUNK)


def _window_sum(ext, w, trailing):
    n = ext.shape[0]
    s, k = ext, 1
    while k < w:
        s = s + pltpu.roll(s, k if trailing else n - k, 0)
        k *= 2
    return s


def _inv_count(row0, n, w):
    t = row0 + lax.broadcasted_iota(jnp.int32, (n, 1), 0)
    return 1.0 / jnp.minimum(t + 1, w).astype(F32)


def _shift_down(v, before, k):
    rows = lax.broadcasted_iota(jnp.int32, before.shape, 0)
    r = pltpu.roll(v, k, 0)
    top = jnp.where(rows < k, pltpu.roll(before, k, 0), r[0:HALO])
    return jnp.concatenate([top, r[HALO:]], axis=0)


def _shift_up(v, after, k):
    n = v.shape[0]
    rows = lax.broadcasted_iota(jnp.int32, after.shape, 0)
    r = pltpu.roll(v, n - k, 0)
    bottom = jnp.where(rows >= HALO - k, pltpu.roll(after, HALO - k, 0), r[n - HALO:])
    return jnp.concatenate([r[:n - HALO], bottom], axis=0)


def _adamw(w, g, m, v):
    m = ADAM_B1 * m + (1.0 - ADAM_B1) * g
    v = ADAM_B2 * v + (1.0 - ADAM_B2) * (g * g)
    m_hat = m / (1.0 - ADAM_B1 ** ADAM_STEP)
    v_hat = v / (1.0 - ADAM_B2 ** ADAM_STEP)
    delta = -ADAM_LR * (m_hat / (jnp.sqrt(v_hat) + ADAM_EPS) + ADAM_WD * w)
    return delta, m, v


def _coords():
    return lax.axis_index("x"), lax.axis_index("y"), lax.axis_index("c")


def _peer(k):
    x, y, c = _coords()
    return (x ^ ((k >> 2) & 1), y ^ ((k >> 1) & 1), c ^ (k & 1))


def _my_index():
    x, y, c = _coords()
    return 4 * x + 2 * y + c


def _adaln_modulation(c_ref, w_ref, b_ref, mod_ref, scx_ref, scbuf, stage, recv, send_sems, recv_sems):
    me = _my_index()
    cv = c_ref[...]
    scbuf[0] = cv * _sigmoid(cv)
    first = [
        pltpu.make_async_remote_copy(scbuf.at[0], scbuf.at[k], send_sems.at[0, k], recv_sems.at[0, k],
                                     device_id=_peer(k), device_id_type=MESH)
        for k in range(1, N_DEV)
    ]
    for cp in first:
        cp.start()
    for cp in first:
        cp.wait()
    scx_ref[...] = jnp.zeros(scx_ref.shape, F32)
    for k in range(N_DEV):
        scx_ref[k:k + 1, :] = scbuf[k]
    prod = _mm(_bf(scx_ref[...]), _bf(w_ref[...]))
    for k in range(N_DEV):
        stage[k] = prod[k:k + 1, :] + b_ref[me]
    second = [
        pltpu.make_async_remote_copy(stage.at[k], recv.at[k], send_sems.at[1, k], recv_sems.at[1, k],
                                     device_id=_peer(k), device_id_type=MESH)
        for k in range(1, N_DEV)
    ]
    for cp in second:
        cp.start()
    mod_ref[me] = stage[0]
    for cp in second:
        cp.wait()
    for k in range(1, N_DEV):
        mod_ref[me ^ k] = recv[k]


class _GatherSteps:
    def __init__(self, ins, outs, stages, send_sems, recv_sems, local_sems):
        self.ins, self.outs, self.stages = ins, outs, stages
        self.send_sems, self.recv_sems, self.local_sems = send_sems, recv_sems, local_sems
        x, y, c = _coords()
        self.c = c
        self.me, self.sibling = (x, y, c), (x, y, 1 - c)
        self.chips = [(1 - x, y), (x, 1 - y), (1 - x, 1 - y)]

    def _copy(self, a, k, block, to, from_stage=False):
        dst = self.outs[a].at[4 * block[0] + 2 * block[1] + block[2]]
        return pltpu.make_async_remote_copy(self.stages[a] if from_stage else dst, dst, self.send_sems.at[a, k],
                                            self.recv_sems.at[a, k], device_id=to, device_id_type=MESH)

    def _local(self, a):
        me = self.me
        return pltpu.make_async_copy(self.stages[a], self.outs[a].at[4 * me[0] + 2 * me[1] + me[2]],
                                     self.local_sems.at[a])

    def _first(self, a):
        cps = [self._copy(a, 0, self.me, self.sibling, from_stage=True)]
        return cps + [self._copy(a, 1 + j, self.me, (*chip, self.c), from_stage=True)
                      for j, chip in enumerate(self.chips)]

    def _passed(self, a, j):
        return self._copy(a, 4 + j, (*self.chips[j], self.c), self.sibling)

    def start(self):
        for a in range(len(self.ins)):
            self.stages[a][...] = self.ins[a][...].astype(self.stages[a].dtype)
            self._local(a).start()
            for cp in self._first(a):
                cp.start()

    def forward(self):
        for a in range(len(self.ins)):
            for j, chip in enumerate(self.chips):
                self._copy(a, 1 + j, (*chip, self.c), self.me).wait_recv()
                self._passed(a, j).start()

    def finish(self):
        for a in range(len(self.ins)):
            self._copy(a, 0, self.sibling, self.me).wait_recv()
            for j, chip in enumerate(self.chips):
                self._copy(a, 4 + j, (*chip, 1 - self.c), self.me).wait_recv()
            for cp in self._first(a) + [self._passed(a, j) for j in range(3)]:
                cp.wait_send()
            self._local(a).wait()


def _gather_scratch(shards, out_dtypes):
    n = len(shards)
    return ([pltpu.VMEM(s.shape, dt) for s, dt in zip(shards, out_dtypes)]
            + [pltpu.SemaphoreType.DMA((n, 7)), pltpu.SemaphoreType.DMA((n, 7)), pltpu.SemaphoreType.DMA((n,))])


def _gather_out_shapes(shards, out_dtypes):
    return tuple(jax.ShapeDtypeStruct((N_DEV, *s.shape), dt) for s, dt in zip(shards, out_dtypes))


def _prologue(c_row, w_ada, b_ada3, shards, out_dtypes):
    n = len(shards)

    def body(*refs):
        c_ref, w_ref, b_ref = refs[:3]
        mod_ref, scx_ref = refs[3 + n:5 + n]
        gather = _GatherSteps(refs[3:3 + n], refs[5 + n:5 + 2 * n], refs[5 + 2 * n:5 + 3 * n],
                              *refs[5 + 3 * n:8 + 3 * n])
        gather.start()
        _adaln_modulation(c_ref, w_ref, b_ref, mod_ref, scx_ref, *refs[8 + 3 * n:])
        gather.forward()
        gather.finish()

    outs = pl.pallas_call(
        body,
        name="prologue",
        out_shape=(jax.ShapeDtypeStruct((N_DEV, 1, MOD_COLS), F32), jax.ShapeDtypeStruct((2 * N_DEV, D), F32))
        + _gather_out_shapes(shards, out_dtypes),
        in_specs=[VMEM_SPEC] * (3 + n),
        out_specs=(VMEM_SPEC, VMEM_SPEC) + (ANY_SPEC,) * n,
        scratch_shapes=_gather_scratch(shards, out_dtypes) + [
            pltpu.VMEM((N_DEV, 1, D), F32),
            pltpu.VMEM((N_DEV, 1, MOD_COLS), F32),
            pltpu.VMEM((N_DEV, 1, MOD_COLS), F32),
            pltpu.SemaphoreType.DMA((2, N_DEV)),
            pltpu.SemaphoreType.DMA((2, N_DEV)),
        ],
        compiler_params=pltpu.CompilerParams(vmem_limit_bytes=VMEM_LIMIT_V7X),
    )(c_row, w_ada, b_ada3, *shards)
    return outs[0], outs[1], outs[2:]


class _ChipExchangeSteps:
    FLIPS = ((1, 0), (0, 1), (1, 1))

    def __init__(self, srcs, dsts, send_sems, recv_sems):
        self.srcs, self.dsts, self.send_sems, self.recv_sems = srcs, dsts, send_sems, recv_sems

    def _copies(self):
        x, y, c = _coords()
        out = []
        for a in range(len(self.srcs)):
            for j, (fx, fy) in enumerate(self.FLIPS):
                k = 3 * a + j
                out.append(pltpu.make_async_remote_copy(
                    self.srcs[a].at[x ^ fx, y ^ fy], self.dsts[a].at[j], self.send_sems.at[k], self.recv_sems.at[k],
                    device_id=(x ^ fx, y ^ fy, c), device_id_type=MESH))
        return out

    def start(self):
        for cp in self._copies():
            cp.start()

    def finish(self):
        for cp in self._copies():
            cp.wait()


def _mix_fwd(x, mod, g_pre, g_post, w_in_t, sgn, w_sp, b_sp_t, w_pool, p_scale, w_out_b, ts, shards, shard_dtypes):
    t_len = x.shape[0]
    nt, nb = t_len // ts, ts // HEAD
    ns = len(shards)

    def body(*refs):
        (x_ref, mod_ref, g1_ref, g2_ref, win_ref, sgn_ref, ws_ref, bst_ref, wp_ref, ps_ref, wout_ref) = refs[:11]
        x1_ref, proj_ref, mixed_ref = refs[11 + ns:14 + ns]
        pbuf, cat = refs[14 + 2 * ns:16 + 2 * ns]
        gather = _GatherSteps(refs[11:11 + ns], refs[14 + ns:14 + 2 * ns], refs[16 + 2 * ns:16 + 3 * ns],
                              *refs[16 + 3 * ns:])
        i = pl.program_id(0)

        @pl.when(i == 0)
        def _():
            pbuf[0:POOL_HALO, :] = jnp.zeros((POOL_HALO, B_WIDTH), F32)
            gather.start()

        @pl.when(i == (3 * nt) // 4)
        def _():
            gather.forward()

        xv = x_ref[...]
        sh, sc, gm = mod_ref[0:1, :], mod_ref[1:2, :], mod_ref[2:3, :]
        h = (xv * _rstd(xv) * g1_ref[...]) * (1.0 + sc) + sh
        proj_ref[...] = _mm_nt(_bf(h), win_ref[...])
        pbuf[POOL_HALO:POOL_HALO + ts, :] = proj_ref[:, 2 * A_WIDTH:]
        smask = _sgu_mask()
        for hd in range(N_HEAD):
            u = _gelu(proj_ref[:, hd * HEAD:(hd + 1) * HEAD])
            v = _gelu(proj_ref[:, A_WIDTH + hd * HEAD:A_WIDTH + (hd + 1) * HEAD])
            vn = _bf(v * _rstd(v) * sgn_ref[hd:hd + 1, :])
            wm = _bf(jnp.where(smask, ws_ref[hd], 0.0))
            bias = bst_ref[:, hd:hd + 1]
            for b in range(nb):
                rows = slice(b * HEAD, (b + 1) * HEAD)
                z = _mm(wm, vn[rows]) + bias
                cat[rows, hd * HEAD:(hd + 1) * HEAD] = _bf(u[rows] * z)
        for g, w in enumerate(WINDOWS):
            cols = slice(g * HEAD, (g + 1) * HEAD)
            ext = pbuf[:, cols]
            pooled = _window_sum(ext, w, True)[POOL_HALO:] * _inv_count(i * ts, ts, w) - ext[POOL_HALO:]
            cat[:, A_WIDTH + g * HEAD:A_WIDTH + (g + 1) * HEAD] = _bf(_mm(_bf(pooled), _bf(wp_ref[g])) * ps_ref[:, cols])
        pbuf[0:POOL_HALO, :] = pbuf[ts:ts + POOL_HALO, :]
        mixed = _mm(cat[...], wout_ref[...])
        mixed_ref[...] = mixed
        x1_ref[...] = xv + gm * (mixed * _rstd(mixed) * g2_ref[...])

        @pl.when(i == nt - 1)
        def _():
            gather.finish()

    tile = lambda wid: pl.BlockSpec((ts, wid), lambda i: (i, 0))
    outs = pl.pallas_call(
        body,
        name="mix_fwd",
        grid=(nt,),
        out_shape=(jax.ShapeDtypeStruct((t_len, D), F32), jax.ShapeDtypeStruct((t_len, IN_WIDTH), F32),
                   jax.ShapeDtypeStruct((t_len, D), F32)) + _gather_out_shapes(shards, shard_dtypes),
        in_specs=[tile(D)] + [VMEM_SPEC] * (10 + ns),
        out_specs=(tile(D), tile(IN_WIDTH), tile(D)) + (ANY_SPEC,) * ns,
        scratch_shapes=[pltpu.VMEM((POOL_HALO + ts, B_WIDTH), F32), pltpu.VMEM((ts, D), BF16)]
        + _gather_scratch(shards, shard_dtypes),
        compiler_params=pltpu.CompilerParams(dimension_semantics=("arbitrary",), vmem_limit_bytes=VMEM_LIMIT_V7X),
    )(x, mod, g_pre, g_post, w_in_t, sgn, w_sp, b_sp_t, w_pool, p_scale, w_out_b, *shards)
    return outs[:3], outs[3:]


def _ffn_fwd(x1, target, mod, g_pre, g_post, w_up_b, conv_w8, conv_b8, w_down_b, ts):
    t_len = x1.shape[0]
    nt = t_len // ts

    def body(x1_ref, tgt_ref, mod_ref, g3_ref, g4_ref, wup_ref, cw_ref, cb_ref, wdown_ref,
             up_ref, f_ref, dx2_ref, loss_ref, ucarry):
        i = pl.program_id(0)

        @pl.when(i == 0)
        def _():
            ucarry[...] = jnp.zeros(ucarry.shape, F32)
            loss_ref[...] = jnp.zeros(loss_ref.shape, F32)

        x1v = x1_ref[...]
        sh, sc, gf = mod_ref[3:4, :], mod_ref[4:5, :], mod_ref[5:6, :]
        h2 = _bf((x1v * _rstd(x1v) * g3_ref[...]) * (1.0 + sc) + sh)
        half = N_DEV // 2

        def up_pair(j):
            return [_mm_nt(h2, wup_ref[jj]) for jj in (j, j + half)]

        f = jnp.zeros((ts, D), F32)
        ups = up_pair(0)
        for j in range(half):
            nxt = up_pair(j + 1) if j + 1 < half else None
            ys = []
            for up, jj in zip(ups, (j, j + half)):
                up_ref[jj] = up
                before = ucarry[jj]
                ucarry[jj] = up[ts - HALO:, :]
                cw = cw_ref[jj]
                ys.append(cb_ref[jj:jj + 1, :] + _shift_down(up, before, 2) * cw[0:1, :]
                          + _shift_down(up, before, 1) * cw[1:2, :] + up * cw[2:3, :])
            gate, val = ys
            act = gate * _sigmoid(gate) * val
            f = f + _mm(_bf(act), wdown_ref[j * FF_CHUNK:(j + 1) * FF_CHUNK, :])
            ups = nxt
        f_ref[...] = f
        x2 = x1v + gf * (f * _rstd(f) * g4_ref[...])
        err = x2 - tgt_ref[...]
        loss_ref[...] += _sum0(err * err)
        dx2_ref[...] = err * (1.0 / D)

    tile = pl.BlockSpec((ts, D), lambda i: (i, 0))
    return pl.pallas_call(
        body,
        name="ffn_fwd",
        grid=(nt,),
        out_shape=(jax.ShapeDtypeStruct((N_DEV, t_len, FF_CHUNK), F32), jax.ShapeDtypeStruct((t_len, D), F32),
                   jax.ShapeDtypeStruct((t_len, D), F32), jax.ShapeDtypeStruct((1, D), F32)),
        in_specs=[tile, tile] + [VMEM_SPEC] * 7,
        out_specs=(pl.BlockSpec((N_DEV, ts, FF_CHUNK), lambda i: (0, i, 0)), tile, tile,
                   pl.BlockSpec((1, D), lambda i: (0, 0))),
        scratch_shapes=[pltpu.VMEM((N_DEV, HALO, FF_CHUNK), F32)],
        compiler_params=pltpu.CompilerParams(dimension_semantics=("arbitrary",), vmem_limit_bytes=VMEM_LIMIT_V7X),
    )(x1, target, mod, g_pre, g_post, w_up_b, conv_w8, conv_b8, w_down_b)


def _ffn_bwd(dx2, f, x1, up, mod, g_pre, g_post, w_up_b, conv_w8, conv_b8, w_down_b, ts):
    t_len = x1.shape[0]
    nt = t_len // ts
    half = N_DEV // 2

    def body(dx2_ref, f_ref, x1_ref, up_ref, halo_ref, mod_ref, g3_ref, g4_ref, wup_ref, cw_ref, cb_ref, wdown_ref,
             dx1_ref, dup_ref, act_ref, df_ref, h2_ref, dmod_ref, dg3_ref, dg4_ref, dcb_ref, dcw_ref,
             dycarry, dh2acc):
        i = pl.program_id(0)
        r = nt - 1 - i

        @pl.when(i == 0)
        def _():
            for ref in (dmod_ref, dg3_ref, dg4_ref, dcb_ref, dcw_ref, dycarry):
                ref[...] = jnp.zeros(ref.shape, F32)

        dx2v, fv, x1v = dx2_ref[...], f_ref[...], x1_ref[...]
        sh, sc, gf = mod_ref[3:4, :], mod_ref[4:5, :], mod_ref[5:6, :]
        g3, g4 = g3_ref[...], g4_ref[...]
        rstd4 = _rstd(fv)
        fh = fv * rstd4
        dmod_ref[2:3, :] += _sum0(dx2v * (fh * g4))
        dr = dx2v * gf
        dg4_ref[...] += _sum0(dr * fh)
        dfh = dr * g4
        dfb = _bf(rstd4 * (dfh - fh * _rowmean(dfh * fh)))
        df_ref[...] = dfb
        rstd3 = _rstd(x1v)
        xh = x1v * rstd3
        n3 = xh * g3
        h2_ref[...] = _bf(n3 * (1.0 + sc) + sh)
        dh2acc[...] = jnp.zeros((ts, D), F32)
        keep = jnp.where(r > 0, 1.0, 0.0).astype(F32)

        def dact_of(j):
            return _mm_nt(dfb, wdown_ref[j * FF_CHUNK:(j + 1) * FF_CHUNK, :])

        dact_next = dact_of(0)
        for j in range(half):
            dact = dact_next
            if j + 1 < half:
                dact_next = dact_of(j + 1)
            ys = []
            for jj in (j, j + half):
                before = halo_ref[jj] * keep
                upc = up_ref[jj]
                cw = cw_ref[jj]
                ys.append(cb_ref[jj:jj + 1, :] + _shift_down(upc, before, 2) * cw[0:1, :]
                          + _shift_down(upc, before, 1) * cw[1:2, :] + upc * cw[2:3, :])
            gate, val = ys
            sg = _sigmoid(gate)
            silu = gate * sg
            act_ref[j] = _bf(silu * val)
            dys = (dact * val * (sg * (1.0 + gate * (1.0 - sg))), dact * silu)
            for q, jj in enumerate((j, j + half)):
                dy = dys[q]
                cw = cw_ref[jj]
                dcb_ref[jj:jj + 1, :] += _sum0(dy)
                after = dycarry[jj]
                dycarry[jj] = dy[0:HALO, :]
                dy1, dy2 = _shift_up(dy, after, 1), _shift_up(dy, after, 2)
                upc = up_ref[jj]
                dcw_ref[jj, 0:1, :] += _sum0(dy2 * upc)
                dcw_ref[jj, 1:2, :] += _sum0(dy1 * upc)
                dcw_ref[jj, 2:3, :] += _sum0(dy * upc)
                dup = _bf(dy * cw[2:3, :] + dy1 * cw[1:2, :] + dy2 * cw[0:1, :])
                dup_ref[jj] = dup
                dh2acc[...] += _mm(dup, wup_ref[jj])
        dh2 = dh2acc[...]
        dmod_ref[0:1, :] += _sum0(dh2)
        dmod_ref[1:2, :] += _sum0(dh2 * n3)
        dn3 = dh2 * (1.0 + sc)
        dg3_ref[...] += _sum0(dn3 * xh)
        dxh = dn3 * g3
        dx1_ref[...] = dx2v + rstd3 * (dxh - xh * _rowmean(dxh * xh))

    tile = pl.BlockSpec((ts, D), lambda i: (nt - 1 - i, 0))
    chunked = lambda n: pl.BlockSpec((n, ts, FF_CHUNK), lambda i: (0, nt - 1 - i, 0))
    halo = pl.BlockSpec((N_DEV, HALO, FF_CHUNK), lambda i: (0, jnp.maximum((nt - 1 - i) * (ts // HALO) - 1, 0), 0))
    const = lambda *shape: pl.BlockSpec(shape, lambda i: (0,) * len(shape))
    return pl.pallas_call(
        body,
        name="ffn_bwd",
        grid=(nt,),
        out_shape=(jax.ShapeDtypeStruct((t_len, D), F32), jax.ShapeDtypeStruct((N_DEV, t_len, FF_CHUNK), BF16),
                   jax.ShapeDtypeStruct((half, t_len, FF_CHUNK), BF16), jax.ShapeDtypeStruct((t_len, D), BF16),
                   jax.ShapeDtypeStruct((t_len, D), BF16), jax.ShapeDtypeStruct((3, D), F32),
                   jax.ShapeDtypeStruct((1, D), F32), jax.ShapeDtypeStruct((1, D), F32),
                   jax.ShapeDtypeStruct((N_DEV, FF_CHUNK), F32), jax.ShapeDtypeStruct((N_DEV, 3, FF_CHUNK), F32)),
        in_specs=[tile, tile, tile, chunked(N_DEV), halo] + [VMEM_SPEC] * 7,
        out_specs=(tile, chunked(N_DEV), chunked(half), tile, tile, const(3, D), const(1, D), const(1, D),
                   const(N_DEV, FF_CHUNK), const(N_DEV, 3, FF_CHUNK)),
        scratch_shapes=[pltpu.VMEM((N_DEV, HALO, FF_CHUNK), F32), pltpu.VMEM((ts, D), F32)],
        compiler_params=pltpu.CompilerParams(dimension_semantics=("arbitrary",), vmem_limit_bytes=VMEM_LIMIT_V7X),
    )(dx2, f, x1, up, up, mod, g_pre, g_post, w_up_b, conv_w8, conv_b8, w_down_b)


def _wgrad_up(h2, dup, ts):
    t_len = h2.shape[0]
    nt, half = t_len // ts, N_DEV // 2

    def body(h2_ref, dup_ref, out_ref):
        @pl.when(pl.program_id(1) == 0)
        def _():
            out_ref[...] = jnp.zeros(out_ref.shape, F32)

        for q in range(half):
            out_ref[q] += _mm_tn(dup_ref[q], h2_ref[...])

    return pl.pallas_call(
        body,
        name="wgrad_up",
        grid=(2, nt),
        out_shape=jax.ShapeDtypeStruct((N_DEV, FF_CHUNK, D), F32),
        in_specs=[pl.BlockSpec((ts, D), lambda g, t: (t, 0)), pl.BlockSpec((half, ts, FF_CHUNK), lambda g, t: (g, t, 0))],
        out_specs=pl.BlockSpec((half, FF_CHUNK, D), lambda g, t: (g, 0, 0)),
        compiler_params=pltpu.CompilerParams(dimension_semantics=("arbitrary", "arbitrary"),
                                             vmem_limit_bytes=VMEM_LIMIT_V7X),
    )(h2, dup)


def _sibling_swap_copies(srcs, dsts, send_sems, recv_sems):
    x, y, c = _coords()
    return [
        pltpu.make_async_remote_copy(srcs[a].at[xs, ys, 1 - c], dsts[a].at[xs, ys], send_sems.at[a, 2 * xs + ys],
                                     recv_sems.at[a, 2 * xs + ys], device_id=(x, y, 1 - c), device_id_type=MESH)
        for a in range(len(srcs)) for xs in range(2) for ys in range(2)
    ]


def _wgrad_down(act, df, ts, swap_src):
    t_len = df.shape[0]
    nt, half = t_len // ts, N_DEV // 2

    def body(act_ref, df_ref, src_ref, out_ref, dst_ref, send_sems, recv_sems):
        t = pl.program_id(0)

        @pl.when(t == 0)
        def _():
            for cp in _sibling_swap_copies([src_ref], [dst_ref], send_sems, recv_sems):
                cp.start()
            out_ref[...] = jnp.zeros(out_ref.shape, F32)

        for q in range(half):
            out_ref[q] += _mm_tn(act_ref[q], df_ref[...])

        @pl.when(t == nt - 1)
        def _():
            for cp in _sibling_swap_copies([src_ref], [dst_ref], send_sems, recv_sems):
                cp.wait()

    return pl.pallas_call(
        body,
        name="wgrad_down",
        grid=(nt,),
        out_shape=(jax.ShapeDtypeStruct((half, FF_CHUNK, D), F32), jax.ShapeDtypeStruct(swap_src.shape[1:], F32)),
        in_specs=[pl.BlockSpec((half, ts, FF_CHUNK), lambda t: (0, t, 0)), pl.BlockSpec((ts, D), lambda t: (t, 0)),
                  ANY_SPEC],
        out_specs=(pl.BlockSpec((half, FF_CHUNK, D), lambda t: (0, 0, 0)), ANY_SPEC),
        scratch_shapes=[pltpu.SemaphoreType.DMA((1, 4)), pltpu.SemaphoreType.DMA((1, 4))],
        compiler_params=pltpu.CompilerParams(dimension_semantics=("arbitrary",), vmem_limit_bytes=VMEM_LIMIT_V7X),
    )(act, df, swap_src)


def _mix_bwd(dx1, x, proj, mixed, mod, g_pre, g_post, w_in_t, sgn, w_sp, b_sp_t, w_pool, p_scale, w_out_b, ts, rs_srcs):
    t_len = x.shape[0]
    nt, nb = t_len // ts, ts // HEAD
    nr = len(rs_srcs)

    def body(*refs):
        (dx1_ref, x_ref, proj_ref, halo_ref, mixed_ref, mod_ref, g1_ref, g2_ref, win_ref, sgn_ref, ws_ref,
         bst_ref, wp_ref, ps_ref, wout_ref) = refs[:15]
        (gx_ref, dwin_ref, dwout_ref, dmod_ref, dg1_ref, dg2_ref, dsgn_ref, dws_ref, dbst_ref, dwp_ref,
         dps_ref) = refs[15 + nr:26 + nr]
        pbuf, dwsbuf, cat, dproj, dcat = refs[26 + 2 * nr:31 + 2 * nr]
        exchange = _ChipExchangeSteps(refs[15:15 + nr], refs[26 + nr:26 + 2 * nr], *refs[31 + 2 * nr:])
        i = pl.program_id(0)
        r = nt - 1 - i

        @pl.when(i == 0)
        def _():
            exchange.start()
            for ref in (dwin_ref, dwout_ref, dmod_ref, dg1_ref, dg2_ref, dsgn_ref, dws_ref, dbst_ref, dwp_ref, dps_ref):
                ref[...] = jnp.zeros(ref.shape, F32)
            dwsbuf[ts:ts + POOL_HALO, :] = jnp.zeros((POOL_HALO, B_WIDTH), F32)

        xv, dx1v, mixed = x_ref[...], dx1_ref[...], mixed_ref[...]
        sh, sc, gm = mod_ref[0:1, :], mod_ref[1:2, :], mod_ref[2:3, :]
        g1, g2 = g1_ref[...], g2_ref[...]
        rstd2 = _rstd(mixed)
        mh = mixed * rstd2
        dmod_ref[2:3, :] += _sum0(dx1v * (mh * g2))
        dr = dx1v * gm
        dg2_ref[...] += _sum0(dr * mh)
        dmh = dr * g2
        dmb = _bf(rstd2 * (dmh - mh * _rowmean(dmh * mh)))
        dcat[...] = _mm_nt(dmb, wout_ref[...])
        smask = _sgu_mask()
        for hd in range(N_HEAD):
            ucols = slice(hd * HEAD, (hd + 1) * HEAD)
            vcols = slice(A_WIDTH + hd * HEAD, A_WIDTH + (hd + 1) * HEAD)
            u, du_dp = _gelu_and_grad(proj_ref[:, ucols])
            v, dv_dp = _gelu_and_grad(proj_ref[:, vcols])
            rs = _rstd(v)
            vhat = v * rs
            gn = sgn_ref[hd:hd + 1, :]
            vn = _bf(vhat * gn)
            wm = _bf(jnp.where(smask, ws_ref[hd], 0.0))
            bias = bst_ref[:, hd:hd + 1]
            dzsum = jnp.zeros((HEAD, HEAD), F32)
            dwm = jnp.zeros((HEAD, HEAD), F32)
            dvn_parts = []
            for b in range(nb):
                rows = slice(b * HEAD, (b + 1) * HEAD)
                z = _mm(wm, vn[rows]) + bias
                da = dcat[rows, ucols]
                cat[rows, ucols] = _bf(u[rows] * z)
                dz = da * u[rows]
                dzsum = dzsum + dz
                dzb = _bf(dz)
                dwm = dwm + _mm_nt(dzb, vn[rows])
                dvn_parts.append(_mm_tn(wm, dzb))
                dproj[rows, ucols] = _bf((da * z) * du_dp[rows])
            dvn = jnp.concatenate(dvn_parts, axis=0)
            dsgn_ref[hd:hd + 1, :] += _sum0(dvn * vhat)
            dvh = dvn * gn
            dproj[:, vcols] = _bf((rs * (dvh - vhat * _rowmean(dvh * vhat))) * dv_dp)
            dws_ref[hd] += jnp.where(smask, dwm, 0.0)
            dbst_ref[:, hd:hd + 1] += jnp.sum(dzsum, axis=1, keepdims=True)
        keep = jnp.where(r > 0, 1.0, 0.0).astype(F32)
        pbuf[0:POOL_HALO, :] = halo_ref[...] * keep
        pbuf[POOL_HALO:POOL_HALO + ts, :] = proj_ref[:, 2 * A_WIDTH:]
        for g, w in enumerate(WINDOWS):
            cols = slice(g * HEAD, (g + 1) * HEAD)
            ccols = slice(A_WIDTH + g * HEAD, A_WIDTH + (g + 1) * HEAD)
            pcols = slice(2 * A_WIDTH + g * HEAD, 2 * A_WIDTH + (g + 1) * HEAD)
            wpg = _bf(wp_ref[g])
            psg = ps_ref[:, cols]
            ext = pbuf[:, cols]
            inv = _inv_count(r * ts, ts, w)
            pb = _bf(_window_sum(ext, w, True)[POOL_HALO:] * inv - ext[POOL_HALO:])
            yb = _mm(pb, wpg)
            dob = dcat[:, ccols]
            cat[:, ccols] = _bf(yb * psg)
            dps_ref[:, cols] += _sum0(dob * yb)
            dyb = _bf(dob * psg)
            dwp_ref[g] += _mm_tn(pb, dyb)
            dpooled = _mm_nt(dyb, wpg)
            dwsbuf[0:ts, cols] = dpooled * inv
            dproj[:, pcols] = _bf(_window_sum(dwsbuf[:, cols], w, False)[0:ts] - dpooled)
        dwsbuf[ts:ts + POOL_HALO, :] = dwsbuf[0:POOL_HALO, :]
        dpb = dproj[...]
        rstd1 = _rstd(xv)
        xh = xv * rstd1
        n1 = xh * g1
        dwin_ref[...] += _mm_tn(dpb, _bf(n1 * (1.0 + sc) + sh))
        dwout_ref[...] += _mm_tn(cat[...], dmb)
        dh = _mm(dpb, win_ref[...])
        dmod_ref[0:1, :] += _sum0(dh)
        dmod_ref[1:2, :] += _sum0(dh * n1)
        dn1 = dh * (1.0 + sc)
        dg1_ref[...] += _sum0(dn1 * xh)
        dxh = dn1 * g1
        gx_ref[...] = dx1v + rstd1 * (dxh - xh * _rowmean(dxh * xh))

        @pl.when(i == nt - 1)
        def _():
            exchange.finish()

    tile = lambda wid: pl.BlockSpec((ts, wid), lambda i: (nt - 1 - i, 0))
    halo = pl.BlockSpec((POOL_HALO, B_WIDTH),
                        lambda i: (jnp.maximum((nt - 1 - i) * (ts // POOL_HALO) - 1, 0), 2 * A_WIDTH // B_WIDTH))
    const = lambda *shape: pl.BlockSpec(shape, lambda i: (0,) * len(shape))
    resident = lambda *shape: pl.BlockSpec(shape, lambda i: (0,) * len(shape), pipeline_mode=pl.Buffered(1))
    outs = pl.pallas_call(
        body,
        name="mix_bwd",
        grid=(nt,),
        out_shape=(jax.ShapeDtypeStruct((t_len, D), F32), jax.ShapeDtypeStruct((IN_WIDTH, D), F32),
                   jax.ShapeDtypeStruct((D, D), F32), jax.ShapeDtypeStruct((3, D), F32),
                   jax.ShapeDtypeStruct((1, D), F32), jax.ShapeDtypeStruct((1, D), F32),
                   jax.ShapeDtypeStruct((N_HEAD, HEAD), F32), jax.ShapeDtypeStruct((N_HEAD, HEAD, HEAD), F32),
                   jax.ShapeDtypeStruct((HEAD, N_HEAD), F32), jax.ShapeDtypeStruct((N_HEAD, HEAD, HEAD), F32),
                   jax.ShapeDtypeStruct((1, B_WIDTH), F32))
        + tuple(jax.ShapeDtypeStruct((3, *s.shape[2:]), s.dtype) for s in rs_srcs),
        in_specs=[tile(D), tile(D), tile(IN_WIDTH), halo, tile(D)] + [VMEM_SPEC] * 10 + [ANY_SPEC] * nr,
        out_specs=(tile(D), resident(IN_WIDTH, D), resident(D, D), const(3, D), const(1, D), const(1, D),
                   const(N_HEAD, HEAD), const(N_HEAD, HEAD, HEAD), const(HEAD, N_HEAD), const(N_HEAD, HEAD, HEAD),
                   const(1, B_WIDTH)) + (ANY_SPEC,) * nr,
        scratch_shapes=[pltpu.VMEM((POOL_HALO + ts, B_WIDTH), F32), pltpu.VMEM((ts + POOL_HALO, B_WIDTH), F32),
                        pltpu.VMEM((ts, D), BF16), pltpu.VMEM((ts, IN_WIDTH), BF16), pltpu.VMEM((ts, D), F32),
                        pltpu.SemaphoreType.DMA((3 * nr,)), pltpu.SemaphoreType.DMA((3 * nr,))],
        compiler_params=pltpu.CompilerParams(dimension_semantics=("arbitrary",), vmem_limit_bytes=VMEM_LIMIT_V7X),
    )(dx1, x, proj, proj, mixed, mod, g_pre, g_post, w_in_t, sgn, w_sp, b_sp_t, w_pool, p_scale, w_out_b, *rs_srcs)
    return outs[:11], outs[11:]


def _pair_add(name, coords, grid, specs_a, specs_b, out_specs, out_shapes, a_arrays, b_arrays):
    n = len(a_arrays)

    def body(co_ref, *refs):
        for k in range(n):
            total = refs[k][...] + refs[n + k][...]
            refs[2 * n + k][...] = total
            refs[3 * n + k][...] = _bf(total)

    outs = pl.pallas_call(
        body,
        name=name,
        grid_spec=pltpu.PrefetchScalarGridSpec(num_scalar_prefetch=1, grid=grid, in_specs=specs_a + specs_b,
                                               out_specs=out_specs * 2),
        out_shape=tuple(jax.ShapeDtypeStruct(s, dt) for dt in (F32, BF16) for s in out_shapes),
        compiler_params=pltpu.CompilerParams(dimension_semantics=("arbitrary",) * len(grid),
                                             vmem_limit_bytes=VMEM_LIMIT_V7X),
    )(coords, *a_arrays, *b_arrays)
    return list(outs[:n]), list(outs[n:])


def _final_add_adamw(coords, s1, r, ws, ms, vs, n_split=4):
    n = len(s1)

    def body(co_ref, *refs):
        for k in range(n):
            s_ref, r_ref, w_ref, m_ref, v_ref = (refs[q * n + k] for q in range(5))
            g_ref, d_ref, nm_ref, nv_ref = (refs[(5 + q) * n + k] for q in range(4))
            g = ((s_ref[...] + r_ref[0].astype(F32)) + r_ref[1].astype(F32)) + r_ref[2].astype(F32)
            g_ref[...] = g
            delta, m, v = _adamw(w_ref[...], g, m_ref[...], v_ref[...])
            d_ref[...] = delta
            nm_ref[...] = m
            nv_ref[...] = v

    def shard_spec(a):
        rows, cols = a.shape
        return pl.BlockSpec((rows // n_split, cols), lambda i, co: (i, 0))

    def mine_spec(a):
        rows, cols = a.shape[2:]
        return pl.BlockSpec((None, None, rows // n_split, cols), lambda i, co: (co[0], co[1], i, 0))

    def recv_spec(a):
        rows, cols = a.shape[1:]
        return pl.BlockSpec((3, rows // n_split, cols), lambda i, co: (0, i, 0))

    in_specs = ([mine_spec(a) for a in s1] + [recv_spec(a) for a in r] + [shard_spec(a) for a in ws] * 3)
    out_specs = [shard_spec(a) for a in ws] * 4
    outs = pl.pallas_call(
        body,
        name="grad_final_adamw",
        grid_spec=pltpu.PrefetchScalarGridSpec(num_scalar_prefetch=1, grid=(n_split,), in_specs=in_specs,
                                               out_specs=out_specs),
        out_shape=tuple(jax.ShapeDtypeStruct(a.shape, F32) for a in ws) * 4,
        compiler_params=pltpu.CompilerParams(dimension_semantics=("arbitrary",), vmem_limit_bytes=VMEM_LIMIT_V7X),
    )(coords, *s1, *r, *ws, *ms, *vs)
    return [tuple(outs[q * n + k] for q in range(4)) for k in range(n)]


def _sibling_swap(tag, g5):
    n = len(g5)

    def body(*refs):
        copies = _sibling_swap_copies(refs[:n], refs[n:2 * n], *refs[2 * n:])
        for cp in copies:
            cp.start()
        for cp in copies:
            cp.wait()

    return list(pl.pallas_call(
        body,
        name="grad_swap_core_" + tag,
        out_shape=tuple(jax.ShapeDtypeStruct(g.shape[1:], F32) for g in g5),
        in_specs=[ANY_SPEC] * n,
        out_specs=(ANY_SPEC,) * n,
        scratch_shapes=[pltpu.SemaphoreType.DMA((n, 4)), pltpu.SemaphoreType.DMA((n, 4))],
    )(*g5))


def _sibling_add(tag, g5, r1, coords, n_split=4):
    shapes = [g.shape[3:] for g in g5]
    spec_g = [pl.BlockSpec((None, None, None, s[0] // n_split, s[1]), lambda i, j, k, co: (i, j, co[2], k, 0))
              for s in shapes]
    spec_r = [pl.BlockSpec((None, None, s[0] // n_split, s[1]), lambda i, j, k, co: (i, j, k, 0)) for s in shapes]
    return _pair_add("grad_add_core_" + tag, coords, (2, 2, n_split), spec_g, spec_r, spec_r,
                     [(2, 2, *s) for s in shapes], g5, r1)


def _tail_exchange(big, partials, pick_mine, dmod3):
    n, nb = len(partials), len(big)
    big_shapes = [g.shape[1:] for g in big]
    big5 = [g.reshape(2, 2, 2, *s) for g, s in zip(big, big_shapes)]
    flips = _ChipExchangeSteps.FLIPS

    def body(*refs):
        g5, p_in, dm_ref = refs[:nb], refs[nb:nb + n], refs[nb + n]
        outs = refs[nb + n + 1:2 * nb + 2 * n + 2]
        g_out, sums, dm2d = outs[:nb], outs[nb:nb + n], outs[nb + n]
        scratch = refs[2 * nb + 2 * n + 2:]
        s1, stage, chip_recv = scratch[:nb], scratch[nb:2 * nb], scratch[2 * nb:3 * nb]
        acc, rbuf = scratch[3 * nb:3 * nb + n], scratch[3 * nb + n:3 * nb + 2 * n]
        (dm_recv, send_sems, recv_sems, dm_send_sems, dm_recv_sems, sib_send, sib_recv, chip_send,
         chip_recv_sems) = scratch[3 * nb + 2 * n:]
        x, y, c = _coords()
        me = 4 * x + 2 * y + c
        sibling = (x, y, 1 - c)
        dm_copies = [
            pltpu.make_async_remote_copy(dm_ref.at[me ^ k], dm_recv.at[k], dm_send_sems.at[k], dm_recv_sems.at[k],
                                         device_id=_peer(k), device_id_type=MESH)
            for k in range(1, N_DEV)
        ]
        for cp in dm_copies:
            cp.start()
        sib_copies = _sibling_swap_copies(g5, s1, sib_send, sib_recv)
        for cp in sib_copies:
            cp.start()
        for a in range(n):
            acc[a][...] = p_in[a][...]

        def small_phase(ph, peer):
            copies = [
                pltpu.make_async_remote_copy(acc[a], rbuf[a].at[ph], send_sems.at[ph, a], recv_sems.at[ph, a],
                                             device_id=peer, device_id_type=MESH)
                for a in range(n)
            ]
            for cp in copies:
                cp.start()
            for cp in copies:
                cp.wait()
            for a in range(n):
                acc[a][...] = acc[a][...] + rbuf[a][ph]

        small_phase(0, sibling)
        for cp in sib_copies:
            cp.wait()
        for a in range(nb):
            for xs in range(2):
                for ys in range(2):
                    total = g5[a][xs, ys, c] + s1[a][xs, ys]
                    s1[a][xs, ys] = total
                    stage[a][xs, ys] = _bf(total)
        chip_copies = [
            pltpu.make_async_remote_copy(stage[a].at[x ^ fx, y ^ fy], chip_recv[a].at[j], chip_send.at[a, j],
                                         chip_recv_sems.at[a, j], device_id=(x ^ fx, y ^ fy, c), device_id_type=MESH)
            for a in range(nb) for j, (fx, fy) in enumerate(flips)
        ]
        for cp in chip_copies:
            cp.start()
        small_phase(1, (1 - x, y, c))
        small_phase(2, (x, 1 - y, c))
        for a in range(n):
            sums[a][...] = acc[a][me] if pick_mine[a] else acc[a][...]
        dm2d[...] = jnp.zeros(dm2d.shape, F32)
        dm2d[0:1, :] = dm_ref[me]
        for cp in dm_copies:
            cp.wait()
        for k in range(1, N_DEV):
            dm2d[k:k + 1, :] = dm_recv[k]
        for cp in chip_copies:
            cp.wait()
        for a in range(nb):
            g_out[a][...] = ((s1[a][x, y] + chip_recv[a][0].astype(F32)) + chip_recv[a][1].astype(F32)) \
                + chip_recv[a][2].astype(F32)

    out_shapes = tuple(jax.ShapeDtypeStruct(s, F32) for s in big_shapes) + tuple(
        jax.ShapeDtypeStruct(p.shape[1:] if pk else p.shape, F32) for p, pk in zip(partials, pick_mine))
    outs = pl.pallas_call(
        body,
        name="tail_exchange",
        out_shape=out_shapes + (jax.ShapeDtypeStruct((2 * N_DEV, MOD_COLS), F32),),
        in_specs=[VMEM_SPEC] * (nb + n + 1),
        out_specs=(VMEM_SPEC,) * (nb + n + 1),
        scratch_shapes=[pltpu.VMEM((2, 2, *s), F32) for s in big_shapes]
        + [pltpu.VMEM((2, 2, *s), BF16) for s in big_shapes]
        + [pltpu.VMEM((3, *s), BF16) for s in big_shapes]
        + [pltpu.VMEM(p.shape, F32) for p in partials]
        + [pltpu.VMEM((3, *p.shape), F32) for p in partials]
        + [pltpu.VMEM((N_DEV, 1, MOD_COLS), F32), pltpu.SemaphoreType.DMA((3, n)), pltpu.SemaphoreType.DMA((3, n)),
           pltpu.SemaphoreType.DMA((N_DEV,)), pltpu.SemaphoreType.DMA((N_DEV,)),
           pltpu.SemaphoreType.DMA((nb, 4)), pltpu.SemaphoreType.DMA((nb, 4)),
           pltpu.SemaphoreType.DMA((nb, 3)), pltpu.SemaphoreType.DMA((nb, 3))],
        compiler_params=pltpu.CompilerParams(vmem_limit_bytes=VMEM_LIMIT_V7X),
    )(*big5, *partials, dmod3)
    return list(outs[:nb]), list(outs[nb:nb + n]), outs[nb + n]


def _small_update(grads, ws, ms, vs, scx, dm2d, w_ada, m_ada, v_ada, loss_lanes):
    n = len(grads)

    def body(*refs):
        g_in, w_in, m_in, v_in = (refs[q * n:(q + 1) * n] for q in range(4))
        scx_ref, dm_ref, wa_ref, ma_ref, va_ref, ll_ref = refs[4 * n:4 * n + 6]
        outs = refs[4 * n + 6:]
        g_out, d_out, nm_out, nv_out = (outs[q * (n + 1):(q + 1) * (n + 1)] for q in range(4))
        loss_ref = outs[4 * (n + 1)]
        for a in range(n + 1):
            if a < n:
                g, w, m, v = g_in[a][...], w_in[a][...], m_in[a][...], v_in[a][...]
            else:
                g = _mm_tn(_bf(scx_ref[...]), _bf(dm_ref[...]))
                w, m, v = wa_ref[...], ma_ref[...], va_ref[...]
            g_out[a][...] = g
            delta, m, v = _adamw(w, g, m, v)
            d_out[a][...] = delta
            nm_out[a][...] = m
            nv_out[a][...] = v
        loss_ref[...] = jnp.sum(ll_ref[...], axis=1, keepdims=True) * (0.5 / D)

    w_shapes = tuple(jax.ShapeDtypeStruct(w.shape, F32) for w in list(ws) + [w_ada])
    outs = pl.pallas_call(
        body,
        name="small_update",
        out_shape=w_shapes * 4 + (jax.ShapeDtypeStruct((1, 1), F32),),
        in_specs=[VMEM_SPEC] * (4 * n + 6),
        out_specs=(VMEM_SPEC,) * (4 * (n + 1) + 1),
        compiler_params=pltpu.CompilerParams(vmem_limit_bytes=VMEM_LIMIT_V7X),
    )(*grads, *ws, *ms, *vs, scx, dm2d, w_ada, m_ada, v_ada, loss_lanes)
    return [tuple(outs[q * (n + 1) + k] for q in range(4)) for k in range(n + 1)], outs[4 * (n + 1)]


def kernel(x, c, w_ada, b_ada, pre_mix_g, post_mix_g, w_in, sgu_norm_g, w_spatial, b_spatial, w_pool, pool_scale, w_out, pre_ffn_g, post_ffn_g, w_up, conv_w, conv_b, w_down, loss_target, m_w_ada, m_b_ada, m_pre_mix_g, m_post_mix_g, m_w_in, m_sgu_norm_g, m_w_spatial, m_b_spatial, m_w_pool, m_pool_scale, m_w_out, m_pre_ffn_g, m_post_ffn_g, m_w_up, m_conv_w, m_conv_b, m_w_down, v_w_ada, v_b_ada, v_pre_mix_g, v_post_mix_g, v_w_in, v_sgu_norm_g, v_w_spatial, v_b_spatial, v_w_pool, v_pool_scale, v_w_out, v_pre_ffn_g, v_post_ffn_g, v_w_up, v_conv_w, v_conv_b, v_w_down):
    t_len = x.shape[1]
    ts = min(256, t_len)
    ts_mix = min(512, t_len)
    ts_w = min(1024, t_len)
    coords = jnp.stack([lax.axis_index("x"), lax.axis_index("y"), lax.axis_index("c")]).astype(jnp.int32)

    w_in_t, w_up_t = w_in[0].T, w_up[0].T
    mod3, scx, (g_in, g_out) = _prologue(c, w_ada[0], b_ada.reshape(N_DEV, 1, MOD_COLS), [w_in_t, w_out[0]],
                                         [BF16, BF16])
    mod = mod3.reshape(N_MOD, D)
    w_in_tb = g_in.reshape(IN_WIDTH, D)
    w_out_b = g_out.reshape(D, D)
    conv_b8 = conv_b.reshape(N_DEV, FF_CHUNK)
    b_sp_t = b_spatial[0].T

    x2d, tgt = x[0], loss_target[0]
    (x1, proj, mixed), (g_up, g_down, g_cw) = _mix_fwd(
        x2d, mod, pre_mix_g, post_mix_g, w_in_tb, sgu_norm_g[0], w_spatial[0], b_sp_t, w_pool[0], pool_scale, w_out_b,
        ts_mix, [w_up_t, w_down[0], conv_w[0]], [BF16, BF16, F32])
    w_down_b = g_down.reshape(FF, D)
    up, f, dx2, loss_lanes = _ffn_fwd(x1, tgt, mod, pre_ffn_g, post_ffn_g, g_up, g_cw, conv_b8, w_down_b, ts)

    (dx1, dup, act, df, h2, dmod_f, d_pre_ffn, d_post_ffn, d_cb8, d_cw8) = _ffn_bwd(
        dx2, f, x1, up, mod, pre_ffn_g, post_ffn_g, g_up, g_cw, conv_b8, w_down_b, ts)
    gw_up = _wgrad_up(h2, dup, ts_w).reshape(2, 2, 2, FF_CHUNK, D)
    gw_down, r1_up = _wgrad_down(act, df, ts_w, gw_up)
    gw_down = gw_down.reshape(2, 2, 2, FF // N_DEV, D)
    s1_ffn, s1_ffn_b = _sibling_add("ffn", [gw_up, gw_down], [r1_up] + _sibling_swap("down", [gw_down]), coords)
    ((grad_x, gw_in, gw_out, dmod_m, d_pre_mix, d_post_mix, d_sgn, d_wsp, d_bsp_t, d_wpool, d_ps), r_ffn) = _mix_bwd(
        dx1, x2d, proj, mixed, mod, pre_mix_g, post_mix_g, w_in_tb, sgu_norm_g[0], w_spatial[0], b_sp_t,
        w_pool[0], pool_scale, w_out_b, ts, s1_ffn_b)
    gw_in = gw_in.reshape(N_DEV, IN_WIDTH // N_DEV, D)
    gw_out = gw_out.reshape(N_DEV, D // N_DEV, D)

    big = _final_add_adamw(coords, s1_ffn, list(r_ffn), [w_up_t, w_down[0]], [m_w_up[0].T, m_w_down[0]],
                           [v_w_up[0].T, v_w_down[0]])
    r_up, r_down = tuple(a.T[None] for a in big[0]), tuple(a[None] for a in big[1])

    dmod = jnp.concatenate([dmod_m, dmod_f], axis=0)
    names = ["b_ada", "pre_mix_g", "post_mix_g", "sgu_norm_g", "w_spatial", "b_spatial", "w_pool", "pool_scale",
             "pre_ffn_g", "post_ffn_g", "conv_w", "conv_b"]
    partials = [dmod.reshape(1, N_MOD * D), d_pre_mix, d_post_mix, d_sgn, d_wsp, d_bsp_t.T, d_wpool, d_ps,
                d_pre_ffn, d_post_ffn, d_cw8, d_cb8.reshape(1, 2 * FF), loss_lanes]
    small_w = [b_ada, pre_mix_g, post_mix_g, sgu_norm_g[0], w_spatial[0], b_spatial[0], w_pool[0], pool_scale,
               pre_ffn_g, post_ffn_g, conv_w[0], conv_b]
    small_m = [m_b_ada, m_pre_mix_g, m_post_mix_g, m_sgu_norm_g[0], m_w_spatial[0], m_b_spatial[0], m_w_pool[0],
               m_pool_scale, m_pre_ffn_g, m_post_ffn_g, m_conv_w[0], m_conv_b]
    small_v = [v_b_ada, v_pre_mix_g, v_post_mix_g, v_sgu_norm_g[0], v_w_spatial[0], v_b_spatial[0], v_w_pool[0],
               v_pool_scale, v_pre_ffn_g, v_post_ffn_g, v_conv_w[0], v_conv_b]
    g_mix, sums, dm2d = _tail_exchange([gw_in, gw_out], partials, [nm == "conv_w" for nm in names] + [False],
                                       dmod.reshape(N_DEV, 1, MOD_COLS))
    small, loss11 = _small_update(
        sums[:-1] + g_mix, small_w + [w_in_t, w_out[0]], small_m + [m_w_in[0].T, m_w_out[0]],
        small_v + [v_w_in[0].T, v_w_out[0]], scx, dm2d, w_ada[0], m_w_ada[0], v_w_ada[0], sums[-1])
    loss = loss11.reshape(())
    lead = {"sgu_norm_g", "w_spatial", "b_spatial", "w_pool", "conv_w", "w_in", "w_out", "w_ada"}
    res = {nm: tuple((a.T if nm == "w_in" else a)[None] if nm in lead else a for a in four)
           for nm, four in zip(names + ["w_in", "w_out", "w_ada"], small)}
    res.update(w_up=r_up, w_down=r_down)

    order = ["w_ada", "b_ada", "pre_mix_g", "post_mix_g", "w_in", "sgu_norm_g", "w_spatial", "b_spatial", "w_pool",
             "pool_scale", "w_out", "pre_ffn_g", "post_ffn_g", "w_up", "conv_w", "conv_b", "w_down"]
    return (loss, grad_x[None], *[res[nm][0] for nm in order], *[res[nm][1] for nm in order],
            *[res[nm][2] for nm in order], *[res[nm][3] for nm in order])
```

```python
import functools
import math

import jax
import jax.numpy as jnp
from jax import lax
from jax.experimental import pallas as pl
from jax.experimental.pallas import tpu as pltpu

F32 = jnp.float32
BF16 = jnp.bfloat16
MESH = pl.DeviceIdType.MESH

EPS = 1e-6
D = 1024
HEAD = 128
N_HEAD = 4
A_WIDTH = 512
B_WIDTH = 512
IN_WIDTH = 1536
WINDOWS = (2, 4, 8, 16)
CHUNK = 64
FF = 2816
N_DEV = 8
FF_CHUNK = 704
N_MOD = 6
MOD_COLS = 768

ADAM_LR = 0.001
ADAM_B1 = 0.9
ADAM_B2 = 0.999
ADAM_EPS = 1e-08
ADAM_WD = 0.01
ADAM_STEP = 10

VMEM_LIMIT_V7X = 60 * 1024 * 1024
HALO = 8
POOL_HALO = 16

VMEM_SPEC = pl.BlockSpec(memory_space=pltpu.VMEM)
ANY_SPEC = pl.BlockSpec(memory_space=pl.ANY)


def _bf(x):
    return x.astype(BF16)


def _mm(a, b):
    return jnp.dot(a, b, preferred_element_type=F32)


def _mm_nt(a, b):
    return lax.dot_general(a, b, (((1,), (1,)), ((), ())), preferred_element_type=F32)


def _mm_tn(a, b):
    return lax.dot_general(a, b, (((0,), (0,)), ((), ())), preferred_element_type=F32)


def _rstd(x):
    return lax.rsqrt(jnp.mean(x * x, axis=-1, keepdims=True) + EPS)


def _sum0(x):
    return jnp.sum(x, axis=0, keepdims=True)


def _rowmean(x):
    return jnp.mean(x, axis=-1, keepdims=True)


_GELU_K = math.sqrt(2.0 / math.pi)


def _gelu_and_grad(x):
    x2 = x * x
    th = jnp.tanh(_GELU_K * (x + 0.044715 * (x * x2)))
    cdf = 0.5 * (1.0 + th)
    grad = cdf + 0.5 * x * (1.0 - th * th) * (_GELU_K * (1.0 + 3.0 * 0.044715 * x2))
    return x * cdf, grad


def _gelu(x):
    return x * (0.5 * (1.0 + jnp.tanh(_GELU_K * (x + 0.044715 * (x * x * x)))))


def _sigmoid(x):
    return 0.5 * jnp.tanh(0.5 * x) + 0.5


def _sgu_mask():
    ri = lax.broadcasted_iota(jnp.int32, (HEAD, HEAD), 0)
    ci = lax.broadcasted_iota(jnp.int32, (HEAD, HEAD), 1)
    return (ci // CHUNK) <= (ri // CHUNK)


def _window_sum(ext, w, trailing):
    n = ext.shape[0]
    s, k = ext, 1
    while k < w:
        s = s + pltpu.roll(s, k if trailing else n - k, 0)
        k *= 2
    return s


def _inv_count(row0, n, w):
    t = row0 + lax.broadcasted_iota(jnp.int32, (n, 1), 0)
    return 1.0 / jnp.minimum(t + 1, w).astype(F32)


def _shift_down(v, before, k):
    rows = lax.broadcasted_iota(jnp.int32, before.shape, 0)
    r = pltpu.roll(v, k, 0)
    top = jnp.where(rows < k, pltpu.roll(before, k, 0), r[0:HALO])
    return jnp.concatenate([top, r[HALO:]], axis=0)


def _shift_up(v, after, k):
    n = v.shape[0]
    rows = lax.broadcasted_iota(jnp.int32, after.shape, 0)
    r = pltpu.roll(v, n - k, 0)
    bottom = jnp.where(rows >= HALO - k, pltpu.roll(after, HALO - k, 0), r[n - HALO:])
    return jnp.concatenate([r[:n - HALO], bottom], axis=0)


def _adamw(w, g, m, v):
    m = ADAM_B1 * m + (1.0 - ADAM_B1) * g
    v = ADAM_B2 * v + (1.0 - ADAM_B2) * (g * g)
    m_hat = m / (1.0 - ADAM_B1 ** ADAM_STEP)
    v_hat = v / (1.0 - ADAM_B2 ** ADAM_STEP)
    delta = -ADAM_LR * (m_hat / (jnp.sqrt(v_hat) + ADAM_EPS) + ADAM_WD * w)
    return delta, m, v


def _coords():
    return lax.axis_index("x"), lax.axis_index("y"), lax.axis_index("c")


def _peer(k):
    x, y, c = _coords()
    return (x ^ ((k >> 2) & 1), y ^ ((k >> 1) & 1), c ^ (k & 1))


def _my_index():
    x, y, c = _coords()
    return 4 * x + 2 * y + c


def _adaln_modulation(c_ref, w_ref, b_ref, mod_ref, scx_ref, scbuf, stage, recv, send_sems, recv_sems):
    me = _my_index()
    cv = c_ref[...]
    scbuf[0] = cv * _sigmoid(cv)
    first = [
        pltpu.make_async_remote_copy(scbuf.at[0], scbuf.at[k], send_sems.at[0, k], recv_sems.at[0, k],
                                     device_id=_peer(k), device_id_type=MESH)
        for k in range(1, N_DEV)
    ]
    for cp in first:
        cp.start()
    for cp in first:
        cp.wait()
    scx_ref[...] = jnp.zeros(scx_ref.shape, F32)
    for k in range(N_DEV):
        scx_ref[k:k + 1, :] = scbuf[k]
    prod = _mm(_bf(scx_ref[...]), _bf(w_ref[...]))
    for k in range(N_DEV):
        stage[k] = prod[k:k + 1, :] + b_ref[me]
    second = [
        pltpu.make_async_remote_copy(stage.at[k], recv.at[k], send_sems.at[1, k], recv_sems.at[1, k],
                                     device_id=_peer(k), device_id_type=MESH)
        for k in range(1, N_DEV)
    ]
    for cp in second:
        cp.start()
    mod_ref[me] = stage[0]
    for cp in second:
        cp.wait()
    for k in range(1, N_DEV):
        mod_ref[me ^ k] = recv[k]


class _GatherSteps:
    def __init__(self, ins, outs, stages, send_sems, recv_sems, local_sems):
        self.ins, self.outs, self.stages = ins, outs, stages
        self.send_sems, self.recv_sems, self.local_sems = send_sems, recv_sems, local_sems
        x, y, c = _coords()
        self.c = c
        self.me, self.sibling = (x, y, c), (x, y, 1 - c)
        self.chips = [(1 - x, y), (x, 1 - y), (1 - x, 1 - y)]

    def _copy(self, a, k, block, to, from_stage=False):
        dst = self.outs[a].at[4 * block[0] + 2 * block[1] + block[2]]
        return pltpu.make_async_remote_copy(self.stages[a] if from_stage else dst, dst, self.send_sems.at[a, k],
                                            self.recv_sems.at[a, k], device_id=to, device_id_type=MESH)

    def _local(self, a):
        me = self.me
        return pltpu.make_async_copy(self.stages[a], self.outs[a].at[4 * me[0] + 2 * me[1] + me[2]],
                                     self.local_sems.at[a])

    def _first(self, a):
        cps = [self._copy(a, 0, self.me, self.sibling, from_stage=True)]
        return cps + [self._copy(a, 1 + j, self.me, (*chip, self.c), from_stage=True)
                      for j, chip in enumerate(self.chips)]

    def _passed(self, a, j):
        return self._copy(a, 4 + j, (*self.chips[j], self.c), self.sibling)

    def start(self):
        for a in range(len(self.ins)):
            block = self.ins[a][...]
            if block.shape != self.stages[a].shape:
                block = block.T
            self.stages[a][...] = block.astype(self.stages[a].dtype)
            self._local(a).start()
            for cp in self._first(a):
                cp.start()

    def forward(self):
        for a in range(len(self.ins)):
            for j, chip in enumerate(self.chips):
                self._copy(a, 1 + j, (*chip, self.c), self.me).wait_recv()
                self._passed(a, j).start()

    def finish(self):
        for a in range(len(self.ins)):
            self._copy(a, 0, self.sibling, self.me).wait_recv()
            for j, chip in enumerate(self.chips):
                self._copy(a, 4 + j, (*chip, 1 - self.c), self.me).wait_recv()
            for cp in self._first(a) + [self._passed(a, j) for j in range(3)]:
                cp.wait_send()
            self._local(a).wait()


def _gather_scratch(shapes, out_dtypes):
    n = len(shapes)
    return ([pltpu.VMEM(s, dt) for s, dt in zip(shapes, out_dtypes)]
            + [pltpu.SemaphoreType.DMA((n, 7)), pltpu.SemaphoreType.DMA((n, 7)), pltpu.SemaphoreType.DMA((n,))])


def _gather_out_shapes(shapes, out_dtypes):
    return tuple(jax.ShapeDtypeStruct((N_DEV, *s), dt) for s, dt in zip(shapes, out_dtypes))


def _prologue(c_row, w_ada, b_ada3, shards, out_dtypes):
    n = len(shards)

    def body(*refs):
        c_ref, w_ref, b_ref = refs[:3]
        mod_ref, scx_ref = refs[3 + n:5 + n]
        gather = _GatherSteps(refs[3:3 + n], refs[5 + n:5 + 2 * n], refs[5 + 2 * n:5 + 3 * n],
                              *refs[5 + 3 * n:8 + 3 * n])
        gather.start()
        _adaln_modulation(c_ref, w_ref, b_ref, mod_ref, scx_ref, *refs[8 + 3 * n:])
        gather.forward()
        gather.finish()

    outs = pl.pallas_call(
        body,
        name="prologue",
        out_shape=(jax.ShapeDtypeStruct((N_DEV, 1, MOD_COLS), F32), jax.ShapeDtypeStruct((2 * N_DEV, D), F32))
        + _gather_out_shapes([s.shape for s in shards], out_dtypes),
        in_specs=[VMEM_SPEC] * (3 + n),
        out_specs=(VMEM_SPEC, VMEM_SPEC) + (ANY_SPEC,) * n,
        scratch_shapes=_gather_scratch([s.shape for s in shards], out_dtypes) + [
            pltpu.VMEM((N_DEV, 1, D), F32),
            pltpu.VMEM((N_DEV, 1, MOD_COLS), F32),
            pltpu.VMEM((N_DEV, 1, MOD_COLS), F32),
            pltpu.SemaphoreType.DMA((2, N_DEV)),
            pltpu.SemaphoreType.DMA((2, N_DEV)),
        ],
        compiler_params=pltpu.CompilerParams(vmem_limit_bytes=VMEM_LIMIT_V7X),
    )(c_row, w_ada, b_ada3, *shards)
    return outs[0], outs[1], outs[2:]


class _ChipExchangeSteps:
    FLIPS = ((1, 0), (0, 1), (1, 1))

    def __init__(self, srcs, dsts, send_sems, recv_sems):
        self.srcs, self.dsts, self.send_sems, self.recv_sems = srcs, dsts, send_sems, recv_sems

    def _copies(self):
        x, y, c = _coords()
        out = []
        for a in range(len(self.srcs)):
            for j, (fx, fy) in enumerate(self.FLIPS):
                k = 3 * a + j
                out.append(pltpu.make_async_remote_copy(
                    self.srcs[a].at[x ^ fx, y ^ fy], self.dsts[a].at[j], self.send_sems.at[k], self.recv_sems.at[k],
                    device_id=(x ^ fx, y ^ fy, c), device_id_type=MESH))
        return out

    def start(self):
        for cp in self._copies():
            cp.start()

    def finish(self):
        for cp in self._copies():
            cp.wait()


def _mix_fwd(x, mod, g_pre, g_post, w_in_t, sgn, w_sp, b_sp_t, w_pool, p_scale, w_out_b, ts, shards, shard_shapes,
             shard_dtypes):
    t_len = x.shape[0]
    nt, nb = t_len // ts, ts // HEAD
    ns = len(shards)

    def body(*refs):
        (x_ref, mod_ref, g1_ref, g2_ref, win_ref, sgn_ref, ws_ref, bst_ref, wp_ref, ps_ref, wout_ref) = refs[:11]
        x1_ref, proj_ref, mixed_ref = refs[11 + ns:14 + ns]
        pbuf, cat = refs[14 + 2 * ns:16 + 2 * ns]
        gather = _GatherSteps(refs[11:11 + ns], refs[14 + ns:14 + 2 * ns], refs[16 + 2 * ns:16 + 3 * ns],
                              *refs[16 + 3 * ns:])
        i = pl.program_id(0)

        @pl.when(i == 0)
        def _():
            pbuf[0:POOL_HALO, :] = jnp.zeros((POOL_HALO, B_WIDTH), F32)
            gather.start()

        @pl.when(i == (3 * nt) // 4)
        def _():
            gather.forward()

        xv = x_ref[...]
        sh, sc, gm = mod_ref[0:1, :], mod_ref[1:2, :], mod_ref[2:3, :]
        h = (xv * _rstd(xv) * g1_ref[...]) * (1.0 + sc) + sh
        proj_ref[...] = _mm_nt(_bf(h), win_ref[...])
        pbuf[POOL_HALO:POOL_HALO + ts, :] = proj_ref[:, 2 * A_WIDTH:]
        smask = _sgu_mask()
        for hd in range(N_HEAD):
            u = _gelu(proj_ref[:, hd * HEAD:(hd + 1) * HEAD])
            v = _gelu(proj_ref[:, A_WIDTH + hd * HEAD:A_WIDTH + (hd + 1) * HEAD])
            vn = _bf(v * _rstd(v) * sgn_ref[hd:hd + 1, :])
            wm = _bf(jnp.where(smask, ws_ref[hd], 0.0))
            bias = bst_ref[:, hd:hd + 1]
            for b in range(nb):
                rows = slice(b * HEAD, (b + 1) * HEAD)
                z = _mm(wm, vn[rows]) + bias
                cat[rows, hd * HEAD:(hd + 1) * HEAD] = _bf(u[rows] * z)
        for g, w in enumerate(WINDOWS):
            cols = slice(g * HEAD, (g + 1) * HEAD)
            ext = pbuf[:, cols]
            pooled = _window_sum(ext, w, True)[POOL_HALO:] * _inv_count(i * ts, ts, w) - ext[POOL_HALO:]
            cat[:, A_WIDTH + g * HEAD:A_WIDTH + (g + 1) * HEAD] = _bf(_mm(_bf(pooled), _bf(wp_ref[g])) * ps_ref[:, cols])
        pbuf[0:POOL_HALO, :] = pbuf[ts:ts + POOL_HALO, :]
        mixed = _mm(cat[...], wout_ref[...])
        mixed_ref[...] = mixed
        x1_ref[...] = xv + gm * (mixed * _rstd(mixed) * g2_ref[...])

        @pl.when(i == nt - 1)
        def _():
            gather.finish()

    tile = lambda wid: pl.BlockSpec((ts, wid), lambda i: (i, 0))
    outs = pl.pallas_call(
        body,
        name="mix_fwd",
        grid=(nt,),
        out_shape=(jax.ShapeDtypeStruct((t_len, D), F32), jax.ShapeDtypeStruct((t_len, IN_WIDTH), F32),
                   jax.ShapeDtypeStruct((t_len, D), F32)) + _gather_out_shapes(shard_shapes, shard_dtypes),
        in_specs=[tile(D)] + [VMEM_SPEC] * (10 + ns),
        out_specs=(tile(D), tile(IN_WIDTH), tile(D)) + (ANY_SPEC,) * ns,
        scratch_shapes=[pltpu.VMEM((POOL_HALO + ts, B_WIDTH), F32), pltpu.VMEM((ts, D), BF16)]
        + _gather_scratch(shard_shapes, shard_dtypes),
        compiler_params=pltpu.CompilerParams(dimension_semantics=("arbitrary",), vmem_limit_bytes=VMEM_LIMIT_V7X),
    )(x, mod, g_pre, g_post, w_in_t, sgn, w_sp, b_sp_t, w_pool, p_scale, w_out_b, *shards)
    return outs[:3], outs[3:]


def _ffn_fwd(x1, target, mod, g_pre, g_post, w_up_b, conv_w8, conv_b8, w_down_b, ts):
    t_len = x1.shape[0]
    nt = t_len // ts

    def body(x1_ref, tgt_ref, mod_ref, g3_ref, g4_ref, wup_ref, cw_ref, cb_ref, wdown_ref,
             up_ref, f_ref, dx2_ref, loss_ref, ucarry):
        i = pl.program_id(0)

        @pl.when(i == 0)
        def _():
            ucarry[...] = jnp.zeros(ucarry.shape, F32)
            loss_ref[...] = jnp.zeros(loss_ref.shape, F32)

        x1v = x1_ref[...]
        sh, sc, gf = mod_ref[3:4, :], mod_ref[4:5, :], mod_ref[5:6, :]
        h2 = _bf((x1v * _rstd(x1v) * g3_ref[...]) * (1.0 + sc) + sh)
        half = N_DEV // 2

        def up_pair(j):
            return [_mm(h2, wup_ref[jj]) for jj in (j, j + half)]

        f = jnp.zeros((ts, D), F32)
        ups = up_pair(0)
        for j in range(half):
            nxt = up_pair(j + 1) if j + 1 < half else None
            ys = []
            for up, jj in zip(ups, (j, j + half)):
                up_ref[jj] = up
                before = ucarry[jj]
                ucarry[jj] = up[ts - HALO:, :]
                cw = cw_ref[jj]
                ys.append(cb_ref[jj:jj + 1, :] + _shift_down(up, before, 2) * cw[0:1, :]
                          + _shift_down(up, before, 1) * cw[1:2, :] + up * cw[2:3, :])
            gate, val = ys
            act = gate * _sigmoid(gate) * val
            f = f + _mm(_bf(act), wdown_ref[j * FF_CHUNK:(j + 1) * FF_CHUNK, :])
            ups = nxt
        f_ref[...] = f
        x2 = x1v + gf * (f * _rstd(f) * g4_ref[...])
        err = x2 - tgt_ref[...]
        loss_ref[...] += _sum0(err * err)
        dx2_ref[...] = err * (1.0 / D)

    tile = pl.BlockSpec((ts, D), lambda i: (i, 0))
    return pl.pallas_call(
        body,
        name="ffn_fwd",
        grid=(nt,),
        out_shape=(jax.ShapeDtypeStruct((N_DEV, t_len, FF_CHUNK), F32), jax.ShapeDtypeStruct((t_len, D), F32),
                   jax.ShapeDtypeStruct((t_len, D), F32), jax.ShapeDtypeStruct((1, D), F32)),
        in_specs=[tile, tile] + [VMEM_SPEC] * 7,
        out_specs=(pl.BlockSpec((N_DEV, ts, FF_CHUNK), lambda i: (0, i, 0)), tile, tile,
                   pl.BlockSpec((1, D), lambda i: (0, 0))),
        scratch_shapes=[pltpu.VMEM((N_DEV, HALO, FF_CHUNK), F32)],
        compiler_params=pltpu.CompilerParams(dimension_semantics=("arbitrary",), vmem_limit_bytes=VMEM_LIMIT_V7X),
    )(x1, target, mod, g_pre, g_post, w_up_b, conv_w8, conv_b8, w_down_b)


def _ffn_bwd(dx2, f, x1, up, mod, g_pre, g_post, w_up_b, conv_w8, conv_b8, w_down_b, ts):
    t_len = x1.shape[0]
    nt = t_len // ts
    half = N_DEV // 2

    def body(dx2_ref, f_ref, x1_ref, up_ref, halo_ref, mod_ref, g3_ref, g4_ref, wup_ref, cw_ref, cb_ref, wdown_ref,
             dx1_ref, dup_ref, act_ref, df_ref, h2_ref, dmod_ref, dg3_ref, dg4_ref, dcb_ref, dcw_ref,
             dycarry, dh2acc):
        i = pl.program_id(0)
        r = nt - 1 - i

        @pl.when(i == 0)
        def _():
            for ref in (dmod_ref, dg3_ref, dg4_ref, dcb_ref, dcw_ref, dycarry):
                ref[...] = jnp.zeros(ref.shape, F32)

        dx2v, fv, x1v = dx2_ref[...], f_ref[...], x1_ref[...]
        sh, sc, gf = mod_ref[3:4, :], mod_ref[4:5, :], mod_ref[5:6, :]
        g3, g4 = g3_ref[...], g4_ref[...]
        rstd4 = _rstd(fv)
        fh = fv * rstd4
        dmod_ref[2:3, :] += _sum0(dx2v * (fh * g4))
        dr = dx2v * gf
        dg4_ref[...] += _sum0(dr * fh)
        dfh = dr * g4
        dfb = _bf(rstd4 * (dfh - fh * _rowmean(dfh * fh)))
        df_ref[...] = dfb
        rstd3 = _rstd(x1v)
        xh = x1v * rstd3
        n3 = xh * g3
        h2_ref[...] = _bf(n3 * (1.0 + sc) + sh)
        dh2acc[...] = jnp.zeros((ts, D), F32)
        keep = jnp.where(r > 0, 1.0, 0.0).astype(F32)

        def dact_of(j):
            return _mm_nt(dfb, wdown_ref[j * FF_CHUNK:(j + 1) * FF_CHUNK, :])

        dact_next = dact_of(0)
        for j in range(half):
            dact = dact_next
            if j + 1 < half:
                dact_next = dact_of(j + 1)
            ys = []
            for jj in (j, j + half):
                before = halo_ref[jj] * keep
                upc = up_ref[jj]
                cw = cw_ref[jj]
                ys.append(cb_ref[jj:jj + 1, :] + _shift_down(upc, before, 2) * cw[0:1, :]
                          + _shift_down(upc, before, 1) * cw[1:2, :] + upc * cw[2:3, :])
            gate, val = ys
            sg = _sigmoid(gate)
            silu = gate * sg
            act_ref[j] = _bf(silu * val)
            dys = (dact * val * (sg * (1.0 + gate * (1.0 - sg))), dact * silu)
            for q, jj in enumerate((j, j + half)):
                dy = dys[q]
                cw = cw_ref[jj]
                dcb_ref[jj:jj + 1, :] += _sum0(dy)
                after = dycarry[jj]
                dycarry[jj] = dy[0:HALO, :]
                dy1, dy2 = _shift_up(dy, after, 1), _shift_up(dy, after, 2)
                upc = up_ref[jj]
                dcw_ref[jj, 0:1, :] += _sum0(dy2 * upc)
                dcw_ref[jj, 1:2, :] += _sum0(dy1 * upc)
                dcw_ref[jj, 2:3, :] += _sum0(dy * upc)
                dup = _bf(dy * cw[2:3, :] + dy1 * cw[1:2, :] + dy2 * cw[0:1, :])
                dup_ref[jj] = dup
                dh2acc[...] += _mm_nt(dup, wup_ref[jj])
        dh2 = dh2acc[...]
        dmod_ref[0:1, :] += _sum0(dh2)
        dmod_ref[1:2, :] += _sum0(dh2 * n3)
        dn3 = dh2 * (1.0 + sc)
        dg3_ref[...] += _sum0(dn3 * xh)
        dxh = dn3 * g3
        dx1_ref[...] = dx2v + rstd3 * (dxh - xh * _rowmean(dxh * xh))

    tile = pl.BlockSpec((ts, D), lambda i: (nt - 1 - i, 0))
    chunked = lambda n: pl.BlockSpec((n, ts, FF_CHUNK), lambda i: (0, nt - 1 - i, 0))
    halo = pl.BlockSpec((N_DEV, HALO, FF_CHUNK), lambda i: (0, jnp.maximum((nt - 1 - i) * (ts // HALO) - 1, 0), 0))
    const = lambda *shape: pl.BlockSpec(shape, lambda i: (0,) * len(shape))
    return pl.pallas_call(
        body,
        name="ffn_bwd",
        grid=(nt,),
        out_shape=(jax.ShapeDtypeStruct((t_len, D), F32), jax.ShapeDtypeStruct((N_DEV, t_len, FF_CHUNK), BF16),
                   jax.ShapeDtypeStruct((half, t_len, FF_CHUNK), BF16), jax.ShapeDtypeStruct((t_len, D), BF16),
                   jax.ShapeDtypeStruct((t_len, D), BF16), jax.ShapeDtypeStruct((3, D), F32),
                   jax.ShapeDtypeStruct((1, D), F32), jax.ShapeDtypeStruct((1, D), F32),
                   jax.ShapeDtypeStruct((N_DEV, FF_CHUNK), F32), jax.ShapeDtypeStruct((N_DEV, 3, FF_CHUNK), F32)),
        in_specs=[tile, tile, tile, chunked(N_DEV), halo] + [VMEM_SPEC] * 7,
        out_specs=(tile, chunked(N_DEV), chunked(half), tile, tile, const(3, D), const(1, D), const(1, D),
                   const(N_DEV, FF_CHUNK), const(N_DEV, 3, FF_CHUNK)),
        scratch_shapes=[pltpu.VMEM((N_DEV, HALO, FF_CHUNK), F32), pltpu.VMEM((ts, D), F32)],
        compiler_params=pltpu.CompilerParams(dimension_semantics=("arbitrary",), vmem_limit_bytes=VMEM_LIMIT_V7X),
    )(dx2, f, x1, up, up, mod, g_pre, g_post, w_up_b, conv_w8, conv_b8, w_down_b)


def _wgrad_up(h2, dup, ts):
    t_len = h2.shape[0]
    nt, half = t_len // ts, N_DEV // 2

    def body(h2_ref, dup_ref, out_ref):
        @pl.when(pl.program_id(1) == 0)
        def _():
            out_ref[...] = jnp.zeros(out_ref.shape, F32)

        for q in range(half):
            out_ref[q] += _mm_tn(dup_ref[q], h2_ref[...])

    return pl.pallas_call(
        body,
        name="wgrad_up",
        grid=(2, nt),
        out_shape=jax.ShapeDtypeStruct((N_DEV, FF_CHUNK, D), F32),
        in_specs=[pl.BlockSpec((ts, D), lambda g, t: (t, 0)), pl.BlockSpec((half, ts, FF_CHUNK), lambda g, t: (g, t, 0))],
        out_specs=pl.BlockSpec((half, FF_CHUNK, D), lambda g, t: (g, 0, 0)),
        compiler_params=pltpu.CompilerParams(dimension_semantics=("arbitrary", "arbitrary"),
                                             vmem_limit_bytes=VMEM_LIMIT_V7X),
    )(h2, dup)


def _sibling_swap_copies(srcs, dsts, send_sems, recv_sems):
    x, y, c = _coords()
    return [
        pltpu.make_async_remote_copy(srcs[a].at[xs, ys, 1 - c], dsts[a].at[xs, ys], send_sems.at[a, 2 * xs + ys],
                                     recv_sems.at[a, 2 * xs + ys], device_id=(x, y, 1 - c), device_id_type=MESH)
        for a in range(len(srcs)) for xs in range(2) for ys in range(2)
    ]


def _wgrad_down(act, df, ts, swap_src):
    t_len = df.shape[0]
    nt, half = t_len // ts, N_DEV // 2

    def body(act_ref, df_ref, src_ref, out_ref, dst_ref, send_sems, recv_sems):
        t = pl.program_id(0)

        @pl.when(t == 0)
        def _():
            for cp in _sibling_swap_copies([src_ref], [dst_ref], send_sems, recv_sems):
                cp.start()
            out_ref[...] = jnp.zeros(out_ref.shape, F32)

        for q in range(half):
            out_ref[q] += _mm_tn(act_ref[q], df_ref[...])

        @pl.when(t == nt - 1)
        def _():
            for cp in _sibling_swap_copies([src_ref], [dst_ref], send_sems, recv_sems):
                cp.wait()

    return pl.pallas_call(
        body,
        name="wgrad_down",
        grid=(nt,),
        out_shape=(jax.ShapeDtypeStruct((half, FF_CHUNK, D), F32), jax.ShapeDtypeStruct(swap_src.shape[1:], F32)),
        in_specs=[pl.BlockSpec((half, ts, FF_CHUNK), lambda t: (0, t, 0)), pl.BlockSpec((ts, D), lambda t: (t, 0)),
                  ANY_SPEC],
        out_specs=(pl.BlockSpec((half, FF_CHUNK, D), lambda t: (0, 0, 0)), ANY_SPEC),
        scratch_shapes=[pltpu.SemaphoreType.DMA((1, 4)), pltpu.SemaphoreType.DMA((1, 4))],
        compiler_params=pltpu.CompilerParams(dimension_semantics=("arbitrary",), vmem_limit_bytes=VMEM_LIMIT_V7X),
    )(act, df, swap_src)


def _mix_bwd(dx1, x, proj, mixed, mod, g_pre, g_post, w_in_t, sgn, w_sp, b_sp_t, w_pool, p_scale, w_out_b, ts, rs_srcs):
    t_len = x.shape[0]
    nt, nb = t_len // ts, ts // HEAD
    nr = len(rs_srcs)

    def body(*refs):
        (dx1_ref, x_ref, proj_ref, halo_ref, mixed_ref, mod_ref, g1_ref, g2_ref, win_ref, sgn_ref, ws_ref,
         bst_ref, wp_ref, ps_ref, wout_ref) = refs[:15]
        (gx_ref, dwin_ref, dwout_ref, dmod_ref, dg1_ref, dg2_ref, dsgn_ref, dws_ref, dbst_ref, dwp_ref,
         dps_ref) = refs[15 + nr:26 + nr]
        pbuf, dwsbuf, cat, dproj, dcat = refs[26 + 2 * nr:31 + 2 * nr]
        exchange = _ChipExchangeSteps(refs[15:15 + nr], refs[26 + nr:26 + 2 * nr], *refs[31 + 2 * nr:])
        i = pl.program_id(0)
        r = nt - 1 - i

        @pl.when(i == 0)
        def _():
            exchange.start()
            for ref in (dwin_ref, dwout_ref, dmod_ref, dg1_ref, dg2_ref, dsgn_ref, dws_ref, dbst_ref, dwp_ref, dps_ref):
                ref[...] = jnp.zeros(ref.shape, F32)
            dwsbuf[ts:ts + POOL_HALO, :] = jnp.zeros((POOL_HALO, B_WIDTH), F32)

        xv, dx1v, mixed = x_ref[...], dx1_ref[...], mixed_ref[...]
        sh, sc, gm = mod_ref[0:1, :], mod_ref[1:2, :], mod_ref[2:3, :]
        g1, g2 = g1_ref[...], g2_ref[...]
        rstd2 = _rstd(mixed)
        mh = mixed * rstd2
        dmod_ref[2:3, :] += _sum0(dx1v * (mh * g2))
        dr = dx1v * gm
        dg2_ref[...] += _sum0(dr * mh)
        dmh = dr * g2
        dmb = _bf(rstd2 * (dmh - mh * _rowmean(dmh * mh)))
        dcat[...] = _mm_nt(dmb, wout_ref[...])
        smask = _sgu_mask()
        for hd in range(N_HEAD):
            ucols = slice(hd * HEAD, (hd + 1) * HEAD)
            vcols = slice(A_WIDTH + hd * HEAD, A_WIDTH + (hd + 1) * HEAD)
            u, du_dp = _gelu_and_grad(proj_ref[:, ucols])
            v, dv_dp = _gelu_and_grad(proj_ref[:, vcols])
            rs = _rstd(v)
            vhat = v * rs
            gn = sgn_ref[hd:hd + 1, :]
            vn = _bf(vhat * gn)
            wm = _bf(jnp.where(smask, ws_ref[hd], 0.0))
            bias = bst_ref[:, hd:hd + 1]
            dzsum = jnp.zeros((HEAD, HEAD), F32)
            dwm = jnp.zeros((HEAD, HEAD), F32)
            dvn_parts = []
            for b in range(nb):
                rows = slice(b * HEAD, (b + 1) * HEAD)
                z = _mm(wm, vn[rows]) + bias
                da = dcat[rows, ucols]
                cat[rows, ucols] = _bf(u[rows] * z)
                dz = da * u[rows]
                dzsum = dzsum + dz
                dzb = _bf(dz)
                dwm = dwm + _mm_nt(dzb, vn[rows])
                dvn_parts.append(_mm_tn(wm, dzb))
                dproj[rows, ucols] = _bf((da * z) * du_dp[rows])
            dvn = jnp.concatenate(dvn_parts, axis=0)
            dsgn_ref[hd:hd + 1, :] += _sum0(dvn * vhat)
            dvh = dvn * gn
            dproj[:, vcols] = _bf((rs * (dvh - vhat * _rowmean(dvh * vhat))) * dv_dp)
            dws_ref[hd] += jnp.where(smask, dwm, 0.0)
            dbst_ref[:, hd:hd + 1] += jnp.sum(dzsum, axis=1, keepdims=True)
        keep = jnp.where(r > 0, 1.0, 0.0).astype(F32)
        pbuf[0:POOL_HALO, :] = halo_ref[...] * keep
        pbuf[POOL_HALO:POOL_HALO + ts, :] = proj_ref[:, 2 * A_WIDTH:]
        for g, w in enumerate(WINDOWS):
            cols = slice(g * HEAD, (g + 1) * HEAD)
            ccols = slice(A_WIDTH + g * HEAD, A_WIDTH + (g + 1) * HEAD)
            pcols = slice(2 * A_WIDTH + g * HEAD, 2 * A_WIDTH + (g + 1) * HEAD)
            wpg = _bf(wp_ref[g])
            psg = ps_ref[:, cols]
            ext = pbuf[:, cols]
            inv = _inv_count(r * ts, ts, w)
            pb = _bf(_window_sum(ext, w, True)[POOL_HALO:] * inv - ext[POOL_HALO:])
            yb = _mm(pb, wpg)
            dob = dcat[:, ccols]
            cat[:, ccols] = _bf(yb * psg)
            dps_ref[:, cols] += _sum0(dob * yb)
            dyb = _bf(dob * psg)
            dwp_ref[g] += _mm_tn(pb, dyb)
            dpooled = _mm_nt(dyb, wpg)
            dwsbuf[0:ts, cols] = dpooled * inv
            dproj[:, pcols] = _bf(_window_sum(dwsbuf[:, cols], w, False)[0:ts] - dpooled)
        dwsbuf[ts:ts + POOL_HALO, :] = dwsbuf[0:POOL_HALO, :]
        dpb = dproj[...]
        rstd1 = _rstd(xv)
        xh = xv * rstd1
        n1 = xh * g1
        dwin_ref[...] += _mm_tn(dpb, _bf(n1 * (1.0 + sc) + sh))
        dwout_ref[...] += _mm_tn(cat[...], dmb)
        dh = _mm(dpb, win_ref[...])
        dmod_ref[0:1, :] += _sum0(dh)
        dmod_ref[1:2, :] += _sum0(dh * n1)
        dn1 = dh * (1.0 + sc)
        dg1_ref[...] += _sum0(dn1 * xh)
        dxh = dn1 * g1
        gx_ref[...] = dx1v + rstd1 * (dxh - xh * _rowmean(dxh * xh))

        @pl.when(i == nt - 1)
        def _():
            exchange.finish()

    tile = lambda wid: pl.BlockSpec((ts, wid), lambda i: (nt - 1 - i, 0))
    halo = pl.BlockSpec((POOL_HALO, B_WIDTH),
                        lambda i: (jnp.maximum((nt - 1 - i) * (ts // POOL_HALO) - 1, 0), 2 * A_WIDTH // B_WIDTH))
    const = lambda *shape: pl.BlockSpec(shape, lambda i: (0,) * len(shape))
    resident = lambda *shape: pl.BlockSpec(shape, lambda i: (0,) * len(shape), pipeline_mode=pl.Buffered(1))
    outs = pl.pallas_call(
        body,
        name="mix_bwd",
        grid=(nt,),
        out_shape=(jax.ShapeDtypeStruct((t_len, D), F32), jax.ShapeDtypeStruct((IN_WIDTH, D), F32),
                   jax.ShapeDtypeStruct((D, D), F32), jax.ShapeDtypeStruct((3, D), F32),
                   jax.ShapeDtypeStruct((1, D), F32), jax.ShapeDtypeStruct((1, D), F32),
                   jax.ShapeDtypeStruct((N_HEAD, HEAD), F32), jax.ShapeDtypeStruct((N_HEAD, HEAD, HEAD), F32),
                   jax.ShapeDtypeStruct((HEAD, N_HEAD), F32), jax.ShapeDtypeStruct((N_HEAD, HEAD, HEAD), F32),
                   jax.ShapeDtypeStruct((1, B_WIDTH), F32))
        + tuple(jax.ShapeDtypeStruct((3, *s.shape[2:]), s.dtype) for s in rs_srcs),
        in_specs=[tile(D), tile(D), tile(IN_WIDTH), halo, tile(D)] + [VMEM_SPEC] * 10 + [ANY_SPEC] * nr,
        out_specs=(tile(D), resident(IN_WIDTH, D), resident(D, D), const(3, D), const(1, D), const(1, D),
                   const(N_HEAD, HEAD), const(N_HEAD, HEAD, HEAD), const(HEAD, N_HEAD), const(N_HEAD, HEAD, HEAD),
                   const(1, B_WIDTH)) + (ANY_SPEC,) * nr,
        scratch_shapes=[pltpu.VMEM((POOL_HALO + ts, B_WIDTH), F32), pltpu.VMEM((ts + POOL_HALO, B_WIDTH), F32),
                        pltpu.VMEM((ts, D), BF16), pltpu.VMEM((ts, IN_WIDTH), BF16), pltpu.VMEM((ts, D), F32),
                        pltpu.SemaphoreType.DMA((3 * nr,)), pltpu.SemaphoreType.DMA((3 * nr,))],
        compiler_params=pltpu.CompilerParams(dimension_semantics=("arbitrary",), vmem_limit_bytes=VMEM_LIMIT_V7X),
    )(dx1, x, proj, proj, mixed, mod, g_pre, g_post, w_in_t, sgn, w_sp, b_sp_t, w_pool, p_scale, w_out_b, *rs_srcs)
    return outs[:11], outs[11:]


def _pair_add(name, coords, grid, specs_a, specs_b, out_specs, out_shapes, a_arrays, b_arrays):
    n = len(a_arrays)

    def body(co_ref, *refs):
        for k in range(n):
            total = refs[k][...] + refs[n + k][...]
            refs[2 * n + k][...] = total
            refs[3 * n + k][...] = _bf(total)

    outs = pl.pallas_call(
        body,
        name=name,
        grid_spec=pltpu.PrefetchScalarGridSpec(num_scalar_prefetch=1, grid=grid, in_specs=specs_a + specs_b,
                                               out_specs=out_specs * 2),
        out_shape=tuple(jax.ShapeDtypeStruct(s, dt) for dt in (F32, BF16) for s in out_shapes),
        compiler_params=pltpu.CompilerParams(dimension_semantics=("arbitrary",) * len(grid),
                                             vmem_limit_bytes=VMEM_LIMIT_V7X),
    )(coords, *a_arrays, *b_arrays)
    return list(outs[:n]), list(outs[n:])


def _final_add_adamw(coords, s1, r, ws, ms, vs, n_split=4):
    n = len(s1)

    def body(co_ref, *refs):
        for k in range(n):
            s_ref, r_ref, w_ref, m_ref, v_ref = (refs[q * n + k] for q in range(5))
            g_ref, d_ref, nm_ref, nv_ref = (refs[(5 + q) * n + k] for q in range(4))
            g = ((s_ref[...] + r_ref[0].astype(F32)) + r_ref[1].astype(F32)) + r_ref[2].astype(F32)
            g_ref[...] = g
            delta, m, v = _adamw(w_ref[...], g, m_ref[...], v_ref[...])
            d_ref[...] = delta
            nm_ref[...] = m
            nv_ref[...] = v

    def shard_spec(a):
        rows, cols = a.shape
        return pl.BlockSpec((rows // n_split, cols), lambda i, co: (i, 0))

    def mine_spec(a):
        rows, cols = a.shape[2:]
        return pl.BlockSpec((None, None, rows // n_split, cols), lambda i, co: (co[0], co[1], i, 0))

    def recv_spec(a):
        rows, cols = a.shape[1:]
        return pl.BlockSpec((3, rows // n_split, cols), lambda i, co: (0, i, 0))

    in_specs = ([mine_spec(a) for a in s1] + [recv_spec(a) for a in r] + [shard_spec(a) for a in ws] * 3)
    out_specs = [shard_spec(a) for a in ws] * 4
    outs = pl.pallas_call(
        body,
        name="grad_final_adamw",
        grid_spec=pltpu.PrefetchScalarGridSpec(num_scalar_prefetch=1, grid=(n_split,), in_specs=in_specs,
                                               out_specs=out_specs),
        out_shape=tuple(jax.ShapeDtypeStruct(a.shape, F32) for a in ws) * 4,
        compiler_params=pltpu.CompilerParams(dimension_semantics=("arbitrary",), vmem_limit_bytes=VMEM_LIMIT_V7X),
    )(coords, *s1, *r, *ws, *ms, *vs)
    return [tuple(outs[q * n + k] for q in range(4)) for k in range(n)]


def _sibling_swap(tag, g5):
    n = len(g5)

    def body(*refs):
        copies = _sibling_swap_copies(refs[:n], refs[n:2 * n], *refs[2 * n:])
        for cp in copies:
            cp.start()
        for cp in copies:
            cp.wait()

    return list(pl.pallas_call(
        body,
        name="grad_swap_core_" + tag,
        out_shape=tuple(jax.ShapeDtypeStruct(g.shape[1:], F32) for g in g5),
        in_specs=[ANY_SPEC] * n,
        out_specs=(ANY_SPEC,) * n,
        scratch_shapes=[pltpu.SemaphoreType.DMA((n, 4)), pltpu.SemaphoreType.DMA((n, 4))],
    )(*g5))


def _sibling_add(tag, g5, r1, coords, n_split=4):
    shapes = [g.shape[3:] for g in g5]
    spec_g = [pl.BlockSpec((None, None, None, s[0] // n_split, s[1]), lambda i, j, k, co: (i, j, co[2], k, 0))
              for s in shapes]
    spec_r = [pl.BlockSpec((None, None, s[0] // n_split, s[1]), lambda i, j, k, co: (i, j, k, 0)) for s in shapes]
    return _pair_add("grad_add_core_" + tag, coords, (2, 2, n_split), spec_g, spec_r, spec_r,
                     [(2, 2, *s) for s in shapes], g5, r1)


def _tail_exchange(big, partials, pick_mine, dmod3):
    n, nb = len(partials), len(big)
    big_shapes = [g.shape[1:] for g in big]
    big5 = [g.reshape(2, 2, 2, *s) for g, s in zip(big, big_shapes)]
    flips = _ChipExchangeSteps.FLIPS

    def body(*refs):
        g5, p_in, dm_ref = refs[:nb], refs[nb:nb + n], refs[nb + n]
        outs = refs[nb + n + 1:2 * nb + 2 * n + 2]
        g_out, sums, dm2d = outs[:nb], outs[nb:nb + n], outs[nb + n]
        scratch = refs[2 * nb + 2 * n + 2:]
        s1, stage, chip_recv = scratch[:nb], scratch[nb:2 * nb], scratch[2 * nb:3 * nb]
        acc, rbuf = scratch[3 * nb:3 * nb + n], scratch[3 * nb + n:3 * nb + 2 * n]
        (dm_recv, send_sems, recv_sems, dm_send_sems, dm_recv_sems, sib_send, sib_recv, chip_send,
         chip_recv_sems) = scratch[3 * nb + 2 * n:]
        x, y, c = _coords()
        me = 4 * x + 2 * y + c
        sibling = (x, y, 1 - c)
        dm_copies = [
            pltpu.make_async_remote_copy(dm_ref.at[me ^ k], dm_recv.at[k], dm_send_sems.at[k], dm_recv_sems.at[k],
                                         device_id=_peer(k), device_id_type=MESH)
            for k in range(1, N_DEV)
        ]
        for cp in dm_copies:
            cp.start()
        sib_copies = _sibling_swap_copies(g5, s1, sib_send, sib_recv)
        for cp in sib_copies:
            cp.start()
        for a in range(n):
            acc[a][...] = p_in[a][...]

        def small_phase(ph, peer):
            copies = [
                pltpu.make_async_remote_copy(acc[a], rbuf[a].at[ph], send_sems.at[ph, a], recv_sems.at[ph, a],
                                             device_id=peer, device_id_type=MESH)
                for a in range(n)
            ]
            for cp in copies:
                cp.start()
            for cp in copies:
                cp.wait()
            for a in range(n):
                acc[a][...] = acc[a][...] + rbuf[a][ph]

        small_phase(0, sibling)
        for cp in sib_copies:
            cp.wait()
        for a in range(nb):
            for xs in range(2):
                for ys in range(2):
                    total = g5[a][xs, ys, c] + s1[a][xs, ys]
                    s1[a][xs, ys] = total
                    stage[a][xs, ys] = _bf(total)
        chip_copies = [
            pltpu.make_async_remote_copy(stage[a].at[x ^ fx, y ^ fy], chip_recv[a].at[j], chip_send.at[a, j],
                                         chip_recv_sems.at[a, j], device_id=(x ^ fx, y ^ fy, c), device_id_type=MESH)
            for a in range(nb) for j, (fx, fy) in enumerate(flips)
        ]
        for cp in chip_copies:
            cp.start()
        small_phase(1, (1 - x, y, c))
        small_phase(2, (x, 1 - y, c))
        for a in range(n):
            sums[a][...] = acc[a][me] if pick_mine[a] else acc[a][...]
        dm2d[...] = jnp.zeros(dm2d.shape, F32)
        dm2d[0:1, :] = dm_ref[me]
        for cp in dm_copies:
            cp.wait()
        for k in range(1, N_DEV):
            dm2d[k:k + 1, :] = dm_recv[k]
        for cp in chip_copies:
            cp.wait()
        for a in range(nb):
            g_out[a][...] = ((s1[a][x, y] + chip_recv[a][0].astype(F32)) + chip_recv[a][1].astype(F32)) \
                + chip_recv[a][2].astype(F32)

    out_shapes = tuple(jax.ShapeDtypeStruct(s, F32) for s in big_shapes) + tuple(
        jax.ShapeDtypeStruct(p.shape[1:] if pk else p.shape, F32) for p, pk in zip(partials, pick_mine))
    outs = pl.pallas_call(
        body,
        name="tail_exchange",
        out_shape=out_shapes + (jax.ShapeDtypeStruct((2 * N_DEV, MOD_COLS), F32),),
        in_specs=[VMEM_SPEC] * (nb + n + 1),
        out_specs=(VMEM_SPEC,) * (nb + n + 1),
        scratch_shapes=[pltpu.VMEM((2, 2, *s), F32) for s in big_shapes]
        + [pltpu.VMEM((2, 2, *s), BF16) for s in big_shapes]
        + [pltpu.VMEM((3, *s), BF16) for s in big_shapes]
        + [pltpu.VMEM(p.shape, F32) for p in partials]
        + [pltpu.VMEM((3, *p.shape), F32) for p in partials]
        + [pltpu.VMEM((N_DEV, 1, MOD_COLS), F32), pltpu.SemaphoreType.DMA((3, n)), pltpu.SemaphoreType.DMA((3, n)),
           pltpu.SemaphoreType.DMA((N_DEV,)), pltpu.SemaphoreType.DMA((N_DEV,)),
           pltpu.SemaphoreType.DMA((nb, 4)), pltpu.SemaphoreType.DMA((nb, 4)),
           pltpu.SemaphoreType.DMA((nb, 3)), pltpu.SemaphoreType.DMA((nb, 3))],
        compiler_params=pltpu.CompilerParams(vmem_limit_bytes=VMEM_LIMIT_V7X),
    )(*big5, *partials, dmod3)
    return list(outs[:nb]), list(outs[nb:nb + n]), outs[nb + n]


def _small_update(grads, ws, ms, vs, scx, dm2d, w_ada, m_ada, v_ada, loss_lanes):
    n = len(grads)

    def body(*refs):
        g_in, w_in, m_in, v_in = (refs[q * n:(q + 1) * n] for q in range(4))
        scx_ref, dm_ref, wa_ref, ma_ref, va_ref, ll_ref = refs[4 * n:4 * n + 6]
        outs = refs[4 * n + 6:]
        g_out, d_out, nm_out, nv_out = (outs[q * (n + 1):(q + 1) * (n + 1)] for q in range(4))
        loss_ref = outs[4 * (n + 1)]
        for a in range(n + 1):
            if a < n:
                g, w, m, v = g_in[a][...], w_in[a][...], m_in[a][...], v_in[a][...]
            else:
                g = _mm_tn(_bf(scx_ref[...]), _bf(dm_ref[...]))
                w, m, v = wa_ref[...], ma_ref[...], va_ref[...]
            g_out[a][...] = g
            delta, m, v = _adamw(w, g, m, v)
            d_out[a][...] = delta
            nm_out[a][...] = m
            nv_out[a][...] = v
        loss_ref[...] = jnp.sum(ll_ref[...], axis=1, keepdims=True) * (0.5 / D)

    w_shapes = tuple(jax.ShapeDtypeStruct(w.shape, F32) for w in list(ws) + [w_ada])
    outs = pl.pallas_call(
        body,
        name="small_update",
        out_shape=w_shapes * 4 + (jax.ShapeDtypeStruct((1, 1), F32),),
        in_specs=[VMEM_SPEC] * (4 * n + 6),
        out_specs=(VMEM_SPEC,) * (4 * (n + 1) + 1),
        compiler_params=pltpu.CompilerParams(vmem_limit_bytes=VMEM_LIMIT_V7X),
    )(*grads, *ws, *ms, *vs, scx, dm2d, w_ada, m_ada, v_ada, loss_lanes)
    return [tuple(outs[q * (n + 1) + k] for q in range(4)) for k in range(n + 1)], outs[4 * (n + 1)]


def kernel(x, c, w_ada, b_ada, pre_mix_g, post_mix_g, w_in, sgu_norm_g, w_spatial, b_spatial, w_pool, pool_scale, w_out, pre_ffn_g, post_ffn_g, w_up, conv_w, conv_b, w_down, loss_target, m_w_ada, m_b_ada, m_pre_mix_g, m_post_mix_g, m_w_in, m_sgu_norm_g, m_w_spatial, m_b_spatial, m_w_pool, m_pool_scale, m_w_out, m_pre_ffn_g, m_post_ffn_g, m_w_up, m_conv_w, m_conv_b, m_w_down, v_w_ada, v_b_ada, v_pre_mix_g, v_post_mix_g, v_w_in, v_sgu_norm_g, v_w_spatial, v_b_spatial, v_w_pool, v_pool_scale, v_w_out, v_pre_ffn_g, v_post_ffn_g, v_w_up, v_conv_w, v_conv_b, v_w_down):
    t_len = x.shape[1]
    ts = min(256, t_len)
    ts_mix = min(512, t_len)
    ts_w = min(1024, t_len)
    coords = jnp.stack([lax.axis_index("x"), lax.axis_index("y"), lax.axis_index("c")]).astype(jnp.int32)

    w_in_t, w_up_t = w_in[0].T, w_up[0].T
    mod3, scx, (g_in, g_out) = _prologue(c, w_ada[0], b_ada.reshape(N_DEV, 1, MOD_COLS), [w_in_t, w_out[0]],
                                         [BF16, BF16])
    mod = mod3.reshape(N_MOD, D)
    w_in_tb = g_in.reshape(IN_WIDTH, D)
    w_out_b = g_out.reshape(D, D)
    conv_b8 = conv_b.reshape(N_DEV, FF_CHUNK)
    b_sp_t = b_spatial[0].T

    x2d, tgt = x[0], loss_target[0]
    (x1, proj, mixed), (g_up, g_down, g_cw) = _mix_fwd(
        x2d, mod, pre_mix_g, post_mix_g, w_in_tb, sgu_norm_g[0], w_spatial[0], b_sp_t, w_pool[0], pool_scale, w_out_b,
        ts_mix, [w_up_t, w_down[0], conv_w[0]], [w_up.shape[1:], w_down.shape[1:], conv_w.shape[1:]], [BF16, BF16, F32])
    w_down_b = g_down.reshape(FF, D)
    up, f, dx2, loss_lanes = _ffn_fwd(x1, tgt, mod, pre_ffn_g, post_ffn_g, g_up, g_cw, conv_b8, w_down_b, ts)

    (dx1, dup, act, df, h2, dmod_f, d_pre_ffn, d_post_ffn, d_cb8, d_cw8) = _ffn_bwd(
        dx2, f, x1, up, mod, pre_ffn_g, post_ffn_g, g_up, g_cw, conv_b8, w_down_b, ts)
    gw_up = _wgrad_up(h2, dup, ts_w).reshape(2, 2, 2, FF_CHUNK, D)
    gw_down, r1_up = _wgrad_down(act, df, ts_w, gw_up)
    gw_down = gw_down.reshape(2, 2, 2, FF // N_DEV, D)
    s1_ffn, s1_ffn_b = _sibling_add("ffn", [gw_up, gw_down], [r1_up] + _sibling_swap("down", [gw_down]), coords)
    ((grad_x, gw_in, gw_out, dmod_m, d_pre_mix, d_post_mix, d_sgn, d_wsp, d_bsp_t, d_wpool, d_ps), r_ffn) = _mix_bwd(
        dx1, x2d, proj, mixed, mod, pre_mix_g, post_mix_g, w_in_tb, sgu_norm_g[0], w_spatial[0], b_sp_t,
        w_pool[0], pool_scale, w_out_b, ts, s1_ffn_b)
    gw_in = gw_in.reshape(N_DEV, IN_WIDTH // N_DEV, D)
    gw_out = gw_out.reshape(N_DEV, D // N_DEV, D)

    big = _final_add_adamw(coords, s1_ffn, list(r_ffn), [w_up_t, w_down[0]], [m_w_up[0].T, m_w_down[0]],
                           [v_w_up[0].T, v_w_down[0]])
    r_up, r_down = tuple(a.T[None] for a in big[0]), tuple(a[None] for a in big[1])

    dmod = jnp.concatenate([dmod_m, dmod_f], axis=0)
    names = ["b_ada", "pre_mix_g", "post_mix_g", "sgu_norm_g", "w_spatial", "b_spatial", "w_pool", "pool_scale",
             "pre_ffn_g", "post_ffn_g", "conv_w", "conv_b"]
    partials = [dmod.reshape(1, N_MOD * D), d_pre_mix, d_post_mix, d_sgn, d_wsp, d_bsp_t.T, d_wpool, d_ps,
                d_pre_ffn, d_post_ffn, d_cw8, d_cb8.reshape(1, 2 * FF), loss_lanes]
    small_w = [b_ada, pre_mix_g, post_mix_g, sgu_norm_g[0], w_spatial[0], b_spatial[0], w_pool[0], pool_scale,
               pre_ffn_g, post_ffn_g, conv_w[0], conv_b]
    small_m = [m_b_ada, m_pre_mix_g, m_post_mix_g, m_sgu_norm_g[0], m_w_spatial[0], m_b_spatial[0], m_w_pool[0],
               m_pool_scale, m_pre_ffn_g, m_post_ffn_g, m_conv_w[0], m_conv_b]
    small_v = [v_b_ada, v_pre_mix_g, v_post_mix_g, v_sgu_norm_g[0], v_w_spatial[0], v_b_spatial[0], v_w_pool[0],
               v_pool_scale, v_pre_ffn_g, v_post_ffn_g, v_conv_w[0], v_conv_b]
    g_mix, sums, dm2d = _tail_exchange([gw_in, gw_out], partials, [nm == "conv_w" for nm in names] + [False],
                                       dmod.reshape(N_DEV, 1, MOD_COLS))
    small, loss11 = _small_update(
        sums[:-1] + g_mix, small_w + [w_in_t, w_out[0]], small_m + [m_w_in[0].T, m_w_out[0]],
        small_v + [v_w_in[0].T, v_w_out[0]], scx, dm2d, w_ada[0], m_w_ada[0], v_w_ada[0], sums[-1])
    loss = loss11.reshape(())
    lead = {"sgu_norm_g", "w_spatial", "b_spatial", "w_pool", "conv_w", "w_in", "w_out", "w_ada"}
    res = {nm: tuple((a.T if nm == "w_in" else a)[None] if nm in lead else a for a in four)
           for nm, four in zip(names + ["w_in", "w_out", "w_ada"], small)}
    res.update(w_up=r_up, w_down=r_down)

    order = ["w_ada", "b_ada", "pre_mix_g", "post_mix_g", "w_in", "sgu_norm_g", "w_spatial", "b_spatial", "w_pool",
             "pool_scale", "w_out", "pre_ffn_g", "post_ffn_g", "w_up", "conv_w", "conv_b", "w_down"]
    return (loss, grad_x[None], *[res[nm][0] for nm in order], *[res[nm][1] for nm in order],
            *[res[nm][2] for nm in order], *[res[nm][3] for nm in order])
```

```python
import functools
import math

import jax
import jax.numpy as jnp
from jax import lax
from jax.experimental import pallas as pl
from jax.experimental.pallas import tpu as pltpu

F32 = jnp.float32
BF16 = jnp.bfloat16
MESH = pl.DeviceIdType.MESH

EPS = 1e-6
D = 1024
HEAD = 128
N_HEAD = 4
A_WIDTH = 512
B_WIDTH = 512
IN_WIDTH = 1536
WINDOWS = (2, 4, 8, 16)
CHUNK = 64
FF = 2816
N_DEV = 8
FF_CHUNK = 704
N_MOD = 6
MOD_COLS = 768

ADAM_LR = 0.001
ADAM_B1 = 0.9
ADAM_B2 = 0.999
ADAM_EPS = 1e-08
ADAM_WD = 0.01
ADAM_STEP = 10

VMEM_LIMIT_V7X = 62 * 1024 * 1024
HALO = 8
POOL_HALO = 16

VMEM_SPEC = pl.BlockSpec(memory_space=pltpu.VMEM)
ANY_SPEC = pl.BlockSpec(memory_space=pl.ANY)


def _bf(x):
    return x.astype(BF16)


def _mm(a, b):
    return jnp.dot(a, b, preferred_element_type=F32)


def _mm_nt(a, b):
    return lax.dot_general(a, b, (((1,), (1,)), ((), ())), preferred_element_type=F32)


def _mm_tn(a, b):
    return lax.dot_general(a, b, (((0,), (0,)), ((), ())), preferred_element_type=F32)


def _rstd(x):
    return lax.rsqrt(jnp.mean(x * x, axis=-1, keepdims=True) + EPS)


def _sum0(x):
    return jnp.sum(x, axis=0, keepdims=True)


def _rowmean(x):
    return jnp.mean(x, axis=-1, keepdims=True)


_GELU_K = math.sqrt(2.0 / math.pi)


def _gelu_and_grad(x):
    x2 = x * x
    th = jnp.tanh(_GELU_K * (x + 0.044715 * (x * x2)))
    cdf = 0.5 * (1.0 + th)
    grad = cdf + 0.5 * x * (1.0 - th * th) * (_GELU_K * (1.0 + 3.0 * 0.044715 * x2))
    return x * cdf, grad


def _gelu(x):
    return x * (0.5 * (1.0 + jnp.tanh(_GELU_K * (x + 0.044715 * (x * x * x)))))


def _sigmoid(x):
    return 0.5 * jnp.tanh(0.5 * x) + 0.5


def _sgu_mask():
    ri = lax.broadcasted_iota(jnp.int32, (HEAD, HEAD), 0)
    ci = lax.broadcasted_iota(jnp.int32, (HEAD, HEAD), 1)
    return (ci // CHUNK) <= (ri // CHUNK)


def _window_sum(ext, w, trailing):
    n = ext.shape[0]
    s, k = ext, 1
    while k < w:
        s = s + pltpu.roll(s, k if trailing else n - k, 0)
        k *= 2
    return s


def _inv_count(row0, n, w):
    t = row0 + lax.broadcasted_iota(jnp.int32, (n, 1), 0)
    return 1.0 / jnp.minimum(t + 1, w).astype(F32)


def _shift_down(v, before, k):
    rows = lax.broadcasted_iota(jnp.int32, before.shape, 0)
    r = pltpu.roll(v, k, 0)
    top = jnp.where(rows < k, pltpu.roll(before, k, 0), r[0:HALO])
    return jnp.concatenate([top, r[HALO:]], axis=0)


def _shift_up(v, after, k):
    n = v.shape[0]
    rows = lax.broadcasted_iota(jnp.int32, after.shape, 0)
    r = pltpu.roll(v, n - k, 0)
    bottom = jnp.where(rows >= HALO - k, pltpu.roll(after, HALO - k, 0), r[n - HALO:])
    return jnp.concatenate([r[:n - HALO], bottom], axis=0)


def _adamw(w, g, m, v):
    m = ADAM_B1 * m + (1.0 - ADAM_B1) * g
    v = ADAM_B2 * v + (1.0 - ADAM_B2) * (g * g)
    m_hat = m / (1.0 - ADAM_B1 ** ADAM_STEP)
    v_hat = v / (1.0 - ADAM_B2 ** ADAM_STEP)
    delta = -ADAM_LR * (m_hat / (jnp.sqrt(v_hat) + ADAM_EPS) + ADAM_WD * w)
    return delta, m, v


def _coords():
    return lax.axis_index("x"), lax.axis_index("y"), lax.axis_index("c")


def _peer(k):
    x, y, c = _coords()
    return (x ^ ((k >> 2) & 1), y ^ ((k >> 1) & 1), c ^ (k & 1))


def _my_index():
    x, y, c = _coords()
    return 4 * x + 2 * y + c


def _adaln_modulation(c_ref, w_ref, b_ref, mod_ref, scx_ref, scbuf, stage, recv, send_sems, recv_sems):
    me = _my_index()
    cv = c_ref[...]
    scbuf[0] = cv * _sigmoid(cv)
    first = [
        pltpu.make_async_remote_copy(scbuf.at[0], scbuf.at[k], send_sems.at[0, k], recv_sems.at[0, k],
                                     device_id=_peer(k), device_id_type=MESH)
        for k in range(1, N_DEV)
    ]
    for cp in first:
        cp.start()
    for cp in first:
        cp.wait()
    scx_ref[...] = jnp.zeros(scx_ref.shape, F32)
    for k in range(N_DEV):
        scx_ref[k:k + 1, :] = scbuf[k]
    prod = _mm(_bf(scx_ref[...]), _bf(w_ref[...]))
    for k in range(N_DEV):
        stage[k] = prod[k:k + 1, :] + b_ref[me]
    second = [
        pltpu.make_async_remote_copy(stage.at[k], recv.at[k], send_sems.at[1, k], recv_sems.at[1, k],
                                     device_id=_peer(k), device_id_type=MESH)
        for k in range(1, N_DEV)
    ]
    for cp in second:
        cp.start()
    mod_ref[me] = stage[0]
    for cp in second:
        cp.wait()
    for k in range(1, N_DEV):
        mod_ref[me ^ k] = recv[k]


class _GatherSteps:
    def __init__(self, ins, outs, stages, send_sems, recv_sems, local_sems):
        self.ins, self.outs, self.stages = ins, outs, stages
        self.send_sems, self.recv_sems, self.local_sems = send_sems, recv_sems, local_sems
        x, y, c = _coords()
        self.c = c
        self.me, self.sibling = (x, y, c), (x, y, 1 - c)
        self.chips = [(1 - x, y), (x, 1 - y), (1 - x, 1 - y)]

    def _copy(self, a, k, block, to, from_stage=False):
        dst = self.outs[a].at[4 * block[0] + 2 * block[1] + block[2]]
        return pltpu.make_async_remote_copy(self.stages[a] if from_stage else dst, dst, self.send_sems.at[a, k],
                                            self.recv_sems.at[a, k], device_id=to, device_id_type=MESH)

    def _local(self, a):
        me = self.me
        return pltpu.make_async_copy(self.stages[a], self.outs[a].at[4 * me[0] + 2 * me[1] + me[2]],
                                     self.local_sems.at[a])

    def _first(self, a):
        cps = [self._copy(a, 0, self.me, self.sibling, from_stage=True)]
        return cps + [self._copy(a, 1 + j, self.me, (*chip, self.c), from_stage=True)
                      for j, chip in enumerate(self.chips)]

    def _passed(self, a, j):
        return self._copy(a, 4 + j, (*self.chips[j], self.c), self.sibling)

    def start(self):
        for a in range(len(self.ins)):
            block = self.ins[a][...]
            if block.shape != self.stages[a].shape:
                block = block.T
            self.stages[a][...] = block.astype(self.stages[a].dtype)
            self._local(a).start()
            for cp in self._first(a):
                cp.start()

    def forward(self):
        for a in range(len(self.ins)):
            for j, chip in enumerate(self.chips):
                self._copy(a, 1 + j, (*chip, self.c), self.me).wait_recv()
                self._passed(a, j).start()

    def finish(self):
        for a in range(len(self.ins)):
            self._copy(a, 0, self.sibling, self.me).wait_recv()
            for j, chip in enumerate(self.chips):
                self._copy(a, 4 + j, (*chip, 1 - self.c), self.me).wait_recv()
            for cp in self._first(a) + [self._passed(a, j) for j in range(3)]:
                cp.wait_send()
            self._local(a).wait()


def _gather_scratch(shapes, out_dtypes):
    n = len(shapes)
    return ([pltpu.VMEM(s, dt) for s, dt in zip(shapes, out_dtypes)]
            + [pltpu.SemaphoreType.DMA((n, 7)), pltpu.SemaphoreType.DMA((n, 7)), pltpu.SemaphoreType.DMA((n,))])


def _gather_out_shapes(shapes, out_dtypes):
    return tuple(jax.ShapeDtypeStruct((N_DEV, *s), dt) for s, dt in zip(shapes, out_dtypes))


def _prologue(c_row, w_ada, b_ada3, shards, out_dtypes):
    n = len(shards)

    def body(*refs):
        c_ref, w_ref, b_ref = refs[:3]
        mod_ref, scx_ref = refs[3 + n:5 + n]
        gather = _GatherSteps(refs[3:3 + n], refs[5 + n:5 + 2 * n], refs[5 + 2 * n:5 + 3 * n],
                              *refs[5 + 3 * n:8 + 3 * n])
        gather.start()
        _adaln_modulation(c_ref, w_ref, b_ref, mod_ref, scx_ref, *refs[8 + 3 * n:])
        gather.forward()
        gather.finish()

    outs = pl.pallas_call(
        body,
        name="prologue",
        out_shape=(jax.ShapeDtypeStruct((N_DEV, 1, MOD_COLS), F32), jax.ShapeDtypeStruct((2 * N_DEV, D), F32))
        + _gather_out_shapes([s.shape for s in shards], out_dtypes),
        in_specs=[VMEM_SPEC] * (3 + n),
        out_specs=(VMEM_SPEC, VMEM_SPEC) + (ANY_SPEC,) * n,
        scratch_shapes=_gather_scratch([s.shape for s in shards], out_dtypes) + [
            pltpu.VMEM((N_DEV, 1, D), F32),
            pltpu.VMEM((N_DEV, 1, MOD_COLS), F32),
            pltpu.VMEM((N_DEV, 1, MOD_COLS), F32),
            pltpu.SemaphoreType.DMA((2, N_DEV)),
            pltpu.SemaphoreType.DMA((2, N_DEV)),
        ],
        compiler_params=pltpu.CompilerParams(vmem_limit_bytes=VMEM_LIMIT_V7X),
    )(c_row, w_ada, b_ada3, *shards)
    return outs[0], outs[1], outs[2:]


class _ChipExchangeSteps:
    FLIPS = ((1, 0), (0, 1), (1, 1))

    def __init__(self, srcs, dsts, send_sems, recv_sems):
        self.srcs, self.dsts, self.send_sems, self.recv_sems = srcs, dsts, send_sems, recv_sems

    def _copies(self):
        x, y, c = _coords()
        out = []
        for a in range(len(self.srcs)):
            for j, (fx, fy) in enumerate(self.FLIPS):
                k = 3 * a + j
                out.append(pltpu.make_async_remote_copy(
                    self.srcs[a].at[x ^ fx, y ^ fy], self.dsts[a].at[j], self.send_sems.at[k], self.recv_sems.at[k],
                    device_id=(x ^ fx, y ^ fy, c), device_id_type=MESH))
        return out

    def start(self):
        for cp in self._copies():
            cp.start()

    def finish(self):
        for cp in self._copies():
            cp.wait()


def _mix_fwd(x, mod, g_pre, g_post, w_in_t, sgn, w_sp, b_sp_t, w_pool, p_scale, w_out_b, ts, shards, shard_shapes,
             shard_dtypes):
    t_len = x.shape[0]
    nt, nb = t_len // ts, ts // HEAD
    ns = len(shards)

    def body(*refs):
        (x_ref, mod_ref, g1_ref, g2_ref, win_ref, sgn_ref, ws_ref, bst_ref, wp_ref, ps_ref, wout_ref) = refs[:11]
        x1_ref, proj_ref, mixed_ref = refs[11 + ns:14 + ns]
        pbuf, cat = refs[14 + 2 * ns:16 + 2 * ns]
        gather = _GatherSteps(refs[11:11 + ns], refs[14 + ns:14 + 2 * ns], refs[16 + 2 * ns:16 + 3 * ns],
                              *refs[16 + 3 * ns:])
        i = pl.program_id(0)

        @pl.when(i == 0)
        def _():
            pbuf[0:POOL_HALO, :] = jnp.zeros((POOL_HALO, B_WIDTH), F32)
            gather.start()

        @pl.when(i == (3 * nt) // 4)
        def _():
            gather.forward()

        xv = x_ref[...]
        sh, sc, gm = mod_ref[0:1, :], mod_ref[1:2, :], mod_ref[2:3, :]
        h = (xv * _rstd(xv) * g1_ref[...]) * (1.0 + sc) + sh
        proj_ref[...] = _mm_nt(_bf(h), win_ref[...])
        pbuf[POOL_HALO:POOL_HALO + ts, :] = proj_ref[:, 2 * A_WIDTH:]
        smask = _sgu_mask()
        for hd in range(N_HEAD):
            u = _gelu(proj_ref[:, hd * HEAD:(hd + 1) * HEAD])
            v = _gelu(proj_ref[:, A_WIDTH + hd * HEAD:A_WIDTH + (hd + 1) * HEAD])
            vn = _bf(v * _rstd(v) * sgn_ref[hd:hd + 1, :])
            wm = _bf(jnp.where(smask, ws_ref[hd], 0.0))
            bias = bst_ref[:, hd:hd + 1]
            for b in range(nb):
                rows = slice(b * HEAD, (b + 1) * HEAD)
                z = _mm(wm, vn[rows]) + bias
                cat[rows, hd * HEAD:(hd + 1) * HEAD] = _bf(u[rows] * z)
        for g, w in enumerate(WINDOWS):
            cols = slice(g * HEAD, (g + 1) * HEAD)
            ext = pbuf[:, cols]
            pooled = _window_sum(ext, w, True)[POOL_HALO:] * _inv_count(i * ts, ts, w) - ext[POOL_HALO:]
            cat[:, A_WIDTH + g * HEAD:A_WIDTH + (g + 1) * HEAD] = _bf(_mm(_bf(pooled), _bf(wp_ref[g])) * ps_ref[:, cols])
        pbuf[0:POOL_HALO, :] = pbuf[ts:ts + POOL_HALO, :]
        mixed = _mm(cat[...], wout_ref[...])
        mixed_ref[...] = mixed
        x1_ref[...] = xv + gm * (mixed * _rstd(mixed) * g2_ref[...])

        @pl.when(i == nt - 1)
        def _():
            gather.finish()

    tile = lambda wid: pl.BlockSpec((ts, wid), lambda i: (i, 0))
    outs = pl.pallas_call(
        body,
        name="mix_fwd",
        grid=(nt,),
        out_shape=(jax.ShapeDtypeStruct((t_len, D), F32), jax.ShapeDtypeStruct((t_len, IN_WIDTH), F32),
                   jax.ShapeDtypeStruct((t_len, D), F32)) + _gather_out_shapes(shard_shapes, shard_dtypes),
        in_specs=[tile(D)] + [VMEM_SPEC] * (10 + ns),
        out_specs=(tile(D), tile(IN_WIDTH), tile(D)) + (ANY_SPEC,) * ns,
        scratch_shapes=[pltpu.VMEM((POOL_HALO + ts, B_WIDTH), F32), pltpu.VMEM((ts, D), BF16)]
        + _gather_scratch(shard_shapes, shard_dtypes),
        compiler_params=pltpu.CompilerParams(dimension_semantics=("arbitrary",), vmem_limit_bytes=VMEM_LIMIT_V7X),
    )(x, mod, g_pre, g_post, w_in_t, sgn, w_sp, b_sp_t, w_pool, p_scale, w_out_b, *shards)
    return outs[:3], outs[3:]


def _ffn_fwd(x1, target, mod, g_pre, g_post, w_up_b, conv_w8, conv_b8, w_down_b, ts):
    t_len = x1.shape[0]
    nt = t_len // ts

    def body(x1_ref, tgt_ref, mod_ref, g3_ref, g4_ref, wup_ref, cw_ref, cb_ref, wdown_ref,
             up_ref, f_ref, dx2_ref, loss_ref, ucarry):
        i = pl.program_id(0)

        @pl.when(i == 0)
        def _():
            ucarry[...] = jnp.zeros(ucarry.shape, F32)
            loss_ref[...] = jnp.zeros(loss_ref.shape, F32)

        x1v = x1_ref[...]
        sh, sc, gf = mod_ref[3:4, :], mod_ref[4:5, :], mod_ref[5:6, :]
        h2 = _bf((x1v * _rstd(x1v) * g3_ref[...]) * (1.0 + sc) + sh)
        half = N_DEV // 2

        def up_pair(j):
            return [_mm(h2, wup_ref[jj]) for jj in (j, j + half)]

        f = jnp.zeros((ts, D), F32)
        ups = up_pair(0)
        for j in range(half):
            nxt = up_pair(j + 1) if j + 1 < half else None
            ys = []
            for up, jj in zip(ups, (j, j + half)):
                up_ref[jj] = up
                before = ucarry[jj]
                ucarry[jj] = up[ts - HALO:, :]
                cw = cw_ref[jj]
                ys.append(cb_ref[jj:jj + 1, :] + _shift_down(up, before, 2) * cw[0:1, :]
                          + _shift_down(up, before, 1) * cw[1:2, :] + up * cw[2:3, :])
            gate, val = ys
            hg = 0.5 * gate
            act = (hg * jnp.tanh(hg) + hg) * val
            f = f + _mm(_bf(act), wdown_ref[j * FF_CHUNK:(j + 1) * FF_CHUNK, :])
            ups = nxt
        f_ref[...] = f
        x2 = x1v + gf * (f * _rstd(f) * g4_ref[...])
        err = x2 - tgt_ref[...]
        loss_ref[...] += _sum0(err * err)
        dx2_ref[...] = err * (1.0 / D)

    tile = pl.BlockSpec((ts, D), lambda i: (i, 0))
    return pl.pallas_call(
        body,
        name="ffn_fwd",
        grid=(nt,),
        out_shape=(jax.ShapeDtypeStruct((N_DEV, t_len, FF_CHUNK), F32), jax.ShapeDtypeStruct((t_len, D), F32),
                   jax.ShapeDtypeStruct((t_len, D), F32), jax.ShapeDtypeStruct((1, D), F32)),
        in_specs=[tile, tile] + [VMEM_SPEC] * 7,
        out_specs=(pl.BlockSpec((N_DEV, ts, FF_CHUNK), lambda i: (0, i, 0)), tile, tile,
                   pl.BlockSpec((1, D), lambda i: (0, 0))),
        scratch_shapes=[pltpu.VMEM((N_DEV, HALO, FF_CHUNK), F32)],
        compiler_params=pltpu.CompilerParams(dimension_semantics=("arbitrary",), vmem_limit_bytes=VMEM_LIMIT_V7X),
    )(x1, target, mod, g_pre, g_post, w_up_b, conv_w8, conv_b8, w_down_b)


def _ffn_bwd(dx2, f, x1, up, mod, g_pre, g_post, w_up_b, conv_w8, conv_b8, w_down_b, ts):
    t_len = x1.shape[0]
    nt = t_len // ts
    half = N_DEV // 2

    def body(dx2_ref, f_ref, x1_ref, up_ref, halo_ref, mod_ref, g3_ref, g4_ref, wup_ref, cw_ref, cb_ref, wdown_ref,
             dx1_ref, dup_ref, act_ref, df_ref, h2_ref, dmod_ref, dg3_ref, dg4_ref, dcb_ref, dcw_ref,
             dycarry, dh2acc):
        i = pl.program_id(0)
        r = nt - 1 - i

        @pl.when(i == 0)
        def _():
            for ref in (dmod_ref, dg3_ref, dg4_ref, dcb_ref, dcw_ref, dycarry):
                ref[...] = jnp.zeros(ref.shape, F32)

        dx2v, fv, x1v = dx2_ref[...], f_ref[...], x1_ref[...]
        sh, sc, gf = mod_ref[3:4, :], mod_ref[4:5, :], mod_ref[5:6, :]
        g3, g4 = g3_ref[...], g4_ref[...]
        rstd4 = _rstd(fv)
        fh = fv * rstd4
        dmod_ref[2:3, :] += _sum0(dx2v * (fh * g4))
        dr = dx2v * gf
        dg4_ref[...] += _sum0(dr * fh)
        dfh = dr * g4
        dfb = _bf(rstd4 * (dfh - fh * _rowmean(dfh * fh)))
        df_ref[...] = dfb
        rstd3 = _rstd(x1v)
        xh = x1v * rstd3
        n3 = xh * g3
        h2_ref[...] = _bf(n3 * (1.0 + sc) + sh)
        dh2acc[...] = jnp.zeros((ts, D), F32)
        keep = jnp.where(r > 0, 1.0, 0.0).astype(F32)

        def dact_of(j):
            return _mm_nt(dfb, wdown_ref[j * FF_CHUNK:(j + 1) * FF_CHUNK, :])

        dact_next = dact_of(0)
        for j in range(half):
            dact = dact_next
            if j + 1 < half:
                dact_next = dact_of(j + 1)
            ys = []
            for jj in (j, j + half):
                before = halo_ref[jj] * keep
                upc = up_ref[jj]
                cw = cw_ref[jj]
                ys.append(cb_ref[jj:jj + 1, :] + _shift_down(upc, before, 2) * cw[0:1, :]
                          + _shift_down(upc, before, 1) * cw[1:2, :] + upc * cw[2:3, :])
            gate, val = ys
            sg = _sigmoid(gate)
            silu = gate * sg
            act_ref[j] = _bf(silu * val)
            dys = (dact * val * (sg + silu * (1.0 - sg)), dact * silu)
            for q, jj in enumerate((j, j + half)):
                dy = dys[q]
                cw = cw_ref[jj]
                dcb_ref[jj:jj + 1, :] += _sum0(dy)
                after = dycarry[jj]
                dycarry[jj] = dy[0:HALO, :]
                dy1, dy2 = _shift_up(dy, after, 1), _shift_up(dy, after, 2)
                upc = up_ref[jj]
                dcw_ref[jj, 0:1, :] += _sum0(dy2 * upc)
                dcw_ref[jj, 1:2, :] += _sum0(dy1 * upc)
                dcw_ref[jj, 2:3, :] += _sum0(dy * upc)
                dup = _bf(dy * cw[2:3, :] + dy1 * cw[1:2, :] + dy2 * cw[0:1, :])
                dup_ref[jj] = dup
                dh2acc[...] += _mm_nt(dup, wup_ref[jj])
        dh2 = dh2acc[...]
        dmod_ref[0:1, :] += _sum0(dh2)
        dmod_ref[1:2, :] += _sum0(dh2 * n3)
        dn3 = dh2 * (1.0 + sc)
        dg3_ref[...] += _sum0(dn3 * xh)
        dxh = dn3 * g3
        dx1_ref[...] = dx2v + rstd3 * (dxh - xh * _rowmean(dxh * xh))

    tile = pl.BlockSpec((ts, D), lambda i: (nt - 1 - i, 0))
    chunked = lambda n: pl.BlockSpec((n, ts, FF_CHUNK), lambda i: (0, nt - 1 - i, 0))
    halo = pl.BlockSpec((N_DEV, HALO, FF_CHUNK), lambda i: (0, jnp.maximum((nt - 1 - i) * (ts // HALO) - 1, 0), 0))
    const = lambda *shape: pl.BlockSpec(shape, lambda i: (0,) * len(shape))
    return pl.pallas_call(
        body,
        name="ffn_bwd",
        grid=(nt,),
        out_shape=(jax.ShapeDtypeStruct((t_len, D), F32), jax.ShapeDtypeStruct((N_DEV, t_len, FF_CHUNK), BF16),
                   jax.ShapeDtypeStruct((half, t_len, FF_CHUNK), BF16), jax.ShapeDtypeStruct((t_len, D), BF16),
                   jax.ShapeDtypeStruct((t_len, D), BF16), jax.ShapeDtypeStruct((3, D), F32),
                   jax.ShapeDtypeStruct((1, D), F32), jax.ShapeDtypeStruct((1, D), F32),
                   jax.ShapeDtypeStruct((N_DEV, FF_CHUNK), F32), jax.ShapeDtypeStruct((N_DEV, 3, FF_CHUNK), F32)),
        in_specs=[tile, tile, tile, chunked(N_DEV), halo] + [VMEM_SPEC] * 7,
        out_specs=(tile, chunked(N_DEV), chunked(half), tile, tile, const(3, D), const(1, D), const(1, D),
                   const(N_DEV, FF_CHUNK), const(N_DEV, 3, FF_CHUNK)),
        scratch_shapes=[pltpu.VMEM((N_DEV, HALO, FF_CHUNK), F32), pltpu.VMEM((ts, D), F32)],
        compiler_params=pltpu.CompilerParams(dimension_semantics=("arbitrary",), vmem_limit_bytes=VMEM_LIMIT_V7X),
    )(dx2, f, x1, up, up, mod, g_pre, g_post, w_up_b, conv_w8, conv_b8, w_down_b)


def _wgrad_up(h2, dup, ts):
    t_len = h2.shape[0]
    nt, half = t_len // ts, N_DEV // 2

    def body(h2_ref, dup_ref, out_ref):
        @pl.when(pl.program_id(1) == 0)
        def _():
            out_ref[...] = jnp.zeros(out_ref.shape, F32)

        for q in range(half):
            out_ref[q] += _mm_tn(dup_ref[q], h2_ref[...])

    return pl.pallas_call(
        body,
        name="wgrad_up",
        grid=(2, nt),
        out_shape=jax.ShapeDtypeStruct((N_DEV, FF_CHUNK, D), F32),
        in_specs=[pl.BlockSpec((ts, D), lambda g, t: (t, 0)), pl.BlockSpec((half, ts, FF_CHUNK), lambda g, t: (g, t, 0))],
        out_specs=pl.BlockSpec((half, FF_CHUNK, D), lambda g, t: (g, 0, 0)),
        compiler_params=pltpu.CompilerParams(dimension_semantics=("arbitrary", "arbitrary"),
                                             vmem_limit_bytes=VMEM_LIMIT_V7X),
    )(h2, dup)


def _sibling_swap_copies(srcs, dsts, send_sems, recv_sems):
    x, y, c = _coords()
    return [
        pltpu.make_async_remote_copy(srcs[a].at[xs, ys, 1 - c], dsts[a].at[xs, ys], send_sems.at[a, 2 * xs + ys],
                                     recv_sems.at[a, 2 * xs + ys], device_id=(x, y, 1 - c), device_id_type=MESH)
        for a in range(len(srcs)) for xs in range(2) for ys in range(2)
    ]


def _wgrad_down(act, df, ts, swap_src):
    t_len = df.shape[0]
    nt, half = t_len // ts, N_DEV // 2

    def body(act_ref, df_ref, src_ref, out_ref, dst_ref, send_sems, recv_sems):
        t = pl.program_id(0)

        @pl.when(t == 0)
        def _():
            for cp in _sibling_swap_copies([src_ref], [dst_ref], send_sems, recv_sems):
                cp.start()
            out_ref[...] = jnp.zeros(out_ref.shape, F32)

        for q in range(half):
            out_ref[q] += _mm_tn(act_ref[q], df_ref[...])

        @pl.when(t == nt - 1)
        def _():
            for cp in _sibling_swap_copies([src_ref], [dst_ref], send_sems, recv_sems):
                cp.wait()

    return pl.pallas_call(
        body,
        name="wgrad_down",
        grid=(nt,),
        out_shape=(jax.ShapeDtypeStruct((half, FF_CHUNK, D), F32), jax.ShapeDtypeStruct(swap_src.shape[1:], F32)),
        in_specs=[pl.BlockSpec((half, ts, FF_CHUNK), lambda t: (0, t, 0)), pl.BlockSpec((ts, D), lambda t: (t, 0)),
                  ANY_SPEC],
        out_specs=(pl.BlockSpec((half, FF_CHUNK, D), lambda t: (0, 0, 0)), ANY_SPEC),
        scratch_shapes=[pltpu.SemaphoreType.DMA((1, 4)), pltpu.SemaphoreType.DMA((1, 4))],
        compiler_params=pltpu.CompilerParams(dimension_semantics=("arbitrary",), vmem_limit_bytes=VMEM_LIMIT_V7X),
    )(act, df, swap_src)


def _mix_bwd(dx1, x, proj, mixed, mod, g_pre, g_post, w_in_t, sgn, w_sp, b_sp_t, w_pool, p_scale, w_out_b, ts, rs_srcs):
    t_len = x.shape[0]
    nt, nb = t_len // ts, ts // HEAD
    nr = len(rs_srcs)

    def body(*refs):
        (dx1_ref, x_ref, proj_ref, halo_ref, mixed_ref, mod_ref, g1_ref, g2_ref, win_ref, sgn_ref, ws_ref,
         bst_ref, wp_ref, ps_ref, wout_ref) = refs[:15]
        (gx_ref, dwin_ref, dwout_ref, dmod_ref, dg1_ref, dg2_ref, dsgn_ref, dws_ref, dbst_ref, dwp_ref,
         dps_ref) = refs[15 + nr:26 + nr]
        pbuf, dwsbuf, cat, dproj, dcat = refs[26 + 2 * nr:31 + 2 * nr]
        exchange = _ChipExchangeSteps(refs[15:15 + nr], refs[26 + nr:26 + 2 * nr], *refs[31 + 2 * nr:])
        i = pl.program_id(0)
        r = nt - 1 - i

        @pl.when(i == 0)
        def _():
            exchange.start()
            for ref in (dwin_ref, dwout_ref, dmod_ref, dg1_ref, dg2_ref, dsgn_ref, dws_ref, dbst_ref, dwp_ref, dps_ref):
                ref[...] = jnp.zeros(ref.shape, F32)
            dwsbuf[ts:ts + POOL_HALO, :] = jnp.zeros((POOL_HALO, B_WIDTH), F32)

        xv, dx1v, mixed = x_ref[...], dx1_ref[...], mixed_ref[...]
        sh, sc, gm = mod_ref[0:1, :], mod_ref[1:2, :], mod_ref[2:3, :]
        g1, g2 = g1_ref[...], g2_ref[...]
        rstd2 = _rstd(mixed)
        mh = mixed * rstd2
        dmod_ref[2:3, :] += _sum0(dx1v * (mh * g2))
        dr = dx1v * gm
        dg2_ref[...] += _sum0(dr * mh)
        dmh = dr * g2
        dmb = _bf(rstd2 * (dmh - mh * _rowmean(dmh * mh)))
        dcat[...] = _mm_nt(dmb, wout_ref[...])
        smask = _sgu_mask()
        for hd in range(N_HEAD):
            ucols = slice(hd * HEAD, (hd + 1) * HEAD)
            vcols = slice(A_WIDTH + hd * HEAD, A_WIDTH + (hd + 1) * HEAD)
            u, du_dp = _gelu_and_grad(proj_ref[:, ucols])
            v, dv_dp = _gelu_and_grad(proj_ref[:, vcols])
            rs = _rstd(v)
            vhat = v * rs
            gn = sgn_ref[hd:hd + 1, :]
            vn = _bf(vhat * gn)
            wm = _bf(jnp.where(smask, ws_ref[hd], 0.0))
            bias = bst_ref[:, hd:hd + 1]
            dzsum = jnp.zeros((HEAD, HEAD), F32)
            dwm = jnp.zeros((HEAD, HEAD), F32)
            dvn_parts = []
            for b in range(nb):
                rows = slice(b * HEAD, (b + 1) * HEAD)
                z = _mm(wm, vn[rows]) + bias
                da = dcat[rows, ucols]
                cat[rows, ucols] = _bf(u[rows] * z)
                dz = da * u[rows]
                dzsum = dzsum + dz
                dzb = _bf(dz)
                dwm = dwm + _mm_nt(dzb, vn[rows])
                dvn_parts.append(_mm_tn(wm, dzb))
                dproj[rows, ucols] = _bf((da * z) * du_dp[rows])
            dvn = jnp.concatenate(dvn_parts, axis=0)
            dsgn_ref[hd:hd + 1, :] += _sum0(dvn * vhat)
            dvh = dvn * gn
            dproj[:, vcols] = _bf((rs * (dvh - vhat * _rowmean(dvh * vhat))) * dv_dp)
            dws_ref[hd] += jnp.where(smask, dwm, 0.0)
            dbst_ref[:, hd:hd + 1] += jnp.sum(dzsum, axis=1, keepdims=True)
        keep = jnp.where(r > 0, 1.0, 0.0).astype(F32)
        pbuf[0:POOL_HALO, :] = halo_ref[...] * keep
        pbuf[POOL_HALO:POOL_HALO + ts, :] = proj_ref[:, 2 * A_WIDTH:]
        for g, w in enumerate(WINDOWS):
            cols = slice(g * HEAD, (g + 1) * HEAD)
            ccols = slice(A_WIDTH + g * HEAD, A_WIDTH + (g + 1) * HEAD)
            pcols = slice(2 * A_WIDTH + g * HEAD, 2 * A_WIDTH + (g + 1) * HEAD)
            wpg = _bf(wp_ref[g])
            psg = ps_ref[:, cols]
            ext = pbuf[:, cols]
            inv = _inv_count(r * ts, ts, w)
            pb = _bf(_window_sum(ext, w, True)[POOL_HALO:] * inv - ext[POOL_HALO:])
            yb = _mm(pb, wpg)
            dob = dcat[:, ccols]
            cat[:, ccols] = _bf(yb * psg)
            dps_ref[:, cols] += _sum0(dob * yb)
            dyb = _bf(dob * psg)
            dwp_ref[g] += _mm_tn(pb, dyb)
            dpooled = _mm_nt(dyb, wpg)
            dwsbuf[0:ts, cols] = dpooled * inv
            dproj[:, pcols] = _bf(_window_sum(dwsbuf[:, cols], w, False)[0:ts] - dpooled)
        dwsbuf[ts:ts + POOL_HALO, :] = dwsbuf[0:POOL_HALO, :]
        dpb = dproj[...]
        rstd1 = _rstd(xv)
        xh = xv * rstd1
        n1 = xh * g1
        dwin_ref[...] += _mm_tn(dpb, _bf(n1 * (1.0 + sc) + sh))
        dwout_ref[...] += _mm_tn(cat[...], dmb)
        dh = _mm(dpb, win_ref[...])
        dmod_ref[0:1, :] += _sum0(dh)
        dmod_ref[1:2, :] += _sum0(dh * n1)
        dn1 = dh * (1.0 + sc)
        dg1_ref[...] += _sum0(dn1 * xh)
        dxh = dn1 * g1
        gx_ref[...] = dx1v + rstd1 * (dxh - xh * _rowmean(dxh * xh))

        @pl.when(i == nt - 1)
        def _():
            exchange.finish()

    tile = lambda wid: pl.BlockSpec((ts, wid), lambda i: (nt - 1 - i, 0))
    halo = pl.BlockSpec((POOL_HALO, B_WIDTH),
                        lambda i: (jnp.maximum((nt - 1 - i) * (ts // POOL_HALO) - 1, 0), 2 * A_WIDTH // B_WIDTH))
    const = lambda *shape: pl.BlockSpec(shape, lambda i: (0,) * len(shape))
    resident = lambda *shape: pl.BlockSpec(shape, lambda i: (0,) * len(shape), pipeline_mode=pl.Buffered(1))
    outs = pl.pallas_call(
        body,
        name="mix_bwd",
        grid=(nt,),
        out_shape=(jax.ShapeDtypeStruct((t_len, D), F32), jax.ShapeDtypeStruct((IN_WIDTH, D), F32),
                   jax.ShapeDtypeStruct((D, D), F32), jax.ShapeDtypeStruct((3, D), F32),
                   jax.ShapeDtypeStruct((1, D), F32), jax.ShapeDtypeStruct((1, D), F32),
                   jax.ShapeDtypeStruct((N_HEAD, HEAD), F32), jax.ShapeDtypeStruct((N_HEAD, HEAD, HEAD), F32),
                   jax.ShapeDtypeStruct((HEAD, N_HEAD), F32), jax.ShapeDtypeStruct((N_HEAD, HEAD, HEAD), F32),
                   jax.ShapeDtypeStruct((1, B_WIDTH), F32))
        + tuple(jax.ShapeDtypeStruct((3, *s.shape[2:]), s.dtype) for s in rs_srcs),
        in_specs=[tile(D), tile(D), tile(IN_WIDTH), halo, tile(D)] + [VMEM_SPEC] * 10 + [ANY_SPEC] * nr,
        out_specs=(tile(D), resident(IN_WIDTH, D), resident(D, D), const(3, D), const(1, D), const(1, D),
                   const(N_HEAD, HEAD), const(N_HEAD, HEAD, HEAD), const(HEAD, N_HEAD), const(N_HEAD, HEAD, HEAD),
                   const(1, B_WIDTH)) + (ANY_SPEC,) * nr,
        scratch_shapes=[pltpu.VMEM((POOL_HALO + ts, B_WIDTH), F32), pltpu.VMEM((ts + POOL_HALO, B_WIDTH), F32),
                        pltpu.VMEM((ts, D), BF16), pltpu.VMEM((ts, IN_WIDTH), BF16), pltpu.VMEM((ts, D), F32),
                        pltpu.SemaphoreType.DMA((3 * nr,)), pltpu.SemaphoreType.DMA((3 * nr,))],
        compiler_params=pltpu.CompilerParams(dimension_semantics=("arbitrary",), vmem_limit_bytes=VMEM_LIMIT_V7X),
    )(dx1, x, proj, proj, mixed, mod, g_pre, g_post, w_in_t, sgn, w_sp, b_sp_t, w_pool, p_scale, w_out_b, *rs_srcs)
    return outs[:11], outs[11:]


def _pair_add(name, coords, grid, specs_a, specs_b, out_specs, out_shapes, a_arrays, b_arrays):
    n = len(a_arrays)

    def body(co_ref, *refs):
        for k in range(n):
            total = refs[k][...] + refs[n + k][...]
            refs[2 * n + k][...] = total
            refs[3 * n + k][...] = _bf(total)

    outs = pl.pallas_call(
        body,
        name=name,
        grid_spec=pltpu.PrefetchScalarGridSpec(num_scalar_prefetch=1, grid=grid, in_specs=specs_a + specs_b,
                                               out_specs=out_specs * 2),
        out_shape=tuple(jax.ShapeDtypeStruct(s, dt) for dt in (F32, BF16) for s in out_shapes),
        compiler_params=pltpu.CompilerParams(dimension_semantics=("arbitrary",) * len(grid),
                                             vmem_limit_bytes=VMEM_LIMIT_V7X),
    )(coords, *a_arrays, *b_arrays)
    return list(outs[:n]), list(outs[n:])


def _final_add_adamw(coords, s1, r, ws, ms, vs, n_split=4):
    n = len(s1)

    def body(co_ref, *refs):
        for k in range(n):
            s_ref, r_ref, w_ref, m_ref, v_ref = (refs[q * n + k] for q in range(5))
            g_ref, d_ref, nm_ref, nv_ref = (refs[(5 + q) * n + k] for q in range(4))
            g = ((s_ref[...] + r_ref[0].astype(F32)) + r_ref[1].astype(F32)) + r_ref[2].astype(F32)
            g_ref[...] = g
            delta, m, v = _adamw(w_ref[...], g, m_ref[...], v_ref[...])
            d_ref[...] = delta
            nm_ref[...] = m
            nv_ref[...] = v

    def shard_spec(a):
        rows, cols = a.shape
        return pl.BlockSpec((rows // n_split, cols), lambda i, co: (i, 0))

    def mine_spec(a):
        rows, cols = a.shape[2:]
        return pl.BlockSpec((None, None, rows // n_split, cols), lambda i, co: (co[0], co[1], i, 0))

    def recv_spec(a):
        rows, cols = a.shape[1:]
        return pl.BlockSpec((3, rows // n_split, cols), lambda i, co: (0, i, 0))

    in_specs = ([mine_spec(a) for a in s1] + [recv_spec(a) for a in r] + [shard_spec(a) for a in ws] * 3)
    out_specs = [shard_spec(a) for a in ws] * 4
    outs = pl.pallas_call(
        body,
        name="grad_final_adamw",
        grid_spec=pltpu.PrefetchScalarGridSpec(num_scalar_prefetch=1, grid=(n_split,), in_specs=in_specs,
                                               out_specs=out_specs),
        out_shape=tuple(jax.ShapeDtypeStruct(a.shape, F32) for a in ws) * 4,
        compiler_params=pltpu.CompilerParams(dimension_semantics=("arbitrary",), vmem_limit_bytes=VMEM_LIMIT_V7X),
    )(coords, *s1, *r, *ws, *ms, *vs)
    return [tuple(outs[q * n + k] for q in range(4)) for k in range(n)]


def _sibling_swap(tag, g5):
    n = len(g5)

    def body(*refs):
        copies = _sibling_swap_copies(refs[:n], refs[n:2 * n], *refs[2 * n:])
        for cp in copies:
            cp.start()
        for cp in copies:
            cp.wait()

    return list(pl.pallas_call(
        body,
        name="grad_swap_core_" + tag,
        out_shape=tuple(jax.ShapeDtypeStruct(g.shape[1:], F32) for g in g5),
        in_specs=[ANY_SPEC] * n,
        out_specs=(ANY_SPEC,) * n,
        scratch_shapes=[pltpu.SemaphoreType.DMA((n, 4)), pltpu.SemaphoreType.DMA((n, 4))],
    )(*g5))


def _sibling_add(tag, g5, r1, coords, n_split=4):
    shapes = [g.shape[3:] for g in g5]
    spec_g = [pl.BlockSpec((None, None, None, s[0] // n_split, s[1]), lambda i, j, k, co: (i, j, co[2], k, 0))
              for s in shapes]
    spec_r = [pl.BlockSpec((None, None, s[0] // n_split, s[1]), lambda i, j, k, co: (i, j, k, 0)) for s in shapes]
    return _pair_add("grad_add_core_" + tag, coords, (2, 2, n_split), spec_g, spec_r, spec_r,
                     [(2, 2, *s) for s in shapes], g5, r1)


def _tail_exchange(big, partials, pick_mine, dmod3):
    n, nb = len(partials), len(big)
    big_shapes = [g.shape[1:] for g in big]
    big5 = [g.reshape(2, 2, 2, *s) for g, s in zip(big, big_shapes)]
    flips = _ChipExchangeSteps.FLIPS

    def body(*refs):
        g5, p_in, dm_ref = refs[:nb], refs[nb:nb + n], refs[nb + n]
        outs = refs[nb + n + 1:2 * nb + 2 * n + 2]
        g_out, sums, dm2d = outs[:nb], outs[nb:nb + n], outs[nb + n]
        scratch = refs[2 * nb + 2 * n + 2:]
        s1, stage, chip_recv = scratch[:nb], scratch[nb:2 * nb], scratch[2 * nb:3 * nb]
        acc, rbuf = scratch[3 * nb:3 * nb + n], scratch[3 * nb + n:3 * nb + 2 * n]
        (dm_recv, send_sems, recv_sems, dm_send_sems, dm_recv_sems, sib_send, sib_recv, chip_send,
         chip_recv_sems) = scratch[3 * nb + 2 * n:]
        x, y, c = _coords()
        me = 4 * x + 2 * y + c
        sibling = (x, y, 1 - c)
        dm_copies = [
            pltpu.make_async_remote_copy(dm_ref.at[me ^ k], dm_recv.at[k], dm_send_sems.at[k], dm_recv_sems.at[k],
                                         device_id=_peer(k), device_id_type=MESH)
            for k in range(1, N_DEV)
        ]
        for cp in dm_copies:
            cp.start()
        sib_copies = _sibling_swap_copies(g5, s1, sib_send, sib_recv)
        for cp in sib_copies:
            cp.start()
        for a in range(n):
            acc[a][...] = p_in[a][...]

        def small_phase(ph, peer):
            copies = [
                pltpu.make_async_remote_copy(acc[a], rbuf[a].at[ph], send_sems.at[ph, a], recv_sems.at[ph, a],
                                             device_id=peer, device_id_type=MESH)
                for a in range(n)
            ]
            for cp in copies:
                cp.start()
            for cp in copies:
                cp.wait()
            for a in range(n):
                acc[a][...] = acc[a][...] + rbuf[a][ph]

        small_phase(0, sibling)
        for cp in sib_copies:
            cp.wait()
        for a in range(nb):
            for xs in range(2):
                for ys in range(2):
                    total = g5[a][xs, ys, c] + s1[a][xs, ys]
                    s1[a][xs, ys] = total
                    stage[a][xs, ys] = _bf(total)
        chip_copies = [
            pltpu.make_async_remote_copy(stage[a].at[x ^ fx, y ^ fy], chip_recv[a].at[j], chip_send.at[a, j],
                                         chip_recv_sems.at[a, j], device_id=(x ^ fx, y ^ fy, c), device_id_type=MESH)
            for a in range(nb) for j, (fx, fy) in enumerate(flips)
        ]
        for cp in chip_copies:
            cp.start()
        small_phase(1, (1 - x, y, c))
        small_phase(2, (x, 1 - y, c))
        for a in range(n):
            sums[a][...] = acc[a][me] if pick_mine[a] else acc[a][...]
        dm2d[...] = jnp.zeros(dm2d.shape, F32)
        dm2d[0:1, :] = dm_ref[me]
        for cp in dm_copies:
            cp.wait()
        for k in range(1, N_DEV):
            dm2d[k:k + 1, :] = dm_recv[k]
        for cp in chip_copies:
            cp.wait()
        for a in range(nb):
            g_out[a][...] = ((s1[a][x, y] + chip_recv[a][0].astype(F32)) + chip_recv[a][1].astype(F32)) \
                + chip_recv[a][2].astype(F32)

    out_shapes = tuple(jax.ShapeDtypeStruct(s, F32) for s in big_shapes) + tuple(
        jax.ShapeDtypeStruct(p.shape[1:] if pk else p.shape, F32) for p, pk in zip(partials, pick_mine))
    outs = pl.pallas_call(
        body,
        name="tail_exchange",
        out_shape=out_shapes + (jax.ShapeDtypeStruct((2 * N_DEV, MOD_COLS), F32),),
        in_specs=[VMEM_SPEC] * (nb + n + 1),
        out_specs=(VMEM_SPEC,) * (nb + n + 1),
        scratch_shapes=[pltpu.VMEM((2, 2, *s), F32) for s in big_shapes]
        + [pltpu.VMEM((2, 2, *s), BF16) for s in big_shapes]
        + [pltpu.VMEM((3, *s), BF16) for s in big_shapes]
        + [pltpu.VMEM(p.shape, F32) for p in partials]
        + [pltpu.VMEM((3, *p.shape), F32) for p in partials]
        + [pltpu.VMEM((N_DEV, 1, MOD_COLS), F32), pltpu.SemaphoreType.DMA((3, n)), pltpu.SemaphoreType.DMA((3, n)),
           pltpu.SemaphoreType.DMA((N_DEV,)), pltpu.SemaphoreType.DMA((N_DEV,)),
           pltpu.SemaphoreType.DMA((nb, 4)), pltpu.SemaphoreType.DMA((nb, 4)),
           pltpu.SemaphoreType.DMA((nb, 3)), pltpu.SemaphoreType.DMA((nb, 3))],
        compiler_params=pltpu.CompilerParams(vmem_limit_bytes=VMEM_LIMIT_V7X),
    )(*big5, *partials, dmod3)
    return list(outs[:nb]), list(outs[nb:nb + n]), outs[nb + n]


def _small_update(grads, ws, ms, vs, scx, dm2d, w_ada, m_ada, v_ada, loss_lanes):
    n = len(grads)

    def body(*refs):
        g_in, w_in, m_in, v_in = (refs[q * n:(q + 1) * n] for q in range(4))
        scx_ref, dm_ref, wa_ref, ma_ref, va_ref, ll_ref = refs[4 * n:4 * n + 6]
        outs = refs[4 * n + 6:]
        g_out, d_out, nm_out, nv_out = (outs[q * (n + 1):(q + 1) * (n + 1)] for q in range(4))
        loss_ref = outs[4 * (n + 1)]
        for a in range(n + 1):
            if a < n:
                g, w, m, v = g_in[a][...], w_in[a][...], m_in[a][...], v_in[a][...]
            else:
                g = _mm_tn(_bf(scx_ref[...]), _bf(dm_ref[...]))
                w, m, v = wa_ref[...], ma_ref[...], va_ref[...]
            g_out[a][...] = g
            delta, m, v = _adamw(w, g, m, v)
            d_out[a][...] = delta
            nm_out[a][...] = m
            nv_out[a][...] = v
        loss_ref[...] = jnp.sum(ll_ref[...], axis=1, keepdims=True) * (0.5 / D)

    w_shapes = tuple(jax.ShapeDtypeStruct(w.shape, F32) for w in list(ws) + [w_ada])
    outs = pl.pallas_call(
        body,
        name="small_update",
        out_shape=w_shapes * 4 + (jax.ShapeDtypeStruct((1, 1), F32),),
        in_specs=[VMEM_SPEC] * (4 * n + 6),
        out_specs=(VMEM_SPEC,) * (4 * (n + 1) + 1),
        compiler_params=pltpu.CompilerParams(vmem_limit_bytes=VMEM_LIMIT_V7X),
    )(*grads, *ws, *ms, *vs, scx, dm2d, w_ada, m_ada, v_ada, loss_lanes)
    return [tuple(outs[q * (n + 1) + k] for q in range(4)) for k in range(n + 1)], outs[4 * (n + 1)]


def kernel(x, c, w_ada, b_ada, pre_mix_g, post_mix_g, w_in, sgu_norm_g, w_spatial, b_spatial, w_pool, pool_scale, w_out, pre_ffn_g, post_ffn_g, w_up, conv_w, conv_b, w_down, loss_target, m_w_ada, m_b_ada, m_pre_mix_g, m_post_mix_g, m_w_in, m_sgu_norm_g, m_w_spatial, m_b_spatial, m_w_pool, m_pool_scale, m_w_out, m_pre_ffn_g, m_post_ffn_g, m_w_up, m_conv_w, m_conv_b, m_w_down, v_w_ada, v_b_ada, v_pre_mix_g, v_post_mix_g, v_w_in, v_sgu_norm_g, v_w_spatial, v_b_spatial, v_w_pool, v_pool_scale, v_w_out, v_pre_ffn_g, v_post_ffn_g, v_w_up, v_conv_w, v_conv_b, v_w_down):
    t_len = x.shape[1]
    ts = min(256, t_len)
    ts_mix = min(512, t_len)
    ts_w = min(1024, t_len)
    coords = jnp.stack([lax.axis_index("x"), lax.axis_index("y"), lax.axis_index("c")]).astype(jnp.int32)

    w_in_t, w_up_t = w_in[0].T, w_up[0].T
    mod3, scx, (g_in, g_out) = _prologue(c, w_ada[0], b_ada.reshape(N_DEV, 1, MOD_COLS), [w_in_t, w_out[0]],
                                         [BF16, BF16])
    mod = mod3.reshape(N_MOD, D)
    w_in_tb = g_in.reshape(IN_WIDTH, D)
    w_out_b = g_out.reshape(D, D)
    conv_b8 = conv_b.reshape(N_DEV, FF_CHUNK)
    b_sp_t = b_spatial[0].T

    x2d, tgt = x[0], loss_target[0]
    (x1, proj, mixed), (g_up, g_down, g_cw) = _mix_fwd(
        x2d, mod, pre_mix_g, post_mix_g, w_in_tb, sgu_norm_g[0], w_spatial[0], b_sp_t, w_pool[0], pool_scale, w_out_b,
        ts_mix, [w_up_t, w_down[0], conv_w[0]], [w_up.shape[1:], w_down.shape[1:], conv_w.shape[1:]], [BF16, BF16, F32])
    w_down_b = g_down.reshape(FF, D)
    up, f, dx2, loss_lanes = _ffn_fwd(x1, tgt, mod, pre_ffn_g, post_ffn_g, g_up, g_cw, conv_b8, w_down_b, ts)

    (dx1, dup, act, df, h2, dmod_f, d_pre_ffn, d_post_ffn, d_cb8, d_cw8) = _ffn_bwd(
        dx2, f, x1, up, mod, pre_ffn_g, post_ffn_g, g_up, g_cw, conv_b8, w_down_b, ts)
    gw_up = _wgrad_up(h2, dup, ts_w).reshape(2, 2, 2, FF_CHUNK, D)
    gw_down, r1_up = _wgrad_down(act, df, ts_w, gw_up)
    gw_down = gw_down.reshape(2, 2, 2, FF // N_DEV, D)
    s1_ffn, s1_ffn_b = _sibling_add("ffn", [gw_up, gw_down], [r1_up] + _sibling_swap("down", [gw_down]), coords)
    ((grad_x, gw_in, gw_out, dmod_m, d_pre_mix, d_post_mix, d_sgn, d_wsp, d_bsp_t, d_wpool, d_ps), r_ffn) = _mix_bwd(
        dx1, x2d, proj, mixed, mod, pre_mix_g, post_mix_g, w_in_tb, sgu_norm_g[0], w_spatial[0], b_sp_t,
        w_pool[0], pool_scale, w_out_b, ts_mix, s1_ffn_b)
    gw_in = gw_in.reshape(N_DEV, IN_WIDTH // N_DEV, D)
    gw_out = gw_out.reshape(N_DEV, D // N_DEV, D)

    big = _final_add_adamw(coords, s1_ffn, list(r_ffn), [w_up_t, w_down[0]], [m_w_up[0].T, m_w_down[0]],
                           [v_w_up[0].T, v_w_down[0]])
    r_up, r_down = tuple(a.T[None] for a in big[0]), tuple(a[None] for a in big[1])

    dmod = jnp.concatenate([dmod_m, dmod_f], axis=0)
    names = ["b_ada", "pre_mix_g", "post_mix_g", "sgu_norm_g", "w_spatial", "b_spatial", "w_pool", "pool_scale",
             "pre_ffn_g", "post_ffn_g", "conv_w", "conv_b"]
    partials = [dmod.reshape(1, N_MOD * D), d_pre_mix, d_post_mix, d_sgn, d_wsp, d_bsp_t.T, d_wpool, d_ps,
                d_pre_ffn, d_post_ffn, d_cw8, d_cb8.reshape(1, 2 * FF), loss_lanes]
    small_w = [b_ada, pre_mix_g, post_mix_g, sgu_norm_g[0], w_spatial[0], b_spatial[0], w_pool[0], pool_scale,
               pre_ffn_g, post_ffn_g, conv_w[0], conv_b]
    small_m = [m_b_ada, m_pre_mix_g, m_post_mix_g, m_sgu_norm_g[0], m_w_spatial[0], m_b_spatial[0], m_w_pool[0],
               m_pool_scale, m_pre_ffn_g, m_post_ffn_g, m_conv_w[0], m_conv_b]
    small_v = [v_b_ada, v_pre_mix_g, v_post_mix_g, v_sgu_norm_g[0], v_w_spatial[0], v_b_spatial[0], v_w_pool[0],
               v_pool_scale, v_pre_ffn_g, v_post_ffn_g, v_conv_w[0], v_conv_b]
    g_mix, sums, dm2d = _tail_exchange([gw_in, gw_out], partials, [nm == "conv_w" for nm in names] + [False],
                                       dmod.reshape(N_DEV, 1, MOD_COLS))
    small, loss11 = _small_update(
        sums[:-1] + g_mix, small_w + [w_in_t, w_out[0]], small_m + [m_w_in[0].T, m_w_out[0]],
        small_v + [v_w_in[0].T, v_w_out[0]], scx, dm2d, w_ada[0], m_w_ada[0], v_w_ada[0], sums[-1])
    loss = loss11.reshape(())
    lead = {"sgu_norm_g", "w_spatial", "b_spatial", "w_pool", "conv_w", "w_in", "w_out", "w_ada"}
    res = {nm: tuple((a.T if nm == "w_in" else a)[None] if nm in lead else a for a in four)
           for nm, four in zip(names + ["w_in", "w_out", "w_ada"], small)}
    res.update(w_up=r_up, w_down=r_down)

    order = ["w_ada", "b_ada", "pre_mix_g", "post_mix_g", "w_in", "sgu_norm_g", "w_spatial", "b_spatial", "w_pool",
             "pool_scale", "w_out", "pre_ffn_g", "post_ffn_g", "w_up", "conv_w", "conv_b", "w_down"]
    return (loss, grad_x[None], *[res[nm][0] for nm in order], *[res[nm][1] for nm in order],
            *[res[nm][2] for nm in order], *[res[nm][3] for nm in order])
```

```python
import functools
import math

import jax
import jax.numpy as jnp
from jax import lax
from jax.experimental import pallas as pl
from jax.experimental.pallas import tpu as pltpu

F32 = jnp.float32
BF16 = jnp.bfloat16
MESH = pl.DeviceIdType.MESH

EPS = 1e-6
D = 1024
HEAD = 128
N_HEAD = 4
A_WIDTH = 512
B_WIDTH = 512
IN_WIDTH = 1536
WINDOWS = (2, 4, 8, 16)
CHUNK = 64
FF = 2816
N_DEV = 8
FF_CHUNK = 704
N_MOD = 6
MOD_COLS = 768

ADAM_LR = 0.001
ADAM_B1 = 0.9
ADAM_B2 = 0.999
ADAM_EPS = 1e-08
ADAM_WD = 0.01
ADAM_STEP = 10

VMEM_LIMIT_V7X = 62 * 1024 * 1024
HALO = 8
POOL_HALO = 16

VMEM_SPEC = pl.BlockSpec(memory_space=pltpu.VMEM)
ANY_SPEC = pl.BlockSpec(memory_space=pl.ANY)


def _bf(x):
    return x.astype(BF16)


def _mm(a, b):
    return jnp.dot(a, b, preferred_element_type=F32)


def _mm_nt(a, b):
    return lax.dot_general(a, b, (((1,), (1,)), ((), ())), preferred_element_type=F32)


def _mm_tn(a, b):
    return lax.dot_general(a, b, (((0,), (0,)), ((), ())), preferred_element_type=F32)


def _rstd(x):
    return lax.rsqrt(jnp.mean(x * x, axis=-1, keepdims=True) + EPS)


def _sum0(x):
    return jnp.sum(x, axis=0, keepdims=True)


def _rowmean(x):
    return jnp.mean(x, axis=-1, keepdims=True)


_GELU_K = math.sqrt(2.0 / math.pi)


def _gelu_and_grad(x):
    x2 = x * x
    th = jnp.tanh(_GELU_K * (x + 0.044715 * (x * x2)))
    cdf = 0.5 * (1.0 + th)
    grad = cdf + 0.5 * x * (1.0 - th * th) * (_GELU_K * (1.0 + 3.0 * 0.044715 * x2))
    return x * cdf, grad


def _gelu(x):
    return x * (0.5 * (1.0 + jnp.tanh(_GELU_K * (x + 0.044715 * (x * x * x)))))


def _sigmoid(x):
    return 0.5 * jnp.tanh(0.5 * x) + 0.5


def _sgu_mask():
    ri = lax.broadcasted_iota(jnp.int32, (HEAD, HEAD), 0)
    ci = lax.broadcasted_iota(jnp.int32, (HEAD, HEAD), 1)
    return (ci // CHUNK) <= (ri // CHUNK)


def _window_sum(ext, w, trailing):
    n = ext.shape[0]
    s, k = ext, 1
    while k < w:
        s = s + pltpu.roll(s, k if trailing else n - k, 0)
        k *= 2
    return s


def _inv_count(row0, n, w):
    t = row0 + lax.broadcasted_iota(jnp.int32, (n, 1), 0)
    return 1.0 / jnp.minimum(t + 1, w).astype(F32)


def _shift_down(v, before, k):
    rows = lax.broadcasted_iota(jnp.int32, before.shape, 0)
    r = pltpu.roll(v, k, 0)
    top = jnp.where(rows < k, pltpu.roll(before, k, 0), r[0:HALO])
    return jnp.concatenate([top, r[HALO:]], axis=0)


def _shift_up(v, after, k):
    n = v.shape[0]
    rows = lax.broadcasted_iota(jnp.int32, after.shape, 0)
    r = pltpu.roll(v, n - k, 0)
    bottom = jnp.where(rows >= HALO - k, pltpu.roll(after, HALO - k, 0), r[n - HALO:])
    return jnp.concatenate([r[:n - HALO], bottom], axis=0)


def _adamw(w, g, m, v):
    m = ADAM_B1 * m + (1.0 - ADAM_B1) * g
    v = ADAM_B2 * v + (1.0 - ADAM_B2) * (g * g)
    m_hat = m / (1.0 - ADAM_B1 ** ADAM_STEP)
    v_hat = v / (1.0 - ADAM_B2 ** ADAM_STEP)
    delta = -ADAM_LR * (m_hat / (jnp.sqrt(v_hat) + ADAM_EPS) + ADAM_WD * w)
    return delta, m, v


def _coords():
    return lax.axis_index("x"), lax.axis_index("y"), lax.axis_index("c")


def _peer(k):
    x, y, c = _coords()
    return (x ^ ((k >> 2) & 1), y ^ ((k >> 1) & 1), c ^ (k & 1))


def _my_index():
    x, y, c = _coords()
    return 4 * x + 2 * y + c


def _adaln_modulation(c_ref, w_ref, b_ref, mod_ref, scx_ref, scbuf, stage, recv, send_sems, recv_sems):
    me = _my_index()
    cv = c_ref[...]
    scbuf[0] = cv * _sigmoid(cv)
    first = [
        pltpu.make_async_remote_copy(scbuf.at[0], scbuf.at[k], send_sems.at[0, k], recv_sems.at[0, k],
                                     device_id=_peer(k), device_id_type=MESH)
        for k in range(1, N_DEV)
    ]
    for cp in first:
        cp.start()
    for cp in first:
        cp.wait()
    scx_ref[...] = jnp.zeros(scx_ref.shape, F32)
    for k in range(N_DEV):
        scx_ref[k:k + 1, :] = scbuf[k]
    prod = _mm(_bf(scx_ref[...]), _bf(w_ref[...]))
    for k in range(N_DEV):
        stage[k] = prod[k:k + 1, :] + b_ref[me]
    second = [
        pltpu.make_async_remote_copy(stage.at[k], recv.at[k], send_sems.at[1, k], recv_sems.at[1, k],
                                     device_id=_peer(k), device_id_type=MESH)
        for k in range(1, N_DEV)
    ]
    for cp in second:
        cp.start()
    mod_ref[me] = stage[0]
    for cp in second:
        cp.wait()
    for k in range(1, N_DEV):
        mod_ref[me ^ k] = recv[k]


class _GatherSteps:
    def __init__(self, ins, outs, stages, send_sems, recv_sems, local_sems):
        self.ins, self.outs, self.stages = ins, outs, stages
        self.send_sems, self.recv_sems, self.local_sems = send_sems, recv_sems, local_sems
        x, y, c = _coords()
        self.c = c
        self.me, self.sibling = (x, y, c), (x, y, 1 - c)
        self.chips = [(1 - x, y), (x, 1 - y), (1 - x, 1 - y)]

    def _copy(self, a, k, block, to, from_stage=False):
        dst = self.outs[a].at[4 * block[0] + 2 * block[1] + block[2]]
        return pltpu.make_async_remote_copy(self.stages[a] if from_stage else dst, dst, self.send_sems.at[a, k],
                                            self.recv_sems.at[a, k], device_id=to, device_id_type=MESH)

    def _local(self, a):
        me = self.me
        return pltpu.make_async_copy(self.stages[a], self.outs[a].at[4 * me[0] + 2 * me[1] + me[2]],
                                     self.local_sems.at[a])

    def _first(self, a):
        cps = [self._copy(a, 0, self.me, self.sibling, from_stage=True)]
        return cps + [self._copy(a, 1 + j, self.me, (*chip, self.c), from_stage=True)
                      for j, chip in enumerate(self.chips)]

    def _passed(self, a, j):
        return self._copy(a, 4 + j, (*self.chips[j], self.c), self.sibling)

    def start(self):
        for a in range(len(self.ins)):
            block = self.ins[a][...]
            if block.shape != self.stages[a].shape:
                block = block.T
            self.stages[a][...] = block.astype(self.stages[a].dtype)
            self._local(a).start()
            for cp in self._first(a):
                cp.start()

    def forward(self):
        for a in range(len(self.ins)):
            for j, chip in enumerate(self.chips):
                self._copy(a, 1 + j, (*chip, self.c), self.me).wait_recv()
                self._passed(a, j).start()

    def finish(self):
        for a in range(len(self.ins)):
            self._copy(a, 0, self.sibling, self.me).wait_recv()
            for j, chip in enumerate(self.chips):
                self._copy(a, 4 + j, (*chip, 1 - self.c), self.me).wait_recv()
            for cp in self._first(a) + [self._passed(a, j) for j in range(3)]:
                cp.wait_send()
            self._local(a).wait()


def _gather_scratch(shapes, out_dtypes):
    n = len(shapes)
    return ([pltpu.VMEM(s, dt) for s, dt in zip(shapes, out_dtypes)]
            + [pltpu.SemaphoreType.DMA((n, 7)), pltpu.SemaphoreType.DMA((n, 7)), pltpu.SemaphoreType.DMA((n,))])


def _gather_out_shapes(shapes, out_dtypes):
    return tuple(jax.ShapeDtypeStruct((N_DEV, *s), dt) for s, dt in zip(shapes, out_dtypes))


def _prologue(c_row, w_ada, b_ada3, shards, out_dtypes):
    n = len(shards)

    def body(*refs):
        c_ref, w_ref, b_ref = refs[:3]
        mod_ref, scx_ref = refs[3 + n:5 + n]
        gather = _GatherSteps(refs[3:3 + n], refs[5 + n:5 + 2 * n], refs[5 + 2 * n:5 + 3 * n],
                              *refs[5 + 3 * n:8 + 3 * n])
        gather.start()
        _adaln_modulation(c_ref, w_ref, b_ref, mod_ref, scx_ref, *refs[8 + 3 * n:])
        gather.forward()
        gather.finish()

    outs = pl.pallas_call(
        body,
        name="prologue",
        out_shape=(jax.ShapeDtypeStruct((N_DEV, 1, MOD_COLS), F32), jax.ShapeDtypeStruct((2 * N_DEV, D), F32))
        + _gather_out_shapes([s.shape for s in shards], out_dtypes),
        in_specs=[VMEM_SPEC] * (3 + n),
        out_specs=(VMEM_SPEC, VMEM_SPEC) + (ANY_SPEC,) * n,
        scratch_shapes=_gather_scratch([s.shape for s in shards], out_dtypes) + [
            pltpu.VMEM((N_DEV, 1, D), F32),
            pltpu.VMEM((N_DEV, 1, MOD_COLS), F32),
            pltpu.VMEM((N_DEV, 1, MOD_COLS), F32),
            pltpu.SemaphoreType.DMA((2, N_DEV)),
            pltpu.SemaphoreType.DMA((2, N_DEV)),
        ],
        compiler_params=pltpu.CompilerParams(vmem_limit_bytes=VMEM_LIMIT_V7X),
    )(c_row, w_ada, b_ada3, *shards)
    return outs[0], outs[1], outs[2:]


class _ChipExchangeSteps:
    FLIPS = ((1, 0), (0, 1), (1, 1))

    def __init__(self, srcs, dsts, send_sems, recv_sems):
        self.srcs, self.dsts, self.send_sems, self.recv_sems = srcs, dsts, send_sems, recv_sems

    def _copies(self):
        x, y, c = _coords()
        out = []
        for a in range(len(self.srcs)):
            for j, (fx, fy) in enumerate(self.FLIPS):
                k = 3 * a + j
                out.append(pltpu.make_async_remote_copy(
                    self.srcs[a].at[x ^ fx, y ^ fy], self.dsts[a].at[j], self.send_sems.at[k], self.recv_sems.at[k],
                    device_id=(x ^ fx, y ^ fy, c), device_id_type=MESH))
        return out

    def start(self):
        for cp in self._copies():
            cp.start()

    def finish(self):
        for cp in self._copies():
            cp.wait()


def _mix_fwd(x, mod, g_pre, g_post, w_in_t, sgn, w_sp, b_sp_t, w_pool, p_scale, w_out_b, ts, shards, shard_shapes,
             shard_dtypes):
    t_len = x.shape[0]
    nt, nb = t_len // ts, ts // HEAD
    ns = len(shards)

    def body(*refs):
        (x_ref, mod_ref, g1_ref, g2_ref, win_ref, sgn_ref, ws_ref, bst_ref, wp_ref, ps_ref, wout_ref) = refs[:11]
        x1_ref, proj_ref, mixed_ref = refs[11 + ns:14 + ns]
        pbuf, cat = refs[14 + 2 * ns:16 + 2 * ns]
        gather = _GatherSteps(refs[11:11 + ns], refs[14 + ns:14 + 2 * ns], refs[16 + 2 * ns:16 + 3 * ns],
                              *refs[16 + 3 * ns:])
        i = pl.program_id(0)

        @pl.when(i == 0)
        def _():
            pbuf[0:POOL_HALO, :] = jnp.zeros((POOL_HALO, B_WIDTH), F32)
            gather.start()

        @pl.when(i == (3 * nt) // 4)
        def _():
            gather.forward()

        xv = x_ref[...]
        sh, sc, gm = mod_ref[0:1, :], mod_ref[1:2, :], mod_ref[2:3, :]
        h = (xv * _rstd(xv) * g1_ref[...]) * (1.0 + sc) + sh
        proj_ref[...] = _mm_nt(_bf(h), win_ref[...])
        pbuf[POOL_HALO:POOL_HALO + ts, :] = proj_ref[:, 2 * A_WIDTH:]
        smask = _sgu_mask()
        for hd in range(N_HEAD):
            u = _gelu(proj_ref[:, hd * HEAD:(hd + 1) * HEAD])
            v = _gelu(proj_ref[:, A_WIDTH + hd * HEAD:A_WIDTH + (hd + 1) * HEAD])
            vn = _bf(v * _rstd(v) * sgn_ref[hd:hd + 1, :])
            wm = _bf(jnp.where(smask, ws_ref[hd], 0.0))
            bias = bst_ref[:, hd:hd + 1]
            for b in range(nb):
                rows = slice(b * HEAD, (b + 1) * HEAD)
                z = _mm(wm, vn[rows]) + bias
                cat[rows, hd * HEAD:(hd + 1) * HEAD] = _bf(u[rows] * z)
        for g, w in enumerate(WINDOWS):
            cols = slice(g * HEAD, (g + 1) * HEAD)
            ext = pbuf[:, cols]
            pooled = _window_sum(ext, w, True)[POOL_HALO:] * _inv_count(i * ts, ts, w) - ext[POOL_HALO:]
            cat[:, A_WIDTH + g * HEAD:A_WIDTH + (g + 1) * HEAD] = _bf(_mm(_bf(pooled), _bf(wp_ref[g])) * ps_ref[:, cols])
        pbuf[0:POOL_HALO, :] = pbuf[ts:ts + POOL_HALO, :]
        mixed = _mm(cat[...], wout_ref[...])
        mixed_ref[...] = mixed
        x1_ref[...] = xv + gm * (mixed * _rstd(mixed) * g2_ref[...])

        @pl.when(i == nt - 1)
        def _():
            gather.finish()

    tile = lambda wid: pl.BlockSpec((ts, wid), lambda i: (i, 0))
    outs = pl.pallas_call(
        body,
        name="mix_fwd",
        grid=(nt,),
        out_shape=(jax.ShapeDtypeStruct((t_len, D), F32), jax.ShapeDtypeStruct((t_len, IN_WIDTH), F32),
                   jax.ShapeDtypeStruct((t_len, D), F32)) + _gather_out_shapes(shard_shapes, shard_dtypes),
        in_specs=[tile(D)] + [VMEM_SPEC] * (10 + ns),
        out_specs=(tile(D), tile(IN_WIDTH), tile(D)) + (ANY_SPEC,) * ns,
        scratch_shapes=[pltpu.VMEM((POOL_HALO + ts, B_WIDTH), F32), pltpu.VMEM((ts, D), BF16)]
        + _gather_scratch(shard_shapes, shard_dtypes),
        compiler_params=pltpu.CompilerParams(dimension_semantics=("arbitrary",), vmem_limit_bytes=VMEM_LIMIT_V7X),
    )(x, mod, g_pre, g_post, w_in_t, sgn, w_sp, b_sp_t, w_pool, p_scale, w_out_b, *shards)
    return outs[:3], outs[3:]


def _ffn_fwd(x1, target, mod, g_pre, g_post, w_up_b, conv_w8, conv_b8, w_down_b, ts):
    t_len = x1.shape[0]
    nt = t_len // ts

    def body(x1_ref, tgt_ref, mod_ref, g3_ref, g4_ref, wup_ref, cw_ref, cb_ref, wdown_ref,
             up_ref, f_ref, dx2_ref, loss_ref, ucarry):
        i = pl.program_id(0)

        @pl.when(i == 0)
        def _():
            ucarry[...] = jnp.zeros(ucarry.shape, F32)
            loss_ref[...] = jnp.zeros(loss_ref.shape, F32)

        x1v = x1_ref[...]
        sh, sc, gf = mod_ref[3:4, :], mod_ref[4:5, :], mod_ref[5:6, :]
        h2 = _bf((x1v * _rstd(x1v) * g3_ref[...]) * (1.0 + sc) + sh)
        half = N_DEV // 2

        def up_pair(j):
            return [_mm(h2, wup_ref[jj]) for jj in (j, j + half)]

        f = jnp.zeros((ts, D), F32)
        ups = up_pair(0)
        for j in range(half):
            nxt = up_pair(j + 1) if j + 1 < half else None
            ys = []
            for up, jj in zip(ups, (j, j + half)):
                up_ref[jj] = up
                before = ucarry[jj]
                ucarry[jj] = up[ts - HALO:, :]
                cw = cw_ref[jj]
                ys.append(cb_ref[jj:jj + 1, :] + _shift_down(up, before, 2) * cw[0:1, :]
                          + _shift_down(up, before, 1) * cw[1:2, :] + up * cw[2:3, :])
            gate, val = ys
            act = gate * _sigmoid(gate) * val
            f = f + _mm(_bf(act), wdown_ref[j * FF_CHUNK:(j + 1) * FF_CHUNK, :])
            ups = nxt
        f_ref[...] = f
        x2 = x1v + gf * (f * _rstd(f) * g4_ref[...])
        err = x2 - tgt_ref[...]
        loss_ref[...] += _sum0(err * err)
        dx2_ref[...] = err * (1.0 / D)

    tile = pl.BlockSpec((ts, D), lambda i: (i, 0))
    return pl.pallas_call(
        body,
        name="ffn_fwd",
        grid=(nt,),
        out_shape=(jax.ShapeDtypeStruct((N_DEV, t_len, FF_CHUNK), F32), jax.ShapeDtypeStruct((t_len, D), F32),
                   jax.ShapeDtypeStruct((t_len, D), F32), jax.ShapeDtypeStruct((1, D), F32)),
        in_specs=[tile, tile] + [VMEM_SPEC] * 7,
        out_specs=(pl.BlockSpec((N_DEV, ts, FF_CHUNK), lambda i: (0, i, 0)), tile, tile,
                   pl.BlockSpec((1, D), lambda i: (0, 0))),
        scratch_shapes=[pltpu.VMEM((N_DEV, HALO, FF_CHUNK), F32)],
        compiler_params=pltpu.CompilerParams(dimension_semantics=("arbitrary",), vmem_limit_bytes=VMEM_LIMIT_V7X),
    )(x1, target, mod, g_pre, g_post, w_up_b, conv_w8, conv_b8, w_down_b)


def _ffn_bwd(dx2, f, x1, up, mod, g_pre, g_post, w_up_b, conv_w8, conv_b8, w_down_b, ts):
    t_len = x1.shape[0]
    nt = t_len // ts
    half = N_DEV // 2

    def body(dx2_ref, f_ref, x1_ref, up_ref, halo_ref, mod_ref, g3_ref, g4_ref, wup_ref, cw_ref, cb_ref, wdown_ref,
             dx1_ref, dup_ref, act_ref, df_ref, h2_ref, dmod_ref, dg3_ref, dg4_ref, dcb_ref, dcw_ref,
             dycarry, dh2acc):
        i = pl.program_id(0)
        r = nt - 1 - i

        @pl.when(i == 0)
        def _():
            for ref in (dmod_ref, dg3_ref, dg4_ref, dcb_ref, dcw_ref, dycarry):
                ref[...] = jnp.zeros(ref.shape, F32)

        dx2v, fv, x1v = dx2_ref[...], f_ref[...], x1_ref[...]
        sh, sc, gf = mod_ref[3:4, :], mod_ref[4:5, :], mod_ref[5:6, :]
        g3, g4 = g3_ref[...], g4_ref[...]
        rstd4 = _rstd(fv)
        fh = fv * rstd4
        dmod_ref[2:3, :] += _sum0(dx2v * (fh * g4))
        dr = dx2v * gf
        dg4_ref[...] += _sum0(dr * fh)
        dfh = dr * g4
        dfb = _bf(rstd4 * (dfh - fh * _rowmean(dfh * fh)))
        df_ref[...] = dfb
        rstd3 = _rstd(x1v)
        xh = x1v * rstd3
        n3 = xh * g3
        h2_ref[...] = _bf(n3 * (1.0 + sc) + sh)
        dh2acc[...] = jnp.zeros((ts, D), F32)
        keep = jnp.where(r > 0, 1.0, 0.0).astype(F32)

        def dact_of(j):
            return _mm_nt(dfb, wdown_ref[j * FF_CHUNK:(j + 1) * FF_CHUNK, :])

        dact_next = dact_of(0)
        for j in range(half):
            dact = dact_next
            if j + 1 < half:
                dact_next = dact_of(j + 1)
            ys = []
            for jj in (j, j + half):
                before = halo_ref[jj] * keep
                upc = up_ref[jj]
                cw = cw_ref[jj]
                ys.append(cb_ref[jj:jj + 1, :] + _shift_down(upc, before, 2) * cw[0:1, :]
                          + _shift_down(upc, before, 1) * cw[1:2, :] + upc * cw[2:3, :])
            gate, val = ys
            sg = _sigmoid(gate)
            silu = gate * sg
            act_ref[j] = _bf(silu * val)
            dys = (dact * val * (sg + silu * (1.0 - sg)), dact * silu)
            for q, jj in enumerate((j, j + half)):
                dy = dys[q]
                cw = cw_ref[jj]
                dcb_ref[jj:jj + 1, :] += _sum0(dy)
                after = dycarry[jj]
                dycarry[jj] = dy[0:HALO, :]
                dy1, dy2 = _shift_up(dy, after, 1), _shift_up(dy, after, 2)
                upc = up_ref[jj]
                dcw_ref[jj, 0:1, :] += _sum0(dy2 * upc)
                dcw_ref[jj, 1:2, :] += _sum0(dy1 * upc)
                dcw_ref[jj, 2:3, :] += _sum0(dy * upc)
                dup = _bf(dy * cw[2:3, :] + dy1 * cw[1:2, :] + dy2 * cw[0:1, :])
                dup_ref[jj] = dup
                dh2acc[...] += _mm_nt(dup, wup_ref[jj])
        dh2 = dh2acc[...]
        dmod_ref[0:1, :] += _sum0(dh2)
        dmod_ref[1:2, :] += _sum0(dh2 * n3)
        dn3 = dh2 * (1.0 + sc)
        dg3_ref[...] += _sum0(dn3 * xh)
        dxh = dn3 * g3
        dx1_ref[...] = dx2v + rstd3 * (dxh - xh * _rowmean(dxh * xh))

    tile = pl.BlockSpec((ts, D), lambda i: (nt - 1 - i, 0))
    chunked = lambda n: pl.BlockSpec((n, ts, FF_CHUNK), lambda i: (0, nt - 1 - i, 0))
    halo = pl.BlockSpec((N_DEV, HALO, FF_CHUNK), lambda i: (0, jnp.maximum((nt - 1 - i) * (ts // HALO) - 1, 0), 0))
    const = lambda *shape: pl.BlockSpec(shape, lambda i: (0,) * len(shape))
    return pl.pallas_call(
        body,
        name="ffn_bwd",
        grid=(nt,),
        out_shape=(jax.ShapeDtypeStruct((t_len, D), F32), jax.ShapeDtypeStruct((N_DEV, t_len, FF_CHUNK), BF16),
                   jax.ShapeDtypeStruct((half, t_len, FF_CHUNK), BF16), jax.ShapeDtypeStruct((t_len, D), BF16),
                   jax.ShapeDtypeStruct((t_len, D), BF16), jax.ShapeDtypeStruct((3, D), F32),
                   jax.ShapeDtypeStruct((1, D), F32), jax.ShapeDtypeStruct((1, D), F32),
                   jax.ShapeDtypeStruct((N_DEV, FF_CHUNK), F32), jax.ShapeDtypeStruct((N_DEV, 3, FF_CHUNK), F32)),
        in_specs=[tile, tile, tile, chunked(N_DEV), halo] + [VMEM_SPEC] * 7,
        out_specs=(tile, chunked(N_DEV), chunked(half), tile, tile, const(3, D), const(1, D), const(1, D),
                   const(N_DEV, FF_CHUNK), const(N_DEV, 3, FF_CHUNK)),
        scratch_shapes=[pltpu.VMEM((N_DEV, HALO, FF_CHUNK), F32), pltpu.VMEM((ts, D), F32)],
        compiler_params=pltpu.CompilerParams(dimension_semantics=("arbitrary",), vmem_limit_bytes=VMEM_LIMIT_V7X),
    )(dx2, f, x1, up, up, mod, g_pre, g_post, w_up_b, conv_w8, conv_b8, w_down_b)


def _wgrad_up(h2, dup, ts):
    t_len = h2.shape[0]
    nt, half = t_len // ts, N_DEV // 2

    def body(h2_ref, dup_ref, out_ref):
        @pl.when(pl.program_id(1) == 0)
        def _():
            out_ref[...] = jnp.zeros(out_ref.shape, F32)

        for q in range(half):
            out_ref[q] += _mm_tn(dup_ref[q], h2_ref[...])

    return pl.pallas_call(
        body,
        name="wgrad_up",
        grid=(2, nt),
        out_shape=jax.ShapeDtypeStruct((N_DEV, FF_CHUNK, D), F32),
        in_specs=[pl.BlockSpec((ts, D), lambda g, t: (t, 0)), pl.BlockSpec((half, ts, FF_CHUNK), lambda g, t: (g, t, 0))],
        out_specs=pl.BlockSpec((half, FF_CHUNK, D), lambda g, t: (g, 0, 0)),
        compiler_params=pltpu.CompilerParams(dimension_semantics=("arbitrary", "arbitrary"),
                                             vmem_limit_bytes=VMEM_LIMIT_V7X),
    )(h2, dup)


def _sibling_swap_copies(srcs, dsts, send_sems, recv_sems):
    x, y, c = _coords()
    return [
        pltpu.make_async_remote_copy(srcs[a].at[xs, ys, 1 - c], dsts[a].at[xs, ys], send_sems.at[a, 2 * xs + ys],
                                     recv_sems.at[a, 2 * xs + ys], device_id=(x, y, 1 - c), device_id_type=MESH)
        for a in range(len(srcs)) for xs in range(2) for ys in range(2)
    ]


def _wgrad_down(act, df, ts, swap_src):
    t_len = df.shape[0]
    nt, half = t_len // ts, N_DEV // 2

    def body(act_ref, df_ref, src_ref, out_ref, dst_ref, send_sems, recv_sems):
        t = pl.program_id(0)

        @pl.when(t == 0)
        def _():
            for cp in _sibling_swap_copies([src_ref], [dst_ref], send_sems, recv_sems):
                cp.start()
            out_ref[...] = jnp.zeros(out_ref.shape, F32)

        for q in range(half):
            out_ref[q] += _mm_tn(act_ref[q], df_ref[...])

        @pl.when(t == nt - 1)
        def _():
            for cp in _sibling_swap_copies([src_ref], [dst_ref], send_sems, recv_sems):
                cp.wait()

    return pl.pallas_call(
        body,
        name="wgrad_down",
        grid=(nt,),
        out_shape=(jax.ShapeDtypeStruct((half, FF_CHUNK, D), F32), jax.ShapeDtypeStruct(swap_src.shape[1:], F32)),
        in_specs=[pl.BlockSpec((half, ts, FF_CHUNK), lambda t: (0, t, 0)), pl.BlockSpec((ts, D), lambda t: (t, 0)),
                  ANY_SPEC],
        out_specs=(pl.BlockSpec((half, FF_CHUNK, D), lambda t: (0, 0, 0)), ANY_SPEC),
        scratch_shapes=[pltpu.SemaphoreType.DMA((1, 4)), pltpu.SemaphoreType.DMA((1, 4))],
        compiler_params=pltpu.CompilerParams(dimension_semantics=("arbitrary",), vmem_limit_bytes=VMEM_LIMIT_V7X),
    )(act, df, swap_src)


def _mix_bwd(dx1, x, proj, mixed, mod, g_pre, g_post, w_in_t, sgn, w_sp, b_sp_t, w_pool, p_scale, w_out_b, ts, rs_srcs):
    t_len = x.shape[0]
    nt, nb = t_len // ts, ts // HEAD
    nr = len(rs_srcs)

    def body(*refs):
        (dx1_ref, x_ref, proj_ref, halo_ref, mixed_ref, mod_ref, g1_ref, g2_ref, win_ref, sgn_ref, ws_ref,
         bst_ref, wp_ref, ps_ref, wout_ref) = refs[:15]
        (gx_ref, dwin_ref, dwout_ref, dmod_ref, dg1_ref, dg2_ref, dsgn_ref, dws_ref, dbst_ref, dwp_ref,
         dps_ref) = refs[15 + nr:26 + nr]
        pbuf, dwsbuf, cat, dproj, dcat = refs[26 + 2 * nr:31 + 2 * nr]
        exchange = _ChipExchangeSteps(refs[15:15 + nr], refs[26 + nr:26 + 2 * nr], *refs[31 + 2 * nr:])
        i = pl.program_id(0)
        r = nt - 1 - i

        @pl.when(i == 0)
        def _():
            exchange.start()
            for ref in (dwin_ref, dwout_ref, dmod_ref, dg1_ref, dg2_ref, dsgn_ref, dws_ref, dbst_ref, dwp_ref, dps_ref):
                ref[...] = jnp.zeros(ref.shape, F32)
            dwsbuf[ts:ts + POOL_HALO, :] = jnp.zeros((POOL_HALO, B_WIDTH), F32)

        xv, dx1v, mixed = x_ref[...], dx1_ref[...], mixed_ref[...]
        sh, sc, gm = mod_ref[0:1, :], mod_ref[1:2, :], mod_ref[2:3, :]
        g1, g2 = g1_ref[...], g2_ref[...]
        rstd2 = _rstd(mixed)
        mh = mixed * rstd2
        dmod_ref[2:3, :] += _sum0(dx1v * (mh * g2))
        dr = dx1v * gm
        dg2_ref[...] += _sum0(dr * mh)
        dmh = dr * g2
        dmb = _bf(rstd2 * (dmh - mh * _rowmean(dmh * mh)))
        dcat[...] = _mm_nt(dmb, wout_ref[...])
        smask = _sgu_mask()
        for hd in range(N_HEAD):
            ucols = slice(hd * HEAD, (hd + 1) * HEAD)
            vcols = slice(A_WIDTH + hd * HEAD, A_WIDTH + (hd + 1) * HEAD)
            u, du_dp = _gelu_and_grad(proj_ref[:, ucols])
            v, dv_dp = _gelu_and_grad(proj_ref[:, vcols])
            rs = _rstd(v)
            vhat = v * rs
            gn = sgn_ref[hd:hd + 1, :]
            vn = _bf(vhat * gn)
            wm = _bf(jnp.where(smask, ws_ref[hd], 0.0))
            bias = bst_ref[:, hd:hd + 1]
            dzsum = jnp.zeros((HEAD, HEAD), F32)
            dwm = jnp.zeros((HEAD, HEAD), F32)
            dvn_parts = []
            for b in range(nb):
                rows = slice(b * HEAD, (b + 1) * HEAD)
                z = _mm(wm, vn[rows]) + bias
                da = dcat[rows, ucols]
                cat[rows, ucols] = _bf(u[rows] * z)
                dz = da * u[rows]
                dzsum = dzsum + dz
                dzb = _bf(dz)
                dwm = dwm + _mm_nt(dzb, vn[rows])
                dvn_parts.append(_mm_tn(wm, dzb))
                dproj[rows, ucols] = _bf((da * z) * du_dp[rows])
            dvn = jnp.concatenate(dvn_parts, axis=0)
            dsgn_ref[hd:hd + 1, :] += _sum0(dvn * vhat)
            dvh = dvn * gn
            dproj[:, vcols] = _bf((rs * (dvh - vhat * _rowmean(dvh * vhat))) * dv_dp)
            dws_ref[hd] += jnp.where(smask, dwm, 0.0)
            dbst_ref[:, hd:hd + 1] += jnp.sum(dzsum, axis=1, keepdims=True)
        keep = jnp.where(r > 0, 1.0, 0.0).astype(F32)
        pbuf[0:POOL_HALO, :] = halo_ref[...] * keep
        pbuf[POOL_HALO:POOL_HALO + ts, :] = proj_ref[:, 2 * A_WIDTH:]
        for g, w in enumerate(WINDOWS):
            cols = slice(g * HEAD, (g + 1) * HEAD)
            ccols = slice(A_WIDTH + g * HEAD, A_WIDTH + (g + 1) * HEAD)
            pcols = slice(2 * A_WIDTH + g * HEAD, 2 * A_WIDTH + (g + 1) * HEAD)
            wpg = _bf(wp_ref[g])
            psg = ps_ref[:, cols]
            ext = pbuf[:, cols]
            inv = _inv_count(r * ts, ts, w)
            pb = _bf(_window_sum(ext, w, True)[POOL_HALO:] * inv - ext[POOL_HALO:])
            yb = _mm(pb, wpg)
            dob = dcat[:, ccols]
            cat[:, ccols] = _bf(yb * psg)
            dps_ref[:, cols] += _sum0(dob * yb)
            dyb = _bf(dob * psg)
            dwp_ref[g] += _mm_tn(pb, dyb)
            dpooled = _mm_nt(dyb, wpg)
            dwsbuf[0:ts, cols] = dpooled * inv
            dproj[:, pcols] = _bf(_window_sum(dwsbuf[:, cols], w, False)[0:ts] - dpooled)
        dwsbuf[ts:ts + POOL_HALO, :] = dwsbuf[0:POOL_HALO, :]
        dpb = dproj[...]
        rstd1 = _rstd(xv)
        xh = xv * rstd1
        n1 = xh * g1
        dwin_ref[...] += _mm_tn(dpb, _bf(n1 * (1.0 + sc) + sh))
        dwout_ref[...] += _mm_tn(cat[...], dmb)
        dh = _mm(dpb, win_ref[...])
        dmod_ref[0:1, :] += _sum0(dh)
        dmod_ref[1:2, :] += _sum0(dh * n1)
        dn1 = dh * (1.0 + sc)
        dg1_ref[...] += _sum0(dn1 * xh)
        dxh = dn1 * g1
        gx_ref[...] = dx1v + rstd1 * (dxh - xh * _rowmean(dxh * xh))

        @pl.when(i == nt - 1)
        def _():
            exchange.finish()

    tile = lambda wid: pl.BlockSpec((ts, wid), lambda i: (nt - 1 - i, 0))
    halo = pl.BlockSpec((POOL_HALO, B_WIDTH),
                        lambda i: (jnp.maximum((nt - 1 - i) * (ts // POOL_HALO) - 1, 0), 2 * A_WIDTH // B_WIDTH))
    const = lambda *shape: pl.BlockSpec(shape, lambda i: (0,) * len(shape))
    resident = lambda *shape: pl.BlockSpec(shape, lambda i: (0,) * len(shape), pipeline_mode=pl.Buffered(1))
    outs = pl.pallas_call(
        body,
        name="mix_bwd",
        grid=(nt,),
        out_shape=(jax.ShapeDtypeStruct((t_len, D), F32), jax.ShapeDtypeStruct((IN_WIDTH, D), F32),
                   jax.ShapeDtypeStruct((D, D), F32), jax.ShapeDtypeStruct((3, D), F32),
                   jax.ShapeDtypeStruct((1, D), F32), jax.ShapeDtypeStruct((1, D), F32),
                   jax.ShapeDtypeStruct((N_HEAD, HEAD), F32), jax.ShapeDtypeStruct((N_HEAD, HEAD, HEAD), F32),
                   jax.ShapeDtypeStruct((HEAD, N_HEAD), F32), jax.ShapeDtypeStruct((N_HEAD, HEAD, HEAD), F32),
                   jax.ShapeDtypeStruct((1, B_WIDTH), F32))
        + tuple(jax.ShapeDtypeStruct((3, *s.shape[2:]), s.dtype) for s in rs_srcs),
        in_specs=[tile(D), tile(D), tile(IN_WIDTH), halo, tile(D)] + [VMEM_SPEC] * 10 + [ANY_SPEC] * nr,
        out_specs=(tile(D), resident(IN_WIDTH, D), resident(D, D), const(3, D), const(1, D), const(1, D),
                   const(N_HEAD, HEAD), const(N_HEAD, HEAD, HEAD), const(HEAD, N_HEAD), const(N_HEAD, HEAD, HEAD),
                   const(1, B_WIDTH)) + (ANY_SPEC,) * nr,
        scratch_shapes=[pltpu.VMEM((POOL_HALO + ts, B_WIDTH), F32), pltpu.VMEM((ts + POOL_HALO, B_WIDTH), F32),
                        pltpu.VMEM((ts, D), BF16), pltpu.VMEM((ts, IN_WIDTH), BF16), pltpu.VMEM((ts, D), F32),
                        pltpu.SemaphoreType.DMA((3 * nr,)), pltpu.SemaphoreType.DMA((3 * nr,))],
        compiler_params=pltpu.CompilerParams(dimension_semantics=("arbitrary",), vmem_limit_bytes=VMEM_LIMIT_V7X),
    )(dx1, x, proj, proj, mixed, mod, g_pre, g_post, w_in_t, sgn, w_sp, b_sp_t, w_pool, p_scale, w_out_b, *rs_srcs)
    return outs[:11], outs[11:]


def _pair_add(name, coords, grid, specs_a, specs_b, out_specs, out_shapes, a_arrays, b_arrays, swap_srcs):
    n, ns = len(a_arrays), len(swap_srcs)
    last = tuple(g - 1 for g in grid)

    def body(co_ref, *refs):
        ids = [pl.program_id(d) for d in range(len(grid))]
        swap = refs[2 * n:2 * n + ns], refs[4 * n + ns:4 * n + 2 * ns], *refs[4 * n + 2 * ns:]
        if ns:
            @pl.when(functools.reduce(jnp.logical_and, [i == 0 for i in ids]))
            def _():
                for cp in _sibling_swap_copies(*swap):
                    cp.start()

        for k in range(n):
            total = refs[k][...] + refs[n + k][...]
            refs[2 * n + ns + k][...] = total
            refs[3 * n + ns + k][...] = _bf(total)

        if ns:
            @pl.when(functools.reduce(jnp.logical_and, [i == e for i, e in zip(ids, last)]))
            def _():
                for cp in _sibling_swap_copies(*swap):
                    cp.wait()

    outs = pl.pallas_call(
        body,
        name=name,
        grid_spec=pltpu.PrefetchScalarGridSpec(
            num_scalar_prefetch=1, grid=grid, in_specs=specs_a + specs_b + [ANY_SPEC] * ns,
            out_specs=out_specs * 2 + [ANY_SPEC] * ns,
            scratch_shapes=[pltpu.SemaphoreType.DMA((ns, 4)), pltpu.SemaphoreType.DMA((ns, 4))] if ns else []),
        out_shape=tuple(jax.ShapeDtypeStruct(s, dt) for dt in (F32, BF16) for s in out_shapes)
        + tuple(jax.ShapeDtypeStruct(g.shape[1:], F32) for g in swap_srcs),
        compiler_params=pltpu.CompilerParams(dimension_semantics=("arbitrary",) * len(grid),
                                             vmem_limit_bytes=VMEM_LIMIT_V7X),
    )(coords, *a_arrays, *b_arrays, *swap_srcs)
    return list(outs[:n]), list(outs[n:2 * n]), list(outs[2 * n:])


def _final_add_adamw(coords, s1, r, ws, ms, vs, n_split=4):
    n = len(s1)

    def body(co_ref, *refs):
        for k in range(n):
            s_ref, r_ref, w_ref, m_ref, v_ref = (refs[q * n + k] for q in range(5))
            g_ref, d_ref, nm_ref, nv_ref = (refs[(5 + q) * n + k] for q in range(4))
            g = ((s_ref[...] + r_ref[0].astype(F32)) + r_ref[1].astype(F32)) + r_ref[2].astype(F32)
            g_ref[...] = g
            delta, m, v = _adamw(w_ref[...], g, m_ref[...], v_ref[...])
            d_ref[...] = delta
            nm_ref[...] = m
            nv_ref[...] = v

    def shard_spec(a):
        rows, cols = a.shape
        return pl.BlockSpec((rows // n_split, cols), lambda i, co: (i, 0))

    def mine_spec(a):
        rows, cols = a.shape[2:]
        return pl.BlockSpec((None, None, rows // n_split, cols), lambda i, co: (co[0], co[1], i, 0))

    def recv_spec(a):
        rows, cols = a.shape[1:]
        return pl.BlockSpec((3, rows // n_split, cols), lambda i, co: (0, i, 0))

    in_specs = ([mine_spec(a) for a in s1] + [recv_spec(a) for a in r] + [shard_spec(a) for a in ws] * 3)
    out_specs = [shard_spec(a) for a in ws] * 4
    outs = pl.pallas_call(
        body,
        name="grad_final_adamw",
        grid_spec=pltpu.PrefetchScalarGridSpec(num_scalar_prefetch=1, grid=(n_split,), in_specs=in_specs,
                                               out_specs=out_specs),
        out_shape=tuple(jax.ShapeDtypeStruct(a.shape, F32) for a in ws) * 4,
        compiler_params=pltpu.CompilerParams(dimension_semantics=("arbitrary",), vmem_limit_bytes=VMEM_LIMIT_V7X),
    )(coords, *s1, *r, *ws, *ms, *vs)
    return [tuple(outs[q * n + k] for q in range(4)) for k in range(n)]


def _sibling_add(tag, g5, r1, coords, swap_srcs=(), n_split=4):
    shapes = [g.shape[3:] for g in g5]
    spec_g = [pl.BlockSpec((None, None, None, s[0] // n_split, s[1]), lambda i, j, k, co: (i, j, co[2], k, 0))
              for s in shapes]
    spec_r = [pl.BlockSpec((None, None, s[0] // n_split, s[1]), lambda i, j, k, co: (i, j, k, 0)) for s in shapes]
    return _pair_add("grad_add_core_" + tag, coords, (2, 2, n_split), spec_g, spec_r, spec_r,
                     [(2, 2, *s) for s in shapes], g5, r1, list(swap_srcs))


def _tail_exchange(big, partials, pick_mine, y_first, dmod3):
    n, nb = len(partials), len(big)
    big_shapes = [g.shape[1:] for g in big]
    big5 = [g.reshape(2, 2, 2, *s) for g, s in zip(big, big_shapes)]
    flips = _ChipExchangeSteps.FLIPS

    def body(*refs):
        g5, p_in, dm_ref = refs[:nb], refs[nb:nb + n], refs[nb + n]
        outs = refs[nb + n + 1:2 * nb + 2 * n + 2]
        g_out, sums, dm2d = outs[:nb], outs[nb:nb + n], outs[nb + n]
        scratch = refs[2 * nb + 2 * n + 2:]
        s1, stage, chip_recv = scratch[:nb], scratch[nb:2 * nb], scratch[2 * nb:3 * nb]
        acc, rbuf = scratch[3 * nb:3 * nb + n], scratch[3 * nb + n:3 * nb + 2 * n]
        (dm_recv, send_sems, recv_sems, dm_send_sems, dm_recv_sems, sib_send, sib_recv, chip_send,
         chip_recv_sems) = scratch[3 * nb + 2 * n:]
        x, y, c = _coords()
        me = 4 * x + 2 * y + c
        sibling = (x, y, 1 - c)
        dm_copies = [
            pltpu.make_async_remote_copy(dm_ref.at[me ^ k], dm_recv.at[k], dm_send_sems.at[k], dm_recv_sems.at[k],
                                         device_id=_peer(k), device_id_type=MESH)
            for k in range(1, N_DEV)
        ]
        for cp in dm_copies:
            cp.start()
        sib_copies = _sibling_swap_copies(g5, s1, sib_send, sib_recv)
        for cp in sib_copies:
            cp.start()
        for a in range(n):
            acc[a][...] = p_in[a][...]

        def small_phase(ph, peers):
            copies = [
                pltpu.make_async_remote_copy(acc[a], rbuf[a].at[ph], send_sems.at[ph, a], recv_sems.at[ph, a],
                                             device_id=peers[y_first[a]], device_id_type=MESH)
                for a in range(n)
            ]
            for cp in copies:
                cp.start()
            for cp in copies:
                cp.wait()
            for a in range(n):
                acc[a][...] = acc[a][...] + rbuf[a][ph]

        small_phase(0, (sibling, sibling))
        for cp in sib_copies:
            cp.wait()
        for a in range(nb):
            for xs in range(2):
                for ys in range(2):
                    total = g5[a][xs, ys, c] + s1[a][xs, ys]
                    s1[a][xs, ys] = total
                    stage[a][xs, ys] = _bf(total)
        chip_copies = [
            pltpu.make_async_remote_copy(stage[a].at[x ^ fx, y ^ fy], chip_recv[a].at[j], chip_send.at[a, j],
                                         chip_recv_sems.at[a, j], device_id=(x ^ fx, y ^ fy, c), device_id_type=MESH)
            for a in range(nb) for j, (fx, fy) in enumerate(flips)
        ]
        for cp in chip_copies:
            cp.start()
        x_peer, y_peer = (1 - x, y, c), (x, 1 - y, c)
        small_phase(1, (x_peer, y_peer))
        small_phase(2, (y_peer, x_peer))
        for a in range(n):
            sums[a][...] = acc[a][me] if pick_mine[a] else acc[a][...]
        dm2d[...] = jnp.zeros(dm2d.shape, F32)
        dm2d[0:1, :] = dm_ref[me]
        for cp in dm_copies:
            cp.wait()
        for k in range(1, N_DEV):
            dm2d[k:k + 1, :] = dm_recv[k]
        for cp in chip_copies:
            cp.wait()
        for a in range(nb):
            g_out[a][...] = ((s1[a][x, y] + chip_recv[a][0].astype(F32)) + chip_recv[a][1].astype(F32)) \
                + chip_recv[a][2].astype(F32)

    out_shapes = tuple(jax.ShapeDtypeStruct(s, F32) for s in big_shapes) + tuple(
        jax.ShapeDtypeStruct(p.shape[1:] if pk else p.shape, F32) for p, pk in zip(partials, pick_mine))
    outs = pl.pallas_call(
        body,
        name="tail_exchange",
        out_shape=out_shapes + (jax.ShapeDtypeStruct((2 * N_DEV, MOD_COLS), F32),),
        in_specs=[VMEM_SPEC] * (nb + n + 1),
        out_specs=(VMEM_SPEC,) * (nb + n + 1),
        scratch_shapes=[pltpu.VMEM((2, 2, *s), F32) for s in big_shapes]
        + [pltpu.VMEM((2, 2, *s), BF16) for s in big_shapes]
        + [pltpu.VMEM((3, *s), BF16) for s in big_shapes]
        + [pltpu.VMEM(p.shape, F32) for p in partials]
        + [pltpu.VMEM((3, *p.shape), F32) for p in partials]
        + [pltpu.VMEM((N_DEV, 1, MOD_COLS), F32), pltpu.SemaphoreType.DMA((3, n)), pltpu.SemaphoreType.DMA((3, n)),
           pltpu.SemaphoreType.DMA((N_DEV,)), pltpu.SemaphoreType.DMA((N_DEV,)),
           pltpu.SemaphoreType.DMA((nb, 4)), pltpu.SemaphoreType.DMA((nb, 4)),
           pltpu.SemaphoreType.DMA((nb, 3)), pltpu.SemaphoreType.DMA((nb, 3))],
        compiler_params=pltpu.CompilerParams(vmem_limit_bytes=VMEM_LIMIT_V7X),
    )(*big5, *partials, dmod3)
    return list(outs[:nb]), list(outs[nb:nb + n]), outs[nb + n]


def _small_update(grads, ws, ms, vs, scx, dm2d, w_ada, m_ada, v_ada, loss_lanes):
    n = len(grads)

    def body(*refs):
        g_in, w_in, m_in, v_in = (refs[q * n:(q + 1) * n] for q in range(4))
        scx_ref, dm_ref, wa_ref, ma_ref, va_ref, ll_ref = refs[4 * n:4 * n + 6]
        outs = refs[4 * n + 6:]
        g_out, d_out, nm_out, nv_out = (outs[q * (n + 1):(q + 1) * (n + 1)] for q in range(4))
        loss_ref = outs[4 * (n + 1)]
        for a in range(n + 1):
            if a < n:
                g, w, m, v = g_in[a][...], w_in[a][...], m_in[a][...], v_in[a][...]
            else:
                g = _mm_tn(_bf(scx_ref[...]), _bf(dm_ref[...]))
                w, m, v = wa_ref[...], ma_ref[...], va_ref[...]
            g_out[a][...] = g
            delta, m, v = _adamw(w, g, m, v)
            d_out[a][...] = delta
            nm_out[a][...] = m
            nv_out[a][...] = v
        loss_ref[...] = jnp.sum(ll_ref[...], axis=1, keepdims=True) * (0.5 / D)

    w_shapes = tuple(jax.ShapeDtypeStruct(w.shape, F32) for w in list(ws) + [w_ada])
    outs = pl.pallas_call(
        body,
        name="small_update",
        out_shape=w_shapes * 4 + (jax.ShapeDtypeStruct((1, 1), F32),),
        in_specs=[VMEM_SPEC] * (4 * n + 6),
        out_specs=(VMEM_SPEC,) * (4 * (n + 1) + 1),
        compiler_params=pltpu.CompilerParams(vmem_limit_bytes=VMEM_LIMIT_V7X),
    )(*grads, *ws, *ms, *vs, scx, dm2d, w_ada, m_ada, v_ada, loss_lanes)
    return [tuple(outs[q * (n + 1) + k] for q in range(4)) for k in range(n + 1)], outs[4 * (n + 1)]


def kernel(x, c, w_ada, b_ada, pre_mix_g, post_mix_g, w_in, sgu_norm_g, w_spatial, b_spatial, w_pool, pool_scale, w_out, pre_ffn_g, post_ffn_g, w_up, conv_w, conv_b, w_down, loss_target, m_w_ada, m_b_ada, m_pre_mix_g, m_post_mix_g, m_w_in, m_sgu_norm_g, m_w_spatial, m_b_spatial, m_w_pool, m_pool_scale, m_w_out, m_pre_ffn_g, m_post_ffn_g, m_w_up, m_conv_w, m_conv_b, m_w_down, v_w_ada, v_b_ada, v_pre_mix_g, v_post_mix_g, v_w_in, v_sgu_norm_g, v_w_spatial, v_b_spatial, v_w_pool, v_pool_scale, v_w_out, v_pre_ffn_g, v_post_ffn_g, v_w_up, v_conv_w, v_conv_b, v_w_down):
    t_len = x.shape[1]
    ts = min(256, t_len)
    ts_mix = min(512, t_len)
    ts_w = min(1024, t_len)
    coords = jnp.stack([lax.axis_index("x"), lax.axis_index("y"), lax.axis_index("c")]).astype(jnp.int32)

    w_in_t, w_up_t = w_in[0].T, w_up[0].T
    mod3, scx, (g_in, g_out) = _prologue(c, w_ada[0], b_ada.reshape(N_DEV, 1, MOD_COLS), [w_in_t, w_out[0]],
                                         [BF16, BF16])
    mod = mod3.reshape(N_MOD, D)
    w_in_tb = g_in.reshape(IN_WIDTH, D)
    w_out_b = g_out.reshape(D, D)
    conv_b8 = conv_b.reshape(N_DEV, FF_CHUNK)
    b_sp_t = b_spatial[0].T

    x2d, tgt = x[0], loss_target[0]
    (x1, proj, mixed), (g_up, g_down, g_cw) = _mix_fwd(
        x2d, mod, pre_mix_g, post_mix_g, w_in_tb, sgu_norm_g[0], w_spatial[0], b_sp_t, w_pool[0], pool_scale, w_out_b,
        ts_mix, [w_up_t, w_down[0], conv_w[0]], [w_up.shape[1:], w_down.shape[1:], conv_w.shape[1:]], [BF16, BF16, F32])
    w_down_b = g_down.reshape(FF, D)
    up, f, dx2, loss_lanes = _ffn_fwd(x1, tgt, mod, pre_ffn_g, post_ffn_g, g_up, g_cw, conv_b8, w_down_b, ts)

    (dx1, dup, act, df, h2, dmod_f, d_pre_ffn, d_post_ffn, d_cb8, d_cw8) = _ffn_bwd(
        dx2, f, x1, up, mod, pre_ffn_g, post_ffn_g, g_up, g_cw, conv_b8, w_down_b, ts)
    gw_up = _wgrad_up(h2, dup, ts_w).reshape(2, 2, 2, FF_CHUNK, D)
    gw_down, r1_up = _wgrad_down(act, df, ts_w, gw_up)
    gw_down = gw_down.reshape(2, 2, 2, FF // N_DEV, D)
    s1_up, s1_up_b, r1_down = _sibling_add("up", [gw_up], [r1_up], coords, swap_srcs=[gw_down])
    s1_down, s1_down_b, _ = _sibling_add("down", [gw_down], r1_down, coords)
    s1_ffn, s1_ffn_b = s1_up + s1_down, s1_up_b + s1_down_b
    ((grad_x, gw_in, gw_out, dmod_m, d_pre_mix, d_post_mix, d_sgn, d_wsp, d_bsp_t, d_wpool, d_ps), r_ffn) = _mix_bwd(
        dx1, x2d, proj, mixed, mod, pre_mix_g, post_mix_g, w_in_tb, sgu_norm_g[0], w_spatial[0], b_sp_t,
        w_pool[0], pool_scale, w_out_b, ts_mix, s1_ffn_b)
    gw_in = gw_in.reshape(N_DEV, IN_WIDTH // N_DEV, D)
    gw_out = gw_out.reshape(N_DEV, D // N_DEV, D)

    big = _final_add_adamw(coords, s1_ffn, list(r_ffn), [w_up_t, w_down[0]], [m_w_up[0].T, m_w_down[0]],
                           [v_w_up[0].T, v_w_down[0]])
    r_up, r_down = tuple(a.T[None] for a in big[0]), tuple(a[None] for a in big[1])

    dmod = jnp.concatenate([dmod_m, dmod_f], axis=0)
    names = ["b_ada", "pre_mix_g", "post_mix_g", "sgu_norm_g", "w_spatial", "b_spatial", "w_pool", "pool_scale",
             "pre_ffn_g", "post_ffn_g", "conv_w", "conv_b"]
    partials = [dmod.reshape(1, N_MOD * D), d_pre_mix, d_post_mix, d_sgn, d_wsp, d_bsp_t.T, d_wpool, d_ps,
                d_pre_ffn, d_post_ffn, d_cw8, d_cb8.reshape(1, 2 * FF), loss_lanes]
    small_w = [b_ada, pre_mix_g, post_mix_g, sgu_norm_g[0], w_spatial[0], b_spatial[0], w_pool[0], pool_scale,
               pre_ffn_g, post_ffn_g, conv_w[0], conv_b]
    small_m = [m_b_ada, m_pre_mix_g, m_post_mix_g, m_sgu_norm_g[0], m_w_spatial[0], m_b_spatial[0], m_w_pool[0],
               m_pool_scale, m_pre_ffn_g, m_post_ffn_g, m_conv_w[0], m_conv_b]
    small_v = [v_b_ada, v_pre_mix_g, v_post_mix_g, v_sgu_norm_g[0], v_w_spatial[0], v_b_spatial[0], v_w_pool[0],
               v_pool_scale, v_pre_ffn_g, v_post_ffn_g, v_conv_w[0], v_conv_b]
    g_mix, sums, dm2d = _tail_exchange([gw_in, gw_out], partials, [nm == "conv_w" for nm in names] + [False],
                                       [int(nm == "w_pool") for nm in names] + [0],
                                       dmod.reshape(N_DEV, 1, MOD_COLS))
    small, loss11 = _small_update(
        sums[:-1] + g_mix, small_w + [w_in_t, w_out[0]], small_m + [m_w_in[0].T, m_w_out[0]],
        small_v + [v_w_in[0].T, v_w_out[0]], scx, dm2d, w_ada[0], m_w_ada[0], v_w_ada[0], sums[-1])
    loss = loss11.reshape(())
    lead = {"sgu_norm_g", "w_spatial", "b_spatial", "w_pool", "conv_w", "w_in", "w_out", "w_ada"}
    res = {nm: tuple((a.T if nm == "w_in" else a)[None] if nm in lead else a for a in four)
           for nm, four in zip(names + ["w_in", "w_out", "w_ada"], small)}
    res.update(w_up=r_up, w_down=r_down)

    order = ["w_ada", "b_ada", "pre_mix_g", "post_mix_g", "w_in", "sgu_norm_g", "w_spatial", "b_spatial", "w_pool",
             "pool_scale", "w_out", "pre_ffn_g", "post_ffn_g", "w_up", "conv_w", "conv_b", "w_down"]
    return (loss, grad_x[None], *[res[nm][0] for nm in order], *[res[nm][1] for nm in order],
            *[res[nm][2] for nm in order], *[res[nm][3] for nm in order])
```

```python
import functools
import math

import jax
import jax.numpy as jnp
from jax import lax
from jax.experimental import pallas as pl
from jax.experimental.pallas import tpu as pltpu

F32 = jnp.float32
BF16 = jnp.bfloat16
MESH = pl.DeviceIdType.MESH

EPS = 1e-6
D = 1024
HEAD = 128
N_HEAD = 4
A_WIDTH = 512
B_WIDTH = 512
IN_WIDTH = 1536
WINDOWS = (2, 4, 8, 16)
CHUNK = 64
FF = 2816
N_DEV = 8
FF_CHUNK = 704
N_MOD = 6
MOD_COLS = 768

ADAM_LR = 0.001
ADAM_B1 = 0.9
ADAM_B2 = 0.999
ADAM_EPS = 1e-08
ADAM_WD = 0.01
ADAM_STEP = 10

VMEM_LIMIT_V7X = 62 * 1024 * 1024
HALO = 8
POOL_HALO = 16

VMEM_SPEC = pl.BlockSpec(memory_space=pltpu.VMEM)
ANY_SPEC = pl.BlockSpec(memory_space=pl.ANY)


def _bf(x):
    return x.astype(BF16)


def _mm(a, b):
    return jnp.dot(a, b, preferred_element_type=F32)


def _mm_nt(a, b):
    return lax.dot_general(a, b, (((1,), (1,)), ((), ())), preferred_element_type=F32)


def _mm_tn(a, b):
    return lax.dot_general(a, b, (((0,), (0,)), ((), ())), preferred_element_type=F32)


def _rstd(x):
    return lax.rsqrt(jnp.mean(x * x, axis=-1, keepdims=True) + EPS)


def _sum0(x):
    return jnp.sum(x, axis=0, keepdims=True)


def _rowmean(x):
    return jnp.mean(x, axis=-1, keepdims=True)


_GELU_K = math.sqrt(2.0 / math.pi)


def _gelu_and_grad(x):
    x2 = x * x
    th = jnp.tanh(_GELU_K * (x + 0.044715 * (x * x2)))
    cdf = 0.5 * (1.0 + th)
    grad = cdf + 0.5 * x * (1.0 - th * th) * (_GELU_K * (1.0 + 3.0 * 0.044715 * x2))
    return x * cdf, grad


def _gelu(x):
    return x * (0.5 * (1.0 + jnp.tanh(_GELU_K * (x + 0.044715 * (x * x * x)))))


def _sigmoid(x):
    return 0.5 * jnp.tanh(0.5 * x) + 0.5


def _sgu_mask():
    ri = lax.broadcasted_iota(jnp.int32, (HEAD, HEAD), 0)
    ci = lax.broadcasted_iota(jnp.int32, (HEAD, HEAD), 1)
    return (ci // CHUNK) <= (ri // CHUNK)


def _window_sum(ext, w, trailing):
    n = ext.shape[0]
    s, k = ext, 1
    while k < w:
        s = s + pltpu.roll(s, k if trailing else n - k, 0)
        k *= 2
    return s


def _inv_count(row0, n, w):
    t = row0 + lax.broadcasted_iota(jnp.int32, (n, 1), 0)
    return 1.0 / jnp.minimum(t + 1, w).astype(F32)


def _shift_down(v, before, k):
    rows = lax.broadcasted_iota(jnp.int32, before.shape, 0)
    r = pltpu.roll(v, k, 0)
    top = jnp.where(rows < k, pltpu.roll(before, k, 0), r[0:HALO])
    return jnp.concatenate([top, r[HALO:]], axis=0)


def _shift_up(v, after, k):
    n = v.shape[0]
    rows = lax.broadcasted_iota(jnp.int32, after.shape, 0)
    r = pltpu.roll(v, n - k, 0)
    bottom = jnp.where(rows >= HALO - k, pltpu.roll(after, HALO - k, 0), r[n - HALO:])
    return jnp.concatenate([r[:n - HALO], bottom], axis=0)


def _adamw(w, g, m, v):
    m = ADAM_B1 * m + (1.0 - ADAM_B1) * g
    v = ADAM_B2 * v + (1.0 - ADAM_B2) * (g * g)
    m_hat = m / (1.0 - ADAM_B1 ** ADAM_STEP)
    v_hat = v / (1.0 - ADAM_B2 ** ADAM_STEP)
    delta = -ADAM_LR * (m_hat / (jnp.sqrt(v_hat) + ADAM_EPS) + ADAM_WD * w)
    return delta, m, v


def _coords():
    return lax.axis_index("x"), lax.axis_index("y"), lax.axis_index("c")


def _peer(k):
    x, y, c = _coords()
    return (x ^ ((k >> 2) & 1), y ^ ((k >> 1) & 1), c ^ (k & 1))


def _my_index():
    x, y, c = _coords()
    return 4 * x + 2 * y + c


def _adaln_modulation(c_ref, w_ref, b_ref, mod_ref, scx_ref, scbuf, stage, recv, send_sems, recv_sems):
    me = _my_index()
    cv = c_ref[...]
    scbuf[0] = cv * _sigmoid(cv)
    first = [
        pltpu.make_async_remote_copy(scbuf.at[0], scbuf.at[k], send_sems.at[0, k], recv_sems.at[0, k],
                                     device_id=_peer(k), device_id_type=MESH)
        for k in range(1, N_DEV)
    ]
    for cp in first:
        cp.start()
    for cp in first:
        cp.wait()
    scx_ref[...] = jnp.zeros(scx_ref.shape, F32)
    for k in range(N_DEV):
        scx_ref[k:k + 1, :] = scbuf[k]
    prod = _mm(_bf(scx_ref[...]), _bf(w_ref[...]))
    for k in range(N_DEV):
        stage[k] = prod[k:k + 1, :] + b_ref[me]
    second = [
        pltpu.make_async_remote_copy(stage.at[k], recv.at[k], send_sems.at[1, k], recv_sems.at[1, k],
                                     device_id=_peer(k), device_id_type=MESH)
        for k in range(1, N_DEV)
    ]
    for cp in second:
        cp.start()
    mod_ref[me] = stage[0]
    for cp in second:
        cp.wait()
    for k in range(1, N_DEV):
        mod_ref[me ^ k] = recv[k]


class _GatherSteps:
    def __init__(self, ins, outs, stages, send_sems, recv_sems, local_sems):
        self.ins, self.outs, self.stages = ins, outs, stages
        self.send_sems, self.recv_sems, self.local_sems = send_sems, recv_sems, local_sems
        x, y, c = _coords()
        self.c = c
        self.me, self.sibling = (x, y, c), (x, y, 1 - c)
        self.chips = [(1 - x, y), (x, 1 - y), (1 - x, 1 - y)]

    def _copy(self, a, k, block, to, from_stage=False):
        dst = self.outs[a].at[4 * block[0] + 2 * block[1] + block[2]]
        return pltpu.make_async_remote_copy(self.stages[a] if from_stage else dst, dst, self.send_sems.at[a, k],
                                            self.recv_sems.at[a, k], device_id=to, device_id_type=MESH)

    def _local(self, a):
        me = self.me
        return pltpu.make_async_copy(self.stages[a], self.outs[a].at[4 * me[0] + 2 * me[1] + me[2]],
                                     self.local_sems.at[a])

    def _first(self, a):
        cps = [self._copy(a, 0, self.me, self.sibling, from_stage=True)]
        return cps + [self._copy(a, 1 + j, self.me, (*chip, self.c), from_stage=True)
                      for j, chip in enumerate(self.chips)]

    def _passed(self, a, j):
        return self._copy(a, 4 + j, (*self.chips[j], self.c), self.sibling)

    def start(self):
        for a in range(len(self.ins)):
            block = self.ins[a][...]
            if block.shape != self.stages[a].shape:
                block = block.T
            self.stages[a][...] = block.astype(self.stages[a].dtype)
            self._local(a).start()
            for cp in self._first(a):
                cp.start()

    def forward(self):
        for a in range(len(self.ins)):
            for j, chip in enumerate(self.chips):
                self._copy(a, 1 + j, (*chip, self.c), self.me).wait_recv()
                self._passed(a, j).start()

    def finish(self):
        for a in range(len(self.ins)):
            self._copy(a, 0, self.sibling, self.me).wait_recv()
            for j, chip in enumerate(self.chips):
                self._copy(a, 4 + j, (*chip, 1 - self.c), self.me).wait_recv()
            for cp in self._first(a) + [self._passed(a, j) for j in range(3)]:
                cp.wait_send()
            self._local(a).wait()


def _gather_scratch(shapes, out_dtypes):
    n = len(shapes)
    return ([pltpu.VMEM(s, dt) for s, dt in zip(shapes, out_dtypes)]
            + [pltpu.SemaphoreType.DMA((n, 7)), pltpu.SemaphoreType.DMA((n, 7)), pltpu.SemaphoreType.DMA((n,))])


def _gather_out_shapes(shapes, out_dtypes):
    return tuple(jax.ShapeDtypeStruct((N_DEV, *s), dt) for s, dt in zip(shapes, out_dtypes))


def _prologue(c_row, w_ada, b_ada3, shards, out_dtypes):
    n = len(shards)

    def body(*refs):
        c_ref, w_ref, b_ref = refs[:3]
        mod_ref, scx_ref = refs[3 + n:5 + n]
        gather = _GatherSteps(refs[3:3 + n], refs[5 + n:5 + 2 * n], refs[5 + 2 * n:5 + 3 * n],
                              *refs[5 + 3 * n:8 + 3 * n])
        gather.start()
        _adaln_modulation(c_ref, w_ref, b_ref, mod_ref, scx_ref, *refs[8 + 3 * n:])
        gather.forward()
        gather.finish()

    outs = pl.pallas_call(
        body,
        name="prologue",
        out_shape=(jax.ShapeDtypeStruct((N_DEV, 1, MOD_COLS), F32), jax.ShapeDtypeStruct((2 * N_DEV, D), F32))
        + _gather_out_shapes([s.shape for s in shards], out_dtypes),
        in_specs=[VMEM_SPEC] * (3 + n),
        out_specs=(VMEM_SPEC, VMEM_SPEC) + (ANY_SPEC,) * n,
        scratch_shapes=_gather_scratch([s.shape for s in shards], out_dtypes) + [
            pltpu.VMEM((N_DEV, 1, D), F32),
            pltpu.VMEM((N_DEV, 1, MOD_COLS), F32),
            pltpu.VMEM((N_DEV, 1, MOD_COLS), F32),
            pltpu.SemaphoreType.DMA((2, N_DEV)),
            pltpu.SemaphoreType.DMA((2, N_DEV)),
        ],
        compiler_params=pltpu.CompilerParams(vmem_limit_bytes=VMEM_LIMIT_V7X),
    )(c_row, w_ada, b_ada3, *shards)
    return outs[0], outs[1], outs[2:]


class _ChipExchangeSteps:
    FLIPS = ((1, 0), (0, 1), (1, 1))

    def __init__(self, srcs, dsts, send_sems, recv_sems):
        self.srcs, self.dsts, self.send_sems, self.recv_sems = srcs, dsts, send_sems, recv_sems

    def _copies(self):
        x, y, c = _coords()
        out = []
        for a in range(len(self.srcs)):
            for j, (fx, fy) in enumerate(self.FLIPS):
                k = 3 * a + j
                out.append(pltpu.make_async_remote_copy(
                    self.srcs[a].at[x ^ fx, y ^ fy], self.dsts[a].at[j], self.send_sems.at[k], self.recv_sems.at[k],
                    device_id=(x ^ fx, y ^ fy, c), device_id_type=MESH))
        return out

    def start(self):
        for cp in self._copies():
            cp.start()

    def finish(self):
        for cp in self._copies():
            cp.wait()


def _mix_fwd(x, mod, g_pre, g_post, w_in_t, sgn, w_sp, b_sp_t, w_pool, p_scale, w_out_b, ts, shards, shard_shapes,
             shard_dtypes):
    t_len = x.shape[0]
    nt, nb = t_len // ts, ts // HEAD
    ns = len(shards)

    def body(*refs):
        (x_ref, mod_ref, g1_ref, g2_ref, win_ref, sgn_ref, ws_ref, bst_ref, wp_ref, ps_ref, wout_ref) = refs[:11]
        x1_ref, proj_ref, mixed_ref = refs[11 + ns:14 + ns]
        pbuf, cat = refs[14 + 2 * ns:16 + 2 * ns]
        gather = _GatherSteps(refs[11:11 + ns], refs[14 + ns:14 + 2 * ns], refs[16 + 2 * ns:16 + 3 * ns],
                              *refs[16 + 3 * ns:])
        i = pl.program_id(0)

        @pl.when(i == 0)
        def _():
            pbuf[0:POOL_HALO, :] = jnp.zeros((POOL_HALO, B_WIDTH), F32)
            gather.start()

        @pl.when(i == (3 * nt) // 4)
        def _():
            gather.forward()

        xv = x_ref[...]
        sh, sc, gm = mod_ref[0:1, :], mod_ref[1:2, :], mod_ref[2:3, :]
        h = (xv * _rstd(xv) * g1_ref[...]) * (1.0 + sc) + sh
        proj_ref[...] = _mm_nt(_bf(h), win_ref[...])
        pbuf[POOL_HALO:POOL_HALO + ts, :] = proj_ref[:, 2 * A_WIDTH:]
        smask = _sgu_mask()
        for hd in range(N_HEAD):
            u = _gelu(proj_ref[:, hd * HEAD:(hd + 1) * HEAD])
            v = _gelu(proj_ref[:, A_WIDTH + hd * HEAD:A_WIDTH + (hd + 1) * HEAD])
            vn = _bf(v * _rstd(v) * sgn_ref[hd:hd + 1, :])
            wm = _bf(jnp.where(smask, ws_ref[hd], 0.0))
            bias = bst_ref[:, hd:hd + 1]
            for b in range(nb):
                rows = slice(b * HEAD, (b + 1) * HEAD)
                z = _mm(wm, vn[rows]) + bias
                cat[rows, hd * HEAD:(hd + 1) * HEAD] = _bf(u[rows] * z)
        for g, w in enumerate(WINDOWS):
            cols = slice(g * HEAD, (g + 1) * HEAD)
            ext = pbuf[:, cols]
            pooled = _window_sum(ext, w, True)[POOL_HALO:] * _inv_count(i * ts, ts, w) - ext[POOL_HALO:]
            cat[:, A_WIDTH + g * HEAD:A_WIDTH + (g + 1) * HEAD] = _bf(_mm(_bf(pooled), _bf(wp_ref[g])) * ps_ref[:, cols])
        pbuf[0:POOL_HALO, :] = pbuf[ts:ts + POOL_HALO, :]
        mixed = _mm(cat[...], wout_ref[...])
        mixed_ref[...] = mixed
        x1_ref[...] = xv + gm * (mixed * _rstd(mixed) * g2_ref[...])

        @pl.when(i == nt - 1)
        def _():
            gather.finish()

    tile = lambda wid: pl.BlockSpec((ts, wid), lambda i: (i, 0))
    outs = pl.pallas_call(
        body,
        name="mix_fwd",
        grid=(nt,),
        out_shape=(jax.ShapeDtypeStruct((t_len, D), F32), jax.ShapeDtypeStruct((t_len, IN_WIDTH), F32),
                   jax.ShapeDtypeStruct((t_len, D), F32)) + _gather_out_shapes(shard_shapes, shard_dtypes),
        in_specs=[tile(D)] + [VMEM_SPEC] * (10 + ns),
        out_specs=(tile(D), tile(IN_WIDTH), tile(D)) + (ANY_SPEC,) * ns,
        scratch_shapes=[pltpu.VMEM((POOL_HALO + ts, B_WIDTH), F32), pltpu.VMEM((ts, D), BF16)]
        + _gather_scratch(shard_shapes, shard_dtypes),
        compiler_params=pltpu.CompilerParams(dimension_semantics=("arbitrary",), vmem_limit_bytes=VMEM_LIMIT_V7X),
    )(x, mod, g_pre, g_post, w_in_t, sgn, w_sp, b_sp_t, w_pool, p_scale, w_out_b, *shards)
    return outs[:3], outs[3:]


def _ffn_fwd(x1, target, mod, g_pre, g_post, w_up_b, conv_w8, conv_b8, w_down_b, ts):
    t_len = x1.shape[0]
    nt = t_len // ts

    def body(x1_ref, tgt_ref, mod_ref, g3_ref, g4_ref, wup_ref, cw_ref, cb_ref, wdown_ref,
             up_ref, f_ref, dx2_ref, loss_ref, ucarry):
        i = pl.program_id(0)

        @pl.when(i == 0)
        def _():
            ucarry[...] = jnp.zeros(ucarry.shape, F32)
            loss_ref[...] = jnp.zeros(loss_ref.shape, F32)

        x1v = x1_ref[...]
        sh, sc, gf = mod_ref[3:4, :], mod_ref[4:5, :], mod_ref[5:6, :]
        h2 = _bf((x1v * _rstd(x1v) * g3_ref[...]) * (1.0 + sc) + sh)
        half = N_DEV // 2

        def up_pair(j):
            return [_mm(h2, wup_ref[jj]) for jj in (j, j + half)]

        f = jnp.zeros((ts, D), F32)
        ups = up_pair(0)
        for j in range(half):
            nxt = up_pair(j + 1) if j + 1 < half else None
            ys = []
            for up, jj in zip(ups, (j, j + half)):
                up_ref[jj] = up
                before = ucarry[jj]
                ucarry[jj] = up[ts - HALO:, :]
                cw = cw_ref[jj]
                ys.append(cb_ref[jj:jj + 1, :] + _shift_down(up, before, 2) * cw[0:1, :]
                          + _shift_down(up, before, 1) * cw[1:2, :] + up * cw[2:3, :])
            gate, val = ys
            act = gate * _sigmoid(gate) * val
            f = f + _mm(_bf(act), wdown_ref[j * FF_CHUNK:(j + 1) * FF_CHUNK, :])
            ups = nxt
        f_ref[...] = f
        x2 = x1v + gf * (f * _rstd(f) * g4_ref[...])
        err = x2 - tgt_ref[...]
        loss_ref[...] += _sum0(err * err)
        dx2_ref[...] = err * (1.0 / D)

    tile = pl.BlockSpec((ts, D), lambda i: (i, 0))
    return pl.pallas_call(
        body,
        name="ffn_fwd",
        grid=(nt,),
        out_shape=(jax.ShapeDtypeStruct((N_DEV, t_len, FF_CHUNK), F32), jax.ShapeDtypeStruct((t_len, D), F32),
                   jax.ShapeDtypeStruct((t_len, D), F32), jax.ShapeDtypeStruct((1, D), F32)),
        in_specs=[tile, tile] + [VMEM_SPEC] * 7,
        out_specs=(pl.BlockSpec((N_DEV, ts, FF_CHUNK), lambda i: (0, i, 0)), tile, tile,
                   pl.BlockSpec((1, D), lambda i: (0, 0))),
        scratch_shapes=[pltpu.VMEM((N_DEV, HALO, FF_CHUNK), F32)],
        compiler_params=pltpu.CompilerParams(dimension_semantics=("arbitrary",), vmem_limit_bytes=VMEM_LIMIT_V7X),
    )(x1, target, mod, g_pre, g_post, w_up_b, conv_w8, conv_b8, w_down_b)


def _ffn_bwd(dx2, f, x1, up, mod, g_pre, g_post, w_up_b, conv_w8, conv_b8, w_down_b, ts):
    t_len = x1.shape[0]
    nt = t_len // ts
    half = N_DEV // 2

    def body(dx2_ref, f_ref, x1_ref, up_ref, halo_ref, mod_ref, g3_ref, g4_ref, wup_ref, cw_ref, cb_ref, wdown_ref,
             dx1_ref, dup_ref, act_ref, df_ref, h2_ref, dmod_ref, dg3_ref, dg4_ref, dcb_ref, dcw_ref,
             dycarry, dh2acc):
        i = pl.program_id(0)
        r = nt - 1 - i

        @pl.when(i == 0)
        def _():
            for ref in (dmod_ref, dg3_ref, dg4_ref, dcb_ref, dcw_ref, dycarry):
                ref[...] = jnp.zeros(ref.shape, F32)

        dx2v, fv, x1v = dx2_ref[...], f_ref[...], x1_ref[...]
        sh, sc, gf = mod_ref[3:4, :], mod_ref[4:5, :], mod_ref[5:6, :]
        g3, g4 = g3_ref[...], g4_ref[...]
        rstd4 = _rstd(fv)
        fh = fv * rstd4
        dmod_ref[2:3, :] += _sum0(dx2v * (fh * g4))
        dr = dx2v * gf
        dg4_ref[...] += _sum0(dr * fh)
        dfh = dr * g4
        dfb = _bf(rstd4 * (dfh - fh * _rowmean(dfh * fh)))
        df_ref[...] = dfb
        rstd3 = _rstd(x1v)
        xh = x1v * rstd3
        n3 = xh * g3
        h2_ref[...] = _bf(n3 * (1.0 + sc) + sh)
        dh2acc[...] = jnp.zeros((ts, D), F32)
        keep = jnp.where(r > 0, 1.0, 0.0).astype(F32)

        def dact_of(j):
            return _mm_nt(dfb, wdown_ref[j * FF_CHUNK:(j + 1) * FF_CHUNK, :])

        dact_next = dact_of(0)
        for j in range(half):
            dact = dact_next
            if j + 1 < half:
                dact_next = dact_of(j + 1)
            ys = []
            for jj in (j, j + half):
                before = halo_ref[jj] * keep
                upc = up_ref[jj]
                cw = cw_ref[jj]
                ys.append(cb_ref[jj:jj + 1, :] + _shift_down(upc, before, 2) * cw[0:1, :]
                          + _shift_down(upc, before, 1) * cw[1:2, :] + upc * cw[2:3, :])
            gate, val = ys
            sg = _sigmoid(gate)
            silu = gate * sg
            act_ref[j] = _bf(silu * val)
            dys = (dact * val * (sg + silu * (1.0 - sg)), dact * silu)
            for q, jj in enumerate((j, j + half)):
                dy = dys[q]
                cw = cw_ref[jj]
                dcb_ref[jj:jj + 1, :] += _sum0(dy)
                after = dycarry[jj]
                dycarry[jj] = dy[0:HALO, :]
                dy1, dy2 = _shift_up(dy, after, 1), _shift_up(dy, after, 2)
                upc = up_ref[jj]
                dcw_ref[jj, 0:1, :] += _sum0(dy2 * upc)
                dcw_ref[jj, 1:2, :] += _sum0(dy1 * upc)
                dcw_ref[jj, 2:3, :] += _sum0(dy * upc)
                dup = _bf(dy * cw[2:3, :] + dy1 * cw[1:2, :] + dy2 * cw[0:1, :])
                dup_ref[jj] = dup
                dh2acc[...] += _mm_nt(dup, wup_ref[jj])
        dh2 = dh2acc[...]
        dmod_ref[0:1, :] += _sum0(dh2)
        dmod_ref[1:2, :] += _sum0(dh2 * n3)
        dn3 = dh2 * (1.0 + sc)
        dg3_ref[...] += _sum0(dn3 * xh)
        dxh = dn3 * g3
        dx1_ref[...] = dx2v + rstd3 * (dxh - xh * _rowmean(dxh * xh))

    tile = pl.BlockSpec((ts, D), lambda i: (nt - 1 - i, 0))
    chunked = lambda n: pl.BlockSpec((n, ts, FF_CHUNK), lambda i: (0, nt - 1 - i, 0))
    halo = pl.BlockSpec((N_DEV, HALO, FF_CHUNK), lambda i: (0, jnp.maximum((nt - 1 - i) * (ts // HALO) - 1, 0), 0))
    const = lambda *shape: pl.BlockSpec(shape, lambda i: (0,) * len(shape))
    return pl.pallas_call(
        body,
        name="ffn_bwd",
        grid=(nt,),
        out_shape=(jax.ShapeDtypeStruct((t_len, D), F32), jax.ShapeDtypeStruct((N_DEV, t_len, FF_CHUNK), BF16),
                   jax.ShapeDtypeStruct((half, t_len, FF_CHUNK), BF16), jax.ShapeDtypeStruct((t_len, D), BF16),
                   jax.ShapeDtypeStruct((t_len, D), BF16), jax.ShapeDtypeStruct((3, D), F32),
                   jax.ShapeDtypeStruct((1, D), F32), jax.ShapeDtypeStruct((1, D), F32),
                   jax.ShapeDtypeStruct((N_DEV, FF_CHUNK), F32), jax.ShapeDtypeStruct((N_DEV, 3, FF_CHUNK), F32)),
        in_specs=[tile, tile, tile, chunked(N_DEV), halo] + [VMEM_SPEC] * 7,
        out_specs=(tile, chunked(N_DEV), chunked(half), tile, tile, const(3, D), const(1, D), const(1, D),
                   const(N_DEV, FF_CHUNK), const(N_DEV, 3, FF_CHUNK)),
        scratch_shapes=[pltpu.VMEM((N_DEV, HALO, FF_CHUNK), F32), pltpu.VMEM((ts, D), F32)],
        compiler_params=pltpu.CompilerParams(dimension_semantics=("arbitrary",), vmem_limit_bytes=VMEM_LIMIT_V7X),
    )(dx2, f, x1, up, up, mod, g_pre, g_post, w_up_b, conv_w8, conv_b8, w_down_b)


def _wgrad_up(h2, dup, ts):
    t_len = h2.shape[0]
    nt, half = t_len // ts, N_DEV // 2

    def body(h2_ref, dup_ref, out_ref):
        @pl.when(pl.program_id(1) == 0)
        def _():
            out_ref[...] = jnp.zeros(out_ref.shape, F32)

        for q in range(half):
            out_ref[q] += _mm_tn(dup_ref[q], h2_ref[...])

    return pl.pallas_call(
        body,
        name="wgrad_up",
        grid=(2, nt),
        out_shape=jax.ShapeDtypeStruct((N_DEV, FF_CHUNK, D), F32),
        in_specs=[pl.BlockSpec((ts, D), lambda g, t: (t, 0)), pl.BlockSpec((half, ts, FF_CHUNK), lambda g, t: (g, t, 0))],
        out_specs=pl.BlockSpec((half, FF_CHUNK, D), lambda g, t: (g, 0, 0)),
        compiler_params=pltpu.CompilerParams(dimension_semantics=("arbitrary", "arbitrary"),
                                             vmem_limit_bytes=VMEM_LIMIT_V7X),
    )(h2, dup)


def _sibling_swap_copies(srcs, dsts, send_sems, recv_sems):
    x, y, c = _coords()
    return [
        pltpu.make_async_remote_copy(srcs[a].at[xs, ys, 1 - c], dsts[a].at[xs, ys], send_sems.at[a, 2 * xs + ys],
                                     recv_sems.at[a, 2 * xs + ys], device_id=(x, y, 1 - c), device_id_type=MESH)
        for a in range(len(srcs)) for xs in range(2) for ys in range(2)
    ]


def _wgrad_down(act, df, ts, swap_src):
    t_len = df.shape[0]
    nt, half = t_len // ts, N_DEV // 2

    def body(act_ref, df_ref, src_ref, out_ref, dst_ref, send_sems, recv_sems):
        t = pl.program_id(0)

        @pl.when(t == 0)
        def _():
            for cp in _sibling_swap_copies([src_ref], [dst_ref], send_sems, recv_sems):
                cp.start()
            out_ref[...] = jnp.zeros(out_ref.shape, F32)

        for q in range(half):
            out_ref[q] += _mm_tn(act_ref[q], df_ref[...])

        @pl.when(t == nt - 1)
        def _():
            for cp in _sibling_swap_copies([src_ref], [dst_ref], send_sems, recv_sems):
                cp.wait()

    return pl.pallas_call(
        body,
        name="wgrad_down",
        grid=(nt,),
        out_shape=(jax.ShapeDtypeStruct((half, FF_CHUNK, D), F32), jax.ShapeDtypeStruct(swap_src.shape[1:], F32)),
        in_specs=[pl.BlockSpec((half, ts, FF_CHUNK), lambda t: (0, t, 0)), pl.BlockSpec((ts, D), lambda t: (t, 0)),
                  ANY_SPEC],
        out_specs=(pl.BlockSpec((half, FF_CHUNK, D), lambda t: (0, 0, 0)), ANY_SPEC),
        scratch_shapes=[pltpu.SemaphoreType.DMA((1, 4)), pltpu.SemaphoreType.DMA((1, 4))],
        compiler_params=pltpu.CompilerParams(dimension_semantics=("arbitrary",), vmem_limit_bytes=VMEM_LIMIT_V7X),
    )(act, df, swap_src)


def _mix_bwd(dx1, x, proj, mixed, mod, g_pre, g_post, w_in_t, sgn, w_sp, b_sp_t, w_pool, p_scale, w_out_b, ts, rs_srcs):
    t_len = x.shape[0]
    nt, nb = t_len // ts, ts // HEAD
    nr = len(rs_srcs)

    def body(*refs):
        (dx1_ref, x_ref, proj_ref, halo_ref, mixed_ref, mod_ref, g1_ref, g2_ref, win_ref, sgn_ref, ws_ref,
         bst_ref, wp_ref, ps_ref, wout_ref) = refs[:15]
        (gx_ref, dwin_ref, dwout_ref, dmod_ref, dg1_ref, dg2_ref, dsgn_ref, dws_ref, dbst_ref, dwp_ref,
         dps_ref) = refs[15 + nr:26 + nr]
        pbuf, dwsbuf, cat, dproj, dcat = refs[26 + 2 * nr:31 + 2 * nr]
        exchange = _ChipExchangeSteps(refs[15:15 + nr], refs[26 + nr:26 + 2 * nr], *refs[31 + 2 * nr:])
        i = pl.program_id(0)
        r = nt - 1 - i

        @pl.when(i == 0)
        def _():
            exchange.start()
            for ref in (dwin_ref, dwout_ref, dmod_ref, dg1_ref, dg2_ref, dsgn_ref, dws_ref, dbst_ref, dwp_ref, dps_ref):
                ref[...] = jnp.zeros(ref.shape, F32)
            dwsbuf[ts:ts + POOL_HALO, :] = jnp.zeros((POOL_HALO, B_WIDTH), F32)

        xv, dx1v, mixed = x_ref[...], dx1_ref[...], mixed_ref[...]
        sh, sc, gm = mod_ref[0:1, :], mod_ref[1:2, :], mod_ref[2:3, :]
        g1, g2 = g1_ref[...], g2_ref[...]
        rstd2 = _rstd(mixed)
        mh = mixed * rstd2
        dmod_ref[2:3, :] += _sum0(dx1v * (mh * g2))
        dr = dx1v * gm
        dg2_ref[...] += _sum0(dr * mh)
        dmh = dr * g2
        dmb = _bf(rstd2 * (dmh - mh * _rowmean(dmh * mh)))
        dcat[...] = _mm_nt(dmb, wout_ref[...])
        smask = _sgu_mask()
        for hd in range(N_HEAD):
            ucols = slice(hd * HEAD, (hd + 1) * HEAD)
            vcols = slice(A_WIDTH + hd * HEAD, A_WIDTH + (hd + 1) * HEAD)
            u, du_dp = _gelu_and_grad(proj_ref[:, ucols])
            v, dv_dp = _gelu_and_grad(proj_ref[:, vcols])
            rs = _rstd(v)
            vhat = v * rs
            gn = sgn_ref[hd:hd + 1, :]
            vn = _bf(vhat * gn)
            wm = _bf(jnp.where(smask, ws_ref[hd], 0.0))
            bias = bst_ref[:, hd:hd + 1]
            dzsum = jnp.zeros((HEAD, HEAD), F32)
            dwm = jnp.zeros((HEAD, HEAD), F32)
            dvn_parts = []
            for b in range(nb):
                rows = slice(b * HEAD, (b + 1) * HEAD)
                z = _mm(wm, vn[rows]) + bias
                da = dcat[rows, ucols]
                cat[rows, ucols] = _bf(u[rows] * z)
                dz = da * u[rows]
                dzsum = dzsum + dz
                dzb = _bf(dz)
                dwm = dwm + _mm_nt(dzb, vn[rows])
                dvn_parts.append(_mm_tn(wm, dzb))
                dproj[rows, ucols] = _bf((da * z) * du_dp[rows])
            dvn = jnp.concatenate(dvn_parts, axis=0)
            dsgn_ref[hd:hd + 1, :] += _sum0(dvn * vhat)
            dvh = dvn * gn
            dproj[:, vcols] = _bf((rs * (dvh - vhat * _rowmean(dvh * vhat))) * dv_dp)
            dws_ref[hd] += jnp.where(smask, dwm, 0.0)
            dbst_ref[:, hd:hd + 1] += jnp.sum(dzsum, axis=1, keepdims=True)
        keep = jnp.where(r > 0, 1.0, 0.0).astype(F32)
        pbuf[0:POOL_HALO, :] = halo_ref[...] * keep
        pbuf[POOL_HALO:POOL_HALO + ts, :] = proj_ref[:, 2 * A_WIDTH:]
        for g, w in enumerate(WINDOWS):
            cols = slice(g * HEAD, (g + 1) * HEAD)
            ccols = slice(A_WIDTH + g * HEAD, A_WIDTH + (g + 1) * HEAD)
            pcols = slice(2 * A_WIDTH + g * HEAD, 2 * A_WIDTH + (g + 1) * HEAD)
            wpg = _bf(wp_ref[g])
            psg = ps_ref[:, cols]
            ext = pbuf[:, cols]
            inv = _inv_count(r * ts, ts, w)
            pb = _bf(_window_sum(ext, w, True)[POOL_HALO:] * inv - ext[POOL_HALO:])
            yb = _mm(pb, wpg)
            dob = dcat[:, ccols]
            cat[:, ccols] = _bf(yb * psg)
            dps_ref[:, cols] += _sum0(dob * yb)
            dyb = _bf(dob * psg)
            dwp_ref[g] += _mm_tn(pb, dyb)
            dpooled = _mm_nt(dyb, wpg)
            dwsbuf[0:ts, cols] = dpooled * inv
            dproj[:, pcols] = _bf(_window_sum(dwsbuf[:, cols], w, False)[0:ts] - dpooled)
        dwsbuf[ts:ts + POOL_HALO, :] = dwsbuf[0:POOL_HALO, :]
        dpb = dproj[...]
        rstd1 = _rstd(xv)
        xh = xv * rstd1
        n1 = xh * g1
        dwin_ref[...] += _mm_tn(dpb, _bf(n1 * (1.0 + sc) + sh))
        dwout_ref[...] += _mm_tn(cat[...], dmb)
        dh = _mm(dpb, win_ref[...])
        dmod_ref[0:1, :] += _sum0(dh)
        dmod_ref[1:2, :] += _sum0(dh * n1)
        dn1 = dh * (1.0 + sc)
        dg1_ref[...] += _sum0(dn1 * xh)
        dxh = dn1 * g1
        gx_ref[...] = dx1v + rstd1 * (dxh - xh * _rowmean(dxh * xh))

        @pl.when(i == nt - 1)
        def _():
            exchange.finish()

    tile = lambda wid: pl.BlockSpec((ts, wid), lambda i: (nt - 1 - i, 0))
    halo = pl.BlockSpec((POOL_HALO, B_WIDTH),
                        lambda i: (jnp.maximum((nt - 1 - i) * (ts // POOL_HALO) - 1, 0), 2 * A_WIDTH // B_WIDTH))
    const = lambda *shape: pl.BlockSpec(shape, lambda i: (0,) * len(shape))
    resident = lambda *shape: pl.BlockSpec(shape, lambda i: (0,) * len(shape), pipeline_mode=pl.Buffered(1))
    outs = pl.pallas_call(
        body,
        name="mix_bwd",
        grid=(nt,),
        out_shape=(jax.ShapeDtypeStruct((t_len, D), F32), jax.ShapeDtypeStruct((IN_WIDTH, D), F32),
                   jax.ShapeDtypeStruct((D, D), F32), jax.ShapeDtypeStruct((3, D), F32),
                   jax.ShapeDtypeStruct((1, D), F32), jax.ShapeDtypeStruct((1, D), F32),
                   jax.ShapeDtypeStruct((N_HEAD, HEAD), F32), jax.ShapeDtypeStruct((N_HEAD, HEAD, HEAD), F32),
                   jax.ShapeDtypeStruct((HEAD, N_HEAD), F32), jax.ShapeDtypeStruct((N_HEAD, HEAD, HEAD), F32),
                   jax.ShapeDtypeStruct((1, B_WIDTH), F32))
        + tuple(jax.ShapeDtypeStruct((3, *s.shape[2:]), s.dtype) for s in rs_srcs),
        in_specs=[tile(D), tile(D), tile(IN_WIDTH), halo, tile(D)] + [VMEM_SPEC] * 10 + [ANY_SPEC] * nr,
        out_specs=(tile(D), resident(IN_WIDTH, D), resident(D, D), const(3, D), const(1, D), const(1, D),
                   const(N_HEAD, HEAD), const(N_HEAD, HEAD, HEAD), const(HEAD, N_HEAD), const(N_HEAD, HEAD, HEAD),
                   const(1, B_WIDTH)) + (ANY_SPEC,) * nr,
        scratch_shapes=[pltpu.VMEM((POOL_HALO + ts, B_WIDTH), F32), pltpu.VMEM((ts + POOL_HALO, B_WIDTH), F32),
                        pltpu.VMEM((ts, D), BF16), pltpu.VMEM((ts, IN_WIDTH), BF16), pltpu.VMEM((ts, D), F32),
                        pltpu.SemaphoreType.DMA((3 * nr,)), pltpu.SemaphoreType.DMA((3 * nr,))],
        compiler_params=pltpu.CompilerParams(dimension_semantics=("arbitrary",), vmem_limit_bytes=VMEM_LIMIT_V7X),
    )(dx1, x, proj, proj, mixed, mod, g_pre, g_post, w_in_t, sgn, w_sp, b_sp_t, w_pool, p_scale, w_out_b, *rs_srcs)
    return outs[:11], outs[11:]


def _pair_add(name, coords, grid, specs_a, specs_b, out_specs, out_shapes, a_arrays, b_arrays, swap_srcs):
    n, ns = len(a_arrays), len(swap_srcs)
    last = tuple(g - 1 for g in grid)

    def body(co_ref, *refs):
        ids = [pl.program_id(d) for d in range(len(grid))]
        swap = refs[2 * n:2 * n + ns], refs[4 * n + ns:4 * n + 2 * ns], *refs[4 * n + 2 * ns:]
        if ns:
            @pl.when(functools.reduce(jnp.logical_and, [i == 0 for i in ids]))
            def _():
                for cp in _sibling_swap_copies(*swap):
                    cp.start()

        for k in range(n):
            total = refs[k][...] + refs[n + k][...]
            refs[2 * n + ns + k][...] = total
            refs[3 * n + ns + k][...] = _bf(total)

        if ns:
            @pl.when(functools.reduce(jnp.logical_and, [i == e for i, e in zip(ids, last)]))
            def _():
                for cp in _sibling_swap_copies(*swap):
                    cp.wait()

    outs = pl.pallas_call(
        body,
        name=name,
        grid_spec=pltpu.PrefetchScalarGridSpec(
            num_scalar_prefetch=1, grid=grid, in_specs=specs_a + specs_b + [ANY_SPEC] * ns,
            out_specs=out_specs * 2 + [ANY_SPEC] * ns,
            scratch_shapes=[pltpu.SemaphoreType.DMA((ns, 4)), pltpu.SemaphoreType.DMA((ns, 4))] if ns else []),
        out_shape=tuple(jax.ShapeDtypeStruct(s, dt) for dt in (F32, BF16) for s in out_shapes)
        + tuple(jax.ShapeDtypeStruct(g.shape[1:], F32) for g in swap_srcs),
        compiler_params=pltpu.CompilerParams(dimension_semantics=("arbitrary",) * len(grid),
                                             vmem_limit_bytes=VMEM_LIMIT_V7X),
    )(coords, *a_arrays, *b_arrays, *swap_srcs)
    return list(outs[:n]), list(outs[n:2 * n]), list(outs[2 * n:])


def _final_add_adamw(coords, s1, r, ws, ms, vs, n_split=4):
    n = len(s1)

    def body(co_ref, *refs):
        for k in range(n):
            s_ref, r_ref, w_ref, m_ref, v_ref = (refs[q * n + k] for q in range(5))
            g_ref, d_ref, nm_ref, nv_ref = (refs[(5 + q) * n + k] for q in range(4))
            g = ((s_ref[...] + r_ref[0].astype(F32)) + r_ref[1].astype(F32)) + r_ref[2].astype(F32)
            g_ref[...] = g
            delta, m, v = _adamw(w_ref[...], g, m_ref[...], v_ref[...])
            d_ref[...] = delta
            nm_ref[...] = m
            nv_ref[...] = v

    def shard_spec(a):
        rows, cols = a.shape
        return pl.BlockSpec((rows // n_split, cols), lambda i, co: (i, 0))

    def mine_spec(a):
        rows, cols = a.shape[2:]
        return pl.BlockSpec((None, None, rows // n_split, cols), lambda i, co: (co[0], co[1], i, 0))

    def recv_spec(a):
        rows, cols = a.shape[1:]
        return pl.BlockSpec((3, rows // n_split, cols), lambda i, co: (0, i, 0))

    in_specs = ([mine_spec(a) for a in s1] + [recv_spec(a) for a in r] + [shard_spec(a) for a in ws] * 3)
    out_specs = [shard_spec(a) for a in ws] * 4
    outs = pl.pallas_call(
        body,
        name="grad_final_adamw",
        grid_spec=pltpu.PrefetchScalarGridSpec(num_scalar_prefetch=1, grid=(n_split,), in_specs=in_specs,
                                               out_specs=out_specs),
        out_shape=tuple(jax.ShapeDtypeStruct(a.shape, F32) for a in ws) * 4,
        compiler_params=pltpu.CompilerParams(dimension_semantics=("arbitrary",), vmem_limit_bytes=VMEM_LIMIT_V7X),
    )(coords, *s1, *r, *ws, *ms, *vs)
    return [tuple(outs[q * n + k] for q in range(4)) for k in range(n)]


def _sibling_add(tag, g5, r1, coords, swap_srcs=(), n_split=4):
    shapes = [g.shape[3:] for g in g5]
    spec_g = [pl.BlockSpec((None, None, None, s[0] // n_split, s[1]), lambda i, j, k, co: (i, j, co[2], k, 0))
              for s in shapes]
    spec_r = [pl.BlockSpec((None, None, s[0] // n_split, s[1]), lambda i, j, k, co: (i, j, k, 0)) for s in shapes]
    return _pair_add("grad_add_core_" + tag, coords, (2, 2, n_split), spec_g, spec_r, spec_r,
                     [(2, 2, *s) for s in shapes], g5, r1, list(swap_srcs))


def _tail_exchange(big, partials, pick_mine, y_first, dmod3):
    n, nb = len(partials), len(big)
    big_shapes = [g.shape[1:] for g in big]
    big5 = [g.reshape(2, 2, 2, *s) for g, s in zip(big, big_shapes)]
    flips = _ChipExchangeSteps.FLIPS

    def body(*refs):
        g5, p_in, dm_ref = refs[:nb], refs[nb:nb + n], refs[nb + n]
        outs = refs[nb + n + 1:2 * nb + 2 * n + 2]
        g_out, sums, dm2d = outs[:nb], outs[nb:nb + n], outs[nb + n]
        scratch = refs[2 * nb + 2 * n + 2:]
        s1, stage, chip_recv = scratch[:nb], scratch[nb:2 * nb], scratch[2 * nb:3 * nb]
        acc, rbuf = scratch[3 * nb:3 * nb + n], scratch[3 * nb + n:3 * nb + 2 * n]
        (dm_recv, send_sems, recv_sems, dm_send_sems, dm_recv_sems, sib_send, sib_recv, chip_send,
         chip_recv_sems) = scratch[3 * nb + 2 * n:]
        x, y, c = _coords()
        me = 4 * x + 2 * y + c
        sibling = (x, y, 1 - c)
        dm_copies = [
            pltpu.make_async_remote_copy(dm_ref.at[me ^ k], dm_recv.at[k], dm_send_sems.at[k], dm_recv_sems.at[k],
                                         device_id=_peer(k), device_id_type=MESH)
            for k in range(1, N_DEV)
        ]
        for cp in dm_copies:
            cp.start()
        sib_copies = _sibling_swap_copies(g5, s1, sib_send, sib_recv)
        for cp in sib_copies:
            cp.start()
        for a in range(n):
            acc[a][...] = p_in[a][...]

        def small_phase(ph, peers):
            copies = [
                pltpu.make_async_remote_copy(acc[a], rbuf[a].at[ph], send_sems.at[ph, a], recv_sems.at[ph, a],
                                             device_id=peers[y_first[a]], device_id_type=MESH)
                for a in range(n)
            ]
            for cp in copies:
                cp.start()
            for cp in copies:
                cp.wait()
            for a in range(n):
                acc[a][...] = acc[a][...] + rbuf[a][ph]

        small_phase(0, (sibling, sibling))
        for cp in sib_copies:
            cp.wait()
        for a in range(nb):
            for xs in range(2):
                for ys in range(2):
                    total = g5[a][xs, ys, c] + s1[a][xs, ys]
                    s1[a][xs, ys] = total
                    stage[a][xs, ys] = _bf(total)
        chip_copies = [
            pltpu.make_async_remote_copy(stage[a].at[x ^ fx, y ^ fy], chip_recv[a].at[j], chip_send.at[a, j],
                                         chip_recv_sems.at[a, j], device_id=(x ^ fx, y ^ fy, c), device_id_type=MESH)
            for a in range(nb) for j, (fx, fy) in enumerate(flips)
        ]
        for cp in chip_copies:
            cp.start()
        x_peer, y_peer = (1 - x, y, c), (x, 1 - y, c)
        small_phase(1, (x_peer, y_peer))
        small_phase(2, (y_peer, x_peer))
        for a in range(n):
            sums[a][...] = acc[a][me] if pick_mine[a] else acc[a][...]
        dm2d[...] = jnp.zeros(dm2d.shape, F32)
        dm2d[0:1, :] = dm_ref[me]
        for cp in dm_copies:
            cp.wait()
        for k in range(1, N_DEV):
            dm2d[k:k + 1, :] = dm_recv[k]
        for cp in chip_copies:
            cp.wait()
        for a in range(nb):
            g_out[a][...] = ((s1[a][x, y] + chip_recv[a][0].astype(F32)) + chip_recv[a][1].astype(F32)) \
                + chip_recv[a][2].astype(F32)

    out_shapes = tuple(jax.ShapeDtypeStruct(s, F32) for s in big_shapes) + tuple(
        jax.ShapeDtypeStruct(p.shape[1:] if pk else p.shape, F32) for p, pk in zip(partials, pick_mine))
    outs = pl.pallas_call(
        body,
        name="tail_exchange",
        out_shape=out_shapes + (jax.ShapeDtypeStruct((2 * N_DEV, MOD_COLS), F32),),
        in_specs=[VMEM_SPEC] * (nb + n + 1),
        out_specs=(VMEM_SPEC,) * (nb + n + 1),
        scratch_shapes=[pltpu.VMEM((2, 2, *s), F32) for s in big_shapes]
        + [pltpu.VMEM((2, 2, *s), BF16) for s in big_shapes]
        + [pltpu.VMEM((3, *s), BF16) for s in big_shapes]
        + [pltpu.VMEM(p.shape, F32) for p in partials]
        + [pltpu.VMEM((3, *p.shape), F32) for p in partials]
        + [pltpu.VMEM((N_DEV, 1, MOD_COLS), F32), pltpu.SemaphoreType.DMA((3, n)), pltpu.SemaphoreType.DMA((3, n)),
           pltpu.SemaphoreType.DMA((N_DEV,)), pltpu.SemaphoreType.DMA((N_DEV,)),
           pltpu.SemaphoreType.DMA((nb, 4)), pltpu.SemaphoreType.DMA((nb, 4)),
           pltpu.SemaphoreType.DMA((nb, 3)), pltpu.SemaphoreType.DMA((nb, 3))],
        compiler_params=pltpu.CompilerParams(vmem_limit_bytes=VMEM_LIMIT_V7X),
    )(*big5, *partials, dmod3)
    return list(outs[:nb]), list(outs[nb:nb + n]), outs[nb + n]


def _small_update(grads, ws, ms, vs, scx, dm2d, w_ada, m_ada, v_ada, loss_lanes):
    n = len(grads)

    def body(*refs):
        g_in, w_in, m_in, v_in = (refs[q * n:(q + 1) * n] for q in range(4))
        scx_ref, dm_ref, wa_ref, ma_ref, va_ref, ll_ref = refs[4 * n:4 * n + 6]
        outs = refs[4 * n + 6:]
        g_out, d_out, nm_out, nv_out = (outs[q * (n + 1):(q + 1) * (n + 1)] for q in range(4))
        loss_ref = outs[4 * (n + 1)]
        for a in range(n + 1):
            if a < n:
                g, w, m, v = g_in[a][...], w_in[a][...], m_in[a][...], v_in[a][...]
            else:
                g = _mm_tn(_bf(scx_ref[...]), _bf(dm_ref[...]))
                w, m, v = wa_ref[...], ma_ref[...], va_ref[...]
            g_out[a][...] = g
            delta, m, v = _adamw(w, g, m, v)
            d_out[a][...] = delta
            nm_out[a][...] = m
            nv_out[a][...] = v
        loss_ref[...] = jnp.sum(ll_ref[...], axis=1, keepdims=True) * (0.5 / D)

    w_shapes = tuple(jax.ShapeDtypeStruct(w.shape, F32) for w in list(ws) + [w_ada])
    outs = pl.pallas_call(
        body,
        name="small_update",
        out_shape=w_shapes * 4 + (jax.ShapeDtypeStruct((1, 1), F32),),
        in_specs=[VMEM_SPEC] * (4 * n + 6),
        out_specs=(VMEM_SPEC,) * (4 * (n + 1) + 1),
        compiler_params=pltpu.CompilerParams(vmem_limit_bytes=VMEM_LIMIT_V7X),
    )(*grads, *ws, *ms, *vs, scx, dm2d, w_ada, m_ada, v_ada, loss_lanes)
    return [tuple(outs[q * (n + 1) + k] for q in range(4)) for k in range(n + 1)], outs[4 * (n + 1)]


def kernel(x, c, w_ada, b_ada, pre_mix_g, post_mix_g, w_in, sgu_norm_g, w_spatial, b_spatial, w_pool, pool_scale, w_out, pre_ffn_g, post_ffn_g, w_up, conv_w, conv_b, w_down, loss_target, m_w_ada, m_b_ada, m_pre_mix_g, m_post_mix_g, m_w_in, m_sgu_norm_g, m_w_spatial, m_b_spatial, m_w_pool, m_pool_scale, m_w_out, m_pre_ffn_g, m_post_ffn_g, m_w_up, m_conv_w, m_conv_b, m_w_down, v_w_ada, v_b_ada, v_pre_mix_g, v_post_mix_g, v_w_in, v_sgu_norm_g, v_w_spatial, v_b_spatial, v_w_pool, v_pool_scale, v_w_out, v_pre_ffn_g, v_post_ffn_g, v_w_up, v_conv_w, v_conv_b, v_w_down):
    t_len = x.shape[1]
    ts = min(256, t_len)
    ts_mix = min(512, t_len)
    ts_w = min(1024, t_len)
    coords = jnp.stack([lax.axis_index("x"), lax.axis_index("y"), lax.axis_index("c")]).astype(jnp.int32)

    w_in_t, w_up_t = w_in[0].T, w_up[0].T
    mod3, scx, (g_in, g_out, g_down) = _prologue(c, w_ada[0], b_ada.reshape(N_DEV, 1, MOD_COLS),
                                                 [w_in_t, w_out[0], w_down[0]], [BF16, BF16, BF16])
    mod = mod3.reshape(N_MOD, D)
    w_in_tb = g_in.reshape(IN_WIDTH, D)
    w_out_b = g_out.reshape(D, D)
    conv_b8 = conv_b.reshape(N_DEV, FF_CHUNK)
    b_sp_t = b_spatial[0].T

    x2d, tgt = x[0], loss_target[0]
    (x1, proj, mixed), (g_up, g_cw) = _mix_fwd(
        x2d, mod, pre_mix_g, post_mix_g, w_in_tb, sgu_norm_g[0], w_spatial[0], b_sp_t, w_pool[0], pool_scale, w_out_b,
        ts_mix, [w_up_t, conv_w[0]], [w_up.shape[1:], conv_w.shape[1:]], [BF16, F32])
    w_down_b = g_down.reshape(FF, D)
    up, f, dx2, loss_lanes = _ffn_fwd(x1, tgt, mod, pre_ffn_g, post_ffn_g, g_up, g_cw, conv_b8, w_down_b, ts)

    (dx1, dup, act, df, h2, dmod_f, d_pre_ffn, d_post_ffn, d_cb8, d_cw8) = _ffn_bwd(
        dx2, f, x1, up, mod, pre_ffn_g, post_ffn_g, g_up, g_cw, conv_b8, w_down_b, ts)
    gw_up = _wgrad_up(h2, dup, ts_w).reshape(2, 2, 2, FF_CHUNK, D)
    gw_down, r1_up = _wgrad_down(act, df, ts_w, gw_up)
    gw_down = gw_down.reshape(2, 2, 2, FF // N_DEV, D)
    s1_up, s1_up_b, r1_down = _sibling_add("up", [gw_up], [r1_up], coords, swap_srcs=[gw_down])
    s1_down, s1_down_b, _ = _sibling_add("down", [gw_down], r1_down, coords)
    s1_ffn, s1_ffn_b = s1_up + s1_down, s1_up_b + s1_down_b
    ((grad_x, gw_in, gw_out, dmod_m, d_pre_mix, d_post_mix, d_sgn, d_wsp, d_bsp_t, d_wpool, d_ps), r_ffn) = _mix_bwd(
        dx1, x2d, proj, mixed, mod, pre_mix_g, post_mix_g, w_in_tb, sgu_norm_g[0], w_spatial[0], b_sp_t,
        w_pool[0], pool_scale, w_out_b, ts_mix, s1_ffn_b)
    gw_in = gw_in.reshape(N_DEV, IN_WIDTH // N_DEV, D)
    gw_out = gw_out.reshape(N_DEV, D // N_DEV, D)

    big = _final_add_adamw(coords, s1_ffn, list(r_ffn), [w_up_t, w_down[0]], [m_w_up[0].T, m_w_down[0]],
                           [v_w_up[0].T, v_w_down[0]])
    r_up, r_down = tuple(a.T[None] for a in big[0]), tuple(a[None] for a in big[1])

    dmod = jnp.concatenate([dmod_m, dmod_f], axis=0)
    names = ["b_ada", "pre_mix_g", "post_mix_g", "sgu_norm_g", "w_spatial", "b_spatial", "w_pool", "pool_scale",
             "pre_ffn_g", "post_ffn_g", "conv_w", "conv_b"]
    partials = [dmod.reshape(1, N_MOD * D), d_pre_mix, d_post_mix, d_sgn, d_wsp, d_bsp_t.T, d_wpool, d_ps,
                d_pre_ffn, d_post_ffn, d_cw8, d_cb8.reshape(1, 2 * FF), loss_lanes]
    small_w = [b_ada, pre_mix_g, post_mix_g, sgu_norm_g[0], w_spatial[0], b_spatial[0], w_pool[0], pool_scale,
               pre_ffn_g, post_ffn_g, conv_w[0], conv_b]
    small_m = [m_b_ada, m_pre_mix_g, m_post_mix_g, m_sgu_norm_g[0], m_w_spatial[0], m_b_spatial[0], m_w_pool[0],
               m_pool_scale, m_pre_ffn_g, m_post_ffn_g, m_conv_w[0], m_conv_b]
    small_v = [v_b_ada, v_pre_mix_g, v_post_mix_g, v_sgu_norm_g[0], v_w_spatial[0], v_b_spatial[0], v_w_pool[0],
               v_pool_scale, v_pre_ffn_g, v_post_ffn_g, v_conv_w[0], v_conv_b]
    g_mix, sums, dm2d = _tail_exchange([gw_in, gw_out], partials, [nm == "conv_w" for nm in names] + [False],
                                       [int(nm == "w_pool") for nm in names] + [0],
                                       dmod.reshape(N_DEV, 1, MOD_COLS))
    small, loss11 = _small_update(
        sums[:-1] + g_mix, small_w + [w_in_t, w_out[0]], small_m + [m_w_in[0].T, m_w_out[0]],
        small_v + [v_w_in[0].T, v_w_out[0]], scx, dm2d, w_ada[0], m_w_ada[0], v_w_ada[0], sums[-1])
    loss = loss11.reshape(())
    lead = {"sgu_norm_g", "w_spatial", "b_spatial", "w_pool", "conv_w", "w_in", "w_out", "w_ada"}
    res = {nm: tuple((a.T if nm == "w_in" else a)[None] if nm in lead else a for a in four)
           for nm, four in zip(names + ["w_in", "w_out", "w_ada"], small)}
    res.update(w_up=r_up, w_down=r_down)

    order = ["w_ada", "b_ada", "pre_mix_g", "post_mix_g", "w_in", "sgu_norm_g", "w_spatial", "b_spatial", "w_pool",
             "pool_scale", "w_out", "pre_ffn_g", "post_ffn_g", "w_up", "conv_w", "conv_b", "w_down"]
    return (loss, grad_x[None], *[res[nm][0] for nm in order], *[res[nm][1] for nm in order],
            *[res[nm][2] for nm in order], *[res[nm][3] for nm in order])
```

```python
import functools
import math

import jax
import jax.numpy as jnp
from jax import lax
from jax.experimental import pallas as pl
from jax.experimental.pallas import tpu as pltpu

F32 = jnp.float32
BF16 = jnp.bfloat16
MESH = pl.DeviceIdType.MESH

EPS = 1e-6
D = 1024
HEAD = 128
N_HEAD = 4
A_WIDTH = 512
B_WIDTH = 512
IN_WIDTH = 1536
WINDOWS = (2, 4, 8, 16)
CHUNK = 64
FF = 2816
N_DEV = 8
FF_CHUNK = 704
N_MOD = 6
MOD_COLS = 768

ADAM_LR = 0.001
ADAM_B1 = 0.9
ADAM_B2 = 0.999
ADAM_EPS = 1e-08
ADAM_WD = 0.01
ADAM_STEP = 10

VMEM_LIMIT_V7X = 62 * 1024 * 1024
HALO = 8
POOL_HALO = 16

VMEM_SPEC = pl.BlockSpec(memory_space=pltpu.VMEM)
ANY_SPEC = pl.BlockSpec(memory_space=pl.ANY)


def _bf(x):
    return x.astype(BF16)


def _mm(a, b):
    return jnp.dot(a, b, preferred_element_type=F32)


def _mm_nt(a, b):
    return lax.dot_general(a, b, (((1,), (1,)), ((), ())), preferred_element_type=F32)


def _mm_tn(a, b):
    return lax.dot_general(a, b, (((0,), (0,)), ((), ())), preferred_element_type=F32)


def _rstd(x):
    return lax.rsqrt(jnp.mean(x * x, axis=-1, keepdims=True) + EPS)


def _sum0(x):
    return jnp.sum(x, axis=0, keepdims=True)


def _rowmean(x):
    return jnp.mean(x, axis=-1, keepdims=True)


_GELU_K = math.sqrt(2.0 / math.pi)


def _gelu_and_grad(x):
    x2 = x * x
    th = jnp.tanh(_GELU_K * (x + 0.044715 * (x * x2)))
    cdf = 0.5 * th + 0.5
    grad = cdf + x * (1.0 - th * th) * ((0.5 * _GELU_K) + (1.5 * 0.044715 * _GELU_K) * x2)
    return x * cdf, grad


def _gelu(x):
    return x * (0.5 * (1.0 + jnp.tanh(_GELU_K * (x + 0.044715 * (x * x * x)))))


def _sigmoid(x):
    return 0.5 * jnp.tanh(0.5 * x) + 0.5


def _sgu_mask():
    ri = lax.broadcasted_iota(jnp.int32, (HEAD, HEAD), 0)
    ci = lax.broadcasted_iota(jnp.int32, (HEAD, HEAD), 1)
    return (ci // CHUNK) <= (ri // CHUNK)


def _window_sum(ext, w, trailing):
    n = ext.shape[0]
    s, k = ext, 1
    while k < w:
        s = s + pltpu.roll(s, k if trailing else n - k, 0)
        k *= 2
    return s


def _inv_count(row0, n, w):
    t = row0 + lax.broadcasted_iota(jnp.int32, (n, 1), 0)
    return 1.0 / jnp.minimum(t + 1, w).astype(F32)


def _shift_down(v, before, k):
    rows = lax.broadcasted_iota(jnp.int32, before.shape, 0)
    r = pltpu.roll(v, k, 0)
    top = jnp.where(rows < k, pltpu.roll(before, k, 0), r[0:HALO])
    return jnp.concatenate([top, r[HALO:]], axis=0)


def _shift_up(v, after, k):
    n = v.shape[0]
    rows = lax.broadcasted_iota(jnp.int32, after.shape, 0)
    r = pltpu.roll(v, n - k, 0)
    bottom = jnp.where(rows >= HALO - k, pltpu.roll(after, HALO - k, 0), r[n - HALO:])
    return jnp.concatenate([r[:n - HALO], bottom], axis=0)


def _adamw(w, g, m, v):
    m = ADAM_B1 * m + (1.0 - ADAM_B1) * g
    v = ADAM_B2 * v + (1.0 - ADAM_B2) * (g * g)
    m_hat = m / (1.0 - ADAM_B1 ** ADAM_STEP)
    v_hat = v / (1.0 - ADAM_B2 ** ADAM_STEP)
    delta = -ADAM_LR * (m_hat / (jnp.sqrt(v_hat) + ADAM_EPS) + ADAM_WD * w)
    return delta, m, v


def _coords():
    return lax.axis_index("x"), lax.axis_index("y"), lax.axis_index("c")


def _peer(k):
    x, y, c = _coords()
    return (x ^ ((k >> 2) & 1), y ^ ((k >> 1) & 1), c ^ (k & 1))


def _my_index():
    x, y, c = _coords()
    return 4 * x + 2 * y + c


def _adaln_modulation(c_ref, w_ref, b_ref, mod_ref, scx_ref, scbuf, stage, recv, send_sems, recv_sems):
    me = _my_index()
    cv = c_ref[...]
    scbuf[0] = cv * _sigmoid(cv)
    first = [
        pltpu.make_async_remote_copy(scbuf.at[0], scbuf.at[k], send_sems.at[0, k], recv_sems.at[0, k],
                                     device_id=_peer(k), device_id_type=MESH)
        for k in range(1, N_DEV)
    ]
    for cp in first:
        cp.start()
    for cp in first:
        cp.wait()
    scx_ref[...] = jnp.zeros(scx_ref.shape, F32)
    for k in range(N_DEV):
        scx_ref[k:k + 1, :] = scbuf[k]
    prod = _mm(_bf(scx_ref[...]), _bf(w_ref[...]))
    for k in range(N_DEV):
        stage[k] = prod[k:k + 1, :] + b_ref[me]
    second = [
        pltpu.make_async_remote_copy(stage.at[k], recv.at[k], send_sems.at[1, k], recv_sems.at[1, k],
                                     device_id=_peer(k), device_id_type=MESH)
        for k in range(1, N_DEV)
    ]
    for cp in second:
        cp.start()
    mod_ref[me] = stage[0]
    for cp in second:
        cp.wait()
    for k in range(1, N_DEV):
        mod_ref[me ^ k] = recv[k]


class _GatherSteps:
    def __init__(self, ins, outs, stages, send_sems, recv_sems, local_sems):
        self.ins, self.outs, self.stages = ins, outs, stages
        self.send_sems, self.recv_sems, self.local_sems = send_sems, recv_sems, local_sems
        x, y, c = _coords()
        self.c = c
        self.me, self.sibling = (x, y, c), (x, y, 1 - c)
        self.chips = [(1 - x, y), (x, 1 - y), (1 - x, 1 - y)]

    def _copy(self, a, k, block, to, from_stage=False):
        dst = self.outs[a].at[4 * block[0] + 2 * block[1] + block[2]]
        return pltpu.make_async_remote_copy(self.stages[a] if from_stage else dst, dst, self.send_sems.at[a, k],
                                            self.recv_sems.at[a, k], device_id=to, device_id_type=MESH)

    def _local(self, a):
        me = self.me
        return pltpu.make_async_copy(self.stages[a], self.outs[a].at[4 * me[0] + 2 * me[1] + me[2]],
                                     self.local_sems.at[a])

    def _first(self, a):
        cps = [self._copy(a, 0, self.me, self.sibling, from_stage=True)]
        return cps + [self._copy(a, 1 + j, self.me, (*chip, self.c), from_stage=True)
                      for j, chip in enumerate(self.chips)]

    def _passed(self, a, j):
        return self._copy(a, 4 + j, (*self.chips[j], self.c), self.sibling)

    def start(self):
        for a in range(len(self.ins)):
            block = self.ins[a][...]
            if block.shape != self.stages[a].shape:
                block = block.T
            self.stages[a][...] = block.astype(self.stages[a].dtype)
            self._local(a).start()
            for cp in self._first(a):
                cp.start()

    def forward(self):
        for a in range(len(self.ins)):
            for j, chip in enumerate(self.chips):
                self._copy(a, 1 + j, (*chip, self.c), self.me).wait_recv()
                self._passed(a, j).start()

    def finish(self):
        for a in range(len(self.ins)):
            self._copy(a, 0, self.sibling, self.me).wait_recv()
            for j, chip in enumerate(self.chips):
                self._copy(a, 4 + j, (*chip, 1 - self.c), self.me).wait_recv()
            for cp in self._first(a) + [self._passed(a, j) for j in range(3)]:
                cp.wait_send()
            self._local(a).wait()


def _gather_scratch(shapes, out_dtypes):
    n = len(shapes)
    return ([pltpu.VMEM(s, dt) for s, dt in zip(shapes, out_dtypes)]
            + [pltpu.SemaphoreType.DMA((n, 7)), pltpu.SemaphoreType.DMA((n, 7)), pltpu.SemaphoreType.DMA((n,))])


def _gather_out_shapes(shapes, out_dtypes):
    return tuple(jax.ShapeDtypeStruct((N_DEV, *s), dt) for s, dt in zip(shapes, out_dtypes))


def _prologue(c_row, w_ada, b_ada3, shards, out_dtypes):
    n = len(shards)

    def body(*refs):
        c_ref, w_ref, b_ref = refs[:3]
        mod_ref, scx_ref = refs[3 + n:5 + n]
        gather = _GatherSteps(refs[3:3 + n], refs[5 + n:5 + 2 * n], refs[5 + 2 * n:5 + 3 * n],
                              *refs[5 + 3 * n:8 + 3 * n])
        gather.start()
        _adaln_modulation(c_ref, w_ref, b_ref, mod_ref, scx_ref, *refs[8 + 3 * n:])
        gather.forward()
        gather.finish()

    outs = pl.pallas_call(
        body,
        name="prologue",
        out_shape=(jax.ShapeDtypeStruct((N_DEV, 1, MOD_COLS), F32), jax.ShapeDtypeStruct((2 * N_DEV, D), F32))
        + _gather_out_shapes([s.shape for s in shards], out_dtypes),
        in_specs=[VMEM_SPEC] * (3 + n),
        out_specs=(VMEM_SPEC, VMEM_SPEC) + (ANY_SPEC,) * n,
        scratch_shapes=_gather_scratch([s.shape for s in shards], out_dtypes) + [
            pltpu.VMEM((N_DEV, 1, D), F32),
            pltpu.VMEM((N_DEV, 1, MOD_COLS), F32),
            pltpu.VMEM((N_DEV, 1, MOD_COLS), F32),
            pltpu.SemaphoreType.DMA((2, N_DEV)),
            pltpu.SemaphoreType.DMA((2, N_DEV)),
        ],
        compiler_params=pltpu.CompilerParams(vmem_limit_bytes=VMEM_LIMIT_V7X),
    )(c_row, w_ada, b_ada3, *shards)
    return outs[0], outs[1], outs[2:]


class _ChipExchangeSteps:
    FLIPS = ((1, 0), (0, 1), (1, 1))

    def __init__(self, srcs, dsts, send_sems, recv_sems):
        self.srcs, self.dsts, self.send_sems, self.recv_sems = srcs, dsts, send_sems, recv_sems

    def _copies(self):
        x, y, c = _coords()
        out = []
        for a in range(len(self.srcs)):
            for j, (fx, fy) in enumerate(self.FLIPS):
                k = 3 * a + j
                out.append(pltpu.make_async_remote_copy(
                    self.srcs[a].at[x ^ fx, y ^ fy], self.dsts[a].at[j], self.send_sems.at[k], self.recv_sems.at[k],
                    device_id=(x ^ fx, y ^ fy, c), device_id_type=MESH))
        return out

    def start(self):
        for cp in self._copies():
            cp.start()

    def finish(self):
        for cp in self._copies():
            cp.wait()


def _mix_fwd(x, mod, g_pre, g_post, w_in_t, sgn, w_sp, b_sp_t, w_pool, p_scale, w_out_b, ts, shards, shard_shapes,
             shard_dtypes):
    t_len = x.shape[0]
    nt, nb = t_len // ts, ts // HEAD
    ns = len(shards)

    def body(*refs):
        (x_ref, mod_ref, g1_ref, g2_ref, win_ref, sgn_ref, ws_ref, bst_ref, wp_ref, ps_ref, wout_ref) = refs[:11]
        x1_ref, proj_ref, mixed_ref = refs[11 + ns:14 + ns]
        pbuf, cat = refs[14 + 2 * ns:16 + 2 * ns]
        gather = _GatherSteps(refs[11:11 + ns], refs[14 + ns:14 + 2 * ns], refs[16 + 2 * ns:16 + 3 * ns],
                              *refs[16 + 3 * ns:])
        i = pl.program_id(0)

        @pl.when(i == 0)
        def _():
            pbuf[0:POOL_HALO, :] = jnp.zeros((POOL_HALO, B_WIDTH), F32)
            gather.start()

        @pl.when(i == (3 * nt) // 4)
        def _():
            gather.forward()

        xv = x_ref[...]
        sh, sc, gm = mod_ref[0:1, :], mod_ref[1:2, :], mod_ref[2:3, :]
        h = (xv * _rstd(xv) * g1_ref[...]) * (1.0 + sc) + sh
        proj_ref[...] = _mm_nt(_bf(h), win_ref[...])
        pbuf[POOL_HALO:POOL_HALO + ts, :] = proj_ref[:, 2 * A_WIDTH:]
        smask = _sgu_mask()
        for hd in range(N_HEAD):
            u = _gelu(proj_ref[:, hd * HEAD:(hd + 1) * HEAD])
            v = _gelu(proj_ref[:, A_WIDTH + hd * HEAD:A_WIDTH + (hd + 1) * HEAD])
            vn = _bf(v * _rstd(v) * sgn_ref[hd:hd + 1, :])
            wm = _bf(jnp.where(smask, ws_ref[hd], 0.0))
            bias = bst_ref[:, hd:hd + 1]
            for b in range(nb):
                rows = slice(b * HEAD, (b + 1) * HEAD)
                z = _mm(wm, vn[rows]) + bias
                cat[rows, hd * HEAD:(hd + 1) * HEAD] = _bf(u[rows] * z)
        for g, w in enumerate(WINDOWS):
            cols = slice(g * HEAD, (g + 1) * HEAD)
            ext = pbuf[:, cols]
            pooled = _window_sum(ext, w, True)[POOL_HALO:] * _inv_count(i * ts, ts, w) - ext[POOL_HALO:]
            cat[:, A_WIDTH + g * HEAD:A_WIDTH + (g + 1) * HEAD] = _bf(_mm(_bf(pooled), _bf(wp_ref[g])) * ps_ref[:, cols])
        pbuf[0:POOL_HALO, :] = pbuf[ts:ts + POOL_HALO, :]
        mixed = _mm(cat[...], wout_ref[...])
        mixed_ref[...] = mixed
        x1_ref[...] = xv + gm * (mixed * _rstd(mixed) * g2_ref[...])

        @pl.when(i == nt - 1)
        def _():
            gather.finish()

    tile = lambda wid: pl.BlockSpec((ts, wid), lambda i: (i, 0))
    outs = pl.pallas_call(
        body,
        name="mix_fwd",
        grid=(nt,),
        out_shape=(jax.ShapeDtypeStruct((t_len, D), F32), jax.ShapeDtypeStruct((t_len, IN_WIDTH), F32),
                   jax.ShapeDtypeStruct((t_len, D), F32)) + _gather_out_shapes(shard_shapes, shard_dtypes),
        in_specs=[tile(D)] + [VMEM_SPEC] * (10 + ns),
        out_specs=(tile(D), tile(IN_WIDTH), tile(D)) + (ANY_SPEC,) * ns,
        scratch_shapes=[pltpu.VMEM((POOL_HALO + ts, B_WIDTH), F32), pltpu.VMEM((ts, D), BF16)]
        + _gather_scratch(shard_shapes, shard_dtypes),
        compiler_params=pltpu.CompilerParams(dimension_semantics=("arbitrary",), vmem_limit_bytes=VMEM_LIMIT_V7X),
    )(x, mod, g_pre, g_post, w_in_t, sgn, w_sp, b_sp_t, w_pool, p_scale, w_out_b, *shards)
    return outs[:3], outs[3:]


def _ffn_fwd(x1, target, mod, g_pre, g_post, w_up_b, conv_w8, conv_b8, w_down_b, ts):
    t_len = x1.shape[0]
    nt = t_len // ts

    def body(x1_ref, tgt_ref, mod_ref, g3_ref, g4_ref, wup_ref, cw_ref, cb_ref, wdown_ref,
             up_ref, f_ref, dx2_ref, loss_ref, ucarry):
        i = pl.program_id(0)

        @pl.when(i == 0)
        def _():
            ucarry[...] = jnp.zeros(ucarry.shape, F32)
            loss_ref[...] = jnp.zeros(loss_ref.shape, F32)

        x1v = x1_ref[...]
        sh, sc, gf = mod_ref[3:4, :], mod_ref[4:5, :], mod_ref[5:6, :]
        h2 = _bf((x1v * _rstd(x1v) * g3_ref[...]) * (1.0 + sc) + sh)
        half = N_DEV // 2

        def up_pair(j):
            return [_mm(h2, wup_ref[jj]) for jj in (j, j + half)]

        f = jnp.zeros((ts, D), F32)
        ups = up_pair(0)
        for j in range(half):
            nxt = up_pair(j + 1) if j + 1 < half else None
            ys = []
            for up, jj in zip(ups, (j, j + half)):
                up_ref[jj] = up
                before = ucarry[jj]
                ucarry[jj] = up[ts - HALO:, :]
                cw = cw_ref[jj]
                ys.append(cb_ref[jj:jj + 1, :] + _shift_down(up, before, 2) * cw[0:1, :]
                          + _shift_down(up, before, 1) * cw[1:2, :] + up * cw[2:3, :])
            gate, val = ys
            act = gate * _sigmoid(gate) * val
            f = f + _mm(_bf(act), wdown_ref[j * FF_CHUNK:(j + 1) * FF_CHUNK, :])
            ups = nxt
        f_ref[...] = f
        x2 = x1v + gf * (f * _rstd(f) * g4_ref[...])
        err = x2 - tgt_ref[...]
        loss_ref[...] += _sum0(err * err)
        dx2_ref[...] = err * (1.0 / D)

    tile = pl.BlockSpec((ts, D), lambda i: (i, 0))
    return pl.pallas_call(
        body,
        name="ffn_fwd",
        grid=(nt,),
        out_shape=(jax.ShapeDtypeStruct((N_DEV, t_len, FF_CHUNK), F32), jax.ShapeDtypeStruct((t_len, D), F32),
                   jax.ShapeDtypeStruct((t_len, D), F32), jax.ShapeDtypeStruct((1, D), F32)),
        in_specs=[tile, tile] + [VMEM_SPEC] * 7,
        out_specs=(pl.BlockSpec((N_DEV, ts, FF_CHUNK), lambda i: (0, i, 0)), tile, tile,
                   pl.BlockSpec((1, D), lambda i: (0, 0))),
        scratch_shapes=[pltpu.VMEM((N_DEV, HALO, FF_CHUNK), F32)],
        compiler_params=pltpu.CompilerParams(dimension_semantics=("arbitrary",), vmem_limit_bytes=VMEM_LIMIT_V7X),
    )(x1, target, mod, g_pre, g_post, w_up_b, conv_w8, conv_b8, w_down_b)


def _ffn_bwd(dx2, f, x1, up, mod, g_pre, g_post, w_up_b, conv_w8, conv_b8, w_down_b, ts):
    t_len = x1.shape[0]
    nt = t_len // ts
    half = N_DEV // 2

    def body(dx2_ref, f_ref, x1_ref, up_ref, halo_ref, mod_ref, g3_ref, g4_ref, wup_ref, cw_ref, cb_ref, wdown_ref,
             dx1_ref, dup_ref, act_ref, df_ref, h2_ref, dmod_ref, dg3_ref, dg4_ref, dcb_ref, dcw_ref,
             dycarry, dh2acc):
        i = pl.program_id(0)
        r = nt - 1 - i

        @pl.when(i == 0)
        def _():
            for ref in (dmod_ref, dg3_ref, dg4_ref, dcb_ref, dcw_ref, dycarry):
                ref[...] = jnp.zeros(ref.shape, F32)

        dx2v, fv, x1v = dx2_ref[...], f_ref[...], x1_ref[...]
        sh, sc, gf = mod_ref[3:4, :], mod_ref[4:5, :], mod_ref[5:6, :]
        g3, g4 = g3_ref[...], g4_ref[...]
        rstd4 = _rstd(fv)
        fh = fv * rstd4
        dmod_ref[2:3, :] += _sum0(dx2v * (fh * g4))
        dr = dx2v * gf
        dg4_ref[...] += _sum0(dr * fh)
        dfh = dr * g4
        dfb = _bf(rstd4 * (dfh - fh * _rowmean(dfh * fh)))
        df_ref[...] = dfb
        rstd3 = _rstd(x1v)
        xh = x1v * rstd3
        n3 = xh * g3
        h2_ref[...] = _bf(n3 * (1.0 + sc) + sh)
        dh2acc[...] = jnp.zeros((ts, D), F32)
        keep = jnp.where(r > 0, 1.0, 0.0).astype(F32)

        def dact_of(j):
            return _mm_nt(dfb, wdown_ref[j * FF_CHUNK:(j + 1) * FF_CHUNK, :])

        dact_next = dact_of(0)
        for j in range(half):
            dact = dact_next
            if j + 1 < half:
                dact_next = dact_of(j + 1)
            ys = []
            for jj in (j, j + half):
                before = halo_ref[jj] * keep
                upc = up_ref[jj]
                cw = cw_ref[jj]
                ys.append(cb_ref[jj:jj + 1, :] + _shift_down(upc, before, 2) * cw[0:1, :]
                          + _shift_down(upc, before, 1) * cw[1:2, :] + upc * cw[2:3, :])
            gate, val = ys
            sg = _sigmoid(gate)
            silu = gate * sg
            act_ref[j] = _bf(silu * val)
            dys = (dact * val * (sg + silu * (1.0 - sg)), dact * silu)
            for q, jj in enumerate((j, j + half)):
                dy = dys[q]
                cw = cw_ref[jj]
                dcb_ref[jj:jj + 1, :] += _sum0(dy)
                after = dycarry[jj]
                dycarry[jj] = dy[0:HALO, :]
                dy1, dy2 = _shift_up(dy, after, 1), _shift_up(dy, after, 2)
                upc = up_ref[jj]
                dcw_ref[jj, 0:1, :] += _sum0(dy2 * upc)
                dcw_ref[jj, 1:2, :] += _sum0(dy1 * upc)
                dcw_ref[jj, 2:3, :] += _sum0(dy * upc)
                dup = _bf(dy * cw[2:3, :] + dy1 * cw[1:2, :] + dy2 * cw[0:1, :])
                dup_ref[jj] = dup
                dh2acc[...] += _mm_nt(dup, wup_ref[jj])
        dh2 = dh2acc[...]
        dmod_ref[0:1, :] += _sum0(dh2)
        dmod_ref[1:2, :] += _sum0(dh2 * n3)
        dn3 = dh2 * (1.0 + sc)
        dg3_ref[...] += _sum0(dn3 * xh)
        dxh = dn3 * g3
        dx1_ref[...] = dx2v + rstd3 * (dxh - xh * _rowmean(dxh * xh))

    tile = pl.BlockSpec((ts, D), lambda i: (nt - 1 - i, 0))
    chunked = lambda n: pl.BlockSpec((n, ts, FF_CHUNK), lambda i: (0, nt - 1 - i, 0))
    halo = pl.BlockSpec((N_DEV, HALO, FF_CHUNK), lambda i: (0, jnp.maximum((nt - 1 - i) * (ts // HALO) - 1, 0), 0))
    const = lambda *shape: pl.BlockSpec(shape, lambda i: (0,) * len(shape))
    return pl.pallas_call(
        body,
        name="ffn_bwd",
        grid=(nt,),
        out_shape=(jax.ShapeDtypeStruct((t_len, D), F32), jax.ShapeDtypeStruct((N_DEV, t_len, FF_CHUNK), BF16),
                   jax.ShapeDtypeStruct((half, t_len, FF_CHUNK), BF16), jax.ShapeDtypeStruct((t_len, D), BF16),
                   jax.ShapeDtypeStruct((t_len, D), BF16), jax.ShapeDtypeStruct((3, D), F32),
                   jax.ShapeDtypeStruct((1, D), F32), jax.ShapeDtypeStruct((1, D), F32),
                   jax.ShapeDtypeStruct((N_DEV, FF_CHUNK), F32), jax.ShapeDtypeStruct((N_DEV, 3, FF_CHUNK), F32)),
        in_specs=[tile, tile, tile, chunked(N_DEV), halo] + [VMEM_SPEC] * 7,
        out_specs=(tile, chunked(N_DEV), chunked(half), tile, tile, const(3, D), const(1, D), const(1, D),
                   const(N_DEV, FF_CHUNK), const(N_DEV, 3, FF_CHUNK)),
        scratch_shapes=[pltpu.VMEM((N_DEV, HALO, FF_CHUNK), F32), pltpu.VMEM((ts, D), F32)],
        compiler_params=pltpu.CompilerParams(dimension_semantics=("arbitrary",), vmem_limit_bytes=VMEM_LIMIT_V7X),
    )(dx2, f, x1, up, up, mod, g_pre, g_post, w_up_b, conv_w8, conv_b8, w_down_b)


def _wgrad_up(h2, dup, ts):
    t_len = h2.shape[0]
    nt, half = t_len // ts, N_DEV // 2

    def body(h2_ref, dup_ref, out_ref):
        @pl.when(pl.program_id(1) == 0)
        def _():
            out_ref[...] = jnp.zeros(out_ref.shape, F32)

        for q in range(half):
            out_ref[q] += _mm_tn(dup_ref[q], h2_ref[...])

    return pl.pallas_call(
        body,
        name="wgrad_up",
        grid=(2, nt),
        out_shape=jax.ShapeDtypeStruct((N_DEV, FF_CHUNK, D), F32),
        in_specs=[pl.BlockSpec((ts, D), lambda g, t: (t, 0)), pl.BlockSpec((half, ts, FF_CHUNK), lambda g, t: (g, t, 0))],
        out_specs=pl.BlockSpec((half, FF_CHUNK, D), lambda g, t: (g, 0, 0), pipeline_mode=pl.Buffered(1)),
        compiler_params=pltpu.CompilerParams(dimension_semantics=("arbitrary", "arbitrary"),
                                             vmem_limit_bytes=VMEM_LIMIT_V7X),
    )(h2, dup)


def _sibling_swap_copies(srcs, dsts, send_sems, recv_sems):
    x, y, c = _coords()
    return [
        pltpu.make_async_remote_copy(srcs[a].at[xs, ys, 1 - c], dsts[a].at[xs, ys], send_sems.at[a, 2 * xs + ys],
                                     recv_sems.at[a, 2 * xs + ys], device_id=(x, y, 1 - c), device_id_type=MESH)
        for a in range(len(srcs)) for xs in range(2) for ys in range(2)
    ]


def _wgrad_down(act, df, ts, swap_src):
    t_len = df.shape[0]
    nt, half = t_len // ts, N_DEV // 2

    def body(act_ref, df_ref, src_ref, out_ref, dst_ref, send_sems, recv_sems):
        t = pl.program_id(0)

        @pl.when(t == 0)
        def _():
            for cp in _sibling_swap_copies([src_ref], [dst_ref], send_sems, recv_sems):
                cp.start()
            out_ref[...] = jnp.zeros(out_ref.shape, F32)

        for q in range(half):
            out_ref[q] += _mm_tn(act_ref[q], df_ref[...])

        @pl.when(t == nt - 1)
        def _():
            for cp in _sibling_swap_copies([src_ref], [dst_ref], send_sems, recv_sems):
                cp.wait()

    return pl.pallas_call(
        body,
        name="wgrad_down",
        grid=(nt,),
        out_shape=(jax.ShapeDtypeStruct((half, FF_CHUNK, D), F32), jax.ShapeDtypeStruct(swap_src.shape[1:], F32)),
        in_specs=[pl.BlockSpec((half, ts, FF_CHUNK), lambda t: (0, t, 0)), pl.BlockSpec((ts, D), lambda t: (t, 0)),
                  ANY_SPEC],
        out_specs=(pl.BlockSpec((half, FF_CHUNK, D), lambda t: (0, 0, 0), pipeline_mode=pl.Buffered(1)), ANY_SPEC),
        scratch_shapes=[pltpu.SemaphoreType.DMA((1, 4)), pltpu.SemaphoreType.DMA((1, 4))],
        compiler_params=pltpu.CompilerParams(dimension_semantics=("arbitrary",), vmem_limit_bytes=VMEM_LIMIT_V7X),
    )(act, df, swap_src)


def _mix_bwd(dx1, x, proj, mixed, mod, g_pre, g_post, w_in_t, sgn, w_sp, b_sp_t, w_pool, p_scale, w_out_b, ts, rs_srcs):
    t_len = x.shape[0]
    nt, nb = t_len // ts, ts // HEAD
    nr = len(rs_srcs)

    def body(*refs):
        (dx1_ref, x_ref, proj_ref, halo_ref, mixed_ref, mod_ref, g1_ref, g2_ref, win_ref, sgn_ref, ws_ref,
         bst_ref, wp_ref, ps_ref, wout_ref) = refs[:15]
        (gx_ref, dwin_ref, dwout_ref, dmod_ref, dg1_ref, dg2_ref, dsgn_ref, dws_ref, dbst_ref, dwp_ref,
         dps_ref) = refs[15 + nr:26 + nr]
        pbuf, dwsbuf, cat, dproj, dcat = refs[26 + 2 * nr:31 + 2 * nr]
        exchange = _ChipExchangeSteps(refs[15:15 + nr], refs[26 + nr:26 + 2 * nr], *refs[31 + 2 * nr:])
        i = pl.program_id(0)
        r = nt - 1 - i

        @pl.when(i == 0)
        def _():
            exchange.start()
            for ref in (dwin_ref, dwout_ref, dmod_ref, dg1_ref, dg2_ref, dsgn_ref, dws_ref, dbst_ref, dwp_ref, dps_ref):
                ref[...] = jnp.zeros(ref.shape, F32)
            dwsbuf[ts:ts + POOL_HALO, :] = jnp.zeros((POOL_HALO, B_WIDTH), F32)

        xv, dx1v, mixed = x_ref[...], dx1_ref[...], mixed_ref[...]
        sh, sc, gm = mod_ref[0:1, :], mod_ref[1:2, :], mod_ref[2:3, :]
        g1, g2 = g1_ref[...], g2_ref[...]
        rstd2 = _rstd(mixed)
        mh = mixed * rstd2
        dmod_ref[2:3, :] += _sum0(dx1v * (mh * g2))
        dr = dx1v * gm
        dg2_ref[...] += _sum0(dr * mh)
        dmh = dr * g2
        dmb = _bf(rstd2 * (dmh - mh * _rowmean(dmh * mh)))
        dcat[...] = _mm_nt(dmb, wout_ref[...])
        smask = _sgu_mask()
        for hd in range(N_HEAD):
            ucols = slice(hd * HEAD, (hd + 1) * HEAD)
            vcols = slice(A_WIDTH + hd * HEAD, A_WIDTH + (hd + 1) * HEAD)
            u, du_dp = _gelu_and_grad(proj_ref[:, ucols])
            v, dv_dp = _gelu_and_grad(proj_ref[:, vcols])
            rs = _rstd(v)
            vhat = v * rs
            gn = sgn_ref[hd:hd + 1, :]
            vn = _bf(vhat * gn)
            wm = _bf(jnp.where(smask, ws_ref[hd], 0.0))
            bias = bst_ref[:, hd:hd + 1]
            dzsum = jnp.zeros((HEAD, HEAD), F32)
            dwm = jnp.zeros((HEAD, HEAD), F32)
            dvn_parts = []
            for b in range(nb):
                rows = slice(b * HEAD, (b + 1) * HEAD)
                z = _mm(wm, vn[rows]) + bias
                da = dcat[rows, ucols]
                cat[rows, ucols] = _bf(u[rows] * z)
                dz = da * u[rows]
                dzsum = dzsum + dz
                dzb = _bf(dz)
                dwm = dwm + _mm_nt(dzb, vn[rows])
                dvn_parts.append(_mm_tn(wm, dzb))
                dproj[rows, ucols] = _bf((da * z) * du_dp[rows])
            dvn = jnp.concatenate(dvn_parts, axis=0)
            dsgn_ref[hd:hd + 1, :] += _sum0(dvn * vhat)
            dvh = dvn * gn
            dproj[:, vcols] = _bf((rs * (dvh - vhat * _rowmean(dvh * vhat))) * dv_dp)
            dws_ref[hd] += jnp.where(smask, dwm, 0.0)
            dbst_ref[:, hd:hd + 1] += jnp.sum(dzsum, axis=1, keepdims=True)
        keep = jnp.where(r > 0, 1.0, 0.0).astype(F32)
        pbuf[0:POOL_HALO, :] = halo_ref[...] * keep
        pbuf[POOL_HALO:POOL_HALO + ts, :] = proj_ref[:, 2 * A_WIDTH:]
        for g, w in enumerate(WINDOWS):
            cols = slice(g * HEAD, (g + 1) * HEAD)
            ccols = slice(A_WIDTH + g * HEAD, A_WIDTH + (g + 1) * HEAD)
            pcols = slice(2 * A_WIDTH + g * HEAD, 2 * A_WIDTH + (g + 1) * HEAD)
            wpg = _bf(wp_ref[g])
            psg = ps_ref[:, cols]
            ext = pbuf[:, cols]
            inv = _inv_count(r * ts, ts, w)
            pb = _bf(_window_sum(ext, w, True)[POOL_HALO:] * inv - ext[POOL_HALO:])
            yb = _mm(pb, wpg)
            dob = dcat[:, ccols]
            cat[:, ccols] = _bf(yb * psg)
            dps_ref[:, cols] += _sum0(dob * yb)
            dyb = _bf(dob * psg)
            dwp_ref[g] += _mm_tn(pb, dyb)
            dpooled = _mm_nt(dyb, wpg)
            dwsbuf[0:ts, cols] = dpooled * inv
            dproj[:, pcols] = _bf(_window_sum(dwsbuf[:, cols], w, False)[0:ts] - dpooled)
        dwsbuf[ts:ts + POOL_HALO, :] = dwsbuf[0:POOL_HALO, :]
        dpb = dproj[...]
        rstd1 = _rstd(xv)
        xh = xv * rstd1
        n1 = xh * g1
        dwin_ref[...] += _mm_tn(dpb, _bf(n1 * (1.0 + sc) + sh))
        dwout_ref[...] += _mm_tn(cat[...], dmb)
        dh = _mm(dpb, win_ref[...])
        dmod_ref[0:1, :] += _sum0(dh)
        dmod_ref[1:2, :] += _sum0(dh * n1)
        dn1 = dh * (1.0 + sc)
        dg1_ref[...] += _sum0(dn1 * xh)
        dxh = dn1 * g1
        gx_ref[...] = dx1v + rstd1 * (dxh - xh * _rowmean(dxh * xh))

        @pl.when(i == nt - 1)
        def _():
            exchange.finish()

    tile = lambda wid: pl.BlockSpec((ts, wid), lambda i: (nt - 1 - i, 0))
    halo = pl.BlockSpec((POOL_HALO, B_WIDTH),
                        lambda i: (jnp.maximum((nt - 1 - i) * (ts // POOL_HALO) - 1, 0), 2 * A_WIDTH // B_WIDTH))
    const = lambda *shape: pl.BlockSpec(shape, lambda i: (0,) * len(shape))
    resident = lambda *shape: pl.BlockSpec(shape, lambda i: (0,) * len(shape), pipeline_mode=pl.Buffered(1))
    outs = pl.pallas_call(
        body,
        name="mix_bwd",
        grid=(nt,),
        out_shape=(jax.ShapeDtypeStruct((t_len, D), F32), jax.ShapeDtypeStruct((IN_WIDTH, D), F32),
                   jax.ShapeDtypeStruct((D, D), F32), jax.ShapeDtypeStruct((3, D), F32),
                   jax.ShapeDtypeStruct((1, D), F32), jax.ShapeDtypeStruct((1, D), F32),
                   jax.ShapeDtypeStruct((N_HEAD, HEAD), F32), jax.ShapeDtypeStruct((N_HEAD, HEAD, HEAD), F32),
                   jax.ShapeDtypeStruct((HEAD, N_HEAD), F32), jax.ShapeDtypeStruct((N_HEAD, HEAD, HEAD), F32),
                   jax.ShapeDtypeStruct((1, B_WIDTH), F32))
        + tuple(jax.ShapeDtypeStruct((3, *s.shape[2:]), s.dtype) for s in rs_srcs),
        in_specs=[tile(D), tile(D), tile(IN_WIDTH), halo, tile(D)] + [VMEM_SPEC] * 10 + [ANY_SPEC] * nr,
        out_specs=(tile(D), resident(IN_WIDTH, D), resident(D, D), const(3, D), const(1, D), const(1, D),
                   const(N_HEAD, HEAD), const(N_HEAD, HEAD, HEAD), const(HEAD, N_HEAD), const(N_HEAD, HEAD, HEAD),
                   const(1, B_WIDTH)) + (ANY_SPEC,) * nr,
        scratch_shapes=[pltpu.VMEM((POOL_HALO + ts, B_WIDTH), F32), pltpu.VMEM((ts + POOL_HALO, B_WIDTH), F32),
                        pltpu.VMEM((ts, D), BF16), pltpu.VMEM((ts, IN_WIDTH), BF16), pltpu.VMEM((ts, D), F32),
                        pltpu.SemaphoreType.DMA((3 * nr,)), pltpu.SemaphoreType.DMA((3 * nr,))],
        compiler_params=pltpu.CompilerParams(dimension_semantics=("arbitrary",), vmem_limit_bytes=VMEM_LIMIT_V7X),
    )(dx1, x, proj, proj, mixed, mod, g_pre, g_post, w_in_t, sgn, w_sp, b_sp_t, w_pool, p_scale, w_out_b, *rs_srcs)
    return outs[:11], outs[11:]


def _pair_add(name, coords, grid, specs_a, specs_b, out_specs, out_shapes, a_arrays, b_arrays, swap_srcs):
    n, ns = len(a_arrays), len(swap_srcs)
    last = tuple(g - 1 for g in grid)

    def body(co_ref, *refs):
        ids = [pl.program_id(d) for d in range(len(grid))]
        swap = refs[2 * n:2 * n + ns], refs[4 * n + ns:4 * n + 2 * ns], *refs[4 * n + 2 * ns:]
        if ns:
            @pl.when(functools.reduce(jnp.logical_and, [i == 0 for i in ids]))
            def _():
                for cp in _sibling_swap_copies(*swap):
                    cp.start()

        for k in range(n):
            total = refs[k][...] + refs[n + k][...]
            refs[2 * n + ns + k][...] = total
            refs[3 * n + ns + k][...] = _bf(total)

        if ns:
            @pl.when(functools.reduce(jnp.logical_and, [i == e for i, e in zip(ids, last)]))
            def _():
                for cp in _sibling_swap_copies(*swap):
                    cp.wait()

    outs = pl.pallas_call(
        body,
        name=name,
        grid_spec=pltpu.PrefetchScalarGridSpec(
            num_scalar_prefetch=1, grid=grid, in_specs=specs_a + specs_b + [ANY_SPEC] * ns,
            out_specs=out_specs * 2 + [ANY_SPEC] * ns,
            scratch_shapes=[pltpu.SemaphoreType.DMA((ns, 4)), pltpu.SemaphoreType.DMA((ns, 4))] if ns else []),
        out_shape=tuple(jax.ShapeDtypeStruct(s, dt) for dt in (F32, BF16) for s in out_shapes)
        + tuple(jax.ShapeDtypeStruct(g.shape[1:], F32) for g in swap_srcs),
        compiler_params=pltpu.CompilerParams(dimension_semantics=("arbitrary",) * len(grid),
                                             vmem_limit_bytes=VMEM_LIMIT_V7X),
    )(coords, *a_arrays, *b_arrays, *swap_srcs)
    return list(outs[:n]), list(outs[n:2 * n]), list(outs[2 * n:])


def _final_add_adamw(coords, s1, r, ws, ms, vs, n_split=4):
    n = len(s1)

    def body(co_ref, *refs):
        for k in range(n):
            s_ref, r_ref, w_ref, m_ref, v_ref = (refs[q * n + k] for q in range(5))
            g_ref, d_ref, nm_ref, nv_ref = (refs[(5 + q) * n + k] for q in range(4))
            g = ((s_ref[...] + r_ref[0].astype(F32)) + r_ref[1].astype(F32)) + r_ref[2].astype(F32)
            g_ref[...] = g
            delta, m, v = _adamw(w_ref[...], g, m_ref[...], v_ref[...])
            d_ref[...] = delta
            nm_ref[...] = m
            nv_ref[...] = v

    def shard_spec(a):
        rows, cols = a.shape
        return pl.BlockSpec((rows // n_split, cols), lambda i, co: (i, 0))

    def mine_spec(a):
        rows, cols = a.shape[2:]
        return pl.BlockSpec((None, None, rows // n_split, cols), lambda i, co: (co[0], co[1], i, 0))

    def recv_spec(a):
        rows, cols = a.shape[1:]
        return pl.BlockSpec((3, rows // n_split, cols), lambda i, co: (0, i, 0))

    in_specs = ([mine_spec(a) for a in s1] + [recv_spec(a) for a in r] + [shard_spec(a) for a in ws] * 3)
    out_specs = [shard_spec(a) for a in ws] * 4
    outs = pl.pallas_call(
        body,
        name="grad_final_adamw",
        grid_spec=pltpu.PrefetchScalarGridSpec(num_scalar_prefetch=1, grid=(n_split,), in_specs=in_specs,
                                               out_specs=out_specs),
        out_shape=tuple(jax.ShapeDtypeStruct(a.shape, F32) for a in ws) * 4,
        compiler_params=pltpu.CompilerParams(dimension_semantics=("arbitrary",), vmem_limit_bytes=VMEM_LIMIT_V7X),
    )(coords, *s1, *r, *ws, *ms, *vs)
    return [tuple(outs[q * n + k] for q in range(4)) for k in range(n)]


def _sibling_add(tag, g5, r1, coords, swap_srcs=(), n_split=4):
    shapes = [g.shape[3:] for g in g5]
    spec_g = [pl.BlockSpec((None, None, None, s[0] // n_split, s[1]), lambda i, j, k, co: (i, j, co[2], k, 0))
              for s in shapes]
    spec_r = [pl.BlockSpec((None, None, s[0] // n_split, s[1]), lambda i, j, k, co: (i, j, k, 0)) for s in shapes]
    return _pair_add("grad_add_core_" + tag, coords, (2, 2, n_split), spec_g, spec_r, spec_r,
                     [(2, 2, *s) for s in shapes], g5, r1, list(swap_srcs))


def _tail_exchange(big, partials, pick_mine, y_first, dmod3):
    n, nb = len(partials), len(big)
    big_shapes = [g.shape[1:] for g in big]
    big5 = [g.reshape(2, 2, 2, *s) for g, s in zip(big, big_shapes)]
    flips = _ChipExchangeSteps.FLIPS

    def body(*refs):
        g5, p_in, dm_ref = refs[:nb], refs[nb:nb + n], refs[nb + n]
        outs = refs[nb + n + 1:2 * nb + 2 * n + 2]
        g_out, sums, dm2d = outs[:nb], outs[nb:nb + n], outs[nb + n]
        scratch = refs[2 * nb + 2 * n + 2:]
        s1, stage, chip_recv = scratch[:nb], scratch[nb:2 * nb], scratch[2 * nb:3 * nb]
        acc, rbuf = scratch[3 * nb:3 * nb + n], scratch[3 * nb + n:3 * nb + 2 * n]
        (dm_recv, send_sems, recv_sems, dm_send_sems, dm_recv_sems, sib_send, sib_recv, chip_send,
         chip_recv_sems) = scratch[3 * nb + 2 * n:]
        x, y, c = _coords()
        me = 4 * x + 2 * y + c
        sibling = (x, y, 1 - c)
        dm_copies = [
            pltpu.make_async_remote_copy(dm_ref.at[me ^ k], dm_recv.at[k], dm_send_sems.at[k], dm_recv_sems.at[k],
                                         device_id=_peer(k), device_id_type=MESH)
            for k in range(1, N_DEV)
        ]
        for cp in dm_copies:
            cp.start()
        sib_copies = _sibling_swap_copies(g5, s1, sib_send, sib_recv)
        for cp in sib_copies:
            cp.start()
        for a in range(n):
            acc[a][...] = p_in[a][...]

        def small_phase(ph, peers):
            copies = [
                pltpu.make_async_remote_copy(acc[a], rbuf[a].at[ph], send_sems.at[ph, a], recv_sems.at[ph, a],
                                             device_id=peers[y_first[a]], device_id_type=MESH)
                for a in range(n)
            ]
            for cp in copies:
                cp.start()
            for cp in copies:
                cp.wait()
            for a in range(n):
                acc[a][...] = acc[a][...] + rbuf[a][ph]

        small_phase(0, (sibling, sibling))
        for cp in sib_copies:
            cp.wait()
        for a in range(nb):
            for xs in range(2):
                for ys in range(2):
                    total = g5[a][xs, ys, c] + s1[a][xs, ys]
                    s1[a][xs, ys] = total
                    stage[a][xs, ys] = _bf(total)
        chip_copies = [
            pltpu.make_async_remote_copy(stage[a].at[x ^ fx, y ^ fy], chip_recv[a].at[j], chip_send.at[a, j],
                                         chip_recv_sems.at[a, j], device_id=(x ^ fx, y ^ fy, c), device_id_type=MESH)
            for a in range(nb) for j, (fx, fy) in enumerate(flips)
        ]
        for cp in chip_copies:
            cp.start()
        x_peer, y_peer = (1 - x, y, c), (x, 1 - y, c)
        small_phase(1, (x_peer, y_peer))
        small_phase(2, (y_peer, x_peer))
        for a in range(n):
            sums[a][...] = acc[a][me] if pick_mine[a] else acc[a][...]
        dm2d[...] = jnp.zeros(dm2d.shape, F32)
        dm2d[0:1, :] = dm_ref[me]
        for cp in dm_copies:
            cp.wait()
        for k in range(1, N_DEV):
            dm2d[k:k + 1, :] = dm_recv[k]
        for cp in chip_copies:
            cp.wait()
        for a in range(nb):
            g_out[a][...] = ((s1[a][x, y] + chip_recv[a][0].astype(F32)) + chip_recv[a][1].astype(F32)) \
                + chip_recv[a][2].astype(F32)

    out_shapes = tuple(jax.ShapeDtypeStruct(s, F32) for s in big_shapes) + tuple(
        jax.ShapeDtypeStruct(p.shape[1:] if pk else p.shape, F32) for p, pk in zip(partials, pick_mine))
    outs = pl.pallas_call(
        body,
        name="tail_exchange",
        out_shape=out_shapes + (jax.ShapeDtypeStruct((2 * N_DEV, MOD_COLS), F32),),
        in_specs=[VMEM_SPEC] * (nb + n + 1),
        out_specs=(VMEM_SPEC,) * (nb + n + 1),
        scratch_shapes=[pltpu.VMEM((2, 2, *s), F32) for s in big_shapes]
        + [pltpu.VMEM((2, 2, *s), BF16) for s in big_shapes]
        + [pltpu.VMEM((3, *s), BF16) for s in big_shapes]
        + [pltpu.VMEM(p.shape, F32) for p in partials]
        + [pltpu.VMEM((3, *p.shape), F32) for p in partials]
        + [pltpu.VMEM((N_DEV, 1, MOD_COLS), F32), pltpu.SemaphoreType.DMA((3, n)), pltpu.SemaphoreType.DMA((3, n)),
           pltpu.SemaphoreType.DMA((N_DEV,)), pltpu.SemaphoreType.DMA((N_DEV,)),
           pltpu.SemaphoreType.DMA((nb, 4)), pltpu.SemaphoreType.DMA((nb, 4)),
           pltpu.SemaphoreType.DMA((nb, 3)), pltpu.SemaphoreType.DMA((nb, 3))],
        compiler_params=pltpu.CompilerParams(vmem_limit_bytes=VMEM_LIMIT_V7X),
    )(*big5, *partials, dmod3)
    return list(outs[:nb]), list(outs[nb:nb + n]), outs[nb + n]


def _small_update(grads, ws, ms, vs, scx, dm2d, w_ada, m_ada, v_ada, loss_lanes):
    n = len(grads)

    def body(*refs):
        g_in, w_in, m_in, v_in = (refs[q * n:(q + 1) * n] for q in range(4))
        scx_ref, dm_ref, wa_ref, ma_ref, va_ref, ll_ref = refs[4 * n:4 * n + 6]
        outs = refs[4 * n + 6:]
        g_out, d_out, nm_out, nv_out = (outs[q * (n + 1):(q + 1) * (n + 1)] for q in range(4))
        loss_ref = outs[4 * (n + 1)]
        for a in range(n + 1):
            if a < n:
                g, w, m, v = g_in[a][...], w_in[a][...], m_in[a][...], v_in[a][...]
            else:
                g = _mm_tn(_bf(scx_ref[...]), _bf(dm_ref[...]))
                w, m, v = wa_ref[...], ma_ref[...], va_ref[...]
            g_out[a][...] = g
            delta, m, v = _adamw(w, g, m, v)
            d_out[a][...] = delta
            nm_out[a][...] = m
            nv_out[a][...] = v
        loss_ref[...] = jnp.sum(ll_ref[...], axis=1, keepdims=True) * (0.5 / D)

    w_shapes = tuple(jax.ShapeDtypeStruct(w.shape, F32) for w in list(ws) + [w_ada])
    outs = pl.pallas_call(
        body,
        name="small_update",
        out_shape=w_shapes * 4 + (jax.ShapeDtypeStruct((1, 1), F32),),
        in_specs=[VMEM_SPEC] * (4 * n + 6),
        out_specs=(VMEM_SPEC,) * (4 * (n + 1) + 1),
        compiler_params=pltpu.CompilerParams(vmem_limit_bytes=VMEM_LIMIT_V7X),
    )(*grads, *ws, *ms, *vs, scx, dm2d, w_ada, m_ada, v_ada, loss_lanes)
    return [tuple(outs[q * (n + 1) + k] for q in range(4)) for k in range(n + 1)], outs[4 * (n + 1)]


def kernel(x, c, w_ada, b_ada, pre_mix_g, post_mix_g, w_in, sgu_norm_g, w_spatial, b_spatial, w_pool, pool_scale, w_out, pre_ffn_g, post_ffn_g, w_up, conv_w, conv_b, w_down, loss_target, m_w_ada, m_b_ada, m_pre_mix_g, m_post_mix_g, m_w_in, m_sgu_norm_g, m_w_spatial, m_b_spatial, m_w_pool, m_pool_scale, m_w_out, m_pre_ffn_g, m_post_ffn_g, m_w_up, m_conv_w, m_conv_b, m_w_down, v_w_ada, v_b_ada, v_pre_mix_g, v_post_mix_g, v_w_in, v_sgu_norm_g, v_w_spatial, v_b_spatial, v_w_pool, v_pool_scale, v_w_out, v_pre_ffn_g, v_post_ffn_g, v_w_up, v_conv_w, v_conv_b, v_w_down):
    t_len = x.shape[1]
    ts = min(256, t_len)
    ts_mix = min(512, t_len)
    ts_w = min(2048, t_len)
    coords = jnp.stack([lax.axis_index("x"), lax.axis_index("y"), lax.axis_index("c")]).astype(jnp.int32)

    w_in_t, w_up_t = w_in[0].T, w_up[0].T
    mod3, scx, (g_in, g_out, g_down) = _prologue(c, w_ada[0], b_ada.reshape(N_DEV, 1, MOD_COLS),
                                                 [w_in_t, w_out[0], w_down[0]], [BF16, BF16, BF16])
    mod = mod3.reshape(N_MOD, D)
    w_in_tb = g_in.reshape(IN_WIDTH, D)
    w_out_b = g_out.reshape(D, D)
    conv_b8 = conv_b.reshape(N_DEV, FF_CHUNK)
    b_sp_t = b_spatial[0].T

    x2d, tgt = x[0], loss_target[0]
    (x1, proj, mixed), (g_up, g_cw) = _mix_fwd(
        x2d, mod, pre_mix_g, post_mix_g, w_in_tb, sgu_norm_g[0], w_spatial[0], b_sp_t, w_pool[0], pool_scale, w_out_b,
        ts_mix, [w_up_t, conv_w[0]], [w_up.shape[1:], conv_w.shape[1:]], [BF16, F32])
    w_down_b = g_down.reshape(FF, D)
    up, f, dx2, loss_lanes = _ffn_fwd(x1, tgt, mod, pre_ffn_g, post_ffn_g, g_up, g_cw, conv_b8, w_down_b, ts)

    (dx1, dup, act, df, h2, dmod_f, d_pre_ffn, d_post_ffn, d_cb8, d_cw8) = _ffn_bwd(
        dx2, f, x1, up, mod, pre_ffn_g, post_ffn_g, g_up, g_cw, conv_b8, w_down_b, ts)
    gw_up = _wgrad_up(h2, dup, ts_w).reshape(2, 2, 2, FF_CHUNK, D)
    gw_down, r1_up = _wgrad_down(act, df, ts_w, gw_up)
    gw_down = gw_down.reshape(2, 2, 2, FF // N_DEV, D)
    s1_up, s1_up_b, r1_down = _sibling_add("up", [gw_up], [r1_up], coords, swap_srcs=[gw_down])
    s1_down, s1_down_b, _ = _sibling_add("down", [gw_down], r1_down, coords)
    s1_ffn, s1_ffn_b = s1_up + s1_down, s1_up_b + s1_down_b
    ((grad_x, gw_in, gw_out, dmod_m, d_pre_mix, d_post_mix, d_sgn, d_wsp, d_bsp_t, d_wpool, d_ps), r_ffn) = _mix_bwd(
        dx1, x2d, proj, mixed, mod, pre_mix_g, post_mix_g, w_in_tb, sgu_norm_g[0], w_spatial[0], b_sp_t,
        w_pool[0], pool_scale, w_out_b, ts_mix, s1_ffn_b)
    gw_in = gw_in.reshape(N_DEV, IN_WIDTH // N_DEV, D)
    gw_out = gw_out.reshape(N_DEV, D // N_DEV, D)

    big = _final_add_adamw(coords, s1_ffn, list(r_ffn), [w_up_t, w_down[0]], [m_w_up[0].T, m_w_down[0]],
                           [v_w_up[0].T, v_w_down[0]])
    r_up, r_down = tuple(a.T[None] for a in big[0]), tuple(a[None] for a in big[1])

    dmod = jnp.concatenate([dmod_m, dmod_f], axis=0)
    names = ["b_ada", "pre_mix_g", "post_mix_g", "sgu_norm_g", "w_spatial", "b_spatial", "w_pool", "pool_scale",
             "pre_ffn_g", "post_ffn_g", "conv_w", "conv_b"]
    partials = [dmod.reshape(1, N_MOD * D), d_pre_mix, d_post_mix, d_sgn, d_wsp, d_bsp_t.T, d_wpool, d_ps,
                d_pre_ffn, d_post_ffn, d_cw8, d_cb8.reshape(1, 2 * FF), loss_lanes]
    small_w = [b_ada, pre_mix_g, post_mix_g, sgu_norm_g[0], w_spatial[0], b_spatial[0], w_pool[0], pool_scale,
               pre_ffn_g, post_ffn_g, conv_w[0], conv_b]
    small_m = [m_b_ada, m_pre_mix_g, m_post_mix_g, m_sgu_norm_g[0], m_w_spatial[0], m_b_spatial[0], m_w_pool[0],
               m_pool_scale, m_pre_ffn_g, m_post_ffn_g, m_conv_w[0], m_conv_b]
    small_v = [v_b_ada, v_pre_mix_g, v_post_mix_g, v_sgu_norm_g[0], v_w_spatial[0], v_b_spatial[0], v_w_pool[0],
               v_pool_scale, v_pre_ffn_g, v_post_ffn_g, v_conv_w[0], v_conv_b]
    g_mix, sums, dm2d = _tail_exchange([gw_in, gw_out], partials, [nm == "conv_w" for nm in names] + [False],
                                       [int(nm == "w_pool") for nm in names] + [0],
                                       dmod.reshape(N_DEV, 1, MOD_COLS))
    small, loss11 = _small_update(
        sums[:-1] + g_mix, small_w + [w_in_t, w_out[0]], small_m + [m_w_in[0].T, m_w_out[0]],
        small_v + [v_w_in[0].T, v_w_out[0]], scx, dm2d, w_ada[0], m_w_ada[0], v_w_ada[0], sums[-1])
    loss = loss11.reshape(())
    lead = {"sgu_norm_g", "w_spatial", "b_spatial", "w_pool", "conv_w", "w_in", "w_out", "w_ada"}
    res = {nm: tuple((a.T if nm == "w_in" else a)[None] if nm in lead else a for a in four)
           for nm, four in zip(names + ["w_in", "w_out", "w_ada"], small)}
    res.update(w_up=r_up, w_down=r_down)

    order = ["w_ada", "b_ada", "pre_mix_g", "post_mix_g", "w_in", "sgu_norm_g", "w_spatial", "b_spatial", "w_pool",
             "pool_scale", "w_out", "pre_ffn_g", "post_ffn_g", "w_up", "conv_w", "conv_b", "w_down"]
    return (loss, grad_x[None], *[res[nm][0] for nm in order], *[res[nm][1] for nm in order],
            *[res[nm][2] for nm in order], *[res[nm][3] for nm in order])
```

```python
import functools
import math

import jax
import jax.numpy as jnp
from jax import lax
from jax.experimental import pallas as pl
from jax.experimental.pallas import tpu as pltpu

F32 = jnp.float32
BF16 = jnp.bfloat16
MESH = pl.DeviceIdType.MESH

EPS = 1e-6
D = 1024
HEAD = 128
N_HEAD = 4
A_WIDTH = 512
B_WIDTH = 512
IN_WIDTH = 1536
WINDOWS = (2, 4, 8, 16)
CHUNK = 64
FF = 2816
N_DEV = 8
FF_CHUNK = 704
N_MOD = 6
MOD_COLS = 768

ADAM_LR = 0.001
ADAM_B1 = 0.9
ADAM_B2 = 0.999
ADAM_EPS = 1e-08
ADAM_WD = 0.01
ADAM_STEP = 10

VMEM_LIMIT_V7X = 62 * 1024 * 1024
HALO = 8
POOL_HALO = 16

VMEM_SPEC = pl.BlockSpec(memory_space=pltpu.VMEM)
ANY_SPEC = pl.BlockSpec(memory_space=pl.ANY)


def _bf(x):
    return x.astype(BF16)


def _mm(a, b):
    return jnp.dot(a, b, preferred_element_type=F32)


def _mm_nt(a, b):
    return lax.dot_general(a, b, (((1,), (1,)), ((), ())), preferred_element_type=F32)


def _mm_tn(a, b):
    return lax.dot_general(a, b, (((0,), (0,)), ((), ())), preferred_element_type=F32)


def _rstd(x):
    return lax.rsqrt(jnp.mean(x * x, axis=-1, keepdims=True) + EPS)


def _sum0(x):
    return jnp.sum(x, axis=0, keepdims=True)


def _rowmean(x):
    return jnp.mean(x, axis=-1, keepdims=True)


_GELU_K = math.sqrt(2.0 / math.pi)


def _gelu_and_grad(x):
    x2 = x * x
    th = jnp.tanh(_GELU_K * (x + 0.044715 * (x * x2)))
    cdf = 0.5 * th + 0.5
    grad = cdf + x * (1.0 - th * th) * ((0.5 * _GELU_K) + (1.5 * 0.044715 * _GELU_K) * x2)
    return x * cdf, grad


def _gelu(x):
    return x * (0.5 * (1.0 + jnp.tanh(_GELU_K * (x + 0.044715 * (x * x * x)))))


def _sigmoid(x):
    return 0.5 * jnp.tanh(0.5 * x) + 0.5


def _sgu_mask():
    ri = lax.broadcasted_iota(jnp.int32, (HEAD, HEAD), 0)
    ci = lax.broadcasted_iota(jnp.int32, (HEAD, HEAD), 1)
    return (ci // CHUNK) <= (ri // CHUNK)


def _window_sum(ext, w, trailing):
    n = ext.shape[0]
    s, k = ext, 1
    while k < w:
        s = s + pltpu.roll(s, k if trailing else n - k, 0)
        k *= 2
    return s


def _inv_count(row0, n, w):
    t = row0 + lax.broadcasted_iota(jnp.int32, (n, 1), 0)
    return 1.0 / jnp.minimum(t + 1, w).astype(F32)


def _shift_down(v, before, k):
    rows = lax.broadcasted_iota(jnp.int32, before.shape, 0)
    r = pltpu.roll(v, k, 0)
    top = jnp.where(rows < k, pltpu.roll(before, k, 0), r[0:HALO])
    return jnp.concatenate([top, r[HALO:]], axis=0)


def _shift_up(v, after, k):
    n = v.shape[0]
    rows = lax.broadcasted_iota(jnp.int32, after.shape, 0)
    r = pltpu.roll(v, n - k, 0)
    bottom = jnp.where(rows >= HALO - k, pltpu.roll(after, HALO - k, 0), r[n - HALO:])
    return jnp.concatenate([r[:n - HALO], bottom], axis=0)


def _adamw(w, g, m, v):
    m = ADAM_B1 * m + (1.0 - ADAM_B1) * g
    v = ADAM_B2 * v + (1.0 - ADAM_B2) * (g * g)
    m_hat = m / (1.0 - ADAM_B1 ** ADAM_STEP)
    v_hat = v / (1.0 - ADAM_B2 ** ADAM_STEP)
    delta = -ADAM_LR * (m_hat / (jnp.sqrt(v_hat) + ADAM_EPS) + ADAM_WD * w)
    return delta, m, v


def _coords():
    return lax.axis_index("x"), lax.axis_index("y"), lax.axis_index("c")


def _peer(k):
    x, y, c = _coords()
    return (x ^ ((k >> 2) & 1), y ^ ((k >> 1) & 1), c ^ (k & 1))


def _my_index():
    x, y, c = _coords()
    return 4 * x + 2 * y + c


def _adaln_modulation(c_ref, w_ref, b_ref, mod_ref, scx_ref, scbuf, stage, recv, send_sems, recv_sems):
    me = _my_index()
    cv = c_ref[...]
    scbuf[0] = cv * _sigmoid(cv)
    first = [
        pltpu.make_async_remote_copy(scbuf.at[0], scbuf.at[k], send_sems.at[0, k], recv_sems.at[0, k],
                                     device_id=_peer(k), device_id_type=MESH)
        for k in range(1, N_DEV)
    ]
    for cp in first:
        cp.start()
    for cp in first:
        cp.wait()
    scx_ref[...] = jnp.zeros(scx_ref.shape, F32)
    for k in range(N_DEV):
        scx_ref[k:k + 1, :] = scbuf[k]
    prod = _mm(_bf(scx_ref[...]), _bf(w_ref[...]))
    for k in range(N_DEV):
        stage[k] = prod[k:k + 1, :] + b_ref[me]
    second = [
        pltpu.make_async_remote_copy(stage.at[k], recv.at[k], send_sems.at[1, k], recv_sems.at[1, k],
                                     device_id=_peer(k), device_id_type=MESH)
        for k in range(1, N_DEV)
    ]
    for cp in second:
        cp.start()
    mod_ref[me] = stage[0]
    for cp in second:
        cp.wait()
    for k in range(1, N_DEV):
        mod_ref[me ^ k] = recv[k]


class _GatherSteps:
    def __init__(self, ins, outs, stages, send_sems, recv_sems, local_sems):
        self.ins, self.outs, self.stages = ins, outs, stages
        self.send_sems, self.recv_sems, self.local_sems = send_sems, recv_sems, local_sems
        x, y, c = _coords()
        self.c = c
        self.me, self.sibling = (x, y, c), (x, y, 1 - c)
        self.chips = [(1 - x, y), (x, 1 - y), (1 - x, 1 - y)]

    def _copy(self, a, k, block, to, from_stage=False):
        dst = self.outs[a].at[4 * block[0] + 2 * block[1] + block[2]]
        return pltpu.make_async_remote_copy(self.stages[a] if from_stage else dst, dst, self.send_sems.at[a, k],
                                            self.recv_sems.at[a, k], device_id=to, device_id_type=MESH)

    def _local(self, a):
        me = self.me
        return pltpu.make_async_copy(self.stages[a], self.outs[a].at[4 * me[0] + 2 * me[1] + me[2]],
                                     self.local_sems.at[a])

    def _first(self, a):
        cps = [self._copy(a, 0, self.me, self.sibling, from_stage=True)]
        return cps + [self._copy(a, 1 + j, self.me, (*chip, self.c), from_stage=True)
                      for j, chip in enumerate(self.chips)]

    def _passed(self, a, j):
        return self._copy(a, 4 + j, (*self.chips[j], self.c), self.sibling)

    def start(self):
        for a in range(len(self.ins)):
            block = self.ins[a][...]
            if block.shape != self.stages[a].shape:
                block = block.T
            self.stages[a][...] = block.astype(self.stages[a].dtype)
            self._local(a).start()
            for cp in self._first(a):
                cp.start()

    def forward(self):
        for a in range(len(self.ins)):
            for j, chip in enumerate(self.chips):
                self._copy(a, 1 + j, (*chip, self.c), self.me).wait_recv()
                self._passed(a, j).start()

    def finish(self):
        for a in range(len(self.ins)):
            self._copy(a, 0, self.sibling, self.me).wait_recv()
            for j, chip in enumerate(self.chips):
                self._copy(a, 4 + j, (*chip, 1 - self.c), self.me).wait_recv()
            for cp in self._first(a) + [self._passed(a, j) for j in range(3)]:
                cp.wait_send()
            self._local(a).wait()


def _gather_scratch(shapes, out_dtypes):
    n = len(shapes)
    return ([pltpu.VMEM(s, dt) for s, dt in zip(shapes, out_dtypes)]
            + [pltpu.SemaphoreType.DMA((n, 7)), pltpu.SemaphoreType.DMA((n, 7)), pltpu.SemaphoreType.DMA((n,))])


def _gather_out_shapes(shapes, out_dtypes):
    return tuple(jax.ShapeDtypeStruct((N_DEV, *s), dt) for s, dt in zip(shapes, out_dtypes))


def _prologue(c_row, w_ada, b_ada3, shards, out_dtypes):
    n = len(shards)

    def body(*refs):
        c_ref, w_ref, b_ref = refs[:3]
        mod_ref, scx_ref = refs[3 + n:5 + n]
        gather = _GatherSteps(refs[3:3 + n], refs[5 + n:5 + 2 * n], refs[5 + 2 * n:5 + 3 * n],
                              *refs[5 + 3 * n:8 + 3 * n])
        gather.start()
        _adaln_modulation(c_ref, w_ref, b_ref, mod_ref, scx_ref, *refs[8 + 3 * n:])
        gather.forward()
        gather.finish()

    outs = pl.pallas_call(
        body,
        name="prologue",
        out_shape=(jax.ShapeDtypeStruct((N_DEV, 1, MOD_COLS), F32), jax.ShapeDtypeStruct((2 * N_DEV, D), F32))
        + _gather_out_shapes([s.shape for s in shards], out_dtypes),
        in_specs=[VMEM_SPEC] * (3 + n),
        out_specs=(VMEM_SPEC, VMEM_SPEC) + (ANY_SPEC,) * n,
        scratch_shapes=_gather_scratch([s.shape for s in shards], out_dtypes) + [
            pltpu.VMEM((N_DEV, 1, D), F32),
            pltpu.VMEM((N_DEV, 1, MOD_COLS), F32),
            pltpu.VMEM((N_DEV, 1, MOD_COLS), F32),
            pltpu.SemaphoreType.DMA((2, N_DEV)),
            pltpu.SemaphoreType.DMA((2, N_DEV)),
        ],
        compiler_params=pltpu.CompilerParams(vmem_limit_bytes=VMEM_LIMIT_V7X),
    )(c_row, w_ada, b_ada3, *shards)
    return outs[0], outs[1], outs[2:]


HBM_SPEC = pl.BlockSpec(memory_space=pltpu.HBM)
SEM_SPEC = pl.BlockSpec(memory_space=pltpu.SEMAPHORE)
GATHER_COLLECTIVE_ID = 0


def _split_gather_copies(src_ref, land_ref, send_sems, recv_sems):
    me = _my_index()
    return [
        (pltpu.make_async_remote_copy(src_ref, land_ref.at[me], send_sems.at[k - 1], recv_sems.at[k - 1],
                                      device_id=_peer(k), device_id_type=MESH),
         pltpu.make_async_remote_copy(src_ref, land_ref.at[me ^ k], send_sems.at[k - 1], recv_sems.at[k - 1],
                                      device_id=_peer(k), device_id_type=MESH))
        for k in range(1, N_DEV)
    ]


def _gather_start(block):
    def body(src_ref, land_ref, send_sems, recv_sems, src_thru, land_thru, token):
        barrier = pltpu.get_barrier_semaphore()
        for k in range(1, N_DEV):
            pl.semaphore_signal(barrier, inc=1, device_id=_peer(k), device_id_type=MESH)
        pl.semaphore_wait(barrier, N_DEV - 1)
        for outgoing, _ in _split_gather_copies(src_ref, land_ref, send_sems, recv_sems):
            outgoing.start()
        token[...] = jnp.zeros_like(token)

    landing = lax.empty((N_DEV, *block.shape), block.dtype)
    return pl.pallas_call(
        body,
        name="gather_start",
        out_shape=(pltpu.SemaphoreType.DMA((N_DEV - 1,)), pltpu.SemaphoreType.DMA((N_DEV - 1,)),
                   pltpu.HBM(block.shape, block.dtype), pltpu.HBM(landing.shape, landing.dtype),
                   jax.ShapeDtypeStruct((8, 128), F32)),
        in_specs=(HBM_SPEC, HBM_SPEC),
        out_specs=(SEM_SPEC, SEM_SPEC, HBM_SPEC, HBM_SPEC, VMEM_SPEC),
        input_output_aliases={0: 2, 1: 3},
        compiler_params=pltpu.CompilerParams(has_side_effects=pltpu.SideEffectType.DATAFLOW_SIDE_EFFECTING,
                                             collective_id=GATHER_COLLECTIVE_ID),
    )(pltpu.with_memory_space_constraint(block, pltpu.HBM), pltpu.with_memory_space_constraint(landing, pltpu.HBM))


def _gather_wait(send_sems, recv_sems, src_thru, land_thru, after):
    def body(src_ref, land_ref, send_sems, recv_sems, after_ref, src_dead, got_ref):
        for outgoing, incoming in _split_gather_copies(src_ref, land_ref, send_sems, recv_sems):
            outgoing.wait_send()
            incoming.wait_recv()

    return pl.pallas_call(
        body,
        name="gather_wait",
        out_shape=(pltpu.HBM(src_thru.shape, src_thru.dtype), pltpu.HBM(land_thru.shape, land_thru.dtype)),
        in_specs=(HBM_SPEC, HBM_SPEC, SEM_SPEC, SEM_SPEC, ANY_SPEC),
        out_specs=(HBM_SPEC, HBM_SPEC),
        input_output_aliases={0: 0, 1: 1},
        compiler_params=pltpu.CompilerParams(has_side_effects=pltpu.SideEffectType.DATAFLOW_SIDE_EFFECTING),
    )(src_thru, land_thru, send_sems, recv_sems, after)[1]


class _ChipExchangeSteps:
    FLIPS = ((1, 0), (0, 1), (1, 1))

    def __init__(self, srcs, dsts, send_sems, recv_sems):
        self.srcs, self.dsts, self.send_sems, self.recv_sems = srcs, dsts, send_sems, recv_sems

    def _copies(self):
        x, y, c = _coords()
        out = []
        for a in range(len(self.srcs)):
            for j, (fx, fy) in enumerate(self.FLIPS):
                k = 3 * a + j
                out.append(pltpu.make_async_remote_copy(
                    self.srcs[a].at[x ^ fx, y ^ fy], self.dsts[a].at[j], self.send_sems.at[k], self.recv_sems.at[k],
                    device_id=(x ^ fx, y ^ fy, c), device_id_type=MESH))
        return out

    def start(self):
        for cp in self._copies():
            cp.start()

    def finish(self):
        for cp in self._copies():
            cp.wait()


def _mix_fwd(x, mod, g_pre, g_post, w_in_t, sgn, w_sp, b_sp_t, w_pool, p_scale, w_out_b, ts, shards, shard_shapes,
             shard_dtypes):
    t_len = x.shape[0]
    nt, nb = t_len // ts, ts // HEAD
    ns = len(shards)

    def body(*refs):
        (x_ref, mod_ref, g1_ref, g2_ref, win_ref, sgn_ref, ws_ref, bst_ref, wp_ref, ps_ref, wout_ref) = refs[:11]
        x1_ref, proj_ref, mixed_ref = refs[11 + ns:14 + ns]
        pbuf, cat = refs[14 + 2 * ns:16 + 2 * ns]
        gather = _GatherSteps(refs[11:11 + ns], refs[14 + ns:14 + 2 * ns], refs[16 + 2 * ns:16 + 3 * ns],
                              *refs[16 + 3 * ns:])
        i = pl.program_id(0)

        @pl.when(i == 0)
        def _():
            pbuf[0:POOL_HALO, :] = jnp.zeros((POOL_HALO, B_WIDTH), F32)
            gather.start()

        @pl.when(i == (3 * nt) // 4)
        def _():
            gather.forward()

        xv = x_ref[...]
        sh, sc, gm = mod_ref[0:1, :], mod_ref[1:2, :], mod_ref[2:3, :]
        h = (xv * _rstd(xv) * g1_ref[...]) * (1.0 + sc) + sh
        proj_ref[...] = _mm_nt(_bf(h), win_ref[...])
        pbuf[POOL_HALO:POOL_HALO + ts, :] = proj_ref[:, 2 * A_WIDTH:]
        smask = _sgu_mask()
        for hd in range(N_HEAD):
            u = _gelu(proj_ref[:, hd * HEAD:(hd + 1) * HEAD])
            v = _gelu(proj_ref[:, A_WIDTH + hd * HEAD:A_WIDTH + (hd + 1) * HEAD])
            vn = _bf(v * _rstd(v) * sgn_ref[hd:hd + 1, :])
            wm = _bf(jnp.where(smask, ws_ref[hd], 0.0))
            bias = bst_ref[:, hd:hd + 1]
            for b in range(nb):
                rows = slice(b * HEAD, (b + 1) * HEAD)
                z = _mm(wm, vn[rows]) + bias
                cat[rows, hd * HEAD:(hd + 1) * HEAD] = _bf(u[rows] * z)
        for g, w in enumerate(WINDOWS):
            cols = slice(g * HEAD, (g + 1) * HEAD)
            ext = pbuf[:, cols]
            pooled = _window_sum(ext, w, True)[POOL_HALO:] * _inv_count(i * ts, ts, w) - ext[POOL_HALO:]
            cat[:, A_WIDTH + g * HEAD:A_WIDTH + (g + 1) * HEAD] = _bf(_mm(_bf(pooled), _bf(wp_ref[g])) * ps_ref[:, cols])
        pbuf[0:POOL_HALO, :] = pbuf[ts:ts + POOL_HALO, :]
        mixed = _mm(cat[...], wout_ref[...])
        mixed_ref[...] = mixed
        x1_ref[...] = xv + gm * (mixed * _rstd(mixed) * g2_ref[...])

        @pl.when(i == nt - 1)
        def _():
            gather.finish()

    tile = lambda wid: pl.BlockSpec((ts, wid), lambda i: (i, 0))
    outs = pl.pallas_call(
        body,
        name="mix_fwd",
        grid=(nt,),
        out_shape=(jax.ShapeDtypeStruct((t_len, D), F32), jax.ShapeDtypeStruct((t_len, IN_WIDTH), F32),
                   jax.ShapeDtypeStruct((t_len, D), F32)) + _gather_out_shapes(shard_shapes, shard_dtypes),
        in_specs=[tile(D)] + [VMEM_SPEC] * (10 + ns),
        out_specs=(tile(D), tile(IN_WIDTH), tile(D)) + (ANY_SPEC,) * ns,
        scratch_shapes=[pltpu.VMEM((POOL_HALO + ts, B_WIDTH), F32), pltpu.VMEM((ts, D), BF16)]
        + _gather_scratch(shard_shapes, shard_dtypes),
        compiler_params=pltpu.CompilerParams(dimension_semantics=("arbitrary",), vmem_limit_bytes=VMEM_LIMIT_V7X),
    )(x, mod, g_pre, g_post, w_in_t, sgn, w_sp, b_sp_t, w_pool, p_scale, w_out_b, *shards)
    return outs[:3], outs[3:]


def _ffn_fwd(x1, target, mod, g_pre, g_post, w_up_b, conv_w8, conv_b8, w_down_b, ts):
    t_len = x1.shape[0]
    nt = t_len // ts

    def body(x1_ref, tgt_ref, mod_ref, g3_ref, g4_ref, wup_ref, cw_ref, cb_ref, wdown_ref,
             up_ref, f_ref, dx2_ref, loss_ref, ucarry):
        i = pl.program_id(0)

        @pl.when(i == 0)
        def _():
            ucarry[...] = jnp.zeros(ucarry.shape, F32)
            loss_ref[...] = jnp.zeros(loss_ref.shape, F32)

        x1v = x1_ref[...]
        sh, sc, gf = mod_ref[3:4, :], mod_ref[4:5, :], mod_ref[5:6, :]
        h2 = _bf((x1v * _rstd(x1v) * g3_ref[...]) * (1.0 + sc) + sh)
        half = N_DEV // 2

        def up_pair(j):
            return [_mm(h2, wup_ref[jj]) for jj in (j, j + half)]

        f = jnp.zeros((ts, D), F32)
        ups = up_pair(0)
        for j in range(half):
            nxt = up_pair(j + 1) if j + 1 < half else None
            ys = []
            for up, jj in zip(ups, (j, j + half)):
                up_ref[jj] = up
                before = ucarry[jj]
                ucarry[jj] = up[ts - HALO:, :]
                cw = cw_ref[jj]
                ys.append(cb_ref[jj:jj + 1, :] + _shift_down(up, before, 2) * cw[0:1, :]
                          + _shift_down(up, before, 1) * cw[1:2, :] + up * cw[2:3, :])
            gate, val = ys
            act = gate * _sigmoid(gate) * val
            f = f + _mm(_bf(act), wdown_ref[j * FF_CHUNK:(j + 1) * FF_CHUNK, :])
            ups = nxt
        f_ref[...] = f
        x2 = x1v + gf * (f * _rstd(f) * g4_ref[...])
        err = x2 - tgt_ref[...]
        loss_ref[...] += _sum0(err * err)
        dx2_ref[...] = err * (1.0 / D)

    tile = pl.BlockSpec((ts, D), lambda i: (i, 0))
    return pl.pallas_call(
        body,
        name="ffn_fwd",
        grid=(nt,),
        out_shape=(jax.ShapeDtypeStruct((N_DEV, t_len, FF_CHUNK), F32), jax.ShapeDtypeStruct((t_len, D), F32),
                   jax.ShapeDtypeStruct((t_len, D), F32), jax.ShapeDtypeStruct((1, D), F32)),
        in_specs=[tile, tile] + [VMEM_SPEC] * 7,
        out_specs=(pl.BlockSpec((N_DEV, ts, FF_CHUNK), lambda i: (0, i, 0)), tile, tile,
                   pl.BlockSpec((1, D), lambda i: (0, 0))),
        scratch_shapes=[pltpu.VMEM((N_DEV, HALO, FF_CHUNK), F32)],
        compiler_params=pltpu.CompilerParams(dimension_semantics=("arbitrary",), vmem_limit_bytes=VMEM_LIMIT_V7X),
    )(x1, target, mod, g_pre, g_post, w_up_b, conv_w8, conv_b8, w_down_b)


def _ffn_bwd(dx2, f, x1, up, mod, g_pre, g_post, w_up_b, conv_w8, conv_b8, w_down_b, ts):
    t_len = x1.shape[0]
    nt = t_len // ts
    half = N_DEV // 2

    def body(dx2_ref, f_ref, x1_ref, up_ref, halo_ref, mod_ref, g3_ref, g4_ref, wup_ref, cw_ref, cb_ref, wdown_ref,
             dx1_ref, dup_ref, act_ref, df_ref, h2_ref, dmod_ref, dg3_ref, dg4_ref, dcb_ref, dcw_ref,
             dycarry, dh2acc):
        i = pl.program_id(0)
        r = nt - 1 - i

        @pl.when(i == 0)
        def _():
            for ref in (dmod_ref, dg3_ref, dg4_ref, dcb_ref, dcw_ref, dycarry):
                ref[...] = jnp.zeros(ref.shape, F32)

        dx2v, fv, x1v = dx2_ref[...], f_ref[...], x1_ref[...]
        sh, sc, gf = mod_ref[3:4, :], mod_ref[4:5, :], mod_ref[5:6, :]
        g3, g4 = g3_ref[...], g4_ref[...]
        rstd4 = _rstd(fv)
        fh = fv * rstd4
        dmod_ref[2:3, :] += _sum0(dx2v * (fh * g4))
        dr = dx2v * gf
        dg4_ref[...] += _sum0(dr * fh)
        dfh = dr * g4
        dfb = _bf(rstd4 * (dfh - fh * _rowmean(dfh * fh)))
        df_ref[...] = dfb
        rstd3 = _rstd(x1v)
        xh = x1v * rstd3
        n3 = xh * g3
        h2_ref[...] = _bf(n3 * (1.0 + sc) + sh)
        dh2acc[...] = jnp.zeros((ts, D), F32)
        keep = jnp.where(r > 0, 1.0, 0.0).astype(F32)

        def dact_of(j):
            return _mm_nt(dfb, wdown_ref[j * FF_CHUNK:(j + 1) * FF_CHUNK, :])

        dact_next = dact_of(0)
        for j in range(half):
            dact = dact_next
            if j + 1 < half:
                dact_next = dact_of(j + 1)
            ys = []
            for jj in (j, j + half):
                before = halo_ref[jj] * keep
                upc = up_ref[jj]
                cw = cw_ref[jj]
                ys.append(cb_ref[jj:jj + 1, :] + _shift_down(upc, before, 2) * cw[0:1, :]
                          + _shift_down(upc, before, 1) * cw[1:2, :] + upc * cw[2:3, :])
            gate, val = ys
            sg = _sigmoid(gate)
            silu = gate * sg
            act_ref[j] = _bf(silu * val)
            dys = (dact * val * (sg + silu * (1.0 - sg)), dact * silu)
            for q, jj in enumerate((j, j + half)):
                dy = dys[q]
                cw = cw_ref[jj]
                dcb_ref[jj:jj + 1, :] += _sum0(dy)
                after = dycarry[jj]
                dycarry[jj] = dy[0:HALO, :]
                dy1, dy2 = _shift_up(dy, after, 1), _shift_up(dy, after, 2)
                upc = up_ref[jj]
                dcw_ref[jj, 0:1, :] += _sum0(dy2 * upc)
                dcw_ref[jj, 1:2, :] += _sum0(dy1 * upc)
                dcw_ref[jj, 2:3, :] += _sum0(dy * upc)
                dup = _bf(dy * cw[2:3, :] + dy1 * cw[1:2, :] + dy2 * cw[0:1, :])
                dup_ref[jj] = dup
                dh2acc[...] += _mm_nt(dup, wup_ref[jj])
        dh2 = dh2acc[...]
        dmod_ref[0:1, :] += _sum0(dh2)
        dmod_ref[1:2, :] += _sum0(dh2 * n3)
        dn3 = dh2 * (1.0 + sc)
        dg3_ref[...] += _sum0(dn3 * xh)
        dxh = dn3 * g3
        dx1_ref[...] = dx2v + rstd3 * (dxh - xh * _rowmean(dxh * xh))

    tile = pl.BlockSpec((ts, D), lambda i: (nt - 1 - i, 0))
    chunked = lambda n: pl.BlockSpec((n, ts, FF_CHUNK), lambda i: (0, nt - 1 - i, 0))
    halo = pl.BlockSpec((N_DEV, HALO, FF_CHUNK), lambda i: (0, jnp.maximum((nt - 1 - i) * (ts // HALO) - 1, 0), 0))
    const = lambda *shape: pl.BlockSpec(shape, lambda i: (0,) * len(shape))
    return pl.pallas_call(
        body,
        name="ffn_bwd",
        grid=(nt,),
        out_shape=(jax.ShapeDtypeStruct((t_len, D), F32), jax.ShapeDtypeStruct((N_DEV, t_len, FF_CHUNK), BF16),
                   jax.ShapeDtypeStruct((half, t_len, FF_CHUNK), BF16), jax.ShapeDtypeStruct((t_len, D), BF16),
                   jax.ShapeDtypeStruct((t_len, D), BF16), jax.ShapeDtypeStruct((3, D), F32),
                   jax.ShapeDtypeStruct((1, D), F32), jax.ShapeDtypeStruct((1, D), F32),
                   jax.ShapeDtypeStruct((N_DEV, FF_CHUNK), F32), jax.ShapeDtypeStruct((N_DEV, 3, FF_CHUNK), F32)),
        in_specs=[tile, tile, tile, chunked(N_DEV), halo] + [VMEM_SPEC] * 7,
        out_specs=(tile, chunked(N_DEV), chunked(half), tile, tile, const(3, D), const(1, D), const(1, D),
                   const(N_DEV, FF_CHUNK), const(N_DEV, 3, FF_CHUNK)),
        scratch_shapes=[pltpu.VMEM((N_DEV, HALO, FF_CHUNK), F32), pltpu.VMEM((ts, D), F32)],
        compiler_params=pltpu.CompilerParams(dimension_semantics=("arbitrary",), vmem_limit_bytes=VMEM_LIMIT_V7X),
    )(dx2, f, x1, up, up, mod, g_pre, g_post, w_up_b, conv_w8, conv_b8, w_down_b)


def _wgrad_up(h2, dup, ts):
    t_len = h2.shape[0]
    nt, half = t_len // ts, N_DEV // 2

    def body(h2_ref, dup_ref, out_ref):
        @pl.when(pl.program_id(1) == 0)
        def _():
            out_ref[...] = jnp.zeros(out_ref.shape, F32)

        for q in range(half):
            out_ref[q] += _mm_tn(dup_ref[q], h2_ref[...])

    return pl.pallas_call(
        body,
        name="wgrad_up",
        grid=(2, nt),
        out_shape=jax.ShapeDtypeStruct((N_DEV, FF_CHUNK, D), F32),
        in_specs=[pl.BlockSpec((ts, D), lambda g, t: (t, 0)), pl.BlockSpec((half, ts, FF_CHUNK), lambda g, t: (g, t, 0))],
        out_specs=pl.BlockSpec((half, FF_CHUNK, D), lambda g, t: (g, 0, 0)),
        compiler_params=pltpu.CompilerParams(dimension_semantics=("arbitrary", "arbitrary"),
                                             vmem_limit_bytes=VMEM_LIMIT_V7X),
    )(h2, dup)


def _sibling_swap_copies(srcs, dsts, send_sems, recv_sems):
    x, y, c = _coords()
    return [
        pltpu.make_async_remote_copy(srcs[a].at[xs, ys, 1 - c], dsts[a].at[xs, ys], send_sems.at[a, 2 * xs + ys],
                                     recv_sems.at[a, 2 * xs + ys], device_id=(x, y, 1 - c), device_id_type=MESH)
        for a in range(len(srcs)) for xs in range(2) for ys in range(2)
    ]


def _wgrad_down(act, df, ts, swap_src):
    t_len = df.shape[0]
    nt, half = t_len // ts, N_DEV // 2

    def body(act_ref, df_ref, src_ref, out_ref, dst_ref, send_sems, recv_sems):
        t = pl.program_id(0)

        @pl.when(t == 0)
        def _():
            for cp in _sibling_swap_copies([src_ref], [dst_ref], send_sems, recv_sems):
                cp.start()
            out_ref[...] = jnp.zeros(out_ref.shape, F32)

        for q in range(half):
            out_ref[q] += _mm_tn(act_ref[q], df_ref[...])

        @pl.when(t == nt - 1)
        def _():
            for cp in _sibling_swap_copies([src_ref], [dst_ref], send_sems, recv_sems):
                cp.wait()

    return pl.pallas_call(
        body,
        name="wgrad_down",
        grid=(nt,),
        out_shape=(jax.ShapeDtypeStruct((half, FF_CHUNK, D), F32), jax.ShapeDtypeStruct(swap_src.shape[1:], F32)),
        in_specs=[pl.BlockSpec((half, ts, FF_CHUNK), lambda t: (0, t, 0)), pl.BlockSpec((ts, D), lambda t: (t, 0)),
                  ANY_SPEC],
        out_specs=(pl.BlockSpec((half, FF_CHUNK, D), lambda t: (0, 0, 0)), ANY_SPEC),
        scratch_shapes=[pltpu.SemaphoreType.DMA((1, 4)), pltpu.SemaphoreType.DMA((1, 4))],
        compiler_params=pltpu.CompilerParams(dimension_semantics=("arbitrary",), vmem_limit_bytes=VMEM_LIMIT_V7X),
    )(act, df, swap_src)


def _mix_bwd(dx1, x, proj, mixed, mod, g_pre, g_post, w_in_t, sgn, w_sp, b_sp_t, w_pool, p_scale, w_out_b, ts, rs_srcs):
    t_len = x.shape[0]
    nt, nb = t_len // ts, ts // HEAD
    nr = len(rs_srcs)

    def body(*refs):
        (dx1_ref, x_ref, proj_ref, halo_ref, mixed_ref, mod_ref, g1_ref, g2_ref, win_ref, sgn_ref, ws_ref,
         bst_ref, wp_ref, ps_ref, wout_ref) = refs[:15]
        (gx_ref, dwin_ref, dwout_ref, dmod_ref, dg1_ref, dg2_ref, dsgn_ref, dws_ref, dbst_ref, dwp_ref,
         dps_ref) = refs[15 + nr:26 + nr]
        pbuf, dwsbuf, cat, dproj, dcat = refs[26 + 2 * nr:31 + 2 * nr]
        exchange = _ChipExchangeSteps(refs[15:15 + nr], refs[26 + nr:26 + 2 * nr], *refs[31 + 2 * nr:])
        i = pl.program_id(0)
        r = nt - 1 - i

        @pl.when(i == 0)
        def _():
            exchange.start()
            for ref in (dwin_ref, dwout_ref, dmod_ref, dg1_ref, dg2_ref, dsgn_ref, dws_ref, dbst_ref, dwp_ref, dps_ref):
                ref[...] = jnp.zeros(ref.shape, F32)
            dwsbuf[ts:ts + POOL_HALO, :] = jnp.zeros((POOL_HALO, B_WIDTH), F32)

        xv, dx1v, mixed = x_ref[...], dx1_ref[...], mixed_ref[...]
        sh, sc, gm = mod_ref[0:1, :], mod_ref[1:2, :], mod_ref[2:3, :]
        g1, g2 = g1_ref[...], g2_ref[...]
        rstd2 = _rstd(mixed)
        mh = mixed * rstd2
        dmod_ref[2:3, :] += _sum0(dx1v * (mh * g2))
        dr = dx1v * gm
        dg2_ref[...] += _sum0(dr * mh)
        dmh = dr * g2
        dmb = _bf(rstd2 * (dmh - mh * _rowmean(dmh * mh)))
        dcat[...] = _mm_nt(dmb, wout_ref[...])
        smask = _sgu_mask()
        for hd in range(N_HEAD):
            ucols = slice(hd * HEAD, (hd + 1) * HEAD)
            vcols = slice(A_WIDTH + hd * HEAD, A_WIDTH + (hd + 1) * HEAD)
            u, du_dp = _gelu_and_grad(proj_ref[:, ucols])
            v, dv_dp = _gelu_and_grad(proj_ref[:, vcols])
            rs = _rstd(v)
            vhat = v * rs
            gn = sgn_ref[hd:hd + 1, :]
            vn = _bf(vhat * gn)
            wm = _bf(jnp.where(smask, ws_ref[hd], 0.0))
            bias = bst_ref[:, hd:hd + 1]
            dzsum = jnp.zeros((HEAD, HEAD), F32)
            dwm = jnp.zeros((HEAD, HEAD), F32)
            dvn_parts = []
            for b in range(nb):
                rows = slice(b * HEAD, (b + 1) * HEAD)
                z = _mm(wm, vn[rows]) + bias
                da = dcat[rows, ucols]
                cat[rows, ucols] = _bf(u[rows] * z)
                dz = da * u[rows]
                dzsum = dzsum + dz
                dzb = _bf(dz)
                dwm = dwm + _mm_nt(dzb, vn[rows])
                dvn_parts.append(_mm_tn(wm, dzb))
                dproj[rows, ucols] = _bf((da * z) * du_dp[rows])
            dvn = jnp.concatenate(dvn_parts, axis=0)
            dsgn_ref[hd:hd + 1, :] += _sum0(dvn * vhat)
            dvh = dvn * gn
            dproj[:, vcols] = _bf((rs * (dvh - vhat * _rowmean(dvh * vhat))) * dv_dp)
            dws_ref[hd] += jnp.where(smask, dwm, 0.0)
            dbst_ref[:, hd:hd + 1] += jnp.sum(dzsum, axis=1, keepdims=True)
        keep = jnp.where(r > 0, 1.0, 0.0).astype(F32)
        pbuf[0:POOL_HALO, :] = halo_ref[...] * keep
        pbuf[POOL_HALO:POOL_HALO + ts, :] = proj_ref[:, 2 * A_WIDTH:]
        for g, w in enumerate(WINDOWS):
            cols = slice(g * HEAD, (g + 1) * HEAD)
            ccols = slice(A_WIDTH + g * HEAD, A_WIDTH + (g + 1) * HEAD)
            pcols = slice(2 * A_WIDTH + g * HEAD, 2 * A_WIDTH + (g + 1) * HEAD)
            wpg = _bf(wp_ref[g])
            psg = ps_ref[:, cols]
            ext = pbuf[:, cols]
            inv = _inv_count(r * ts, ts, w)
            pb = _bf(_window_sum(ext, w, True)[POOL_HALO:] * inv - ext[POOL_HALO:])
            yb = _mm(pb, wpg)
            dob = dcat[:, ccols]
            cat[:, ccols] = _bf(yb * psg)
            dps_ref[:, cols] += _sum0(dob * yb)
            dyb = _bf(dob * psg)
            dwp_ref[g] += _mm_tn(pb, dyb)
            dpooled = _mm_nt(dyb, wpg)
            dwsbuf[0:ts, cols] = dpooled * inv
            dproj[:, pcols] = _bf(_window_sum(dwsbuf[:, cols], w, False)[0:ts] - dpooled)
        dwsbuf[ts:ts + POOL_HALO, :] = dwsbuf[0:POOL_HALO, :]
        dpb = dproj[...]
        rstd1 = _rstd(xv)
        xh = xv * rstd1
        n1 = xh * g1
        dwin_ref[...] += _mm_tn(dpb, _bf(n1 * (1.0 + sc) + sh))
        dwout_ref[...] += _mm_tn(cat[...], dmb)
        dh = _mm(dpb, win_ref[...])
        dmod_ref[0:1, :] += _sum0(dh)
        dmod_ref[1:2, :] += _sum0(dh * n1)
        dn1 = dh * (1.0 + sc)
        dg1_ref[...] += _sum0(dn1 * xh)
        dxh = dn1 * g1
        gx_ref[...] = dx1v + rstd1 * (dxh - xh * _rowmean(dxh * xh))

        @pl.when(i == nt - 1)
        def _():
            exchange.finish()

    tile = lambda wid: pl.BlockSpec((ts, wid), lambda i: (nt - 1 - i, 0))
    halo = pl.BlockSpec((POOL_HALO, B_WIDTH),
                        lambda i: (jnp.maximum((nt - 1 - i) * (ts // POOL_HALO) - 1, 0), 2 * A_WIDTH // B_WIDTH))
    const = lambda *shape: pl.BlockSpec(shape, lambda i: (0,) * len(shape))
    resident = lambda *shape: pl.BlockSpec(shape, lambda i: (0,) * len(shape), pipeline_mode=pl.Buffered(1))
    outs = pl.pallas_call(
        body,
        name="mix_bwd",
        grid=(nt,),
        out_shape=(jax.ShapeDtypeStruct((t_len, D), F32), jax.ShapeDtypeStruct((IN_WIDTH, D), F32),
                   jax.ShapeDtypeStruct((D, D), F32), jax.ShapeDtypeStruct((3, D), F32),
                   jax.ShapeDtypeStruct((1, D), F32), jax.ShapeDtypeStruct((1, D), F32),
                   jax.ShapeDtypeStruct((N_HEAD, HEAD), F32), jax.ShapeDtypeStruct((N_HEAD, HEAD, HEAD), F32),
                   jax.ShapeDtypeStruct((HEAD, N_HEAD), F32), jax.ShapeDtypeStruct((N_HEAD, HEAD, HEAD), F32),
                   jax.ShapeDtypeStruct((1, B_WIDTH), F32))
        + tuple(jax.ShapeDtypeStruct((3, *s.shape[2:]), s.dtype) for s in rs_srcs),
        in_specs=[tile(D), tile(D), tile(IN_WIDTH), halo, tile(D)] + [VMEM_SPEC] * 10 + [ANY_SPEC] * nr,
        out_specs=(tile(D), resident(IN_WIDTH, D), resident(D, D), const(3, D), const(1, D), const(1, D),
                   const(N_HEAD, HEAD), const(N_HEAD, HEAD, HEAD), const(HEAD, N_HEAD), const(N_HEAD, HEAD, HEAD),
                   const(1, B_WIDTH)) + (ANY_SPEC,) * nr,
        scratch_shapes=[pltpu.VMEM((POOL_HALO + ts, B_WIDTH), F32), pltpu.VMEM((ts + POOL_HALO, B_WIDTH), F32),
                        pltpu.VMEM((ts, D), BF16), pltpu.VMEM((ts, IN_WIDTH), BF16), pltpu.VMEM((ts, D), F32),
                        pltpu.SemaphoreType.DMA((3 * nr,)), pltpu.SemaphoreType.DMA((3 * nr,))],
        compiler_params=pltpu.CompilerParams(dimension_semantics=("arbitrary",), vmem_limit_bytes=VMEM_LIMIT_V7X),
    )(dx1, x, proj, proj, mixed, mod, g_pre, g_post, w_in_t, sgn, w_sp, b_sp_t, w_pool, p_scale, w_out_b, *rs_srcs)
    return outs[:11], outs[11:]


def _pair_add(name, coords, grid, specs_a, specs_b, out_specs, out_shapes, a_arrays, b_arrays, swap_srcs):
    n, ns = len(a_arrays), len(swap_srcs)
    last = tuple(g - 1 for g in grid)

    def body(co_ref, *refs):
        ids = [pl.program_id(d) for d in range(len(grid))]
        swap = refs[2 * n:2 * n + ns], refs[4 * n + ns:4 * n + 2 * ns], *refs[4 * n + 2 * ns:]
        if ns:
            @pl.when(functools.reduce(jnp.logical_and, [i == 0 for i in ids]))
            def _():
                for cp in _sibling_swap_copies(*swap):
                    cp.start()

        for k in range(n):
            total = refs[k][...] + refs[n + k][...]
            refs[2 * n + ns + k][...] = total
            refs[3 * n + ns + k][...] = _bf(total)

        if ns:
            @pl.when(functools.reduce(jnp.logical_and, [i == e for i, e in zip(ids, last)]))
            def _():
                for cp in _sibling_swap_copies(*swap):
                    cp.wait()

    outs = pl.pallas_call(
        body,
        name=name,
        grid_spec=pltpu.PrefetchScalarGridSpec(
            num_scalar_prefetch=1, grid=grid, in_specs=specs_a + specs_b + [ANY_SPEC] * ns,
            out_specs=out_specs * 2 + [ANY_SPEC] * ns,
            scratch_shapes=[pltpu.SemaphoreType.DMA((ns, 4)), pltpu.SemaphoreType.DMA((ns, 4))] if ns else []),
        out_shape=tuple(jax.ShapeDtypeStruct(s, dt) for dt in (F32, BF16) for s in out_shapes)
        + tuple(jax.ShapeDtypeStruct(g.shape[1:], F32) for g in swap_srcs),
        compiler_params=pltpu.CompilerParams(dimension_semantics=("arbitrary",) * len(grid),
                                             vmem_limit_bytes=VMEM_LIMIT_V7X),
    )(coords, *a_arrays, *b_arrays, *swap_srcs)
    return list(outs[:n]), list(outs[n:2 * n]), list(outs[2 * n:])


def _final_add_adamw(coords, s1, r, ws, ms, vs, n_split=4):
    n = len(s1)

    def body(co_ref, *refs):
        for k in range(n):
            s_ref, r_ref, w_ref, m_ref, v_ref = (refs[q * n + k] for q in range(5))
            g_ref, d_ref, nm_ref, nv_ref = (refs[(5 + q) * n + k] for q in range(4))
            g = ((s_ref[...] + r_ref[0].astype(F32)) + r_ref[1].astype(F32)) + r_ref[2].astype(F32)
            g_ref[...] = g
            delta, m, v = _adamw(w_ref[...], g, m_ref[...], v_ref[...])
            d_ref[...] = delta
            nm_ref[...] = m
            nv_ref[...] = v

    def shard_spec(a):
        rows, cols = a.shape
        return pl.BlockSpec((rows // n_split, cols), lambda i, co: (i, 0))

    def mine_spec(a):
        rows, cols = a.shape[2:]
        return pl.BlockSpec((None, None, rows // n_split, cols), lambda i, co: (co[0], co[1], i, 0))

    def recv_spec(a):
        rows, cols = a.shape[1:]
        return pl.BlockSpec((3, rows // n_split, cols), lambda i, co: (0, i, 0))

    in_specs = ([mine_spec(a) for a in s1] + [recv_spec(a) for a in r] + [shard_spec(a) for a in ws] * 3)
    out_specs = [shard_spec(a) for a in ws] * 4
    outs = pl.pallas_call(
        body,
        name="grad_final_adamw",
        grid_spec=pltpu.PrefetchScalarGridSpec(num_scalar_prefetch=1, grid=(n_split,), in_specs=in_specs,
                                               out_specs=out_specs),
        out_shape=tuple(jax.ShapeDtypeStruct(a.shape, F32) for a in ws) * 4,
        compiler_params=pltpu.CompilerParams(dimension_semantics=("arbitrary",), vmem_limit_bytes=VMEM_LIMIT_V7X),
    )(coords, *s1, *r, *ws, *ms, *vs)
    return [tuple(outs[q * n + k] for q in range(4)) for k in range(n)]


def _sibling_add(tag, g5, r1, coords, swap_srcs=(), n_split=4):
    shapes = [g.shape[3:] for g in g5]
    spec_g = [pl.BlockSpec((None, None, None, s[0] // n_split, s[1]), lambda i, j, k, co: (i, j, co[2], k, 0))
              for s in shapes]
    spec_r = [pl.BlockSpec((None, None, s[0] // n_split, s[1]), lambda i, j, k, co: (i, j, k, 0)) for s in shapes]
    return _pair_add("grad_add_core_" + tag, coords, (2, 2, n_split), spec_g, spec_r, spec_r,
                     [(2, 2, *s) for s in shapes], g5, r1, list(swap_srcs))


def _tail_exchange(big, partials, pick_mine, y_first, dmod3):
    n, nb = len(partials), len(big)
    big_shapes = [g.shape[1:] for g in big]
    big5 = [g.reshape(2, 2, 2, *s) for g, s in zip(big, big_shapes)]
    flips = _ChipExchangeSteps.FLIPS

    def body(*refs):
        g5, p_in, dm_ref = refs[:nb], refs[nb:nb + n], refs[nb + n]
        outs = refs[nb + n + 1:2 * nb + 2 * n + 2]
        g_out, sums, dm2d = outs[:nb], outs[nb:nb + n], outs[nb + n]
        scratch = refs[2 * nb + 2 * n + 2:]
        s1, stage, chip_recv = scratch[:nb], scratch[nb:2 * nb], scratch[2 * nb:3 * nb]
        acc, rbuf = scratch[3 * nb:3 * nb + n], scratch[3 * nb + n:3 * nb + 2 * n]
        (dm_recv, send_sems, recv_sems, dm_send_sems, dm_recv_sems, sib_send, sib_recv, chip_send,
         chip_recv_sems) = scratch[3 * nb + 2 * n:]
        x, y, c = _coords()
        me = 4 * x + 2 * y + c
        sibling = (x, y, 1 - c)
        dm_copies = [
            pltpu.make_async_remote_copy(dm_ref.at[me ^ k], dm_recv.at[k], dm_send_sems.at[k], dm_recv_sems.at[k],
                                         device_id=_peer(k), device_id_type=MESH)
            for k in range(1, N_DEV)
        ]
        for cp in dm_copies:
            cp.start()
        sib_copies = _sibling_swap_copies(g5, s1, sib_send, sib_recv)
        for cp in sib_copies:
            cp.start()
        for a in range(n):
            acc[a][...] = p_in[a][...]

        def small_phase(ph, peers):
            copies = [
                pltpu.make_async_remote_copy(acc[a], rbuf[a].at[ph], send_sems.at[ph, a], recv_sems.at[ph, a],
                                             device_id=peers[y_first[a]], device_id_type=MESH)
                for a in range(n)
            ]
            for cp in copies:
                cp.start()
            for cp in copies:
                cp.wait()
            for a in range(n):
                acc[a][...] = acc[a][...] + rbuf[a][ph]

        small_phase(0, (sibling, sibling))
        for cp in sib_copies:
            cp.wait()
        for a in range(nb):
            for xs in range(2):
                for ys in range(2):
                    total = g5[a][xs, ys, c] + s1[a][xs, ys]
                    s1[a][xs, ys] = total
                    stage[a][xs, ys] = _bf(total)
        chip_copies = [
            pltpu.make_async_remote_copy(stage[a].at[x ^ fx, y ^ fy], chip_recv[a].at[j], chip_send.at[a, j],
                                         chip_recv_sems.at[a, j], device_id=(x ^ fx, y ^ fy, c), device_id_type=MESH)
            for a in range(nb) for j, (fx, fy) in enumerate(flips)
        ]
        for cp in chip_copies:
            cp.start()
        x_peer, y_peer = (1 - x, y, c), (x, 1 - y, c)
        small_phase(1, (x_peer, y_peer))
        small_phase(2, (y_peer, x_peer))
        for a in range(n):
            sums[a][...] = acc[a][me] if pick_mine[a] else acc[a][...]
        dm2d[...] = jnp.zeros(dm2d.shape, F32)
        dm2d[0:1, :] = dm_ref[me]
        for cp in dm_copies:
            cp.wait()
        for k in range(1, N_DEV):
            dm2d[k:k + 1, :] = dm_recv[k]
        for cp in chip_copies:
            cp.wait()
        for a in range(nb):
            g_out[a][...] = ((s1[a][x, y] + chip_recv[a][0].astype(F32)) + chip_recv[a][1].astype(F32)) \
                + chip_recv[a][2].astype(F32)

    out_shapes = tuple(jax.ShapeDtypeStruct(s, F32) for s in big_shapes) + tuple(
        jax.ShapeDtypeStruct(p.shape[1:] if pk else p.shape, F32) for p, pk in zip(partials, pick_mine))
    outs = pl.pallas_call(
        body,
        name="tail_exchange",
        out_shape=out_shapes + (jax.ShapeDtypeStruct((2 * N_DEV, MOD_COLS), F32),),
        in_specs=[VMEM_SPEC] * (nb + n + 1),
        out_specs=(VMEM_SPEC,) * (nb + n + 1),
        scratch_shapes=[pltpu.VMEM((2, 2, *s), F32) for s in big_shapes]
        + [pltpu.VMEM((2, 2, *s), BF16) for s in big_shapes]
        + [pltpu.VMEM((3, *s), BF16) for s in big_shapes]
        + [pltpu.VMEM(p.shape, F32) for p in partials]
        + [pltpu.VMEM((3, *p.shape), F32) for p in partials]
        + [pltpu.VMEM((N_DEV, 1, MOD_COLS), F32), pltpu.SemaphoreType.DMA((3, n)), pltpu.SemaphoreType.DMA((3, n)),
           pltpu.SemaphoreType.DMA((N_DEV,)), pltpu.SemaphoreType.DMA((N_DEV,)),
           pltpu.SemaphoreType.DMA((nb, 4)), pltpu.SemaphoreType.DMA((nb, 4)),
           pltpu.SemaphoreType.DMA((nb, 3)), pltpu.SemaphoreType.DMA((nb, 3))],
        compiler_params=pltpu.CompilerParams(vmem_limit_bytes=VMEM_LIMIT_V7X),
    )(*big5, *partials, dmod3)
    return list(outs[:nb]), list(outs[nb:nb + n]), outs[nb + n]


def _small_update(grads, ws, ms, vs, scx, dm2d, w_ada, m_ada, v_ada, loss_lanes):
    n = len(grads)

    def body(*refs):
        g_in, w_in, m_in, v_in = (refs[q * n:(q + 1) * n] for q in range(4))
        scx_ref, dm_ref, wa_ref, ma_ref, va_ref, ll_ref = refs[4 * n:4 * n + 6]
        outs = refs[4 * n + 6:]
        g_out, d_out, nm_out, nv_out = (outs[q * (n + 1):(q + 1) * (n + 1)] for q in range(4))
        loss_ref = outs[4 * (n + 1)]
        for a in range(n + 1):
            if a < n:
                g, w, m, v = g_in[a][...], w_in[a][...], m_in[a][...], v_in[a][...]
            else:
                g = _mm_tn(_bf(scx_ref[...]), _bf(dm_ref[...]))
                w, m, v = wa_ref[...], ma_ref[...], va_ref[...]
            g_out[a][...] = g
            delta, m, v = _adamw(w, g, m, v)
            d_out[a][...] = delta
            nm_out[a][...] = m
            nv_out[a][...] = v
        loss_ref[...] = jnp.sum(ll_ref[...], axis=1, keepdims=True) * (0.5 / D)

    w_shapes = tuple(jax.ShapeDtypeStruct(w.shape, F32) for w in list(ws) + [w_ada])
    outs = pl.pallas_call(
        body,
        name="small_update",
        out_shape=w_shapes * 4 + (jax.ShapeDtypeStruct((1, 1), F32),),
        in_specs=[VMEM_SPEC] * (4 * n + 6),
        out_specs=(VMEM_SPEC,) * (4 * (n + 1) + 1),
        compiler_params=pltpu.CompilerParams(vmem_limit_bytes=VMEM_LIMIT_V7X),
    )(*grads, *ws, *ms, *vs, scx, dm2d, w_ada, m_ada, v_ada, loss_lanes)
    return [tuple(outs[q * (n + 1) + k] for q in range(4)) for k in range(n + 1)], outs[4 * (n + 1)]


def kernel(x, c, w_ada, b_ada, pre_mix_g, post_mix_g, w_in, sgu_norm_g, w_spatial, b_spatial, w_pool, pool_scale, w_out, pre_ffn_g, post_ffn_g, w_up, conv_w, conv_b, w_down, loss_target, m_w_ada, m_b_ada, m_pre_mix_g, m_post_mix_g, m_w_in, m_sgu_norm_g, m_w_spatial, m_b_spatial, m_w_pool, m_pool_scale, m_w_out, m_pre_ffn_g, m_post_ffn_g, m_w_up, m_conv_w, m_conv_b, m_w_down, v_w_ada, v_b_ada, v_pre_mix_g, v_post_mix_g, v_w_in, v_sgu_norm_g, v_w_spatial, v_b_spatial, v_w_pool, v_pool_scale, v_w_out, v_pre_ffn_g, v_post_ffn_g, v_w_up, v_conv_w, v_conv_b, v_w_down):
    t_len = x.shape[1]
    ts = min(256, t_len)
    ts_mix = min(512, t_len)
    ts_w = min(1024, t_len)
    coords = jnp.stack([lax.axis_index("x"), lax.axis_index("y"), lax.axis_index("c")]).astype(jnp.int32)

    w_in_t, w_up_t = w_in[0].T, w_up[0].T
    w_down_mine = w_down[0].astype(BF16)
    down_send, down_recv, down_src, down_land, token = _gather_start(w_down_mine)
    c_after = c * (1.0 + token[0:1, 0:1])
    mod3, scx, (g_in, g_out) = _prologue(c_after, w_ada[0], b_ada.reshape(N_DEV, 1, MOD_COLS), [w_in_t, w_out[0]],
                                         [BF16, BF16])
    mod = mod3.reshape(N_MOD, D)
    w_in_tb = g_in.reshape(IN_WIDTH, D)
    w_out_b = g_out.reshape(D, D)
    conv_b8 = conv_b.reshape(N_DEV, FF_CHUNK)
    b_sp_t = b_spatial[0].T

    x2d, tgt = x[0], loss_target[0]
    (x1, proj, mixed), (g_up, g_cw) = _mix_fwd(
        x2d, mod, pre_mix_g, post_mix_g, w_in_tb, sgu_norm_g[0], w_spatial[0], b_sp_t, w_pool[0], pool_scale, w_out_b,
        ts_mix, [w_up_t, conv_w[0]], [w_up.shape[1:], conv_w.shape[1:]], [BF16, F32])
    g_down = _gather_wait(down_send, down_recv, down_src, down_land, x1)
    my_index = 4 * lax.axis_index("x") + 2 * lax.axis_index("y") + lax.axis_index("c")
    w_down_b = lax.dynamic_update_index_in_dim(g_down, w_down_mine, my_index, 0).reshape(FF, D)
    up, f, dx2, loss_lanes = _ffn_fwd(x1, tgt, mod, pre_ffn_g, post_ffn_g, g_up, g_cw, conv_b8, w_down_b, ts)

    (dx1, dup, act, df, h2, dmod_f, d_pre_ffn, d_post_ffn, d_cb8, d_cw8) = _ffn_bwd(
        dx2, f, x1, up, mod, pre_ffn_g, post_ffn_g, g_up, g_cw, conv_b8, w_down_b, ts)
    gw_up = _wgrad_up(h2, dup, ts_w).reshape(2, 2, 2, FF_CHUNK, D)
    gw_down, r1_up = _wgrad_down(act, df, ts_w, gw_up)
    gw_down = gw_down.reshape(2, 2, 2, FF // N_DEV, D)
    s1_up, s1_up_b, r1_down = _sibling_add("up", [gw_up], [r1_up], coords, swap_srcs=[gw_down])
    s1_down, s1_down_b, _ = _sibling_add("down", [gw_down], r1_down, coords)
    s1_ffn, s1_ffn_b = s1_up + s1_down, s1_up_b + s1_down_b
    ((grad_x, gw_in, gw_out, dmod_m, d_pre_mix, d_post_mix, d_sgn, d_wsp, d_bsp_t, d_wpool, d_ps), r_ffn) = _mix_bwd(
        dx1, x2d, proj, mixed, mod, pre_mix_g, post_mix_g, w_in_tb, sgu_norm_g[0], w_spatial[0], b_sp_t,
        w_pool[0], pool_scale, w_out_b, ts_mix, s1_ffn_b)
    gw_in = gw_in.reshape(N_DEV, IN_WIDTH // N_DEV, D)
    gw_out = gw_out.reshape(N_DEV, D // N_DEV, D)

    big = _final_add_adamw(coords, s1_ffn, list(r_ffn), [w_up_t, w_down[0]], [m_w_up[0].T, m_w_down[0]],
                           [v_w_up[0].T, v_w_down[0]])
    r_up, r_down = tuple(a.T[None] for a in big[0]), tuple(a[None] for a in big[1])

    dmod = jnp.concatenate([dmod_m, dmod_f], axis=0)
    names = ["b_ada", "pre_mix_g", "post_mix_g", "sgu_norm_g", "w_spatial", "b_spatial", "w_pool", "pool_scale",
             "pre_ffn_g", "post_ffn_g", "conv_w", "conv_b"]
    partials = [dmod.reshape(1, N_MOD * D), d_pre_mix, d_post_mix, d_sgn, d_wsp, d_bsp_t.T, d_wpool, d_ps,
                d_pre_ffn, d_post_ffn, d_cw8, d_cb8.reshape(1, 2 * FF), loss_lanes]
    small_w = [b_ada, pre_mix_g, post_mix_g, sgu_norm_g[0], w_spatial[0], b_spatial[0], w_pool[0], pool_scale,
               pre_ffn_g, post_ffn_g, conv_w[0], conv_b]
    small_m = [m_b_ada, m_pre_mix_g, m_post_mix_g, m_sgu_norm_g[0], m_w_spatial[0], m_b_spatial[0], m_w_pool[0],
               m_pool_scale, m_pre_ffn_g, m_post_ffn_g, m_conv_w[0], m_conv_b]
    small_v = [v_b_ada, v_pre_mix_g, v_post_mix_g, v_sgu_norm_g[0], v_w_spatial[0], v_b_spatial[0], v_w_pool[0],
               v_pool_scale, v_pre_ffn_g, v_post_ffn_g, v_conv_w[0], v_conv_b]
    g_mix, sums, dm2d = _tail_exchange([gw_in, gw_out], partials, [nm == "conv_w" for nm in names] + [False],
                                       [int(nm == "w_pool") for nm in names] + [0],
                                       dmod.reshape(N_DEV, 1, MOD_COLS))
    small, loss11 = _small_update(
        sums[:-1] + g_mix, small_w + [w_in_t, w_out[0]], small_m + [m_w_in[0].T, m_w_out[0]],
        small_v + [v_w_in[0].T, v_w_out[0]], scx, dm2d, w_ada[0], m_w_ada[0], v_w_ada[0], sums[-1])
    loss = loss11.reshape(())
    lead = {"sgu_norm_g", "w_spatial", "b_spatial", "w_pool", "conv_w", "w_in", "w_out", "w_ada"}
    res = {nm: tuple((a.T if nm == "w_in" else a)[None] if nm in lead else a for a in four)
           for nm, four in zip(names + ["w_in", "w_out", "w_ada"], small)}
    res.update(w_up=r_up, w_down=r_down)

    order = ["w_ada", "b_ada", "pre_mix_g", "post_mix_g", "w_in", "sgu_norm_g", "w_spatial", "b_spatial", "w_pool",
             "pool_scale", "w_out", "pre_ffn_g", "post_ffn_g", "w_up", "conv_w", "conv_b", "w_down"]
    return (loss, grad_x[None], *[res[nm][0] for nm in order], *[res[nm][1] for nm in order],
            *[res[nm][2] for nm in order], *[res[nm][3] for nm in order])
```

```python
import functools
import math

import jax
import jax.numpy as jnp
from jax import lax
from jax.experimental import pallas as pl
from jax.experimental.pallas import tpu as pltpu

F32 = jnp.float32
BF16 = jnp.bfloat16
MESH = pl.DeviceIdType.MESH

EPS = 1e-6
D = 1024
HEAD = 128
N_HEAD = 4
A_WIDTH = 512
B_WIDTH = 512
IN_WIDTH = 1536
WINDOWS = (2, 4, 8, 16)
CHUNK = 64
FF = 2816
N_DEV = 8
FF_CHUNK = 704
N_MOD = 6
MOD_COLS = 768

ADAM_LR = 0.001
ADAM_B1 = 0.9
ADAM_B2 = 0.999
ADAM_EPS = 1e-08
ADAM_WD = 0.01
ADAM_STEP = 10

VMEM_LIMIT_V7X = 62 * 1024 * 1024
HALO = 8
POOL_HALO = 16

VMEM_SPEC = pl.BlockSpec(memory_space=pltpu.VMEM)
ANY_SPEC = pl.BlockSpec(memory_space=pl.ANY)


def _bf(x):
    return x.astype(BF16)


def _mm(a, b):
    return jnp.dot(a, b, preferred_element_type=F32)


def _mm_nt(a, b):
    return lax.dot_general(a, b, (((1,), (1,)), ((), ())), preferred_element_type=F32)


def _mm_tn(a, b):
    return lax.dot_general(a, b, (((0,), (0,)), ((), ())), preferred_element_type=F32)


def _rstd(x):
    return lax.rsqrt(jnp.mean(x * x, axis=-1, keepdims=True) + EPS)


def _sum0(x):
    return jnp.sum(x, axis=0, keepdims=True)


def _rowmean(x):
    return jnp.mean(x, axis=-1, keepdims=True)


_GELU_K = math.sqrt(2.0 / math.pi)


def _gelu_and_grad(x):
    x2 = x * x
    th = jnp.tanh(_GELU_K * (x + 0.044715 * (x * x2)))
    cdf = 0.5 * th + 0.5
    grad = cdf + x * (1.0 - th * th) * ((0.5 * _GELU_K) + (1.5 * 0.044715 * _GELU_K) * x2)
    return x * cdf, grad


def _gelu(x):
    return x * (0.5 * (1.0 + jnp.tanh(_GELU_K * (x + 0.044715 * (x * x * x)))))


def _sigmoid(x):
    return 0.5 * jnp.tanh(0.5 * x) + 0.5


def _sgu_mask():
    ri = lax.broadcasted_iota(jnp.int32, (HEAD, HEAD), 0)
    ci = lax.broadcasted_iota(jnp.int32, (HEAD, HEAD), 1)
    return (ci // CHUNK) <= (ri // CHUNK)


def _window_sum(ext, w, trailing):
    n = ext.shape[0]
    s, k = ext, 1
    while k < w:
        s = s + pltpu.roll(s, k if trailing else n - k, 0)
        k *= 2
    return s


def _inv_count(row0, n, w):
    t = row0 + lax.broadcasted_iota(jnp.int32, (n, 1), 0)
    return 1.0 / jnp.minimum(t + 1, w).astype(F32)


def _shift_down(v, before, k):
    rows = lax.broadcasted_iota(jnp.int32, before.shape, 0)
    r = pltpu.roll(v, k, 0)
    top = jnp.where(rows < k, pltpu.roll(before, k, 0), r[0:HALO])
    return jnp.concatenate([top, r[HALO:]], axis=0)


def _shift_up(v, after, k):
    n = v.shape[0]
    rows = lax.broadcasted_iota(jnp.int32, after.shape, 0)
    r = pltpu.roll(v, n - k, 0)
    bottom = jnp.where(rows >= HALO - k, pltpu.roll(after, HALO - k, 0), r[n - HALO:])
    return jnp.concatenate([r[:n - HALO], bottom], axis=0)


def _adamw(w, g, m, v):
    m = ADAM_B1 * m + (1.0 - ADAM_B1) * g
    v = ADAM_B2 * v + (1.0 - ADAM_B2) * (g * g)
    m_hat = m / (1.0 - ADAM_B1 ** ADAM_STEP)
    v_hat = v / (1.0 - ADAM_B2 ** ADAM_STEP)
    delta = -ADAM_LR * (m_hat / (jnp.sqrt(v_hat) + ADAM_EPS) + ADAM_WD * w)
    return delta, m, v


def _coords():
    return lax.axis_index("x"), lax.axis_index("y"), lax.axis_index("c")


def _peer(k):
    x, y, c = _coords()
    return (x ^ ((k >> 2) & 1), y ^ ((k >> 1) & 1), c ^ (k & 1))


def _my_index():
    x, y, c = _coords()
    return 4 * x + 2 * y + c


def _adaln_modulation(c_ref, w_ref, b_ref, mod_ref, scx_ref, scbuf, stage, recv, send_sems, recv_sems):
    me = _my_index()
    cv = c_ref[...]
    scbuf[0] = cv * _sigmoid(cv)
    first = [
        pltpu.make_async_remote_copy(scbuf.at[0], scbuf.at[k], send_sems.at[0, k], recv_sems.at[0, k],
                                     device_id=_peer(k), device_id_type=MESH)
        for k in range(1, N_DEV)
    ]
    for cp in first:
        cp.start()
    for cp in first:
        cp.wait()
    scx_ref[...] = jnp.zeros(scx_ref.shape, F32)
    for k in range(N_DEV):
        scx_ref[k:k + 1, :] = scbuf[k]
    prod = _mm(_bf(scx_ref[...]), _bf(w_ref[...]))
    for k in range(N_DEV):
        stage[k] = prod[k:k + 1, :] + b_ref[me]
    second = [
        pltpu.make_async_remote_copy(stage.at[k], recv.at[k], send_sems.at[1, k], recv_sems.at[1, k],
                                     device_id=_peer(k), device_id_type=MESH)
        for k in range(1, N_DEV)
    ]
    for cp in second:
        cp.start()
    mod_ref[me] = stage[0]
    for cp in second:
        cp.wait()
    for k in range(1, N_DEV):
        mod_ref[me ^ k] = recv[k]


class _GatherSteps:
    def __init__(self, ins, outs, stages, send_sems, recv_sems, local_sems):
        self.ins, self.outs, self.stages = ins, outs, stages
        self.send_sems, self.recv_sems, self.local_sems = send_sems, recv_sems, local_sems
        x, y, c = _coords()
        self.c = c
        self.me, self.sibling = (x, y, c), (x, y, 1 - c)
        self.chips = [(1 - x, y), (x, 1 - y), (1 - x, 1 - y)]

    def _copy(self, a, k, block, to, from_stage=False):
        dst = self.outs[a].at[4 * block[0] + 2 * block[1] + block[2]]
        return pltpu.make_async_remote_copy(self.stages[a] if from_stage else dst, dst, self.send_sems.at[a, k],
                                            self.recv_sems.at[a, k], device_id=to, device_id_type=MESH)

    def _local(self, a):
        me = self.me
        return pltpu.make_async_copy(self.stages[a], self.outs[a].at[4 * me[0] + 2 * me[1] + me[2]],
                                     self.local_sems.at[a])

    def _first(self, a):
        cps = [self._copy(a, 0, self.me, self.sibling, from_stage=True)]
        return cps + [self._copy(a, 1 + j, self.me, (*chip, self.c), from_stage=True)
                      for j, chip in enumerate(self.chips)]

    def _passed(self, a, j):
        return self._copy(a, 4 + j, (*self.chips[j], self.c), self.sibling)

    def start(self):
        for a in range(len(self.ins)):
            block = self.ins[a][...]
            if block.shape != self.stages[a].shape:
                block = block.T
            self.stages[a][...] = block.astype(self.stages[a].dtype)
            self._local(a).start()
            for cp in self._first(a):
                cp.start()

    def forward(self):
        for a in range(len(self.ins)):
            for j, chip in enumerate(self.chips):
                self._copy(a, 1 + j, (*chip, self.c), self.me).wait_recv()
                self._passed(a, j).start()

    def finish(self):
        for a in range(len(self.ins)):
            self._copy(a, 0, self.sibling, self.me).wait_recv()
            for j, chip in enumerate(self.chips):
                self._copy(a, 4 + j, (*chip, 1 - self.c), self.me).wait_recv()
            for cp in self._first(a) + [self._passed(a, j) for j in range(3)]:
                cp.wait_send()
            self._local(a).wait()


def _gather_scratch(shapes, out_dtypes):
    n = len(shapes)
    return ([pltpu.VMEM(s, dt) for s, dt in zip(shapes, out_dtypes)]
            + [pltpu.SemaphoreType.DMA((n, 7)), pltpu.SemaphoreType.DMA((n, 7)), pltpu.SemaphoreType.DMA((n,))])


def _gather_out_shapes(shapes, out_dtypes):
    return tuple(jax.ShapeDtypeStruct((N_DEV, *s), dt) for s, dt in zip(shapes, out_dtypes))


def _prologue(c_row, w_ada, b_ada3, shards, out_dtypes):
    n = len(shards)

    def body(*refs):
        c_ref, w_ref, b_ref = refs[:3]
        mod_ref, scx_ref = refs[3 + n:5 + n]
        gather = _GatherSteps(refs[3:3 + n], refs[5 + n:5 + 2 * n], refs[5 + 2 * n:5 + 3 * n],
                              *refs[5 + 3 * n:8 + 3 * n])
        gather.start()
        _adaln_modulation(c_ref, w_ref, b_ref, mod_ref, scx_ref, *refs[8 + 3 * n:])
        gather.forward()
        gather.finish()

    outs = pl.pallas_call(
        body,
        name="prologue",
        out_shape=(jax.ShapeDtypeStruct((N_DEV, 1, MOD_COLS), F32), jax.ShapeDtypeStruct((2 * N_DEV, D), F32))
        + _gather_out_shapes([s.shape for s in shards], out_dtypes),
        in_specs=[VMEM_SPEC] * (3 + n),
        out_specs=(VMEM_SPEC, VMEM_SPEC) + (ANY_SPEC,) * n,
        scratch_shapes=_gather_scratch([s.shape for s in shards], out_dtypes) + [
            pltpu.VMEM((N_DEV, 1, D), F32),
            pltpu.VMEM((N_DEV, 1, MOD_COLS), F32),
            pltpu.VMEM((N_DEV, 1, MOD_COLS), F32),
            pltpu.SemaphoreType.DMA((2, N_DEV)),
            pltpu.SemaphoreType.DMA((2, N_DEV)),
        ],
        compiler_params=pltpu.CompilerParams(vmem_limit_bytes=VMEM_LIMIT_V7X),
    )(c_row, w_ada, b_ada3, *shards)
    return outs[0], outs[1], outs[2:]


class _ChipExchangeSteps:
    FLIPS = ((1, 0), (0, 1), (1, 1))

    def __init__(self, srcs, dsts, send_sems, recv_sems):
        self.srcs, self.dsts, self.send_sems, self.recv_sems = srcs, dsts, send_sems, recv_sems

    def _copies(self):
        x, y, c = _coords()
        out = []
        for a in range(len(self.srcs)):
            for j, (fx, fy) in enumerate(self.FLIPS):
                k = 3 * a + j
                out.append(pltpu.make_async_remote_copy(
                    self.srcs[a].at[x ^ fx, y ^ fy], self.dsts[a].at[j], self.send_sems.at[k], self.recv_sems.at[k],
                    device_id=(x ^ fx, y ^ fy, c), device_id_type=MESH))
        return out

    def start(self):
        for cp in self._copies():
            cp.start()

    def finish(self):
        for cp in self._copies():
            cp.wait()


def _mix_fwd(x, mod, g_pre, g_post, w_in_t, sgn, w_sp, b_sp_t, w_pool, p_scale, w_out_b, ts, shards, shard_shapes,
             shard_dtypes):
    t_len = x.shape[0]
    nt, nb = t_len // ts, ts // HEAD
    ns = len(shards)

    def body(*refs):
        (x_ref, mod_ref, g1_ref, g2_ref, win_ref, sgn_ref, ws_ref, bst_ref, wp_ref, ps_ref, wout_ref) = refs[:11]
        x1_ref, proj_ref, mixed_ref = refs[11 + ns:14 + ns]
        pbuf, cat = refs[14 + 2 * ns:16 + 2 * ns]
        gather = _GatherSteps(refs[11:11 + ns], refs[14 + ns:14 + 2 * ns], refs[16 + 2 * ns:16 + 3 * ns],
                              *refs[16 + 3 * ns:])
        i = pl.program_id(0)

        @pl.when(i == 0)
        def _():
            pbuf[0:POOL_HALO, :] = jnp.zeros((POOL_HALO, B_WIDTH), F32)
            gather.start()

        @pl.when(i == (3 * nt) // 4)
        def _():
            gather.forward()

        xv = x_ref[...]
        sh, sc, gm = mod_ref[0:1, :], mod_ref[1:2, :], mod_ref[2:3, :]
        h = (xv * _rstd(xv) * g1_ref[...]) * (1.0 + sc) + sh
        proj_ref[...] = _mm_nt(_bf(h), win_ref[...])
        pbuf[POOL_HALO:POOL_HALO + ts, :] = proj_ref[:, 2 * A_WIDTH:]
        smask = _sgu_mask()
        for hd in range(N_HEAD):
            u = _gelu(proj_ref[:, hd * HEAD:(hd + 1) * HEAD])
            v = _gelu(proj_ref[:, A_WIDTH + hd * HEAD:A_WIDTH + (hd + 1) * HEAD])
            vn = _bf(v * _rstd(v) * sgn_ref[hd:hd + 1, :])
            wm = _bf(jnp.where(smask, ws_ref[hd], 0.0))
            bias = bst_ref[:, hd:hd + 1]
            for b in range(nb):
                rows = slice(b * HEAD, (b + 1) * HEAD)
                z = _mm(wm, vn[rows]) + bias
                cat[rows, hd * HEAD:(hd + 1) * HEAD] = _bf(u[rows] * z)
        for g, w in enumerate(WINDOWS):
            cols = slice(g * HEAD, (g + 1) * HEAD)
            ext = pbuf[:, cols]
            pooled = _window_sum(ext, w, True)[POOL_HALO:] * _inv_count(i * ts, ts, w) - ext[POOL_HALO:]
            cat[:, A_WIDTH + g * HEAD:A_WIDTH + (g + 1) * HEAD] = _bf(_mm(_bf(pooled), _bf(wp_ref[g])) * ps_ref[:, cols])
        pbuf[0:POOL_HALO, :] = pbuf[ts:ts + POOL_HALO, :]
        mixed = _mm(cat[...], wout_ref[...])
        mixed_ref[...] = mixed
        x1_ref[...] = xv + gm * (mixed * _rstd(mixed) * g2_ref[...])

        @pl.when(i == nt - 1)
        def _():
            gather.finish()

    tile = lambda wid: pl.BlockSpec((ts, wid), lambda i: (i, 0))
    outs = pl.pallas_call(
        body,
        name="mix_fwd",
        grid=(nt,),
        out_shape=(jax.ShapeDtypeStruct((t_len, D), F32), jax.ShapeDtypeStruct((t_len, IN_WIDTH), F32),
                   jax.ShapeDtypeStruct((t_len, D), F32)) + _gather_out_shapes(shard_shapes, shard_dtypes),
        in_specs=[tile(D)] + [VMEM_SPEC] * (10 + ns),
        out_specs=(tile(D), tile(IN_WIDTH), tile(D)) + (ANY_SPEC,) * ns,
        scratch_shapes=[pltpu.VMEM((POOL_HALO + ts, B_WIDTH), F32), pltpu.VMEM((ts, D), BF16)]
        + _gather_scratch(shard_shapes, shard_dtypes),
        compiler_params=pltpu.CompilerParams(dimension_semantics=("arbitrary",), vmem_limit_bytes=VMEM_LIMIT_V7X),
    )(x, mod, g_pre, g_post, w_in_t, sgn, w_sp, b_sp_t, w_pool, p_scale, w_out_b, *shards)
    return outs[:3], outs[3:]


def _ffn_fwd(x1, target, mod, g_pre, g_post, w_up_b, conv_w8, conv_b8, w_down_b, ts):
    t_len = x1.shape[0]
    nt = t_len // ts

    def body(x1_ref, tgt_ref, mod_ref, g3_ref, g4_ref, wup_ref, cw_ref, cb_ref, wdown_ref,
             up_ref, f_ref, dx2_ref, loss_ref, ucarry):
        i = pl.program_id(0)

        @pl.when(i == 0)
        def _():
            ucarry[...] = jnp.zeros(ucarry.shape, F32)
            loss_ref[...] = jnp.zeros(loss_ref.shape, F32)

        x1v = x1_ref[...]
        sh, sc, gf = mod_ref[3:4, :], mod_ref[4:5, :], mod_ref[5:6, :]
        h2 = _bf((x1v * _rstd(x1v) * g3_ref[...]) * (1.0 + sc) + sh)
        half = N_DEV // 2

        def up_pair(j):
            return [_mm(h2, wup_ref[jj]) for jj in (j, j + half)]

        f = jnp.zeros((ts, D), F32)
        ups = up_pair(0)
        for j in range(half):
            nxt = up_pair(j + 1) if j + 1 < half else None
            ys = []
            for up, jj in zip(ups, (j, j + half)):
                up_ref[jj] = up
                before = ucarry[jj]
                ucarry[jj] = up[ts - HALO:, :]
                cw = cw_ref[jj]
                ys.append(cb_ref[jj:jj + 1, :] + _shift_down(up, before, 2) * cw[0:1, :]
                          + _shift_down(up, before, 1) * cw[1:2, :] + up * cw[2:3, :])
            gate, val = ys
            act = gate * _sigmoid(gate) * val
            f = f + _mm(_bf(act), wdown_ref[j * FF_CHUNK:(j + 1) * FF_CHUNK, :])
            ups = nxt
        f_ref[...] = f
        x2 = x1v + gf * (f * _rstd(f) * g4_ref[...])
        err = x2 - tgt_ref[...]
        loss_ref[...] += _sum0(err * err)
        dx2_ref[...] = err * (1.0 / D)

    tile = pl.BlockSpec((ts, D), lambda i: (i, 0))
    return pl.pallas_call(
        body,
        name="ffn_fwd",
        grid=(nt,),
        out_shape=(jax.ShapeDtypeStruct((N_DEV, t_len, FF_CHUNK), F32), jax.ShapeDtypeStruct((t_len, D), F32),
                   jax.ShapeDtypeStruct((t_len, D), F32), jax.ShapeDtypeStruct((1, D), F32)),
        in_specs=[tile, tile] + [VMEM_SPEC] * 7,
        out_specs=(pl.BlockSpec((N_DEV, ts, FF_CHUNK), lambda i: (0, i, 0)), tile, tile,
                   pl.BlockSpec((1, D), lambda i: (0, 0))),
        scratch_shapes=[pltpu.VMEM((N_DEV, HALO, FF_CHUNK), F32)],
        compiler_params=pltpu.CompilerParams(dimension_semantics=("arbitrary",), vmem_limit_bytes=VMEM_LIMIT_V7X),
    )(x1, target, mod, g_pre, g_post, w_up_b, conv_w8, conv_b8, w_down_b)


def _ffn_bwd(dx2, f, x1, up, mod, g_pre, g_post, w_up_b, conv_w8, conv_b8, w_down_b, ts):
    t_len = x1.shape[0]
    nt = t_len // ts
    half = N_DEV // 2

    def body(dx2_ref, f_ref, x1_ref, up_ref, halo_ref, mod_ref, g3_ref, g4_ref, wup_ref, cw_ref, cb_ref, wdown_ref,
             dx1_ref, dup_ref, act_ref, df_ref, h2_ref, dmod_ref, dg3_ref, dg4_ref, dcb_ref, dcw_ref,
             dycarry, dh2acc):
        i = pl.program_id(0)
        r = nt - 1 - i

        @pl.when(i == 0)
        def _():
            for ref in (dmod_ref, dg3_ref, dg4_ref, dcb_ref, dcw_ref, dycarry):
                ref[...] = jnp.zeros(ref.shape, F32)

        dx2v, fv, x1v = dx2_ref[...], f_ref[...], x1_ref[...]
        sh, sc, gf = mod_ref[3:4, :], mod_ref[4:5, :], mod_ref[5:6, :]
        g3, g4 = g3_ref[...], g4_ref[...]
        rstd4 = _rstd(fv)
        fh = fv * rstd4
        dmod_ref[2:3, :] += _sum0(dx2v * (fh * g4))
        dr = dx2v * gf
        dg4_ref[...] += _sum0(dr * fh)
        dfh = dr * g4
        dfb = _bf(rstd4 * (dfh - fh * _rowmean(dfh * fh)))
        df_ref[...] = dfb
        rstd3 = _rstd(x1v)
        xh = x1v * rstd3
        n3 = xh * g3
        h2_ref[...] = _bf(n3 * (1.0 + sc) + sh)
        dh2acc[...] = jnp.zeros((ts, D), F32)
        keep = jnp.where(r > 0, 1.0, 0.0).astype(F32)

        def dact_of(j):
            return _mm_nt(dfb, wdown_ref[j * FF_CHUNK:(j + 1) * FF_CHUNK, :])

        dact_next = dact_of(0)
        for j in range(half):
            dact = dact_next
            if j + 1 < half:
                dact_next = dact_of(j + 1)
            ys = []
            for jj in (j, j + half):
                before = halo_ref[jj] * keep
                upc = up_ref[jj]
                cw = cw_ref[jj]
                ys.append(cb_ref[jj:jj + 1, :] + _shift_down(upc, before, 2) * cw[0:1, :]
                          + _shift_down(upc, before, 1) * cw[1:2, :] + upc * cw[2:3, :])
            gate, val = ys
            sg = _sigmoid(gate)
            silu = gate * sg
            act_ref[j] = _bf(silu * val)
            dys = (dact * val * (sg + silu * (1.0 - sg)), dact * silu)
            for q, jj in enumerate((j, j + half)):
                dy = dys[q]
                cw = cw_ref[jj]
                dcb_ref[jj:jj + 1, :] += _sum0(dy)
                after = dycarry[jj]
                dycarry[jj] = dy[0:HALO, :]
                dy1, dy2 = _shift_up(dy, after, 1), _shift_up(dy, after, 2)
                upc = up_ref[jj]
                dcw_ref[jj, 0:1, :] += _sum0(dy2 * upc)
                dcw_ref[jj, 1:2, :] += _sum0(dy1 * upc)
                dcw_ref[jj, 2:3, :] += _sum0(dy * upc)
                dup = _bf(dy * cw[2:3, :] + dy1 * cw[1:2, :] + dy2 * cw[0:1, :])
                dup_ref[jj] = dup
                dh2acc[...] += _mm_nt(dup, wup_ref[jj])
        dh2 = dh2acc[...]
        dmod_ref[0:1, :] += _sum0(dh2)
        dmod_ref[1:2, :] += _sum0(dh2 * n3)
        dn3 = dh2 * (1.0 + sc)
        dg3_ref[...] += _sum0(dn3 * xh)
        dxh = dn3 * g3
        dx1_ref[...] = dx2v + rstd3 * (dxh - xh * _rowmean(dxh * xh))

    tile = pl.BlockSpec((ts, D), lambda i: (nt - 1 - i, 0))
    chunked = lambda n: pl.BlockSpec((n, ts, FF_CHUNK), lambda i: (0, nt - 1 - i, 0))
    halo = pl.BlockSpec((N_DEV, HALO, FF_CHUNK), lambda i: (0, jnp.maximum((nt - 1 - i) * (ts // HALO) - 1, 0), 0))
    const = lambda *shape: pl.BlockSpec(shape, lambda i: (0,) * len(shape))
    return pl.pallas_call(
        body,
        name="ffn_bwd",
        grid=(nt,),
        out_shape=(jax.ShapeDtypeStruct((t_len, D), F32), jax.ShapeDtypeStruct((N_DEV, t_len, FF_CHUNK), BF16),
                   jax.ShapeDtypeStruct((half, t_len, FF_CHUNK), BF16), jax.ShapeDtypeStruct((t_len, D), BF16),
                   jax.ShapeDtypeStruct((t_len, D), BF16), jax.ShapeDtypeStruct((3, D), F32),
                   jax.ShapeDtypeStruct((1, D), F32), jax.ShapeDtypeStruct((1, D), F32),
                   jax.ShapeDtypeStruct((N_DEV, FF_CHUNK), F32), jax.ShapeDtypeStruct((N_DEV, 3, FF_CHUNK), F32)),
        in_specs=[tile, tile, tile, chunked(N_DEV), halo] + [VMEM_SPEC] * 7,
        out_specs=(tile, chunked(N_DEV), chunked(half), tile, tile, const(3, D), const(1, D), const(1, D),
                   const(N_DEV, FF_CHUNK), const(N_DEV, 3, FF_CHUNK)),
        scratch_shapes=[pltpu.VMEM((N_DEV, HALO, FF_CHUNK), F32), pltpu.VMEM((ts, D), F32)],
        compiler_params=pltpu.CompilerParams(dimension_semantics=("arbitrary",), vmem_limit_bytes=VMEM_LIMIT_V7X),
    )(dx2, f, x1, up, up, mod, g_pre, g_post, w_up_b, conv_w8, conv_b8, w_down_b)


def _wgrad_up(h2, dup, ts):
    t_len = h2.shape[0]
    nt, half = t_len // ts, N_DEV // 2

    def body(h2_ref, dup_ref, out_ref):
        @pl.when(pl.program_id(1) == 0)
        def _():
            out_ref[...] = jnp.zeros(out_ref.shape, F32)

        for q in range(half):
            out_ref[q] += _mm_tn(dup_ref[q], h2_ref[...])

    return pl.pallas_call(
        body,
        name="wgrad_up",
        grid=(2, nt),
        out_shape=jax.ShapeDtypeStruct((N_DEV, FF_CHUNK, D), F32),
        in_specs=[pl.BlockSpec((ts, D), lambda g, t: (t, 0)), pl.BlockSpec((half, ts, FF_CHUNK), lambda g, t: (g, t, 0))],
        out_specs=pl.BlockSpec((half, FF_CHUNK, D), lambda g, t: (g, 0, 0)),
        compiler_params=pltpu.CompilerParams(dimension_semantics=("arbitrary", "arbitrary"),
                                             vmem_limit_bytes=VMEM_LIMIT_V7X),
    )(h2, dup)


def _sibling_swap_copies(srcs, dsts, send_sems, recv_sems):
    x, y, c = _coords()
    return [
        pltpu.make_async_remote_copy(srcs[a].at[xs, ys, 1 - c], dsts[a].at[xs, ys], send_sems.at[a, 2 * xs + ys],
                                     recv_sems.at[a, 2 * xs + ys], device_id=(x, y, 1 - c), device_id_type=MESH)
        for a in range(len(srcs)) for xs in range(2) for ys in range(2)
    ]


def _wgrad_down(act, df, ts, swap_src):
    t_len = df.shape[0]
    nt, half = t_len // ts, N_DEV // 2

    def body(act_ref, df_ref, src_ref, out_ref, dst_ref, send_sems, recv_sems):
        t = pl.program_id(0)

        @pl.when(t == 0)
        def _():
            for cp in _sibling_swap_copies([src_ref], [dst_ref], send_sems, recv_sems):
                cp.start()
            out_ref[...] = jnp.zeros(out_ref.shape, F32)

        for q in range(half):
            out_ref[q] += _mm_tn(act_ref[q], df_ref[...])

        @pl.when(t == nt - 1)
        def _():
            for cp in _sibling_swap_copies([src_ref], [dst_ref], send_sems, recv_sems):
                cp.wait()

    return pl.pallas_call(
        body,
        name="wgrad_down",
        grid=(nt,),
        out_shape=(jax.ShapeDtypeStruct((half, FF_CHUNK, D), F32), jax.ShapeDtypeStruct(swap_src.shape[1:], F32)),
        in_specs=[pl.BlockSpec((half, ts, FF_CHUNK), lambda t: (0, t, 0)), pl.BlockSpec((ts, D), lambda t: (t, 0)),
                  ANY_SPEC],
        out_specs=(pl.BlockSpec((half, FF_CHUNK, D), lambda t: (0, 0, 0)), ANY_SPEC),
        scratch_shapes=[pltpu.SemaphoreType.DMA((1, 4)), pltpu.SemaphoreType.DMA((1, 4))],
        compiler_params=pltpu.CompilerParams(dimension_semantics=("arbitrary",), vmem_limit_bytes=VMEM_LIMIT_V7X),
    )(act, df, swap_src)


def _mix_bwd(dx1, x, proj, mixed, mod, g_pre, g_post, w_in_t, sgn, w_sp, b_sp_t, w_pool, p_scale, w_out_b, ts, rs_srcs):
    t_len = x.shape[0]
    nt, nb = t_len // ts, ts // HEAD
    nr = len(rs_srcs)

    def body(*refs):
        (dx1_ref, x_ref, proj_ref, halo_ref, mixed_ref, mod_ref, g1_ref, g2_ref, win_ref, sgn_ref, ws_ref,
         bst_ref, wp_ref, ps_ref, wout_ref) = refs[:15]
        (gx_ref, dwin_ref, dwout_ref, dmod_ref, dg1_ref, dg2_ref, dsgn_ref, dws_ref, dbst_ref, dwp_ref,
         dps_ref) = refs[15 + nr:26 + nr]
        pbuf, dwsbuf, cat, dproj, dcat = refs[26 + 2 * nr:31 + 2 * nr]
        exchange = _ChipExchangeSteps(refs[15:15 + nr], refs[26 + nr:26 + 2 * nr], *refs[31 + 2 * nr:])
        i = pl.program_id(0)
        r = nt - 1 - i

        @pl.when(i == 0)
        def _():
            exchange.start()
            for ref in (dwin_ref, dwout_ref, dmod_ref, dg1_ref, dg2_ref, dsgn_ref, dws_ref, dbst_ref, dwp_ref, dps_ref):
                ref[...] = jnp.zeros(ref.shape, F32)
            dwsbuf[ts:ts + POOL_HALO, :] = jnp.zeros((POOL_HALO, B_WIDTH), F32)

        xv, dx1v, mixed = x_ref[...], dx1_ref[...], mixed_ref[...]
        sh, sc, gm = mod_ref[0:1, :], mod_ref[1:2, :], mod_ref[2:3, :]
        g1, g2 = g1_ref[...], g2_ref[...]
        rstd2 = _rstd(mixed)
        mh = mixed * rstd2
        dmod_ref[2:3, :] += _sum0(dx1v * (mh * g2))
        dr = dx1v * gm
        dg2_ref[...] += _sum0(dr * mh)
        dmh = dr * g2
        dmb = _bf(rstd2 * (dmh - mh * _rowmean(dmh * mh)))
        dcat[...] = _mm_nt(dmb, wout_ref[...])
        smask = _sgu_mask()
        for hd in range(N_HEAD):
            ucols = slice(hd * HEAD, (hd + 1) * HEAD)
            vcols = slice(A_WIDTH + hd * HEAD, A_WIDTH + (hd + 1) * HEAD)
            u, du_dp = _gelu_and_grad(proj_ref[:, ucols])
            v, dv_dp = _gelu_and_grad(proj_ref[:, vcols])
            rs = _rstd(v)
            vhat = v * rs
            gn = sgn_ref[hd:hd + 1, :]
            vn = _bf(vhat * gn)
            wm = _bf(jnp.where(smask, ws_ref[hd], 0.0))
            bias = bst_ref[:, hd:hd + 1]
            dzsum = jnp.zeros((HEAD, HEAD), F32)
            dwm = jnp.zeros((HEAD, HEAD), F32)
            dvn_parts = []
            for b in range(nb):
                rows = slice(b * HEAD, (b + 1) * HEAD)
                z = _mm(wm, vn[rows]) + bias
                da = dcat[rows, ucols]
                cat[rows, ucols] = _bf(u[rows] * z)
                dz = da * u[rows]
                dzsum = dzsum + dz
                dzb = _bf(dz)
                dwm = dwm + _mm_nt(dzb, vn[rows])
                dvn_parts.append(_mm_tn(wm, dzb))
                dproj[rows, ucols] = _bf((da * z) * du_dp[rows])
            dvn = jnp.concatenate(dvn_parts, axis=0)
            dsgn_ref[hd:hd + 1, :] += _sum0(dvn * vhat)
            dvh = dvn * gn
            dproj[:, vcols] = _bf((rs * (dvh - vhat * _rowmean(dvh * vhat))) * dv_dp)
            dws_ref[hd] += jnp.where(smask, dwm, 0.0)
            dbst_ref[:, hd:hd + 1] += jnp.sum(dzsum, axis=1, keepdims=True)
        keep = jnp.where(r > 0, 1.0, 0.0).astype(F32)
        pbuf[0:POOL_HALO, :] = halo_ref[...] * keep
        pbuf[POOL_HALO:POOL_HALO + ts, :] = proj_ref[:, 2 * A_WIDTH:]
        for g, w in enumerate(WINDOWS):
            cols = slice(g * HEAD, (g + 1) * HEAD)
            ccols = slice(A_WIDTH + g * HEAD, A_WIDTH + (g + 1) * HEAD)
            pcols = slice(2 * A_WIDTH + g * HEAD, 2 * A_WIDTH + (g + 1) * HEAD)
            wpg = _bf(wp_ref[g])
            psg = ps_ref[:, cols]
            ext = pbuf[:, cols]
            inv = _inv_count(r * ts, ts, w)
            pb = _bf(_window_sum(ext, w, True)[POOL_HALO:] * inv - ext[POOL_HALO:])
            yb = _mm(pb, wpg)
            dob = dcat[:, ccols]
            cat[:, ccols] = _bf(yb * psg)
            dps_ref[:, cols] += _sum0(dob * yb)
            dyb = _bf(dob * psg)
            dwp_ref[g] += _mm_tn(pb, dyb)
            dpooled = _mm_nt(dyb, wpg)
            dwsbuf[0:ts, cols] = dpooled * inv
            dproj[:, pcols] = _bf(_window_sum(dwsbuf[:, cols], w, False)[0:ts] - dpooled)
        dwsbuf[ts:ts + POOL_HALO, :] = dwsbuf[0:POOL_HALO, :]
        dpb = dproj[...]
        rstd1 = _rstd(xv)
        xh = xv * rstd1
        n1 = xh * g1
        dwin_ref[...] += _mm_tn(dpb, _bf(n1 * (1.0 + sc) + sh))
        dwout_ref[...] += _mm_tn(cat[...], dmb)
        dh = _mm(dpb, win_ref[...])
        dmod_ref[0:1, :] += _sum0(dh)
        dmod_ref[1:2, :] += _sum0(dh * n1)
        dn1 = dh * (1.0 + sc)
        dg1_ref[...] += _sum0(dn1 * xh)
        dxh = dn1 * g1
        gx_ref[...] = dx1v + rstd1 * (dxh - xh * _rowmean(dxh * xh))

        @pl.when(i == nt - 1)
        def _():
            exchange.finish()

    tile = lambda wid: pl.BlockSpec((ts, wid), lambda i: (nt - 1 - i, 0))
    halo = pl.BlockSpec((POOL_HALO, B_WIDTH),
                        lambda i: (jnp.maximum((nt - 1 - i) * (ts // POOL_HALO) - 1, 0), 2 * A_WIDTH // B_WIDTH))
    const = lambda *shape: pl.BlockSpec(shape, lambda i: (0,) * len(shape))
    resident = lambda *shape: pl.BlockSpec(shape, lambda i: (0,) * len(shape), pipeline_mode=pl.Buffered(1))
    outs = pl.pallas_call(
        body,
        name="mix_bwd",
        grid=(nt,),
        out_shape=(jax.ShapeDtypeStruct((t_len, D), F32), jax.ShapeDtypeStruct((IN_WIDTH, D), F32),
                   jax.ShapeDtypeStruct((D, D), F32), jax.ShapeDtypeStruct((3, D), F32),
                   jax.ShapeDtypeStruct((1, D), F32), jax.ShapeDtypeStruct((1, D), F32),
                   jax.ShapeDtypeStruct((N_HEAD, HEAD), F32), jax.ShapeDtypeStruct((N_HEAD, HEAD, HEAD), F32),
                   jax.ShapeDtypeStruct((HEAD, N_HEAD), F32), jax.ShapeDtypeStruct((N_HEAD, HEAD, HEAD), F32),
                   jax.ShapeDtypeStruct((1, B_WIDTH), F32))
        + tuple(jax.ShapeDtypeStruct((3, *s.shape[2:]), s.dtype) for s in rs_srcs),
        in_specs=[tile(D), tile(D), tile(IN_WIDTH), halo, tile(D)] + [VMEM_SPEC] * 10 + [ANY_SPEC] * nr,
        out_specs=(tile(D), resident(IN_WIDTH, D), resident(D, D), const(3, D), const(1, D), const(1, D),
                   const(N_HEAD, HEAD), const(N_HEAD, HEAD, HEAD), const(HEAD, N_HEAD), const(N_HEAD, HEAD, HEAD),
                   const(1, B_WIDTH)) + (ANY_SPEC,) * nr,
        scratch_shapes=[pltpu.VMEM((POOL_HALO + ts, B_WIDTH), F32), pltpu.VMEM((ts + POOL_HALO, B_WIDTH), F32),
                        pltpu.VMEM((ts, D), BF16), pltpu.VMEM((ts, IN_WIDTH), BF16), pltpu.VMEM((ts, D), F32),
                        pltpu.SemaphoreType.DMA((3 * nr,)), pltpu.SemaphoreType.DMA((3 * nr,))],
        compiler_params=pltpu.CompilerParams(dimension_semantics=("arbitrary",), vmem_limit_bytes=VMEM_LIMIT_V7X),
    )(dx1, x, proj, proj, mixed, mod, g_pre, g_post, w_in_t, sgn, w_sp, b_sp_t, w_pool, p_scale, w_out_b, *rs_srcs)
    return outs[:11], outs[11:]


def _pair_add(name, coords, grid, specs_a, specs_b, out_specs, out_shapes, a_arrays, b_arrays, swap_srcs):
    n, ns = len(a_arrays), len(swap_srcs)
    last = tuple(g - 1 for g in grid)

    def body(co_ref, *refs):
        ids = [pl.program_id(d) for d in range(len(grid))]
        swap = refs[2 * n:2 * n + ns], refs[4 * n + ns:4 * n + 2 * ns], *refs[4 * n + 2 * ns:]
        if ns:
            @pl.when(functools.reduce(jnp.logical_and, [i == 0 for i in ids]))
            def _():
                for cp in _sibling_swap_copies(*swap):
                    cp.start()

        for k in range(n):
            total = refs[k][...] + refs[n + k][...]
            refs[2 * n + ns + k][...] = total
            refs[3 * n + ns + k][...] = _bf(total)

        if ns:
            @pl.when(functools.reduce(jnp.logical_and, [i == e for i, e in zip(ids, last)]))
            def _():
                for cp in _sibling_swap_copies(*swap):
                    cp.wait()

    outs = pl.pallas_call(
        body,
        name=name,
        grid_spec=pltpu.PrefetchScalarGridSpec(
            num_scalar_prefetch=1, grid=grid, in_specs=specs_a + specs_b + [ANY_SPEC] * ns,
            out_specs=out_specs * 2 + [ANY_SPEC] * ns,
            scratch_shapes=[pltpu.SemaphoreType.DMA((ns, 4)), pltpu.SemaphoreType.DMA((ns, 4))] if ns else []),
        out_shape=tuple(jax.ShapeDtypeStruct(s, dt) for dt in (F32, BF16) for s in out_shapes)
        + tuple(jax.ShapeDtypeStruct(g.shape[1:], F32) for g in swap_srcs),
        compiler_params=pltpu.CompilerParams(dimension_semantics=("arbitrary",) * len(grid),
                                             vmem_limit_bytes=VMEM_LIMIT_V7X),
    )(coords, *a_arrays, *b_arrays, *swap_srcs)
    return list(outs[:n]), list(outs[n:2 * n]), list(outs[2 * n:])


def _final_add_adamw(coords, s1, r, ws, ms, vs, n_split=4):
    n = len(s1)

    def body(co_ref, *refs):
        for k in range(n):
            s_ref, r_ref, w_ref, m_ref, v_ref = (refs[q * n + k] for q in range(5))
            g_ref, d_ref, nm_ref, nv_ref = (refs[(5 + q) * n + k] for q in range(4))
            g = ((s_ref[...] + r_ref[0].astype(F32)) + r_ref[1].astype(F32)) + r_ref[2].astype(F32)
            g_ref[...] = g
            delta, m, v = _adamw(w_ref[...], g, m_ref[...], v_ref[...])
            d_ref[...] = delta
            nm_ref[...] = m
            nv_ref[...] = v

    def shard_spec(a):
        rows, cols = a.shape
        return pl.BlockSpec((rows // n_split, cols), lambda i, co: (i, 0))

    def mine_spec(a):
        rows, cols = a.shape[2:]
        return pl.BlockSpec((None, None, rows // n_split, cols), lambda i, co: (co[0], co[1], i, 0))

    def recv_spec(a):
        rows, cols = a.shape[1:]
        return pl.BlockSpec((3, rows // n_split, cols), lambda i, co: (0, i, 0))

    in_specs = ([mine_spec(a) for a in s1] + [recv_spec(a) for a in r] + [shard_spec(a) for a in ws] * 3)
    out_specs = [shard_spec(a) for a in ws] * 4
    outs = pl.pallas_call(
        body,
        name="grad_final_adamw",
        grid_spec=pltpu.PrefetchScalarGridSpec(num_scalar_prefetch=1, grid=(n_split,), in_specs=in_specs,
                                               out_specs=out_specs),
        out_shape=tuple(jax.ShapeDtypeStruct(a.shape, F32) for a in ws) * 4,
        compiler_params=pltpu.CompilerParams(dimension_semantics=("arbitrary",), vmem_limit_bytes=VMEM_LIMIT_V7X),
    )(coords, *s1, *r, *ws, *ms, *vs)
    return [tuple(outs[q * n + k] for q in range(4)) for k in range(n)]


def _sibling_add(tag, g5, r1, coords, swap_srcs=(), n_split=4):
    shapes = [g.shape[3:] for g in g5]
    spec_g = [pl.BlockSpec((None, None, None, s[0] // n_split, s[1]), lambda i, j, k, co: (i, j, co[2], k, 0))
              for s in shapes]
    spec_r = [pl.BlockSpec((None, None, s[0] // n_split, s[1]), lambda i, j, k, co: (i, j, k, 0)) for s in shapes]
    return _pair_add("grad_add_core_" + tag, coords, (2, 2, n_split), spec_g, spec_r, spec_r,
                     [(2, 2, *s) for s in shapes], g5, r1, list(swap_srcs))


def _tail_exchange(big, partials, pick_mine, y_first, dmod3):
    n, nb = len(partials), len(big)
    big_shapes = [g.shape[1:] for g in big]
    big5 = [g.reshape(2, 2, 2, *s) for g, s in zip(big, big_shapes)]
    flips = _ChipExchangeSteps.FLIPS

    def body(*refs):
        g5, p_in, dm_ref = refs[:nb], refs[nb:nb + n], refs[nb + n]
        outs = refs[nb + n + 1:2 * nb + 2 * n + 2]
        g_out, sums, dm2d = outs[:nb], outs[nb:nb + n], outs[nb + n]
        scratch = refs[2 * nb + 2 * n + 2:]
        s1, stage, chip_recv = scratch[:nb], scratch[nb:2 * nb], scratch[2 * nb:3 * nb]
        acc, rbuf = scratch[3 * nb:3 * nb + n], scratch[3 * nb + n:3 * nb + 2 * n]
        (dm_recv, send_sems, recv_sems, dm_send_sems, dm_recv_sems, sib_send, sib_recv, chip_send,
         chip_recv_sems) = scratch[3 * nb + 2 * n:]
        x, y, c = _coords()
        me = 4 * x + 2 * y + c
        sibling = (x, y, 1 - c)
        dm_copies = [
            pltpu.make_async_remote_copy(dm_ref.at[me ^ k], dm_recv.at[k], dm_send_sems.at[k], dm_recv_sems.at[k],
                                         device_id=_peer(k), device_id_type=MESH)
            for k in range(1, N_DEV)
        ]
        for cp in dm_copies:
            cp.start()
        sib_copies = _sibling_swap_copies(g5, s1, sib_send, sib_recv)
        for cp in sib_copies:
            cp.start()
        for a in range(n):
            acc[a][...] = p_in[a][...]

        def small_phase(ph, peers):
            copies = [
                pltpu.make_async_remote_copy(acc[a], rbuf[a].at[ph], send_sems.at[ph, a], recv_sems.at[ph, a],
                                             device_id=peers[y_first[a]], device_id_type=MESH)
                for a in range(n)
            ]
            for cp in copies:
                cp.start()
            for cp in copies:
                cp.wait()
            for a in range(n):
                acc[a][...] = acc[a][...] + rbuf[a][ph]

        small_phase(0, (sibling, sibling))
        for cp in sib_copies:
            cp.wait()
        for a in range(nb):
            for xs in range(2):
                for ys in range(2):
                    total = g5[a][xs, ys, c] + s1[a][xs, ys]
                    s1[a][xs, ys] = total
                    stage[a][xs, ys] = _bf(total)
        chip_copies = [
            pltpu.make_async_remote_copy(stage[a].at[x ^ fx, y ^ fy], chip_recv[a].at[j], chip_send.at[a, j],
                                         chip_recv_sems.at[a, j], device_id=(x ^ fx, y ^ fy, c), device_id_type=MESH)
            for a in range(nb) for j, (fx, fy) in enumerate(flips)
        ]
        for cp in chip_copies:
            cp.start()
        x_peer, y_peer = (1 - x, y, c), (x, 1 - y, c)
        small_phase(1, (x_peer, y_peer))
        small_phase(2, (y_peer, x_peer))
        for a in range(n):
            sums[a][...] = acc[a][me] if pick_mine[a] else acc[a][...]
        dm2d[...] = jnp.zeros(dm2d.shape, F32)
        dm2d[0:1, :] = dm_ref[me]
        for cp in dm_copies:
            cp.wait()
        for k in range(1, N_DEV):
            dm2d[k:k + 1, :] = dm_recv[k]
        for cp in chip_copies:
            cp.wait()
        for a in range(nb):
            g_out[a][...] = ((s1[a][x, y] + chip_recv[a][0].astype(F32)) + chip_recv[a][1].astype(F32)) \
                + chip_recv[a][2].astype(F32)

    out_shapes = tuple(jax.ShapeDtypeStruct(s, F32) for s in big_shapes) + tuple(
        jax.ShapeDtypeStruct(p.shape[1:] if pk else p.shape, F32) for p, pk in zip(partials, pick_mine))
    outs = pl.pallas_call(
        body,
        name="tail_exchange",
        out_shape=out_shapes + (jax.ShapeDtypeStruct((2 * N_DEV, MOD_COLS), F32),),
        in_specs=[VMEM_SPEC] * (nb + n + 1),
        out_specs=(VMEM_SPEC,) * (nb + n + 1),
        scratch_shapes=[pltpu.VMEM((2, 2, *s), F32) for s in big_shapes]
        + [pltpu.VMEM((2, 2, *s), BF16) for s in big_shapes]
        + [pltpu.VMEM((3, *s), BF16) for s in big_shapes]
        + [pltpu.VMEM(p.shape, F32) for p in partials]
        + [pltpu.VMEM((3, *p.shape), F32) for p in partials]
        + [pltpu.VMEM((N_DEV, 1, MOD_COLS), F32), pltpu.SemaphoreType.DMA((3, n)), pltpu.SemaphoreType.DMA((3, n)),
           pltpu.SemaphoreType.DMA((N_DEV,)), pltpu.SemaphoreType.DMA((N_DEV,)),
           pltpu.SemaphoreType.DMA((nb, 4)), pltpu.SemaphoreType.DMA((nb, 4)),
           pltpu.SemaphoreType.DMA((nb, 3)), pltpu.SemaphoreType.DMA((nb, 3))],
        compiler_params=pltpu.CompilerParams(vmem_limit_bytes=VMEM_LIMIT_V7X),
    )(*big5, *partials, dmod3)
    return list(outs[:nb]), list(outs[nb:nb + n]), outs[nb + n]


def _small_update(grads, ws, ms, vs, scx, dm2d, w_ada, m_ada, v_ada, loss_lanes):
    n = len(grads)

    def body(*refs):
        g_in, w_in, m_in, v_in = (refs[q * n:(q + 1) * n] for q in range(4))
        scx_ref, dm_ref, wa_ref, ma_ref, va_ref, ll_ref = refs[4 * n:4 * n + 6]
        outs = refs[4 * n + 6:]
        g_out, d_out, nm_out, nv_out = (outs[q * (n + 1):(q + 1) * (n + 1)] for q in range(4))
        loss_ref = outs[4 * (n + 1)]
        for a in range(n + 1):
            if a < n:
                g, w, m, v = g_in[a][...], w_in[a][...], m_in[a][...], v_in[a][...]
            else:
                g = _mm_tn(_bf(scx_ref[...]), _bf(dm_ref[...]))
                w, m, v = wa_ref[...], ma_ref[...], va_ref[...]
            g_out[a][...] = g
            delta, m, v = _adamw(w, g, m, v)
            d_out[a][...] = delta
            nm_out[a][...] = m
            nv_out[a][...] = v
        loss_ref[...] = jnp.sum(ll_ref[...], axis=1, keepdims=True) * (0.5 / D)

    w_shapes = tuple(jax.ShapeDtypeStruct(w.shape, F32) for w in list(ws) + [w_ada])
    outs = pl.pallas_call(
        body,
        name="small_update",
        out_shape=w_shapes * 4 + (jax.ShapeDtypeStruct((1, 1), F32),),
        in_specs=[VMEM_SPEC] * (4 * n + 6),
        out_specs=(VMEM_SPEC,) * (4 * (n + 1) + 1),
        compiler_params=pltpu.CompilerParams(vmem_limit_bytes=VMEM_LIMIT_V7X),
    )(*grads, *ws, *ms, *vs, scx, dm2d, w_ada, m_ada, v_ada, loss_lanes)
    return [tuple(outs[q * (n + 1) + k] for q in range(4)) for k in range(n + 1)], outs[4 * (n + 1)]


def kernel(x, c, w_ada, b_ada, pre_mix_g, post_mix_g, w_in, sgu_norm_g, w_spatial, b_spatial, w_pool, pool_scale, w_out, pre_ffn_g, post_ffn_g, w_up, conv_w, conv_b, w_down, loss_target, m_w_ada, m_b_ada, m_pre_mix_g, m_post_mix_g, m_w_in, m_sgu_norm_g, m_w_spatial, m_b_spatial, m_w_pool, m_pool_scale, m_w_out, m_pre_ffn_g, m_post_ffn_g, m_w_up, m_conv_w, m_conv_b, m_w_down, v_w_ada, v_b_ada, v_pre_mix_g, v_post_mix_g, v_w_in, v_sgu_norm_g, v_w_spatial, v_b_spatial, v_w_pool, v_pool_scale, v_w_out, v_pre_ffn_g, v_post_ffn_g, v_w_up, v_conv_w, v_conv_b, v_w_down):
    t_len = x.shape[1]
    ts = min(256, t_len)
    ts_mix = min(512, t_len)
    ts_w = min(1024, t_len)
    coords = jnp.stack([lax.axis_index("x"), lax.axis_index("y"), lax.axis_index("c")]).astype(jnp.int32)

    w_in_t, w_up_t = w_in[0].T, w_up[0].T
    mod3, scx, (g_in, g_out, g_down) = _prologue(c, w_ada[0], b_ada.reshape(N_DEV, 1, MOD_COLS),
                                                 [w_in_t, w_out[0], w_down[0]], [BF16, BF16, BF16])
    mod = mod3.reshape(N_MOD, D)
    w_in_tb = g_in.reshape(IN_WIDTH, D)
    w_out_b = g_out.reshape(D, D)
    conv_b8 = conv_b.reshape(N_DEV, FF_CHUNK)
    b_sp_t = b_spatial[0].T

    x2d, tgt = x[0], loss_target[0]
    (x1, proj, mixed), (g_up, g_cw) = _mix_fwd(
        x2d, mod, pre_mix_g, post_mix_g, w_in_tb, sgu_norm_g[0], w_spatial[0], b_sp_t, w_pool[0], pool_scale, w_out_b,
        ts_mix, [w_up_t, conv_w[0]], [w_up.shape[1:], conv_w.shape[1:]], [BF16, F32])
    w_down_b = g_down.reshape(FF, D)
    up, f, dx2, loss_lanes = _ffn_fwd(x1, tgt, mod, pre_ffn_g, post_ffn_g, g_up, g_cw, conv_b8, w_down_b, ts)

    (dx1, dup, act, df, h2, dmod_f, d_pre_ffn, d_post_ffn, d_cb8, d_cw8) = _ffn_bwd(
        dx2, f, x1, up, mod, pre_ffn_g, post_ffn_g, g_up, g_cw, conv_b8, w_down_b, ts)
    gw_up = _wgrad_up(h2, dup, ts_w).reshape(2, 2, 2, FF_CHUNK, D)
    gw_down, r1_up = _wgrad_down(act, df, ts_w, gw_up)
    gw_down = gw_down.reshape(2, 2, 2, FF // N_DEV, D)
    s1_up, s1_up_b, r1_down = _sibling_add("up", [gw_up], [r1_up], coords, swap_srcs=[gw_down])
    s1_down, s1_down_b, _ = _sibling_add("down", [gw_down], r1_down, coords)
    s1_ffn, s1_ffn_b = s1_up + s1_down, s1_up_b + s1_down_b
    ((grad_x, gw_in, gw_out, dmod_m, d_pre_mix, d_post_mix, d_sgn, d_wsp, d_bsp_t, d_wpool, d_ps), r_ffn) = _mix_bwd(
        dx1, x2d, proj, mixed, mod, pre_mix_g, post_mix_g, w_in_tb, sgu_norm_g[0], w_spatial[0], b_sp_t,
        w_pool[0], pool_scale, w_out_b, ts_mix, s1_ffn_b)
    gw_in = gw_in.reshape(N_DEV, IN_WIDTH // N_DEV, D)
    gw_out = gw_out.reshape(N_DEV, D // N_DEV, D)

    big = _final_add_adamw(coords, s1_ffn, list(r_ffn), [w_up_t, w_down[0]], [m_w_up[0].T, m_w_down[0]],
                           [v_w_up[0].T, v_w_down[0]])
    r_up, r_down = tuple(a.T[None] for a in big[0]), tuple(a[None] for a in big[1])

    dmod = jnp.concatenate([dmod_m, dmod_f], axis=0)
    names = ["b_ada", "pre_mix_g", "post_mix_g", "sgu_norm_g", "w_spatial", "b_spatial", "w_pool", "pool_scale",
             "pre_ffn_g", "post_ffn_g", "conv_w", "conv_b"]
    partials = [dmod.reshape(1, N_MOD * D), d_pre_mix, d_post_mix, d_sgn, d_wsp, d_bsp_t.T, d_wpool, d_ps,
                d_pre_ffn, d_post_ffn, d_cw8, d_cb8.reshape(1, 2 * FF), loss_lanes]
    small_w = [b_ada, pre_mix_g, post_mix_g, sgu_norm_g[0], w_spatial[0], b_spatial[0], w_pool[0], pool_scale,
               pre_ffn_g, post_ffn_g, conv_w[0], conv_b]
    small_m = [m_b_ada, m_pre_mix_g, m_post_mix_g, m_sgu_norm_g[0], m_w_spatial[0], m_b_spatial[0], m_w_pool[0],
               m_pool_scale, m_pre_ffn_g, m_post_ffn_g, m_conv_w[0], m_conv_b]
    small_v = [v_b_ada, v_pre_mix_g, v_post_mix_g, v_sgu_norm_g[0], v_w_spatial[0], v_b_spatial[0], v_w_pool[0],
               v_pool_scale, v_pre_ffn_g, v_post_ffn_g, v_conv_w[0], v_conv_b]
    g_mix, sums, dm2d = _tail_exchange([gw_in, gw_out], partials, [nm == "conv_w" for nm in names] + [False],
                                       [int(nm == "w_pool") for nm in names] + [0],
                                       dmod.reshape(N_DEV, 1, MOD_COLS))
    small, loss11 = _small_update(
        sums[:-1] + g_mix, small_w + [w_in_t, w_out[0]], small_m + [m_w_in[0].T, m_w_out[0]],
        small_v + [v_w_in[0].T, v_w_out[0]], scx, dm2d, w_ada[0], m_w_ada[0], v_w_ada[0], sums[-1])
    loss = loss11.reshape(())
    lead = {"sgu_norm_g", "w_spatial", "b_spatial", "w_pool", "conv_w", "w_in", "w_out", "w_ada"}
    res = {nm: tuple((a.T if nm == "w_in" else a)[None] if nm in lead else a for a in four)
           for nm, four in zip(names + ["w_in", "w_out", "w_ada"], small)}
    res.update(w_up=r_up, w_down=r_down)

    order = ["w_ada", "b_ada", "pre_mix_g", "post_mix_g", "w_in", "sgu_norm_g", "w_spatial", "b_spatial", "w_pool",
             "pool_scale", "w_out", "pre_ffn_g", "post_ffn_g", "w_up", "conv_w", "conv_b", "w_down"]
    return (loss, grad_x[None], *[res[nm][0] for nm in order], *[res[nm][1] for nm in order],
            *[res[nm][2] for nm in order], *[res[nm][3] for nm in order])
```

```python
import functools
import math

import jax
import jax.numpy as jnp
from jax import lax
from jax.experimental import pallas as pl
from jax.experimental.pallas import tpu as pltpu

F32 = jnp.float32
BF16 = jnp.bfloat16
MESH = pl.DeviceIdType.MESH

EPS = 1e-6
D = 1024
HEAD = 128
N_HEAD = 4
A_WIDTH = 512
B_WIDTH = 512
IN_WIDTH = 1536
WINDOWS = (2, 4, 8, 16)
CHUNK = 64
FF = 2816
N_DEV = 8
FF_CHUNK = 704
N_MOD = 6
MOD_COLS = 768

ADAM_LR = 0.001
ADAM_B1 = 0.9
ADAM_B2 = 0.999
ADAM_EPS = 1e-08
ADAM_WD = 0.01
ADAM_STEP = 10

VMEM_LIMIT_V7X = 62 * 1024 * 1024
HALO = 8
POOL_HALO = 16

VMEM_SPEC = pl.BlockSpec(memory_space=pltpu.VMEM)
ANY_SPEC = pl.BlockSpec(memory_space=pl.ANY)


def _bf(x):
    return x.astype(BF16)


def _mm(a, b):
    return jnp.dot(a, b, preferred_element_type=F32)


def _mm_nt(a, b):
    return lax.dot_general(a, b, (((1,), (1,)), ((), ())), preferred_element_type=F32)


def _mm_tn(a, b):
    return lax.dot_general(a, b, (((0,), (0,)), ((), ())), preferred_element_type=F32)


def _rstd(x):
    return lax.rsqrt(jnp.mean(x * x, axis=-1, keepdims=True) + EPS)


def _sum0(x):
    return jnp.sum(x, axis=0, keepdims=True)


def _rowmean(x):
    return jnp.mean(x, axis=-1, keepdims=True)


_GELU_K = math.sqrt(2.0 / math.pi)


def _gelu_and_grad(x):
    x2 = x * x
    th = jnp.tanh(_GELU_K * (x + 0.044715 * (x * x2)))
    cdf = 0.5 * th + 0.5
    grad = cdf + x * (1.0 - th * th) * ((0.5 * _GELU_K) + (1.5 * 0.044715 * _GELU_K) * x2)
    return x * cdf, grad


def _gelu(x):
    return x * (0.5 * (1.0 + jnp.tanh(_GELU_K * (x + 0.044715 * (x * x * x)))))


def _sigmoid(x):
    return 0.5 * jnp.tanh(0.5 * x) + 0.5


def _sgu_mask():
    ri = lax.broadcasted_iota(jnp.int32, (HEAD, HEAD), 0)
    ci = lax.broadcasted_iota(jnp.int32, (HEAD, HEAD), 1)
    return (ci // CHUNK) <= (ri // CHUNK)


def _window_sum(ext, w, trailing):
    n = ext.shape[0]
    s, k = ext, 1
    while k < w:
        s = s + pltpu.roll(s, k if trailing else n - k, 0)
        k *= 2
    return s


def _inv_count(row0, n, w):
    t = row0 + lax.broadcasted_iota(jnp.int32, (n, 1), 0)
    return 1.0 / jnp.minimum(t + 1, w).astype(F32)


def _shift_down(v, before, k):
    rows = lax.broadcasted_iota(jnp.int32, before.shape, 0)
    r = pltpu.roll(v, k, 0)
    top = jnp.where(rows < k, pltpu.roll(before, k, 0), r[0:HALO])
    return jnp.concatenate([top, r[HALO:]], axis=0)


def _shift_up(v, after, k):
    n = v.shape[0]
    rows = lax.broadcasted_iota(jnp.int32, after.shape, 0)
    r = pltpu.roll(v, n - k, 0)
    bottom = jnp.where(rows >= HALO - k, pltpu.roll(after, HALO - k, 0), r[n - HALO:])
    return jnp.concatenate([r[:n - HALO], bottom], axis=0)


def _adamw(w, g, m, v):
    m = ADAM_B1 * m + (1.0 - ADAM_B1) * g
    v = ADAM_B2 * v + (1.0 - ADAM_B2) * (g * g)
    m_hat = m / (1.0 - ADAM_B1 ** ADAM_STEP)
    v_hat = v / (1.0 - ADAM_B2 ** ADAM_STEP)
    delta = -ADAM_LR * (m_hat / (jnp.sqrt(v_hat) + ADAM_EPS) + ADAM_WD * w)
    return delta, m, v


def _coords():
    return lax.axis_index("x"), lax.axis_index("y"), lax.axis_index("c")


def _peer(k):
    x, y, c = _coords()
    return (x ^ ((k >> 2) & 1), y ^ ((k >> 1) & 1), c ^ (k & 1))


def _my_index():
    x, y, c = _coords()
    return 4 * x + 2 * y + c


def _adaln_modulation(c_ref, w_ref, b_ref, mod_ref, scx_ref, scbuf, stage, recv, send_sems, recv_sems):
    me = _my_index()
    cv = c_ref[...]
    scbuf[0] = cv * _sigmoid(cv)
    first = [
        pltpu.make_async_remote_copy(scbuf.at[0], scbuf.at[k], send_sems.at[0, k], recv_sems.at[0, k],
                                     device_id=_peer(k), device_id_type=MESH)
        for k in range(1, N_DEV)
    ]
    for cp in first:
        cp.start()
    for cp in first:
        cp.wait()
    scx_ref[...] = jnp.zeros(scx_ref.shape, F32)
    for k in range(N_DEV):
        scx_ref[k:k + 1, :] = scbuf[k]
    prod = _mm(_bf(scx_ref[...]), _bf(w_ref[...]))
    for k in range(N_DEV):
        stage[k] = prod[k:k + 1, :] + b_ref[me]
    second = [
        pltpu.make_async_remote_copy(stage.at[k], recv.at[k], send_sems.at[1, k], recv_sems.at[1, k],
                                     device_id=_peer(k), device_id_type=MESH)
        for k in range(1, N_DEV)
    ]
    for cp in second:
        cp.start()
    mod_ref[me] = stage[0]
    for cp in second:
        cp.wait()
    for k in range(1, N_DEV):
        mod_ref[me ^ k] = recv[k]


class _GatherSteps:
    def __init__(self, ins, outs, stages, send_sems, recv_sems, local_sems):
        self.ins, self.outs, self.stages = ins, outs, stages
        self.send_sems, self.recv_sems, self.local_sems = send_sems, recv_sems, local_sems
        x, y, c = _coords()
        self.c = c
        self.me, self.sibling = (x, y, c), (x, y, 1 - c)
        self.chips = [(1 - x, y), (x, 1 - y), (1 - x, 1 - y)]

    def _copy(self, a, k, block, to, from_stage=False):
        dst = self.outs[a].at[4 * block[0] + 2 * block[1] + block[2]]
        return pltpu.make_async_remote_copy(self.stages[a] if from_stage else dst, dst, self.send_sems.at[a, k],
                                            self.recv_sems.at[a, k], device_id=to, device_id_type=MESH)

    def _local(self, a):
        me = self.me
        return pltpu.make_async_copy(self.stages[a], self.outs[a].at[4 * me[0] + 2 * me[1] + me[2]],
                                     self.local_sems.at[a])

    def _first(self, a):
        cps = [self._copy(a, 0, self.me, self.sibling, from_stage=True)]
        return cps + [self._copy(a, 1 + j, self.me, (*chip, self.c), from_stage=True)
                      for j, chip in enumerate(self.chips)]

    def _passed(self, a, j):
        return self._copy(a, 4 + j, (*self.chips[j], self.c), self.sibling)

    def start(self):
        for a in range(len(self.ins)):
            block = self.ins[a][...]
            if block.shape != self.stages[a].shape:
                block = block.T
            self.stages[a][...] = block.astype(self.stages[a].dtype)
            self._local(a).start()
            for cp in self._first(a):
                cp.start()

    def forward(self):
        for a in range(len(self.ins)):
            for j, chip in enumerate(self.chips):
                self._copy(a, 1 + j, (*chip, self.c), self.me).wait_recv()
                self._passed(a, j).start()

    def finish(self):
        for a in range(len(self.ins)):
            self._copy(a, 0, self.sibling, self.me).wait_recv()
            for j, chip in enumerate(self.chips):
                self._copy(a, 4 + j, (*chip, 1 - self.c), self.me).wait_recv()
            for cp in self._first(a) + [self._passed(a, j) for j in range(3)]:
                cp.wait_send()
            self._local(a).wait()


def _gather_scratch(shapes, out_dtypes):
    n = len(shapes)
    return ([pltpu.VMEM(s, dt) for s, dt in zip(shapes, out_dtypes)]
            + [pltpu.SemaphoreType.DMA((n, 7)), pltpu.SemaphoreType.DMA((n, 7)), pltpu.SemaphoreType.DMA((n,))])


def _gather_out_shapes(shapes, out_dtypes):
    return tuple(jax.ShapeDtypeStruct((N_DEV, *s), dt) for s, dt in zip(shapes, out_dtypes))


def _prologue(c_row, w_ada, b_ada3, shards, out_dtypes):
    n = len(shards)

    def body(*refs):
        c_ref, w_ref, b_ref = refs[:3]
        mod_ref, scx_ref = refs[3 + n:5 + n]
        gather = _GatherSteps(refs[3:3 + n], refs[5 + n:5 + 2 * n], refs[5 + 2 * n:5 + 3 * n],
                              *refs[5 + 3 * n:8 + 3 * n])
        gather.start()
        _adaln_modulation(c_ref, w_ref, b_ref, mod_ref, scx_ref, *refs[8 + 3 * n:])
        gather.forward()
        gather.finish()

    outs = pl.pallas_call(
        body,
        name="prologue",
        out_shape=(jax.ShapeDtypeStruct((N_DEV, 1, MOD_COLS), F32), jax.ShapeDtypeStruct((2 * N_DEV, D), F32))
        + _gather_out_shapes([s.shape for s in shards], out_dtypes),
        in_specs=[VMEM_SPEC] * (3 + n),
        out_specs=(VMEM_SPEC, VMEM_SPEC) + (ANY_SPEC,) * n,
        scratch_shapes=_gather_scratch([s.shape for s in shards], out_dtypes) + [
            pltpu.VMEM((N_DEV, 1, D), F32),
            pltpu.VMEM((N_DEV, 1, MOD_COLS), F32),
            pltpu.VMEM((N_DEV, 1, MOD_COLS), F32),
            pltpu.SemaphoreType.DMA((2, N_DEV)),
            pltpu.SemaphoreType.DMA((2, N_DEV)),
        ],
        compiler_params=pltpu.CompilerParams(vmem_limit_bytes=VMEM_LIMIT_V7X),
    )(c_row, w_ada, b_ada3, *shards)
    return outs[0], outs[1], outs[2:]


class _ChipExchangeSteps:
    FLIPS = ((1, 0), (0, 1), (1, 1))

    def __init__(self, srcs, dsts, send_sems, recv_sems):
        self.srcs, self.dsts, self.send_sems, self.recv_sems = srcs, dsts, send_sems, recv_sems

    def _copies(self):
        x, y, c = _coords()
        out = []
        for a in range(len(self.srcs)):
            for j, (fx, fy) in enumerate(self.FLIPS):
                k = 3 * a + j
                out.append(pltpu.make_async_remote_copy(
                    self.srcs[a].at[x ^ fx, y ^ fy], self.dsts[a].at[j], self.send_sems.at[k], self.recv_sems.at[k],
                    device_id=(x ^ fx, y ^ fy, c), device_id_type=MESH))
        return out

    def start(self):
        for cp in self._copies():
            cp.start()

    def finish(self):
        for cp in self._copies():
            cp.wait()


def _mix_fwd(x, mod, g_pre, g_post, w_in_t, sgn, w_sp, b_sp_t, w_pool, p_scale, w_out_b, ts, shards, shard_shapes,
             shard_dtypes):
    t_len = x.shape[0]
    nt, nb = t_len // ts, ts // HEAD
    ns = len(shards)

    def body(*refs):
        (x_ref, mod_ref, g1_ref, g2_ref, win_ref, sgn_ref, ws_ref, bst_ref, wp_ref, ps_ref, wout_ref) = refs[:11]
        x1_ref, proj_ref, mixed_ref = refs[11 + ns:14 + ns]
        pbuf, cat = refs[14 + 2 * ns:16 + 2 * ns]
        gather = _GatherSteps(refs[11:11 + ns], refs[14 + ns:14 + 2 * ns], refs[16 + 2 * ns:16 + 3 * ns],
                              *refs[16 + 3 * ns:])
        i = pl.program_id(0)

        @pl.when(i == 0)
        def _():
            pbuf[0:POOL_HALO, :] = jnp.zeros((POOL_HALO, B_WIDTH), F32)
            gather.start()

        @pl.when(i == (3 * nt) // 4)
        def _():
            gather.forward()

        xv = x_ref[...]
        sh, sc, gm = mod_ref[0:1, :], mod_ref[1:2, :], mod_ref[2:3, :]
        h = (xv * _rstd(xv) * g1_ref[...]) * (1.0 + sc) + sh
        proj_ref[...] = _mm_nt(_bf(h), win_ref[...])
        pbuf[POOL_HALO:POOL_HALO + ts, :] = proj_ref[:, 2 * A_WIDTH:]
        smask = _sgu_mask()
        for hd in range(N_HEAD):
            u = _gelu(proj_ref[:, hd * HEAD:(hd + 1) * HEAD])
            v = _gelu(proj_ref[:, A_WIDTH + hd * HEAD:A_WIDTH + (hd + 1) * HEAD])
            vn = _bf(v * _rstd(v) * sgn_ref[hd:hd + 1, :])
            wm = _bf(jnp.where(smask, ws_ref[hd], 0.0))
            bias = bst_ref[:, hd:hd + 1]
            for b in range(nb):
                rows = slice(b * HEAD, (b + 1) * HEAD)
                z = _mm(wm, vn[rows]) + bias
                cat[rows, hd * HEAD:(hd + 1) * HEAD] = _bf(u[rows] * z)
        for g, w in enumerate(WINDOWS):
            cols = slice(g * HEAD, (g + 1) * HEAD)
            ext = pbuf[:, cols]
            pooled = _window_sum(ext, w, True)[POOL_HALO:] * _inv_count(i * ts, ts, w) - ext[POOL_HALO:]
            cat[:, A_WIDTH + g * HEAD:A_WIDTH + (g + 1) * HEAD] = _bf(_mm(_bf(pooled), _bf(wp_ref[g])) * ps_ref[:, cols])
        pbuf[0:POOL_HALO, :] = pbuf[ts:ts + POOL_HALO, :]
        mixed = _mm(cat[...], wout_ref[...])
        mixed_ref[...] = mixed
        x1_ref[...] = xv + gm * (mixed * _rstd(mixed) * g2_ref[...])

        @pl.when(i == nt - 1)
        def _():
            gather.finish()

    tile = lambda wid: pl.BlockSpec((ts, wid), lambda i: (i, 0))
    outs = pl.pallas_call(
        body,
        name="mix_fwd",
        grid=(nt,),
        out_shape=(jax.ShapeDtypeStruct((t_len, D), F32), jax.ShapeDtypeStruct((t_len, IN_WIDTH), F32),
                   jax.ShapeDtypeStruct((t_len, D), F32)) + _gather_out_shapes(shard_shapes, shard_dtypes),
        in_specs=[tile(D)] + [VMEM_SPEC] * (10 + ns),
        out_specs=(tile(D), tile(IN_WIDTH), tile(D)) + (ANY_SPEC,) * ns,
        scratch_shapes=[pltpu.VMEM((POOL_HALO + ts, B_WIDTH), F32), pltpu.VMEM((ts, D), BF16)]
        + _gather_scratch(shard_shapes, shard_dtypes),
        compiler_params=pltpu.CompilerParams(dimension_semantics=("arbitrary",), vmem_limit_bytes=VMEM_LIMIT_V7X),
    )(x, mod, g_pre, g_post, w_in_t, sgn, w_sp, b_sp_t, w_pool, p_scale, w_out_b, *shards)
    return outs[:3], outs[3:]


def _ffn_fwd(x1, target, mod, g_pre, g_post, w_up_b, conv_w8, conv_b8, w_down_b, ts):
    t_len = x1.shape[0]
    nt = t_len // ts

    def body(x1_ref, x1_next_ref, tgt_ref, mod_ref, g3_ref, g4_ref, wup_ref, cw_ref, cb_ref, wdown_ref,
             up_ref, f_ref, dx2_ref, loss_ref, ucarry, h2buf):
        i = pl.program_id(0)
        sh, sc, gf = mod_ref[3:4, :], mod_ref[4:5, :], mod_ref[5:6, :]

        def modulated_norm(v):
            return _bf((v * _rstd(v) * g3_ref[...]) * (1.0 + sc) + sh)

        @pl.when(i == 0)
        def _():
            ucarry[...] = jnp.zeros(ucarry.shape, F32)
            loss_ref[...] = jnp.zeros(loss_ref.shape, F32)
            h2buf[...] = modulated_norm(x1_ref[...])

        x1v = x1_ref[...]
        h2 = h2buf[...]
        half = N_DEV // 2

        def up_pair(j):
            return [_mm(h2, wup_ref[jj]) for jj in (j, j + half)]

        f = jnp.zeros((ts, D), F32)
        ups = up_pair(0)
        for j in range(half):
            nxt = up_pair(j + 1) if j + 1 < half else None
            ys = []
            for up, jj in zip(ups, (j, j + half)):
                up_ref[jj] = up
                before = ucarry[jj]
                ucarry[jj] = up[ts - HALO:, :]
                cw = cw_ref[jj]
                ys.append(cb_ref[jj:jj + 1, :] + _shift_down(up, before, 2) * cw[0:1, :]
                          + _shift_down(up, before, 1) * cw[1:2, :] + up * cw[2:3, :])
            gate, val = ys
            act = gate * _sigmoid(gate) * val
            f = f + _mm(_bf(act), wdown_ref[j * FF_CHUNK:(j + 1) * FF_CHUNK, :])
            ups = nxt
        h2buf[...] = modulated_norm(x1_next_ref[...])
        f_ref[...] = f
        x2 = x1v + gf * (f * _rstd(f) * g4_ref[...])
        err = x2 - tgt_ref[...]
        loss_ref[...] += _sum0(err * err)
        dx2_ref[...] = err * (1.0 / D)

    tile = pl.BlockSpec((ts, D), lambda i: (i, 0))
    return pl.pallas_call(
        body,
        name="ffn_fwd",
        grid=(nt,),
        out_shape=(jax.ShapeDtypeStruct((N_DEV, t_len, FF_CHUNK), F32), jax.ShapeDtypeStruct((t_len, D), F32),
                   jax.ShapeDtypeStruct((t_len, D), F32), jax.ShapeDtypeStruct((1, D), F32)),
        in_specs=[tile, pl.BlockSpec((ts, D), lambda i: (jnp.minimum(i + 1, nt - 1), 0)), tile] + [VMEM_SPEC] * 7,
        out_specs=(pl.BlockSpec((N_DEV, ts, FF_CHUNK), lambda i: (0, i, 0)), tile, tile,
                   pl.BlockSpec((1, D), lambda i: (0, 0))),
        scratch_shapes=[pltpu.VMEM((N_DEV, HALO, FF_CHUNK), F32), pltpu.VMEM((ts, D), BF16)],
        compiler_params=pltpu.CompilerParams(dimension_semantics=("arbitrary",), vmem_limit_bytes=VMEM_LIMIT_V7X),
    )(x1, x1, target, mod, g_pre, g_post, w_up_b, conv_w8, conv_b8, w_down_b)


def _ffn_bwd(dx2, f, x1, up, mod, g_pre, g_post, w_up_b, conv_w8, conv_b8, w_down_b, ts):
    t_len = x1.shape[0]
    nt = t_len // ts
    half = N_DEV // 2

    def body(dx2_ref, f_ref, x1_ref, up_ref, halo_ref, mod_ref, g3_ref, g4_ref, wup_ref, cw_ref, cb_ref, wdown_ref,
             dx1_ref, dup_ref, act_ref, df_ref, h2_ref, dmod_ref, dg3_ref, dg4_ref, dcb_ref, dcw_ref,
             dycarry, dh2acc):
        i = pl.program_id(0)
        r = nt - 1 - i

        @pl.when(i == 0)
        def _():
            for ref in (dmod_ref, dg3_ref, dg4_ref, dcb_ref, dcw_ref, dycarry):
                ref[...] = jnp.zeros(ref.shape, F32)

        dx2v, fv, x1v = dx2_ref[...], f_ref[...], x1_ref[...]
        sh, sc, gf = mod_ref[3:4, :], mod_ref[4:5, :], mod_ref[5:6, :]
        g3, g4 = g3_ref[...], g4_ref[...]
        rstd4 = _rstd(fv)
        fh = fv * rstd4
        dmod_ref[2:3, :] += _sum0(dx2v * (fh * g4))
        dr = dx2v * gf
        dg4_ref[...] += _sum0(dr * fh)
        dfh = dr * g4
        dfb = _bf(rstd4 * (dfh - fh * _rowmean(dfh * fh)))
        df_ref[...] = dfb
        rstd3 = _rstd(x1v)
        xh = x1v * rstd3
        n3 = xh * g3
        h2_ref[...] = _bf(n3 * (1.0 + sc) + sh)
        dh2acc[...] = jnp.zeros((ts, D), F32)
        keep = jnp.where(r > 0, 1.0, 0.0).astype(F32)

        def dact_of(j):
            return _mm_nt(dfb, wdown_ref[j * FF_CHUNK:(j + 1) * FF_CHUNK, :])

        dact_next = dact_of(0)
        for j in range(half):
            dact = dact_next
            if j + 1 < half:
                dact_next = dact_of(j + 1)
            ys = []
            for jj in (j, j + half):
                before = halo_ref[jj] * keep
                upc = up_ref[jj]
                cw = cw_ref[jj]
                ys.append(cb_ref[jj:jj + 1, :] + _shift_down(upc, before, 2) * cw[0:1, :]
                          + _shift_down(upc, before, 1) * cw[1:2, :] + upc * cw[2:3, :])
            gate, val = ys
            sg = _sigmoid(gate)
            silu = gate * sg
            act_ref[j] = _bf(silu * val)
            dys = (dact * val * (sg + silu * (1.0 - sg)), dact * silu)
            for q, jj in enumerate((j, j + half)):
                dy = dys[q]
                cw = cw_ref[jj]
                dcb_ref[jj:jj + 1, :] += _sum0(dy)
                after = dycarry[jj]
                dycarry[jj] = dy[0:HALO, :]
                dy1, dy2 = _shift_up(dy, after, 1), _shift_up(dy, after, 2)
                upc = up_ref[jj]
                dcw_ref[jj, 0:1, :] += _sum0(dy2 * upc)
                dcw_ref[jj, 1:2, :] += _sum0(dy1 * upc)
                dcw_ref[jj, 2:3, :] += _sum0(dy * upc)
                dup = _bf(dy * cw[2:3, :] + dy1 * cw[1:2, :] + dy2 * cw[0:1, :])
                dup_ref[jj] = dup
                dh2acc[...] += _mm_nt(dup, wup_ref[jj])
        dh2 = dh2acc[...]
        dmod_ref[0:1, :] += _sum0(dh2)
        dmod_ref[1:2, :] += _sum0(dh2 * n3)
        dn3 = dh2 * (1.0 + sc)
        dg3_ref[...] += _sum0(dn3 * xh)
        dxh = dn3 * g3
        dx1_ref[...] = dx2v + rstd3 * (dxh - xh * _rowmean(dxh * xh))

    tile = pl.BlockSpec((ts, D), lambda i: (nt - 1 - i, 0))
    chunked = lambda n: pl.BlockSpec((n, ts, FF_CHUNK), lambda i: (0, nt - 1 - i, 0))
    halo = pl.BlockSpec((N_DEV, HALO, FF_CHUNK), lambda i: (0, jnp.maximum((nt - 1 - i) * (ts // HALO) - 1, 0), 0))
    const = lambda *shape: pl.BlockSpec(shape, lambda i: (0,) * len(shape))
    return pl.pallas_call(
        body,
        name="ffn_bwd",
        grid=(nt,),
        out_shape=(jax.ShapeDtypeStruct((t_len, D), F32), jax.ShapeDtypeStruct((N_DEV, t_len, FF_CHUNK), BF16),
                   jax.ShapeDtypeStruct((half, t_len, FF_CHUNK), BF16), jax.ShapeDtypeStruct((t_len, D), BF16),
                   jax.ShapeDtypeStruct((t_len, D), BF16), jax.ShapeDtypeStruct((3, D), F32),
                   jax.ShapeDtypeStruct((1, D), F32), jax.ShapeDtypeStruct((1, D), F32),
                   jax.ShapeDtypeStruct((N_DEV, FF_CHUNK), F32), jax.ShapeDtypeStruct((N_DEV, 3, FF_CHUNK), F32)),
        in_specs=[tile, tile, tile, chunked(N_DEV), halo] + [VMEM_SPEC] * 7,
        out_specs=(tile, chunked(N_DEV), chunked(half), tile, tile, const(3, D), const(1, D), const(1, D),
                   const(N_DEV, FF_CHUNK), const(N_DEV, 3, FF_CHUNK)),
        scratch_shapes=[pltpu.VMEM((N_DEV, HALO, FF_CHUNK), F32), pltpu.VMEM((ts, D), F32)],
        compiler_params=pltpu.CompilerParams(dimension_semantics=("arbitrary",), vmem_limit_bytes=VMEM_LIMIT_V7X),
    )(dx2, f, x1, up, up, mod, g_pre, g_post, w_up_b, conv_w8, conv_b8, w_down_b)


def _wgrad_up(h2, dup, ts):
    t_len = h2.shape[0]
    nt, half = t_len // ts, N_DEV // 2

    def body(h2_ref, dup_ref, out_ref):
        @pl.when(pl.program_id(1) == 0)
        def _():
            out_ref[...] = jnp.zeros(out_ref.shape, F32)

        for q in range(half):
            out_ref[q] += _mm_tn(dup_ref[q], h2_ref[...])

    return pl.pallas_call(
        body,
        name="wgrad_up",
        grid=(2, nt),
        out_shape=jax.ShapeDtypeStruct((N_DEV, FF_CHUNK, D), F32),
        in_specs=[pl.BlockSpec((ts, D), lambda g, t: (t, 0)), pl.BlockSpec((half, ts, FF_CHUNK), lambda g, t: (g, t, 0))],
        out_specs=pl.BlockSpec((half, FF_CHUNK, D), lambda g, t: (g, 0, 0)),
        compiler_params=pltpu.CompilerParams(dimension_semantics=("arbitrary", "arbitrary"),
                                             vmem_limit_bytes=VMEM_LIMIT_V7X),
    )(h2, dup)


def _sibling_swap_copies(srcs, dsts, send_sems, recv_sems):
    x, y, c = _coords()
    return [
        pltpu.make_async_remote_copy(srcs[a].at[xs, ys, 1 - c], dsts[a].at[xs, ys], send_sems.at[a, 2 * xs + ys],
                                     recv_sems.at[a, 2 * xs + ys], device_id=(x, y, 1 - c), device_id_type=MESH)
        for a in range(len(srcs)) for xs in range(2) for ys in range(2)
    ]


def _wgrad_down(act, df, ts, swap_src):
    t_len = df.shape[0]
    nt, half = t_len // ts, N_DEV // 2

    def body(act_ref, df_ref, src_ref, out_ref, dst_ref, send_sems, recv_sems):
        t = pl.program_id(0)

        @pl.when(t == 0)
        def _():
            for cp in _sibling_swap_copies([src_ref], [dst_ref], send_sems, recv_sems):
                cp.start()
            out_ref[...] = jnp.zeros(out_ref.shape, F32)

        for q in range(half):
            out_ref[q] += _mm_tn(act_ref[q], df_ref[...])

        @pl.when(t == nt - 1)
        def _():
            for cp in _sibling_swap_copies([src_ref], [dst_ref], send_sems, recv_sems):
                cp.wait()

    return pl.pallas_call(
        body,
        name="wgrad_down",
        grid=(nt,),
        out_shape=(jax.ShapeDtypeStruct((half, FF_CHUNK, D), F32), jax.ShapeDtypeStruct(swap_src.shape[1:], F32)),
        in_specs=[pl.BlockSpec((half, ts, FF_CHUNK), lambda t: (0, t, 0)), pl.BlockSpec((ts, D), lambda t: (t, 0)),
                  ANY_SPEC],
        out_specs=(pl.BlockSpec((half, FF_CHUNK, D), lambda t: (0, 0, 0)), ANY_SPEC),
        scratch_shapes=[pltpu.SemaphoreType.DMA((1, 4)), pltpu.SemaphoreType.DMA((1, 4))],
        compiler_params=pltpu.CompilerParams(dimension_semantics=("arbitrary",), vmem_limit_bytes=VMEM_LIMIT_V7X),
    )(act, df, swap_src)


def _mix_bwd(dx1, x, proj, mixed, mod, g_pre, g_post, w_in_t, sgn, w_sp, b_sp_t, w_pool, p_scale, w_out_b, ts, rs_srcs):
    t_len = x.shape[0]
    nt, nb = t_len // ts, ts // HEAD
    nr = len(rs_srcs)

    def body(*refs):
        (dx1_ref, x_ref, proj_ref, halo_ref, mixed_ref, mod_ref, g1_ref, g2_ref, win_ref, sgn_ref, ws_ref,
         bst_ref, wp_ref, ps_ref, wout_ref) = refs[:15]
        (gx_ref, dwin_ref, dwout_ref, dmod_ref, dg1_ref, dg2_ref, dsgn_ref, dws_ref, dbst_ref, dwp_ref,
         dps_ref) = refs[15 + nr:26 + nr]
        pbuf, dwsbuf, cat, dproj, dcat = refs[26 + 2 * nr:31 + 2 * nr]
        exchange = _ChipExchangeSteps(refs[15:15 + nr], refs[26 + nr:26 + 2 * nr], *refs[31 + 2 * nr:])
        i = pl.program_id(0)
        r = nt - 1 - i

        @pl.when(i == 0)
        def _():
            exchange.start()
            for ref in (dwin_ref, dwout_ref, dmod_ref, dg1_ref, dg2_ref, dsgn_ref, dws_ref, dbst_ref, dwp_ref, dps_ref):
                ref[...] = jnp.zeros(ref.shape, F32)
            dwsbuf[ts:ts + POOL_HALO, :] = jnp.zeros((POOL_HALO, B_WIDTH), F32)

        xv, dx1v, mixed = x_ref[...], dx1_ref[...], mixed_ref[...]
        sh, sc, gm = mod_ref[0:1, :], mod_ref[1:2, :], mod_ref[2:3, :]
        g1, g2 = g1_ref[...], g2_ref[...]
        rstd2 = _rstd(mixed)
        mh = mixed * rstd2
        dmod_ref[2:3, :] += _sum0(dx1v * (mh * g2))
        dr = dx1v * gm
        dg2_ref[...] += _sum0(dr * mh)
        dmh = dr * g2
        dmb = _bf(rstd2 * (dmh - mh * _rowmean(dmh * mh)))
        dcat[...] = _mm_nt(dmb, wout_ref[...])
        smask = _sgu_mask()
        for hd in range(N_HEAD):
            ucols = slice(hd * HEAD, (hd + 1) * HEAD)
            vcols = slice(A_WIDTH + hd * HEAD, A_WIDTH + (hd + 1) * HEAD)
            u, du_dp = _gelu_and_grad(proj_ref[:, ucols])
            v, dv_dp = _gelu_and_grad(proj_ref[:, vcols])
            rs = _rstd(v)
            vhat = v * rs
            gn = sgn_ref[hd:hd + 1, :]
            vn = _bf(vhat * gn)
            wm = _bf(jnp.where(smask, ws_ref[hd], 0.0))
            bias = bst_ref[:, hd:hd + 1]
            dzsum = jnp.zeros((HEAD, HEAD), F32)
            dwm = jnp.zeros((HEAD, HEAD), F32)
            dvn_parts = []
            for b in range(nb):
                rows = slice(b * HEAD, (b + 1) * HEAD)
                z = _mm(wm, vn[rows]) + bias
                da = dcat[rows, ucols]
                cat[rows, ucols] = _bf(u[rows] * z)
                dz = da * u[rows]
                dzsum = dzsum + dz
                dzb = _bf(dz)
                dwm = dwm + _mm_nt(dzb, vn[rows])
                dvn_parts.append(_mm_tn(wm, dzb))
                dproj[rows, ucols] = _bf((da * z) * du_dp[rows])
            dvn = jnp.concatenate(dvn_parts, axis=0)
            dsgn_ref[hd:hd + 1, :] += _sum0(dvn * vhat)
            dvh = dvn * gn
            dproj[:, vcols] = _bf((rs * (dvh - vhat * _rowmean(dvh * vhat))) * dv_dp)
            dws_ref[hd] += jnp.where(smask, dwm, 0.0)
            dbst_ref[:, hd:hd + 1] += jnp.sum(dzsum, axis=1, keepdims=True)
        keep = jnp.where(r > 0, 1.0, 0.0).astype(F32)
        pbuf[0:POOL_HALO, :] = halo_ref[...] * keep
        pbuf[POOL_HALO:POOL_HALO + ts, :] = proj_ref[:, 2 * A_WIDTH:]
        for g, w in enumerate(WINDOWS):
            cols = slice(g * HEAD, (g + 1) * HEAD)
            ccols = slice(A_WIDTH + g * HEAD, A_WIDTH + (g + 1) * HEAD)
            pcols = slice(2 * A_WIDTH + g * HEAD, 2 * A_WIDTH + (g + 1) * HEAD)
            wpg = _bf(wp_ref[g])
            psg = ps_ref[:, cols]
            ext = pbuf[:, cols]
            inv = _inv_count(r * ts, ts, w)
            pb = _bf(_window_sum(ext, w, True)[POOL_HALO:] * inv - ext[POOL_HALO:])
            yb = _mm(pb, wpg)
            dob = dcat[:, ccols]
            cat[:, ccols] = _bf(yb * psg)
            dps_ref[:, cols] += _sum0(dob * yb)
            dyb = _bf(dob * psg)
            dwp_ref[g] += _mm_tn(pb, dyb)
            dpooled = _mm_nt(dyb, wpg)
            dwsbuf[0:ts, cols] = dpooled * inv
            dproj[:, pcols] = _bf(_window_sum(dwsbuf[:, cols], w, False)[0:ts] - dpooled)
        dwsbuf[ts:ts + POOL_HALO, :] = dwsbuf[0:POOL_HALO, :]
        dpb = dproj[...]
        rstd1 = _rstd(xv)
        xh = xv * rstd1
        n1 = xh * g1
        dwin_ref[...] += _mm_tn(dpb, _bf(n1 * (1.0 + sc) + sh))
        dwout_ref[...] += _mm_tn(cat[...], dmb)
        dh = _mm(dpb, win_ref[...])
        dmod_ref[0:1, :] += _sum0(dh)
        dmod_ref[1:2, :] += _sum0(dh * n1)
        dn1 = dh * (1.0 + sc)
        dg1_ref[...] += _sum0(dn1 * xh)
        dxh = dn1 * g1
        gx_ref[...] = dx1v + rstd1 * (dxh - xh * _rowmean(dxh * xh))

        @pl.when(i == nt - 1)
        def _():
            exchange.finish()

    tile = lambda wid: pl.BlockSpec((ts, wid), lambda i: (nt - 1 - i, 0))
    halo = pl.BlockSpec((POOL_HALO, B_WIDTH),
                        lambda i: (jnp.maximum((nt - 1 - i) * (ts // POOL_HALO) - 1, 0), 2 * A_WIDTH // B_WIDTH))
    const = lambda *shape: pl.BlockSpec(shape, lambda i: (0,) * len(shape))
    resident = lambda *shape: pl.BlockSpec(shape, lambda i: (0,) * len(shape), pipeline_mode=pl.Buffered(1))
    outs = pl.pallas_call(
        body,
        name="mix_bwd",
        grid=(nt,),
        out_shape=(jax.ShapeDtypeStruct((t_len, D), F32), jax.ShapeDtypeStruct((IN_WIDTH, D), F32),
                   jax.ShapeDtypeStruct((D, D), F32), jax.ShapeDtypeStruct((3, D), F32),
                   jax.ShapeDtypeStruct((1, D), F32), jax.ShapeDtypeStruct((1, D), F32),
                   jax.ShapeDtypeStruct((N_HEAD, HEAD), F32), jax.ShapeDtypeStruct((N_HEAD, HEAD, HEAD), F32),
                   jax.ShapeDtypeStruct((HEAD, N_HEAD), F32), jax.ShapeDtypeStruct((N_HEAD, HEAD, HEAD), F32),
                   jax.ShapeDtypeStruct((1, B_WIDTH), F32))
        + tuple(jax.ShapeDtypeStruct((3, *s.shape[2:]), s.dtype) for s in rs_srcs),
        in_specs=[tile(D), tile(D), tile(IN_WIDTH), halo, tile(D)] + [VMEM_SPEC] * 10 + [ANY_SPEC] * nr,
        out_specs=(tile(D), resident(IN_WIDTH, D), resident(D, D), const(3, D), const(1, D), const(1, D),
                   const(N_HEAD, HEAD), const(N_HEAD, HEAD, HEAD), const(HEAD, N_HEAD), const(N_HEAD, HEAD, HEAD),
                   const(1, B_WIDTH)) + (ANY_SPEC,) * nr,
        scratch_shapes=[pltpu.VMEM((POOL_HALO + ts, B_WIDTH), F32), pltpu.VMEM((ts + POOL_HALO, B_WIDTH), F32),
                        pltpu.VMEM((ts, D), BF16), pltpu.VMEM((ts, IN_WIDTH), BF16), pltpu.VMEM((ts, D), F32),
                        pltpu.SemaphoreType.DMA((3 * nr,)), pltpu.SemaphoreType.DMA((3 * nr,))],
        compiler_params=pltpu.CompilerParams(dimension_semantics=("arbitrary",), vmem_limit_bytes=VMEM_LIMIT_V7X),
    )(dx1, x, proj, proj, mixed, mod, g_pre, g_post, w_in_t, sgn, w_sp, b_sp_t, w_pool, p_scale, w_out_b, *rs_srcs)
    return outs[:11], outs[11:]


def _pair_add(name, coords, grid, specs_a, specs_b, out_specs, out_shapes, a_arrays, b_arrays, swap_srcs):
    n, ns = len(a_arrays), len(swap_srcs)
    last = tuple(g - 1 for g in grid)

    def body(co_ref, *refs):
        ids = [pl.program_id(d) for d in range(len(grid))]
        swap = refs[2 * n:2 * n + ns], refs[4 * n + ns:4 * n + 2 * ns], *refs[4 * n + 2 * ns:]
        if ns:
            @pl.when(functools.reduce(jnp.logical_and, [i == 0 for i in ids]))
            def _():
                for cp in _sibling_swap_copies(*swap):
                    cp.start()

        for k in range(n):
            total = refs[k][...] + refs[n + k][...]
            refs[2 * n + ns + k][...] = total
            refs[3 * n + ns + k][...] = _bf(total)

        if ns:
            @pl.when(functools.reduce(jnp.logical_and, [i == e for i, e in zip(ids, last)]))
            def _():
                for cp in _sibling_swap_copies(*swap):
                    cp.wait()

    outs = pl.pallas_call(
        body,
        name=name,
        grid_spec=pltpu.PrefetchScalarGridSpec(
            num_scalar_prefetch=1, grid=grid, in_specs=specs_a + specs_b + [ANY_SPEC] * ns,
            out_specs=out_specs * 2 + [ANY_SPEC] * ns,
            scratch_shapes=[pltpu.SemaphoreType.DMA((ns, 4)), pltpu.SemaphoreType.DMA((ns, 4))] if ns else []),
        out_shape=tuple(jax.ShapeDtypeStruct(s, dt) for dt in (F32, BF16) for s in out_shapes)
        + tuple(jax.ShapeDtypeStruct(g.shape[1:], F32) for g in swap_srcs),
        compiler_params=pltpu.CompilerParams(dimension_semantics=("arbitrary",) * len(grid),
                                             vmem_limit_bytes=VMEM_LIMIT_V7X),
    )(coords, *a_arrays, *b_arrays, *swap_srcs)
    return list(outs[:n]), list(outs[n:2 * n]), list(outs[2 * n:])


def _final_add_adamw(coords, s1, r, ws, ms, vs, n_split=4):
    n = len(s1)

    def body(co_ref, *refs):
        for k in range(n):
            s_ref, r_ref, w_ref, m_ref, v_ref = (refs[q * n + k] for q in range(5))
            g_ref, d_ref, nm_ref, nv_ref = (refs[(5 + q) * n + k] for q in range(4))
            g = ((s_ref[...] + r_ref[0].astype(F32)) + r_ref[1].astype(F32)) + r_ref[2].astype(F32)
            g_ref[...] = g
            delta, m, v = _adamw(w_ref[...], g, m_ref[...], v_ref[...])
            d_ref[...] = delta
            nm_ref[...] = m
            nv_ref[...] = v

    def shard_spec(a):
        rows, cols = a.shape
        return pl.BlockSpec((rows // n_split, cols), lambda i, co: (i, 0))

    def mine_spec(a):
        rows, cols = a.shape[2:]
        return pl.BlockSpec((None, None, rows // n_split, cols), lambda i, co: (co[0], co[1], i, 0))

    def recv_spec(a):
        rows, cols = a.shape[1:]
        return pl.BlockSpec((3, rows // n_split, cols), lambda i, co: (0, i, 0))

    in_specs = ([mine_spec(a) for a in s1] + [recv_spec(a) for a in r] + [shard_spec(a) for a in ws] * 3)
    out_specs = [shard_spec(a) for a in ws] * 4
    outs = pl.pallas_call(
        body,
        name="grad_final_adamw",
        grid_spec=pltpu.PrefetchScalarGridSpec(num_scalar_prefetch=1, grid=(n_split,), in_specs=in_specs,
                                               out_specs=out_specs),
        out_shape=tuple(jax.ShapeDtypeStruct(a.shape, F32) for a in ws) * 4,
        compiler_params=pltpu.CompilerParams(dimension_semantics=("arbitrary",), vmem_limit_bytes=VMEM_LIMIT_V7X),
    )(coords, *s1, *r, *ws, *ms, *vs)
    return [tuple(outs[q * n + k] for q in range(4)) for k in range(n)]


def _sibling_add(tag, g5, r1, coords, swap_srcs=(), n_split=4):
    shapes = [g.shape[3:] for g in g5]
    spec_g = [pl.BlockSpec((None, None, None, s[0] // n_split, s[1]), lambda i, j, k, co: (i, j, co[2], k, 0))
              for s in shapes]
    spec_r = [pl.BlockSpec((None, None, s[0] // n_split, s[1]), lambda i, j, k, co: (i, j, k, 0)) for s in shapes]
    return _pair_add("grad_add_core_" + tag, coords, (2, 2, n_split), spec_g, spec_r, spec_r,
                     [(2, 2, *s) for s in shapes], g5, r1, list(swap_srcs))


def _tail_exchange(big, partials, pick_mine, y_first, dmod3):
    n, nb = len(partials), len(big)
    big_shapes = [g.shape[1:] for g in big]
    big5 = [g.reshape(2, 2, 2, *s) for g, s in zip(big, big_shapes)]
    flips = _ChipExchangeSteps.FLIPS

    def body(*refs):
        g5, p_in, dm_ref = refs[:nb], refs[nb:nb + n], refs[nb + n]
        outs = refs[nb + n + 1:2 * nb + 2 * n + 2]
        g_out, sums, dm2d = outs[:nb], outs[nb:nb + n], outs[nb + n]
        scratch = refs[2 * nb + 2 * n + 2:]
        s1, stage, chip_recv = scratch[:nb], scratch[nb:2 * nb], scratch[2 * nb:3 * nb]
        acc, rbuf = scratch[3 * nb:3 * nb + n], scratch[3 * nb + n:3 * nb + 2 * n]
        (dm_recv, send_sems, recv_sems, dm_send_sems, dm_recv_sems, sib_send, sib_recv, chip_send,
         chip_recv_sems) = scratch[3 * nb + 2 * n:]
        x, y, c = _coords()
        me = 4 * x + 2 * y + c
        sibling = (x, y, 1 - c)
        dm_copies = [
            pltpu.make_async_remote_copy(dm_ref.at[me ^ k], dm_recv.at[k], dm_send_sems.at[k], dm_recv_sems.at[k],
                                         device_id=_peer(k), device_id_type=MESH)
            for k in range(1, N_DEV)
        ]
        for cp in dm_copies:
            cp.start()
        sib_copies = _sibling_swap_copies(g5, s1, sib_send, sib_recv)
        for cp in sib_copies:
            cp.start()
        for a in range(n):
            acc[a][...] = p_in[a][...]

        def small_phase(ph, peers):
            copies = [
                pltpu.make_async_remote_copy(acc[a], rbuf[a].at[ph], send_sems.at[ph, a], recv_sems.at[ph, a],
                                             device_id=peers[y_first[a]], device_id_type=MESH)
                for a in range(n)
            ]
            for cp in copies:
                cp.start()
            for cp in copies:
                cp.wait()
            for a in range(n):
                acc[a][...] = acc[a][...] + rbuf[a][ph]

        small_phase(0, (sibling, sibling))
        for cp in sib_copies:
            cp.wait()
        for a in range(nb):
            for xs in range(2):
                for ys in range(2):
                    total = g5[a][xs, ys, c] + s1[a][xs, ys]
                    s1[a][xs, ys] = total
                    stage[a][xs, ys] = _bf(total)
        chip_copies = [
            pltpu.make_async_remote_copy(stage[a].at[x ^ fx, y ^ fy], chip_recv[a].at[j], chip_send.at[a, j],
                                         chip_recv_sems.at[a, j], device_id=(x ^ fx, y ^ fy, c), device_id_type=MESH)
            for a in range(nb) for j, (fx, fy) in enumerate(flips)
        ]
        for cp in chip_copies:
            cp.start()
        x_peer, y_peer = (1 - x, y, c), (x, 1 - y, c)
        small_phase(1, (x_peer, y_peer))
        small_phase(2, (y_peer, x_peer))
        for a in range(n):
            sums[a][...] = acc[a][me] if pick_mine[a] else acc[a][...]
        dm2d[...] = jnp.zeros(dm2d.shape, F32)
        dm2d[0:1, :] = dm_ref[me]
        for cp in dm_copies:
            cp.wait()
        for k in range(1, N_DEV):
            dm2d[k:k + 1, :] = dm_recv[k]
        for cp in chip_copies:
            cp.wait()
        for a in range(nb):
            g_out[a][...] = ((s1[a][x, y] + chip_recv[a][0].astype(F32)) + chip_recv[a][1].astype(F32)) \
                + chip_recv[a][2].astype(F32)

    out_shapes = tuple(jax.ShapeDtypeStruct(s, F32) for s in big_shapes) + tuple(
        jax.ShapeDtypeStruct(p.shape[1:] if pk else p.shape, F32) for p, pk in zip(partials, pick_mine))
    outs = pl.pallas_call(
        body,
        name="tail_exchange",
        out_shape=out_shapes + (jax.ShapeDtypeStruct((2 * N_DEV, MOD_COLS), F32),),
        in_specs=[VMEM_SPEC] * (nb + n + 1),
        out_specs=(VMEM_SPEC,) * (nb + n + 1),
        scratch_shapes=[pltpu.VMEM((2, 2, *s), F32) for s in big_shapes]
        + [pltpu.VMEM((2, 2, *s), BF16) for s in big_shapes]
        + [pltpu.VMEM((3, *s), BF16) for s in big_shapes]
        + [pltpu.VMEM(p.shape, F32) for p in partials]
        + [pltpu.VMEM((3, *p.shape), F32) for p in partials]
        + [pltpu.VMEM((N_DEV, 1, MOD_COLS), F32), pltpu.SemaphoreType.DMA((3, n)), pltpu.SemaphoreType.DMA((3, n)),
           pltpu.SemaphoreType.DMA((N_DEV,)), pltpu.SemaphoreType.DMA((N_DEV,)),
           pltpu.SemaphoreType.DMA((nb, 4)), pltpu.SemaphoreType.DMA((nb, 4)),
           pltpu.SemaphoreType.DMA((nb, 3)), pltpu.SemaphoreType.DMA((nb, 3))],
        compiler_params=pltpu.CompilerParams(vmem_limit_bytes=VMEM_LIMIT_V7X),
    )(*big5, *partials, dmod3)
    return list(outs[:nb]), list(outs[nb:nb + n]), outs[nb + n]


def _small_update(grads, ws, ms, vs, scx, dm2d, w_ada, m_ada, v_ada, loss_lanes):
    n = len(grads)

    def body(*refs):
        g_in, w_in, m_in, v_in = (refs[q * n:(q + 1) * n] for q in range(4))
        scx_ref, dm_ref, wa_ref, ma_ref, va_ref, ll_ref = refs[4 * n:4 * n + 6]
        outs = refs[4 * n + 6:]
        g_out, d_out, nm_out, nv_out = (outs[q * (n + 1):(q + 1) * (n + 1)] for q in range(4))
        loss_ref = outs[4 * (n + 1)]
        for a in range(n + 1):
            if a < n:
                g, w, m, v = g_in[a][...], w_in[a][...], m_in[a][...], v_in[a][...]
            else:
                g = _mm_tn(_bf(scx_ref[...]), _bf(dm_ref[...]))
                w, m, v = wa_ref[...], ma_ref[...], va_ref[...]
            g_out[a][...] = g
            delta, m, v = _adamw(w, g, m, v)
            d_out[a][...] = delta
            nm_out[a][...] = m
            nv_out[a][...] = v
        loss_ref[...] = jnp.sum(ll_ref[...], axis=1, keepdims=True) * (0.5 / D)

    w_shapes = tuple(jax.ShapeDtypeStruct(w.shape, F32) for w in list(ws) + [w_ada])
    outs = pl.pallas_call(
        body,
        name="small_update",
        out_shape=w_shapes * 4 + (jax.ShapeDtypeStruct((1, 1), F32),),
        in_specs=[VMEM_SPEC] * (4 * n + 6),
        out_specs=(VMEM_SPEC,) * (4 * (n + 1) + 1),
        compiler_params=pltpu.CompilerParams(vmem_limit_bytes=VMEM_LIMIT_V7X),
    )(*grads, *ws, *ms, *vs, scx, dm2d, w_ada, m_ada, v_ada, loss_lanes)
    return [tuple(outs[q * (n + 1) + k] for q in range(4)) for k in range(n + 1)], outs[4 * (n + 1)]


def kernel(x, c, w_ada, b_ada, pre_mix_g, post_mix_g, w_in, sgu_norm_g, w_spatial, b_spatial, w_pool, pool_scale, w_out, pre_ffn_g, post_ffn_g, w_up, conv_w, conv_b, w_down, loss_target, m_w_ada, m_b_ada, m_pre_mix_g, m_post_mix_g, m_w_in, m_sgu_norm_g, m_w_spatial, m_b_spatial, m_w_pool, m_pool_scale, m_w_out, m_pre_ffn_g, m_post_ffn_g, m_w_up, m_conv_w, m_conv_b, m_w_down, v_w_ada, v_b_ada, v_pre_mix_g, v_post_mix_g, v_w_in, v_sgu_norm_g, v_w_spatial, v_b_spatial, v_w_pool, v_pool_scale, v_w_out, v_pre_ffn_g, v_post_ffn_g, v_w_up, v_conv_w, v_conv_b, v_w_down):
    t_len = x.shape[1]
    ts = min(256, t_len)
    ts_mix = min(512, t_len)
    ts_w = min(1024, t_len)
    coords = jnp.stack([lax.axis_index("x"), lax.axis_index("y"), lax.axis_index("c")]).astype(jnp.int32)

    w_in_t, w_up_t = w_in[0].T, w_up[0].T
    mod3, scx, (g_in, g_out, g_down) = _prologue(c, w_ada[0], b_ada.reshape(N_DEV, 1, MOD_COLS),
                                                 [w_in_t, w_out[0], w_down[0]], [BF16, BF16, BF16])
    mod = mod3.reshape(N_MOD, D)
    w_in_tb = g_in.reshape(IN_WIDTH, D)
    w_out_b = g_out.reshape(D, D)
    conv_b8 = conv_b.reshape(N_DEV, FF_CHUNK)
    b_sp_t = b_spatial[0].T

    x2d, tgt = x[0], loss_target[0]
    (x1, proj, mixed), (g_up, g_cw) = _mix_fwd(
        x2d, mod, pre_mix_g, post_mix_g, w_in_tb, sgu_norm_g[0], w_spatial[0], b_sp_t, w_pool[0], pool_scale, w_out_b,
        ts_mix, [w_up_t, conv_w[0]], [w_up.shape[1:], conv_w.shape[1:]], [BF16, F32])
    w_down_b = g_down.reshape(FF, D)
    up, f, dx2, loss_lanes = _ffn_fwd(x1, tgt, mod, pre_ffn_g, post_ffn_g, g_up, g_cw, conv_b8, w_down_b, ts)

    (dx1, dup, act, df, h2, dmod_f, d_pre_ffn, d_post_ffn, d_cb8, d_cw8) = _ffn_bwd(
        dx2, f, x1, up, mod, pre_ffn_g, post_ffn_g, g_up, g_cw, conv_b8, w_down_b, ts)
    gw_up = _wgrad_up(h2, dup, ts_w).reshape(2, 2, 2, FF_CHUNK, D)
    gw_down, r1_up = _wgrad_down(act, df, ts_w, gw_up)
    gw_down = gw_down.reshape(2, 2, 2, FF // N_DEV, D)
    s1_up, s1_up_b, r1_down = _sibling_add("up", [gw_up], [r1_up], coords, swap_srcs=[gw_down])
    s1_down, s1_down_b, _ = _sibling_add("down", [gw_down], r1_down, coords)
    s1_ffn, s1_ffn_b = s1_up + s1_down, s1_up_b + s1_down_b
    ((grad_x, gw_in, gw_out, dmod_m, d_pre_mix, d_post_mix, d_sgn, d_wsp, d_bsp_t, d_wpool, d_ps), r_ffn) = _mix_bwd(
        dx1, x2d, proj, mixed, mod, pre_mix_g, post_mix_g, w_in_tb, sgu_norm_g[0], w_spatial[0], b_sp_t,
        w_pool[0], pool_scale, w_out_b, ts_mix, s1_ffn_b)
    gw_in = gw_in.reshape(N_DEV, IN_WIDTH // N_DEV, D)
    gw_out = gw_out.reshape(N_DEV, D // N_DEV, D)

    big = _final_add_adamw(coords, s1_ffn, list(r_ffn), [w_up_t, w_down[0]], [m_w_up[0].T, m_w_down[0]],
                           [v_w_up[0].T, v_w_down[0]])
    r_up, r_down = tuple(a.T[None] for a in big[0]), tuple(a[None] for a in big[1])

    dmod = jnp.concatenate([dmod_m, dmod_f], axis=0)
    names = ["b_ada", "pre_mix_g", "post_mix_g", "sgu_norm_g", "w_spatial", "b_spatial", "w_pool", "pool_scale",
             "pre_ffn_g", "post_ffn_g", "conv_w", "conv_b"]
    partials = [dmod.reshape(1, N_MOD * D), d_pre_mix, d_post_mix, d_sgn, d_wsp, d_bsp_t.T, d_wpool, d_ps,
                d_pre_ffn, d_post_ffn, d_cw8, d_cb8.reshape(1, 2 * FF), loss_lanes]
    small_w = [b_ada, pre_mix_g, post_mix_g, sgu_norm_g[0], w_spatial[0], b_spatial[0], w_pool[0], pool_scale,
               pre_ffn_g, post_ffn_g, conv_w[0], conv_b]
    small_m = [m_b_ada, m_pre_mix_g, m_post_mix_g, m_sgu_norm_g[0], m_w_spatial[0], m_b_spatial[0], m_w_pool[0],
               m_pool_scale, m_pre_ffn_g, m_post_ffn_g, m_conv_w[0], m_conv_b]
    small_v = [v_b_ada, v_pre_mix_g, v_post_mix_g, v_sgu_norm_g[0], v_w_spatial[0], v_b_spatial[0], v_w_pool[0],
               v_pool_scale, v_pre_ffn_g, v_post_ffn_g, v_conv_w[0], v_conv_b]
    g_mix, sums, dm2d = _tail_exchange([gw_in, gw_out], partials, [nm == "conv_w" for nm in names] + [False],
                                       [int(nm == "w_pool") for nm in names] + [0],
                                       dmod.reshape(N_DEV, 1, MOD_COLS))
    small, loss11 = _small_update(
        sums[:-1] + g_mix, small_w + [w_in_t, w_out[0]], small_m + [m_w_in[0].T, m_w_out[0]],
        small_v + [v_w_in[0].T, v_w_out[0]], scx, dm2d, w_ada[0], m_w_ada[0], v_w_ada[0], sums[-1])
    loss = loss11.reshape(())
    lead = {"sgu_norm_g", "w_spatial", "b_spatial", "w_pool", "conv_w", "w_in", "w_out", "w_ada"}
    res = {nm: tuple((a.T if nm == "w_in" else a)[None] if nm in lead else a for a in four)
           for nm, four in zip(names + ["w_in", "w_out", "w_ada"], small)}
    res.update(w_up=r_up, w_down=r_down)

    order = ["w_ada", "b_ada", "pre_mix_g", "post_mix_g", "w_in", "sgu_norm_g", "w_spatial", "b_spatial", "w_pool",
             "pool_scale", "w_out", "pre_ffn_g", "post_ffn_g", "w_up", "conv_w", "conv_b", "w_down"]
    return (loss, grad_x[None], *[res[nm][0] for nm in order], *[res[nm][1] for nm in order],
            *[res[nm][2] for nm in order], *[res[nm][3] for nm in order])
```

```python
import functools
import math

import jax
import jax.numpy as jnp
from jax import lax
from jax.experimental import pallas as pl
from jax.experimental.pallas import tpu as pltpu

F32 = jnp.float32
BF16 = jnp.bfloat16
MESH = pl.DeviceIdType.MESH

EPS = 1e-6
D = 1024
HEAD = 128
N_HEAD = 4
A_WIDTH = 512
B_WIDTH = 512
IN_WIDTH = 1536
WINDOWS = (2, 4, 8, 16)
CHUNK = 64
FF = 2816
N_DEV = 8
FF_CHUNK = 704
N_MOD = 6
MOD_COLS = 768

ADAM_LR = 0.001
ADAM_B1 = 0.9
ADAM_B2 = 0.999
ADAM_EPS = 1e-08
ADAM_WD = 0.01
ADAM_STEP = 10

VMEM_LIMIT_V7X = 62 * 1024 * 1024
HALO = 8
POOL_HALO = 16

VMEM_SPEC = pl.BlockSpec(memory_space=pltpu.VMEM)
ANY_SPEC = pl.BlockSpec(memory_space=pl.ANY)


def _bf(x):
    return x.astype(BF16)


def _mm(a, b):
    return jnp.dot(a, b, preferred_element_type=F32)


def _mm_nt(a, b):
    return lax.dot_general(a, b, (((1,), (1,)), ((), ())), preferred_element_type=F32)


def _mm_tn(a, b):
    return lax.dot_general(a, b, (((0,), (0,)), ((), ())), preferred_element_type=F32)


def _rstd(x):
    return lax.rsqrt(jnp.mean(x * x, axis=-1, keepdims=True) + EPS)


def _sum0(x):
    return jnp.sum(x, axis=0, keepdims=True)


def _rowmean(x):
    return jnp.mean(x, axis=-1, keepdims=True)


_GELU_K = math.sqrt(2.0 / math.pi)


def _gelu_and_grad(x):
    x2 = x * x
    th = jnp.tanh(_GELU_K * (x + 0.044715 * (x * x2)))
    cdf = 0.5 * th + 0.5
    grad = cdf + x * (1.0 - th * th) * ((0.5 * _GELU_K) + (1.5 * 0.044715 * _GELU_K) * x2)
    return x * cdf, grad


def _gelu(x):
    return x * (0.5 * (1.0 + jnp.tanh(_GELU_K * (x + 0.044715 * (x * x * x)))))


def _sigmoid(x):
    return 0.5 * jnp.tanh(0.5 * x) + 0.5


def _sgu_mask():
    ri = lax.broadcasted_iota(jnp.int32, (HEAD, HEAD), 0)
    ci = lax.broadcasted_iota(jnp.int32, (HEAD, HEAD), 1)
    return (ci // CHUNK) <= (ri // CHUNK)


def _window_sum(ext, w, trailing):
    n = ext.shape[0]
    s, k = ext, 1
    while k < w:
        s = s + pltpu.roll(s, k if trailing else n - k, 0)
        k *= 2
    return s


def _inv_count(row0, n, w):
    t = row0 + lax.broadcasted_iota(jnp.int32, (n, 1), 0)
    return 1.0 / jnp.minimum(t + 1, w).astype(F32)


def _shift_down(v, before, k):
    rows = lax.broadcasted_iota(jnp.int32, before.shape, 0)
    r = pltpu.roll(v, k, 0)
    top = jnp.where(rows < k, pltpu.roll(before, k, 0), r[0:HALO])
    return jnp.concatenate([top, r[HALO:]], axis=0)


def _shift_up(v, after, k):
    n = v.shape[0]
    rows = lax.broadcasted_iota(jnp.int32, after.shape, 0)
    r = pltpu.roll(v, n - k, 0)
    bottom = jnp.where(rows >= HALO - k, pltpu.roll(after, HALO - k, 0), r[n - HALO:])
    return jnp.concatenate([r[:n - HALO], bottom], axis=0)


def _adamw(w, g, m, v):
    m = ADAM_B1 * m + (1.0 - ADAM_B1) * g
    v = ADAM_B2 * v + (1.0 - ADAM_B2) * (g * g)
    m_hat = m / (1.0 - ADAM_B1 ** ADAM_STEP)
    v_hat = v / (1.0 - ADAM_B2 ** ADAM_STEP)
    delta = -ADAM_LR * (m_hat / (jnp.sqrt(v_hat) + ADAM_EPS) + ADAM_WD * w)
    return delta, m, v


def _coords():
    return lax.axis_index("x"), lax.axis_index("y"), lax.axis_index("c")


def _peer(k):
    x, y, c = _coords()
    return (x ^ ((k >> 2) & 1), y ^ ((k >> 1) & 1), c ^ (k & 1))


def _my_index():
    x, y, c = _coords()
    return 4 * x + 2 * y + c


def _adaln_modulation(c_ref, w_ref, b_ref, mod_ref, scx_ref, scbuf, stage, recv, send_sems, recv_sems):
    me = _my_index()
    cv = c_ref[...]
    scbuf[0] = cv * _sigmoid(cv)
    first = [
        pltpu.make_async_remote_copy(scbuf.at[0], scbuf.at[k], send_sems.at[0, k], recv_sems.at[0, k],
                                     device_id=_peer(k), device_id_type=MESH)
        for k in range(1, N_DEV)
    ]
    for cp in first:
        cp.start()
    for cp in first:
        cp.wait()
    scx_ref[...] = jnp.zeros(scx_ref.shape, F32)
    for k in range(N_DEV):
        scx_ref[k:k + 1, :] = scbuf[k]
    prod = _mm(_bf(scx_ref[...]), _bf(w_ref[...]))
    for k in range(N_DEV):
        stage[k] = prod[k:k + 1, :] + b_ref[me]
    second = [
        pltpu.make_async_remote_copy(stage.at[k], recv.at[k], send_sems.at[1, k], recv_sems.at[1, k],
                                     device_id=_peer(k), device_id_type=MESH)
        for k in range(1, N_DEV)
    ]
    for cp in second:
        cp.start()
    mod_ref[me] = stage[0]
    for cp in second:
        cp.wait()
    for k in range(1, N_DEV):
        mod_ref[me ^ k] = recv[k]


class _GatherSteps:
    def __init__(self, ins, outs, stages, send_sems, recv_sems, local_sems):
        self.ins, self.outs, self.stages = ins, outs, stages
        self.send_sems, self.recv_sems, self.local_sems = send_sems, recv_sems, local_sems
        x, y, c = _coords()
        self.c = c
        self.me, self.sibling = (x, y, c), (x, y, 1 - c)
        self.chips = [(1 - x, y), (x, 1 - y), (1 - x, 1 - y)]

    def _copy(self, a, k, block, to, from_stage=False):
        dst = self.outs[a].at[4 * block[0] + 2 * block[1] + block[2]]
        return pltpu.make_async_remote_copy(self.stages[a] if from_stage else dst, dst, self.send_sems.at[a, k],
                                            self.recv_sems.at[a, k], device_id=to, device_id_type=MESH)

    def _local(self, a):
        me = self.me
        return pltpu.make_async_copy(self.stages[a], self.outs[a].at[4 * me[0] + 2 * me[1] + me[2]],
                                     self.local_sems.at[a])

    def _first(self, a):
        cps = [self._copy(a, 0, self.me, self.sibling, from_stage=True)]
        return cps + [self._copy(a, 1 + j, self.me, (*chip, self.c), from_stage=True)
                      for j, chip in enumerate(self.chips)]

    def _passed(self, a, j):
        return self._copy(a, 4 + j, (*self.chips[j], self.c), self.sibling)

    def start(self):
        for a in range(len(self.ins)):
            block = self.ins[a][...]
            if block.shape != self.stages[a].shape:
                block = block.T
            self.stages[a][...] = block.astype(self.stages[a].dtype)
            self._local(a).start()
            for cp in self._first(a):
                cp.start()

    def forward(self):
        for a in range(len(self.ins)):
            for j, chip in enumerate(self.chips):
                self._copy(a, 1 + j, (*chip, self.c), self.me).wait_recv()
                self._passed(a, j).start()

    def finish(self):
        for a in range(len(self.ins)):
            self._copy(a, 0, self.sibling, self.me).wait_recv()
            for j, chip in enumerate(self.chips):
                self._copy(a, 4 + j, (*chip, 1 - self.c), self.me).wait_recv()
            for cp in self._first(a) + [self._passed(a, j) for j in range(3)]:
                cp.wait_send()
            self._local(a).wait()


def _gather_scratch(shapes, out_dtypes):
    n = len(shapes)
    return ([pltpu.VMEM(s, dt) for s, dt in zip(shapes, out_dtypes)]
            + [pltpu.SemaphoreType.DMA((n, 7)), pltpu.SemaphoreType.DMA((n, 7)), pltpu.SemaphoreType.DMA((n,))])


def _gather_out_shapes(shapes, out_dtypes):
    return tuple(jax.ShapeDtypeStruct((N_DEV, *s), dt) for s, dt in zip(shapes, out_dtypes))


def _prologue(c_row, w_ada, b_ada3, shards, out_dtypes):
    n = len(shards)

    def body(*refs):
        c_ref, w_ref, b_ref = refs[:3]
        mod_ref, scx_ref = refs[3 + n:5 + n]
        gather = _GatherSteps(refs[3:3 + n], refs[5 + n:5 + 2 * n], refs[5 + 2 * n:5 + 3 * n],
                              *refs[5 + 3 * n:8 + 3 * n])
        gather.start()
        _adaln_modulation(c_ref, w_ref, b_ref, mod_ref, scx_ref, *refs[8 + 3 * n:])
        gather.forward()
        gather.finish()

    outs = pl.pallas_call(
        body,
        name="prologue",
        out_shape=(jax.ShapeDtypeStruct((N_DEV, 1, MOD_COLS), F32), jax.ShapeDtypeStruct((2 * N_DEV, D), F32))
        + _gather_out_shapes([s.shape for s in shards], out_dtypes),
        in_specs=[VMEM_SPEC] * (3 + n),
        out_specs=(VMEM_SPEC, VMEM_SPEC) + (ANY_SPEC,) * n,
        scratch_shapes=_gather_scratch([s.shape for s in shards], out_dtypes) + [
            pltpu.VMEM((N_DEV, 1, D), F32),
            pltpu.VMEM((N_DEV, 1, MOD_COLS), F32),
            pltpu.VMEM((N_DEV, 1, MOD_COLS), F32),
            pltpu.SemaphoreType.DMA((2, N_DEV)),
            pltpu.SemaphoreType.DMA((2, N_DEV)),
        ],
        compiler_params=pltpu.CompilerParams(vmem_limit_bytes=VMEM_LIMIT_V7X),
    )(c_row, w_ada, b_ada3, *shards)
    return outs[0], outs[1], outs[2:]


class _ChipExchangeSteps:
    FLIPS = ((1, 0), (0, 1), (1, 1))

    def __init__(self, srcs, dsts, send_sems, recv_sems):
        self.srcs, self.dsts, self.send_sems, self.recv_sems = srcs, dsts, send_sems, recv_sems

    def _copies(self):
        x, y, c = _coords()
        out = []
        for a in range(len(self.srcs)):
            for j, (fx, fy) in enumerate(self.FLIPS):
                k = 3 * a + j
                out.append(pltpu.make_async_remote_copy(
                    self.srcs[a].at[x ^ fx, y ^ fy], self.dsts[a].at[j], self.send_sems.at[k], self.recv_sems.at[k],
                    device_id=(x ^ fx, y ^ fy, c), device_id_type=MESH))
        return out

    def start(self):
        for cp in self._copies():
            cp.start()

    def finish(self):
        for cp in self._copies():
            cp.wait()


def _mix_fwd(x, mod, g_pre, g_post, w_in_t, sgn, w_sp, b_sp_t, w_pool, p_scale, w_out_b, ts, shards, shard_shapes,
             shard_dtypes):
    t_len = x.shape[0]
    nt, nb = t_len // ts, ts // HEAD
    ns = len(shards)

    def body(*refs):
        (x_ref, mod_ref, g1_ref, g2_ref, win_ref, sgn_ref, ws_ref, bst_ref, wp_ref, ps_ref, wout_ref) = refs[:11]
        x1_ref, proj_ref, mixed_ref = refs[11 + ns:14 + ns]
        pbuf, cat = refs[14 + 2 * ns:16 + 2 * ns]
        gather = _GatherSteps(refs[11:11 + ns], refs[14 + ns:14 + 2 * ns], refs[16 + 2 * ns:16 + 3 * ns],
                              *refs[16 + 3 * ns:])
        i = pl.program_id(0)

        @pl.when(i == 0)
        def _():
            pbuf[0:POOL_HALO, :] = jnp.zeros((POOL_HALO, B_WIDTH), F32)
            gather.start()

        @pl.when(i == (5 * nt) // 8)
        def _():
            gather.forward()

        xv = x_ref[...]
        sh, sc, gm = mod_ref[0:1, :], mod_ref[1:2, :], mod_ref[2:3, :]
        h = (xv * _rstd(xv) * g1_ref[...]) * (1.0 + sc) + sh
        proj_ref[...] = _mm_nt(_bf(h), win_ref[...])
        pbuf[POOL_HALO:POOL_HALO + ts, :] = proj_ref[:, 2 * A_WIDTH:]
        smask = _sgu_mask()
        for hd in range(N_HEAD):
            u = _gelu(proj_ref[:, hd * HEAD:(hd + 1) * HEAD])
            v = _gelu(proj_ref[:, A_WIDTH + hd * HEAD:A_WIDTH + (hd + 1) * HEAD])
            vn = _bf(v * _rstd(v) * sgn_ref[hd:hd + 1, :])
            wm = _bf(jnp.where(smask, ws_ref[hd], 0.0))
            bias = bst_ref[:, hd:hd + 1]
            for b in range(nb):
                rows = slice(b * HEAD, (b + 1) * HEAD)
                z = _mm(wm, vn[rows]) + bias
                cat[rows, hd * HEAD:(hd + 1) * HEAD] = _bf(u[rows] * z)
        for g, w in enumerate(WINDOWS):
            cols = slice(g * HEAD, (g + 1) * HEAD)
            ext = pbuf[:, cols]
            pooled = _window_sum(ext, w, True)[POOL_HALO:] * _inv_count(i * ts, ts, w) - ext[POOL_HALO:]
            cat[:, A_WIDTH + g * HEAD:A_WIDTH + (g + 1) * HEAD] = _bf(_mm(_bf(pooled), _bf(wp_ref[g])) * ps_ref[:, cols])
        pbuf[0:POOL_HALO, :] = pbuf[ts:ts + POOL_HALO, :]
        mixed = _mm(cat[...], wout_ref[...])
        mixed_ref[...] = mixed
        x1_ref[...] = xv + gm * (mixed * _rstd(mixed) * g2_ref[...])

        @pl.when(i == nt - 1)
        def _():
            gather.finish()

    tile = lambda wid: pl.BlockSpec((ts, wid), lambda i: (i, 0))
    outs = pl.pallas_call(
        body,
        name="mix_fwd",
        grid=(nt,),
        out_shape=(jax.ShapeDtypeStruct((t_len, D), F32), jax.ShapeDtypeStruct((t_len, IN_WIDTH), F32),
                   jax.ShapeDtypeStruct((t_len, D), F32)) + _gather_out_shapes(shard_shapes, shard_dtypes),
        in_specs=[tile(D)] + [VMEM_SPEC] * (10 + ns),
        out_specs=(tile(D), tile(IN_WIDTH), tile(D)) + (ANY_SPEC,) * ns,
        scratch_shapes=[pltpu.VMEM((POOL_HALO + ts, B_WIDTH), F32), pltpu.VMEM((ts, D), BF16)]
        + _gather_scratch(shard_shapes, shard_dtypes),
        compiler_params=pltpu.CompilerParams(dimension_semantics=("arbitrary",), vmem_limit_bytes=VMEM_LIMIT_V7X),
    )(x, mod, g_pre, g_post, w_in_t, sgn, w_sp, b_sp_t, w_pool, p_scale, w_out_b, *shards)
    return outs[:3], outs[3:]


def _ffn_fwd(x1, target, mod, g_pre, g_post, w_up_b, conv_w8, conv_b8, w_down_b, ts):
    t_len = x1.shape[0]
    nt = t_len // ts

    def body(x1_ref, tgt_ref, mod_ref, g3_ref, g4_ref, wup_ref, cw_ref, cb_ref, wdown_ref,
             up_ref, f_ref, dx2_ref, loss_ref, ucarry):
        i = pl.program_id(0)

        @pl.when(i == 0)
        def _():
            ucarry[...] = jnp.zeros(ucarry.shape, F32)
            loss_ref[...] = jnp.zeros(loss_ref.shape, F32)

        x1v = x1_ref[...]
        sh, sc, gf = mod_ref[3:4, :], mod_ref[4:5, :], mod_ref[5:6, :]
        h2 = _bf((x1v * _rstd(x1v) * g3_ref[...]) * (1.0 + sc) + sh)
        half = N_DEV // 2

        def up_pair(j):
            return [_mm(h2, wup_ref[jj]) for jj in (j, j + half)]

        f = jnp.zeros((ts, D), F32)
        ups = up_pair(0)
        for j in range(half):
            nxt = up_pair(j + 1) if j + 1 < half else None
            ys = []
            for up, jj in zip(ups, (j, j + half)):
                up_ref[jj] = up
                before = ucarry[jj]
                ucarry[jj] = up[ts - HALO:, :]
                cw = cw_ref[jj]
                ys.append(cb_ref[jj:jj + 1, :] + _shift_down(up, before, 2) * cw[0:1, :]
                          + _shift_down(up, before, 1) * cw[1:2, :] + up * cw[2:3, :])
            gate, val = ys
            act = gate * _sigmoid(gate) * val
            f = f + _mm(_bf(act), wdown_ref[j * FF_CHUNK:(j + 1) * FF_CHUNK, :])
            ups = nxt
        f_ref[...] = f
        x2 = x1v + gf * (f * _rstd(f) * g4_ref[...])
        err = x2 - tgt_ref[...]
        loss_ref[...] += _sum0(err * err)
        dx2_ref[...] = err * (1.0 / D)

    tile = pl.BlockSpec((ts, D), lambda i: (i, 0))
    return pl.pallas_call(
        body,
        name="ffn_fwd",
        grid=(nt,),
        out_shape=(jax.ShapeDtypeStruct((N_DEV, t_len, FF_CHUNK), F32), jax.ShapeDtypeStruct((t_len, D), F32),
                   jax.ShapeDtypeStruct((t_len, D), F32), jax.ShapeDtypeStruct((1, D), F32)),
        in_specs=[tile, tile] + [VMEM_SPEC] * 7,
        out_specs=(pl.BlockSpec((N_DEV, ts, FF_CHUNK), lambda i: (0, i, 0)), tile, tile,
                   pl.BlockSpec((1, D), lambda i: (0, 0))),
        scratch_shapes=[pltpu.VMEM((N_DEV, HALO, FF_CHUNK), F32)],
        compiler_params=pltpu.CompilerParams(dimension_semantics=("arbitrary",), vmem_limit_bytes=VMEM_LIMIT_V7X),
    )(x1, target, mod, g_pre, g_post, w_up_b, conv_w8, conv_b8, w_down_b)


def _ffn_bwd(dx2, f, x1, up, mod, g_pre, g_post, w_up_b, conv_w8, conv_b8, w_down_b, ts):
    t_len = x1.shape[0]
    nt = t_len // ts
    half = N_DEV // 2

    def body(dx2_ref, f_ref, x1_ref, up_ref, halo_ref, mod_ref, g3_ref, g4_ref, wup_ref, cw_ref, cb_ref, wdown_ref,
             dx1_ref, dup_ref, act_ref, df_ref, h2_ref, dmod_ref, dg3_ref, dg4_ref, dcb_ref, dcw_ref,
             dycarry, dh2acc):
        i = pl.program_id(0)
        r = nt - 1 - i

        @pl.when(i == 0)
        def _():
            for ref in (dmod_ref, dg3_ref, dg4_ref, dcb_ref, dcw_ref, dycarry):
                ref[...] = jnp.zeros(ref.shape, F32)

        dx2v, fv, x1v = dx2_ref[...], f_ref[...], x1_ref[...]
        sh, sc, gf = mod_ref[3:4, :], mod_ref[4:5, :], mod_ref[5:6, :]
        g3, g4 = g3_ref[...], g4_ref[...]
        rstd4 = _rstd(fv)
        fh = fv * rstd4
        dmod_ref[2:3, :] += _sum0(dx2v * (fh * g4))
        dr = dx2v * gf
        dg4_ref[...] += _sum0(dr * fh)
        dfh = dr * g4
        dfb = _bf(rstd4 * (dfh - fh * _rowmean(dfh * fh)))
        df_ref[...] = dfb
        rstd3 = _rstd(x1v)
        xh = x1v * rstd3
        n3 = xh * g3
        h2_ref[...] = _bf(n3 * (1.0 + sc) + sh)
        dh2acc[...] = jnp.zeros((ts, D), F32)
        keep = jnp.where(r > 0, 1.0, 0.0).astype(F32)

        def dact_of(j):
            return _mm_nt(dfb, wdown_ref[j * FF_CHUNK:(j + 1) * FF_CHUNK, :])

        dact_next = dact_of(0)
        for j in range(half):
            dact = dact_next
            if j + 1 < half:
                dact_next = dact_of(j + 1)
            ys = []
            for jj in (j, j + half):
                before = halo_ref[jj] * keep
                upc = up_ref[jj]
                cw = cw_ref[jj]
                ys.append(cb_ref[jj:jj + 1, :] + _shift_down(upc, before, 2) * cw[0:1, :]
                          + _shift_down(upc, before, 1) * cw[1:2, :] + upc * cw[2:3, :])
            gate, val = ys
            sg = _sigmoid(gate)
            silu = gate * sg
            act_ref[j] = _bf(silu * val)
            dys = (dact * val * (sg + silu * (1.0 - sg)), dact * silu)
            for q, jj in enumerate((j, j + half)):
                dy = dys[q]
                cw = cw_ref[jj]
                dcb_ref[jj:jj + 1, :] += _sum0(dy)
                after = dycarry[jj]
                dycarry[jj] = dy[0:HALO, :]
                dy1, dy2 = _shift_up(dy, after, 1), _shift_up(dy, after, 2)
                upc = up_ref[jj]
                dcw_ref[jj, 0:1, :] += _sum0(dy2 * upc)
                dcw_ref[jj, 1:2, :] += _sum0(dy1 * upc)
                dcw_ref[jj, 2:3, :] += _sum0(dy * upc)
                dup = _bf(dy * cw[2:3, :] + dy1 * cw[1:2, :] + dy2 * cw[0:1, :])
                dup_ref[jj] = dup
                dh2acc[...] += _mm_nt(dup, wup_ref[jj])
        dh2 = dh2acc[...]
        dmod_ref[0:1, :] += _sum0(dh2)
        dmod_ref[1:2, :] += _sum0(dh2 * n3)
        dn3 = dh2 * (1.0 + sc)
        dg3_ref[...] += _sum0(dn3 * xh)
        dxh = dn3 * g3
        dx1_ref[...] = dx2v + rstd3 * (dxh - xh * _rowmean(dxh * xh))

    tile = pl.BlockSpec((ts, D), lambda i: (nt - 1 - i, 0))
    chunked = lambda n: pl.BlockSpec((n, ts, FF_CHUNK), lambda i: (0, nt - 1 - i, 0))
    halo = pl.BlockSpec((N_DEV, HALO, FF_CHUNK), lambda i: (0, jnp.maximum((nt - 1 - i) * (ts // HALO) - 1, 0), 0))
    const = lambda *shape: pl.BlockSpec(shape, lambda i: (0,) * len(shape))
    return pl.pallas_call(
        body,
        name="ffn_bwd",
        grid=(nt,),
        out_shape=(jax.ShapeDtypeStruct((t_len, D), F32), jax.ShapeDtypeStruct((N_DEV, t_len, FF_CHUNK), BF16),
                   jax.ShapeDtypeStruct((half, t_len, FF_CHUNK), BF16), jax.ShapeDtypeStruct((t_len, D), BF16),
                   jax.ShapeDtypeStruct((t_len, D), BF16), jax.ShapeDtypeStruct((3, D), F32),
                   jax.ShapeDtypeStruct((1, D), F32), jax.ShapeDtypeStruct((1, D), F32),
                   jax.ShapeDtypeStruct((N_DEV, FF_CHUNK), F32), jax.ShapeDtypeStruct((N_DEV, 3, FF_CHUNK), F32)),
        in_specs=[tile, tile, tile, chunked(N_DEV), halo] + [VMEM_SPEC] * 7,
        out_specs=(tile, chunked(N_DEV), chunked(half), tile, tile, const(3, D), const(1, D), const(1, D),
                   const(N_DEV, FF_CHUNK), const(N_DEV, 3, FF_CHUNK)),
        scratch_shapes=[pltpu.VMEM((N_DEV, HALO, FF_CHUNK), F32), pltpu.VMEM((ts, D), F32)],
        compiler_params=pltpu.CompilerParams(dimension_semantics=("arbitrary",), vmem_limit_bytes=VMEM_LIMIT_V7X),
    )(dx2, f, x1, up, up, mod, g_pre, g_post, w_up_b, conv_w8, conv_b8, w_down_b)


def _wgrad_up(h2, dup, ts):
    t_len = h2.shape[0]
    nt, half = t_len // ts, N_DEV // 2

    def body(h2_ref, dup_ref, out_ref):
        @pl.when(pl.program_id(1) == 0)
        def _():
            out_ref[...] = jnp.zeros(out_ref.shape, F32)

        for q in range(half):
            out_ref[q] += _mm_tn(dup_ref[q], h2_ref[...])

    return pl.pallas_call(
        body,
        name="wgrad_up",
        grid=(2, nt),
        out_shape=jax.ShapeDtypeStruct((N_DEV, FF_CHUNK, D), F32),
        in_specs=[pl.BlockSpec((ts, D), lambda g, t: (t, 0)), pl.BlockSpec((half, ts, FF_CHUNK), lambda g, t: (g, t, 0))],
        out_specs=pl.BlockSpec((half, FF_CHUNK, D), lambda g, t: (g, 0, 0)),
        compiler_params=pltpu.CompilerParams(dimension_semantics=("arbitrary", "arbitrary"),
                                             vmem_limit_bytes=VMEM_LIMIT_V7X),
    )(h2, dup)


def _sibling_swap_copies(srcs, dsts, send_sems, recv_sems):
    x, y, c = _coords()
    return [
        pltpu.make_async_remote_copy(srcs[a].at[xs, ys, 1 - c], dsts[a].at[xs, ys], send_sems.at[a, 2 * xs + ys],
                                     recv_sems.at[a, 2 * xs + ys], device_id=(x, y, 1 - c), device_id_type=MESH)
        for a in range(len(srcs)) for xs in range(2) for ys in range(2)
    ]


def _wgrad_down(act, df, ts, swap_src):
    t_len = df.shape[0]
    nt, half = t_len // ts, N_DEV // 2

    def body(act_ref, df_ref, src_ref, out_ref, dst_ref, send_sems, recv_sems):
        t = pl.program_id(0)

        @pl.when(t == 0)
        def _():
            for cp in _sibling_swap_copies([src_ref], [dst_ref], send_sems, recv_sems):
                cp.start()
            out_ref[...] = jnp.zeros(out_ref.shape, F32)

        for q in range(half):
            out_ref[q] += _mm_tn(act_ref[q], df_ref[...])

        @pl.when(t == nt - 1)
        def _():
            for cp in _sibling_swap_copies([src_ref], [dst_ref], send_sems, recv_sems):
                cp.wait()

    return pl.pallas_call(
        body,
        name="wgrad_down",
        grid=(nt,),
        out_shape=(jax.ShapeDtypeStruct((half, FF_CHUNK, D), F32), jax.ShapeDtypeStruct(swap_src.shape[1:], F32)),
        in_specs=[pl.BlockSpec((half, ts, FF_CHUNK), lambda t: (0, t, 0)), pl.BlockSpec((ts, D), lambda t: (t, 0)),
                  ANY_SPEC],
        out_specs=(pl.BlockSpec((half, FF_CHUNK, D), lambda t: (0, 0, 0)), ANY_SPEC),
        scratch_shapes=[pltpu.SemaphoreType.DMA((1, 4)), pltpu.SemaphoreType.DMA((1, 4))],
        compiler_params=pltpu.CompilerParams(dimension_semantics=("arbitrary",), vmem_limit_bytes=VMEM_LIMIT_V7X),
    )(act, df, swap_src)


def _mix_bwd(dx1, x, proj, mixed, mod, g_pre, g_post, w_in_t, sgn, w_sp, b_sp_t, w_pool, p_scale, w_out_b, ts, rs_srcs):
    t_len = x.shape[0]
    nt, nb = t_len // ts, ts // HEAD
    nr = len(rs_srcs)

    def body(*refs):
        (dx1_ref, x_ref, proj_ref, halo_ref, mixed_ref, mod_ref, g1_ref, g2_ref, win_ref, sgn_ref, ws_ref,
         bst_ref, wp_ref, ps_ref, wout_ref) = refs[:15]
        (gx_ref, dwin_ref, dwout_ref, dmod_ref, dg1_ref, dg2_ref, dsgn_ref, dws_ref, dbst_ref, dwp_ref,
         dps_ref) = refs[15 + nr:26 + nr]
        pbuf, dwsbuf, cat, dproj, dcat = refs[26 + 2 * nr:31 + 2 * nr]
        exchange = _ChipExchangeSteps(refs[15:15 + nr], refs[26 + nr:26 + 2 * nr], *refs[31 + 2 * nr:])
        i = pl.program_id(0)
        r = nt - 1 - i

        @pl.when(i == 0)
        def _():
            exchange.start()
            for ref in (dwin_ref, dwout_ref, dmod_ref, dg1_ref, dg2_ref, dsgn_ref, dws_ref, dbst_ref, dwp_ref, dps_ref):
                ref[...] = jnp.zeros(ref.shape, F32)
            dwsbuf[ts:ts + POOL_HALO, :] = jnp.zeros((POOL_HALO, B_WIDTH), F32)

        xv, dx1v, mixed = x_ref[...], dx1_ref[...], mixed_ref[...]
        sh, sc, gm = mod_ref[0:1, :], mod_ref[1:2, :], mod_ref[2:3, :]
        g1, g2 = g1_ref[...], g2_ref[...]
        rstd2 = _rstd(mixed)
        mh = mixed * rstd2
        dmod_ref[2:3, :] += _sum0(dx1v * (mh * g2))
        dr = dx1v * gm
        dg2_ref[...] += _sum0(dr * mh)
        dmh = dr * g2
        dmb = _bf(rstd2 * (dmh - mh * _rowmean(dmh * mh)))
        dcat[...] = _mm_nt(dmb, wout_ref[...])
        smask = _sgu_mask()
        for hd in range(N_HEAD):
            ucols = slice(hd * HEAD, (hd + 1) * HEAD)
            vcols = slice(A_WIDTH + hd * HEAD, A_WIDTH + (hd + 1) * HEAD)
            u, du_dp = _gelu_and_grad(proj_ref[:, ucols])
            v, dv_dp = _gelu_and_grad(proj_ref[:, vcols])
            rs = _rstd(v)
            vhat = v * rs
            gn = sgn_ref[hd:hd + 1, :]
            vn = _bf(vhat * gn)
            wm = _bf(jnp.where(smask, ws_ref[hd], 0.0))
            bias = bst_ref[:, hd:hd + 1]
            dzsum = jnp.zeros((HEAD, HEAD), F32)
            dwm = jnp.zeros((HEAD, HEAD), F32)
            dvn_parts = []
            for b in range(nb):
                rows = slice(b * HEAD, (b + 1) * HEAD)
                z = _mm(wm, vn[rows]) + bias
                da = dcat[rows, ucols]
                cat[rows, ucols] = _bf(u[rows] * z)
                dz = da * u[rows]
                dzsum = dzsum + dz
                dzb = _bf(dz)
                dwm = dwm + _mm_nt(dzb, vn[rows])
                dvn_parts.append(_mm_tn(wm, dzb))
                dproj[rows, ucols] = _bf((da * z) * du_dp[rows])
            dvn = jnp.concatenate(dvn_parts, axis=0)
            dsgn_ref[hd:hd + 1, :] += _sum0(dvn * vhat)
            dvh = dvn * gn
            dproj[:, vcols] = _bf((rs * (dvh - vhat * _rowmean(dvh * vhat))) * dv_dp)
            dws_ref[hd] += jnp.where(smask, dwm, 0.0)
            dbst_ref[:, hd:hd + 1] += jnp.sum(dzsum, axis=1, keepdims=True)
        keep = jnp.where(r > 0, 1.0, 0.0).astype(F32)
        pbuf[0:POOL_HALO, :] = halo_ref[...] * keep
        pbuf[POOL_HALO:POOL_HALO + ts, :] = proj_ref[:, 2 * A_WIDTH:]
        for g, w in enumerate(WINDOWS):
            cols = slice(g * HEAD, (g + 1) * HEAD)
            ccols = slice(A_WIDTH + g * HEAD, A_WIDTH + (g + 1) * HEAD)
            pcols = slice(2 * A_WIDTH + g * HEAD, 2 * A_WIDTH + (g + 1) * HEAD)
            wpg = _bf(wp_ref[g])
            psg = ps_ref[:, cols]
            ext = pbuf[:, cols]
            inv = _inv_count(r * ts, ts, w)
            pb = _bf(_window_sum(ext, w, True)[POOL_HALO:] * inv - ext[POOL_HALO:])
            yb = _mm(pb, wpg)
            dob = dcat[:, ccols]
            cat[:, ccols] = _bf(yb * psg)
            dps_ref[:, cols] += _sum0(dob * yb)
            dyb = _bf(dob * psg)
            dwp_ref[g] += _mm_tn(pb, dyb)
            dpooled = _mm_nt(dyb, wpg)
            dwsbuf[0:ts, cols] = dpooled * inv
            dproj[:, pcols] = _bf(_window_sum(dwsbuf[:, cols], w, False)[0:ts] - dpooled)
        dwsbuf[ts:ts + POOL_HALO, :] = dwsbuf[0:POOL_HALO, :]
        dpb = dproj[...]
        rstd1 = _rstd(xv)
        xh = xv * rstd1
        n1 = xh * g1
        dwin_ref[...] += _mm_tn(dpb, _bf(n1 * (1.0 + sc) + sh))
        dwout_ref[...] += _mm_tn(cat[...], dmb)
        dh = _mm(dpb, win_ref[...])
        dmod_ref[0:1, :] += _sum0(dh)
        dmod_ref[1:2, :] += _sum0(dh * n1)
        dn1 = dh * (1.0 + sc)
        dg1_ref[...] += _sum0(dn1 * xh)
        dxh = dn1 * g1
        gx_ref[...] = dx1v + rstd1 * (dxh - xh * _rowmean(dxh * xh))

        @pl.when(i == nt - 1)
        def _():
            exchange.finish()

    tile = lambda wid: pl.BlockSpec((ts, wid), lambda i: (nt - 1 - i, 0))
    halo = pl.BlockSpec((POOL_HALO, B_WIDTH),
                        lambda i: (jnp.maximum((nt - 1 - i) * (ts // POOL_HALO) - 1, 0), 2 * A_WIDTH // B_WIDTH))
    const = lambda *shape: pl.BlockSpec(shape, lambda i: (0,) * len(shape))
    resident = lambda *shape: pl.BlockSpec(shape, lambda i: (0,) * len(shape), pipeline_mode=pl.Buffered(1))
    outs = pl.pallas_call(
        body,
        name="mix_bwd",
        grid=(nt,),
        out_shape=(jax.ShapeDtypeStruct((t_len, D), F32), jax.ShapeDtypeStruct((IN_WIDTH, D), F32),
                   jax.ShapeDtypeStruct((D, D), F32), jax.ShapeDtypeStruct((3, D), F32),
                   jax.ShapeDtypeStruct((1, D), F32), jax.ShapeDtypeStruct((1, D), F32),
                   jax.ShapeDtypeStruct((N_HEAD, HEAD), F32), jax.ShapeDtypeStruct((N_HEAD, HEAD, HEAD), F32),
                   jax.ShapeDtypeStruct((HEAD, N_HEAD), F32), jax.ShapeDtypeStruct((N_HEAD, HEAD, HEAD), F32),
                   jax.ShapeDtypeStruct((1, B_WIDTH), F32))
        + tuple(jax.ShapeDtypeStruct((3, *s.shape[2:]), s.dtype) for s in rs_srcs),
        in_specs=[tile(D), tile(D), tile(IN_WIDTH), halo, tile(D)] + [VMEM_SPEC] * 10 + [ANY_SPEC] * nr,
        out_specs=(tile(D), resident(IN_WIDTH, D), resident(D, D), const(3, D), const(1, D), const(1, D),
                   const(N_HEAD, HEAD), const(N_HEAD, HEAD, HEAD), const(HEAD, N_HEAD), const(N_HEAD, HEAD, HEAD),
                   const(1, B_WIDTH)) + (ANY_SPEC,) * nr,
        scratch_shapes=[pltpu.VMEM((POOL_HALO + ts, B_WIDTH), F32), pltpu.VMEM((ts + POOL_HALO, B_WIDTH), F32),
                        pltpu.VMEM((ts, D), BF16), pltpu.VMEM((ts, IN_WIDTH), BF16), pltpu.VMEM((ts, D), F32),
                        pltpu.SemaphoreType.DMA((3 * nr,)), pltpu.SemaphoreType.DMA((3 * nr,))],
        compiler_params=pltpu.CompilerParams(dimension_semantics=("arbitrary",), vmem_limit_bytes=VMEM_LIMIT_V7X),
    )(dx1, x, proj, proj, mixed, mod, g_pre, g_post, w_in_t, sgn, w_sp, b_sp_t, w_pool, p_scale, w_out_b, *rs_srcs)
    return outs[:11], outs[11:]


def _pair_add(name, coords, grid, specs_a, specs_b, out_specs, out_shapes, a_arrays, b_arrays, swap_srcs):
    n, ns = len(a_arrays), len(swap_srcs)
    last = tuple(g - 1 for g in grid)

    def body(co_ref, *refs):
        ids = [pl.program_id(d) for d in range(len(grid))]
        swap = refs[2 * n:2 * n + ns], refs[4 * n + ns:4 * n + 2 * ns], *refs[4 * n + 2 * ns:]
        if ns:
            @pl.when(functools.reduce(jnp.logical_and, [i == 0 for i in ids]))
            def _():
                for cp in _sibling_swap_copies(*swap):
                    cp.start()

        for k in range(n):
            total = refs[k][...] + refs[n + k][...]
            refs[2 * n + ns + k][...] = total
            refs[3 * n + ns + k][...] = _bf(total)

        if ns:
            @pl.when(functools.reduce(jnp.logical_and, [i == e for i, e in zip(ids, last)]))
            def _():
                for cp in _sibling_swap_copies(*swap):
                    cp.wait()

    outs = pl.pallas_call(
        body,
        name=name,
        grid_spec=pltpu.PrefetchScalarGridSpec(
            num_scalar_prefetch=1, grid=grid, in_specs=specs_a + specs_b + [ANY_SPEC] * ns,
            out_specs=out_specs * 2 + [ANY_SPEC] * ns,
            scratch_shapes=[pltpu.SemaphoreType.DMA((ns, 4)), pltpu.SemaphoreType.DMA((ns, 4))] if ns else []),
        out_shape=tuple(jax.ShapeDtypeStruct(s, dt) for dt in (F32, BF16) for s in out_shapes)
        + tuple(jax.ShapeDtypeStruct(g.shape[1:], F32) for g in swap_srcs),
        compiler_params=pltpu.CompilerParams(dimension_semantics=("arbitrary",) * len(grid),
                                             vmem_limit_bytes=VMEM_LIMIT_V7X),
    )(coords, *a_arrays, *b_arrays, *swap_srcs)
    return list(outs[:n]), list(outs[n:2 * n]), list(outs[2 * n:])


def _final_add_adamw(coords, s1, r, ws, ms, vs, n_split=4):
    n = len(s1)

    def body(co_ref, *refs):
        for k in range(n):
            s_ref, r_ref, w_ref, m_ref, v_ref = (refs[q * n + k] for q in range(5))
            g_ref, d_ref, nm_ref, nv_ref = (refs[(5 + q) * n + k] for q in range(4))
            g = ((s_ref[...] + r_ref[0].astype(F32)) + r_ref[1].astype(F32)) + r_ref[2].astype(F32)
            g_ref[...] = g
            delta, m, v = _adamw(w_ref[...], g, m_ref[...], v_ref[...])
            d_ref[...] = delta
            nm_ref[...] = m
            nv_ref[...] = v

    def shard_spec(a):
        rows, cols = a.shape
        return pl.BlockSpec((rows // n_split, cols), lambda i, co: (i, 0))

    def mine_spec(a):
        rows, cols = a.shape[2:]
        return pl.BlockSpec((None, None, rows // n_split, cols), lambda i, co: (co[0], co[1], i, 0))

    def recv_spec(a):
        rows, cols = a.shape[1:]
        return pl.BlockSpec((3, rows // n_split, cols), lambda i, co: (0, i, 0))

    in_specs = ([mine_spec(a) for a in s1] + [recv_spec(a) for a in r] + [shard_spec(a) for a in ws] * 3)
    out_specs = [shard_spec(a) for a in ws] * 4
    outs = pl.pallas_call(
        body,
        name="grad_final_adamw",
        grid_spec=pltpu.PrefetchScalarGridSpec(num_scalar_prefetch=1, grid=(n_split,), in_specs=in_specs,
                                               out_specs=out_specs),
        out_shape=tuple(jax.ShapeDtypeStruct(a.shape, F32) for a in ws) * 4,
        compiler_params=pltpu.CompilerParams(dimension_semantics=("arbitrary",), vmem_limit_bytes=VMEM_LIMIT_V7X),
    )(coords, *s1, *r, *ws, *ms, *vs)
    return [tuple(outs[q * n + k] for q in range(4)) for k in range(n)]


def _sibling_add(tag, g5, r1, coords, swap_srcs=(), n_split=4):
    shapes = [g.shape[3:] for g in g5]
    spec_g = [pl.BlockSpec((None, None, None, s[0] // n_split, s[1]), lambda i, j, k, co: (i, j, co[2], k, 0))
              for s in shapes]
    spec_r = [pl.BlockSpec((None, None, s[0] // n_split, s[1]), lambda i, j, k, co: (i, j, k, 0)) for s in shapes]
    return _pair_add("grad_add_core_" + tag, coords, (2, 2, n_split), spec_g, spec_r, spec_r,
                     [(2, 2, *s) for s in shapes], g5, r1, list(swap_srcs))


def _tail_exchange(big, partials, pick_mine, y_first, dmod3):
    n, nb = len(partials), len(big)
    big_shapes = [g.shape[1:] for g in big]
    big5 = [g.reshape(2, 2, 2, *s) for g, s in zip(big, big_shapes)]
    flips = _ChipExchangeSteps.FLIPS

    def body(*refs):
        g5, p_in, dm_ref = refs[:nb], refs[nb:nb + n], refs[nb + n]
        outs = refs[nb + n + 1:2 * nb + 2 * n + 2]
        g_out, sums, dm2d = outs[:nb], outs[nb:nb + n], outs[nb + n]
        scratch = refs[2 * nb + 2 * n + 2:]
        s1, stage, chip_recv = scratch[:nb], scratch[nb:2 * nb], scratch[2 * nb:3 * nb]
        acc, rbuf = scratch[3 * nb:3 * nb + n], scratch[3 * nb + n:3 * nb + 2 * n]
        (dm_recv, send_sems, recv_sems, dm_send_sems, dm_recv_sems, sib_send, sib_recv, chip_send,
         chip_recv_sems) = scratch[3 * nb + 2 * n:]
        x, y, c = _coords()
        me = 4 * x + 2 * y + c
        sibling = (x, y, 1 - c)
        dm_copies = [
            pltpu.make_async_remote_copy(dm_ref.at[me ^ k], dm_recv.at[k], dm_send_sems.at[k], dm_recv_sems.at[k],
                                         device_id=_peer(k), device_id_type=MESH)
            for k in range(1, N_DEV)
        ]
        for cp in dm_copies:
            cp.start()
        sib_copies = _sibling_swap_copies(g5, s1, sib_send, sib_recv)
        for cp in sib_copies:
            cp.start()
        for a in range(n):
            acc[a][...] = p_in[a][...]

        def small_phase(ph, peers):
            copies = [
                pltpu.make_async_remote_copy(acc[a], rbuf[a].at[ph], send_sems.at[ph, a], recv_sems.at[ph, a],
                                             device_id=peers[y_first[a]], device_id_type=MESH)
                for a in range(n)
            ]
            for cp in copies:
                cp.start()
            for cp in copies:
                cp.wait()
            for a in range(n):
                acc[a][...] = acc[a][...] + rbuf[a][ph]

        small_phase(0, (sibling, sibling))
        for cp in sib_copies:
            cp.wait()
        for a in range(nb):
            for xs in range(2):
                for ys in range(2):
                    total = g5[a][xs, ys, c] + s1[a][xs, ys]
                    s1[a][xs, ys] = total
                    stage[a][xs, ys] = _bf(total)
        chip_copies = [
            pltpu.make_async_remote_copy(stage[a].at[x ^ fx, y ^ fy], chip_recv[a].at[j], chip_send.at[a, j],
                                         chip_recv_sems.at[a, j], device_id=(x ^ fx, y ^ fy, c), device_id_type=MESH)
            for a in range(nb) for j, (fx, fy) in enumerate(flips)
        ]
        for cp in chip_copies:
            cp.start()
        x_peer, y_peer = (1 - x, y, c), (x, 1 - y, c)
        small_phase(1, (x_peer, y_peer))
        small_phase(2, (y_peer, x_peer))
        for a in range(n):
            sums[a][...] = acc[a][me] if pick_mine[a] else acc[a][...]
        dm2d[...] = jnp.zeros(dm2d.shape, F32)
        dm2d[0:1, :] = dm_ref[me]
        for cp in dm_copies:
            cp.wait()
        for k in range(1, N_DEV):
            dm2d[k:k + 1, :] = dm_recv[k]
        for cp in chip_copies:
            cp.wait()
        for a in range(nb):
            g_out[a][...] = ((s1[a][x, y] + chip_recv[a][0].astype(F32)) + chip_recv[a][1].astype(F32)) \
                + chip_recv[a][2].astype(F32)

    out_shapes = tuple(jax.ShapeDtypeStruct(s, F32) for s in big_shapes) + tuple(
        jax.ShapeDtypeStruct(p.shape[1:] if pk else p.shape, F32) for p, pk in zip(partials, pick_mine))
    outs = pl.pallas_call(
        body,
        name="tail_exchange",
        out_shape=out_shapes + (jax.ShapeDtypeStruct((2 * N_DEV, MOD_COLS), F32),),
        in_specs=[VMEM_SPEC] * (nb + n + 1),
        out_specs=(VMEM_SPEC,) * (nb + n + 1),
        scratch_shapes=[pltpu.VMEM((2, 2, *s), F32) for s in big_shapes]
        + [pltpu.VMEM((2, 2, *s), BF16) for s in big_shapes]
        + [pltpu.VMEM((3, *s), BF16) for s in big_shapes]
        + [pltpu.VMEM(p.shape, F32) for p in partials]
        + [pltpu.VMEM((3, *p.shape), F32) for p in partials]
        + [pltpu.VMEM((N_DEV, 1, MOD_COLS), F32), pltpu.SemaphoreType.DMA((3, n)), pltpu.SemaphoreType.DMA((3, n)),
           pltpu.SemaphoreType.DMA((N_DEV,)), pltpu.SemaphoreType.DMA((N_DEV,)),
           pltpu.SemaphoreType.DMA((nb, 4)), pltpu.SemaphoreType.DMA((nb, 4)),
           pltpu.SemaphoreType.DMA((nb, 3)), pltpu.SemaphoreType.DMA((nb, 3))],
        compiler_params=pltpu.CompilerParams(vmem_limit_bytes=VMEM_LIMIT_V7X),
    )(*big5, *partials, dmod3)
    return list(outs[:nb]), list(outs[nb:nb + n]), outs[nb + n]


def _small_update(grads, ws, ms, vs, scx, dm2d, w_ada, m_ada, v_ada, loss_lanes):
    n = len(grads)

    def body(*refs):
        g_in, w_in, m_in, v_in = (refs[q * n:(q + 1) * n] for q in range(4))
        scx_ref, dm_ref, wa_ref, ma_ref, va_ref, ll_ref = refs[4 * n:4 * n + 6]
        outs = refs[4 * n + 6:]
        g_out, d_out, nm_out, nv_out = (outs[q * (n + 1):(q + 1) * (n + 1)] for q in range(4))
        loss_ref = outs[4 * (n + 1)]
        for a in range(n + 1):
            if a < n:
                g, w, m, v = g_in[a][...], w_in[a][...], m_in[a][...], v_in[a][...]
            else:
                g = _mm_tn(_bf(scx_ref[...]), _bf(dm_ref[...]))
                w, m, v = wa_ref[...], ma_ref[...], va_ref[...]
            g_out[a][...] = g
            delta, m, v = _adamw(w, g, m, v)
            d_out[a][...] = delta
            nm_out[a][...] = m
            nv_out[a][...] = v
        loss_ref[...] = jnp.sum(ll_ref[...], axis=1, keepdims=True) * (0.5 / D)

    w_shapes = tuple(jax.ShapeDtypeStruct(w.shape, F32) for w in list(ws) + [w_ada])
    outs = pl.pallas_call(
        body,
        name="small_update",
        out_shape=w_shapes * 4 + (jax.ShapeDtypeStruct((1, 1), F32),),
        in_specs=[VMEM_SPEC] * (4 * n + 6),
        out_specs=(VMEM_SPEC,) * (4 * (n + 1) + 1),
        compiler_params=pltpu.CompilerParams(vmem_limit_bytes=VMEM_LIMIT_V7X),
    )(*grads, *ws, *ms, *vs, scx, dm2d, w_ada, m_ada, v_ada, loss_lanes)
    return [tuple(outs[q * (n + 1) + k] for q in range(4)) for k in range(n + 1)], outs[4 * (n + 1)]


def kernel(x, c, w_ada, b_ada, pre_mix_g, post_mix_g, w_in, sgu_norm_g, w_spatial, b_spatial, w_pool, pool_scale, w_out, pre_ffn_g, post_ffn_g, w_up, conv_w, conv_b, w_down, loss_target, m_w_ada, m_b_ada, m_pre_mix_g, m_post_mix_g, m_w_in, m_sgu_norm_g, m_w_spatial, m_b_spatial, m_w_pool, m_pool_scale, m_w_out, m_pre_ffn_g, m_post_ffn_g, m_w_up, m_conv_w, m_conv_b, m_w_down, v_w_ada, v_b_ada, v_pre_mix_g, v_post_mix_g, v_w_in, v_sgu_norm_g, v_w_spatial, v_b_spatial, v_w_pool, v_pool_scale, v_w_out, v_pre_ffn_g, v_post_ffn_g, v_w_up, v_conv_w, v_conv_b, v_w_down):
    t_len = x.shape[1]
    ts = min(256, t_len)
    ts_mix = min(512, t_len)
    ts_w = min(1024, t_len)
    coords = jnp.stack([lax.axis_index("x"), lax.axis_index("y"), lax.axis_index("c")]).astype(jnp.int32)

    w_in_t, w_up_t = w_in[0].T, w_up[0].T
    mod3, scx, (g_in, g_out, g_down) = _prologue(c, w_ada[0], b_ada.reshape(N_DEV, 1, MOD_COLS),
                                                 [w_in_t, w_out[0], w_down[0]], [BF16, BF16, BF16])
    mod = mod3.reshape(N_MOD, D)
    w_in_tb = g_in.reshape(IN_WIDTH, D)
    w_out_b = g_out.reshape(D, D)
    conv_b8 = conv_b.reshape(N_DEV, FF_CHUNK)
    b_sp_t = b_spatial[0].T

    x2d, tgt = x[0], loss_target[0]
    (x1, proj, mixed), (g_up, g_cw) = _mix_fwd(
        x2d, mod, pre_mix_g, post_mix_g, w_in_tb, sgu_norm_g[0], w_spatial[0], b_sp_t, w_pool[0], pool_scale, w_out_b,
        ts_mix, [w_up_t, conv_w[0]], [w_up.shape[1:], conv_w.shape[1:]], [BF16, F32])
    w_down_b = g_down.reshape(FF, D)
    up, f, dx2, loss_lanes = _ffn_fwd(x1, tgt, mod, pre_ffn_g, post_ffn_g, g_up, g_cw, conv_b8, w_down_b, ts)

    (dx1, dup, act, df, h2, dmod_f, d_pre_ffn, d_post_ffn, d_cb8, d_cw8) = _ffn_bwd(
        dx2, f, x1, up, mod, pre_ffn_g, post_ffn_g, g_up, g_cw, conv_b8, w_down_b, ts)
    gw_up = _wgrad_up(h2, dup, ts_w).reshape(2, 2, 2, FF_CHUNK, D)
    gw_down, r1_up = _wgrad_down(act, df, ts_w, gw_up)
    gw_down = gw_down.reshape(2, 2, 2, FF // N_DEV, D)
    s1_up, s1_up_b, r1_down = _sibling_add("up", [gw_up], [r1_up], coords, swap_srcs=[gw_down])
    s1_down, s1_down_b, _ = _sibling_add("down", [gw_down], r1_down, coords)
    s1_ffn, s1_ffn_b = s1_up + s1_down, s1_up_b + s1_down_b
    ((grad_x, gw_in, gw_out, dmod_m, d_pre_mix, d_post_mix, d_sgn, d_wsp, d_bsp_t, d_wpool, d_ps), r_ffn) = _mix_bwd(
        dx1, x2d, proj, mixed, mod, pre_mix_g, post_mix_g, w_in_tb, sgu_norm_g[0], w_spatial[0], b_sp_t,
        w_pool[0], pool_scale, w_out_b, ts_mix, s1_ffn_b)
    gw_in = gw_in.reshape(N_DEV, IN_WIDTH // N_DEV, D)
    gw_out = gw_out.reshape(N_DEV, D // N_DEV, D)

    big = _final_add_adamw(coords, s1_ffn, list(r_ffn), [w_up_t, w_down[0]], [m_w_up[0].T, m_w_down[0]],
                           [v_w_up[0].T, v_w_down[0]])
    r_up, r_down = tuple(a.T[None] for a in big[0]), tuple(a[None] for a in big[1])

    dmod = jnp.concatenate([dmod_m, dmod_f], axis=0)
    names = ["b_ada", "pre_mix_g", "post_mix_g", "sgu_norm_g", "w_spatial", "b_spatial", "w_pool", "pool_scale",
             "pre_ffn_g", "post_ffn_g", "conv_w", "conv_b"]
    partials = [dmod.reshape(1, N_MOD * D), d_pre_mix, d_post_mix, d_sgn, d_wsp, d_bsp_t.T, d_wpool, d_ps,
                d_pre_ffn, d_post_ffn, d_cw8, d_cb8.reshape(1, 2 * FF), loss_lanes]
    small_w = [b_ada, pre_mix_g, post_mix_g, sgu_norm_g[0], w_spatial[0], b_spatial[0], w_pool[0], pool_scale,
               pre_ffn_g, post_ffn_g, conv_w[0], conv_b]
    small_m = [m_b_ada, m_pre_mix_g, m_post_mix_g, m_sgu_norm_g[0], m_w_spatial[0], m_b_spatial[0], m_w_pool[0],
               m_pool_scale, m_pre_ffn_g, m_post_ffn_g, m_conv_w[0], m_conv_b]
    small_v = [v_b_ada, v_pre_mix_g, v_post_mix_g, v_sgu_norm_g[0], v_w_spatial[0], v_b_spatial[0], v_w_pool[0],
               v_pool_scale, v_pre_ffn_g, v_post_ffn_g, v_conv_w[0], v_conv_b]
    g_mix, sums, dm2d = _tail_exchange([gw_in, gw_out], partials, [nm == "conv_w" for nm in names] + [False],
                                       [int(nm == "w_pool") for nm in names] + [0],
                                       dmod.reshape(N_DEV, 1, MOD_COLS))
    small, loss11 = _small_update(
        sums[:-1] + g_mix, small_w + [w_in_t, w_out[0]], small_m + [m_w_in[0].T, m_w_out[0]],
        small_v + [v_w_in[0].T, v_w_out[0]], scx, dm2d, w_ada[0], m_w_ada[0], v_w_ada[0], sums[-1])
    loss = loss11.reshape(())
    lead = {"sgu_norm_g", "w_spatial", "b_spatial", "w_pool", "conv_w", "w_in", "w_out", "w_ada"}
    res = {nm: tuple((a.T if nm == "w_in" else a)[None] if nm in lead else a for a in four)
           for nm, four in zip(names + ["w_in", "w_out", "w_ada"], small)}
    res.update(w_up=r_up, w_down=r_down)

    order = ["w_ada", "b_ada", "pre_mix_g", "post_mix_g", "w_in", "sgu_norm_g", "w_spatial", "b_spatial", "w_pool",
             "pool_scale", "w_out", "pre_ffn_g", "post_ffn_g", "w_up", "conv_w", "conv_b", "w_down"]
    return (loss, grad_x[None], *[res[nm][0] for nm in order], *[res[nm][1] for nm in order],
            *[res[nm][2] for nm in order], *[res[nm][3] for nm in order])
```

```python
import functools
import math

import jax
import jax.numpy as jnp
from jax import lax
from jax.experimental import pallas as pl
from jax.experimental.pallas import tpu as pltpu

F32 = jnp.float32
BF16 = jnp.bfloat16
MESH = pl.DeviceIdType.MESH

EPS = 1e-6
D = 1024
HEAD = 128
N_HEAD = 4
A_WIDTH = 512
B_WIDTH = 512
IN_WIDTH = 1536
WINDOWS = (2, 4, 8, 16)
CHUNK = 64
FF = 2816
N_DEV = 8
FF_CHUNK = 704
N_MOD = 6
MOD_COLS = 768

ADAM_LR = 0.001
ADAM_B1 = 0.9
ADAM_B2 = 0.999
ADAM_EPS = 1e-08
ADAM_WD = 0.01
ADAM_STEP = 10

VMEM_LIMIT_V7X = 62 * 1024 * 1024
HALO = 8
POOL_HALO = 16

VMEM_SPEC = pl.BlockSpec(memory_space=pltpu.VMEM)
ANY_SPEC = pl.BlockSpec(memory_space=pl.ANY)


def _bf(x):
    return x.astype(BF16)


def _mm(a, b):
    return jnp.dot(a, b, preferred_element_type=F32)


def _mm_nt(a, b):
    return lax.dot_general(a, b, (((1,), (1,)), ((), ())), preferred_element_type=F32)


def _mm_tn(a, b):
    return lax.dot_general(a, b, (((0,), (0,)), ((), ())), preferred_element_type=F32)


def _rstd(x):
    return lax.rsqrt(jnp.mean(x * x, axis=-1, keepdims=True) + EPS)


def _sum0(x):
    return jnp.sum(x, axis=0, keepdims=True)


def _rowmean(x):
    return jnp.mean(x, axis=-1, keepdims=True)


_GELU_K = math.sqrt(2.0 / math.pi)


def _gelu_and_grad(x):
    x2 = x * x
    th = jnp.tanh(_GELU_K * (x + 0.044715 * (x * x2)))
    cdf = 0.5 * th + 0.5
    grad = cdf + x * (1.0 - th * th) * ((0.5 * _GELU_K) + (1.5 * 0.044715 * _GELU_K) * x2)
    return x * cdf, grad


def _gelu(x):
    return x * (0.5 * (1.0 + jnp.tanh(_GELU_K * (x + 0.044715 * (x * x * x)))))


def _sigmoid(x):
    return 0.5 * jnp.tanh(0.5 * x) + 0.5


def _sgu_mask():
    ri = lax.broadcasted_iota(jnp.int32, (HEAD, HEAD), 0)
    ci = lax.broadcasted_iota(jnp.int32, (HEAD, HEAD), 1)
    return (ci // CHUNK) <= (ri // CHUNK)


def _window_sum(ext, w, trailing):
    n = ext.shape[0]
    s, k = ext, 1
    while k < w:
        s = s + pltpu.roll(s, k if trailing else n - k, 0)
        k *= 2
    return s


def _inv_count(row0, n, w):
    t = row0 + lax.broadcasted_iota(jnp.int32, (n, 1), 0)
    return 1.0 / jnp.minimum(t + 1, w).astype(F32)


def _shift_down(v, before, k):
    rows = lax.broadcasted_iota(jnp.int32, before.shape, 0)
    r = pltpu.roll(v, k, 0)
    top = jnp.where(rows < k, pltpu.roll(before, k, 0), r[0:HALO])
    return jnp.concatenate([top, r[HALO:]], axis=0)


def _shift_up(v, after, k):
    n = v.shape[0]
    rows = lax.broadcasted_iota(jnp.int32, after.shape, 0)
    r = pltpu.roll(v, n - k, 0)
    bottom = jnp.where(rows >= HALO - k, pltpu.roll(after, HALO - k, 0), r[n - HALO:])
    return jnp.concatenate([r[:n - HALO], bottom], axis=0)


def _adamw(w, g, m, v):
    m = ADAM_B1 * m + (1.0 - ADAM_B1) * g
    v = ADAM_B2 * v + (1.0 - ADAM_B2) * (g * g)
    m_hat = m / (1.0 - ADAM_B1 ** ADAM_STEP)
    v_hat = v / (1.0 - ADAM_B2 ** ADAM_STEP)
    delta = -ADAM_LR * (m_hat / (jnp.sqrt(v_hat) + ADAM_EPS) + ADAM_WD * w)
    return delta, m, v


def _coords():
    return lax.axis_index("x"), lax.axis_index("y"), lax.axis_index("c")


def _peer(k):
    x, y, c = _coords()
    return (x ^ ((k >> 2) & 1), y ^ ((k >> 1) & 1), c ^ (k & 1))


def _my_index():
    x, y, c = _coords()
    return 4 * x + 2 * y + c


def _adaln_modulation(c_ref, w_ref, b_ref, mod_ref, scx_ref, scbuf, stage, recv, send_sems, recv_sems):
    me = _my_index()
    cv = c_ref[...]
    scbuf[0] = cv * _sigmoid(cv)
    first = [
        pltpu.make_async_remote_copy(scbuf.at[0], scbuf.at[k], send_sems.at[0, k], recv_sems.at[0, k],
                                     device_id=_peer(k), device_id_type=MESH)
        for k in range(1, N_DEV)
    ]
    for cp in first:
        cp.start()
    for cp in first:
        cp.wait()
    scx_ref[...] = jnp.zeros(scx_ref.shape, F32)
    for k in range(N_DEV):
        scx_ref[k:k + 1, :] = scbuf[k]
    prod = _mm(_bf(scx_ref[...]), _bf(w_ref[...]))
    for k in range(N_DEV):
        stage[k] = prod[k:k + 1, :] + b_ref[me]
    second = [
        pltpu.make_async_remote_copy(stage.at[k], recv.at[k], send_sems.at[1, k], recv_sems.at[1, k],
                                     device_id=_peer(k), device_id_type=MESH)
        for k in range(1, N_DEV)
    ]
    for cp in second:
        cp.start()
    mod_ref[me] = stage[0]
    for cp in second:
        cp.wait()
    for k in range(1, N_DEV):
        mod_ref[me ^ k] = recv[k]


class _GatherSteps:
    def __init__(self, ins, outs, stages, send_sems, recv_sems, local_sems):
        self.ins, self.outs, self.stages = ins, outs, stages
        self.send_sems, self.recv_sems, self.local_sems = send_sems, recv_sems, local_sems
        x, y, c = _coords()
        self.c = c
        self.me, self.sibling = (x, y, c), (x, y, 1 - c)
        self.chips = [(1 - x, y), (x, 1 - y), (1 - x, 1 - y)]

    def _copy(self, a, k, block, to, from_stage=False):
        dst = self.outs[a].at[4 * block[0] + 2 * block[1] + block[2]]
        return pltpu.make_async_remote_copy(self.stages[a] if from_stage else dst, dst, self.send_sems.at[a, k],
                                            self.recv_sems.at[a, k], device_id=to, device_id_type=MESH)

    def _local(self, a):
        me = self.me
        return pltpu.make_async_copy(self.stages[a], self.outs[a].at[4 * me[0] + 2 * me[1] + me[2]],
                                     self.local_sems.at[a])

    def _first(self, a):
        cps = [self._copy(a, 0, self.me, self.sibling, from_stage=True)]
        return cps + [self._copy(a, 1 + j, self.me, (*chip, self.c), from_stage=True)
                      for j, chip in enumerate(self.chips)]

    def _passed(self, a, j):
        return self._copy(a, 4 + j, (*self.chips[j], self.c), self.sibling)

    def start(self):
        for a in range(len(self.ins)):
            block = self.ins[a][...]
            if block.shape != self.stages[a].shape:
                block = block.T
            self.stages[a][...] = block.astype(self.stages[a].dtype)
            self._local(a).start()
            for cp in self._first(a):
                cp.start()

    def forward(self):
        for a in range(len(self.ins)):
            for j, chip in enumerate(self.chips):
                self._copy(a, 1 + j, (*chip, self.c), self.me).wait_recv()
                self._passed(a, j).start()

    def finish(self):
        for a in range(len(self.ins)):
            self._copy(a, 0, self.sibling, self.me).wait_recv()
            for j, chip in enumerate(self.chips):
                self._copy(a, 4 + j, (*chip, 1 - self.c), self.me).wait_recv()
            for cp in self._first(a) + [self._passed(a, j) for j in range(3)]:
                cp.wait_send()
            self._local(a).wait()


def _gather_scratch(shapes, out_dtypes):
    n = len(shapes)
    return ([pltpu.VMEM(s, dt) for s, dt in zip(shapes, out_dtypes)]
            + [pltpu.SemaphoreType.DMA((n, 7)), pltpu.SemaphoreType.DMA((n, 7)), pltpu.SemaphoreType.DMA((n,))])


def _gather_out_shapes(shapes, out_dtypes):
    return tuple(jax.ShapeDtypeStruct((N_DEV, *s), dt) for s, dt in zip(shapes, out_dtypes))


def _prologue(c_row, w_ada, b_ada3, shards, out_dtypes):
    n = len(shards)

    def body(*refs):
        c_ref, w_ref, b_ref = refs[:3]
        mod_ref, scx_ref = refs[3 + n:5 + n]
        gather = _GatherSteps(refs[3:3 + n], refs[5 + n:5 + 2 * n], refs[5 + 2 * n:5 + 3 * n],
                              *refs[5 + 3 * n:8 + 3 * n])
        gather.start()
        _adaln_modulation(c_ref, w_ref, b_ref, mod_ref, scx_ref, *refs[8 + 3 * n:])
        gather.forward()
        gather.finish()

    outs = pl.pallas_call(
        body,
        name="prologue",
        out_shape=(jax.ShapeDtypeStruct((N_DEV, 1, MOD_COLS), F32), jax.ShapeDtypeStruct((2 * N_DEV, D), F32))
        + _gather_out_shapes([s.shape for s in shards], out_dtypes),
        in_specs=[VMEM_SPEC] * (3 + n),
        out_specs=(VMEM_SPEC, VMEM_SPEC) + (ANY_SPEC,) * n,
        scratch_shapes=_gather_scratch([s.shape for s in shards], out_dtypes) + [
            pltpu.VMEM((N_DEV, 1, D), F32),
            pltpu.VMEM((N_DEV, 1, MOD_COLS), F32),
            pltpu.VMEM((N_DEV, 1, MOD_COLS), F32),
            pltpu.SemaphoreType.DMA((2, N_DEV)),
            pltpu.SemaphoreType.DMA((2, N_DEV)),
        ],
        compiler_params=pltpu.CompilerParams(vmem_limit_bytes=VMEM_LIMIT_V7X),
    )(c_row, w_ada, b_ada3, *shards)
    return outs[0], outs[1], outs[2:]


class _ChipExchangeSteps:
    FLIPS = ((1, 0), (0, 1), (1, 1))

    def __init__(self, srcs, dsts, send_sems, recv_sems):
        self.srcs, self.dsts, self.send_sems, self.recv_sems = srcs, dsts, send_sems, recv_sems

    def _copies(self):
        x, y, c = _coords()
        out = []
        for a in range(len(self.srcs)):
            for j, (fx, fy) in enumerate(self.FLIPS):
                k = 3 * a + j
                out.append(pltpu.make_async_remote_copy(
                    self.srcs[a].at[x ^ fx, y ^ fy], self.dsts[a].at[j], self.send_sems.at[k], self.recv_sems.at[k],
                    device_id=(x ^ fx, y ^ fy, c), device_id_type=MESH))
        return out

    def start(self):
        for cp in self._copies():
            cp.start()

    def finish(self):
        for cp in self._copies():
            cp.wait()


def _mix_fwd(x, mod, g_pre, g_post, w_in_t, sgn, w_sp, b_sp_t, w_pool, p_scale, w_out_b, ts, shards, shard_shapes,
             shard_dtypes):
    t_len = x.shape[0]
    nt, nb = t_len // ts, ts // HEAD
    ns = len(shards)

    def body(*refs):
        (x_ref, mod_ref, g1_ref, g2_ref, win_ref, sgn_ref, ws_ref, bst_ref, wp_ref, ps_ref, wout_ref) = refs[:11]
        x1_ref, proj_ref, mixed_ref = refs[11 + ns:14 + ns]
        pbuf, cat = refs[14 + 2 * ns:16 + 2 * ns]
        gather = _GatherSteps(refs[11:11 + ns], refs[14 + ns:14 + 2 * ns], refs[16 + 2 * ns:16 + 3 * ns],
                              *refs[16 + 3 * ns:])
        i = pl.program_id(0)

        @pl.when(i == 0)
        def _():
            pbuf[0:POOL_HALO, :] = jnp.zeros((POOL_HALO, B_WIDTH), F32)
            gather.start()

        @pl.when(i == (7 * nt) // 8)
        def _():
            gather.forward()

        xv = x_ref[...]
        sh, sc, gm = mod_ref[0:1, :], mod_ref[1:2, :], mod_ref[2:3, :]
        h = (xv * _rstd(xv) * g1_ref[...]) * (1.0 + sc) + sh
        proj_ref[...] = _mm_nt(_bf(h), win_ref[...])
        pbuf[POOL_HALO:POOL_HALO + ts, :] = proj_ref[:, 2 * A_WIDTH:]
        smask = _sgu_mask()
        for hd in range(N_HEAD):
            u = _gelu(proj_ref[:, hd * HEAD:(hd + 1) * HEAD])
            v = _gelu(proj_ref[:, A_WIDTH + hd * HEAD:A_WIDTH + (hd + 1) * HEAD])
            vn = _bf(v * _rstd(v) * sgn_ref[hd:hd + 1, :])
            wm = _bf(jnp.where(smask, ws_ref[hd], 0.0))
            bias = bst_ref[:, hd:hd + 1]
            for b in range(nb):
                rows = slice(b * HEAD, (b + 1) * HEAD)
                z = _mm(wm, vn[rows]) + bias
                cat[rows, hd * HEAD:(hd + 1) * HEAD] = _bf(u[rows] * z)
        for g, w in enumerate(WINDOWS):
            cols = slice(g * HEAD, (g + 1) * HEAD)
            ext = pbuf[:, cols]
            pooled = _window_sum(ext, w, True)[POOL_HALO:] * _inv_count(i * ts, ts, w) - ext[POOL_HALO:]
            cat[:, A_WIDTH + g * HEAD:A_WIDTH + (g + 1) * HEAD] = _bf(_mm(_bf(pooled), _bf(wp_ref[g])) * ps_ref[:, cols])
        pbuf[0:POOL_HALO, :] = pbuf[ts:ts + POOL_HALO, :]
        mixed = _mm(cat[...], wout_ref[...])
        mixed_ref[...] = mixed
        x1_ref[...] = xv + gm * (mixed * _rstd(mixed) * g2_ref[...])

        @pl.when(i == nt - 1)
        def _():
            gather.finish()

    tile = lambda wid: pl.BlockSpec((ts, wid), lambda i: (i, 0))
    outs = pl.pallas_call(
        body,
        name="mix_fwd",
        grid=(nt,),
        out_shape=(jax.ShapeDtypeStruct((t_len, D), F32), jax.ShapeDtypeStruct((t_len, IN_WIDTH), F32),
                   jax.ShapeDtypeStruct((t_len, D), F32)) + _gather_out_shapes(shard_shapes, shard_dtypes),
        in_specs=[tile(D)] + [VMEM_SPEC] * (10 + ns),
        out_specs=(tile(D), tile(IN_WIDTH), tile(D)) + (ANY_SPEC,) * ns,
        scratch_shapes=[pltpu.VMEM((POOL_HALO + ts, B_WIDTH), F32), pltpu.VMEM((ts, D), BF16)]
        + _gather_scratch(shard_shapes, shard_dtypes),
        compiler_params=pltpu.CompilerParams(dimension_semantics=("arbitrary",), vmem_limit_bytes=VMEM_LIMIT_V7X),
    )(x, mod, g_pre, g_post, w_in_t, sgn, w_sp, b_sp_t, w_pool, p_scale, w_out_b, *shards)
    return outs[:3], outs[3:]


def _ffn_fwd(x1, target, mod, g_pre, g_post, w_up_b, conv_w8, conv_b8, w_down_b, ts):
    t_len = x1.shape[0]
    nt = t_len // ts

    def body(x1_ref, tgt_ref, mod_ref, g3_ref, g4_ref, wup_ref, cw_ref, cb_ref, wdown_ref,
             up_ref, f_ref, dx2_ref, loss_ref, ucarry):
        i = pl.program_id(0)

        @pl.when(i == 0)
        def _():
            ucarry[...] = jnp.zeros(ucarry.shape, F32)
            loss_ref[...] = jnp.zeros(loss_ref.shape, F32)

        x1v = x1_ref[...]
        sh, sc, gf = mod_ref[3:4, :], mod_ref[4:5, :], mod_ref[5:6, :]
        h2 = _bf((x1v * _rstd(x1v) * g3_ref[...]) * (1.0 + sc) + sh)
        half = N_DEV // 2

        def up_pair(j):
            return [_mm(h2, wup_ref[jj]) for jj in (j, j + half)]

        f = jnp.zeros((ts, D), F32)
        ups = up_pair(0)
        for j in range(half):
            nxt = up_pair(j + 1) if j + 1 < half else None
            ys = []
            for up, jj in zip(ups, (j, j + half)):
                up_ref[jj] = up
                before = ucarry[jj]
                ucarry[jj] = up[ts - HALO:, :]
                cw = cw_ref[jj]
                ys.append(cb_ref[jj:jj + 1, :] + _shift_down(up, before, 2) * cw[0:1, :]
                          + _shift_down(up, before, 1) * cw[1:2, :] + up * cw[2:3, :])
            gate, val = ys
            act = gate * _sigmoid(gate) * val
            f = f + _mm(_bf(act), wdown_ref[j * FF_CHUNK:(j + 1) * FF_CHUNK, :])
            ups = nxt
        f_ref[...] = f
        x2 = x1v + gf * (f * _rstd(f) * g4_ref[...])
        err = x2 - tgt_ref[...]
        loss_ref[...] += _sum0(err * err)
        dx2_ref[...] = err * (1.0 / D)

    tile = pl.BlockSpec((ts, D), lambda i: (i, 0))
    return pl.pallas_call(
        body,
        name="ffn_fwd",
        grid=(nt,),
        out_shape=(jax.ShapeDtypeStruct((N_DEV, t_len, FF_CHUNK), F32), jax.ShapeDtypeStruct((t_len, D), F32),
                   jax.ShapeDtypeStruct((t_len, D), F32), jax.ShapeDtypeStruct((1, D), F32)),
        in_specs=[tile, tile] + [VMEM_SPEC] * 7,
        out_specs=(pl.BlockSpec((N_DEV, ts, FF_CHUNK), lambda i: (0, i, 0)), tile, tile,
                   pl.BlockSpec((1, D), lambda i: (0, 0))),
        scratch_shapes=[pltpu.VMEM((N_DEV, HALO, FF_CHUNK), F32)],
        compiler_params=pltpu.CompilerParams(dimension_semantics=("arbitrary",), vmem_limit_bytes=VMEM_LIMIT_V7X),
    )(x1, target, mod, g_pre, g_post, w_up_b, conv_w8, conv_b8, w_down_b)


def _ffn_bwd(dx2, f, x1, up, mod, g_pre, g_post, w_up_b, conv_w8, conv_b8, w_down_b, ts):
    t_len = x1.shape[0]
    nt = t_len // ts
    half = N_DEV // 2

    def body(dx2_ref, f_ref, x1_ref, up_ref, halo_ref, mod_ref, g3_ref, g4_ref, wup_ref, cw_ref, cb_ref, wdown_ref,
             dx1_ref, dup_ref, act_ref, df_ref, h2_ref, dmod_ref, dg3_ref, dg4_ref, dcb_ref, dcw_ref,
             dycarry, dh2acc):
        i = pl.program_id(0)
        r = nt - 1 - i

        @pl.when(i == 0)
        def _():
            for ref in (dmod_ref, dg3_ref, dg4_ref, dcb_ref, dcw_ref, dycarry):
                ref[...] = jnp.zeros(ref.shape, F32)

        dx2v, fv, x1v = dx2_ref[...], f_ref[...], x1_ref[...]
        sh, sc, gf = mod_ref[3:4, :], mod_ref[4:5, :], mod_ref[5:6, :]
        g3, g4 = g3_ref[...], g4_ref[...]
        rstd4 = _rstd(fv)
        fh = fv * rstd4
        dmod_ref[2:3, :] += _sum0(dx2v * (fh * g4))
        dr = dx2v * gf
        dg4_ref[...] += _sum0(dr * fh)
        dfh = dr * g4
        dfb = _bf(rstd4 * (dfh - fh * _rowmean(dfh * fh)))
        df_ref[...] = dfb
        rstd3 = _rstd(x1v)
        xh = x1v * rstd3
        n3 = xh * g3
        h2_ref[...] = _bf(n3 * (1.0 + sc) + sh)
        dh2acc[...] = jnp.zeros((ts, D), F32)
        keep = jnp.where(r > 0, 1.0, 0.0).astype(F32)

        def dact_of(j):
            return _mm_nt(dfb, wdown_ref[j * FF_CHUNK:(j + 1) * FF_CHUNK, :])

        dact_next = dact_of(0)
        for j in range(half):
            dact = dact_next
            if j + 1 < half:
                dact_next = dact_of(j + 1)
            ys = []
            for jj in (j, j + half):
                before = halo_ref[jj] * keep
                upc = up_ref[jj]
                cw = cw_ref[jj]
                ys.append(cb_ref[jj:jj + 1, :] + _shift_down(upc, before, 2) * cw[0:1, :]
                          + _shift_down(upc, before, 1) * cw[1:2, :] + upc * cw[2:3, :])
            gate, val = ys
            sg = _sigmoid(gate)
            silu = gate * sg
            act_ref[j] = _bf(silu * val)
            dys = (dact * val * (sg + silu * (1.0 - sg)), dact * silu)
            for q, jj in enumerate((j, j + half)):
                dy = dys[q]
                cw = cw_ref[jj]
                dcb_ref[jj:jj + 1, :] += _sum0(dy)
                after = dycarry[jj]
                dycarry[jj] = dy[0:HALO, :]
                dy1, dy2 = _shift_up(dy, after, 1), _shift_up(dy, after, 2)
                upc = up_ref[jj]
                dcw_ref[jj, 0:1, :] += _sum0(dy2 * upc)
                dcw_ref[jj, 1:2, :] += _sum0(dy1 * upc)
                dcw_ref[jj, 2:3, :] += _sum0(dy * upc)
                dup = _bf(dy * cw[2:3, :] + dy1 * cw[1:2, :] + dy2 * cw[0:1, :])
                dup_ref[jj] = dup
                dh2acc[...] += _mm_nt(dup, wup_ref[jj])
        dh2 = dh2acc[...]
        dmod_ref[0:1, :] += _sum0(dh2)
        dmod_ref[1:2, :] += _sum0(dh2 * n3)
        dn3 = dh2 * (1.0 + sc)
        dg3_ref[...] += _sum0(dn3 * xh)
        dxh = dn3 * g3
        dx1_ref[...] = dx2v + rstd3 * (dxh - xh * _rowmean(dxh * xh))

    tile = pl.BlockSpec((ts, D), lambda i: (nt - 1 - i, 0))
    chunked = lambda n: pl.BlockSpec((n, ts, FF_CHUNK), lambda i: (0, nt - 1 - i, 0))
    halo = pl.BlockSpec((N_DEV, HALO, FF_CHUNK), lambda i: (0, jnp.maximum((nt - 1 - i) * (ts // HALO) - 1, 0), 0))
    const = lambda *shape: pl.BlockSpec(shape, lambda i: (0,) * len(shape))
    return pl.pallas_call(
        body,
        name="ffn_bwd",
        grid=(nt,),
        out_shape=(jax.ShapeDtypeStruct((t_len, D), F32), jax.ShapeDtypeStruct((N_DEV, t_len, FF_CHUNK), BF16),
                   jax.ShapeDtypeStruct((half, t_len, FF_CHUNK), BF16), jax.ShapeDtypeStruct((t_len, D), BF16),
                   jax.ShapeDtypeStruct((t_len, D), BF16), jax.ShapeDtypeStruct((3, D), F32),
                   jax.ShapeDtypeStruct((1, D), F32), jax.ShapeDtypeStruct((1, D), F32),
                   jax.ShapeDtypeStruct((N_DEV, FF_CHUNK), F32), jax.ShapeDtypeStruct((N_DEV, 3, FF_CHUNK), F32)),
        in_specs=[tile, tile, tile, chunked(N_DEV), halo] + [VMEM_SPEC] * 7,
        out_specs=(tile, chunked(N_DEV), chunked(half), tile, tile, const(3, D), const(1, D), const(1, D),
                   const(N_DEV, FF_CHUNK), const(N_DEV, 3, FF_CHUNK)),
        scratch_shapes=[pltpu.VMEM((N_DEV, HALO, FF_CHUNK), F32), pltpu.VMEM((ts, D), F32)],
        compiler_params=pltpu.CompilerParams(dimension_semantics=("arbitrary",), vmem_limit_bytes=VMEM_LIMIT_V7X),
    )(dx2, f, x1, up, up, mod, g_pre, g_post, w_up_b, conv_w8, conv_b8, w_down_b)


def _wgrad_up(h2, dup, ts):
    t_len = h2.shape[0]
    nt, half = t_len // ts, N_DEV // 2

    def body(h2_ref, dup_ref, out_ref):
        @pl.when(pl.program_id(1) == 0)
        def _():
            out_ref[...] = jnp.zeros(out_ref.shape, F32)

        for q in range(half):
            out_ref[q] += _mm_tn(dup_ref[q], h2_ref[...])

    return pl.pallas_call(
        body,
        name="wgrad_up",
        grid=(2, nt),
        out_shape=jax.ShapeDtypeStruct((N_DEV, FF_CHUNK, D), F32),
        in_specs=[pl.BlockSpec((ts, D), lambda g, t: (t, 0)), pl.BlockSpec((half, ts, FF_CHUNK), lambda g, t: (g, t, 0))],
        out_specs=pl.BlockSpec((half, FF_CHUNK, D), lambda g, t: (g, 0, 0)),
        compiler_params=pltpu.CompilerParams(dimension_semantics=("arbitrary", "arbitrary"),
                                             vmem_limit_bytes=VMEM_LIMIT_V7X),
    )(h2, dup)


def _sibling_swap_copies(srcs, dsts, send_sems, recv_sems):
    x, y, c = _coords()
    return [
        pltpu.make_async_remote_copy(srcs[a].at[xs, ys, 1 - c], dsts[a].at[xs, ys], send_sems.at[a, 2 * xs + ys],
                                     recv_sems.at[a, 2 * xs + ys], device_id=(x, y, 1 - c), device_id_type=MESH)
        for a in range(len(srcs)) for xs in range(2) for ys in range(2)
    ]


def _wgrad_down(act, df, ts, swap_src):
    t_len = df.shape[0]
    nt, half = t_len // ts, N_DEV // 2

    def body(act_ref, df_ref, src_ref, out_ref, dst_ref, send_sems, recv_sems):
        t = pl.program_id(0)

        @pl.when(t == 0)
        def _():
            for cp in _sibling_swap_copies([src_ref], [dst_ref], send_sems, recv_sems):
                cp.start()
            out_ref[...] = jnp.zeros(out_ref.shape, F32)

        for q in range(half):
            out_ref[q] += _mm_tn(act_ref[q], df_ref[...])

        @pl.when(t == nt - 1)
        def _():
            for cp in _sibling_swap_copies([src_ref], [dst_ref], send_sems, recv_sems):
                cp.wait()

    return pl.pallas_call(
        body,
        name="wgrad_down",
        grid=(nt,),
        out_shape=(jax.ShapeDtypeStruct((half, FF_CHUNK, D), F32), jax.ShapeDtypeStruct(swap_src.shape[1:], F32)),
        in_specs=[pl.BlockSpec((half, ts, FF_CHUNK), lambda t: (0, t, 0)), pl.BlockSpec((ts, D), lambda t: (t, 0)),
                  ANY_SPEC],
        out_specs=(pl.BlockSpec((half, FF_CHUNK, D), lambda t: (0, 0, 0)), ANY_SPEC),
        scratch_shapes=[pltpu.SemaphoreType.DMA((1, 4)), pltpu.SemaphoreType.DMA((1, 4))],
        compiler_params=pltpu.CompilerParams(dimension_semantics=("arbitrary",), vmem_limit_bytes=VMEM_LIMIT_V7X),
    )(act, df, swap_src)


def _mix_bwd(dx1, x, proj, mixed, mod, g_pre, g_post, w_in_t, sgn, w_sp, b_sp_t, w_pool, p_scale, w_out_b, ts, rs_srcs):
    t_len = x.shape[0]
    nt, nb = t_len // ts, ts // HEAD
    nr = len(rs_srcs)

    def body(*refs):
        (dx1_ref, x_ref, proj_ref, halo_ref, mixed_ref, mod_ref, g1_ref, g2_ref, win_ref, sgn_ref, ws_ref,
         bst_ref, wp_ref, ps_ref, wout_ref) = refs[:15]
        (gx_ref, dwin_ref, dwout_ref, dmod_ref, dg1_ref, dg2_ref, dsgn_ref, dws_ref, dbst_ref, dwp_ref,
         dps_ref) = refs[15 + nr:26 + nr]
        pbuf, dwsbuf, cat, dproj, dcat = refs[26 + 2 * nr:31 + 2 * nr]
        exchange = _ChipExchangeSteps(refs[15:15 + nr], refs[26 + nr:26 + 2 * nr], *refs[31 + 2 * nr:])
        i = pl.program_id(0)
        r = nt - 1 - i

        @pl.when(i == 0)
        def _():
            exchange.start()
            for ref in (dwin_ref, dwout_ref, dmod_ref, dg1_ref, dg2_ref, dsgn_ref, dws_ref, dbst_ref, dwp_ref, dps_ref):
                ref[...] = jnp.zeros(ref.shape, F32)
            dwsbuf[ts:ts + POOL_HALO, :] = jnp.zeros((POOL_HALO, B_WIDTH), F32)

        xv, dx1v, mixed = x_ref[...], dx1_ref[...], mixed_ref[...]
        sh, sc, gm = mod_ref[0:1, :], mod_ref[1:2, :], mod_ref[2:3, :]
        g1, g2 = g1_ref[...], g2_ref[...]
        rstd2 = _rstd(mixed)
        mh = mixed * rstd2
        dmod_ref[2:3, :] += _sum0(dx1v * (mh * g2))
        dr = dx1v * gm
        dg2_ref[...] += _sum0(dr * mh)
        dmh = dr * g2
        dmb = _bf(rstd2 * (dmh - mh * _rowmean(dmh * mh)))
        dcat[...] = _mm_nt(dmb, wout_ref[...])
        smask = _sgu_mask()
        for hd in range(N_HEAD):
            ucols = slice(hd * HEAD, (hd + 1) * HEAD)
            vcols = slice(A_WIDTH + hd * HEAD, A_WIDTH + (hd + 1) * HEAD)
            u, du_dp = _gelu_and_grad(proj_ref[:, ucols])
            v, dv_dp = _gelu_and_grad(proj_ref[:, vcols])
            rs = _rstd(v)
            vhat = v * rs
            gn = sgn_ref[hd:hd + 1, :]
            vn = _bf(vhat * gn)
            wm = _bf(jnp.where(smask, ws_ref[hd], 0.0))
            bias = bst_ref[:, hd:hd + 1]
            dzsum = jnp.zeros((HEAD, HEAD), F32)
            dwm = jnp.zeros((HEAD, HEAD), F32)
            dvn_parts = []
            for b in range(nb):
                rows = slice(b * HEAD, (b + 1) * HEAD)
                z = _mm(wm, vn[rows]) + bias
                da = dcat[rows, ucols]
                cat[rows, ucols] = _bf(u[rows] * z)
                dz = da * u[rows]
                dzsum = dzsum + dz
                dzb = _bf(dz)
                dwm = dwm + _mm_nt(dzb, vn[rows])
                dvn_parts.append(_mm_tn(wm, dzb))
                dproj[rows, ucols] = _bf((da * z) * du_dp[rows])
            dvn = jnp.concatenate(dvn_parts, axis=0)
            dsgn_ref[hd:hd + 1, :] += _sum0(dvn * vhat)
            dvh = dvn * gn
            dproj[:, vcols] = _bf((rs * (dvh - vhat * _rowmean(dvh * vhat))) * dv_dp)
            dws_ref[hd] += jnp.where(smask, dwm, 0.0)
            dbst_ref[:, hd:hd + 1] += jnp.sum(dzsum, axis=1, keepdims=True)
        keep = jnp.where(r > 0, 1.0, 0.0).astype(F32)
        pbuf[0:POOL_HALO, :] = halo_ref[...] * keep
        pbuf[POOL_HALO:POOL_HALO + ts, :] = proj_ref[:, 2 * A_WIDTH:]
        for g, w in enumerate(WINDOWS):
            cols = slice(g * HEAD, (g + 1) * HEAD)
            ccols = slice(A_WIDTH + g * HEAD, A_WIDTH + (g + 1) * HEAD)
            pcols = slice(2 * A_WIDTH + g * HEAD, 2 * A_WIDTH + (g + 1) * HEAD)
            wpg = _bf(wp_ref[g])
            psg = ps_ref[:, cols]
            ext = pbuf[:, cols]
            inv = _inv_count(r * ts, ts, w)
            pb = _bf(_window_sum(ext, w, True)[POOL_HALO:] * inv - ext[POOL_HALO:])
            yb = _mm(pb, wpg)
            dob = dcat[:, ccols]
            cat[:, ccols] = _bf(yb * psg)
            dps_ref[:, cols] += _sum0(dob * yb)
            dyb = _bf(dob * psg)
            dwp_ref[g] += _mm_tn(pb, dyb)
            dpooled = _mm_nt(dyb, wpg)
            dwsbuf[0:ts, cols] = dpooled * inv
            dproj[:, pcols] = _bf(_window_sum(dwsbuf[:, cols], w, False)[0:ts] - dpooled)
        dwsbuf[ts:ts + POOL_HALO, :] = dwsbuf[0:POOL_HALO, :]
        dpb = dproj[...]
        rstd1 = _rstd(xv)
        xh = xv * rstd1
        n1 = xh * g1
        dwin_ref[...] += _mm_tn(dpb, _bf(n1 * (1.0 + sc) + sh))
        dwout_ref[...] += _mm_tn(cat[...], dmb)
        dh = _mm(dpb, win_ref[...])
        dmod_ref[0:1, :] += _sum0(dh)
        dmod_ref[1:2, :] += _sum0(dh * n1)
        dn1 = dh * (1.0 + sc)
        dg1_ref[...] += _sum0(dn1 * xh)
        dxh = dn1 * g1
        gx_ref[...] = dx1v + rstd1 * (dxh - xh * _rowmean(dxh * xh))

        @pl.when(i == nt - 1)
        def _():
            exchange.finish()

    tile = lambda wid: pl.BlockSpec((ts, wid), lambda i: (nt - 1 - i, 0))
    halo = pl.BlockSpec((POOL_HALO, B_WIDTH),
                        lambda i: (jnp.maximum((nt - 1 - i) * (ts // POOL_HALO) - 1, 0), 2 * A_WIDTH // B_WIDTH))
    const = lambda *shape: pl.BlockSpec(shape, lambda i: (0,) * len(shape))
    resident = lambda *shape: pl.BlockSpec(shape, lambda i: (0,) * len(shape), pipeline_mode=pl.Buffered(1))
    outs = pl.pallas_call(
        body,
        name="mix_bwd",
        grid=(nt,),
        out_shape=(jax.ShapeDtypeStruct((t_len, D), F32), jax.ShapeDtypeStruct((IN_WIDTH, D), F32),
                   jax.ShapeDtypeStruct((D, D), F32), jax.ShapeDtypeStruct((3, D), F32),
                   jax.ShapeDtypeStruct((1, D), F32), jax.ShapeDtypeStruct((1, D), F32),
                   jax.ShapeDtypeStruct((N_HEAD, HEAD), F32), jax.ShapeDtypeStruct((N_HEAD, HEAD, HEAD), F32),
                   jax.ShapeDtypeStruct((HEAD, N_HEAD), F32), jax.ShapeDtypeStruct((N_HEAD, HEAD, HEAD), F32),
                   jax.ShapeDtypeStruct((1, B_WIDTH), F32))
        + tuple(jax.ShapeDtypeStruct((3, *s.shape[2:]), s.dtype) for s in rs_srcs),
        in_specs=[tile(D), tile(D), tile(IN_WIDTH), halo, tile(D)] + [VMEM_SPEC] * 10 + [ANY_SPEC] * nr,
        out_specs=(tile(D), resident(IN_WIDTH, D), resident(D, D), const(3, D), const(1, D), const(1, D),
                   const(N_HEAD, HEAD), const(N_HEAD, HEAD, HEAD), const(HEAD, N_HEAD), const(N_HEAD, HEAD, HEAD),
                   const(1, B_WIDTH)) + (ANY_SPEC,) * nr,
        scratch_shapes=[pltpu.VMEM((POOL_HALO + ts, B_WIDTH), F32), pltpu.VMEM((ts + POOL_HALO, B_WIDTH), F32),
                        pltpu.VMEM((ts, D), BF16), pltpu.VMEM((ts, IN_WIDTH), BF16), pltpu.VMEM((ts, D), F32),
                        pltpu.SemaphoreType.DMA((3 * nr,)), pltpu.SemaphoreType.DMA((3 * nr,))],
        compiler_params=pltpu.CompilerParams(dimension_semantics=("arbitrary",), vmem_limit_bytes=VMEM_LIMIT_V7X),
    )(dx1, x, proj, proj, mixed, mod, g_pre, g_post, w_in_t, sgn, w_sp, b_sp_t, w_pool, p_scale, w_out_b, *rs_srcs)
    return outs[:11], outs[11:]


def _pair_add(name, coords, grid, specs_a, specs_b, out_specs, out_shapes, a_arrays, b_arrays, swap_srcs):
    n, ns = len(a_arrays), len(swap_srcs)
    last = tuple(g - 1 for g in grid)

    def body(co_ref, *refs):
        ids = [pl.program_id(d) for d in range(len(grid))]
        swap = refs[2 * n:2 * n + ns], refs[4 * n + ns:4 * n + 2 * ns], *refs[4 * n + 2 * ns:]
        if ns:
            @pl.when(functools.reduce(jnp.logical_and, [i == 0 for i in ids]))
            def _():
                for cp in _sibling_swap_copies(*swap):
                    cp.start()

        for k in range(n):
            total = refs[k][...] + refs[n + k][...]
            refs[2 * n + ns + k][...] = total
            refs[3 * n + ns + k][...] = _bf(total)

        if ns:
            @pl.when(functools.reduce(jnp.logical_and, [i == e for i, e in zip(ids, last)]))
            def _():
                for cp in _sibling_swap_copies(*swap):
                    cp.wait()

    outs = pl.pallas_call(
        body,
        name=name,
        grid_spec=pltpu.PrefetchScalarGridSpec(
            num_scalar_prefetch=1, grid=grid, in_specs=specs_a + specs_b + [ANY_SPEC] * ns,
            out_specs=out_specs * 2 + [ANY_SPEC] * ns,
            scratch_shapes=[pltpu.SemaphoreType.DMA((ns, 4)), pltpu.SemaphoreType.DMA((ns, 4))] if ns else []),
        out_shape=tuple(jax.ShapeDtypeStruct(s, dt) for dt in (F32, BF16) for s in out_shapes)
        + tuple(jax.ShapeDtypeStruct(g.shape[1:], F32) for g in swap_srcs),
        compiler_params=pltpu.CompilerParams(dimension_semantics=("arbitrary",) * len(grid),
                                             vmem_limit_bytes=VMEM_LIMIT_V7X),
    )(coords, *a_arrays, *b_arrays, *swap_srcs)
    return list(outs[:n]), list(outs[n:2 * n]), list(outs[2 * n:])


def _final_add_adamw(coords, s1, r, ws, ms, vs, n_split=4):
    n = len(s1)

    def body(co_ref, *refs):
        for k in range(n):
            s_ref, r_ref, w_ref, m_ref, v_ref = (refs[q * n + k] for q in range(5))
            g_ref, d_ref, nm_ref, nv_ref = (refs[(5 + q) * n + k] for q in range(4))
            g = ((s_ref[...] + r_ref[0].astype(F32)) + r_ref[1].astype(F32)) + r_ref[2].astype(F32)
            g_ref[...] = g
            delta, m, v = _adamw(w_ref[...], g, m_ref[...], v_ref[...])
            d_ref[...] = delta
            nm_ref[...] = m
            nv_ref[...] = v

    def shard_spec(a):
        rows, cols = a.shape
        return pl.BlockSpec((rows // n_split, cols), lambda i, co: (i, 0))

    def mine_spec(a):
        rows, cols = a.shape[2:]
        return pl.BlockSpec((None, None, rows // n_split, cols), lambda i, co: (co[0], co[1], i, 0))

    def recv_spec(a):
        rows, cols = a.shape[1:]
        return pl.BlockSpec((3, rows // n_split, cols), lambda i, co: (0, i, 0))

    in_specs = ([mine_spec(a) for a in s1] + [recv_spec(a) for a in r] + [shard_spec(a) for a in ws] * 3)
    out_specs = [shard_spec(a) for a in ws] * 4
    outs = pl.pallas_call(
        body,
        name="grad_final_adamw",
        grid_spec=pltpu.PrefetchScalarGridSpec(num_scalar_prefetch=1, grid=(n_split,), in_specs=in_specs,
                                               out_specs=out_specs),
        out_shape=tuple(jax.ShapeDtypeStruct(a.shape, F32) for a in ws) * 4,
        compiler_params=pltpu.CompilerParams(dimension_semantics=("arbitrary",), vmem_limit_bytes=VMEM_LIMIT_V7X),
    )(coords, *s1, *r, *ws, *ms, *vs)
    return [tuple(outs[q * n + k] for q in range(4)) for k in range(n)]


def _sibling_add(tag, g5, r1, coords, swap_srcs=(), n_split=4):
    shapes = [g.shape[3:] for g in g5]
    spec_g = [pl.BlockSpec((None, None, None, s[0] // n_split, s[1]), lambda i, j, k, co: (i, j, co[2], k, 0))
              for s in shapes]
    spec_r = [pl.BlockSpec((None, None, s[0] // n_split, s[1]), lambda i, j, k, co: (i, j, k, 0)) for s in shapes]
    return _pair_add("grad_add_core_" + tag, coords, (2, 2, n_split), spec_g, spec_r, spec_r,
                     [(2, 2, *s) for s in shapes], g5, r1, list(swap_srcs))


def _tail_exchange(big, partials, pick_mine, y_first, dmod3):
    n, nb = len(partials), len(big)
    big_shapes = [g.shape[1:] for g in big]
    big5 = [g.reshape(2, 2, 2, *s) for g, s in zip(big, big_shapes)]
    flips = _ChipExchangeSteps.FLIPS

    def body(*refs):
        g5, p_in, dm_ref = refs[:nb], refs[nb:nb + n], refs[nb + n]
        outs = refs[nb + n + 1:2 * nb + 2 * n + 2]
        g_out, sums, dm2d = outs[:nb], outs[nb:nb + n], outs[nb + n]
        scratch = refs[2 * nb + 2 * n + 2:]
        s1, stage, chip_recv = scratch[:nb], scratch[nb:2 * nb], scratch[2 * nb:3 * nb]
        acc, rbuf = scratch[3 * nb:3 * nb + n], scratch[3 * nb + n:3 * nb + 2 * n]
        (dm_recv, send_sems, recv_sems, dm_send_sems, dm_recv_sems, sib_send, sib_recv, chip_send,
         chip_recv_sems) = scratch[3 * nb + 2 * n:]
        x, y, c = _coords()
        me = 4 * x + 2 * y + c
        sibling = (x, y, 1 - c)
        dm_copies = [
            pltpu.make_async_remote_copy(dm_ref.at[me ^ k], dm_recv.at[k], dm_send_sems.at[k], dm_recv_sems.at[k],
                                         device_id=_peer(k), device_id_type=MESH)
            for k in range(1, N_DEV)
        ]
        for cp in dm_copies:
            cp.start()
        sib_copies = _sibling_swap_copies(g5, s1, sib_send, sib_recv)
        for cp in sib_copies:
            cp.start()
        for a in range(n):
            acc[a][...] = p_in[a][...]

        def small_phase(ph, peers):
            copies = [
                pltpu.make_async_remote_copy(acc[a], rbuf[a].at[ph], send_sems.at[ph, a], recv_sems.at[ph, a],
                                             device_id=peers[y_first[a]], device_id_type=MESH)
                for a in range(n)
            ]
            for cp in copies:
                cp.start()
            for cp in copies:
                cp.wait()
            for a in range(n):
                acc[a][...] = acc[a][...] + rbuf[a][ph]

        small_phase(0, (sibling, sibling))
        for cp in sib_copies:
            cp.wait()
        for a in range(nb):
            for xs in range(2):
                for ys in range(2):
                    total = g5[a][xs, ys, c] + s1[a][xs, ys]
                    s1[a][xs, ys] = total
                    stage[a][xs, ys] = _bf(total)
        chip_copies = [
            pltpu.make_async_remote_copy(stage[a].at[x ^ fx, y ^ fy], chip_recv[a].at[j], chip_send.at[a, j],
                                         chip_recv_sems.at[a, j], device_id=(x ^ fx, y ^ fy, c), device_id_type=MESH)
            for a in range(nb) for j, (fx, fy) in enumerate(flips)
        ]
        for cp in chip_copies:
            cp.start()
        x_peer, y_peer = (1 - x, y, c), (x, 1 - y, c)
        small_phase(1, (x_peer, y_peer))
        small_phase(2, (y_peer, x_peer))
        for a in range(n):
            sums[a][...] = acc[a][me] if pick_mine[a] else acc[a][...]
        dm2d[...] = jnp.zeros(dm2d.shape, F32)
        dm2d[0:1, :] = dm_ref[me]
        for cp in dm_copies:
            cp.wait()
        for k in range(1, N_DEV):
            dm2d[k:k + 1, :] = dm_recv[k]
        for cp in chip_copies:
            cp.wait()
        for a in range(nb):
            g_out[a][...] = ((s1[a][x, y] + chip_recv[a][0].astype(F32)) + chip_recv[a][1].astype(F32)) \
                + chip_recv[a][2].astype(F32)

    out_shapes = tuple(jax.ShapeDtypeStruct(s, F32) for s in big_shapes) + tuple(
        jax.ShapeDtypeStruct(p.shape[1:] if pk else p.shape, F32) for p, pk in zip(partials, pick_mine))
    outs = pl.pallas_call(
        body,
        name="tail_exchange",
        out_shape=out_shapes + (jax.ShapeDtypeStruct((2 * N_DEV, MOD_COLS), F32),),
        in_specs=[VMEM_SPEC] * (nb + n + 1),
        out_specs=(VMEM_SPEC,) * (nb + n + 1),
        scratch_shapes=[pltpu.VMEM((2, 2, *s), F32) for s in big_shapes]
        + [pltpu.VMEM((2, 2, *s), BF16) for s in big_shapes]
        + [pltpu.VMEM((3, *s), BF16) for s in big_shapes]
        + [pltpu.VMEM(p.shape, F32) for p in partials]
        + [pltpu.VMEM((3, *p.shape), F32) for p in partials]
        + [pltpu.VMEM((N_DEV, 1, MOD_COLS), F32), pltpu.SemaphoreType.DMA((3, n)), pltpu.SemaphoreType.DMA((3, n)),
           pltpu.SemaphoreType.DMA((N_DEV,)), pltpu.SemaphoreType.DMA((N_DEV,)),
           pltpu.SemaphoreType.DMA((nb, 4)), pltpu.SemaphoreType.DMA((nb, 4)),
           pltpu.SemaphoreType.DMA((nb, 3)), pltpu.SemaphoreType.DMA((nb, 3))],
        compiler_params=pltpu.CompilerParams(vmem_limit_bytes=VMEM_LIMIT_V7X),
    )(*big5, *partials, dmod3)
    return list(outs[:nb]), list(outs[nb:nb + n]), outs[nb + n]


def _small_update(grads, ws, ms, vs, scx, dm2d, w_ada, m_ada, v_ada, loss_lanes):
    n = len(grads)

    def body(*refs):
        g_in, w_in, m_in, v_in = (refs[q * n:(q + 1) * n] for q in range(4))
        scx_ref, dm_ref, wa_ref, ma_ref, va_ref, ll_ref = refs[4 * n:4 * n + 6]
        outs = refs[4 * n + 6:]
        g_out, d_out, nm_out, nv_out = (outs[q * (n + 1):(q + 1) * (n + 1)] for q in range(4))
        loss_ref = outs[4 * (n + 1)]
        for a in range(n + 1):
            if a < n:
                g, w, m, v = g_in[a][...], w_in[a][...], m_in[a][...], v_in[a][...]
            else:
                g = _mm_tn(_bf(scx_ref[...]), _bf(dm_ref[...]))
                w, m, v = wa_ref[...], ma_ref[...], va_ref[...]
            g_out[a][...] = g
            delta, m, v = _adamw(w, g, m, v)
            d_out[a][...] = delta
            nm_out[a][...] = m
            nv_out[a][...] = v
        loss_ref[...] = jnp.sum(ll_ref[...], axis=1, keepdims=True) * (0.5 / D)

    w_shapes = tuple(jax.ShapeDtypeStruct(w.shape, F32) for w in list(ws) + [w_ada])
    outs = pl.pallas_call(
        body,
        name="small_update",
        out_shape=w_shapes * 4 + (jax.ShapeDtypeStruct((1, 1), F32),),
        in_specs=[VMEM_SPEC] * (4 * n + 6),
        out_specs=(VMEM_SPEC,) * (4 * (n + 1) + 1),
        compiler_params=pltpu.CompilerParams(vmem_limit_bytes=VMEM_LIMIT_V7X),
    )(*grads, *ws, *ms, *vs, scx, dm2d, w_ada, m_ada, v_ada, loss_lanes)
    return [tuple(outs[q * (n + 1) + k] for q in range(4)) for k in range(n + 1)], outs[4 * (n + 1)]


def kernel(x, c, w_ada, b_ada, pre_mix_g, post_mix_g, w_in, sgu_norm_g, w_spatial, b_spatial, w_pool, pool_scale, w_out, pre_ffn_g, post_ffn_g, w_up, conv_w, conv_b, w_down, loss_target, m_w_ada, m_b_ada, m_pre_mix_g, m_post_mix_g, m_w_in, m_sgu_norm_g, m_w_spatial, m_b_spatial, m_w_pool, m_pool_scale, m_w_out, m_pre_ffn_g, m_post_ffn_g, m_w_up, m_conv_w, m_conv_b, m_w_down, v_w_ada, v_b_ada, v_pre_mix_g, v_post_mix_g, v_w_in, v_sgu_norm_g, v_w_spatial, v_b_spatial, v_w_pool, v_pool_scale, v_w_out, v_pre_ffn_g, v_post_ffn_g, v_w_up, v_conv_w, v_conv_b, v_w_down):
    t_len = x.shape[1]
    ts = min(256, t_len)
    ts_mix = min(512, t_len)
    ts_w = min(1024, t_len)
    coords = jnp.stack([lax.axis_index("x"), lax.axis_index("y"), lax.axis_index("c")]).astype(jnp.int32)

    w_in_t, w_up_t = w_in[0].T, w_up[0].T
    mod3, scx, (g_in, g_out, g_down) = _prologue(c, w_ada[0], b_ada.reshape(N_DEV, 1, MOD_COLS),
                                                 [w_in_t, w_out[0], w_down[0]], [BF16, BF16, BF16])
    mod = mod3.reshape(N_MOD, D)
    w_in_tb = g_in.reshape(IN_WIDTH, D)
    w_out_b = g_out.reshape(D, D)
    conv_b8 = conv_b.reshape(N_DEV, FF_CHUNK)
    b_sp_t = b_spatial[0].T

    x2d, tgt = x[0], loss_target[0]
    (x1, proj, mixed), (g_up, g_cw) = _mix_fwd(
        x2d, mod, pre_mix_g, post_mix_g, w_in_tb, sgu_norm_g[0], w_spatial[0], b_sp_t, w_pool[0], pool_scale, w_out_b,
        ts_mix, [w_up_t, conv_w[0]], [w_up.shape[1:], conv_w.shape[1:]], [BF16, F32])
    w_down_b = g_down.reshape(FF, D)
    up, f, dx2, loss_lanes = _ffn_fwd(x1, tgt, mod, pre_ffn_g, post_ffn_g, g_up, g_cw, conv_b8, w_down_b, ts)

    (dx1, dup, act, df, h2, dmod_f, d_pre_ffn, d_post_ffn, d_cb8, d_cw8) = _ffn_bwd(
        dx2, f, x1, up, mod, pre_ffn_g, post_ffn_g, g_up, g_cw, conv_b8, w_down_b, ts)
    gw_up = _wgrad_up(h2, dup, ts_w).reshape(2, 2, 2, FF_CHUNK, D)
    gw_down, r1_up = _wgrad_down(act, df, ts_w, gw_up)
    gw_down = gw_down.reshape(2, 2, 2, FF // N_DEV, D)
    s1_up, s1_up_b, r1_down = _sibling_add("up", [gw_up], [r1_up], coords, swap_srcs=[gw_down])
    s1_down, s1_down_b, _ = _sibling_add("down", [gw_down], r1_down, coords)
    s1_ffn, s1_ffn_b = s1_up + s1_down, s1_up_b + s1_down_b
    ((grad_x, gw_in, gw_out, dmod_m, d_pre_mix, d_post_mix, d_sgn, d_wsp, d_bsp_t, d_wpool, d_ps), r_ffn) = _mix_bwd(
        dx1, x2d, proj, mixed, mod, pre_mix_g, post_mix_g, w_in_tb, sgu_norm_g[0], w_spatial[0], b_sp_t,
        w_pool[0], pool_scale, w_out_b, ts_mix, s1_ffn_b)
    gw_in = gw_in.reshape(N_DEV, IN_WIDTH // N_DEV, D)
    gw_out = gw_out.reshape(N_DEV, D // N_DEV, D)

    big = _final_add_adamw(coords, s1_ffn, list(r_ffn), [w_up_t, w_down[0]], [m_w_up[0].T, m_w_down[0]],
                           [v_w_up[0].T, v_w_down[0]])
    r_up, r_down = tuple(a.T[None] for a in big[0]), tuple(a[None] for a in big[1])

    dmod = jnp.concatenate([dmod_m, dmod_f], axis=0)
    names = ["b_ada", "pre_mix_g", "post_mix_g", "sgu_norm_g", "w_spatial", "b_spatial", "w_pool", "pool_scale",
             "pre_ffn_g", "post_ffn_g", "conv_w", "conv_b"]
    partials = [dmod.reshape(1, N_MOD * D), d_pre_mix, d_post_mix, d_sgn, d_wsp, d_bsp_t.T, d_wpool, d_ps,
                d_pre_ffn, d_post_ffn, d_cw8, d_cb8.reshape(1, 2 * FF), loss_lanes]
    small_w = [b_ada, pre_mix_g, post_mix_g, sgu_norm_g[0], w_spatial[0], b_spatial[0], w_pool[0], pool_scale,
               pre_ffn_g, post_ffn_g, conv_w[0], conv_b]
    small_m = [m_b_ada, m_pre_mix_g, m_post_mix_g, m_sgu_norm_g[0], m_w_spatial[0], m_b_spatial[0], m_w_pool[0],
               m_pool_scale, m_pre_ffn_g, m_post_ffn_g, m_conv_w[0], m_conv_b]
    small_v = [v_b_ada, v_pre_mix_g, v_post_mix_g, v_sgu_norm_g[0], v_w_spatial[0], v_b_spatial[0], v_w_pool[0],
               v_pool_scale, v_pre_ffn_g, v_post_ffn_g, v_conv_w[0], v_conv_b]
    g_mix, sums, dm2d = _tail_exchange([gw_in, gw_out], partials, [nm == "conv_w" for nm in names] + [False],
                                       [int(nm == "w_pool") for nm in names] + [0],
                                       dmod.reshape(N_DEV, 1, MOD_COLS))
    small, loss11 = _small_update(
        sums[:-1] + g_mix, small_w + [w_in_t, w_out[0]], small_m + [m_w_in[0].T, m_w_out[0]],
        small_v + [v_w_in[0].T, v_w_out[0]], scx, dm2d, w_ada[0], m_w_ada[0], v_w_ada[0], sums[-1])
    loss = loss11.reshape(())
    lead = {"sgu_norm_g", "w_spatial", "b_spatial", "w_pool", "conv_w", "w_in", "w_out", "w_ada"}
    res = {nm: tuple((a.T if nm == "w_in" else a)[None] if nm in lead else a for a in four)
           for nm, four in zip(names + ["w_in", "w_out", "w_ada"], small)}
    res.update(w_up=r_up, w_down=r_down)

    order = ["w_ada", "b_ada", "pre_mix_g", "post_mix_g", "w_in", "sgu_norm_g", "w_spatial", "b_spatial", "w_pool",
             "pool_scale", "w_out", "pre_ffn_g", "post_ffn_g", "w_up", "conv_w", "conv_b", "w_down"]
    return (loss, grad_x[None], *[res[nm][0] for nm in order], *[res[nm][1] for nm in order],
            *[res[nm][2] for nm in order], *[res[nm][3] for nm in order])
```

```python
import functools
import math

import jax
import jax.numpy as jnp
from jax import lax
from jax.experimental import pallas as pl
from jax.experimental.pallas import tpu as pltpu

F32 = jnp.float32
BF16 = jnp.bfloat16
MESH = pl.DeviceIdType.MESH

EPS = 1e-6
D = 1024
HEAD = 128
N_HEAD = 4
A_WIDTH = 512
B_WIDTH = 512
IN_WIDTH = 1536
WINDOWS = (2, 4, 8, 16)
CHUNK = 64
FF = 2816
N_DEV = 8
FF_CHUNK = 704
N_MOD = 6
MOD_COLS = 768

ADAM_LR = 0.001
ADAM_B1 = 0.9
ADAM_B2 = 0.999
ADAM_EPS = 1e-08
ADAM_WD = 0.01
ADAM_STEP = 10

VMEM_LIMIT_V7X = 62 * 1024 * 1024
HALO = 8
POOL_HALO = 16

VMEM_SPEC = pl.BlockSpec(memory_space=pltpu.VMEM)
ANY_SPEC = pl.BlockSpec(memory_space=pl.ANY)


def _bf(x):
    return x.astype(BF16)


def _mm(a, b):
    return jnp.dot(a, b, preferred_element_type=F32)


def _mm_nt(a, b):
    return lax.dot_general(a, b, (((1,), (1,)), ((), ())), preferred_element_type=F32)


def _mm_tn(a, b):
    return lax.dot_general(a, b, (((0,), (0,)), ((), ())), preferred_element_type=F32)


def _rstd(x):
    return lax.rsqrt(jnp.mean(x * x, axis=-1, keepdims=True) + EPS)


def _sum0(x):
    return jnp.sum(x, axis=0, keepdims=True)


def _rowmean(x):
    return jnp.mean(x, axis=-1, keepdims=True)


_GELU_K = math.sqrt(2.0 / math.pi)


def _gelu_and_grad(x):
    x2 = x * x
    th = jnp.tanh(_GELU_K * (x + 0.044715 * (x * x2)))
    cdf = 0.5 * th + 0.5
    grad = cdf + x * (1.0 - th * th) * ((0.5 * _GELU_K) + (1.5 * 0.044715 * _GELU_K) * x2)
    return x * cdf, grad


def _gelu(x):
    return x * (0.5 * (1.0 + jnp.tanh(_GELU_K * (x + 0.044715 * (x * x * x)))))


def _sigmoid(x):
    return 0.5 * jnp.tanh(0.5 * x) + 0.5


def _sgu_mask():
    ri = lax.broadcasted_iota(jnp.int32, (HEAD, HEAD), 0)
    ci = lax.broadcasted_iota(jnp.int32, (HEAD, HEAD), 1)
    return (ci // CHUNK) <= (ri // CHUNK)


def _window_sum(ext, w, trailing):
    n = ext.shape[0]
    s, k = ext, 1
    while k < w:
        s = s + pltpu.roll(s, k if trailing else n - k, 0)
        k *= 2
    return s


def _inv_count(row0, n, w):
    t = row0 + lax.broadcasted_iota(jnp.int32, (n, 1), 0)
    return 1.0 / jnp.minimum(t + 1, w).astype(F32)


def _shift_down(v, before, k):
    rows = lax.broadcasted_iota(jnp.int32, before.shape, 0)
    r = pltpu.roll(v, k, 0)
    top = jnp.where(rows < k, pltpu.roll(before, k, 0), r[0:HALO])
    return jnp.concatenate([top, r[HALO:]], axis=0)


def _shift_up(v, after, k):
    n = v.shape[0]
    rows = lax.broadcasted_iota(jnp.int32, after.shape, 0)
    r = pltpu.roll(v, n - k, 0)
    bottom = jnp.where(rows >= HALO - k, pltpu.roll(after, HALO - k, 0), r[n - HALO:])
    return jnp.concatenate([r[:n - HALO], bottom], axis=0)


def _adamw(w, g, m, v):
    m = ADAM_B1 * m + (1.0 - ADAM_B1) * g
    v = ADAM_B2 * v + (1.0 - ADAM_B2) * (g * g)
    m_hat = m / (1.0 - ADAM_B1 ** ADAM_STEP)
    v_hat = v / (1.0 - ADAM_B2 ** ADAM_STEP)
    delta = -ADAM_LR * (m_hat / (jnp.sqrt(v_hat) + ADAM_EPS) + ADAM_WD * w)
    return delta, m, v


def _coords():
    return lax.axis_index("x"), lax.axis_index("y"), lax.axis_index("c")


def _peer(k):
    x, y, c = _coords()
    return (x ^ ((k >> 2) & 1), y ^ ((k >> 1) & 1), c ^ (k & 1))


def _my_index():
    x, y, c = _coords()
    return 4 * x + 2 * y + c


def _adaln_modulation(c_ref, w_ref, b_ref, mod_ref, scx_ref, scbuf, stage, recv, send_sems, recv_sems):
    me = _my_index()
    cv = c_ref[...]
    scbuf[0] = cv * _sigmoid(cv)
    first = [
        pltpu.make_async_remote_copy(scbuf.at[0], scbuf.at[k], send_sems.at[0, k], recv_sems.at[0, k],
                                     device_id=_peer(k), device_id_type=MESH)
        for k in range(1, N_DEV)
    ]
    for cp in first:
        cp.start()
    for cp in first:
        cp.wait()
    scx_ref[...] = jnp.zeros(scx_ref.shape, F32)
    for k in range(N_DEV):
        scx_ref[k:k + 1, :] = scbuf[k]
    prod = _mm(_bf(scx_ref[...]), _bf(w_ref[...]))
    for k in range(N_DEV):
        stage[k] = prod[k:k + 1, :] + b_ref[me]
    second = [
        pltpu.make_async_remote_copy(stage.at[k], recv.at[k], send_sems.at[1, k], recv_sems.at[1, k],
                                     device_id=_peer(k), device_id_type=MESH)
        for k in range(1, N_DEV)
    ]
    for cp in second:
        cp.start()
    mod_ref[me] = stage[0]
    for cp in second:
        cp.wait()
    for k in range(1, N_DEV):
        mod_ref[me ^ k] = recv[k]


class _GatherSteps:
    def __init__(self, ins, outs, stages, send_sems, recv_sems, local_sems):
        self.ins, self.outs, self.stages = ins, outs, stages
        self.send_sems, self.recv_sems, self.local_sems = send_sems, recv_sems, local_sems
        x, y, c = _coords()
        self.c = c
        self.me, self.sibling = (x, y, c), (x, y, 1 - c)
        self.chips = [(1 - x, y), (x, 1 - y), (1 - x, 1 - y)]

    def _copy(self, a, k, block, to, from_stage=False):
        dst = self.outs[a].at[4 * block[0] + 2 * block[1] + block[2]]
        return pltpu.make_async_remote_copy(self.stages[a] if from_stage else dst, dst, self.send_sems.at[a, k],
                                            self.recv_sems.at[a, k], device_id=to, device_id_type=MESH)

    def _local(self, a):
        me = self.me
        return pltpu.make_async_copy(self.stages[a], self.outs[a].at[4 * me[0] + 2 * me[1] + me[2]],
                                     self.local_sems.at[a])

    def _first(self, a):
        cps = [self._copy(a, 0, self.me, self.sibling, from_stage=True)]
        return cps + [self._copy(a, 1 + j, self.me, (*chip, self.c), from_stage=True)
                      for j, chip in enumerate(self.chips)]

    def _passed(self, a, j):
        return self._copy(a, 4 + j, (*self.chips[j], self.c), self.sibling)

    def start(self):
        for a in range(len(self.ins)):
            block = self.ins[a][...]
            if block.shape != self.stages[a].shape:
                block = block.T
            self.stages[a][...] = block.astype(self.stages[a].dtype)
            self._local(a).start()
            for cp in self._first(a):
                cp.start()

    def forward(self):
        for a in range(len(self.ins)):
            for j, chip in enumerate(self.chips):
                self._copy(a, 1 + j, (*chip, self.c), self.me).wait_recv()
                self._passed(a, j).start()

    def finish(self):
        for a in range(len(self.ins)):
            self._copy(a, 0, self.sibling, self.me).wait_recv()
            for j, chip in enumerate(self.chips):
                self._copy(a, 4 + j, (*chip, 1 - self.c), self.me).wait_recv()
            for cp in self._first(a) + [self._passed(a, j) for j in range(3)]:
                cp.wait_send()
            self._local(a).wait()


def _gather_scratch(shapes, out_dtypes):
    n = len(shapes)
    return ([pltpu.VMEM(s, dt) for s, dt in zip(shapes, out_dtypes)]
            + [pltpu.SemaphoreType.DMA((n, 7)), pltpu.SemaphoreType.DMA((n, 7)), pltpu.SemaphoreType.DMA((n,))])


def _gather_out_shapes(shapes, out_dtypes):
    return tuple(jax.ShapeDtypeStruct((N_DEV, *s), dt) for s, dt in zip(shapes, out_dtypes))


def _prologue(c_row, w_ada, b_ada3, shards, out_dtypes):
    n = len(shards)

    def body(*refs):
        c_ref, w_ref, b_ref = refs[:3]
        mod_ref, scx_ref = refs[3 + n:5 + n]
        gather = _GatherSteps(refs[3:3 + n], refs[5 + n:5 + 2 * n], refs[5 + 2 * n:5 + 3 * n],
                              *refs[5 + 3 * n:8 + 3 * n])
        gather.start()
        _adaln_modulation(c_ref, w_ref, b_ref, mod_ref, scx_ref, *refs[8 + 3 * n:])
        gather.forward()
        gather.finish()

    outs = pl.pallas_call(
        body,
        name="prologue",
        out_shape=(jax.ShapeDtypeStruct((N_DEV, 1, MOD_COLS), F32), jax.ShapeDtypeStruct((2 * N_DEV, D), F32))
        + _gather_out_shapes([s.shape for s in shards], out_dtypes),
        in_specs=[VMEM_SPEC] * (3 + n),
        out_specs=(VMEM_SPEC, VMEM_SPEC) + (ANY_SPEC,) * n,
        scratch_shapes=_gather_scratch([s.shape for s in shards], out_dtypes) + [
            pltpu.VMEM((N_DEV, 1, D), F32),
            pltpu.VMEM((N_DEV, 1, MOD_COLS), F32),
            pltpu.VMEM((N_DEV, 1, MOD_COLS), F32),
            pltpu.SemaphoreType.DMA((2, N_DEV)),
            pltpu.SemaphoreType.DMA((2, N_DEV)),
        ],
        compiler_params=pltpu.CompilerParams(vmem_limit_bytes=VMEM_LIMIT_V7X),
    )(c_row, w_ada, b_ada3, *shards)
    return outs[0], outs[1], outs[2:]


class _ChipExchangeSteps:
    FLIPS = ((1, 0), (0, 1), (1, 1))

    def __init__(self, srcs, dsts, send_sems, recv_sems):
        self.srcs, self.dsts, self.send_sems, self.recv_sems = srcs, dsts, send_sems, recv_sems

    def _copies(self):
        x, y, c = _coords()
        out = []
        for a in range(len(self.srcs)):
            for j, (fx, fy) in enumerate(self.FLIPS):
                k = 3 * a + j
                out.append(pltpu.make_async_remote_copy(
                    self.srcs[a].at[x ^ fx, y ^ fy], self.dsts[a].at[j], self.send_sems.at[k], self.recv_sems.at[k],
                    device_id=(x ^ fx, y ^ fy, c), device_id_type=MESH))
        return out

    def start(self):
        for cp in self._copies():
            cp.start()

    def finish(self):
        for cp in self._copies():
            cp.wait()


def _mix_fwd(x, mod, g_pre, g_post, w_in_t, sgn, w_sp, b_sp_t, w_pool, p_scale, w_out_b, ts, shards, shard_shapes,
             shard_dtypes):
    t_len = x.shape[0]
    nt, nb = t_len // ts, ts // HEAD
    ns = len(shards)

    def body(*refs):
        (x_ref, mod_ref, g1_ref, g2_ref, win_ref, sgn_ref, ws_ref, bst_ref, wp_ref, ps_ref, wout_ref) = refs[:11]
        x1_ref, proj_ref, mixed_ref = refs[11 + ns:14 + ns]
        pbuf, cat = refs[14 + 2 * ns:16 + 2 * ns]
        gather = _GatherSteps(refs[11:11 + ns], refs[14 + ns:14 + 2 * ns], refs[16 + 2 * ns:16 + 3 * ns],
                              *refs[16 + 3 * ns:])
        i = pl.program_id(0)

        @pl.when(i == 0)
        def _():
            pbuf[0:POOL_HALO, :] = jnp.zeros((POOL_HALO, B_WIDTH), F32)
            gather.start()

        @pl.when(i == (7 * nt) // 8)
        def _():
            gather.forward()

        xv = x_ref[...]
        sh, sc, gm = mod_ref[0:1, :], mod_ref[1:2, :], mod_ref[2:3, :]
        h = (xv * _rstd(xv) * g1_ref[...]) * (1.0 + sc) + sh
        proj_ref[...] = _mm_nt(_bf(h), win_ref[...])
        pbuf[POOL_HALO:POOL_HALO + ts, :] = proj_ref[:, 2 * A_WIDTH:]
        smask = _sgu_mask()
        for hd in range(N_HEAD):
            u = _gelu(proj_ref[:, hd * HEAD:(hd + 1) * HEAD])
            v = _gelu(proj_ref[:, A_WIDTH + hd * HEAD:A_WIDTH + (hd + 1) * HEAD])
            vn = _bf(v * _rstd(v) * sgn_ref[hd:hd + 1, :])
            wm = _bf(jnp.where(smask, ws_ref[hd], 0.0))
            bias = bst_ref[:, hd:hd + 1]
            for b in range(nb):
                rows = slice(b * HEAD, (b + 1) * HEAD)
                z = _mm(wm, vn[rows]) + bias
                cat[rows, hd * HEAD:(hd + 1) * HEAD] = _bf(u[rows] * z)
        for g, w in enumerate(WINDOWS):
            cols = slice(g * HEAD, (g + 1) * HEAD)
            ext = pbuf[:, cols]
            pooled = _window_sum(ext, w, True)[POOL_HALO:] * _inv_count(i * ts, ts, w) - ext[POOL_HALO:]
            cat[:, A_WIDTH + g * HEAD:A_WIDTH + (g + 1) * HEAD] = _bf(_mm(_bf(pooled), _bf(wp_ref[g])) * ps_ref[:, cols])
        pbuf[0:POOL_HALO, :] = pbuf[ts:ts + POOL_HALO, :]
        mixed = _mm(cat[...], wout_ref[...])
        mixed_ref[...] = mixed
        x1_ref[...] = xv + gm * (mixed * _rstd(mixed) * g2_ref[...])

        @pl.when(i == nt - 1)
        def _():
            gather.finish()

    tile = lambda wid: pl.BlockSpec((ts, wid), lambda i: (i, 0))
    outs = pl.pallas_call(
        body,
        name="mix_fwd",
        grid=(nt,),
        out_shape=(jax.ShapeDtypeStruct((t_len, D), F32), jax.ShapeDtypeStruct((t_len, IN_WIDTH), F32),
                   jax.ShapeDtypeStruct((t_len, D), F32)) + _gather_out_shapes(shard_shapes, shard_dtypes),
        in_specs=[tile(D)] + [VMEM_SPEC] * (10 + ns),
        out_specs=(tile(D), tile(IN_WIDTH), tile(D)) + (ANY_SPEC,) * ns,
        scratch_shapes=[pltpu.VMEM((POOL_HALO + ts, B_WIDTH), F32), pltpu.VMEM((ts, D), BF16)]
        + _gather_scratch(shard_shapes, shard_dtypes),
        compiler_params=pltpu.CompilerParams(dimension_semantics=("arbitrary",), vmem_limit_bytes=VMEM_LIMIT_V7X),
    )(x, mod, g_pre, g_post, w_in_t, sgn, w_sp, b_sp_t, w_pool, p_scale, w_out_b, *shards)
    return outs[:3], outs[3:]


def _ffn_fwd(x1, target, mod, g_pre, g_post, w_up_b, conv_w8, conv_b8, w_down_b, ts):
    t_len = x1.shape[0]
    nt = t_len // ts

    def body(x1_ref, tgt_ref, mod_ref, g3_ref, g4_ref, wup_ref, cw_ref, cb_ref, wdown_ref,
             up_ref, f_ref, dx2_ref, loss_ref, ucarry):
        i = pl.program_id(0)

        @pl.when(i == 0)
        def _():
            ucarry[...] = jnp.zeros(ucarry.shape, F32)
            loss_ref[...] = jnp.zeros(loss_ref.shape, F32)

        x1v = x1_ref[...]
        sh, sc, gf = mod_ref[3:4, :], mod_ref[4:5, :], mod_ref[5:6, :]
        h2 = _bf((x1v * _rstd(x1v) * g3_ref[...]) * (1.0 + sc) + sh)
        half = N_DEV // 2

        def up_pair(j):
            return [_mm(h2, wup_ref[jj]) for jj in (j, j + half)]

        f = jnp.zeros((ts, D), F32)
        ups = up_pair(0)
        for j in range(half):
            nxt = up_pair(j + 1) if j + 1 < half else None
            ys = []
            for up, jj in zip(ups, (j, j + half)):
                up_ref[jj] = up
                before = ucarry[jj]
                ucarry[jj] = up[ts - HALO:, :]
                cw = cw_ref[jj]
                ys.append(cb_ref[jj:jj + 1, :] + _shift_down(up, before, 2) * cw[0:1, :]
                          + _shift_down(up, before, 1) * cw[1:2, :] + up * cw[2:3, :])
            gate, val = ys
            act = gate * _sigmoid(gate) * val
            f = f + _mm(_bf(act), wdown_ref[j * FF_CHUNK:(j + 1) * FF_CHUNK, :])
            ups = nxt
        f_ref[...] = f
        x2 = x1v + gf * (f * _rstd(f) * g4_ref[...])
        err = x2 - tgt_ref[...]
        loss_ref[...] += _sum0(err * err)
        dx2_ref[...] = err * (1.0 / D)

    tile = pl.BlockSpec((ts, D), lambda i: (i, 0))
    return pl.pallas_call(
        body,
        name="ffn_fwd",
        grid=(nt,),
        out_shape=(jax.ShapeDtypeStruct((N_DEV, t_len, FF_CHUNK), F32), jax.ShapeDtypeStruct((t_len, D), F32),
                   jax.ShapeDtypeStruct((t_len, D), F32), jax.ShapeDtypeStruct((1, D), F32)),
        in_specs=[tile, tile] + [VMEM_SPEC] * 7,
        out_specs=(pl.BlockSpec((N_DEV, ts, FF_CHUNK), lambda i: (0, i, 0)), tile, tile,
                   pl.BlockSpec((1, D), lambda i: (0, 0))),
        scratch_shapes=[pltpu.VMEM((N_DEV, HALO, FF_CHUNK), F32)],
        compiler_params=pltpu.CompilerParams(dimension_semantics=("arbitrary",), vmem_limit_bytes=VMEM_LIMIT_V7X),
    )(x1, target, mod, g_pre, g_post, w_up_b, conv_w8, conv_b8, w_down_b)


def _ffn_bwd(dx2, f, x1, up, mod, g_pre, g_post, w_up_b, conv_w8, conv_b8, w_down_b, ts):
    t_len = x1.shape[0]
    nt = t_len // ts
    half = N_DEV // 2

    def body(dx2_ref, f_ref, x1_ref, up_ref, halo_ref, mod_ref, g3_ref, g4_ref, wup_ref, cw_ref, cb_ref, wdown_ref,
             dx1_ref, dup_ref, act_ref, df_ref, h2_ref, dmod_ref, dg3_ref, dg4_ref, dcb_ref, dcw_ref,
             dycarry, dh2acc):
        i = pl.program_id(0)
        r = nt - 1 - i

        @pl.when(i == 0)
        def _():
            for ref in (dmod_ref, dg3_ref, dg4_ref, dcb_ref, dcw_ref, dycarry):
                ref[...] = jnp.zeros(ref.shape, F32)

        dx2v, fv, x1v = dx2_ref[...], f_ref[...], x1_ref[...]
        sh, sc, gf = mod_ref[3:4, :], mod_ref[4:5, :], mod_ref[5:6, :]
        g3, g4 = g3_ref[...], g4_ref[...]
        rstd4 = _rstd(fv)
        fh = fv * rstd4
        dmod_ref[2:3, :] += _sum0(dx2v * (fh * g4))
        dr = dx2v * gf
        dg4_ref[...] += _sum0(dr * fh)
        dfh = dr * g4
        dfb = _bf(rstd4 * (dfh - fh * _rowmean(dfh * fh)))
        df_ref[...] = dfb
        rstd3 = _rstd(x1v)
        xh = x1v * rstd3
        n3 = xh * g3
        h2_ref[...] = _bf(n3 * (1.0 + sc) + sh)
        dh2acc[...] = jnp.zeros((ts, D), F32)
        keep = jnp.where(r > 0, 1.0, 0.0).astype(F32)

        def dact_of(j):
            return _mm_nt(dfb, wdown_ref[j * FF_CHUNK:(j + 1) * FF_CHUNK, :])

        dact_next = dact_of(0)
        for j in range(half):
            dact = dact_next
            if j + 1 < half:
                dact_next = dact_of(j + 1)
            ys = []
            for jj in (j, j + half):
                before = halo_ref[jj] * keep
                upc = up_ref[jj]
                cw = cw_ref[jj]
                ys.append(cb_ref[jj:jj + 1, :] + _shift_down(upc, before, 2) * cw[0:1, :]
                          + _shift_down(upc, before, 1) * cw[1:2, :] + upc * cw[2:3, :])
            gate, val = ys
            sg = _sigmoid(gate)
            silu = gate * sg
            act_ref[j] = _bf(silu * val)
            dys = (dact * val * (sg + silu * (1.0 - sg)), dact * silu)
            for q, jj in enumerate((j, j + half)):
                dy = dys[q]
                cw = cw_ref[jj]
                dcb_ref[jj:jj + 1, :] += _sum0(dy)
                after = dycarry[jj]
                dycarry[jj] = dy[0:HALO, :]
                dy1, dy2 = _shift_up(dy, after, 1), _shift_up(dy, after, 2)
                upc = up_ref[jj]
                dcw_ref[jj, 0:1, :] += _sum0(dy2 * upc)
                dcw_ref[jj, 1:2, :] += _sum0(dy1 * upc)
                dcw_ref[jj, 2:3, :] += _sum0(dy * upc)
                dup = _bf(dy * cw[2:3, :] + dy1 * cw[1:2, :] + dy2 * cw[0:1, :])
                dup_ref[jj] = dup
                dh2acc[...] += _mm_nt(dup, wup_ref[jj])
        dh2 = dh2acc[...]
        dmod_ref[0:1, :] += _sum0(dh2)
        dmod_ref[1:2, :] += _sum0(dh2 * n3)
        dn3 = dh2 * (1.0 + sc)
        dg3_ref[...] += _sum0(dn3 * xh)
        dxh = dn3 * g3
        dx1_ref[...] = dx2v + rstd3 * (dxh - xh * _rowmean(dxh * xh))

    tile = pl.BlockSpec((ts, D), lambda i: (nt - 1 - i, 0))
    chunked = lambda n: pl.BlockSpec((n, ts, FF_CHUNK), lambda i: (0, nt - 1 - i, 0))
    halo = pl.BlockSpec((N_DEV, HALO, FF_CHUNK), lambda i: (0, jnp.maximum((nt - 1 - i) * (ts // HALO) - 1, 0), 0))
    const = lambda *shape: pl.BlockSpec(shape, lambda i: (0,) * len(shape))
    return pl.pallas_call(
        body,
        name="ffn_bwd",
        grid=(nt,),
        out_shape=(jax.ShapeDtypeStruct((t_len, D), F32), jax.ShapeDtypeStruct((N_DEV, t_len, FF_CHUNK), BF16),
                   jax.ShapeDtypeStruct((half, t_len, FF_CHUNK), BF16), jax.ShapeDtypeStruct((t_len, D), BF16),
                   jax.ShapeDtypeStruct((t_len, D), BF16), jax.ShapeDtypeStruct((3, D), F32),
                   jax.ShapeDtypeStruct((1, D), F32), jax.ShapeDtypeStruct((1, D), F32),
                   jax.ShapeDtypeStruct((N_DEV, FF_CHUNK), F32), jax.ShapeDtypeStruct((N_DEV, 3, FF_CHUNK), F32)),
        in_specs=[tile, tile, tile, chunked(N_DEV), halo] + [VMEM_SPEC] * 7,
        out_specs=(tile, chunked(N_DEV), chunked(half), tile, tile, const(3, D), const(1, D), const(1, D),
                   const(N_DEV, FF_CHUNK), const(N_DEV, 3, FF_CHUNK)),
        scratch_shapes=[pltpu.VMEM((N_DEV, HALO, FF_CHUNK), F32), pltpu.VMEM((ts, D), F32)],
        compiler_params=pltpu.CompilerParams(dimension_semantics=("arbitrary",), vmem_limit_bytes=VMEM_LIMIT_V7X),
    )(dx2, f, x1, up, up, mod, g_pre, g_post, w_up_b, conv_w8, conv_b8, w_down_b)


def _wgrad_up(h2, dup, ts):
    t_len = h2.shape[0]
    nt, half = t_len // ts, N_DEV // 2

    def body(h2_ref, dup_ref, out_ref):
        @pl.when(pl.program_id(1) == 0)
        def _():
            out_ref[...] = jnp.zeros(out_ref.shape, F32)

        for q in range(half):
            out_ref[q] += _mm_tn(dup_ref[q], h2_ref[...])

    return pl.pallas_call(
        body,
        name="wgrad_up",
        grid=(2, nt),
        out_shape=jax.ShapeDtypeStruct((N_DEV, FF_CHUNK, D), F32),
        in_specs=[pl.BlockSpec((ts, D), lambda g, t: (t, 0)), pl.BlockSpec((half, ts, FF_CHUNK), lambda g, t: (g, t, 0))],
        out_specs=pl.BlockSpec((half, FF_CHUNK, D), lambda g, t: (g, 0, 0)),
        compiler_params=pltpu.CompilerParams(dimension_semantics=("arbitrary", "arbitrary"),
                                             vmem_limit_bytes=VMEM_LIMIT_V7X),
    )(h2, dup)


def _sibling_swap_copies(srcs, dsts, send_sems, recv_sems):
    x, y, c = _coords()
    return [
        pltpu.make_async_remote_copy(srcs[a].at[xs, ys, 1 - c], dsts[a].at[xs, ys], send_sems.at[a, 2 * xs + ys],
                                     recv_sems.at[a, 2 * xs + ys], device_id=(x, y, 1 - c), device_id_type=MESH)
        for a in range(len(srcs)) for xs in range(2) for ys in range(2)
    ]


def _wgrad_down(act, df, ts, swap_src):
    t_len = df.shape[0]
    nt, half = t_len // ts, N_DEV // 2

    def body(act_ref, df_ref, src_ref, out_ref, dst_ref, send_sems, recv_sems):
        t = pl.program_id(0)

        @pl.when(t == 0)
        def _():
            for cp in _sibling_swap_copies([src_ref], [dst_ref], send_sems, recv_sems):
                cp.start()
            out_ref[...] = jnp.zeros(out_ref.shape, F32)

        for q in range(half):
            out_ref[q] += _mm_tn(act_ref[q], df_ref[...])

        @pl.when(t == nt - 1)
        def _():
            for cp in _sibling_swap_copies([src_ref], [dst_ref], send_sems, recv_sems):
                cp.wait()

    return pl.pallas_call(
        body,
        name="wgrad_down",
        grid=(nt,),
        out_shape=(jax.ShapeDtypeStruct((half, FF_CHUNK, D), F32), jax.ShapeDtypeStruct(swap_src.shape[1:], F32)),
        in_specs=[pl.BlockSpec((half, ts, FF_CHUNK), lambda t: (0, t, 0)), pl.BlockSpec((ts, D), lambda t: (t, 0)),
                  ANY_SPEC],
        out_specs=(pl.BlockSpec((half, FF_CHUNK, D), lambda t: (0, 0, 0)), ANY_SPEC),
        scratch_shapes=[pltpu.SemaphoreType.DMA((1, 4)), pltpu.SemaphoreType.DMA((1, 4))],
        compiler_params=pltpu.CompilerParams(dimension_semantics=("arbitrary",), vmem_limit_bytes=VMEM_LIMIT_V7X),
    )(act, df, swap_src)


def _mix_bwd(dx1, x, proj, mixed, mod, g_pre, g_post, w_in_t, sgn, w_sp, b_sp_t, w_pool, p_scale, w_out_b, ts, rs_srcs):
    t_len = x.shape[0]
    nt, nb = t_len // ts, ts // HEAD
    nr = len(rs_srcs)

    def body(*refs):
        (dx1_ref, x_ref, proj_ref, halo_ref, mixed_ref, mod_ref, g1_ref, g2_ref, win_ref, sgn_ref, ws_ref,
         bst_ref, wp_ref, ps_ref, wout_ref) = refs[:15]
        (gx_ref, dwin_ref, dwout_ref, dmod_ref, dg1_ref, dg2_ref, dsgn_ref, dws_ref, dbst_ref, dwp_ref,
         dps_ref) = refs[15 + nr:26 + nr]
        pbuf, dwsbuf, cat, dproj, dcat = refs[26 + 2 * nr:31 + 2 * nr]
        exchange = _ChipExchangeSteps(refs[15:15 + nr], refs[26 + nr:26 + 2 * nr], *refs[31 + 2 * nr:])
        i = pl.program_id(0)
        r = nt - 1 - i

        @pl.when(i == 0)
        def _():
            exchange.start()
            for ref in (dwin_ref, dwout_ref, dmod_ref, dg1_ref, dg2_ref, dsgn_ref, dws_ref, dbst_ref, dwp_ref, dps_ref):
                ref[...] = jnp.zeros(ref.shape, F32)
            dwsbuf[ts:ts + POOL_HALO, :] = jnp.zeros((POOL_HALO, B_WIDTH), F32)

        xv, dx1v, mixed = x_ref[...], dx1_ref[...], mixed_ref[...]
        sh, sc, gm = mod_ref[0:1, :], mod_ref[1:2, :], mod_ref[2:3, :]
        g1, g2 = g1_ref[...], g2_ref[...]
        rstd2 = _rstd(mixed)
        mh = mixed * rstd2
        dmod_ref[2:3, :] += _sum0(dx1v * (mh * g2))
        dr = dx1v * gm
        dg2_ref[...] += _sum0(dr * mh)
        dmh = dr * g2
        dmb = _bf(rstd2 * (dmh - mh * _rowmean(dmh * mh)))
        dcat[...] = _mm_nt(dmb, wout_ref[...])
        smask = _sgu_mask()
        for hd in range(N_HEAD):
            ucols = slice(hd * HEAD, (hd + 1) * HEAD)
            vcols = slice(A_WIDTH + hd * HEAD, A_WIDTH + (hd + 1) * HEAD)
            u, du_dp = _gelu_and_grad(proj_ref[:, ucols])
            v, dv_dp = _gelu_and_grad(proj_ref[:, vcols])
            rs = _rstd(v)
            vhat = v * rs
            gn = sgn_ref[hd:hd + 1, :]
            vn = _bf(vhat * gn)
            wm = _bf(jnp.where(smask, ws_ref[hd], 0.0))
            bias = bst_ref[:, hd:hd + 1]
            dzsum = jnp.zeros((HEAD, HEAD), F32)
            dwm = jnp.zeros((HEAD, HEAD), F32)
            dvn_parts = []
            for b in range(nb):
                rows = slice(b * HEAD, (b + 1) * HEAD)
                z = _mm(wm, vn[rows]) + bias
                da = dcat[rows, ucols]
                cat[rows, ucols] = _bf(u[rows] * z)
                dz = da * u[rows]
                dzsum = dzsum + dz
                dzb = _bf(dz)
                dwm = dwm + _mm_nt(dzb, vn[rows])
                dvn_parts.append(_mm_tn(wm, dzb))
                dproj[rows, ucols] = _bf((da * z) * du_dp[rows])
            dvn = jnp.concatenate(dvn_parts, axis=0)
            dsgn_ref[hd:hd + 1, :] += _sum0(dvn * vhat)
            dvh = dvn * gn
            dproj[:, vcols] = _bf((rs * (dvh - vhat * _rowmean(dvh * vhat))) * dv_dp)
            dws_ref[hd] += jnp.where(smask, dwm, 0.0)
            dbst_ref[:, hd:hd + 1] += jnp.sum(dzsum, axis=1, keepdims=True)
        keep = jnp.where(r > 0, 1.0, 0.0).astype(F32)
        pbuf[0:POOL_HALO, :] = halo_ref[...] * keep
        pbuf[POOL_HALO:POOL_HALO + ts, :] = proj_ref[:, 2 * A_WIDTH:]
        for g, w in enumerate(WINDOWS):
            cols = slice(g * HEAD, (g + 1) * HEAD)
            ccols = slice(A_WIDTH + g * HEAD, A_WIDTH + (g + 1) * HEAD)
            pcols = slice(2 * A_WIDTH + g * HEAD, 2 * A_WIDTH + (g + 1) * HEAD)
            wpg = _bf(wp_ref[g])
            psg = ps_ref[:, cols]
            ext = pbuf[:, cols]
            inv = _inv_count(r * ts, ts, w)
            pb = _bf(_window_sum(ext, w, True)[POOL_HALO:] * inv - ext[POOL_HALO:])
            yb = _mm(pb, wpg)
            dob = dcat[:, ccols]
            cat[:, ccols] = _bf(yb * psg)
            dps_ref[:, cols] += _sum0(dob * yb)
            dyb = _bf(dob * psg)
            dwp_ref[g] += _mm_tn(pb, dyb)
            dpooled = _mm_nt(dyb, wpg)
            dwsbuf[0:ts, cols] = dpooled * inv
            dproj[:, pcols] = _bf(_window_sum(dwsbuf[:, cols], w, False)[0:ts] - dpooled)
        dwsbuf[ts:ts + POOL_HALO, :] = dwsbuf[0:POOL_HALO, :]
        dpb = dproj[...]
        rstd1 = _rstd(xv)
        xh = xv * rstd1
        n1 = xh * g1
        dwin_ref[...] += _mm_tn(dpb, _bf(n1 * (1.0 + sc) + sh))
        dwout_ref[...] += _mm_tn(cat[...], dmb)
        dh = _mm(dpb, win_ref[...])
        dmod_ref[0:1, :] += _sum0(dh)
        dmod_ref[1:2, :] += _sum0(dh * n1)
        dn1 = dh * (1.0 + sc)
        dg1_ref[...] += _sum0(dn1 * xh)
        dxh = dn1 * g1
        gx_ref[...] = dx1v + rstd1 * (dxh - xh * _rowmean(dxh * xh))

        @pl.when(i == nt - 1)
        def _():
            exchange.finish()

    tile = lambda wid: pl.BlockSpec((ts, wid), lambda i: (nt - 1 - i, 0))
    halo = pl.BlockSpec((POOL_HALO, B_WIDTH),
                        lambda i: (jnp.maximum((nt - 1 - i) * (ts // POOL_HALO) - 1, 0), 2 * A_WIDTH // B_WIDTH))
    const = lambda *shape: pl.BlockSpec(shape, lambda i: (0,) * len(shape))
    resident = lambda *shape: pl.BlockSpec(shape, lambda i: (0,) * len(shape), pipeline_mode=pl.Buffered(1))
    outs = pl.pallas_call(
        body,
        name="mix_bwd",
        grid=(nt,),
        out_shape=(jax.ShapeDtypeStruct((t_len, D), F32), jax.ShapeDtypeStruct((IN_WIDTH, D), F32),
                   jax.ShapeDtypeStruct((D, D), F32), jax.ShapeDtypeStruct((3, D), F32),
                   jax.ShapeDtypeStruct((1, D), F32), jax.ShapeDtypeStruct((1, D), F32),
                   jax.ShapeDtypeStruct((N_HEAD, HEAD), F32), jax.ShapeDtypeStruct((N_HEAD, HEAD, HEAD), F32),
                   jax.ShapeDtypeStruct((HEAD, N_HEAD), F32), jax.ShapeDtypeStruct((N_HEAD, HEAD, HEAD), F32),
                   jax.ShapeDtypeStruct((1, B_WIDTH), F32))
        + tuple(jax.ShapeDtypeStruct((3, *s.shape[2:]), s.dtype) for s in rs_srcs),
        in_specs=[tile(D), tile(D), tile(IN_WIDTH), halo, tile(D)] + [VMEM_SPEC] * 10 + [ANY_SPEC] * nr,
        out_specs=(tile(D), resident(IN_WIDTH, D), resident(D, D), const(3, D), const(1, D), const(1, D),
                   const(N_HEAD, HEAD), const(N_HEAD, HEAD, HEAD), const(HEAD, N_HEAD), const(N_HEAD, HEAD, HEAD),
                   const(1, B_WIDTH)) + (ANY_SPEC,) * nr,
        scratch_shapes=[pltpu.VMEM((POOL_HALO + ts, B_WIDTH), F32), pltpu.VMEM((ts + POOL_HALO, B_WIDTH), F32),
                        pltpu.VMEM((ts, D), BF16), pltpu.VMEM((ts, IN_WIDTH), BF16), pltpu.VMEM((ts, D), F32),
                        pltpu.SemaphoreType.DMA((3 * nr,)), pltpu.SemaphoreType.DMA((3 * nr,))],
        compiler_params=pltpu.CompilerParams(dimension_semantics=("arbitrary",), vmem_limit_bytes=VMEM_LIMIT_V7X),
    )(dx1, x, proj, proj, mixed, mod, g_pre, g_post, w_in_t, sgn, w_sp, b_sp_t, w_pool, p_scale, w_out_b, *rs_srcs)
    return outs[:11], outs[11:]


def _pair_add(name, coords, grid, specs_a, specs_b, out_specs, out_shapes, a_arrays, b_arrays, swap_srcs):
    n, ns = len(a_arrays), len(swap_srcs)
    last = tuple(g - 1 for g in grid)

    def body(co_ref, *refs):
        ids = [pl.program_id(d) for d in range(len(grid))]
        swap = refs[2 * n:2 * n + ns], refs[4 * n + ns:4 * n + 2 * ns], *refs[4 * n + 2 * ns:]
        if ns:
            @pl.when(functools.reduce(jnp.logical_and, [i == 0 for i in ids]))
            def _():
                for cp in _sibling_swap_copies(*swap):
                    cp.start()

        for k in range(n):
            total = refs[k][...] + refs[n + k][...]
            refs[2 * n + ns + k][...] = total
            refs[3 * n + ns + k][...] = _bf(total)

        if ns:
            @pl.when(functools.reduce(jnp.logical_and, [i == e for i, e in zip(ids, last)]))
            def _():
                for cp in _sibling_swap_copies(*swap):
                    cp.wait()

    outs = pl.pallas_call(
        body,
        name=name,
        grid_spec=pltpu.PrefetchScalarGridSpec(
            num_scalar_prefetch=1, grid=grid, in_specs=specs_a + specs_b + [ANY_SPEC] * ns,
            out_specs=out_specs * 2 + [ANY_SPEC] * ns,
            scratch_shapes=[pltpu.SemaphoreType.DMA((ns, 4)), pltpu.SemaphoreType.DMA((ns, 4))] if ns else []),
        out_shape=tuple(jax.ShapeDtypeStruct(s, dt) for dt in (F32, BF16) for s in out_shapes)
        + tuple(jax.ShapeDtypeStruct(g.shape[1:], F32) for g in swap_srcs),
        compiler_params=pltpu.CompilerParams(dimension_semantics=("arbitrary",) * len(grid),
                                             vmem_limit_bytes=VMEM_LIMIT_V7X),
    )(coords, *a_arrays, *b_arrays, *swap_srcs)
    return list(outs[:n]), list(outs[n:2 * n]), list(outs[2 * n:])


def _final_add_adamw(coords, s1, r, ws, ms, vs, n_split=4):
    n = len(s1)

    def body(co_ref, *refs):
        for k in range(n):
            s_ref, r_ref, w_ref, m_ref, v_ref = (refs[q * n + k] for q in range(5))
            g_ref, d_ref, nm_ref, nv_ref = (refs[(5 + q) * n + k] for q in range(4))
            g = ((s_ref[...] + r_ref[0].astype(F32)) + r_ref[1].astype(F32)) + r_ref[2].astype(F32)
            g_ref[...] = g
            delta, m, v = _adamw(w_ref[...], g, m_ref[...], v_ref[...])
            d_ref[...] = delta
            nm_ref[...] = m
            nv_ref[...] = v

    def shard_spec(a):
        rows, cols = a.shape
        return pl.BlockSpec((rows // n_split, cols), lambda i, co: (i, 0))

    def mine_spec(a):
        rows, cols = a.shape[2:]
        return pl.BlockSpec((None, None, rows // n_split, cols), lambda i, co: (co[0], co[1], i, 0))

    def recv_spec(a):
        rows, cols = a.shape[1:]
        return pl.BlockSpec((3, rows // n_split, cols), lambda i, co: (0, i, 0))

    in_specs = ([mine_spec(a) for a in s1] + [recv_spec(a) for a in r] + [shard_spec(a) for a in ws] * 3)
    out_specs = [shard_spec(a) for a in ws] * 4
    outs = pl.pallas_call(
        body,
        name="grad_final_adamw",
        grid_spec=pltpu.PrefetchScalarGridSpec(num_scalar_prefetch=1, grid=(n_split,), in_specs=in_specs,
                                               out_specs=out_specs),
        out_shape=tuple(jax.ShapeDtypeStruct(a.shape, F32) for a in ws) * 4,
        compiler_params=pltpu.CompilerParams(dimension_semantics=("arbitrary",), vmem_limit_bytes=VMEM_LIMIT_V7X),
    )(coords, *s1, *r, *ws, *ms, *vs)
    return [tuple(outs[q * n + k] for q in range(4)) for k in range(n)]


def _sibling_add(tag, g5, r1, coords, swap_srcs=(), n_split=4):
    shapes = [g.shape[3:] for g in g5]
    spec_g = [pl.BlockSpec((None, None, None, s[0] // n_split, s[1]), lambda i, j, k, co: (i, j, co[2], k, 0))
              for s in shapes]
    spec_r = [pl.BlockSpec((None, None, s[0] // n_split, s[1]), lambda i, j, k, co: (i, j, k, 0)) for s in shapes]
    return _pair_add("grad_add_core_" + tag, coords, (2, 2, n_split), spec_g, spec_r, spec_r,
                     [(2, 2, *s) for s in shapes], g5, r1, list(swap_srcs))


def _tail_exchange(big, partials, pick_mine, y_first, dmod3):
    n, nb = len(partials), len(big)
    big_shapes = [g.shape[1:] for g in big]
    big5 = [g.reshape(2, 2, 2, *s) for g, s in zip(big, big_shapes)]
    flips = _ChipExchangeSteps.FLIPS

    def body(*refs):
        g5, p_in, dm_ref = refs[:nb], refs[nb:nb + n], refs[nb + n]
        outs = refs[nb + n + 1:2 * nb + 2 * n + 2]
        g_out, sums, dm2d = outs[:nb], outs[nb:nb + n], outs[nb + n]
        scratch = refs[2 * nb + 2 * n + 2:]
        s1, stage, chip_recv = scratch[:nb], scratch[nb:2 * nb], scratch[2 * nb:3 * nb]
        acc, rbuf = scratch[3 * nb:3 * nb + n], scratch[3 * nb + n:3 * nb + 2 * n]
        (dm_recv, send_sems, recv_sems, dm_send_sems, dm_recv_sems, sib_send, sib_recv, chip_send,
         chip_recv_sems) = scratch[3 * nb + 2 * n:]
        x, y, c = _coords()
        me = 4 * x + 2 * y + c
        sibling = (x, y, 1 - c)
        dm_copies = [
            pltpu.make_async_remote_copy(dm_ref.at[me ^ k], dm_recv.at[k], dm_send_sems.at[k], dm_recv_sems.at[k],
                                         device_id=_peer(k), device_id_type=MESH)
            for k in range(1, N_DEV)
        ]
        for cp in dm_copies:
            cp.start()
        sib_copies = _sibling_swap_copies(g5, s1, sib_send, sib_recv)
        for cp in sib_copies:
            cp.start()
        for a in range(n):
            acc[a][...] = p_in[a][...]

        def small_phase(ph, peers):
            copies = [
                pltpu.make_async_remote_copy(acc[a], rbuf[a].at[ph], send_sems.at[ph, a], recv_sems.at[ph, a],
                                             device_id=peers[y_first[a]], device_id_type=MESH)
                for a in range(n)
            ]
            for cp in copies:
                cp.start()
            for cp in copies:
                cp.wait()
            for a in range(n):
                acc[a][...] = acc[a][...] + rbuf[a][ph]

        small_phase(0, (sibling, sibling))
        for cp in sib_copies:
            cp.wait()
        for a in range(nb):
            for xs in range(2):
                for ys in range(2):
                    total = g5[a][xs, ys, c] + s1[a][xs, ys]
                    s1[a][xs, ys] = total
                    stage[a][xs, ys] = _bf(total)
        chip_copies = [
            pltpu.make_async_remote_copy(stage[a].at[x ^ fx, y ^ fy], chip_recv[a].at[j], chip_send.at[a, j],
                                         chip_recv_sems.at[a, j], device_id=(x ^ fx, y ^ fy, c), device_id_type=MESH)
            for a in range(nb) for j, (fx, fy) in enumerate(flips)
        ]
        for cp in chip_copies:
            cp.start()
        x_peer, y_peer = (1 - x, y, c), (x, 1 - y, c)
        small_phase(1, (x_peer, y_peer))
        small_phase(2, (y_peer, x_peer))
        for a in range(n):
            sums[a][...] = acc[a][me] if pick_mine[a] else acc[a][...]
        dm2d[...] = jnp.zeros(dm2d.shape, F32)
        dm2d[0:1, :] = dm_ref[me]
        for cp in dm_copies:
            cp.wait()
        for k in range(1, N_DEV):
            dm2d[k:k + 1, :] = dm_recv[k]
        for cp in chip_copies:
            cp.wait()
        for a in range(nb):
            g_out[a][...] = ((s1[a][x, y] + chip_recv[a][0].astype(F32)) + chip_recv[a][1].astype(F32)) \
                + chip_recv[a][2].astype(F32)

    out_shapes = tuple(jax.ShapeDtypeStruct(s, F32) for s in big_shapes) + tuple(
        jax.ShapeDtypeStruct(p.shape[1:] if pk else p.shape, F32) for p, pk in zip(partials, pick_mine))
    outs = pl.pallas_call(
        body,
        name="tail_exchange",
        out_shape=out_shapes + (jax.ShapeDtypeStruct((2 * N_DEV, MOD_COLS), F32),),
        in_specs=[VMEM_SPEC] * (nb + n + 1),
        out_specs=(VMEM_SPEC,) * (nb + n + 1),
        scratch_shapes=[pltpu.VMEM((2, 2, *s), F32) for s in big_shapes]
        + [pltpu.VMEM((2, 2, *s), BF16) for s in big_shapes]
        + [pltpu.VMEM((3, *s), BF16) for s in big_shapes]
        + [pltpu.VMEM(p.shape, F32) for p in partials]
        + [pltpu.VMEM((3, *p.shape), F32) for p in partials]
        + [pltpu.VMEM((N_DEV, 1, MOD_COLS), F32), pltpu.SemaphoreType.DMA((3, n)), pltpu.SemaphoreType.DMA((3, n)),
           pltpu.SemaphoreType.DMA((N_DEV,)), pltpu.SemaphoreType.DMA((N_DEV,)),
           pltpu.SemaphoreType.DMA((nb, 4)), pltpu.SemaphoreType.DMA((nb, 4)),
           pltpu.SemaphoreType.DMA((nb, 3)), pltpu.SemaphoreType.DMA((nb, 3))],
        compiler_params=pltpu.CompilerParams(vmem_limit_bytes=VMEM_LIMIT_V7X),
    )(*big5, *partials, dmod3)
    return list(outs[:nb]), list(outs[nb:nb + n]), outs[nb + n]


def _small_update(grads, ws, ms, vs, scx, dm2d, w_ada, m_ada, v_ada, loss_lanes):
    n = len(grads)

    def body(*refs):
        g_in, w_in, m_in, v_in = (refs[q * n:(q + 1) * n] for q in range(4))
        scx_ref, dm_ref, wa_ref, ma_ref, va_ref, ll_ref = refs[4 * n:4 * n + 6]
        outs = refs[4 * n + 6:]
        g_out, d_out, nm_out, nv_out = (outs[q * (n + 1):(q + 1) * (n + 1)] for q in range(4))
        loss_ref = outs[4 * (n + 1)]
        for a in range(n + 1):
            if a < n:
                g, w, m, v = g_in[a][...], w_in[a][...], m_in[a][...], v_in[a][...]
            else:
                g = _mm_tn(_bf(scx_ref[...]), _bf(dm_ref[...]))
                w, m, v = wa_ref[...], ma_ref[...], va_ref[...]
            g_out[a][...] = g
            delta, m, v = _adamw(w, g, m, v)
            d_out[a][...] = delta
            nm_out[a][...] = m
            nv_out[a][...] = v
        loss_ref[...] = jnp.sum(ll_ref[...], axis=1, keepdims=True) * (0.5 / D)

    w_shapes = tuple(jax.ShapeDtypeStruct(w.shape, F32) for w in list(ws) + [w_ada])
    outs = pl.pallas_call(
        body,
        name="small_update",
        out_shape=w_shapes * 4 + (jax.ShapeDtypeStruct((1, 1), F32),),
        in_specs=[VMEM_SPEC] * (4 * n + 6),
        out_specs=(VMEM_SPEC,) * (4 * (n + 1) + 1),
        compiler_params=pltpu.CompilerParams(vmem_limit_bytes=VMEM_LIMIT_V7X),
    )(*grads, *ws, *ms, *vs, scx, dm2d, w_ada, m_ada, v_ada, loss_lanes)
    return [tuple(outs[q * (n + 1) + k] for q in range(4)) for k in range(n + 1)], outs[4 * (n + 1)]


def kernel(x, c, w_ada, b_ada, pre_mix_g, post_mix_g, w_in, sgu_norm_g, w_spatial, b_spatial, w_pool, pool_scale, w_out, pre_ffn_g, post_ffn_g, w_up, conv_w, conv_b, w_down, loss_target, m_w_ada, m_b_ada, m_pre_mix_g, m_post_mix_g, m_w_in, m_sgu_norm_g, m_w_spatial, m_b_spatial, m_w_pool, m_pool_scale, m_w_out, m_pre_ffn_g, m_post_ffn_g, m_w_up, m_conv_w, m_conv_b, m_w_down, v_w_ada, v_b_ada, v_pre_mix_g, v_post_mix_g, v_w_in, v_sgu_norm_g, v_w_spatial, v_b_spatial, v_w_pool, v_pool_scale, v_w_out, v_pre_ffn_g, v_post_ffn_g, v_w_up, v_conv_w, v_conv_b, v_w_down):
    t_len = x.shape[1]
    ts = min(256, t_len)
    ts_mix = min(512, t_len)
    ts_w = min(1024, t_len)
    coords = jnp.stack([lax.axis_index("x"), lax.axis_index("y"), lax.axis_index("c")]).astype(jnp.int32)

    w_in_t, w_up_t = w_in[0].T, w_up[0].T
    mod3, scx, (g_in, g_out) = _prologue(c, w_ada[0], b_ada.reshape(N_DEV, 1, MOD_COLS), [w_in_t, w_out[0]],
                                         [BF16, BF16])
    mod = mod3.reshape(N_MOD, D)
    w_in_tb = g_in.reshape(IN_WIDTH, D)
    w_out_b = g_out.reshape(D, D)
    conv_b8 = conv_b.reshape(N_DEV, FF_CHUNK)
    b_sp_t = b_spatial[0].T

    x2d, tgt = x[0], loss_target[0]
    (x1, proj, mixed), (g_up, g_down, g_cw) = _mix_fwd(
        x2d, mod, pre_mix_g, post_mix_g, w_in_tb, sgu_norm_g[0], w_spatial[0], b_sp_t, w_pool[0], pool_scale, w_out_b,
        ts_mix, [w_up_t, w_down[0], conv_w[0]], [w_up.shape[1:], w_down.shape[1:], conv_w.shape[1:]], [BF16, BF16, F32])
    w_down_b = g_down.reshape(FF, D)
    up, f, dx2, loss_lanes = _ffn_fwd(x1, tgt, mod, pre_ffn_g, post_ffn_g, g_up, g_cw, conv_b8, w_down_b, ts)

    (dx1, dup, act, df, h2, dmod_f, d_pre_ffn, d_post_ffn, d_cb8, d_cw8) = _ffn_bwd(
        dx2, f, x1, up, mod, pre_ffn_g, post_ffn_g, g_up, g_cw, conv_b8, w_down_b, ts)
    gw_up = _wgrad_up(h2, dup, ts_w).reshape(2, 2, 2, FF_CHUNK, D)
    gw_down, r1_up = _wgrad_down(act, df, ts_w, gw_up)
    gw_down = gw_down.reshape(2, 2, 2, FF // N_DEV, D)
    s1_up, s1_up_b, r1_down = _sibling_add("up", [gw_up], [r1_up], coords, swap_srcs=[gw_down])
    s1_down, s1_down_b, _ = _sibling_add("down", [gw_down], r1_down, coords)
    s1_ffn, s1_ffn_b = s1_up + s1_down, s1_up_b + s1_down_b
    ((grad_x, gw_in, gw_out, dmod_m, d_pre_mix, d_post_mix, d_sgn, d_wsp, d_bsp_t, d_wpool, d_ps), r_ffn) = _mix_bwd(
        dx1, x2d, proj, mixed, mod, pre_mix_g, post_mix_g, w_in_tb, sgu_norm_g[0], w_spatial[0], b_sp_t,
        w_pool[0], pool_scale, w_out_b, ts_mix, s1_ffn_b)
    gw_in = gw_in.reshape(N_DEV, IN_WIDTH // N_DEV, D)
    gw_out = gw_out.reshape(N_DEV, D // N_DEV, D)

    big = _final_add_adamw(coords, s1_ffn, list(r_ffn), [w_up_t, w_down[0]], [m_w_up[0].T, m_w_down[0]],
                           [v_w_up[0].T, v_w_down[0]])
    r_up, r_down = tuple(a.T[None] for a in big[0]), tuple(a[None] for a in big[1])

    dmod = jnp.concatenate([dmod_m, dmod_f], axis=0)
    names = ["b_ada", "pre_mix_g", "post_mix_g", "sgu_norm_g", "w_spatial", "b_spatial", "w_pool", "pool_scale",
             "pre_ffn_g", "post_ffn_g", "conv_w", "conv_b"]
    partials = [dmod.reshape(1, N_MOD * D), d_pre_mix, d_post_mix, d_sgn, d_wsp, d_bsp_t.T, d_wpool, d_ps,
                d_pre_ffn, d_post_ffn, d_cw8, d_cb8.reshape(1, 2 * FF), loss_lanes]
    small_w = [b_ada, pre_mix_g, post_mix_g, sgu_norm_g[0], w_spatial[0], b_spatial[0], w_pool[0], pool_scale,
               pre_ffn_g, post_ffn_g, conv_w[0], conv_b]
    small_m = [m_b_ada, m_pre_mix_g, m_post_mix_g, m_sgu_norm_g[0], m_w_spatial[0], m_b_spatial[0], m_w_pool[0],
               m_pool_scale, m_pre_ffn_g, m_post_ffn_g, m_conv_w[0], m_conv_b]
    small_v = [v_b_ada, v_pre_mix_g, v_post_mix_g, v_sgu_norm_g[0], v_w_spatial[0], v_b_spatial[0], v_w_pool[0],
               v_pool_scale, v_pre_ffn_g, v_post_ffn_g, v_conv_w[0], v_conv_b]
    g_mix, sums, dm2d = _tail_exchange([gw_in, gw_out], partials, [nm == "conv_w" for nm in names] + [False],
                                       [int(nm == "w_pool") for nm in names] + [0],
                                       dmod.reshape(N_DEV, 1, MOD_COLS))
    small, loss11 = _small_update(
        sums[:-1] + g_mix, small_w + [w_in_t, w_out[0]], small_m + [m_w_in[0].T, m_w_out[0]],
        small_v + [v_w_in[0].T, v_w_out[0]], scx, dm2d, w_ada[0], m_w_ada[0], v_w_ada[0], sums[-1])
    loss = loss11.reshape(())
    lead = {"sgu_norm_g", "w_spatial", "b_spatial", "w_pool", "conv_w", "w_in", "w_out", "w_ada"}
    res = {nm: tuple((a.T if nm == "w_in" else a)[None] if nm in lead else a for a in four)
           for nm, four in zip(names + ["w_in", "w_out", "w_ada"], small)}
    res.update(w_up=r_up, w_down=r_down)

    order = ["w_ada", "b_ada", "pre_mix_g", "post_mix_g", "w_in", "sgu_norm_g", "w_spatial", "b_spatial", "w_pool",
             "pool_scale", "w_out", "pre_ffn_g", "post_ffn_g", "w_up", "conv_w", "conv_b", "w_down"]
    return (loss, grad_x[None], *[res[nm][0] for nm in order], *[res[nm][1] for nm in order],
            *[res[nm][2] for nm in order], *[res[nm][3] for nm in order])
```

```python
import functools
import math

import jax
import jax.numpy as jnp
from jax import lax
from jax.experimental import pallas as pl
from jax.experimental.pallas import tpu as pltpu

F32 = jnp.float32
BF16 = jnp.bfloat16
MESH = pl.DeviceIdType.MESH

EPS = 1e-6
D = 1024
HEAD = 128
N_HEAD = 4
A_WIDTH = 512
B_WIDTH = 512
IN_WIDTH = 1536
WINDOWS = (2, 4, 8, 16)
CHUNK = 64
FF = 2816
N_DEV = 8
FF_CHUNK = 704
N_MOD = 6
MOD_COLS = 768

ADAM_LR = 0.001
ADAM_B1 = 0.9
ADAM_B2 = 0.999
ADAM_EPS = 1e-08
ADAM_WD = 0.01
ADAM_STEP = 10

VMEM_LIMIT_V7X = 62 * 1024 * 1024
HALO = 8
POOL_HALO = 16

VMEM_SPEC = pl.BlockSpec(memory_space=pltpu.VMEM)
ANY_SPEC = pl.BlockSpec(memory_space=pl.ANY)


def _bf(x):
    return x.astype(BF16)


def _mm(a, b):
    return jnp.dot(a, b, preferred_element_type=F32)


def _mm_nt(a, b):
    return lax.dot_general(a, b, (((1,), (1,)), ((), ())), preferred_element_type=F32)


def _mm_tn(a, b):
    return lax.dot_general(a, b, (((0,), (0,)), ((), ())), preferred_element_type=F32)


def _rstd(x):
    return lax.rsqrt(jnp.mean(x * x, axis=-1, keepdims=True) + EPS)


def _sum0(x):
    return jnp.sum(x, axis=0, keepdims=True)


def _rowmean(x):
    return jnp.mean(x, axis=-1, keepdims=True)


_GELU_K = math.sqrt(2.0 / math.pi)


def _gelu_and_grad(x):
    x2 = x * x
    th = jnp.tanh(_GELU_K * (x + 0.044715 * (x * x2)))
    cdf = 0.5 * th + 0.5
    grad = cdf + x * (1.0 - th * th) * ((0.5 * _GELU_K) + (1.5 * 0.044715 * _GELU_K) * x2)
    return x * cdf, grad


def _gelu(x):
    return x * (0.5 * (1.0 + jnp.tanh(_GELU_K * (x + 0.044715 * (x * x * x)))))


def _sigmoid(x):
    return 0.5 * jnp.tanh(0.5 * x) + 0.5


def _sgu_mask():
    ri = lax.broadcasted_iota(jnp.int32, (HEAD, HEAD), 0)
    ci = lax.broadcasted_iota(jnp.int32, (HEAD, HEAD), 1)
    return (ci // CHUNK) <= (ri // CHUNK)


def _window_sum(ext, w, trailing):
    n = ext.shape[0]
    s, k = ext, 1
    while k < w:
        s = s + pltpu.roll(s, k if trailing else n - k, 0)
        k *= 2
    return s


def _inv_count(row0, n, w):
    t = row0 + lax.broadcasted_iota(jnp.int32, (n, 1), 0)
    return 1.0 / jnp.minimum(t + 1, w).astype(F32)


def _shift_down(v, before, k):
    rows = lax.broadcasted_iota(jnp.int32, before.shape, 0)
    r = pltpu.roll(v, k, 0)
    top = jnp.where(rows < k, pltpu.roll(before, k, 0), r[0:HALO])
    return jnp.concatenate([top, r[HALO:]], axis=0)


def _shift_up(v, after, k):
    n = v.shape[0]
    rows = lax.broadcasted_iota(jnp.int32, after.shape, 0)
    r = pltpu.roll(v, n - k, 0)
    bottom = jnp.where(rows >= HALO - k, pltpu.roll(after, HALO - k, 0), r[n - HALO:])
    return jnp.concatenate([r[:n - HALO], bottom], axis=0)


def _adamw(w, g, m, v):
    m = ADAM_B1 * m + (1.0 - ADAM_B1) * g
    v = ADAM_B2 * v + (1.0 - ADAM_B2) * (g * g)
    m_hat = m / (1.0 - ADAM_B1 ** ADAM_STEP)
    v_hat = v / (1.0 - ADAM_B2 ** ADAM_STEP)
    delta = -ADAM_LR * (m_hat / (jnp.sqrt(v_hat) + ADAM_EPS) + ADAM_WD * w)
    return delta, m, v


def _coords():
    return lax.axis_index("x"), lax.axis_index("y"), lax.axis_index("c")


def _peer(k):
    x, y, c = _coords()
    return (x ^ ((k >> 2) & 1), y ^ ((k >> 1) & 1), c ^ (k & 1))


def _my_index():
    x, y, c = _coords()
    return 4 * x + 2 * y + c


def _adaln_modulation(c_ref, w_ref, b_ref, mod_ref, scx_ref, scbuf, stage, recv, send_sems, recv_sems):
    me = _my_index()
    cv = c_ref[...]
    scbuf[0] = cv * _sigmoid(cv)
    first = [
        pltpu.make_async_remote_copy(scbuf.at[0], scbuf.at[k], send_sems.at[0, k], recv_sems.at[0, k],
                                     device_id=_peer(k), device_id_type=MESH)
        for k in range(1, N_DEV)
    ]
    for cp in first:
        cp.start()
    for cp in first:
        cp.wait()
    scx_ref[...] = jnp.zeros(scx_ref.shape, F32)
    for k in range(N_DEV):
        scx_ref[k:k + 1, :] = scbuf[k]
    prod = _mm(_bf(scx_ref[...]), _bf(w_ref[...]))
    for k in range(N_DEV):
        stage[k] = prod[k:k + 1, :] + b_ref[me]
    second = [
        pltpu.make_async_remote_copy(stage.at[k], recv.at[k], send_sems.at[1, k], recv_sems.at[1, k],
                                     device_id=_peer(k), device_id_type=MESH)
        for k in range(1, N_DEV)
    ]
    for cp in second:
        cp.start()
    mod_ref[me] = stage[0]
    for cp in second:
        cp.wait()
    for k in range(1, N_DEV):
        mod_ref[me ^ k] = recv[k]


class _GatherSteps:
    def __init__(self, ins, outs, stages, send_sems, recv_sems, local_sems):
        self.ins, self.outs, self.stages = ins, outs, stages
        self.send_sems, self.recv_sems, self.local_sems = send_sems, recv_sems, local_sems
        x, y, c = _coords()
        self.c = c
        self.me, self.sibling = (x, y, c), (x, y, 1 - c)
        self.chips = [(1 - x, y), (x, 1 - y), (1 - x, 1 - y)]

    def _copy(self, a, k, block, to, from_stage=False):
        dst = self.outs[a].at[4 * block[0] + 2 * block[1] + block[2]]
        return pltpu.make_async_remote_copy(self.stages[a] if from_stage else dst, dst, self.send_sems.at[a, k],
                                            self.recv_sems.at[a, k], device_id=to, device_id_type=MESH)

    def _local(self, a):
        me = self.me
        return pltpu.make_async_copy(self.stages[a], self.outs[a].at[4 * me[0] + 2 * me[1] + me[2]],
                                     self.local_sems.at[a])

    def _first(self, a):
        cps = [self._copy(a, 0, self.me, self.sibling, from_stage=True)]
        return cps + [self._copy(a, 1 + j, self.me, (*chip, self.c), from_stage=True)
                      for j, chip in enumerate(self.chips)]

    def _passed(self, a, j):
        return self._copy(a, 4 + j, (*self.chips[j], self.c), self.sibling)

    def start(self):
        for a in range(len(self.ins)):
            block = self.ins[a][...]
            if block.shape != self.stages[a].shape:
                block = block.T
            self.stages[a][...] = block.astype(self.stages[a].dtype)
            self._local(a).start()
            for cp in self._first(a):
                cp.start()

    def forward(self, arrays=None):
        for a in range(len(self.ins)) if arrays is None else arrays:
            for j, chip in enumerate(self.chips):
                self._copy(a, 1 + j, (*chip, self.c), self.me).wait_recv()
                self._passed(a, j).start()

    def finish(self):
        for a in range(len(self.ins)):
            self._copy(a, 0, self.sibling, self.me).wait_recv()
            for j, chip in enumerate(self.chips):
                self._copy(a, 4 + j, (*chip, 1 - self.c), self.me).wait_recv()
            for cp in self._first(a) + [self._passed(a, j) for j in range(3)]:
                cp.wait_send()
            self._local(a).wait()


def _gather_scratch(shapes, out_dtypes):
    n = len(shapes)
    return ([pltpu.VMEM(s, dt) for s, dt in zip(shapes, out_dtypes)]
            + [pltpu.SemaphoreType.DMA((n, 7)), pltpu.SemaphoreType.DMA((n, 7)), pltpu.SemaphoreType.DMA((n,))])


def _gather_out_shapes(shapes, out_dtypes):
    return tuple(jax.ShapeDtypeStruct((N_DEV, *s), dt) for s, dt in zip(shapes, out_dtypes))


def _prologue(c_row, w_ada, b_ada3, shards, out_dtypes):
    n = len(shards)

    def body(*refs):
        c_ref, w_ref, b_ref = refs[:3]
        mod_ref, scx_ref = refs[3 + n:5 + n]
        gather = _GatherSteps(refs[3:3 + n], refs[5 + n:5 + 2 * n], refs[5 + 2 * n:5 + 3 * n],
                              *refs[5 + 3 * n:8 + 3 * n])
        gather.start()
        _adaln_modulation(c_ref, w_ref, b_ref, mod_ref, scx_ref, *refs[8 + 3 * n:])
        gather.forward()
        gather.finish()

    outs = pl.pallas_call(
        body,
        name="prologue",
        out_shape=(jax.ShapeDtypeStruct((N_DEV, 1, MOD_COLS), F32), jax.ShapeDtypeStruct((2 * N_DEV, D), F32))
        + _gather_out_shapes([s.shape for s in shards], out_dtypes),
        in_specs=[VMEM_SPEC] * (3 + n),
        out_specs=(VMEM_SPEC, VMEM_SPEC) + (ANY_SPEC,) * n,
        scratch_shapes=_gather_scratch([s.shape for s in shards], out_dtypes) + [
            pltpu.VMEM((N_DEV, 1, D), F32),
            pltpu.VMEM((N_DEV, 1, MOD_COLS), F32),
            pltpu.VMEM((N_DEV, 1, MOD_COLS), F32),
            pltpu.SemaphoreType.DMA((2, N_DEV)),
            pltpu.SemaphoreType.DMA((2, N_DEV)),
        ],
        compiler_params=pltpu.CompilerParams(vmem_limit_bytes=VMEM_LIMIT_V7X),
    )(c_row, w_ada, b_ada3, *shards)
    return outs[0], outs[1], outs[2:]


class _ChipExchangeSteps:
    FLIPS = ((1, 0), (0, 1), (1, 1))

    def __init__(self, srcs, dsts, send_sems, recv_sems):
        self.srcs, self.dsts, self.send_sems, self.recv_sems = srcs, dsts, send_sems, recv_sems

    def _copies(self):
        x, y, c = _coords()
        out = []
        for a in range(len(self.srcs)):
            for j, (fx, fy) in enumerate(self.FLIPS):
                k = 3 * a + j
                out.append(pltpu.make_async_remote_copy(
                    self.srcs[a].at[x ^ fx, y ^ fy], self.dsts[a].at[j], self.send_sems.at[k], self.recv_sems.at[k],
                    device_id=(x ^ fx, y ^ fy, c), device_id_type=MESH))
        return out

    def start(self):
        for cp in self._copies():
            cp.start()

    def finish(self):
        for cp in self._copies():
            cp.wait()


def _mix_fwd(x, mod, g_pre, g_post, w_in_t, sgn, w_sp, b_sp_t, w_pool, p_scale, w_out_b, ts, shards, shard_shapes,
             shard_dtypes):
    t_len = x.shape[0]
    nt, nb = t_len // ts, ts // HEAD
    ns = len(shards)

    def body(*refs):
        (x_ref, mod_ref, g1_ref, g2_ref, win_ref, sgn_ref, ws_ref, bst_ref, wp_ref, ps_ref, wout_ref) = refs[:11]
        x1_ref, proj_ref, mixed_ref = refs[11 + ns:14 + ns]
        pbuf, cat = refs[14 + 2 * ns:16 + 2 * ns]
        gather = _GatherSteps(refs[11:11 + ns], refs[14 + ns:14 + 2 * ns], refs[16 + 2 * ns:16 + 3 * ns],
                              *refs[16 + 3 * ns:])
        i = pl.program_id(0)

        @pl.when(i == 0)
        def _():
            pbuf[0:POOL_HALO, :] = jnp.zeros((POOL_HALO, B_WIDTH), F32)
            gather.start()

        @pl.when(i == (3 * nt) // 4)
        def _():
            gather.forward([0])

        @pl.when(i == nt - 1)
        def _():
            gather.forward(range(1, ns))

        xv = x_ref[...]
        sh, sc, gm = mod_ref[0:1, :], mod_ref[1:2, :], mod_ref[2:3, :]
        h = (xv * _rstd(xv) * g1_ref[...]) * (1.0 + sc) + sh
        proj_ref[...] = _mm_nt(_bf(h), win_ref[...])
        pbuf[POOL_HALO:POOL_HALO + ts, :] = proj_ref[:, 2 * A_WIDTH:]
        smask = _sgu_mask()
        for hd in range(N_HEAD):
            u = _gelu(proj_ref[:, hd * HEAD:(hd + 1) * HEAD])
            v = _gelu(proj_ref[:, A_WIDTH + hd * HEAD:A_WIDTH + (hd + 1) * HEAD])
            vn = _bf(v * _rstd(v) * sgn_ref[hd:hd + 1, :])
            wm = _bf(jnp.where(smask, ws_ref[hd], 0.0))
            bias = bst_ref[:, hd:hd + 1]
            for b in range(nb):
                rows = slice(b * HEAD, (b + 1) * HEAD)
                z = _mm(wm, vn[rows]) + bias
                cat[rows, hd * HEAD:(hd + 1) * HEAD] = _bf(u[rows] * z)
        for g, w in enumerate(WINDOWS):
            cols = slice(g * HEAD, (g + 1) * HEAD)
            ext = pbuf[:, cols]
            pooled = _window_sum(ext, w, True)[POOL_HALO:] * _inv_count(i * ts, ts, w) - ext[POOL_HALO:]
            cat[:, A_WIDTH + g * HEAD:A_WIDTH + (g + 1) * HEAD] = _bf(_mm(_bf(pooled), _bf(wp_ref[g])) * ps_ref[:, cols])
        pbuf[0:POOL_HALO, :] = pbuf[ts:ts + POOL_HALO, :]
        mixed = _mm(cat[...], wout_ref[...])
        mixed_ref[...] = mixed
        x1_ref[...] = xv + gm * (mixed * _rstd(mixed) * g2_ref[...])

        @pl.when(i == nt - 1)
        def _():
            gather.finish()

    tile = lambda wid: pl.BlockSpec((ts, wid), lambda i: (i, 0))
    outs = pl.pallas_call(
        body,
        name="mix_fwd",
        grid=(nt,),
        out_shape=(jax.ShapeDtypeStruct((t_len, D), F32), jax.ShapeDtypeStruct((t_len, IN_WIDTH), F32),
                   jax.ShapeDtypeStruct((t_len, D), F32)) + _gather_out_shapes(shard_shapes, shard_dtypes),
        in_specs=[tile(D)] + [VMEM_SPEC] * (10 + ns),
        out_specs=(tile(D), tile(IN_WIDTH), tile(D)) + (ANY_SPEC,) * ns,
        scratch_shapes=[pltpu.VMEM((POOL_HALO + ts, B_WIDTH), F32), pltpu.VMEM((ts, D), BF16)]
        + _gather_scratch(shard_shapes, shard_dtypes),
        compiler_params=pltpu.CompilerParams(dimension_semantics=("arbitrary",), vmem_limit_bytes=VMEM_LIMIT_V7X),
    )(x, mod, g_pre, g_post, w_in_t, sgn, w_sp, b_sp_t, w_pool, p_scale, w_out_b, *shards)
    return outs[:3], outs[3:]


def _ffn_fwd(x1, target, mod, g_pre, g_post, w_up_b, conv_w8, conv_b8, w_down_b, ts):
    t_len = x1.shape[0]
    nt = t_len // ts

    def body(x1_ref, tgt_ref, mod_ref, g3_ref, g4_ref, wup_ref, cw_ref, cb_ref, wdown_ref,
             up_ref, f_ref, dx2_ref, loss_ref, ucarry):
        i = pl.program_id(0)

        @pl.when(i == 0)
        def _():
            ucarry[...] = jnp.zeros(ucarry.shape, F32)
            loss_ref[...] = jnp.zeros(loss_ref.shape, F32)

        x1v = x1_ref[...]
        sh, sc, gf = mod_ref[3:4, :], mod_ref[4:5, :], mod_ref[5:6, :]
        h2 = _bf((x1v * _rstd(x1v) * g3_ref[...]) * (1.0 + sc) + sh)
        half = N_DEV // 2

        def up_pair(j):
            return [_mm(h2, wup_ref[jj]) for jj in (j, j + half)]

        f = jnp.zeros((ts, D), F32)
        ups = up_pair(0)
        for j in range(half):
            nxt = up_pair(j + 1) if j + 1 < half else None
            ys = []
            for up, jj in zip(ups, (j, j + half)):
                up_ref[jj] = up
                before = ucarry[jj]
                ucarry[jj] = up[ts - HALO:, :]
                cw = cw_ref[jj]
                ys.append(cb_ref[jj:jj + 1, :] + _shift_down(up, before, 2) * cw[0:1, :]
                          + _shift_down(up, before, 1) * cw[1:2, :] + up * cw[2:3, :])
            gate, val = ys
            act = gate * _sigmoid(gate) * val
            f = f + _mm(_bf(act), wdown_ref[j * FF_CHUNK:(j + 1) * FF_CHUNK, :])
            ups = nxt
        f_ref[...] = f
        x2 = x1v + gf * (f * _rstd(f) * g4_ref[...])
        err = x2 - tgt_ref[...]
        loss_ref[...] += _sum0(err * err)
        dx2_ref[...] = err * (1.0 / D)

    tile = pl.BlockSpec((ts, D), lambda i: (i, 0))
    return pl.pallas_call(
        body,
        name="ffn_fwd",
        grid=(nt,),
        out_shape=(jax.ShapeDtypeStruct((N_DEV, t_len, FF_CHUNK), F32), jax.ShapeDtypeStruct((t_len, D), F32),
                   jax.ShapeDtypeStruct((t_len, D), F32), jax.ShapeDtypeStruct((1, D), F32)),
        in_specs=[tile, tile] + [VMEM_SPEC] * 7,
        out_specs=(pl.BlockSpec((N_DEV, ts, FF_CHUNK), lambda i: (0, i, 0)), tile, tile,
                   pl.BlockSpec((1, D), lambda i: (0, 0))),
        scratch_shapes=[pltpu.VMEM((N_DEV, HALO, FF_CHUNK), F32)],
        compiler_params=pltpu.CompilerParams(dimension_semantics=("arbitrary",), vmem_limit_bytes=VMEM_LIMIT_V7X),
    )(x1, target, mod, g_pre, g_post, w_up_b, conv_w8, conv_b8, w_down_b)


def _ffn_bwd(dx2, f, x1, up, mod, g_pre, g_post, w_up_b, conv_w8, conv_b8, w_down_b, ts):
    t_len = x1.shape[0]
    nt = t_len // ts
    half = N_DEV // 2

    def body(dx2_ref, f_ref, x1_ref, up_ref, halo_ref, mod_ref, g3_ref, g4_ref, wup_ref, cw_ref, cb_ref, wdown_ref,
             dx1_ref, dup_ref, act_ref, df_ref, h2_ref, dmod_ref, dg3_ref, dg4_ref, dcb_ref, dcw_ref,
             dycarry, dh2acc):
        i = pl.program_id(0)
        r = nt - 1 - i

        @pl.when(i == 0)
        def _():
            for ref in (dmod_ref, dg3_ref, dg4_ref, dcb_ref, dcw_ref, dycarry):
                ref[...] = jnp.zeros(ref.shape, F32)

        dx2v, fv, x1v = dx2_ref[...], f_ref[...], x1_ref[...]
        sh, sc, gf = mod_ref[3:4, :], mod_ref[4:5, :], mod_ref[5:6, :]
        g3, g4 = g3_ref[...], g4_ref[...]
        rstd4 = _rstd(fv)
        fh = fv * rstd4
        dmod_ref[2:3, :] += _sum0(dx2v * (fh * g4))
        dr = dx2v * gf
        dg4_ref[...] += _sum0(dr * fh)
        dfh = dr * g4
        dfb = _bf(rstd4 * (dfh - fh * _rowmean(dfh * fh)))
        df_ref[...] = dfb
        rstd3 = _rstd(x1v)
        xh = x1v * rstd3
        n3 = xh * g3
        h2_ref[...] = _bf(n3 * (1.0 + sc) + sh)
        dh2acc[...] = jnp.zeros((ts, D), F32)
        keep = jnp.where(r > 0, 1.0, 0.0).astype(F32)

        def dact_of(j):
            return _mm_nt(dfb, wdown_ref[j * FF_CHUNK:(j + 1) * FF_CHUNK, :])

        dact_next = dact_of(0)
        for j in range(half):
            dact = dact_next
            if j + 1 < half:
                dact_next = dact_of(j + 1)
            ys = []
            for jj in (j, j + half):
                before = halo_ref[jj] * keep
                upc = up_ref[jj]
                cw = cw_ref[jj]
                ys.append(cb_ref[jj:jj + 1, :] + _shift_down(upc, before, 2) * cw[0:1, :]
                          + _shift_down(upc, before, 1) * cw[1:2, :] + upc * cw[2:3, :])
            gate, val = ys
            sg = _sigmoid(gate)
            silu = gate * sg
            act_ref[j] = _bf(silu * val)
            dys = (dact * val * (sg + silu * (1.0 - sg)), dact * silu)
            for q, jj in enumerate((j, j + half)):
                dy = dys[q]
                cw = cw_ref[jj]
                dcb_ref[jj:jj + 1, :] += _sum0(dy)
                after = dycarry[jj]
                dycarry[jj] = dy[0:HALO, :]
                dy1, dy2 = _shift_up(dy, after, 1), _shift_up(dy, after, 2)
                upc = up_ref[jj]
                dcw_ref[jj, 0:1, :] += _sum0(dy2 * upc)
                dcw_ref[jj, 1:2, :] += _sum0(dy1 * upc)
                dcw_ref[jj, 2:3, :] += _sum0(dy * upc)
                dup = _bf(dy * cw[2:3, :] + dy1 * cw[1:2, :] + dy2 * cw[0:1, :])
                dup_ref[jj] = dup
                dh2acc[...] += _mm_nt(dup, wup_ref[jj])
        dh2 = dh2acc[...]
        dmod_ref[0:1, :] += _sum0(dh2)
        dmod_ref[1:2, :] += _sum0(dh2 * n3)
        dn3 = dh2 * (1.0 + sc)
        dg3_ref[...] += _sum0(dn3 * xh)
        dxh = dn3 * g3
        dx1_ref[...] = dx2v + rstd3 * (dxh - xh * _rowmean(dxh * xh))

    tile = pl.BlockSpec((ts, D), lambda i: (nt - 1 - i, 0))
    chunked = lambda n: pl.BlockSpec((n, ts, FF_CHUNK), lambda i: (0, nt - 1 - i, 0))
    halo = pl.BlockSpec((N_DEV, HALO, FF_CHUNK), lambda i: (0, jnp.maximum((nt - 1 - i) * (ts // HALO) - 1, 0), 0))
    const = lambda *shape: pl.BlockSpec(shape, lambda i: (0,) * len(shape))
    return pl.pallas_call(
        body,
        name="ffn_bwd",
        grid=(nt,),
        out_shape=(jax.ShapeDtypeStruct((t_len, D), F32), jax.ShapeDtypeStruct((N_DEV, t_len, FF_CHUNK), BF16),
                   jax.ShapeDtypeStruct((half, t_len, FF_CHUNK), BF16), jax.ShapeDtypeStruct((t_len, D), BF16),
                   jax.ShapeDtypeStruct((t_len, D), BF16), jax.ShapeDtypeStruct((3, D), F32),
                   jax.ShapeDtypeStruct((1, D), F32), jax.ShapeDtypeStruct((1, D), F32),
                   jax.ShapeDtypeStruct((N_DEV, FF_CHUNK), F32), jax.ShapeDtypeStruct((N_DEV, 3, FF_CHUNK), F32)),
        in_specs=[tile, tile, tile, chunked(N_DEV), halo] + [VMEM_SPEC] * 7,
        out_specs=(tile, chunked(N_DEV), chunked(half), tile, tile, const(3, D), const(1, D), const(1, D),
                   const(N_DEV, FF_CHUNK), const(N_DEV, 3, FF_CHUNK)),
        scratch_shapes=[pltpu.VMEM((N_DEV, HALO, FF_CHUNK), F32), pltpu.VMEM((ts, D), F32)],
        compiler_params=pltpu.CompilerParams(dimension_semantics=("arbitrary",), vmem_limit_bytes=VMEM_LIMIT_V7X),
    )(dx2, f, x1, up, up, mod, g_pre, g_post, w_up_b, conv_w8, conv_b8, w_down_b)


def _wgrad_up(h2, dup, ts):
    t_len = h2.shape[0]
    nt, half = t_len // ts, N_DEV // 2

    def body(h2_ref, dup_ref, out_ref):
        @pl.when(pl.program_id(1) == 0)
        def _():
            out_ref[...] = jnp.zeros(out_ref.shape, F32)

        for q in range(half):
            out_ref[q] += _mm_tn(dup_ref[q], h2_ref[...])

    return pl.pallas_call(
        body,
        name="wgrad_up",
        grid=(2, nt),
        out_shape=jax.ShapeDtypeStruct((N_DEV, FF_CHUNK, D), F32),
        in_specs=[pl.BlockSpec((ts, D), lambda g, t: (t, 0)), pl.BlockSpec((half, ts, FF_CHUNK), lambda g, t: (g, t, 0))],
        out_specs=pl.BlockSpec((half, FF_CHUNK, D), lambda g, t: (g, 0, 0)),
        compiler_params=pltpu.CompilerParams(dimension_semantics=("arbitrary", "arbitrary"),
                                             vmem_limit_bytes=VMEM_LIMIT_V7X),
    )(h2, dup)


def _sibling_swap_copies(srcs, dsts, send_sems, recv_sems):
    x, y, c = _coords()
    return [
        pltpu.make_async_remote_copy(srcs[a].at[xs, ys, 1 - c], dsts[a].at[xs, ys], send_sems.at[a, 2 * xs + ys],
                                     recv_sems.at[a, 2 * xs + ys], device_id=(x, y, 1 - c), device_id_type=MESH)
        for a in range(len(srcs)) for xs in range(2) for ys in range(2)
    ]


def _wgrad_down(act, df, ts, swap_src):
    t_len = df.shape[0]
    nt, half = t_len // ts, N_DEV // 2

    def body(act_ref, df_ref, src_ref, out_ref, dst_ref, send_sems, recv_sems):
        t = pl.program_id(0)

        @pl.when(t == 0)
        def _():
            for cp in _sibling_swap_copies([src_ref], [dst_ref], send_sems, recv_sems):
                cp.start()
            out_ref[...] = jnp.zeros(out_ref.shape, F32)

        for q in range(half):
            out_ref[q] += _mm_tn(act_ref[q], df_ref[...])

        @pl.when(t == nt - 1)
        def _():
            for cp in _sibling_swap_copies([src_ref], [dst_ref], send_sems, recv_sems):
                cp.wait()

    return pl.pallas_call(
        body,
        name="wgrad_down",
        grid=(nt,),
        out_shape=(jax.ShapeDtypeStruct((half, FF_CHUNK, D), F32), jax.ShapeDtypeStruct(swap_src.shape[1:], F32)),
        in_specs=[pl.BlockSpec((half, ts, FF_CHUNK), lambda t: (0, t, 0)), pl.BlockSpec((ts, D), lambda t: (t, 0)),
                  ANY_SPEC],
        out_specs=(pl.BlockSpec((half, FF_CHUNK, D), lambda t: (0, 0, 0)), ANY_SPEC),
        scratch_shapes=[pltpu.SemaphoreType.DMA((1, 4)), pltpu.SemaphoreType.DMA((1, 4))],
        compiler_params=pltpu.CompilerParams(dimension_semantics=("arbitrary",), vmem_limit_bytes=VMEM_LIMIT_V7X),
    )(act, df, swap_src)


def _mix_bwd(dx1, x, proj, mixed, mod, g_pre, g_post, w_in_t, sgn, w_sp, b_sp_t, w_pool, p_scale, w_out_b, ts, rs_srcs):
    t_len = x.shape[0]
    nt, nb = t_len // ts, ts // HEAD
    nr = len(rs_srcs)

    def body(*refs):
        (dx1_ref, x_ref, proj_ref, halo_ref, mixed_ref, mod_ref, g1_ref, g2_ref, win_ref, sgn_ref, ws_ref,
         bst_ref, wp_ref, ps_ref, wout_ref) = refs[:15]
        (gx_ref, dwin_ref, dwout_ref, dmod_ref, dg1_ref, dg2_ref, dsgn_ref, dws_ref, dbst_ref, dwp_ref,
         dps_ref) = refs[15 + nr:26 + nr]
        pbuf, dwsbuf, cat, dproj, dcat = refs[26 + 2 * nr:31 + 2 * nr]
        exchange = _ChipExchangeSteps(refs[15:15 + nr], refs[26 + nr:26 + 2 * nr], *refs[31 + 2 * nr:])
        i = pl.program_id(0)
        r = nt - 1 - i

        @pl.when(i == 0)
        def _():
            exchange.start()
            for ref in (dwin_ref, dwout_ref, dmod_ref, dg1_ref, dg2_ref, dsgn_ref, dws_ref, dbst_ref, dwp_ref, dps_ref):
                ref[...] = jnp.zeros(ref.shape, F32)
            dwsbuf[ts:ts + POOL_HALO, :] = jnp.zeros((POOL_HALO, B_WIDTH), F32)

        xv, dx1v, mixed = x_ref[...], dx1_ref[...], mixed_ref[...]
        sh, sc, gm = mod_ref[0:1, :], mod_ref[1:2, :], mod_ref[2:3, :]
        g1, g2 = g1_ref[...], g2_ref[...]
        rstd2 = _rstd(mixed)
        mh = mixed * rstd2
        dmod_ref[2:3, :] += _sum0(dx1v * (mh * g2))
        dr = dx1v * gm
        dg2_ref[...] += _sum0(dr * mh)
        dmh = dr * g2
        dmb = _bf(rstd2 * (dmh - mh * _rowmean(dmh * mh)))
        dcat[...] = _mm_nt(dmb, wout_ref[...])
        smask = _sgu_mask()
        for hd in range(N_HEAD):
            ucols = slice(hd * HEAD, (hd + 1) * HEAD)
            vcols = slice(A_WIDTH + hd * HEAD, A_WIDTH + (hd + 1) * HEAD)
            u, du_dp = _gelu_and_grad(proj_ref[:, ucols])
            v, dv_dp = _gelu_and_grad(proj_ref[:, vcols])
            rs = _rstd(v)
            vhat = v * rs
            gn = sgn_ref[hd:hd + 1, :]
            vn = _bf(vhat * gn)
            wm = _bf(jnp.where(smask, ws_ref[hd], 0.0))
            bias = bst_ref[:, hd:hd + 1]
            dzsum = jnp.zeros((HEAD, HEAD), F32)
            dwm = jnp.zeros((HEAD, HEAD), F32)
            dvn_parts = []
            for b in range(nb):
                rows = slice(b * HEAD, (b + 1) * HEAD)
                z = _mm(wm, vn[rows]) + bias
                da = dcat[rows, ucols]
                cat[rows, ucols] = _bf(u[rows] * z)
                dz = da * u[rows]
                dzsum = dzsum + dz
                dzb = _bf(dz)
                dwm = dwm + _mm_nt(dzb, vn[rows])
                dvn_parts.append(_mm_tn(wm, dzb))
                dproj[rows, ucols] = _bf((da * z) * du_dp[rows])
            dvn = jnp.concatenate(dvn_parts, axis=0)
            dsgn_ref[hd:hd + 1, :] += _sum0(dvn * vhat)
            dvh = dvn * gn
            dproj[:, vcols] = _bf((rs * (dvh - vhat * _rowmean(dvh * vhat))) * dv_dp)
            dws_ref[hd] += jnp.where(smask, dwm, 0.0)
            dbst_ref[:, hd:hd + 1] += jnp.sum(dzsum, axis=1, keepdims=True)
        keep = jnp.where(r > 0, 1.0, 0.0).astype(F32)
        pbuf[0:POOL_HALO, :] = halo_ref[...] * keep
        pbuf[POOL_HALO:POOL_HALO + ts, :] = proj_ref[:, 2 * A_WIDTH:]
        for g, w in enumerate(WINDOWS):
            cols = slice(g * HEAD, (g + 1) * HEAD)
            ccols = slice(A_WIDTH + g * HEAD, A_WIDTH + (g + 1) * HEAD)
            pcols = slice(2 * A_WIDTH + g * HEAD, 2 * A_WIDTH + (g + 1) * HEAD)
            wpg = _bf(wp_ref[g])
            psg = ps_ref[:, cols]
            ext = pbuf[:, cols]
            inv = _inv_count(r * ts, ts, w)
            pb = _bf(_window_sum(ext, w, True)[POOL_HALO:] * inv - ext[POOL_HALO:])
            yb = _mm(pb, wpg)
            dob = dcat[:, ccols]
            cat[:, ccols] = _bf(yb * psg)
            dps_ref[:, cols] += _sum0(dob * yb)
            dyb = _bf(dob * psg)
            dwp_ref[g] += _mm_tn(pb, dyb)
            dpooled = _mm_nt(dyb, wpg)
            dwsbuf[0:ts, cols] = dpooled * inv
            dproj[:, pcols] = _bf(_window_sum(dwsbuf[:, cols], w, False)[0:ts] - dpooled)
        dwsbuf[ts:ts + POOL_HALO, :] = dwsbuf[0:POOL_HALO, :]
        dpb = dproj[...]
        rstd1 = _rstd(xv)
        xh = xv * rstd1
        n1 = xh * g1
        dwin_ref[...] += _mm_tn(dpb, _bf(n1 * (1.0 + sc) + sh))
        dwout_ref[...] += _mm_tn(cat[...], dmb)
        dh = _mm(dpb, win_ref[...])
        dmod_ref[0:1, :] += _sum0(dh)
        dmod_ref[1:2, :] += _sum0(dh * n1)
        dn1 = dh * (1.0 + sc)
        dg1_ref[...] += _sum0(dn1 * xh)
        dxh = dn1 * g1
        gx_ref[...] = dx1v + rstd1 * (dxh - xh * _rowmean(dxh * xh))

        @pl.when(i == nt - 1)
        def _():
            exchange.finish()

    tile = lambda wid: pl.BlockSpec((ts, wid), lambda i: (nt - 1 - i, 0))
    halo = pl.BlockSpec((POOL_HALO, B_WIDTH),
                        lambda i: (jnp.maximum((nt - 1 - i) * (ts // POOL_HALO) - 1, 0), 2 * A_WIDTH // B_WIDTH))
    const = lambda *shape: pl.BlockSpec(shape, lambda i: (0,) * len(shape))
    resident = lambda *shape: pl.BlockSpec(shape, lambda i: (0,) * len(shape), pipeline_mode=pl.Buffered(1))
    outs = pl.pallas_call(
        body,
        name="mix_bwd",
        grid=(nt,),
        out_shape=(jax.ShapeDtypeStruct((t_len, D), F32), jax.ShapeDtypeStruct((IN_WIDTH, D), F32),
                   jax.ShapeDtypeStruct((D, D), F32), jax.ShapeDtypeStruct((3, D), F32),
                   jax.ShapeDtypeStruct((1, D), F32), jax.ShapeDtypeStruct((1, D), F32),
                   jax.ShapeDtypeStruct((N_HEAD, HEAD), F32), jax.ShapeDtypeStruct((N_HEAD, HEAD, HEAD), F32),
                   jax.ShapeDtypeStruct((HEAD, N_HEAD), F32), jax.ShapeDtypeStruct((N_HEAD, HEAD, HEAD), F32),
                   jax.ShapeDtypeStruct((1, B_WIDTH), F32))
        + tuple(jax.ShapeDtypeStruct((3, *s.shape[2:]), s.dtype) for s in rs_srcs),
        in_specs=[tile(D), tile(D), tile(IN_WIDTH), halo, tile(D)] + [VMEM_SPEC] * 10 + [ANY_SPEC] * nr,
        out_specs=(tile(D), resident(IN_WIDTH, D), resident(D, D), const(3, D), const(1, D), const(1, D),
                   const(N_HEAD, HEAD), const(N_HEAD, HEAD, HEAD), const(HEAD, N_HEAD), const(N_HEAD, HEAD, HEAD),
                   const(1, B_WIDTH)) + (ANY_SPEC,) * nr,
        scratch_shapes=[pltpu.VMEM((POOL_HALO + ts, B_WIDTH), F32), pltpu.VMEM((ts + POOL_HALO, B_WIDTH), F32),
                        pltpu.VMEM((ts, D), BF16), pltpu.VMEM((ts, IN_WIDTH), BF16), pltpu.VMEM((ts, D), F32),
                        pltpu.SemaphoreType.DMA((3 * nr,)), pltpu.SemaphoreType.DMA((3 * nr,))],
        compiler_params=pltpu.CompilerParams(dimension_semantics=("arbitrary",), vmem_limit_bytes=VMEM_LIMIT_V7X),
    )(dx1, x, proj, proj, mixed, mod, g_pre, g_post, w_in_t, sgn, w_sp, b_sp_t, w_pool, p_scale, w_out_b, *rs_srcs)
    return outs[:11], outs[11:]


def _pair_add(name, coords, grid, specs_a, specs_b, out_specs, out_shapes, a_arrays, b_arrays, swap_srcs):
    n, ns = len(a_arrays), len(swap_srcs)
    last = tuple(g - 1 for g in grid)

    def body(co_ref, *refs):
        ids = [pl.program_id(d) for d in range(len(grid))]
        swap = refs[2 * n:2 * n + ns], refs[4 * n + ns:4 * n + 2 * ns], *refs[4 * n + 2 * ns:]
        if ns:
            @pl.when(functools.reduce(jnp.logical_and, [i == 0 for i in ids]))
            def _():
                for cp in _sibling_swap_copies(*swap):
                    cp.start()

        for k in range(n):
            total = refs[k][...] + refs[n + k][...]
            refs[2 * n + ns + k][...] = total
            refs[3 * n + ns + k][...] = _bf(total)

        if ns:
            @pl.when(functools.reduce(jnp.logical_and, [i == e for i, e in zip(ids, last)]))
            def _():
                for cp in _sibling_swap_copies(*swap):
                    cp.wait()

    outs = pl.pallas_call(
        body,
        name=name,
        grid_spec=pltpu.PrefetchScalarGridSpec(
            num_scalar_prefetch=1, grid=grid, in_specs=specs_a + specs_b + [ANY_SPEC] * ns,
            out_specs=out_specs * 2 + [ANY_SPEC] * ns,
            scratch_shapes=[pltpu.SemaphoreType.DMA((ns, 4)), pltpu.SemaphoreType.DMA((ns, 4))] if ns else []),
        out_shape=tuple(jax.ShapeDtypeStruct(s, dt) for dt in (F32, BF16) for s in out_shapes)
        + tuple(jax.ShapeDtypeStruct(g.shape[1:], F32) for g in swap_srcs),
        compiler_params=pltpu.CompilerParams(dimension_semantics=("arbitrary",) * len(grid),
                                             vmem_limit_bytes=VMEM_LIMIT_V7X),
    )(coords, *a_arrays, *b_arrays, *swap_srcs)
    return list(outs[:n]), list(outs[n:2 * n]), list(outs[2 * n:])


def _final_add_adamw(coords, s1, r, ws, ms, vs, n_split=4):
    n = len(s1)

    def body(co_ref, *refs):
        for k in range(n):
            s_ref, r_ref, w_ref, m_ref, v_ref = (refs[q * n + k] for q in range(5))
            g_ref, d_ref, nm_ref, nv_ref = (refs[(5 + q) * n + k] for q in range(4))
            g = ((s_ref[...] + r_ref[0].astype(F32)) + r_ref[1].astype(F32)) + r_ref[2].astype(F32)
            g_ref[...] = g
            delta, m, v = _adamw(w_ref[...], g, m_ref[...], v_ref[...])
            d_ref[...] = delta
            nm_ref[...] = m
            nv_ref[...] = v

    def shard_spec(a):
        rows, cols = a.shape
        return pl.BlockSpec((rows // n_split, cols), lambda i, co: (i, 0))

    def mine_spec(a):
        rows, cols = a.shape[2:]
        return pl.BlockSpec((None, None, rows // n_split, cols), lambda i, co: (co[0], co[1], i, 0))

    def recv_spec(a):
        rows, cols = a.shape[1:]
        return pl.BlockSpec((3, rows // n_split, cols), lambda i, co: (0, i, 0))

    in_specs = ([mine_spec(a) for a in s1] + [recv_spec(a) for a in r] + [shard_spec(a) for a in ws] * 3)
    out_specs = [shard_spec(a) for a in ws] * 4
    outs = pl.pallas_call(
        body,
        name="grad_final_adamw",
        grid_spec=pltpu.PrefetchScalarGridSpec(num_scalar_prefetch=1, grid=(n_split,), in_specs=in_specs,
                                               out_specs=out_specs),
        out_shape=tuple(jax.ShapeDtypeStruct(a.shape, F32) for a in ws) * 4,
        compiler_params=pltpu.CompilerParams(dimension_semantics=("arbitrary",), vmem_limit_bytes=VMEM_LIMIT_V7X),
    )(coords, *s1, *r, *ws, *ms, *vs)
    return [tuple(outs[q * n + k] for q in range(4)) for k in range(n)]


def _sibling_add(tag, g5, r1, coords, swap_srcs=(), n_split=4):
    shapes = [g.shape[3:] for g in g5]
    spec_g = [pl.BlockSpec((None, None, None, s[0] // n_split, s[1]), lambda i, j, k, co: (i, j, co[2], k, 0))
              for s in shapes]
    spec_r = [pl.BlockSpec((None, None, s[0] // n_split, s[1]), lambda i, j, k, co: (i, j, k, 0)) for s in shapes]
    return _pair_add("grad_add_core_" + tag, coords, (2, 2, n_split), spec_g, spec_r, spec_r,
                     [(2, 2, *s) for s in shapes], g5, r1, list(swap_srcs))


def _tail_exchange(big, partials, pick_mine, y_first, dmod3):
    n, nb = len(partials), len(big)
    big_shapes = [g.shape[1:] for g in big]
    big5 = [g.reshape(2, 2, 2, *s) for g, s in zip(big, big_shapes)]
    flips = _ChipExchangeSteps.FLIPS

    def body(*refs):
        g5, p_in, dm_ref = refs[:nb], refs[nb:nb + n], refs[nb + n]
        outs = refs[nb + n + 1:2 * nb + 2 * n + 2]
        g_out, sums, dm2d = outs[:nb], outs[nb:nb + n], outs[nb + n]
        scratch = refs[2 * nb + 2 * n + 2:]
        s1, stage, chip_recv = scratch[:nb], scratch[nb:2 * nb], scratch[2 * nb:3 * nb]
        acc, rbuf = scratch[3 * nb:3 * nb + n], scratch[3 * nb + n:3 * nb + 2 * n]
        (dm_recv, send_sems, recv_sems, dm_send_sems, dm_recv_sems, sib_send, sib_recv, chip_send,
         chip_recv_sems) = scratch[3 * nb + 2 * n:]
        x, y, c = _coords()
        me = 4 * x + 2 * y + c
        sibling = (x, y, 1 - c)
        dm_copies = [
            pltpu.make_async_remote_copy(dm_ref.at[me ^ k], dm_recv.at[k], dm_send_sems.at[k], dm_recv_sems.at[k],
                                         device_id=_peer(k), device_id_type=MESH)
            for k in range(1, N_DEV)
        ]
        for cp in dm_copies:
            cp.start()
        sib_copies = _sibling_swap_copies(g5, s1, sib_send, sib_recv)
        for cp in sib_copies:
            cp.start()
        for a in range(n):
            acc[a][...] = p_in[a][...]

        def small_phase(ph, peers):
            copies = [
                pltpu.make_async_remote_copy(acc[a], rbuf[a].at[ph], send_sems.at[ph, a], recv_sems.at[ph, a],
                                             device_id=peers[y_first[a]], device_id_type=MESH)
                for a in range(n)
            ]
            for cp in copies:
                cp.start()
            for cp in copies:
                cp.wait()
            for a in range(n):
                acc[a][...] = acc[a][...] + rbuf[a][ph]

        small_phase(0, (sibling, sibling))
        for cp in sib_copies:
            cp.wait()
        for a in range(nb):
            for xs in range(2):
                for ys in range(2):
                    total = g5[a][xs, ys, c] + s1[a][xs, ys]
                    s1[a][xs, ys] = total
                    stage[a][xs, ys] = _bf(total)
        chip_copies = [
            pltpu.make_async_remote_copy(stage[a].at[x ^ fx, y ^ fy], chip_recv[a].at[j], chip_send.at[a, j],
                                         chip_recv_sems.at[a, j], device_id=(x ^ fx, y ^ fy, c), device_id_type=MESH)
            for a in range(nb) for j, (fx, fy) in enumerate(flips)
        ]
        for cp in chip_copies:
            cp.start()
        x_peer, y_peer = (1 - x, y, c), (x, 1 - y, c)
        small_phase(1, (x_peer, y_peer))
        small_phase(2, (y_peer, x_peer))
        for a in range(n):
            sums[a][...] = acc[a][me] if pick_mine[a] else acc[a][...]
        dm2d[...] = jnp.zeros(dm2d.shape, F32)
        dm2d[0:1, :] = dm_ref[me]
        for cp in dm_copies:
            cp.wait()
        for k in range(1, N_DEV):
            dm2d[k:k + 1, :] = dm_recv[k]
        for cp in chip_copies:
            cp.wait()
        for a in range(nb):
            g_out[a][...] = ((s1[a][x, y] + chip_recv[a][0].astype(F32)) + chip_recv[a][1].astype(F32)) \
                + chip_recv[a][2].astype(F32)

    out_shapes = tuple(jax.ShapeDtypeStruct(s, F32) for s in big_shapes) + tuple(
        jax.ShapeDtypeStruct(p.shape[1:] if pk else p.shape, F32) for p, pk in zip(partials, pick_mine))
    outs = pl.pallas_call(
        body,
        name="tail_exchange",
        out_shape=out_shapes + (jax.ShapeDtypeStruct((2 * N_DEV, MOD_COLS), F32),),
        in_specs=[VMEM_SPEC] * (nb + n + 1),
        out_specs=(VMEM_SPEC,) * (nb + n + 1),
        scratch_shapes=[pltpu.VMEM((2, 2, *s), F32) for s in big_shapes]
        + [pltpu.VMEM((2, 2, *s), BF16) for s in big_shapes]
        + [pltpu.VMEM((3, *s), BF16) for s in big_shapes]
        + [pltpu.VMEM(p.shape, F32) for p in partials]
        + [pltpu.VMEM((3, *p.shape), F32) for p in partials]
        + [pltpu.VMEM((N_DEV, 1, MOD_COLS), F32), pltpu.SemaphoreType.DMA((3, n)), pltpu.SemaphoreType.DMA((3, n)),
           pltpu.SemaphoreType.DMA((N_DEV,)), pltpu.SemaphoreType.DMA((N_DEV,)),
           pltpu.SemaphoreType.DMA((nb, 4)), pltpu.SemaphoreType.DMA((nb, 4)),
           pltpu.SemaphoreType.DMA((nb, 3)), pltpu.SemaphoreType.DMA((nb, 3))],
        compiler_params=pltpu.CompilerParams(vmem_limit_bytes=VMEM_LIMIT_V7X),
    )(*big5, *partials, dmod3)
    return list(outs[:nb]), list(outs[nb:nb + n]), outs[nb + n]


def _small_update(grads, ws, ms, vs, scx, dm2d, w_ada, m_ada, v_ada, loss_lanes):
    n = len(grads)

    def body(*refs):
        g_in, w_in, m_in, v_in = (refs[q * n:(q + 1) * n] for q in range(4))
        scx_ref, dm_ref, wa_ref, ma_ref, va_ref, ll_ref = refs[4 * n:4 * n + 6]
        outs = refs[4 * n + 6:]
        g_out, d_out, nm_out, nv_out = (outs[q * (n + 1):(q + 1) * (n + 1)] for q in range(4))
        loss_ref = outs[4 * (n + 1)]
        for a in range(n + 1):
            if a < n:
                g, w, m, v = g_in[a][...], w_in[a][...], m_in[a][...], v_in[a][...]
            else:
                g = _mm_tn(_bf(scx_ref[...]), _bf(dm_ref[...]))
                w, m, v = wa_ref[...], ma_ref[...], va_ref[...]
            g_out[a][...] = g
            delta, m, v = _adamw(w, g, m, v)
            d_out[a][...] = delta
            nm_out[a][...] = m
            nv_out[a][...] = v
        loss_ref[...] = jnp.sum(ll_ref[...], axis=1, keepdims=True) * (0.5 / D)

    w_shapes = tuple(jax.ShapeDtypeStruct(w.shape, F32) for w in list(ws) + [w_ada])
    outs = pl.pallas_call(
        body,
        name="small_update",
        out_shape=w_shapes * 4 + (jax.ShapeDtypeStruct((1, 1), F32),),
        in_specs=[VMEM_SPEC] * (4 * n + 6),
        out_specs=(VMEM_SPEC,) * (4 * (n + 1) + 1),
        compiler_params=pltpu.CompilerParams(vmem_limit_bytes=VMEM_LIMIT_V7X),
    )(*grads, *ws, *ms, *vs, scx, dm2d, w_ada, m_ada, v_ada, loss_lanes)
    return [tuple(outs[q * (n + 1) + k] for q in range(4)) for k in range(n + 1)], outs[4 * (n + 1)]


def kernel(x, c, w_ada, b_ada, pre_mix_g, post_mix_g, w_in, sgu_norm_g, w_spatial, b_spatial, w_pool, pool_scale, w_out, pre_ffn_g, post_ffn_g, w_up, conv_w, conv_b, w_down, loss_target, m_w_ada, m_b_ada, m_pre_mix_g, m_post_mix_g, m_w_in, m_sgu_norm_g, m_w_spatial, m_b_spatial, m_w_pool, m_pool_scale, m_w_out, m_pre_ffn_g, m_post_ffn_g, m_w_up, m_conv_w, m_conv_b, m_w_down, v_w_ada, v_b_ada, v_pre_mix_g, v_post_mix_g, v_w_in, v_sgu_norm_g, v_w_spatial, v_b_spatial, v_w_pool, v_pool_scale, v_w_out, v_pre_ffn_g, v_post_ffn_g, v_w_up, v_conv_w, v_conv_b, v_w_down):
    t_len = x.shape[1]
    ts = min(256, t_len)
    ts_mix = min(512, t_len)
    ts_w = min(1024, t_len)
    coords = jnp.stack([lax.axis_index("x"), lax.axis_index("y"), lax.axis_index("c")]).astype(jnp.int32)

    w_in_t, w_up_t = w_in[0].T, w_up[0].T
    mod3, scx, (g_in, g_out) = _prologue(c, w_ada[0], b_ada.reshape(N_DEV, 1, MOD_COLS), [w_in_t, w_out[0]],
                                         [BF16, BF16])
    mod = mod3.reshape(N_MOD, D)
    w_in_tb = g_in.reshape(IN_WIDTH, D)
    w_out_b = g_out.reshape(D, D)
    conv_b8 = conv_b.reshape(N_DEV, FF_CHUNK)
    b_sp_t = b_spatial[0].T

    x2d, tgt = x[0], loss_target[0]
    (x1, proj, mixed), (g_up, g_down, g_cw) = _mix_fwd(
        x2d, mod, pre_mix_g, post_mix_g, w_in_tb, sgu_norm_g[0], w_spatial[0], b_sp_t, w_pool[0], pool_scale, w_out_b,
        ts_mix, [w_up_t, w_down[0], conv_w[0]], [w_up.shape[1:], w_down.shape[1:], conv_w.shape[1:]], [BF16, BF16, F32])
    w_down_b = g_down.reshape(FF, D)
    up, f, dx2, loss_lanes = _ffn_fwd(x1, tgt, mod, pre_ffn_g, post_ffn_g, g_up, g_cw, conv_b8, w_down_b, ts)

    (dx1, dup, act, df, h2, dmod_f, d_pre_ffn, d_post_ffn, d_cb8, d_cw8) = _ffn_bwd(
        dx2, f, x1, up, mod, pre_ffn_g, post_ffn_g, g_up, g_cw, conv_b8, w_down_b, ts)
    gw_up = _wgrad_up(h2, dup, ts_w).reshape(2, 2, 2, FF_CHUNK, D)
    gw_down, r1_up = _wgrad_down(act, df, ts_w, gw_up)
    gw_down = gw_down.reshape(2, 2, 2, FF // N_DEV, D)
    s1_up, s1_up_b, r1_down = _sibling_add("up", [gw_up], [r1_up], coords, swap_srcs=[gw_down])
    s1_down, s1_down_b, _ = _sibling_add("down", [gw_down], r1_down, coords)
    s1_ffn, s1_ffn_b = s1_up + s1_down, s1_up_b + s1_down_b
    ((grad_x, gw_in, gw_out, dmod_m, d_pre_mix, d_post_mix, d_sgn, d_wsp, d_bsp_t, d_wpool, d_ps), r_ffn) = _mix_bwd(
        dx1, x2d, proj, mixed, mod, pre_mix_g, post_mix_g, w_in_tb, sgu_norm_g[0], w_spatial[0], b_sp_t,
        w_pool[0], pool_scale, w_out_b, ts_mix, s1_ffn_b)
    gw_in = gw_in.reshape(N_DEV, IN_WIDTH // N_DEV, D)
    gw_out = gw_out.reshape(N_DEV, D // N_DEV, D)

    big = _final_add_adamw(coords, s1_ffn, list(r_ffn), [w_up_t, w_down[0]], [m_w_up[0].T, m_w_down[0]],
                           [v_w_up[0].T, v_w_down[0]])
    r_up, r_down = tuple(a.T[None] for a in big[0]), tuple(a[None] for a in big[1])

    dmod = jnp.concatenate([dmod_m, dmod_f], axis=0)
    names = ["b_ada", "pre_mix_g", "post_mix_g", "sgu_norm_g", "w_spatial", "b_spatial", "w_pool", "pool_scale",
             "pre_ffn_g", "post_ffn_g", "conv_w", "conv_b"]
    partials = [dmod.reshape(1, N_MOD * D), d_pre_mix, d_post_mix, d_sgn, d_wsp, d_bsp_t.T, d_wpool, d_ps,
                d_pre_ffn, d_post_ffn, d_cw8, d_cb8.reshape(1, 2 * FF), loss_lanes]
    small_w = [b_ada, pre_mix_g, post_mix_g, sgu_norm_g[0], w_spatial[0], b_spatial[0], w_pool[0], pool_scale,
               pre_ffn_g, post_ffn_g, conv_w[0], conv_b]
    small_m = [m_b_ada, m_pre_mix_g, m_post_mix_g, m_sgu_norm_g[0], m_w_spatial[0], m_b_spatial[0], m_w_pool[0],
               m_pool_scale, m_pre_ffn_g, m_post_ffn_g, m_conv_w[0], m_conv_b]
    small_v = [v_b_ada, v_pre_mix_g, v_post_mix_g, v_sgu_norm_g[0], v_w_spatial[0], v_b_spatial[0], v_w_pool[0],
               v_pool_scale, v_pre_ffn_g, v_post_ffn_g, v_conv_w[0], v_conv_b]
    g_mix, sums, dm2d = _tail_exchange([gw_in, gw_out], partials, [nm == "conv_w" for nm in names] + [False],
                                       [int(nm == "w_pool") for nm in names] + [0],
                                       dmod.reshape(N_DEV, 1, MOD_COLS))
    small, loss11 = _small_update(
        sums[:-1] + g_mix, small_w + [w_in_t, w_out[0]], small_m + [m_w_in[0].T, m_w_out[0]],
        small_v + [v_w_in[0].T, v_w_out[0]], scx, dm2d, w_ada[0], m_w_ada[0], v_w_ada[0], sums[-1])
    loss = loss11.reshape(())
    lead = {"sgu_norm_g", "w_spatial", "b_spatial", "w_pool", "conv_w", "w_in", "w_out", "w_ada"}
    res = {nm: tuple((a.T if nm == "w_in" else a)[None] if nm in lead else a for a in four)
           for nm, four in zip(names + ["w_in", "w_out", "w_ada"], small)}
    res.update(w_up=r_up, w_down=r_down)

    order = ["w_ada", "b_ada", "pre_mix_g", "post_mix_g", "w_in", "sgu_norm_g", "w_spatial", "b_spatial", "w_pool",
             "pool_scale", "w_out", "pre_ffn_g", "post_ffn_g", "w_up", "conv_w", "conv_b", "w_down"]
    return (loss, grad_x[None], *[res[nm][0] for nm in order], *[res[nm][1] for nm in order],
            *[res[nm][2] for nm in order], *[res[nm][3] for nm in order])
```

```python
import functools
import math

import jax
import jax.numpy as jnp
from jax import lax
from jax.experimental import pallas as pl
from jax.experimental.pallas import tpu as pltpu

F32 = jnp.float32
BF16 = jnp.bfloat16
MESH = pl.DeviceIdType.MESH

EPS = 1e-6
D = 1024
HEAD = 128
N_HEAD = 4
A_WIDTH = 512
B_WIDTH = 512
IN_WIDTH = 1536
WINDOWS = (2, 4, 8, 16)
CHUNK = 64
FF = 2816
N_DEV = 8
FF_CHUNK = 704
N_MOD = 6
MOD_COLS = 768

ADAM_LR = 0.001
ADAM_B1 = 0.9
ADAM_B2 = 0.999
ADAM_EPS = 1e-08
ADAM_WD = 0.01
ADAM_STEP = 10

VMEM_LIMIT_V7X = 62 * 1024 * 1024
HALO = 8
POOL_HALO = 16

VMEM_SPEC = pl.BlockSpec(memory_space=pltpu.VMEM)
ANY_SPEC = pl.BlockSpec(memory_space=pl.ANY)


def _bf(x):
    return x.astype(BF16)


def _mm(a, b):
    return jnp.dot(a, b, preferred_element_type=F32)


def _mm_nt(a, b):
    return lax.dot_general(a, b, (((1,), (1,)), ((), ())), preferred_element_type=F32)


def _mm_tn(a, b):
    return lax.dot_general(a, b, (((0,), (0,)), ((), ())), preferred_element_type=F32)


def _rstd(x):
    return lax.rsqrt(jnp.mean(x * x, axis=-1, keepdims=True) + EPS)


def _sum0(x):
    return jnp.sum(x, axis=0, keepdims=True)


def _rowmean(x):
    return jnp.mean(x, axis=-1, keepdims=True)


_GELU_K = math.sqrt(2.0 / math.pi)


def _gelu_and_grad(x):
    x2 = x * x
    th = jnp.tanh(_GELU_K * (x + 0.044715 * (x * x2)))
    cdf = 0.5 * th + 0.5
    grad = cdf + x * (1.0 - th * th) * ((0.5 * _GELU_K) + (1.5 * 0.044715 * _GELU_K) * x2)
    return x * cdf, grad


def _gelu(x):
    return x * (0.5 * (1.0 + jnp.tanh(_GELU_K * (x + 0.044715 * (x * x * x)))))


def _sigmoid(x):
    return 0.5 * jnp.tanh(0.5 * x) + 0.5


def _sgu_mask():
    ri = lax.broadcasted_iota(jnp.int32, (HEAD, HEAD), 0)
    ci = lax.broadcasted_iota(jnp.int32, (HEAD, HEAD), 1)
    return (ci // CHUNK) <= (ri // CHUNK)


def _window_sum(ext, w, trailing):
    n = ext.shape[0]
    s, k = ext, 1
    while k < w:
        s = s + pltpu.roll(s, k if trailing else n - k, 0)
        k *= 2
    return s


def _inv_count(row0, n, w):
    t = row0 + lax.broadcasted_iota(jnp.int32, (n, 1), 0)
    return 1.0 / jnp.minimum(t + 1, w).astype(F32)


def _shift_down(v, before, k):
    rows = lax.broadcasted_iota(jnp.int32, before.shape, 0)
    r = pltpu.roll(v, k, 0)
    top = jnp.where(rows < k, pltpu.roll(before, k, 0), r[0:HALO])
    return jnp.concatenate([top, r[HALO:]], axis=0)


def _shift_up(v, after, k):
    n = v.shape[0]
    rows = lax.broadcasted_iota(jnp.int32, after.shape, 0)
    r = pltpu.roll(v, n - k, 0)
    bottom = jnp.where(rows >= HALO - k, pltpu.roll(after, HALO - k, 0), r[n - HALO:])
    return jnp.concatenate([r[:n - HALO], bottom], axis=0)


def _adamw(w, g, m, v):
    m = ADAM_B1 * m + (1.0 - ADAM_B1) * g
    v = ADAM_B2 * v + (1.0 - ADAM_B2) * (g * g)
    m_hat = m / (1.0 - ADAM_B1 ** ADAM_STEP)
    v_hat = v / (1.0 - ADAM_B2 ** ADAM_STEP)
    delta = -ADAM_LR * (m_hat / (jnp.sqrt(v_hat) + ADAM_EPS) + ADAM_WD * w)
    return delta, m, v


def _coords():
    return lax.axis_index("x"), lax.axis_index("y"), lax.axis_index("c")


def _peer(k):
    x, y, c = _coords()
    return (x ^ ((k >> 2) & 1), y ^ ((k >> 1) & 1), c ^ (k & 1))


def _my_index():
    x, y, c = _coords()
    return 4 * x + 2 * y + c


def _adaln_modulation(c_ref, w_ref, b_ref, mod_ref, scx_ref, scbuf, stage, recv, send_sems, recv_sems):
    me = _my_index()
    cv = c_ref[...]
    scbuf[0] = cv * _sigmoid(cv)
    first = [
        pltpu.make_async_remote_copy(scbuf.at[0], scbuf.at[k], send_sems.at[0, k], recv_sems.at[0, k],
                                     device_id=_peer(k), device_id_type=MESH)
        for k in range(1, N_DEV)
    ]
    for cp in first:
        cp.start()
    for cp in first:
        cp.wait()
    scx_ref[...] = jnp.zeros(scx_ref.shape, F32)
    for k in range(N_DEV):
        scx_ref[k:k + 1, :] = scbuf[k]
    prod = _mm(_bf(scx_ref[...]), _bf(w_ref[...]))
    for k in range(N_DEV):
        stage[k] = prod[k:k + 1, :] + b_ref[me]
    second = [
        pltpu.make_async_remote_copy(stage.at[k], recv.at[k], send_sems.at[1, k], recv_sems.at[1, k],
                                     device_id=_peer(k), device_id_type=MESH)
        for k in range(1, N_DEV)
    ]
    for cp in second:
        cp.start()
    mod_ref[me] = stage[0]
    for cp in second:
        cp.wait()
    for k in range(1, N_DEV):
        mod_ref[me ^ k] = recv[k]


class _GatherSteps:
    def __init__(self, ins, outs, stages, send_sems, recv_sems, local_sems):
        self.ins, self.outs, self.stages = ins, outs, stages
        self.send_sems, self.recv_sems, self.local_sems = send_sems, recv_sems, local_sems
        x, y, c = _coords()
        self.c = c
        self.me, self.sibling = (x, y, c), (x, y, 1 - c)
        self.chips = [(1 - x, y), (x, 1 - y), (1 - x, 1 - y)]

    def _copy(self, a, k, block, to, from_stage=False):
        dst = self.outs[a].at[4 * block[0] + 2 * block[1] + block[2]]
        return pltpu.make_async_remote_copy(self.stages[a] if from_stage else dst, dst, self.send_sems.at[a, k],
                                            self.recv_sems.at[a, k], device_id=to, device_id_type=MESH)

    def _local(self, a):
        me = self.me
        return pltpu.make_async_copy(self.stages[a], self.outs[a].at[4 * me[0] + 2 * me[1] + me[2]],
                                     self.local_sems.at[a])

    def _first(self, a):
        cps = [self._copy(a, 0, self.me, self.sibling, from_stage=True)]
        return cps + [self._copy(a, 1 + j, self.me, (*chip, self.c), from_stage=True)
                      for j, chip in enumerate(self.chips)]

    def _passed(self, a, j):
        return self._copy(a, 4 + j, (*self.chips[j], self.c), self.sibling)

    def start(self):
        for a in range(len(self.ins)):
            block = self.ins[a][...]
            if block.shape != self.stages[a].shape:
                block = block.T
            self.stages[a][...] = block.astype(self.stages[a].dtype)
            self._local(a).start()
            for cp in self._first(a):
                cp.start()

    def forward(self, arrays=None):
        for a in range(len(self.ins)) if arrays is None else arrays:
            for j, chip in enumerate(self.chips):
                self._copy(a, 1 + j, (*chip, self.c), self.me).wait_recv()
                self._passed(a, j).start()

    def finish(self):
        for a in range(len(self.ins)):
            self._copy(a, 0, self.sibling, self.me).wait_recv()
            for j, chip in enumerate(self.chips):
                self._copy(a, 4 + j, (*chip, 1 - self.c), self.me).wait_recv()
            for cp in self._first(a) + [self._passed(a, j) for j in range(3)]:
                cp.wait_send()
            self._local(a).wait()


def _gather_scratch(shapes, out_dtypes):
    n = len(shapes)
    return ([pltpu.VMEM(s, dt) for s, dt in zip(shapes, out_dtypes)]
            + [pltpu.SemaphoreType.DMA((n, 7)), pltpu.SemaphoreType.DMA((n, 7)), pltpu.SemaphoreType.DMA((n,))])


def _gather_out_shapes(shapes, out_dtypes):
    return tuple(jax.ShapeDtypeStruct((N_DEV, *s), dt) for s, dt in zip(shapes, out_dtypes))


def _prologue(c_row, w_ada, b_ada3, shards, out_dtypes):
    n = len(shards)

    def body(*refs):
        c_ref, w_ref, b_ref = refs[:3]
        mod_ref, scx_ref = refs[3 + n:5 + n]
        gather = _GatherSteps(refs[3:3 + n], refs[5 + n:5 + 2 * n], refs[5 + 2 * n:5 + 3 * n],
                              *refs[5 + 3 * n:8 + 3 * n])
        gather.start()
        _adaln_modulation(c_ref, w_ref, b_ref, mod_ref, scx_ref, *refs[8 + 3 * n:])
        gather.forward()
        gather.finish()

    outs = pl.pallas_call(
        body,
        name="prologue",
        out_shape=(jax.ShapeDtypeStruct((N_DEV, 1, MOD_COLS), F32), jax.ShapeDtypeStruct((2 * N_DEV, D), F32))
        + _gather_out_shapes([s.shape for s in shards], out_dtypes),
        in_specs=[VMEM_SPEC] * (3 + n),
        out_specs=(VMEM_SPEC, VMEM_SPEC) + (ANY_SPEC,) * n,
        scratch_shapes=_gather_scratch([s.shape for s in shards], out_dtypes) + [
            pltpu.VMEM((N_DEV, 1, D), F32),
            pltpu.VMEM((N_DEV, 1, MOD_COLS), F32),
            pltpu.VMEM((N_DEV, 1, MOD_COLS), F32),
            pltpu.SemaphoreType.DMA((2, N_DEV)),
            pltpu.SemaphoreType.DMA((2, N_DEV)),
        ],
        compiler_params=pltpu.CompilerParams(vmem_limit_bytes=VMEM_LIMIT_V7X),
    )(c_row, w_ada, b_ada3, *shards)
    return outs[0], outs[1], outs[2:]


class _ChipExchangeSteps:
    FLIPS = ((1, 0), (0, 1), (1, 1))

    def __init__(self, srcs, dsts, send_sems, recv_sems):
        self.srcs, self.dsts, self.send_sems, self.recv_sems = srcs, dsts, send_sems, recv_sems

    def _copies(self):
        x, y, c = _coords()
        out = []
        for a in range(len(self.srcs)):
            for j, (fx, fy) in enumerate(self.FLIPS):
                k = 3 * a + j
                out.append(pltpu.make_async_remote_copy(
                    self.srcs[a].at[x ^ fx, y ^ fy], self.dsts[a].at[j], self.send_sems.at[k], self.recv_sems.at[k],
                    device_id=(x ^ fx, y ^ fy, c), device_id_type=MESH))
        return out

    def start(self):
        for cp in self._copies():
            cp.start()

    def finish(self):
        for cp in self._copies():
            cp.wait()


def _mix_fwd(x, mod, g_pre, g_post, w_in_t, sgn, w_sp, b_sp_t, w_pool, p_scale, w_out_b, ts, shards, shard_shapes,
             shard_dtypes):
    t_len = x.shape[0]
    nt, nb = t_len // ts, ts // HEAD
    ns = len(shards)

    def body(*refs):
        (x_ref, mod_ref, g1_ref, g2_ref, win_ref, sgn_ref, ws_ref, bst_ref, wp_ref, ps_ref, wout_ref) = refs[:11]
        x1_ref, proj_ref, mixed_ref = refs[11 + ns:14 + ns]
        pbuf, cat = refs[14 + 2 * ns:16 + 2 * ns]
        gather = _GatherSteps(refs[11:11 + ns], refs[14 + ns:14 + 2 * ns], refs[16 + 2 * ns:16 + 3 * ns],
                              *refs[16 + 3 * ns:])
        i = pl.program_id(0)

        @pl.when(i == 0)
        def _():
            pbuf[0:POOL_HALO, :] = jnp.zeros((POOL_HALO, B_WIDTH), F32)
            gather.start()

        @pl.when(i == (13 * nt) // 16)
        def _():
            gather.forward([0])

        @pl.when(i == nt - 1)
        def _():
            gather.forward(range(1, ns))

        xv = x_ref[...]
        sh, sc, gm = mod_ref[0:1, :], mod_ref[1:2, :], mod_ref[2:3, :]
        h = (xv * _rstd(xv) * g1_ref[...]) * (1.0 + sc) + sh
        proj_ref[...] = _mm_nt(_bf(h), win_ref[...])
        pbuf[POOL_HALO:POOL_HALO + ts, :] = proj_ref[:, 2 * A_WIDTH:]
        smask = _sgu_mask()
        for hd in range(N_HEAD):
            u = _gelu(proj_ref[:, hd * HEAD:(hd + 1) * HEAD])
            v = _gelu(proj_ref[:, A_WIDTH + hd * HEAD:A_WIDTH + (hd + 1) * HEAD])
            vn = _bf(v * _rstd(v) * sgn_ref[hd:hd + 1, :])
            wm = _bf(jnp.where(smask, ws_ref[hd], 0.0))
            bias = bst_ref[:, hd:hd + 1]
            for b in range(nb):
                rows = slice(b * HEAD, (b + 1) * HEAD)
                z = _mm(wm, vn[rows]) + bias
                cat[rows, hd * HEAD:(hd + 1) * HEAD] = _bf(u[rows] * z)
        for g, w in enumerate(WINDOWS):
            cols = slice(g * HEAD, (g + 1) * HEAD)
            ext = pbuf[:, cols]
            pooled = _window_sum(ext, w, True)[POOL_HALO:] * _inv_count(i * ts, ts, w) - ext[POOL_HALO:]
            cat[:, A_WIDTH + g * HEAD:A_WIDTH + (g + 1) * HEAD] = _bf(_mm(_bf(pooled), _bf(wp_ref[g])) * ps_ref[:, cols])
        pbuf[0:POOL_HALO, :] = pbuf[ts:ts + POOL_HALO, :]
        mixed = _mm(cat[...], wout_ref[...])
        mixed_ref[...] = mixed
        x1_ref[...] = xv + gm * (mixed * _rstd(mixed) * g2_ref[...])

        @pl.when(i == nt - 1)
        def _():
            gather.finish()

    tile = lambda wid: pl.BlockSpec((ts, wid), lambda i: (i, 0))
    outs = pl.pallas_call(
        body,
        name="mix_fwd",
        grid=(nt,),
        out_shape=(jax.ShapeDtypeStruct((t_len, D), F32), jax.ShapeDtypeStruct((t_len, IN_WIDTH), F32),
                   jax.ShapeDtypeStruct((t_len, D), F32)) + _gather_out_shapes(shard_shapes, shard_dtypes),
        in_specs=[tile(D)] + [VMEM_SPEC] * (10 + ns),
        out_specs=(tile(D), tile(IN_WIDTH), tile(D)) + (ANY_SPEC,) * ns,
        scratch_shapes=[pltpu.VMEM((POOL_HALO + ts, B_WIDTH), F32), pltpu.VMEM((ts, D), BF16)]
        + _gather_scratch(shard_shapes, shard_dtypes),
        compiler_params=pltpu.CompilerParams(dimension_semantics=("arbitrary",), vmem_limit_bytes=VMEM_LIMIT_V7X),
    )(x, mod, g_pre, g_post, w_in_t, sgn, w_sp, b_sp_t, w_pool, p_scale, w_out_b, *shards)
    return outs[:3], outs[3:]


def _ffn_fwd(x1, target, mod, g_pre, g_post, w_up_b, conv_w8, conv_b8, w_down_b, ts):
    t_len = x1.shape[0]
    nt = t_len // ts

    def body(x1_ref, tgt_ref, mod_ref, g3_ref, g4_ref, wup_ref, cw_ref, cb_ref, wdown_ref,
             up_ref, f_ref, dx2_ref, loss_ref, ucarry):
        i = pl.program_id(0)

        @pl.when(i == 0)
        def _():
            ucarry[...] = jnp.zeros(ucarry.shape, F32)
            loss_ref[...] = jnp.zeros(loss_ref.shape, F32)

        x1v = x1_ref[...]
        sh, sc, gf = mod_ref[3:4, :], mod_ref[4:5, :], mod_ref[5:6, :]
        h2 = _bf((x1v * _rstd(x1v) * g3_ref[...]) * (1.0 + sc) + sh)
        half = N_DEV // 2

        def up_pair(j):
            return [_mm(h2, wup_ref[jj]) for jj in (j, j + half)]

        f = jnp.zeros((ts, D), F32)
        ups = up_pair(0)
        for j in range(half):
            nxt = up_pair(j + 1) if j + 1 < half else None
            ys = []
            for up, jj in zip(ups, (j, j + half)):
                up_ref[jj] = up
                before = ucarry[jj]
                ucarry[jj] = up[ts - HALO:, :]
                cw = cw_ref[jj]
                ys.append(cb_ref[jj:jj + 1, :] + _shift_down(up, before, 2) * cw[0:1, :]
                          + _shift_down(up, before, 1) * cw[1:2, :] + up * cw[2:3, :])
            gate, val = ys
            act = gate * _sigmoid(gate) * val
            f = f + _mm(_bf(act), wdown_ref[j * FF_CHUNK:(j + 1) * FF_CHUNK, :])
            ups = nxt
        f_ref[...] = f
        x2 = x1v + gf * (f * _rstd(f) * g4_ref[...])
        err = x2 - tgt_ref[...]
        loss_ref[...] += _sum0(err * err)
        dx2_ref[...] = err * (1.0 / D)

    tile = pl.BlockSpec((ts, D), lambda i: (i, 0))
    return pl.pallas_call(
        body,
        name="ffn_fwd",
        grid=(nt,),
        out_shape=(jax.ShapeDtypeStruct((N_DEV, t_len, FF_CHUNK), F32), jax.ShapeDtypeStruct((t_len, D), F32),
                   jax.ShapeDtypeStruct((t_len, D), F32), jax.ShapeDtypeStruct((1, D), F32)),
        in_specs=[tile, tile] + [VMEM_SPEC] * 7,
        out_specs=(pl.BlockSpec((N_DEV, ts, FF_CHUNK), lambda i: (0, i, 0)), tile, tile,
                   pl.BlockSpec((1, D), lambda i: (0, 0))),
        scratch_shapes=[pltpu.VMEM((N_DEV, HALO, FF_CHUNK), F32)],
        compiler_params=pltpu.CompilerParams(dimension_semantics=("arbitrary",), vmem_limit_bytes=VMEM_LIMIT_V7X),
    )(x1, target, mod, g_pre, g_post, w_up_b, conv_w8, conv_b8, w_down_b)


def _ffn_bwd(dx2, f, x1, up, mod, g_pre, g_post, w_up_b, conv_w8, conv_b8, w_down_b, ts):
    t_len = x1.shape[0]
    nt = t_len // ts
    half = N_DEV // 2

    def body(dx2_ref, f_ref, x1_ref, up_ref, halo_ref, mod_ref, g3_ref, g4_ref, wup_ref, cw_ref, cb_ref, wdown_ref,
             dx1_ref, dup_ref, act_ref, df_ref, h2_ref, dmod_ref, dg3_ref, dg4_ref, dcb_ref, dcw_ref,
             dycarry, dh2acc):
        i = pl.program_id(0)
        r = nt - 1 - i

        @pl.when(i == 0)
        def _():
            for ref in (dmod_ref, dg3_ref, dg4_ref, dcb_ref, dcw_ref, dycarry):
                ref[...] = jnp.zeros(ref.shape, F32)

        dx2v, fv, x1v = dx2_ref[...], f_ref[...], x1_ref[...]
        sh, sc, gf = mod_ref[3:4, :], mod_ref[4:5, :], mod_ref[5:6, :]
        g3, g4 = g3_ref[...], g4_ref[...]
        rstd4 = _rstd(fv)
        fh = fv * rstd4
        dmod_ref[2:3, :] += _sum0(dx2v * (fh * g4))
        dr = dx2v * gf
        dg4_ref[...] += _sum0(dr * fh)
        dfh = dr * g4
        dfb = _bf(rstd4 * (dfh - fh * _rowmean(dfh * fh)))
        df_ref[...] = dfb
        rstd3 = _rstd(x1v)
        xh = x1v * rstd3
        n3 = xh * g3
        h2_ref[...] = _bf(n3 * (1.0 + sc) + sh)
        dh2acc[...] = jnp.zeros((ts, D), F32)
        keep = jnp.where(r > 0, 1.0, 0.0).astype(F32)

        def dact_of(j):
            return _mm_nt(dfb, wdown_ref[j * FF_CHUNK:(j + 1) * FF_CHUNK, :])

        dact_next = dact_of(0)
        for j in range(half):
            dact = dact_next
            if j + 1 < half:
                dact_next = dact_of(j + 1)
            ys = []
            for jj in (j, j + half):
                before = halo_ref[jj] * keep
                upc = up_ref[jj]
                cw = cw_ref[jj]
                ys.append(cb_ref[jj:jj + 1, :] + _shift_down(upc, before, 2) * cw[0:1, :]
                          + _shift_down(upc, before, 1) * cw[1:2, :] + upc * cw[2:3, :])
            gate, val = ys
            sg = _sigmoid(gate)
            silu = gate * sg
            act_ref[j] = _bf(silu * val)
            dys = (dact * val * (sg + silu * (1.0 - sg)), dact * silu)
            for q, jj in enumerate((j, j + half)):
                dy = dys[q]
                cw = cw_ref[jj]
                dcb_ref[jj:jj + 1, :] += _sum0(dy)
                after = dycarry[jj]
                dycarry[jj] = dy[0:HALO, :]
                dy1, dy2 = _shift_up(dy, after, 1), _shift_up(dy, after, 2)
                upc = up_ref[jj]
                dcw_ref[jj, 0:1, :] += _sum0(dy2 * upc)
                dcw_ref[jj, 1:2, :] += _sum0(dy1 * upc)
                dcw_ref[jj, 2:3, :] += _sum0(dy * upc)
                dup = _bf(dy * cw[2:3, :] + dy1 * cw[1:2, :] + dy2 * cw[0:1, :])
                dup_ref[jj] = dup
                dh2acc[...] += _mm_nt(dup, wup_ref[jj])
        dh2 = dh2acc[...]
        dmod_ref[0:1, :] += _sum0(dh2)
        dmod_ref[1:2, :] += _sum0(dh2 * n3)
        dn3 = dh2 * (1.0 + sc)
        dg3_ref[...] += _sum0(dn3 * xh)
        dxh = dn3 * g3
        dx1_ref[...] = dx2v + rstd3 * (dxh - xh * _rowmean(dxh * xh))

    tile = pl.BlockSpec((ts, D), lambda i: (nt - 1 - i, 0))
    chunked = lambda n: pl.BlockSpec((n, ts, FF_CHUNK), lambda i: (0, nt - 1 - i, 0))
    halo = pl.BlockSpec((N_DEV, HALO, FF_CHUNK), lambda i: (0, jnp.maximum((nt - 1 - i) * (ts // HALO) - 1, 0), 0))
    const = lambda *shape: pl.BlockSpec(shape, lambda i: (0,) * len(shape))
    return pl.pallas_call(
        body,
        name="ffn_bwd",
        grid=(nt,),
        out_shape=(jax.ShapeDtypeStruct((t_len, D), F32), jax.ShapeDtypeStruct((N_DEV, t_len, FF_CHUNK), BF16),
                   jax.ShapeDtypeStruct((half, t_len, FF_CHUNK), BF16), jax.ShapeDtypeStruct((t_len, D), BF16),
                   jax.ShapeDtypeStruct((t_len, D), BF16), jax.ShapeDtypeStruct((3, D), F32),
                   jax.ShapeDtypeStruct((1, D), F32), jax.ShapeDtypeStruct((1, D), F32),
                   jax.ShapeDtypeStruct((N_DEV, FF_CHUNK), F32), jax.ShapeDtypeStruct((N_DEV, 3, FF_CHUNK), F32)),
        in_specs=[tile, tile, tile, chunked(N_DEV), halo] + [VMEM_SPEC] * 7,
        out_specs=(tile, chunked(N_DEV), chunked(half), tile, tile, const(3, D), const(1, D), const(1, D),
                   const(N_DEV, FF_CHUNK), const(N_DEV, 3, FF_CHUNK)),
        scratch_shapes=[pltpu.VMEM((N_DEV, HALO, FF_CHUNK), F32), pltpu.VMEM((ts, D), F32)],
        compiler_params=pltpu.CompilerParams(dimension_semantics=("arbitrary",), vmem_limit_bytes=VMEM_LIMIT_V7X),
    )(dx2, f, x1, up, up, mod, g_pre, g_post, w_up_b, conv_w8, conv_b8, w_down_b)


def _wgrad_up(h2, dup, ts):
    t_len = h2.shape[0]
    nt, half = t_len // ts, N_DEV // 2

    def body(h2_ref, dup_ref, out_ref):
        @pl.when(pl.program_id(1) == 0)
        def _():
            out_ref[...] = jnp.zeros(out_ref.shape, F32)

        for q in range(half):
            out_ref[q] += _mm_tn(dup_ref[q], h2_ref[...])

    return pl.pallas_call(
        body,
        name="wgrad_up",
        grid=(2, nt),
        out_shape=jax.ShapeDtypeStruct((N_DEV, FF_CHUNK, D), F32),
        in_specs=[pl.BlockSpec((ts, D), lambda g, t: (t, 0)), pl.BlockSpec((half, ts, FF_CHUNK), lambda g, t: (g, t, 0))],
        out_specs=pl.BlockSpec((half, FF_CHUNK, D), lambda g, t: (g, 0, 0)),
        compiler_params=pltpu.CompilerParams(dimension_semantics=("arbitrary", "arbitrary"),
                                             vmem_limit_bytes=VMEM_LIMIT_V7X),
    )(h2, dup)


def _sibling_swap_copies(srcs, dsts, send_sems, recv_sems):
    x, y, c = _coords()
    return [
        pltpu.make_async_remote_copy(srcs[a].at[xs, ys, 1 - c], dsts[a].at[xs, ys], send_sems.at[a, 2 * xs + ys],
                                     recv_sems.at[a, 2 * xs + ys], device_id=(x, y, 1 - c), device_id_type=MESH)
        for a in range(len(srcs)) for xs in range(2) for ys in range(2)
    ]


def _wgrad_down(act, df, ts, swap_src):
    t_len = df.shape[0]
    nt, half = t_len // ts, N_DEV // 2

    def body(act_ref, df_ref, src_ref, out_ref, dst_ref, send_sems, recv_sems):
        t = pl.program_id(0)

        @pl.when(t == 0)
        def _():
            for cp in _sibling_swap_copies([src_ref], [dst_ref], send_sems, recv_sems):
                cp.start()
            out_ref[...] = jnp.zeros(out_ref.shape, F32)

        for q in range(half):
            out_ref[q] += _mm_tn(act_ref[q], df_ref[...])

        @pl.when(t == nt - 1)
        def _():
            for cp in _sibling_swap_copies([src_ref], [dst_ref], send_sems, recv_sems):
                cp.wait()

    return pl.pallas_call(
        body,
        name="wgrad_down",
        grid=(nt,),
        out_shape=(jax.ShapeDtypeStruct((half, FF_CHUNK, D), F32), jax.ShapeDtypeStruct(swap_src.shape[1:], F32)),
        in_specs=[pl.BlockSpec((half, ts, FF_CHUNK), lambda t: (0, t, 0)), pl.BlockSpec((ts, D), lambda t: (t, 0)),
                  ANY_SPEC],
        out_specs=(pl.BlockSpec((half, FF_CHUNK, D), lambda t: (0, 0, 0)), ANY_SPEC),
        scratch_shapes=[pltpu.SemaphoreType.DMA((1, 4)), pltpu.SemaphoreType.DMA((1, 4))],
        compiler_params=pltpu.CompilerParams(dimension_semantics=("arbitrary",), vmem_limit_bytes=VMEM_LIMIT_V7X),
    )(act, df, swap_src)


def _mix_bwd(dx1, x, proj, mixed, mod, g_pre, g_post, w_in_t, sgn, w_sp, b_sp_t, w_pool, p_scale, w_out_b, ts, rs_srcs):
    t_len = x.shape[0]
    nt, nb = t_len // ts, ts // HEAD
    nr = len(rs_srcs)

    def body(*refs):
        (dx1_ref, x_ref, proj_ref, halo_ref, mixed_ref, mod_ref, g1_ref, g2_ref, win_ref, sgn_ref, ws_ref,
         bst_ref, wp_ref, ps_ref, wout_ref) = refs[:15]
        (gx_ref, dwin_ref, dwout_ref, dmod_ref, dg1_ref, dg2_ref, dsgn_ref, dws_ref, dbst_ref, dwp_ref,
         dps_ref) = refs[15 + nr:26 + nr]
        pbuf, dwsbuf, cat, dproj, dcat = refs[26 + 2 * nr:31 + 2 * nr]
        exchange = _ChipExchangeSteps(refs[15:15 + nr], refs[26 + nr:26 + 2 * nr], *refs[31 + 2 * nr:])
        i = pl.program_id(0)
        r = nt - 1 - i

        @pl.when(i == 0)
        def _():
            exchange.start()
            for ref in (dwin_ref, dwout_ref, dmod_ref, dg1_ref, dg2_ref, dsgn_ref, dws_ref, dbst_ref, dwp_ref, dps_ref):
                ref[...] = jnp.zeros(ref.shape, F32)
            dwsbuf[ts:ts + POOL_HALO, :] = jnp.zeros((POOL_HALO, B_WIDTH), F32)

        xv, dx1v, mixed = x_ref[...], dx1_ref[...], mixed_ref[...]
        sh, sc, gm = mod_ref[0:1, :], mod_ref[1:2, :], mod_ref[2:3, :]
        g1, g2 = g1_ref[...], g2_ref[...]
        rstd2 = _rstd(mixed)
        mh = mixed * rstd2
        dmod_ref[2:3, :] += _sum0(dx1v * (mh * g2))
        dr = dx1v * gm
        dg2_ref[...] += _sum0(dr * mh)
        dmh = dr * g2
        dmb = _bf(rstd2 * (dmh - mh * _rowmean(dmh * mh)))
        dcat[...] = _mm_nt(dmb, wout_ref[...])
        smask = _sgu_mask()
        for hd in range(N_HEAD):
            ucols = slice(hd * HEAD, (hd + 1) * HEAD)
            vcols = slice(A_WIDTH + hd * HEAD, A_WIDTH + (hd + 1) * HEAD)
            u, du_dp = _gelu_and_grad(proj_ref[:, ucols])
            v, dv_dp = _gelu_and_grad(proj_ref[:, vcols])
            rs = _rstd(v)
            vhat = v * rs
            gn = sgn_ref[hd:hd + 1, :]
            vn = _bf(vhat * gn)
            wm = _bf(jnp.where(smask, ws_ref[hd], 0.0))
            bias = bst_ref[:, hd:hd + 1]
            dzsum = jnp.zeros((HEAD, HEAD), F32)
            dwm = jnp.zeros((HEAD, HEAD), F32)
            dvn_parts = []
            for b in range(nb):
                rows = slice(b * HEAD, (b + 1) * HEAD)
                z = _mm(wm, vn[rows]) + bias
                da = dcat[rows, ucols]
                cat[rows, ucols] = _bf(u[rows] * z)
                dz = da * u[rows]
                dzsum = dzsum + dz
                dzb = _bf(dz)
                dwm = dwm + _mm_nt(dzb, vn[rows])
                dvn_parts.append(_mm_tn(wm, dzb))
                dproj[rows, ucols] = _bf((da * z) * du_dp[rows])
            dvn = jnp.concatenate(dvn_parts, axis=0)
            dsgn_ref[hd:hd + 1, :] += _sum0(dvn * vhat)
            dvh = dvn * gn
            dproj[:, vcols] = _bf((rs * (dvh - vhat * _rowmean(dvh * vhat))) * dv_dp)
            dws_ref[hd] += jnp.where(smask, dwm, 0.0)
            dbst_ref[:, hd:hd + 1] += jnp.sum(dzsum, axis=1, keepdims=True)
        keep = jnp.where(r > 0, 1.0, 0.0).astype(F32)
        pbuf[0:POOL_HALO, :] = halo_ref[...] * keep
        pbuf[POOL_HALO:POOL_HALO + ts, :] = proj_ref[:, 2 * A_WIDTH:]
        for g, w in enumerate(WINDOWS):
            cols = slice(g * HEAD, (g + 1) * HEAD)
            ccols = slice(A_WIDTH + g * HEAD, A_WIDTH + (g + 1) * HEAD)
            pcols = slice(2 * A_WIDTH + g * HEAD, 2 * A_WIDTH + (g + 1) * HEAD)
            wpg = _bf(wp_ref[g])
            psg = ps_ref[:, cols]
            ext = pbuf[:, cols]
            inv = _inv_count(r * ts, ts, w)
            pb = _bf(_window_sum(ext, w, True)[POOL_HALO:] * inv - ext[POOL_HALO:])
            yb = _mm(pb, wpg)
            dob = dcat[:, ccols]
            cat[:, ccols] = _bf(yb * psg)
            dps_ref[:, cols] += _sum0(dob * yb)
            dyb = _bf(dob * psg)
            dwp_ref[g] += _mm_tn(pb, dyb)
            dpooled = _mm_nt(dyb, wpg)
            dwsbuf[0:ts, cols] = dpooled * inv
            dproj[:, pcols] = _bf(_window_sum(dwsbuf[:, cols], w, False)[0:ts] - dpooled)
        dwsbuf[ts:ts + POOL_HALO, :] = dwsbuf[0:POOL_HALO, :]
        dpb = dproj[...]
        rstd1 = _rstd(xv)
        xh = xv * rstd1
        n1 = xh * g1
        dwin_ref[...] += _mm_tn(dpb, _bf(n1 * (1.0 + sc) + sh))
        dwout_ref[...] += _mm_tn(cat[...], dmb)
        dh = _mm(dpb, win_ref[...])
        dmod_ref[0:1, :] += _sum0(dh)
        dmod_ref[1:2, :] += _sum0(dh * n1)
        dn1 = dh * (1.0 + sc)
        dg1_ref[...] += _sum0(dn1 * xh)
        dxh = dn1 * g1
        gx_ref[...] = dx1v + rstd1 * (dxh - xh * _rowmean(dxh * xh))

        @pl.when(i == nt - 1)
        def _():
            exchange.finish()

    tile = lambda wid: pl.BlockSpec((ts, wid), lambda i: (nt - 1 - i, 0))
    halo = pl.BlockSpec((POOL_HALO, B_WIDTH),
                        lambda i: (jnp.maximum((nt - 1 - i) * (ts // POOL_HALO) - 1, 0), 2 * A_WIDTH // B_WIDTH))
    const = lambda *shape: pl.BlockSpec(shape, lambda i: (0,) * len(shape))
    resident = lambda *shape: pl.BlockSpec(shape, lambda i: (0,) * len(shape), pipeline_mode=pl.Buffered(1))
    outs = pl.pallas_call(
        body,
        name="mix_bwd",
        grid=(nt,),
        out_shape=(jax.ShapeDtypeStruct((t_len, D), F32), jax.ShapeDtypeStruct((IN_WIDTH, D), F32),
                   jax.ShapeDtypeStruct((D, D), F32), jax.ShapeDtypeStruct((3, D), F32),
                   jax.ShapeDtypeStruct((1, D), F32), jax.ShapeDtypeStruct((1, D), F32),
                   jax.ShapeDtypeStruct((N_HEAD, HEAD), F32), jax.ShapeDtypeStruct((N_HEAD, HEAD, HEAD), F32),
                   jax.ShapeDtypeStruct((HEAD, N_HEAD), F32), jax.ShapeDtypeStruct((N_HEAD, HEAD, HEAD), F32),
                   jax.ShapeDtypeStruct((1, B_WIDTH), F32))
        + tuple(jax.ShapeDtypeStruct((3, *s.shape[2:]), s.dtype) for s in rs_srcs),
        in_specs=[tile(D), tile(D), tile(IN_WIDTH), halo, tile(D)] + [VMEM_SPEC] * 10 + [ANY_SPEC] * nr,
        out_specs=(tile(D), resident(IN_WIDTH, D), resident(D, D), const(3, D), const(1, D), const(1, D),
                   const(N_HEAD, HEAD), const(N_HEAD, HEAD, HEAD), const(HEAD, N_HEAD), const(N_HEAD, HEAD, HEAD),
                   const(1, B_WIDTH)) + (ANY_SPEC,) * nr,
        scratch_shapes=[pltpu.VMEM((POOL_HALO + ts, B_WIDTH), F32), pltpu.VMEM((ts + POOL_HALO, B_WIDTH), F32),
                        pltpu.VMEM((ts, D), BF16), pltpu.VMEM((ts, IN_WIDTH), BF16), pltpu.VMEM((ts, D), F32),
                        pltpu.SemaphoreType.DMA((3 * nr,)), pltpu.SemaphoreType.DMA((3 * nr,))],
        compiler_params=pltpu.CompilerParams(dimension_semantics=("arbitrary",), vmem_limit_bytes=VMEM_LIMIT_V7X),
    )(dx1, x, proj, proj, mixed, mod, g_pre, g_post, w_in_t, sgn, w_sp, b_sp_t, w_pool, p_scale, w_out_b, *rs_srcs)
    return outs[:11], outs[11:]


def _pair_add(name, coords, grid, specs_a, specs_b, out_specs, out_shapes, a_arrays, b_arrays, swap_srcs):
    n, ns = len(a_arrays), len(swap_srcs)
    last = tuple(g - 1 for g in grid)

    def body(co_ref, *refs):
        ids = [pl.program_id(d) for d in range(len(grid))]
        swap = refs[2 * n:2 * n + ns], refs[4 * n + ns:4 * n + 2 * ns], *refs[4 * n + 2 * ns:]
        if ns:
            @pl.when(functools.reduce(jnp.logical_and, [i == 0 for i in ids]))
            def _():
                for cp in _sibling_swap_copies(*swap):
                    cp.start()

        for k in range(n):
            total = refs[k][...] + refs[n + k][...]
            refs[2 * n + ns + k][...] = total
            refs[3 * n + ns + k][...] = _bf(total)

        if ns:
            @pl.when(functools.reduce(jnp.logical_and, [i == e for i, e in zip(ids, last)]))
            def _():
                for cp in _sibling_swap_copies(*swap):
                    cp.wait()

    outs = pl.pallas_call(
        body,
        name=name,
        grid_spec=pltpu.PrefetchScalarGridSpec(
            num_scalar_prefetch=1, grid=grid, in_specs=specs_a + specs_b + [ANY_SPEC] * ns,
            out_specs=out_specs * 2 + [ANY_SPEC] * ns,
            scratch_shapes=[pltpu.SemaphoreType.DMA((ns, 4)), pltpu.SemaphoreType.DMA((ns, 4))] if ns else []),
        out_shape=tuple(jax.ShapeDtypeStruct(s, dt) for dt in (F32, BF16) for s in out_shapes)
        + tuple(jax.ShapeDtypeStruct(g.shape[1:], F32) for g in swap_srcs),
        compiler_params=pltpu.CompilerParams(dimension_semantics=("arbitrary",) * len(grid),
                                             vmem_limit_bytes=VMEM_LIMIT_V7X),
    )(coords, *a_arrays, *b_arrays, *swap_srcs)
    return list(outs[:n]), list(outs[n:2 * n]), list(outs[2 * n:])


def _final_add_adamw(coords, s1, r, ws, ms, vs, n_split=4):
    n = len(s1)

    def body(co_ref, *refs):
        for k in range(n):
            s_ref, r_ref, w_ref, m_ref, v_ref = (refs[q * n + k] for q in range(5))
            g_ref, d_ref, nm_ref, nv_ref = (refs[(5 + q) * n + k] for q in range(4))
            g = ((s_ref[...] + r_ref[0].astype(F32)) + r_ref[1].astype(F32)) + r_ref[2].astype(F32)
            g_ref[...] = g
            delta, m, v = _adamw(w_ref[...], g, m_ref[...], v_ref[...])
            d_ref[...] = delta
            nm_ref[...] = m
            nv_ref[...] = v

    def shard_spec(a):
        rows, cols = a.shape
        return pl.BlockSpec((rows // n_split, cols), lambda i, co: (i, 0))

    def mine_spec(a):
        rows, cols = a.shape[2:]
        return pl.BlockSpec((None, None, rows // n_split, cols), lambda i, co: (co[0], co[1], i, 0))

    def recv_spec(a):
        rows, cols = a.shape[1:]
        return pl.BlockSpec((3, rows // n_split, cols), lambda i, co: (0, i, 0))

    in_specs = ([mine_spec(a) for a in s1] + [recv_spec(a) for a in r] + [shard_spec(a) for a in ws] * 3)
    out_specs = [shard_spec(a) for a in ws] * 4
    outs = pl.pallas_call(
        body,
        name="grad_final_adamw",
        grid_spec=pltpu.PrefetchScalarGridSpec(num_scalar_prefetch=1, grid=(n_split,), in_specs=in_specs,
                                               out_specs=out_specs),
        out_shape=tuple(jax.ShapeDtypeStruct(a.shape, F32) for a in ws) * 4,
        compiler_params=pltpu.CompilerParams(dimension_semantics=("arbitrary",), vmem_limit_bytes=VMEM_LIMIT_V7X),
    )(coords, *s1, *r, *ws, *ms, *vs)
    return [tuple(outs[q * n + k] for q in range(4)) for k in range(n)]


def _sibling_add(tag, g5, r1, coords, swap_srcs=(), n_split=4):
    shapes = [g.shape[3:] for g in g5]
    spec_g = [pl.BlockSpec((None, None, None, s[0] // n_split, s[1]), lambda i, j, k, co: (i, j, co[2], k, 0))
              for s in shapes]
    spec_r = [pl.BlockSpec((None, None, s[0] // n_split, s[1]), lambda i, j, k, co: (i, j, k, 0)) for s in shapes]
    return _pair_add("grad_add_core_" + tag, coords, (2, 2, n_split), spec_g, spec_r, spec_r,
                     [(2, 2, *s) for s in shapes], g5, r1, list(swap_srcs))


def _tail_exchange(big, partials, pick_mine, y_first, dmod3):
    n, nb = len(partials), len(big)
    big_shapes = [g.shape[1:] for g in big]
    big5 = [g.reshape(2, 2, 2, *s) for g, s in zip(big, big_shapes)]
    flips = _ChipExchangeSteps.FLIPS

    def body(*refs):
        g5, p_in, dm_ref = refs[:nb], refs[nb:nb + n], refs[nb + n]
        outs = refs[nb + n + 1:2 * nb + 2 * n + 2]
        g_out, sums, dm2d = outs[:nb], outs[nb:nb + n], outs[nb + n]
        scratch = refs[2 * nb + 2 * n + 2:]
        s1, stage, chip_recv = scratch[:nb], scratch[nb:2 * nb], scratch[2 * nb:3 * nb]
        acc, rbuf = scratch[3 * nb:3 * nb + n], scratch[3 * nb + n:3 * nb + 2 * n]
        (dm_recv, send_sems, recv_sems, dm_send_sems, dm_recv_sems, sib_send, sib_recv, chip_send,
         chip_recv_sems) = scratch[3 * nb + 2 * n:]
        x, y, c = _coords()
        me = 4 * x + 2 * y + c
        sibling = (x, y, 1 - c)
        dm_copies = [
            pltpu.make_async_remote_copy(dm_ref.at[me ^ k], dm_recv.at[k], dm_send_sems.at[k], dm_recv_sems.at[k],
                                         device_id=_peer(k), device_id_type=MESH)
            for k in range(1, N_DEV)
        ]
        for cp in dm_copies:
            cp.start()
        sib_copies = _sibling_swap_copies(g5, s1, sib_send, sib_recv)
        for cp in sib_copies:
            cp.start()
        for a in range(n):
            acc[a][...] = p_in[a][...]

        def small_phase(ph, peers):
            copies = [
                pltpu.make_async_remote_copy(acc[a], rbuf[a].at[ph], send_sems.at[ph, a], recv_sems.at[ph, a],
                                             device_id=peers[y_first[a]], device_id_type=MESH)
                for a in range(n)
            ]
            for cp in copies:
                cp.start()
            for cp in copies:
                cp.wait()
            for a in range(n):
                acc[a][...] = acc[a][...] + rbuf[a][ph]

        small_phase(0, (sibling, sibling))
        for cp in sib_copies:
            cp.wait()
        for a in range(nb):
            for xs in range(2):
                for ys in range(2):
                    total = g5[a][xs, ys, c] + s1[a][xs, ys]
                    s1[a][xs, ys] = total
                    stage[a][xs, ys] = _bf(total)
        chip_copies = [
            pltpu.make_async_remote_copy(stage[a].at[x ^ fx, y ^ fy], chip_recv[a].at[j], chip_send.at[a, j],
                                         chip_recv_sems.at[a, j], device_id=(x ^ fx, y ^ fy, c), device_id_type=MESH)
            for a in range(nb) for j, (fx, fy) in enumerate(flips)
        ]
        for cp in chip_copies:
            cp.start()
        x_peer, y_peer = (1 - x, y, c), (x, 1 - y, c)
        small_phase(1, (x_peer, y_peer))
        small_phase(2, (y_peer, x_peer))
        for a in range(n):
            sums[a][...] = acc[a][me] if pick_mine[a] else acc[a][...]
        dm2d[...] = jnp.zeros(dm2d.shape, F32)
        dm2d[0:1, :] = dm_ref[me]
        for cp in dm_copies:
            cp.wait()
        for k in range(1, N_DEV):
            dm2d[k:k + 1, :] = dm_recv[k]
        for cp in chip_copies:
            cp.wait()
        for a in range(nb):
            g_out[a][...] = ((s1[a][x, y] + chip_recv[a][0].astype(F32)) + chip_recv[a][1].astype(F32)) \
                + chip_recv[a][2].astype(F32)

    out_shapes = tuple(jax.ShapeDtypeStruct(s, F32) for s in big_shapes) + tuple(
        jax.ShapeDtypeStruct(p.shape[1:] if pk else p.shape, F32) for p, pk in zip(partials, pick_mine))
    outs = pl.pallas_call(
        body,
        name="tail_exchange",
        out_shape=out_shapes + (jax.ShapeDtypeStruct((2 * N_DEV, MOD_COLS), F32),),
        in_specs=[VMEM_SPEC] * (nb + n + 1),
        out_specs=(VMEM_SPEC,) * (nb + n + 1),
        scratch_shapes=[pltpu.VMEM((2, 2, *s), F32) for s in big_shapes]
        + [pltpu.VMEM((2, 2, *s), BF16) for s in big_shapes]
        + [pltpu.VMEM((3, *s), BF16) for s in big_shapes]
        + [pltpu.VMEM(p.shape, F32) for p in partials]
        + [pltpu.VMEM((3, *p.shape), F32) for p in partials]
        + [pltpu.VMEM((N_DEV, 1, MOD_COLS), F32), pltpu.SemaphoreType.DMA((3, n)), pltpu.SemaphoreType.DMA((3, n)),
           pltpu.SemaphoreType.DMA((N_DEV,)), pltpu.SemaphoreType.DMA((N_DEV,)),
           pltpu.SemaphoreType.DMA((nb, 4)), pltpu.SemaphoreType.DMA((nb, 4)),
           pltpu.SemaphoreType.DMA((nb, 3)), pltpu.SemaphoreType.DMA((nb, 3))],
        compiler_params=pltpu.CompilerParams(vmem_limit_bytes=VMEM_LIMIT_V7X),
    )(*big5, *partials, dmod3)
    return list(outs[:nb]), list(outs[nb:nb + n]), outs[nb + n]


def _small_update(grads, ws, ms, vs, scx, dm2d, w_ada, m_ada, v_ada, loss_lanes):
    n = len(grads)

    def body(*refs):
        g_in, w_in, m_in, v_in = (refs[q * n:(q + 1) * n] for q in range(4))
        scx_ref, dm_ref, wa_ref, ma_ref, va_ref, ll_ref = refs[4 * n:4 * n + 6]
        outs = refs[4 * n + 6:]
        g_out, d_out, nm_out, nv_out = (outs[q * (n + 1):(q + 1) * (n + 1)] for q in range(4))
        loss_ref = outs[4 * (n + 1)]
        for a in range(n + 1):
            if a < n:
                g, w, m, v = g_in[a][...], w_in[a][...], m_in[a][...], v_in[a][...]
            else:
                g = _mm_tn(_bf(scx_ref[...]), _bf(dm_ref[...]))
                w, m, v = wa_ref[...], ma_ref[...], va_ref[...]
            g_out[a][...] = g
            delta, m, v = _adamw(w, g, m, v)
            d_out[a][...] = delta
            nm_out[a][...] = m
            nv_out[a][...] = v
        loss_ref[...] = jnp.sum(ll_ref[...], axis=1, keepdims=True) * (0.5 / D)

    w_shapes = tuple(jax.ShapeDtypeStruct(w.shape, F32) for w in list(ws) + [w_ada])
    outs = pl.pallas_call(
        body,
        name="small_update",
        out_shape=w_shapes * 4 + (jax.ShapeDtypeStruct((1, 1), F32),),
        in_specs=[VMEM_SPEC] * (4 * n + 6),
        out_specs=(VMEM_SPEC,) * (4 * (n + 1) + 1),
        compiler_params=pltpu.CompilerParams(vmem_limit_bytes=VMEM_LIMIT_V7X),
    )(*grads, *ws, *ms, *vs, scx, dm2d, w_ada, m_ada, v_ada, loss_lanes)
    return [tuple(outs[q * (n + 1) + k] for q in range(4)) for k in range(n + 1)], outs[4 * (n + 1)]


def kernel(x, c, w_ada, b_ada, pre_mix_g, post_mix_g, w_in, sgu_norm_g, w_spatial, b_spatial, w_pool, pool_scale, w_out, pre_ffn_g, post_ffn_g, w_up, conv_w, conv_b, w_down, loss_target, m_w_ada, m_b_ada, m_pre_mix_g, m_post_mix_g, m_w_in, m_sgu_norm_g, m_w_spatial, m_b_spatial, m_w_pool, m_pool_scale, m_w_out, m_pre_ffn_g, m_post_ffn_g, m_w_up, m_conv_w, m_conv_b, m_w_down, v_w_ada, v_b_ada, v_pre_mix_g, v_post_mix_g, v_w_in, v_sgu_norm_g, v_w_spatial, v_b_spatial, v_w_pool, v_pool_scale, v_w_out, v_pre_ffn_g, v_post_ffn_g, v_w_up, v_conv_w, v_conv_b, v_w_down):
    t_len = x.shape[1]
    ts = min(256, t_len)
    ts_mix = min(512, t_len)
    ts_w = min(1024, t_len)
    coords = jnp.stack([lax.axis_index("x"), lax.axis_index("y"), lax.axis_index("c")]).astype(jnp.int32)

    w_in_t, w_up_t = w_in[0].T, w_up[0].T
    mod3, scx, (g_in, g_out) = _prologue(c, w_ada[0], b_ada.reshape(N_DEV, 1, MOD_COLS), [w_in_t, w_out[0]],
                                         [BF16, BF16])
    mod = mod3.reshape(N_MOD, D)
    w_in_tb = g_in.reshape(IN_WIDTH, D)
    w_out_b = g_out.reshape(D, D)
    conv_b8 = conv_b.reshape(N_DEV, FF_CHUNK)
    b_sp_t = b_spatial[0].T

    x2d, tgt = x[0], loss_target[0]
    (x1, proj, mixed), (g_up, g_down, g_cw) = _mix_fwd(
        x2d, mod, pre_mix_g, post_mix_g, w_in_tb, sgu_norm_g[0], w_spatial[0], b_sp_t, w_pool[0], pool_scale, w_out_b,
        ts_mix, [w_up_t, w_down[0], conv_w[0]], [w_up.shape[1:], w_down.shape[1:], conv_w.shape[1:]], [BF16, BF16, F32])
    w_down_b = g_down.reshape(FF, D)
    up, f, dx2, loss_lanes = _ffn_fwd(x1, tgt, mod, pre_ffn_g, post_ffn_g, g_up, g_cw, conv_b8, w_down_b, ts)

    (dx1, dup, act, df, h2, dmod_f, d_pre_ffn, d_post_ffn, d_cb8, d_cw8) = _ffn_bwd(
        dx2, f, x1, up, mod, pre_ffn_g, post_ffn_g, g_up, g_cw, conv_b8, w_down_b, ts)
    gw_up = _wgrad_up(h2, dup, ts_w).reshape(2, 2, 2, FF_CHUNK, D)
    gw_down, r1_up = _wgrad_down(act, df, ts_w, gw_up)
    gw_down = gw_down.reshape(2, 2, 2, FF // N_DEV, D)
    s1_up, s1_up_b, r1_down = _sibling_add("up", [gw_up], [r1_up], coords, swap_srcs=[gw_down])
    s1_down, s1_down_b, _ = _sibling_add("down", [gw_down], r1_down, coords)
    s1_ffn, s1_ffn_b = s1_up + s1_down, s1_up_b + s1_down_b
    ((grad_x, gw_in, gw_out, dmod_m, d_pre_mix, d_post_mix, d_sgn, d_wsp, d_bsp_t, d_wpool, d_ps), r_ffn) = _mix_bwd(
        dx1, x2d, proj, mixed, mod, pre_mix_g, post_mix_g, w_in_tb, sgu_norm_g[0], w_spatial[0], b_sp_t,
        w_pool[0], pool_scale, w_out_b, ts_mix, s1_ffn_b)
    gw_in = gw_in.reshape(N_DEV, IN_WIDTH // N_DEV, D)
    gw_out = gw_out.reshape(N_DEV, D // N_DEV, D)

    big = _final_add_adamw(coords, s1_ffn, list(r_ffn), [w_up_t, w_down[0]], [m_w_up[0].T, m_w_down[0]],
                           [v_w_up[0].T, v_w_down[0]])
    r_up, r_down = tuple(a.T[None] for a in big[0]), tuple(a[None] for a in big[1])

    dmod = jnp.concatenate([dmod_m, dmod_f], axis=0)
    names = ["b_ada", "pre_mix_g", "post_mix_g", "sgu_norm_g", "w_spatial", "b_spatial", "w_pool", "pool_scale",
             "pre_ffn_g", "post_ffn_g", "conv_w", "conv_b"]
    partials = [dmod.reshape(1, N_MOD * D), d_pre_mix, d_post_mix, d_sgn, d_wsp, d_bsp_t.T, d_wpool, d_ps,
                d_pre_ffn, d_post_ffn, d_cw8, d_cb8.reshape(1, 2 * FF), loss_lanes]
    small_w = [b_ada, pre_mix_g, post_mix_g, sgu_norm_g[0], w_spatial[0], b_spatial[0], w_pool[0], pool_scale,
               pre_ffn_g, post_ffn_g, conv_w[0], conv_b]
    small_m = [m_b_ada, m_pre_mix_g, m_post_mix_g, m_sgu_norm_g[0], m_w_spatial[0], m_b_spatial[0], m_w_pool[0],
               m_pool_scale, m_pre_ffn_g, m_post_ffn_g, m_conv_w[0], m_conv_b]
    small_v = [v_b_ada, v_pre_mix_g, v_post_mix_g, v_sgu_norm_g[0], v_w_spatial[0], v_b_spatial[0], v_w_pool[0],
               v_pool_scale, v_pre_ffn_g, v_post_ffn_g, v_conv_w[0], v_conv_b]
    g_mix, sums, dm2d = _tail_exchange([gw_in, gw_out], partials, [nm == "conv_w" for nm in names] + [False],
                                       [int(nm == "w_pool") for nm in names] + [0],
                                       dmod.reshape(N_DEV, 1, MOD_COLS))
    small, loss11 = _small_update(
        sums[:-1] + g_mix, small_w + [w_in_t, w_out[0]], small_m + [m_w_in[0].T, m_w_out[0]],
        small_v + [v_w_in[0].T, v_w_out[0]], scx, dm2d, w_ada[0], m_w_ada[0], v_w_ada[0], sums[-1])
    loss = loss11.reshape(())
    lead = {"sgu_norm_g", "w_spatial", "b_spatial", "w_pool", "conv_w", "w_in", "w_out", "w_ada"}
    res = {nm: tuple((a.T if nm == "w_in" else a)[None] if nm in lead else a for a in four)
           for nm, four in zip(names + ["w_in", "w_out", "w_ada"], small)}
    res.update(w_up=r_up, w_down=r_down)

    order = ["w_ada", "b_ada", "pre_mix_g", "post_mix_g", "w_in", "sgu_norm_g", "w_spatial", "b_spatial", "w_pool",
             "pool_scale", "w_out", "pre_ffn_g", "post_ffn_g", "w_up", "conv_w", "conv_b", "w_down"]
    return (loss, grad_x[None], *[res[nm][0] for nm in order], *[res[nm][1] for nm in order],
            *[res[nm][2] for nm in order], *[res[nm][3] for nm in order])
```

```python
import functools
import math

import jax
import jax.numpy as jnp
from jax import lax
from jax.experimental import pallas as pl
from jax.experimental.pallas import tpu as pltpu

F32 = jnp.float32
BF16 = jnp.bfloat16
MESH = pl.DeviceIdType.MESH

EPS = 1e-6
D = 1024
HEAD = 128
N_HEAD = 4
A_WIDTH = 512
B_WIDTH = 512
IN_WIDTH = 1536
WINDOWS = (2, 4, 8, 16)
CHUNK = 64
FF = 2816
N_DEV = 8
FF_CHUNK = 704
N_MOD = 6
MOD_COLS = 768

ADAM_LR = 0.001
ADAM_B1 = 0.9
ADAM_B2 = 0.999
ADAM_EPS = 1e-08
ADAM_WD = 0.01
ADAM_STEP = 10

VMEM_LIMIT_V7X = 62 * 1024 * 1024
HALO = 8
POOL_HALO = 16

VMEM_SPEC = pl.BlockSpec(memory_space=pltpu.VMEM)
ANY_SPEC = pl.BlockSpec(memory_space=pl.ANY)


def _bf(x):
    return x.astype(BF16)


def _mm(a, b):
    return jnp.dot(a, b, preferred_element_type=F32)


def _mm_nt(a, b):
    return lax.dot_general(a, b, (((1,), (1,)), ((), ())), preferred_element_type=F32)


def _mm_tn(a, b):
    return lax.dot_general(a, b, (((0,), (0,)), ((), ())), preferred_element_type=F32)


def _rstd(x):
    return lax.rsqrt(jnp.mean(x * x, axis=-1, keepdims=True) + EPS)


def _sum0(x):
    return jnp.sum(x, axis=0, keepdims=True)


def _rowmean(x):
    return jnp.mean(x, axis=-1, keepdims=True)


_GELU_K = math.sqrt(2.0 / math.pi)


def _gelu_and_grad(x):
    x2 = x * x
    th = jnp.tanh(_GELU_K * (x + 0.044715 * (x * x2)))
    cdf = 0.5 * th + 0.5
    grad = cdf + x * (1.0 - th * th) * ((0.5 * _GELU_K) + (1.5 * 0.044715 * _GELU_K) * x2)
    return x * cdf, grad


def _gelu(x):
    return x * (0.5 * (1.0 + jnp.tanh(_GELU_K * (x + 0.044715 * (x * x * x)))))


def _sigmoid(x):
    return 0.5 * jnp.tanh(0.5 * x) + 0.5


def _sgu_mask():
    ri = lax.broadcasted_iota(jnp.int32, (HEAD, HEAD), 0)
    ci = lax.broadcasted_iota(jnp.int32, (HEAD, HEAD), 1)
    return (ci // CHUNK) <= (ri // CHUNK)


def _window_sum(ext, w, trailing):
    n = ext.shape[0]
    s, k = ext, 1
    while k < w:
        s = s + pltpu.roll(s, k if trailing else n - k, 0)
        k *= 2
    return s


def _inv_count(row0, n, w):
    t = row0 + lax.broadcasted_iota(jnp.int32, (n, 1), 0)
    return 1.0 / jnp.minimum(t + 1, w).astype(F32)


def _shift_down(v, before, k):
    rows = lax.broadcasted_iota(jnp.int32, before.shape, 0)
    r = pltpu.roll(v, k, 0)
    top = jnp.where(rows < k, pltpu.roll(before, k, 0), r[0:HALO])
    return jnp.concatenate([top, r[HALO:]], axis=0)


def _shift_up(v, after, k):
    n = v.shape[0]
    rows = lax.broadcasted_iota(jnp.int32, after.shape, 0)
    r = pltpu.roll(v, n - k, 0)
    bottom = jnp.where(rows >= HALO - k, pltpu.roll(after, HALO - k, 0), r[n - HALO:])
    return jnp.concatenate([r[:n - HALO], bottom], axis=0)


def _adamw(w, g, m, v):
    m = ADAM_B1 * m + (1.0 - ADAM_B1) * g
    v = ADAM_B2 * v + (1.0 - ADAM_B2) * (g * g)
    m_hat = m / (1.0 - ADAM_B1 ** ADAM_STEP)
    v_hat = v / (1.0 - ADAM_B2 ** ADAM_STEP)
    delta = -ADAM_LR * (m_hat / (jnp.sqrt(v_hat) + ADAM_EPS) + ADAM_WD * w)
    return delta, m, v


def _coords():
    return lax.axis_index("x"), lax.axis_index("y"), lax.axis_index("c")


def _peer(k):
    x, y, c = _coords()
    return (x ^ ((k >> 2) & 1), y ^ ((k >> 1) & 1), c ^ (k & 1))


def _my_index():
    x, y, c = _coords()
    return 4 * x + 2 * y + c


def _adaln_modulation(c_ref, w_ref, b_ref, mod_ref, scx_ref, scbuf, stage, recv, send_sems, recv_sems):
    me = _my_index()
    cv = c_ref[...]
    scbuf[0] = cv * _sigmoid(cv)
    first = [
        pltpu.make_async_remote_copy(scbuf.at[0], scbuf.at[k], send_sems.at[0, k], recv_sems.at[0, k],
                                     device_id=_peer(k), device_id_type=MESH)
        for k in range(1, N_DEV)
    ]
    for cp in first:
        cp.start()
    for cp in first:
        cp.wait()
    scx_ref[...] = jnp.zeros(scx_ref.shape, F32)
    for k in range(N_DEV):
        scx_ref[k:k + 1, :] = scbuf[k]
    prod = _mm(_bf(scx_ref[...]), _bf(w_ref[...]))
    for k in range(N_DEV):
        stage[k] = prod[k:k + 1, :] + b_ref[me]
    second = [
        pltpu.make_async_remote_copy(stage.at[k], recv.at[k], send_sems.at[1, k], recv_sems.at[1, k],
                                     device_id=_peer(k), device_id_type=MESH)
        for k in range(1, N_DEV)
    ]
    for cp in second:
        cp.start()
    mod_ref[me] = stage[0]
    for cp in second:
        cp.wait()
    for k in range(1, N_DEV):
        mod_ref[me ^ k] = recv[k]


class _GatherSteps:
    def __init__(self, ins, outs, stages, send_sems, recv_sems, local_sems):
        self.ins, self.outs, self.stages = ins, outs, stages
        self.send_sems, self.recv_sems, self.local_sems = send_sems, recv_sems, local_sems
        x, y, c = _coords()
        self.c = c
        self.me, self.sibling = (x, y, c), (x, y, 1 - c)
        self.chips = [(1 - x, y), (x, 1 - y), (1 - x, 1 - y)]

    def _copy(self, a, k, block, to, from_stage=False):
        dst = self.outs[a].at[4 * block[0] + 2 * block[1] + block[2]]
        return pltpu.make_async_remote_copy(self.stages[a] if from_stage else dst, dst, self.send_sems.at[a, k],
                                            self.recv_sems.at[a, k], device_id=to, device_id_type=MESH)

    def _local(self, a):
        me = self.me
        return pltpu.make_async_copy(self.stages[a], self.outs[a].at[4 * me[0] + 2 * me[1] + me[2]],
                                     self.local_sems.at[a])

    def _first(self, a):
        cps = [self._copy(a, 0, self.me, self.sibling, from_stage=True)]
        return cps + [self._copy(a, 1 + j, self.me, (*chip, self.c), from_stage=True)
                      for j, chip in enumerate(self.chips)]

    def _passed(self, a, j):
        return self._copy(a, 4 + j, (*self.chips[j], self.c), self.sibling)

    def start(self):
        for a in range(len(self.ins)):
            block = self.ins[a][...]
            if block.shape != self.stages[a].shape:
                block = block.T
            self.stages[a][...] = block.astype(self.stages[a].dtype)
            self._local(a).start()
            for cp in self._first(a):
                cp.start()

    def forward(self, arrays=None):
        for a in range(len(self.ins)) if arrays is None else arrays:
            for j, chip in enumerate(self.chips):
                self._copy(a, 1 + j, (*chip, self.c), self.me).wait_recv()
                self._passed(a, j).start()

    def finish(self):
        for a in range(len(self.ins)):
            self._copy(a, 0, self.sibling, self.me).wait_recv()
            for j, chip in enumerate(self.chips):
                self._copy(a, 4 + j, (*chip, 1 - self.c), self.me).wait_recv()
            for cp in self._first(a) + [self._passed(a, j) for j in range(3)]:
                cp.wait_send()
            self._local(a).wait()


def _gather_scratch(shapes, out_dtypes):
    n = len(shapes)
    return ([pltpu.VMEM(s, dt) for s, dt in zip(shapes, out_dtypes)]
            + [pltpu.SemaphoreType.DMA((n, 7)), pltpu.SemaphoreType.DMA((n, 7)), pltpu.SemaphoreType.DMA((n,))])


def _gather_out_shapes(shapes, out_dtypes):
    return tuple(jax.ShapeDtypeStruct((N_DEV, *s), dt) for s, dt in zip(shapes, out_dtypes))


def _prologue(c_row, w_ada, b_ada3, shards, out_dtypes):
    n = len(shards)

    def body(*refs):
        c_ref, w_ref, b_ref = refs[:3]
        mod_ref, scx_ref = refs[3 + n:5 + n]
        gather = _GatherSteps(refs[3:3 + n], refs[5 + n:5 + 2 * n], refs[5 + 2 * n:5 + 3 * n],
                              *refs[5 + 3 * n:8 + 3 * n])
        gather.start()
        _adaln_modulation(c_ref, w_ref, b_ref, mod_ref, scx_ref, *refs[8 + 3 * n:])
        gather.forward()
        gather.finish()

    outs = pl.pallas_call(
        body,
        name="prologue",
        out_shape=(jax.ShapeDtypeStruct((N_DEV, 1, MOD_COLS), F32), jax.ShapeDtypeStruct((2 * N_DEV, D), F32))
        + _gather_out_shapes([s.shape for s in shards], out_dtypes),
        in_specs=[VMEM_SPEC] * (3 + n),
        out_specs=(VMEM_SPEC, VMEM_SPEC) + (ANY_SPEC,) * n,
        scratch_shapes=_gather_scratch([s.shape for s in shards], out_dtypes) + [
            pltpu.VMEM((N_DEV, 1, D), F32),
            pltpu.VMEM((N_DEV, 1, MOD_COLS), F32),
            pltpu.VMEM((N_DEV, 1, MOD_COLS), F32),
            pltpu.SemaphoreType.DMA((2, N_DEV)),
            pltpu.SemaphoreType.DMA((2, N_DEV)),
        ],
        compiler_params=pltpu.CompilerParams(vmem_limit_bytes=VMEM_LIMIT_V7X),
    )(c_row, w_ada, b_ada3, *shards)
    return outs[0], outs[1], outs[2:]


class _ChipExchangeSteps:
    FLIPS = ((1, 0), (0, 1), (1, 1))

    def __init__(self, srcs, dsts, send_sems, recv_sems):
        self.srcs, self.dsts, self.send_sems, self.recv_sems = srcs, dsts, send_sems, recv_sems

    def _copies(self):
        x, y, c = _coords()
        out = []
        for a in range(len(self.srcs)):
            for j, (fx, fy) in enumerate(self.FLIPS):
                k = 3 * a + j
                out.append(pltpu.make_async_remote_copy(
                    self.srcs[a].at[x ^ fx, y ^ fy], self.dsts[a].at[j], self.send_sems.at[k], self.recv_sems.at[k],
                    device_id=(x ^ fx, y ^ fy, c), device_id_type=MESH))
        return out

    def start(self):
        for cp in self._copies():
            cp.start()

    def finish(self):
        for cp in self._copies():
            cp.wait()


def _mix_fwd(x, mod, g_pre, g_post, w_in_t, sgn, w_sp, b_sp_t, w_pool, p_scale, w_out_b, ts, shards, shard_shapes,
             shard_dtypes):
    t_len = x.shape[0]
    nt, nb = t_len // ts, ts // HEAD
    ns = len(shards)

    def body(*refs):
        (x_ref, mod_ref, g1_ref, g2_ref, win_ref, sgn_ref, ws_ref, bst_ref, wp_ref, ps_ref, wout_ref) = refs[:11]
        x1_ref, proj_ref, mixed_ref = refs[11 + ns:14 + ns]
        pbuf, cat = refs[14 + 2 * ns:16 + 2 * ns]
        gather = _GatherSteps(refs[11:11 + ns], refs[14 + ns:14 + 2 * ns], refs[16 + 2 * ns:16 + 3 * ns],
                              *refs[16 + 3 * ns:])
        i = pl.program_id(0)

        @pl.when(i == 0)
        def _():
            pbuf[0:POOL_HALO, :] = jnp.zeros((POOL_HALO, B_WIDTH), F32)
            gather.start()

        @pl.when(i == (3 * nt) // 4)
        def _():
            gather.forward([0])

        @pl.when(i == nt - 1)
        def _():
            gather.forward(range(1, ns))

        xv = x_ref[...]
        sh, sc, gm = mod_ref[0:1, :], mod_ref[1:2, :], mod_ref[2:3, :]
        h = (xv * _rstd(xv) * g1_ref[...]) * (1.0 + sc) + sh
        proj_ref[...] = _mm_nt(_bf(h), win_ref[...])
        pbuf[POOL_HALO:POOL_HALO + ts, :] = proj_ref[:, 2 * A_WIDTH:]
        smask = _sgu_mask()
        for hd in range(N_HEAD):
            u = _gelu(proj_ref[:, hd * HEAD:(hd + 1) * HEAD])
            v = _gelu(proj_ref[:, A_WIDTH + hd * HEAD:A_WIDTH + (hd + 1) * HEAD])
            vn = _bf(v * _rstd(v) * sgn_ref[hd:hd + 1, :])
            wm = _bf(jnp.where(smask, ws_ref[hd], 0.0))
            bias = bst_ref[:, hd:hd + 1]
            for b in range(nb):
                rows = slice(b * HEAD, (b + 1) * HEAD)
                z = _mm(wm, vn[rows]) + bias
                cat[rows, hd * HEAD:(hd + 1) * HEAD] = _bf(u[rows] * z)
        for g, w in enumerate(WINDOWS):
            cols = slice(g * HEAD, (g + 1) * HEAD)
            ext = pbuf[:, cols]
            pooled = _window_sum(ext, w, True)[POOL_HALO:] * _inv_count(i * ts, ts, w) - ext[POOL_HALO:]
            cat[:, A_WIDTH + g * HEAD:A_WIDTH + (g + 1) * HEAD] = _bf(_mm(_bf(pooled), _bf(wp_ref[g])) * ps_ref[:, cols])
        pbuf[0:POOL_HALO, :] = pbuf[ts:ts + POOL_HALO, :]
        mixed = _mm(cat[...], wout_ref[...])
        mixed_ref[...] = mixed
        x1_ref[...] = xv + gm * (mixed * _rstd(mixed) * g2_ref[...])

        @pl.when(i == nt - 1)
        def _():
            gather.finish()

    tile = lambda wid: pl.BlockSpec((ts, wid), lambda i: (i, 0))
    outs = pl.pallas_call(
        body,
        name="mix_fwd",
        grid=(nt,),
        out_shape=(jax.ShapeDtypeStruct((t_len, D), F32), jax.ShapeDtypeStruct((t_len, IN_WIDTH), F32),
                   jax.ShapeDtypeStruct((t_len, D), F32)) + _gather_out_shapes(shard_shapes, shard_dtypes),
        in_specs=[tile(D)] + [VMEM_SPEC] * (10 + ns),
        out_specs=(tile(D), tile(IN_WIDTH), tile(D)) + (ANY_SPEC,) * ns,
        scratch_shapes=[pltpu.VMEM((POOL_HALO + ts, B_WIDTH), F32), pltpu.VMEM((ts, D), BF16)]
        + _gather_scratch(shard_shapes, shard_dtypes),
        compiler_params=pltpu.CompilerParams(dimension_semantics=("arbitrary",), vmem_limit_bytes=VMEM_LIMIT_V7X),
    )(x, mod, g_pre, g_post, w_in_t, sgn, w_sp, b_sp_t, w_pool, p_scale, w_out_b, *shards)
    return outs[:3], outs[3:]


def _ffn_fwd(x1, target, mod, g_pre, g_post, w_up_b, conv_w8, conv_b8, w_down_b, ts):
    t_len = x1.shape[0]
    nt = t_len // ts

    def body(x1_ref, tgt_ref, mod_ref, g3_ref, g4_ref, wup_ref, cw_ref, cb_ref, wdown_ref,
             up_ref, act_ref, f_ref, dx2_ref, loss_ref, ucarry):
        i = pl.program_id(0)

        @pl.when(i == 0)
        def _():
            ucarry[...] = jnp.zeros(ucarry.shape, F32)
            loss_ref[...] = jnp.zeros(loss_ref.shape, F32)

        x1v = x1_ref[...]
        sh, sc, gf = mod_ref[3:4, :], mod_ref[4:5, :], mod_ref[5:6, :]
        h2 = _bf((x1v * _rstd(x1v) * g3_ref[...]) * (1.0 + sc) + sh)
        half = N_DEV // 2

        def up_pair(j):
            return [_mm(h2, wup_ref[jj]) for jj in (j, j + half)]

        f = jnp.zeros((ts, D), F32)
        ups = up_pair(0)
        for j in range(half):
            nxt = up_pair(j + 1) if j + 1 < half else None
            ys = []
            for up, jj in zip(ups, (j, j + half)):
                up_ref[jj] = up
                before = ucarry[jj]
                ucarry[jj] = up[ts - HALO:, :]
                cw = cw_ref[jj]
                ys.append(cb_ref[jj:jj + 1, :] + _shift_down(up, before, 2) * cw[0:1, :]
                          + _shift_down(up, before, 1) * cw[1:2, :] + up * cw[2:3, :])
            gate, val = ys
            act = _bf(gate * _sigmoid(gate) * val)
            act_ref[j] = act
            f = f + _mm(act, wdown_ref[j * FF_CHUNK:(j + 1) * FF_CHUNK, :])
            ups = nxt
        f_ref[...] = f
        x2 = x1v + gf * (f * _rstd(f) * g4_ref[...])
        err = x2 - tgt_ref[...]
        loss_ref[...] += _sum0(err * err)
        dx2_ref[...] = err * (1.0 / D)

    tile = pl.BlockSpec((ts, D), lambda i: (i, 0))
    return pl.pallas_call(
        body,
        name="ffn_fwd",
        grid=(nt,),
        out_shape=(jax.ShapeDtypeStruct((N_DEV, t_len, FF_CHUNK), F32),
                   jax.ShapeDtypeStruct((N_DEV // 2, t_len, FF_CHUNK), BF16), jax.ShapeDtypeStruct((t_len, D), F32),
                   jax.ShapeDtypeStruct((t_len, D), F32), jax.ShapeDtypeStruct((1, D), F32)),
        in_specs=[tile, tile] + [VMEM_SPEC] * 7,
        out_specs=(pl.BlockSpec((N_DEV, ts, FF_CHUNK), lambda i: (0, i, 0)),
                   pl.BlockSpec((N_DEV // 2, ts, FF_CHUNK), lambda i: (0, i, 0)), tile, tile,
                   pl.BlockSpec((1, D), lambda i: (0, 0))),
        scratch_shapes=[pltpu.VMEM((N_DEV, HALO, FF_CHUNK), F32)],
        compiler_params=pltpu.CompilerParams(dimension_semantics=("arbitrary",), vmem_limit_bytes=VMEM_LIMIT_V7X),
    )(x1, target, mod, g_pre, g_post, w_up_b, conv_w8, conv_b8, w_down_b)


def _ffn_bwd(dx2, f, x1, up, mod, g_pre, g_post, w_up_b, conv_w8, conv_b8, w_down_b, ts):
    t_len = x1.shape[0]
    nt = t_len // ts
    half = N_DEV // 2

    def body(dx2_ref, f_ref, x1_ref, up_ref, halo_ref, mod_ref, g3_ref, g4_ref, wup_ref, cw_ref, cb_ref, wdown_ref,
             dx1_ref, dup_ref, df_ref, h2_ref, dmod_ref, dg3_ref, dg4_ref, dcb_ref, dcw_ref,
             dycarry, dh2acc):
        i = pl.program_id(0)
        r = nt - 1 - i

        @pl.when(i == 0)
        def _():
            for ref in (dmod_ref, dg3_ref, dg4_ref, dcb_ref, dcw_ref, dycarry):
                ref[...] = jnp.zeros(ref.shape, F32)

        dx2v, fv, x1v = dx2_ref[...], f_ref[...], x1_ref[...]
        sh, sc, gf = mod_ref[3:4, :], mod_ref[4:5, :], mod_ref[5:6, :]
        g3, g4 = g3_ref[...], g4_ref[...]
        rstd4 = _rstd(fv)
        fh = fv * rstd4
        dmod_ref[2:3, :] += _sum0(dx2v * (fh * g4))
        dr = dx2v * gf
        dg4_ref[...] += _sum0(dr * fh)
        dfh = dr * g4
        dfb = _bf(rstd4 * (dfh - fh * _rowmean(dfh * fh)))
        df_ref[...] = dfb
        rstd3 = _rstd(x1v)
        xh = x1v * rstd3
        n3 = xh * g3
        h2_ref[...] = _bf(n3 * (1.0 + sc) + sh)
        dh2acc[...] = jnp.zeros((ts, D), F32)
        keep = jnp.where(r > 0, 1.0, 0.0).astype(F32)

        def dact_of(j):
            return _mm_nt(dfb, wdown_ref[j * FF_CHUNK:(j + 1) * FF_CHUNK, :])

        dact_next = dact_of(0)
        for j in range(half):
            dact = dact_next
            if j + 1 < half:
                dact_next = dact_of(j + 1)
            ys = []
            for jj in (j, j + half):
                before = halo_ref[jj] * keep
                upc = up_ref[jj]
                cw = cw_ref[jj]
                ys.append(cb_ref[jj:jj + 1, :] + _shift_down(upc, before, 2) * cw[0:1, :]
                          + _shift_down(upc, before, 1) * cw[1:2, :] + upc * cw[2:3, :])
            gate, val = ys
            sg = _sigmoid(gate)
            silu = gate * sg
            dys = (dact * val * (sg + silu * (1.0 - sg)), dact * silu)
            for q, jj in enumerate((j, j + half)):
                dy = dys[q]
                cw = cw_ref[jj]
                dcb_ref[jj:jj + 1, :] += _sum0(dy)
                after = dycarry[jj]
                dycarry[jj] = dy[0:HALO, :]
                dy1, dy2 = _shift_up(dy, after, 1), _shift_up(dy, after, 2)
                upc = up_ref[jj]
                dcw_ref[jj, 0:1, :] += _sum0(dy2 * upc)
                dcw_ref[jj, 1:2, :] += _sum0(dy1 * upc)
                dcw_ref[jj, 2:3, :] += _sum0(dy * upc)
                dup = _bf(dy * cw[2:3, :] + dy1 * cw[1:2, :] + dy2 * cw[0:1, :])
                dup_ref[jj] = dup
                dh2acc[...] += _mm_nt(dup, wup_ref[jj])
        dh2 = dh2acc[...]
        dmod_ref[0:1, :] += _sum0(dh2)
        dmod_ref[1:2, :] += _sum0(dh2 * n3)
        dn3 = dh2 * (1.0 + sc)
        dg3_ref[...] += _sum0(dn3 * xh)
        dxh = dn3 * g3
        dx1_ref[...] = dx2v + rstd3 * (dxh - xh * _rowmean(dxh * xh))

    tile = pl.BlockSpec((ts, D), lambda i: (nt - 1 - i, 0))
    chunked = lambda n: pl.BlockSpec((n, ts, FF_CHUNK), lambda i: (0, nt - 1 - i, 0))
    halo = pl.BlockSpec((N_DEV, HALO, FF_CHUNK), lambda i: (0, jnp.maximum((nt - 1 - i) * (ts // HALO) - 1, 0), 0))
    const = lambda *shape: pl.BlockSpec(shape, lambda i: (0,) * len(shape))
    return pl.pallas_call(
        body,
        name="ffn_bwd",
        grid=(nt,),
        out_shape=(jax.ShapeDtypeStruct((t_len, D), F32), jax.ShapeDtypeStruct((N_DEV, t_len, FF_CHUNK), BF16),
                   jax.ShapeDtypeStruct((t_len, D), BF16),
                   jax.ShapeDtypeStruct((t_len, D), BF16), jax.ShapeDtypeStruct((3, D), F32),
                   jax.ShapeDtypeStruct((1, D), F32), jax.ShapeDtypeStruct((1, D), F32),
                   jax.ShapeDtypeStruct((N_DEV, FF_CHUNK), F32), jax.ShapeDtypeStruct((N_DEV, 3, FF_CHUNK), F32)),
        in_specs=[tile, tile, tile, chunked(N_DEV), halo] + [VMEM_SPEC] * 7,
        out_specs=(tile, chunked(N_DEV), tile, tile, const(3, D), const(1, D), const(1, D),
                   const(N_DEV, FF_CHUNK), const(N_DEV, 3, FF_CHUNK)),
        scratch_shapes=[pltpu.VMEM((N_DEV, HALO, FF_CHUNK), F32), pltpu.VMEM((ts, D), F32)],
        compiler_params=pltpu.CompilerParams(dimension_semantics=("arbitrary",), vmem_limit_bytes=VMEM_LIMIT_V7X),
    )(dx2, f, x1, up, up, mod, g_pre, g_post, w_up_b, conv_w8, conv_b8, w_down_b)


def _wgrad_up(h2, dup, ts):
    t_len = h2.shape[0]
    nt, half = t_len // ts, N_DEV // 2

    def body(h2_ref, dup_ref, out_ref):
        @pl.when(pl.program_id(1) == 0)
        def _():
            out_ref[...] = jnp.zeros(out_ref.shape, F32)

        for q in range(half):
            out_ref[q] += _mm_tn(dup_ref[q], h2_ref[...])

    return pl.pallas_call(
        body,
        name="wgrad_up",
        grid=(2, nt),
        out_shape=jax.ShapeDtypeStruct((N_DEV, FF_CHUNK, D), F32),
        in_specs=[pl.BlockSpec((ts, D), lambda g, t: (t, 0)), pl.BlockSpec((half, ts, FF_CHUNK), lambda g, t: (g, t, 0))],
        out_specs=pl.BlockSpec((half, FF_CHUNK, D), lambda g, t: (g, 0, 0)),
        compiler_params=pltpu.CompilerParams(dimension_semantics=("arbitrary", "arbitrary"),
                                             vmem_limit_bytes=VMEM_LIMIT_V7X),
    )(h2, dup)


def _sibling_swap_copies(srcs, dsts, send_sems, recv_sems):
    x, y, c = _coords()
    return [
        pltpu.make_async_remote_copy(srcs[a].at[xs, ys, 1 - c], dsts[a].at[xs, ys], send_sems.at[a, 2 * xs + ys],
                                     recv_sems.at[a, 2 * xs + ys], device_id=(x, y, 1 - c), device_id_type=MESH)
        for a in range(len(srcs)) for xs in range(2) for ys in range(2)
    ]


def _wgrad_down(act, df, ts, swap_src):
    t_len = df.shape[0]
    nt, half = t_len // ts, N_DEV // 2

    def body(act_ref, df_ref, src_ref, out_ref, dst_ref, send_sems, recv_sems):
        t = pl.program_id(0)

        @pl.when(t == 0)
        def _():
            for cp in _sibling_swap_copies([src_ref], [dst_ref], send_sems, recv_sems):
                cp.start()
            out_ref[...] = jnp.zeros(out_ref.shape, F32)

        for q in range(half):
            out_ref[q] += _mm_tn(act_ref[q], df_ref[...])

        @pl.when(t == nt - 1)
        def _():
            for cp in _sibling_swap_copies([src_ref], [dst_ref], send_sems, recv_sems):
                cp.wait()

    return pl.pallas_call(
        body,
        name="wgrad_down",
        grid=(nt,),
        out_shape=(jax.ShapeDtypeStruct((half, FF_CHUNK, D), F32), jax.ShapeDtypeStruct(swap_src.shape[1:], F32)),
        in_specs=[pl.BlockSpec((half, ts, FF_CHUNK), lambda t: (0, t, 0)), pl.BlockSpec((ts, D), lambda t: (t, 0)),
                  ANY_SPEC],
        out_specs=(pl.BlockSpec((half, FF_CHUNK, D), lambda t: (0, 0, 0)), ANY_SPEC),
        scratch_shapes=[pltpu.SemaphoreType.DMA((1, 4)), pltpu.SemaphoreType.DMA((1, 4))],
        compiler_params=pltpu.CompilerParams(dimension_semantics=("arbitrary",), vmem_limit_bytes=VMEM_LIMIT_V7X),
    )(act, df, swap_src)


def _mix_bwd(dx1, x, proj, mixed, mod, g_pre, g_post, w_in_t, sgn, w_sp, b_sp_t, w_pool, p_scale, w_out_b, ts, rs_srcs):
    t_len = x.shape[0]
    nt, nb = t_len // ts, ts // HEAD
    nr = len(rs_srcs)

    def body(*refs):
        (dx1_ref, x_ref, proj_ref, halo_ref, mixed_ref, mod_ref, g1_ref, g2_ref, win_ref, sgn_ref, ws_ref,
         bst_ref, wp_ref, ps_ref, wout_ref) = refs[:15]
        (gx_ref, dwin_ref, dwout_ref, dmod_ref, dg1_ref, dg2_ref, dsgn_ref, dws_ref, dbst_ref, dwp_ref,
         dps_ref) = refs[15 + nr:26 + nr]
        pbuf, dwsbuf, cat, dproj, dcat = refs[26 + 2 * nr:31 + 2 * nr]
        exchange = _ChipExchangeSteps(refs[15:15 + nr], refs[26 + nr:26 + 2 * nr], *refs[31 + 2 * nr:])
        i = pl.program_id(0)
        r = nt - 1 - i

        @pl.when(i == 0)
        def _():
            exchange.start()
            for ref in (dwin_ref, dwout_ref, dmod_ref, dg1_ref, dg2_ref, dsgn_ref, dws_ref, dbst_ref, dwp_ref, dps_ref):
                ref[...] = jnp.zeros(ref.shape, F32)
            dwsbuf[ts:ts + POOL_HALO, :] = jnp.zeros((POOL_HALO, B_WIDTH), F32)

        xv, dx1v, mixed = x_ref[...], dx1_ref[...], mixed_ref[...]
        sh, sc, gm = mod_ref[0:1, :], mod_ref[1:2, :], mod_ref[2:3, :]
        g1, g2 = g1_ref[...], g2_ref[...]
        rstd2 = _rstd(mixed)
        mh = mixed * rstd2
        dmod_ref[2:3, :] += _sum0(dx1v * (mh * g2))
        dr = dx1v * gm
        dg2_ref[...] += _sum0(dr * mh)
        dmh = dr * g2
        dmb = _bf(rstd2 * (dmh - mh * _rowmean(dmh * mh)))
        dcat[...] = _mm_nt(dmb, wout_ref[...])
        smask = _sgu_mask()
        for hd in range(N_HEAD):
            ucols = slice(hd * HEAD, (hd + 1) * HEAD)
            vcols = slice(A_WIDTH + hd * HEAD, A_WIDTH + (hd + 1) * HEAD)
            u, du_dp = _gelu_and_grad(proj_ref[:, ucols])
            v, dv_dp = _gelu_and_grad(proj_ref[:, vcols])
            rs = _rstd(v)
            vhat = v * rs
            gn = sgn_ref[hd:hd + 1, :]
            vn = _bf(vhat * gn)
            wm = _bf(jnp.where(smask, ws_ref[hd], 0.0))
            bias = bst_ref[:, hd:hd + 1]
            dzsum = jnp.zeros((HEAD, HEAD), F32)
            dwm = jnp.zeros((HEAD, HEAD), F32)
            dvn_parts = []
            for b in range(nb):
                rows = slice(b * HEAD, (b + 1) * HEAD)
                z = _mm(wm, vn[rows]) + bias
                da = dcat[rows, ucols]
                cat[rows, ucols] = _bf(u[rows] * z)
                dz = da * u[rows]
                dzsum = dzsum + dz
                dzb = _bf(dz)
                dwm = dwm + _mm_nt(dzb, vn[rows])
                dvn_parts.append(_mm_tn(wm, dzb))
                dproj[rows, ucols] = _bf((da * z) * du_dp[rows])
            dvn = jnp.concatenate(dvn_parts, axis=0)
            dsgn_ref[hd:hd + 1, :] += _sum0(dvn * vhat)
            dvh = dvn * gn
            dproj[:, vcols] = _bf((rs * (dvh - vhat * _rowmean(dvh * vhat))) * dv_dp)
            dws_ref[hd] += jnp.where(smask, dwm, 0.0)
            dbst_ref[:, hd:hd + 1] += jnp.sum(dzsum, axis=1, keepdims=True)
        keep = jnp.where(r > 0, 1.0, 0.0).astype(F32)
        pbuf[0:POOL_HALO, :] = halo_ref[...] * keep
        pbuf[POOL_HALO:POOL_HALO + ts, :] = proj_ref[:, 2 * A_WIDTH:]
        for g, w in enumerate(WINDOWS):
            cols = slice(g * HEAD, (g + 1) * HEAD)
            ccols = slice(A_WIDTH + g * HEAD, A_WIDTH + (g + 1) * HEAD)
            pcols = slice(2 * A_WIDTH + g * HEAD, 2 * A_WIDTH + (g + 1) * HEAD)
            wpg = _bf(wp_ref[g])
            psg = ps_ref[:, cols]
            ext = pbuf[:, cols]
            inv = _inv_count(r * ts, ts, w)
            pb = _bf(_window_sum(ext, w, True)[POOL_HALO:] * inv - ext[POOL_HALO:])
            yb = _mm(pb, wpg)
            dob = dcat[:, ccols]
            cat[:, ccols] = _bf(yb * psg)
            dps_ref[:, cols] += _sum0(dob * yb)
            dyb = _bf(dob * psg)
            dwp_ref[g] += _mm_tn(pb, dyb)
            dpooled = _mm_nt(dyb, wpg)
            dwsbuf[0:ts, cols] = dpooled * inv
            dproj[:, pcols] = _bf(_window_sum(dwsbuf[:, cols], w, False)[0:ts] - dpooled)
        dwsbuf[ts:ts + POOL_HALO, :] = dwsbuf[0:POOL_HALO, :]
        dpb = dproj[...]
        rstd1 = _rstd(xv)
        xh = xv * rstd1
        n1 = xh * g1
        dwin_ref[...] += _mm_tn(dpb, _bf(n1 * (1.0 + sc) + sh))
        dwout_ref[...] += _mm_tn(cat[...], dmb)
        dh = _mm(dpb, win_ref[...])
        dmod_ref[0:1, :] += _sum0(dh)
        dmod_ref[1:2, :] += _sum0(dh * n1)
        dn1 = dh * (1.0 + sc)
        dg1_ref[...] += _sum0(dn1 * xh)
        dxh = dn1 * g1
        gx_ref[...] = dx1v + rstd1 * (dxh - xh * _rowmean(dxh * xh))

        @pl.when(i == nt - 1)
        def _():
            exchange.finish()

    tile = lambda wid: pl.BlockSpec((ts, wid), lambda i: (nt - 1 - i, 0))
    halo = pl.BlockSpec((POOL_HALO, B_WIDTH),
                        lambda i: (jnp.maximum((nt - 1 - i) * (ts // POOL_HALO) - 1, 0), 2 * A_WIDTH // B_WIDTH))
    const = lambda *shape: pl.BlockSpec(shape, lambda i: (0,) * len(shape))
    resident = lambda *shape: pl.BlockSpec(shape, lambda i: (0,) * len(shape), pipeline_mode=pl.Buffered(1))
    outs = pl.pallas_call(
        body,
        name="mix_bwd",
        grid=(nt,),
        out_shape=(jax.ShapeDtypeStruct((t_len, D), F32), jax.ShapeDtypeStruct((IN_WIDTH, D), F32),
                   jax.ShapeDtypeStruct((D, D), F32), jax.ShapeDtypeStruct((3, D), F32),
                   jax.ShapeDtypeStruct((1, D), F32), jax.ShapeDtypeStruct((1, D), F32),
                   jax.ShapeDtypeStruct((N_HEAD, HEAD), F32), jax.ShapeDtypeStruct((N_HEAD, HEAD, HEAD), F32),
                   jax.ShapeDtypeStruct((HEAD, N_HEAD), F32), jax.ShapeDtypeStruct((N_HEAD, HEAD, HEAD), F32),
                   jax.ShapeDtypeStruct((1, B_WIDTH), F32))
        + tuple(jax.ShapeDtypeStruct((3, *s.shape[2:]), s.dtype) for s in rs_srcs),
        in_specs=[tile(D), tile(D), tile(IN_WIDTH), halo, tile(D)] + [VMEM_SPEC] * 10 + [ANY_SPEC] * nr,
        out_specs=(tile(D), resident(IN_WIDTH, D), resident(D, D), const(3, D), const(1, D), const(1, D),
                   const(N_HEAD, HEAD), const(N_HEAD, HEAD, HEAD), const(HEAD, N_HEAD), const(N_HEAD, HEAD, HEAD),
                   const(1, B_WIDTH)) + (ANY_SPEC,) * nr,
        scratch_shapes=[pltpu.VMEM((POOL_HALO + ts, B_WIDTH), F32), pltpu.VMEM((ts + POOL_HALO, B_WIDTH), F32),
                        pltpu.VMEM((ts, D), BF16), pltpu.VMEM((ts, IN_WIDTH), BF16), pltpu.VMEM((ts, D), F32),
                        pltpu.SemaphoreType.DMA((3 * nr,)), pltpu.SemaphoreType.DMA((3 * nr,))],
        compiler_params=pltpu.CompilerParams(dimension_semantics=("arbitrary",), vmem_limit_bytes=VMEM_LIMIT_V7X),
    )(dx1, x, proj, proj, mixed, mod, g_pre, g_post, w_in_t, sgn, w_sp, b_sp_t, w_pool, p_scale, w_out_b, *rs_srcs)
    return outs[:11], outs[11:]


def _pair_add(name, coords, grid, specs_a, specs_b, out_specs, out_shapes, a_arrays, b_arrays, swap_srcs):
    n, ns = len(a_arrays), len(swap_srcs)
    last = tuple(g - 1 for g in grid)

    def body(co_ref, *refs):
        ids = [pl.program_id(d) for d in range(len(grid))]
        swap = refs[2 * n:2 * n + ns], refs[4 * n + ns:4 * n + 2 * ns], *refs[4 * n + 2 * ns:]
        if ns:
            @pl.when(functools.reduce(jnp.logical_and, [i == 0 for i in ids]))
            def _():
                for cp in _sibling_swap_copies(*swap):
                    cp.start()

        for k in range(n):
            total = refs[k][...] + refs[n + k][...]
            refs[2 * n + ns + k][...] = total
            refs[3 * n + ns + k][...] = _bf(total)

        if ns:
            @pl.when(functools.reduce(jnp.logical_and, [i == e for i, e in zip(ids, last)]))
            def _():
                for cp in _sibling_swap_copies(*swap):
                    cp.wait()

    outs = pl.pallas_call(
        body,
        name=name,
        grid_spec=pltpu.PrefetchScalarGridSpec(
            num_scalar_prefetch=1, grid=grid, in_specs=specs_a + specs_b + [ANY_SPEC] * ns,
            out_specs=out_specs * 2 + [ANY_SPEC] * ns,
            scratch_shapes=[pltpu.SemaphoreType.DMA((ns, 4)), pltpu.SemaphoreType.DMA((ns, 4))] if ns else []),
        out_shape=tuple(jax.ShapeDtypeStruct(s, dt) for dt in (F32, BF16) for s in out_shapes)
        + tuple(jax.ShapeDtypeStruct(g.shape[1:], F32) for g in swap_srcs),
        compiler_params=pltpu.CompilerParams(dimension_semantics=("arbitrary",) * len(grid),
                                             vmem_limit_bytes=VMEM_LIMIT_V7X),
    )(coords, *a_arrays, *b_arrays, *swap_srcs)
    return list(outs[:n]), list(outs[n:2 * n]), list(outs[2 * n:])


def _final_add_adamw(coords, s1, r, ws, ms, vs, n_split=4):
    n = len(s1)

    def body(co_ref, *refs):
        for k in range(n):
            s_ref, r_ref, w_ref, m_ref, v_ref = (refs[q * n + k] for q in range(5))
            g_ref, d_ref, nm_ref, nv_ref = (refs[(5 + q) * n + k] for q in range(4))
            g = ((s_ref[...] + r_ref[0].astype(F32)) + r_ref[1].astype(F32)) + r_ref[2].astype(F32)
            g_ref[...] = g
            delta, m, v = _adamw(w_ref[...], g, m_ref[...], v_ref[...])
            d_ref[...] = delta
            nm_ref[...] = m
            nv_ref[...] = v

    def shard_spec(a):
        rows, cols = a.shape
        return pl.BlockSpec((rows // n_split, cols), lambda i, co: (i, 0))

    def mine_spec(a):
        rows, cols = a.shape[2:]
        return pl.BlockSpec((None, None, rows // n_split, cols), lambda i, co: (co[0], co[1], i, 0))

    def recv_spec(a):
        rows, cols = a.shape[1:]
        return pl.BlockSpec((3, rows // n_split, cols), lambda i, co: (0, i, 0))

    in_specs = ([mine_spec(a) for a in s1] + [recv_spec(a) for a in r] + [shard_spec(a) for a in ws] * 3)
    out_specs = [shard_spec(a) for a in ws] * 4
    outs = pl.pallas_call(
        body,
        name="grad_final_adamw",
        grid_spec=pltpu.PrefetchScalarGridSpec(num_scalar_prefetch=1, grid=(n_split,), in_specs=in_specs,
                                               out_specs=out_specs),
        out_shape=tuple(jax.ShapeDtypeStruct(a.shape, F32) for a in ws) * 4,
        compiler_params=pltpu.CompilerParams(dimension_semantics=("arbitrary",), vmem_limit_bytes=VMEM_LIMIT_V7X),
    )(coords, *s1, *r, *ws, *ms, *vs)
    return [tuple(outs[q * n + k] for q in range(4)) for k in range(n)]


def _sibling_add(tag, g5, r1, coords, swap_srcs=(), n_split=4):
    shapes = [g.shape[3:] for g in g5]
    spec_g = [pl.BlockSpec((None, None, None, s[0] // n_split, s[1]), lambda i, j, k, co: (i, j, co[2], k, 0))
              for s in shapes]
    spec_r = [pl.BlockSpec((None, None, s[0] // n_split, s[1]), lambda i, j, k, co: (i, j, k, 0)) for s in shapes]
    return _pair_add("grad_add_core_" + tag, coords, (2, 2, n_split), spec_g, spec_r, spec_r,
                     [(2, 2, *s) for s in shapes], g5, r1, list(swap_srcs))


def _tail_exchange(big, partials, pick_mine, y_first, dmod3):
    n, nb = len(partials), len(big)
    big_shapes = [g.shape[1:] for g in big]
    big5 = [g.reshape(2, 2, 2, *s) for g, s in zip(big, big_shapes)]
    flips = _ChipExchangeSteps.FLIPS

    def body(*refs):
        g5, p_in, dm_ref = refs[:nb], refs[nb:nb + n], refs[nb + n]
        outs = refs[nb + n + 1:2 * nb + 2 * n + 2]
        g_out, sums, dm2d = outs[:nb], outs[nb:nb + n], outs[nb + n]
        scratch = refs[2 * nb + 2 * n + 2:]
        s1, stage, chip_recv = scratch[:nb], scratch[nb:2 * nb], scratch[2 * nb:3 * nb]
        acc, rbuf = scratch[3 * nb:3 * nb + n], scratch[3 * nb + n:3 * nb + 2 * n]
        (dm_recv, send_sems, recv_sems, dm_send_sems, dm_recv_sems, sib_send, sib_recv, chip_send,
         chip_recv_sems) = scratch[3 * nb + 2 * n:]
        x, y, c = _coords()
        me = 4 * x + 2 * y + c
        sibling = (x, y, 1 - c)
        dm_copies = [
            pltpu.make_async_remote_copy(dm_ref.at[me ^ k], dm_recv.at[k], dm_send_sems.at[k], dm_recv_sems.at[k],
                                         device_id=_peer(k), device_id_type=MESH)
            for k in range(1, N_DEV)
        ]
        for cp in dm_copies:
            cp.start()
        sib_copies = _sibling_swap_copies(g5, s1, sib_send, sib_recv)
        for cp in sib_copies:
            cp.start()
        for a in range(n):
            acc[a][...] = p_in[a][...]

        def small_phase(ph, peers):
            copies = [
                pltpu.make_async_remote_copy(acc[a], rbuf[a].at[ph], send_sems.at[ph, a], recv_sems.at[ph, a],
                                             device_id=peers[y_first[a]], device_id_type=MESH)
                for a in range(n)
            ]
            for cp in copies:
                cp.start()
            for cp in copies:
                cp.wait()
            for a in range(n):
                acc[a][...] = acc[a][...] + rbuf[a][ph]

        small_phase(0, (sibling, sibling))
        for cp in sib_copies:
            cp.wait()
        for a in range(nb):
            for xs in range(2):
                for ys in range(2):
                    total = g5[a][xs, ys, c] + s1[a][xs, ys]
                    s1[a][xs, ys] = total
                    stage[a][xs, ys] = _bf(total)
        chip_copies = [
            pltpu.make_async_remote_copy(stage[a].at[x ^ fx, y ^ fy], chip_recv[a].at[j], chip_send.at[a, j],
                                         chip_recv_sems.at[a, j], device_id=(x ^ fx, y ^ fy, c), device_id_type=MESH)
            for a in range(nb) for j, (fx, fy) in enumerate(flips)
        ]
        for cp in chip_copies:
            cp.start()
        x_peer, y_peer = (1 - x, y, c), (x, 1 - y, c)
        small_phase(1, (x_peer, y_peer))
        small_phase(2, (y_peer, x_peer))
        for a in range(n):
            sums[a][...] = acc[a][me] if pick_mine[a] else acc[a][...]
        dm2d[...] = jnp.zeros(dm2d.shape, F32)
        dm2d[0:1, :] = dm_ref[me]
        for cp in dm_copies:
            cp.wait()
        for k in range(1, N_DEV):
            dm2d[k:k + 1, :] = dm_recv[k]
        for cp in chip_copies:
            cp.wait()
        for a in range(nb):
            g_out[a][...] = ((s1[a][x, y] + chip_recv[a][0].astype(F32)) + chip_recv[a][1].astype(F32)) \
                + chip_recv[a][2].astype(F32)

    out_shapes = tuple(jax.ShapeDtypeStruct(s, F32) for s in big_shapes) + tuple(
        jax.ShapeDtypeStruct(p.shape[1:] if pk else p.shape, F32) for p, pk in zip(partials, pick_mine))
    outs = pl.pallas_call(
        body,
        name="tail_exchange",
        out_shape=out_shapes + (jax.ShapeDtypeStruct((2 * N_DEV, MOD_COLS), F32),),
        in_specs=[VMEM_SPEC] * (nb + n + 1),
        out_specs=(VMEM_SPEC,) * (nb + n + 1),
        scratch_shapes=[pltpu.VMEM((2, 2, *s), F32) for s in big_shapes]
        + [pltpu.VMEM((2, 2, *s), BF16) for s in big_shapes]
        + [pltpu.VMEM((3, *s), BF16) for s in big_shapes]
        + [pltpu.VMEM(p.shape, F32) for p in partials]
        + [pltpu.VMEM((3, *p.shape), F32) for p in partials]
        + [pltpu.VMEM((N_DEV, 1, MOD_COLS), F32), pltpu.SemaphoreType.DMA((3, n)), pltpu.SemaphoreType.DMA((3, n)),
           pltpu.SemaphoreType.DMA((N_DEV,)), pltpu.SemaphoreType.DMA((N_DEV,)),
           pltpu.SemaphoreType.DMA((nb, 4)), pltpu.SemaphoreType.DMA((nb, 4)),
           pltpu.SemaphoreType.DMA((nb, 3)), pltpu.SemaphoreType.DMA((nb, 3))],
        compiler_params=pltpu.CompilerParams(vmem_limit_bytes=VMEM_LIMIT_V7X),
    )(*big5, *partials, dmod3)
    return list(outs[:nb]), list(outs[nb:nb + n]), outs[nb + n]


def _small_update(grads, ws, ms, vs, scx, dm2d, w_ada, m_ada, v_ada, loss_lanes):
    n = len(grads)

    def body(*refs):
        g_in, w_in, m_in, v_in = (refs[q * n:(q + 1) * n] for q in range(4))
        scx_ref, dm_ref, wa_ref, ma_ref, va_ref, ll_ref = refs[4 * n:4 * n + 6]
        outs = refs[4 * n + 6:]
        g_out, d_out, nm_out, nv_out = (outs[q * (n + 1):(q + 1) * (n + 1)] for q in range(4))
        loss_ref = outs[4 * (n + 1)]
        for a in range(n + 1):
            if a < n:
                g, w, m, v = g_in[a][...], w_in[a][...], m_in[a][...], v_in[a][...]
            else:
                g = _mm_tn(_bf(scx_ref[...]), _bf(dm_ref[...]))
                w, m, v = wa_ref[...], ma_ref[...], va_ref[...]
            g_out[a][...] = g
            delta, m, v = _adamw(w, g, m, v)
            d_out[a][...] = delta
            nm_out[a][...] = m
            nv_out[a][...] = v
        loss_ref[...] = jnp.sum(ll_ref[...], axis=1, keepdims=True) * (0.5 / D)

    w_shapes = tuple(jax.ShapeDtypeStruct(w.shape, F32) for w in list(ws) + [w_ada])
    outs = pl.pallas_call(
        body,
        name="small_update",
        out_shape=w_shapes * 4 + (jax.ShapeDtypeStruct((1, 1), F32),),
        in_specs=[VMEM_SPEC] * (4 * n + 6),
        out_specs=(VMEM_SPEC,) * (4 * (n + 1) + 1),
        compiler_params=pltpu.CompilerParams(vmem_limit_bytes=VMEM_LIMIT_V7X),
    )(*grads, *ws, *ms, *vs, scx, dm2d, w_ada, m_ada, v_ada, loss_lanes)
    return [tuple(outs[q * (n + 1) + k] for q in range(4)) for k in range(n + 1)], outs[4 * (n + 1)]


def kernel(x, c, w_ada, b_ada, pre_mix_g, post_mix_g, w_in, sgu_norm_g, w_spatial, b_spatial, w_pool, pool_scale, w_out, pre_ffn_g, post_ffn_g, w_up, conv_w, conv_b, w_down, loss_target, m_w_ada, m_b_ada, m_pre_mix_g, m_post_mix_g, m_w_in, m_sgu_norm_g, m_w_spatial, m_b_spatial, m_w_pool, m_pool_scale, m_w_out, m_pre_ffn_g, m_post_ffn_g, m_w_up, m_conv_w, m_conv_b, m_w_down, v_w_ada, v_b_ada, v_pre_mix_g, v_post_mix_g, v_w_in, v_sgu_norm_g, v_w_spatial, v_b_spatial, v_w_pool, v_pool_scale, v_w_out, v_pre_ffn_g, v_post_ffn_g, v_w_up, v_conv_w, v_conv_b, v_w_down):
    t_len = x.shape[1]
    ts = min(256, t_len)
    ts_mix = min(512, t_len)
    ts_w = min(1024, t_len)
    coords = jnp.stack([lax.axis_index("x"), lax.axis_index("y"), lax.axis_index("c")]).astype(jnp.int32)

    w_in_t, w_up_t = w_in[0].T, w_up[0].T
    mod3, scx, (g_in, g_out) = _prologue(c, w_ada[0], b_ada.reshape(N_DEV, 1, MOD_COLS), [w_in_t, w_out[0]],
                                         [BF16, BF16])
    mod = mod3.reshape(N_MOD, D)
    w_in_tb = g_in.reshape(IN_WIDTH, D)
    w_out_b = g_out.reshape(D, D)
    conv_b8 = conv_b.reshape(N_DEV, FF_CHUNK)
    b_sp_t = b_spatial[0].T

    x2d, tgt = x[0], loss_target[0]
    (x1, proj, mixed), (g_up, g_down, g_cw) = _mix_fwd(
        x2d, mod, pre_mix_g, post_mix_g, w_in_tb, sgu_norm_g[0], w_spatial[0], b_sp_t, w_pool[0], pool_scale, w_out_b,
        ts_mix, [w_up_t, w_down[0], conv_w[0]], [w_up.shape[1:], w_down.shape[1:], conv_w.shape[1:]], [BF16, BF16, F32])
    w_down_b = g_down.reshape(FF, D)
    up, act, f, dx2, loss_lanes = _ffn_fwd(x1, tgt, mod, pre_ffn_g, post_ffn_g, g_up, g_cw, conv_b8, w_down_b, ts)

    (dx1, dup, df, h2, dmod_f, d_pre_ffn, d_post_ffn, d_cb8, d_cw8) = _ffn_bwd(
        dx2, f, x1, up, mod, pre_ffn_g, post_ffn_g, g_up, g_cw, conv_b8, w_down_b, ts)
    gw_up = _wgrad_up(h2, dup, ts_w).reshape(2, 2, 2, FF_CHUNK, D)
    gw_down, r1_up = _wgrad_down(act, df, ts_w, gw_up)
    gw_down = gw_down.reshape(2, 2, 2, FF // N_DEV, D)
    s1_up, s1_up_b, r1_down = _sibling_add("up", [gw_up], [r1_up], coords, swap_srcs=[gw_down])
    s1_down, s1_down_b, _ = _sibling_add("down", [gw_down], r1_down, coords)
    s1_ffn, s1_ffn_b = s1_up + s1_down, s1_up_b + s1_down_b
    ((grad_x, gw_in, gw_out, dmod_m, d_pre_mix, d_post_mix, d_sgn, d_wsp, d_bsp_t, d_wpool, d_ps), r_ffn) = _mix_bwd(
        dx1, x2d, proj, mixed, mod, pre_mix_g, post_mix_g, w_in_tb, sgu_norm_g[0], w_spatial[0], b_sp_t,
        w_pool[0], pool_scale, w_out_b, ts_mix, s1_ffn_b)
    gw_in = gw_in.reshape(N_DEV, IN_WIDTH // N_DEV, D)
    gw_out = gw_out.reshape(N_DEV, D // N_DEV, D)

    big = _final_add_adamw(coords, s1_ffn, list(r_ffn), [w_up_t, w_down[0]], [m_w_up[0].T, m_w_down[0]],
                           [v_w_up[0].T, v_w_down[0]])
    r_up, r_down = tuple(a.T[None] for a in big[0]), tuple(a[None] for a in big[1])

    dmod = jnp.concatenate([dmod_m, dmod_f], axis=0)
    names = ["b_ada", "pre_mix_g", "post_mix_g", "sgu_norm_g", "w_spatial", "b_spatial", "w_pool", "pool_scale",
             "pre_ffn_g", "post_ffn_g", "conv_w", "conv_b"]
    partials = [dmod.reshape(1, N_MOD * D), d_pre_mix, d_post_mix, d_sgn, d_wsp, d_bsp_t.T, d_wpool, d_ps,
                d_pre_ffn, d_post_ffn, d_cw8, d_cb8.reshape(1, 2 * FF), loss_lanes]
    small_w = [b_ada, pre_mix_g, post_mix_g, sgu_norm_g[0], w_spatial[0], b_spatial[0], w_pool[0], pool_scale,
               pre_ffn_g, post_ffn_g, conv_w[0], conv_b]
    small_m = [m_b_ada, m_pre_mix_g, m_post_mix_g, m_sgu_norm_g[0], m_w_spatial[0], m_b_spatial[0], m_w_pool[0],
               m_pool_scale, m_pre_ffn_g, m_post_ffn_g, m_conv_w[0], m_conv_b]
    small_v = [v_b_ada, v_pre_mix_g, v_post_mix_g, v_sgu_norm_g[0], v_w_spatial[0], v_b_spatial[0], v_w_pool[0],
               v_pool_scale, v_pre_ffn_g, v_post_ffn_g, v_conv_w[0], v_conv_b]
    g_mix, sums, dm2d = _tail_exchange([gw_in, gw_out], partials, [nm == "conv_w" for nm in names] + [False],
                                       [int(nm == "w_pool") for nm in names] + [0],
                                       dmod.reshape(N_DEV, 1, MOD_COLS))
    small, loss11 = _small_update(
        sums[:-1] + g_mix, small_w + [w_in_t, w_out[0]], small_m + [m_w_in[0].T, m_w_out[0]],
        small_v + [v_w_in[0].T, v_w_out[0]], scx, dm2d, w_ada[0], m_w_ada[0], v_w_ada[0], sums[-1])
    loss = loss11.reshape(())
    lead = {"sgu_norm_g", "w_spatial", "b_spatial", "w_pool", "conv_w", "w_in", "w_out", "w_ada"}
    res = {nm: tuple((a.T if nm == "w_in" else a)[None] if nm in lead else a for a in four)
           for nm, four in zip(names + ["w_in", "w_out", "w_ada"], small)}
    res.update(w_up=r_up, w_down=r_down)

    order = ["w_ada", "b_ada", "pre_mix_g", "post_mix_g", "w_in", "sgu_norm_g", "w_spatial", "b_spatial", "w_pool",
             "pool_scale", "w_out", "pre_ffn_g", "post_ffn_g", "w_up", "conv_w", "conv_b", "w_down"]
    return (loss, grad_x[None], *[res[nm][0] for nm in order], *[res[nm][1] for nm in order],
            *[res[nm][2] for nm in order], *[res[nm][3] for nm in order])
```
